```python
import jax, jax.numpy as jnp
from jax import lax
import numpy as np

D_MODEL = 1024
BATCH = 8
SEQ = 8192
DEPTH = 1

ATTN_WIDTH = D_MODEL // 2
ATTN_HEADS = 8
ATTN_HEAD_DIM = ATTN_WIDTH // ATTN_HEADS
DILATED_PAIRS = ((128, 1), (512, 4), (2048, 16))
ATTN_BLOCK = 128
ROPE_THETA = 10000.0
HGRN_WIDTH = D_MODEL - ATTN_WIDTH
HGRN_EXPAND = 128
HGRN_HEADS = HGRN_WIDTH // HGRN_EXPAND
HGRN_CHUNK = 16
MIX_WIDTH = ATTN_WIDTH + HGRN_WIDTH
IN_PROJ_WIDTH = 3 * ATTN_WIDTH + 4 * HGRN_WIDTH
FFN_HIDDEN = ((-(-8 * D_MODEL // 3) + 255) // 256) * 256
NORM_EPS = 1e-6

kernel_name = "hymba_dilated_attn_hgrn2_block"


def rmsnorm(x, w):
    xf = x.astype(jnp.float32)
    y = xf * lax.rsqrt(jnp.mean(xf * xf, axis=-1, keepdims=True) + NORM_EPS)
    return (y * w.astype(jnp.float32)).astype(x.dtype)


def rotary(x):
    S, Dh = x.shape[1], x.shape[3]
    half = Dh // 2
    inv_freq = ROPE_THETA ** (-jnp.arange(half, dtype=jnp.float32) / half)
    ang = jnp.arange(S, dtype=jnp.float32)[:, None] * inv_freq[None, :]
    cos = jnp.cos(ang)[None, :, None, :]
    sin = jnp.sin(ang)[None, :, None, :]
    xf = x.astype(jnp.float32)
    x1, x2 = xf[..., :half], xf[..., half:]
    return jnp.concatenate([x1 * cos - x2 * sin, x2 * cos + x1 * sin], axis=-1)


def dilated_window_attention(q, k, v, window, dilation):
    B, S, H, Dh = q.shape
    L = S // dilation
    W = window // dilation
    n_blk = -(-L // ATTN_BLOCK)
    Lp = n_blk * ATTN_BLOCK

    def to_blocks(t):
        t = t.reshape(B, L, dilation, H, Dh).transpose(0, 2, 1, 3, 4)
        t = jnp.pad(t, ((0, 0), (0, 0), (0, Lp - L), (0, 0), (0, 0)))
        return t.reshape(B, dilation, n_blk, ATTN_BLOCK, H, Dh)

    def with_prev(t):
        prev = jnp.pad(t, ((0, 0), (0, 0), (1, 0), (0, 0), (0, 0), (0, 0)))[:, :, :-1]
        return jnp.concatenate([prev, t], axis=3)

    qb = to_blocks(q)
    kc = with_prev(to_blocks(k))
    vc = with_prev(to_blocks(v))
    scores = jnp.einsum('bdnqhe,bdnkhe->bdnhqk', qb, kc) * (Dh ** -0.5)
    qi = jnp.arange(ATTN_BLOCK)[:, None]
    kj = jnp.arange(2 * ATTN_BLOCK)[None, :]
    delta = ATTN_BLOCK + qi - kj
    blk = jnp.arange(n_blk)[:, None, None]
    valid = (delta >= 0) & (delta <= W) & ((blk > 0) | (kj >= ATTN_BLOCK))[...]
    scores = jnp.where(valid[None, None, :, None], scores, -jnp.inf)
    m = jnp.max(scores, axis=-1, keepdims=True)
    p = jnp.exp(scores - m)
    s = jnp.sum(p, axis=-1, keepdims=True)
    out = jnp.einsum('bdnhqk,bdnkhe->bdnqhe', p, vc) / s.transpose(0, 1, 2, 4, 3, 5)
    lse = (m + jnp.log(s))[..., 0].transpose(0, 1, 2, 4, 3)
    out = out.reshape(B, dilation, Lp, H, Dh)[:, :, :L].transpose(0, 2, 1, 3, 4).reshape(B, S, H, Dh)
    lse = lse.reshape(B, dilation, Lp, H)[:, :, :L].transpose(0, 2, 1, 3).reshape(B, S, H)
    return out, lse


def dilated_attention_group(q, k, v):
    B, S, _ = q.shape
    qh = rotary(q.reshape(B, S, ATTN_HEADS, ATTN_HEAD_DIM))
    kh = rotary(k.reshape(B, S, ATTN_HEADS, ATTN_HEAD_DIM))
    vh = v.reshape(B, S, ATTN_HEADS, ATTN_HEAD_DIM).astype(jnp.float32)
    outs, lses = [], []
    for window, dilation in DILATED_PAIRS:
        o, l = dilated_window_attention(qh, kh, vh, window, dilation)
        outs.append(o)
        lses.append(l)
    weights = jax.nn.softmax(jnp.stack(lses, axis=0), axis=0)
    y = jnp.sum(weights[..., None] * jnp.stack(outs, axis=0), axis=0)
    return y.reshape(B, S, ATTN_WIDTH)


def hgrn2_group(q, f_logit, i, g, lb, norm_w):
    B, S, _ = q.shape
    H, Dk, C = HGRN_HEADS, HGRN_EXPAND, HGRN_CHUNK
    N = S // C
    f = lb + (1.0 - lb) * jax.nn.sigmoid(f_logit.astype(jnp.float32))
    log_f = jnp.log(f)
    key = 1.0 - f
    qf = jax.nn.silu(q.astype(jnp.float32))

    def chunks(t):
        return t.reshape(B, N, C, H, Dk).transpose(0, 3, 1, 2, 4)

    qc, kc, vc, lfc = chunks(qf), chunks(key), chunks(i.astype(jnp.float32)), chunks(log_f)
    b = jnp.cumsum(lfc, axis=3)
    causal = jnp.tril(jnp.ones((C, C), dtype=bool))
    diff = b[:, :, :, :, None, :] - b[:, :, :, None, :, :]
    decay = jnp.exp(jnp.where(causal[:, :, None], diff, -jnp.inf))
    scores = jnp.einsum('bhntd,bhnsd,bhntsd->bhnts', qc, kc, decay)
    o_intra = jnp.einsum('bhnts,bhnsv->bhntv', scores, vc)

    b_last = b[:, :, :, -1:, :]
    q_inter = qc * jnp.exp(b)
    k_upd = kc * jnp.exp(b_last - b)
    chunk_decay = jnp.exp(b_last[:, :, :, 0, :])

    def step(state, xs):
        qn, kn, vn, dn = xs
        o = jnp.einsum('bhtd,bhdv->bhtv', qn, state)
        state = dn[..., None] * state + jnp.einsum('bhtd,bhtv->bhdv', kn, vn)
        return state, o

    xs = (jnp.moveaxis(q_inter, 2, 0), jnp.moveaxis(k_upd, 2, 0),
          jnp.moveaxis(vc, 2, 0), jnp.moveaxis(chunk_decay, 2, 0))
    state0 = jnp.zeros((B, H, Dk, Dk), dtype=jnp.float32)
    _, o_inter = lax.scan(step, state0, xs)
    o = o_intra + jnp.moveaxis(o_inter, 0, 2)
    o = o.transpose(0, 2, 3, 1, 4).reshape(B, S, H, Dk)
    o = o * lax.rsqrt(jnp.mean(o * o, axis=-1, keepdims=True) + NORM_EPS)
    o = o.reshape(B, S, HGRN_WIDTH) * norm_w.astype(jnp.float32)
    return o * jax.nn.silu(g.astype(jnp.float32))


def _fwd_setup_inputs(seed: int = 0) -> dict:
    key = jax.random.key(seed)
    ks = jax.random.split(key, 10)
    f32 = jnp.float32
    x = jax.random.normal(ks[0], (BATCH, SEQ, D_MODEL), f32)
    norm1_w = 1.0 + 0.02 * jax.random.normal(ks[1], (DEPTH, D_MODEL), f32)
    w_in = jax.random.normal(ks[2], (DEPTH, D_MODEL, IN_PROJ_WIDTH), f32) * D_MODEL ** -0.5
    lb_logits = 0.5 * jax.random.normal(ks[3], (DEPTH + 1, HGRN_WIDTH), f32)
    hgrn_norm_w = 1.0 + 0.02 * jax.random.normal(ks[4], (DEPTH, HGRN_WIDTH), f32)
    w_out = jax.random.normal(ks[5], (DEPTH, MIX_WIDTH, D_MODEL), f32) * MIX_WIDTH ** -0.5
    norm2_w = 1.0 + 0.02 * jax.random.normal(ks[6], (DEPTH, D_MODEL), f32)
    w_gate_up = jax.random.normal(ks[7], (DEPTH, D_MODEL, 2 * FFN_HIDDEN), f32) * D_MODEL ** -0.5
    w_down = jax.random.normal(ks[8], (DEPTH, FFN_HIDDEN, D_MODEL), f32) * FFN_HIDDEN ** -0.5
    final_norm_w = 1.0 + 0.02 * jax.random.normal(ks[9], (D_MODEL,), f32)
    return {"x": x, "norm1_w": norm1_w, "w_in": w_in, "lb_logits": lb_logits,
            "hgrn_norm_w": hgrn_norm_w, "w_out": w_out, "norm2_w": norm2_w,
            "w_gate_up": w_gate_up, "w_down": w_down, "final_norm_w": final_norm_w}


def _fwd_reference(x, norm1_w, w_in, lb_logits, hgrn_norm_w, w_out, norm2_w, w_gate_up, w_down, final_norm_w):
    lb_table = jnp.cumsum(jax.nn.softmax(lb_logits.astype(jnp.float32), axis=0), axis=0)
    h = x
    for l in range(DEPTH):
        u = rmsnorm(h, norm1_w[l])
        proj = jnp.einsum('bsd,de->bse', u, w_in[l])
        a = ATTN_WIDTH
        qa, ka, va = proj[..., :a], proj[..., a:2 * a], proj[..., 2 * a:3 * a]
        o = 3 * a
        w = HGRN_WIDTH
        qb, fb, ib, gb = (proj[..., o:o + w], proj[..., o + w:o + 2 * w],
                          proj[..., o + 2 * w:o + 3 * w], proj[..., o + 3 * w:o + 4 * w])
        ya = dilated_attention_group(qa, ka, va)
        yb = hgrn2_group(qb, fb, ib, gb, lb_table[l], hgrn_norm_w[l])
        mixed = jnp.concatenate([ya, yb], axis=-1).astype(h.dtype)
        h = h + jnp.einsum('bse,ed->bsd', mixed, w_out[l])
        u2 = rmsnorm(h, norm2_w[l])
        gu = jnp.einsum('bsd,df->bsf', u2, w_gate_up[l])
        gate, up = gu[..., :FFN_HIDDEN], gu[..., FFN_HIDDEN:]
        h = h + jnp.einsum('bsf,fd->bsd', jax.nn.silu(gate) * up, w_down[l])
    return rmsnorm(h, final_norm_w)


import jax as _jax
import jax.numpy as _jnp

TWIN_FORMAT = 'train_step'
FWD_PARAMS = ['x', 'norm1_w', 'w_in', 'lb_logits', 'hgrn_norm_w', 'w_out', 'norm2_w', 'w_gate_up', 'w_down', 'final_norm_w']
TWIN_WEIGHTS = ['norm1_w', 'w_in', 'lb_logits', 'hgrn_norm_w', 'w_out', 'norm2_w', 'w_gate_up', 'w_down', 'final_norm_w']
TWIN_DIFF_INPUT = 'x'
TWIN_INPUTS = ['x', 'norm1_w', 'w_in', 'lb_logits', 'hgrn_norm_w', 'w_out', 'norm2_w', 'w_gate_up', 'w_down', 'final_norm_w', 'loss_target', 'm_norm1_w', 'm_w_in', 'm_lb_logits', 'm_hgrn_norm_w', 'm_w_out', 'm_norm2_w', 'm_w_gate_up', 'm_w_down', 'm_final_norm_w', 'v_norm1_w', 'v_w_in', 'v_lb_logits', 'v_hgrn_norm_w', 'v_w_out', 'v_norm2_w', 'v_w_gate_up', 'v_w_down', 'v_final_norm_w']
TWIN_OUTPUTS = ['loss', 'grad_x', 'grad_norm1_w', 'grad_w_in', 'grad_lb_logits', 'grad_hgrn_norm_w', 'grad_w_out', 'grad_norm2_w', 'grad_w_gate_up', 'grad_w_down', 'grad_final_norm_w', 'delta_norm1_w', 'delta_w_in', 'delta_lb_logits', 'delta_hgrn_norm_w', 'delta_w_out', 'delta_norm2_w', 'delta_w_gate_up', 'delta_w_down', 'delta_final_norm_w', 'new_m_norm1_w', 'new_m_w_in', 'new_m_lb_logits', 'new_m_hgrn_norm_w', 'new_m_w_out', 'new_m_norm2_w', 'new_m_w_gate_up', 'new_m_w_down', 'new_m_final_norm_w', 'new_v_norm1_w', 'new_v_w_in', 'new_v_lb_logits', 'new_v_hgrn_norm_w', 'new_v_w_out', 'new_v_norm2_w', 'new_v_w_gate_up', 'new_v_w_down', 'new_v_final_norm_w']
TWIN_LEAF_KINDS = {'loss': 'loss', 'grad_x': 'grad_x', 'grad_norm1_w': 'grad_w', 'grad_w_in': 'grad_w', 'grad_lb_logits': 'grad_w', 'grad_hgrn_norm_w': 'grad_w', 'grad_w_out': 'grad_w', 'grad_norm2_w': 'grad_w', 'grad_w_gate_up': 'grad_w', 'grad_w_down': 'grad_w', 'grad_final_norm_w': 'grad_w', 'delta_norm1_w': 'delta_w', 'delta_w_in': 'delta_w', 'delta_lb_logits': 'delta_w', 'delta_hgrn_norm_w': 'delta_w', 'delta_w_out': 'delta_w', 'delta_norm2_w': 'delta_w', 'delta_w_gate_up': 'delta_w', 'delta_w_down': 'delta_w', 'delta_final_norm_w': 'delta_w', 'new_m_norm1_w': 'new_m', 'new_m_w_in': 'new_m', 'new_m_lb_logits': 'new_m', 'new_m_hgrn_norm_w': 'new_m', 'new_m_w_out': 'new_m', 'new_m_norm2_w': 'new_m', 'new_m_w_gate_up': 'new_m', 'new_m_w_down': 'new_m', 'new_m_final_norm_w': 'new_m', 'new_v_norm1_w': 'new_v', 'new_v_w_in': 'new_v', 'new_v_lb_logits': 'new_v', 'new_v_hgrn_norm_w': 'new_v', 'new_v_w_out': 'new_v', 'new_v_norm2_w': 'new_v', 'new_v_w_gate_up': 'new_v', 'new_v_w_down': 'new_v', 'new_v_final_norm_w': 'new_v'}


def _forward(args):
    return _fwd_reference(*[args[k] for k in FWD_PARAMS])


def _output_shape():
    def fwd():
        inp = _fwd_setup_inputs(0)
        return _fwd_reference(*[inp[k] for k in FWD_PARAMS])
    out = _jax.eval_shape(fwd)
    return out.shape, out.dtype

N_MICROBATCH = 1
ADAM_LR = 0.001
ADAM_B1 = 0.9
ADAM_B2 = 0.999
ADAM_EPS = 1e-08
ADAM_WD = 0.01
ADAM_STEP = 10
PER_EXAMPLE_BATCH_AXIS = {'x': 0, 'loss_target': 0}
SHARED_INPUTS = []
_WEIGHT_DTYPES = {'norm1_w': _jnp.float32, 'w_in': _jnp.float32, 'lb_logits': _jnp.float32, 'hgrn_norm_w': _jnp.float32, 'w_out': _jnp.float32, 'norm2_w': _jnp.float32, 'w_gate_up': _jnp.float32, 'w_down': _jnp.float32, 'final_norm_w': _jnp.float32}
MOMENT_SCALE = {'norm1_w': 1.725784e-01, 'w_in': 8.949087e-02, 'lb_logits': 1.475353e-02, 'hgrn_norm_w': 1.510576e-01, 'w_out': 1.142842e-01, 'norm2_w': 1.846738e-01, 'w_gate_up': 7.408544e-02, 'w_down': 1.211441e-01, 'final_norm_w': 6.402272e+01}


def _to_microbatches(a, axis):
    t = _jnp.moveaxis(a, axis, 0)
    t = t.reshape((N_MICROBATCH, t.shape[0] // N_MICROBATCH) + t.shape[1:])
    return _jnp.moveaxis(t, 1, axis + 1)


def setup_inputs(seed: int = 0) -> dict:
    inp = _fwd_setup_inputs(seed)
    key = _jax.random.fold_in(_jax.random.key(seed), 7919)
    shape, _ = _output_shape()
    out = dict(inp)
    out["loss_target"] = _jax.random.normal(_jax.random.fold_in(key, 0), shape, _jnp.float32)
    for i, name in enumerate(TWIN_WEIGHTS):
        w = inp[name].astype(_jnp.float32)
        if MOMENT_SCALE is None:
            s = _jnp.sqrt(_jnp.mean(_jnp.square(w)) + 1e-30)
        else:
            s = MOMENT_SCALE[name]
        km, kv = _jax.random.split(_jax.random.fold_in(key, i + 1))
        out[name] = w
        out["m_" + name] = s * _jax.random.normal(km, w.shape, _jnp.float32)
        out["v_" + name] = (s * s) * _jax.random.uniform(kv, w.shape, _jnp.float32, 0.5, 1.5)
    if N_MICROBATCH > 1:
        for name, axis in PER_EXAMPLE_BATCH_AXIS.items():
            out[name] = _to_microbatches(out[name], axis)
    return {'x': out['x'], 'norm1_w': out['norm1_w'], 'w_in': out['w_in'], 'lb_logits': out['lb_logits'], 'hgrn_norm_w': out['hgrn_norm_w'], 'w_out': out['w_out'], 'norm2_w': out['norm2_w'], 'w_gate_up': out['w_gate_up'], 'w_down': out['w_down'], 'final_norm_w': out['final_norm_w'], 'loss_target': out['loss_target'], 'm_norm1_w': out['m_norm1_w'], 'm_w_in': out['m_w_in'], 'm_lb_logits': out['m_lb_logits'], 'm_hgrn_norm_w': out['m_hgrn_norm_w'], 'm_w_out': out['m_w_out'], 'm_norm2_w': out['m_norm2_w'], 'm_w_gate_up': out['m_w_gate_up'], 'm_w_down': out['m_w_down'], 'm_final_norm_w': out['m_final_norm_w'], 'v_norm1_w': out['v_norm1_w'], 'v_w_in': out['v_w_in'], 'v_lb_logits': out['v_lb_logits'], 'v_hgrn_norm_w': out['v_hgrn_norm_w'], 'v_w_out': out['v_w_out'], 'v_norm2_w': out['v_norm2_w'], 'v_w_gate_up': out['v_w_gate_up'], 'v_w_down': out['v_w_down'], 'v_final_norm_w': out['v_final_norm_w']}


def _loss(weights, diff, rest, loss_target):
    with _jax.named_scope("forward"):
        args = {**rest, TWIN_DIFF_INPUT: diff, **{k: w.astype(_WEIGHT_DTYPES[k]) for k, w in weights.items()}}
        y = _forward(args)
    with _jax.named_scope("loss_head"):
        err = _jnp.square(y.astype(_jnp.float32) - loss_target)
        return 0.5 * _jnp.sum(_jnp.mean(err, axis=-1)) if err.ndim else 0.5 * err


def _adamw(w, g, m, v):
    m = ADAM_B1 * m + (1.0 - ADAM_B1) * g
    v = ADAM_B2 * v + (1.0 - ADAM_B2) * _jnp.square(g)
    m_hat = m / (1.0 - ADAM_B1 ** ADAM_STEP)
    v_hat = v / (1.0 - ADAM_B2 ** ADAM_STEP)
    delta = -ADAM_LR * (m_hat / (_jnp.sqrt(v_hat) + ADAM_EPS) + ADAM_WD * w)
    return delta, m, v


def reference(x, norm1_w, w_in, lb_logits, hgrn_norm_w, w_out, norm2_w, w_gate_up, w_down, final_norm_w, loss_target, m_norm1_w, m_w_in, m_lb_logits, m_hgrn_norm_w, m_w_out, m_norm2_w, m_w_gate_up, m_w_down, m_final_norm_w, v_norm1_w, v_w_in, v_lb_logits, v_hgrn_norm_w, v_w_out, v_norm2_w, v_w_gate_up, v_w_down, v_final_norm_w):
    given = dict(x=x, norm1_w=norm1_w, w_in=w_in, lb_logits=lb_logits, hgrn_norm_w=hgrn_norm_w, w_out=w_out, norm2_w=norm2_w, w_gate_up=w_gate_up, w_down=w_down, final_norm_w=final_norm_w, loss_target=loss_target, m_norm1_w=m_norm1_w, m_w_in=m_w_in, m_lb_logits=m_lb_logits, m_hgrn_norm_w=m_hgrn_norm_w, m_w_out=m_w_out, m_norm2_w=m_norm2_w, m_w_gate_up=m_w_gate_up, m_w_down=m_w_down, m_final_norm_w=m_final_norm_w, v_norm1_w=v_norm1_w, v_w_in=v_w_in, v_lb_logits=v_lb_logits, v_hgrn_norm_w=v_hgrn_norm_w, v_w_out=v_w_out, v_norm2_w=v_norm2_w, v_w_gate_up=v_w_gate_up, v_w_down=v_w_down, v_final_norm_w=v_final_norm_w)
    weights = {n: given[n] for n in TWIN_WEIGHTS}
    shared = {n: given[n] for n in SHARED_INPUTS}
    per_example = {n: given[n] for n in ['x']}
    grad_fn = _jax.value_and_grad(_loss, argnums=(0, 1))

    def one_microbatch(ex, loss_target):
        ex = dict(ex)
        diff = ex.pop(TWIN_DIFF_INPUT)
        return grad_fn(weights, diff, {**shared, **ex}, loss_target)

    if N_MICROBATCH == 1:
        loss, (grad_w, grad_x) = one_microbatch(per_example, given["loss_target"])
    else:
        def body(carry, xs):
            loss_sum, grad_sum = carry
            l_k, (gw_k, gx_k) = one_microbatch(xs[0], xs[1])
            with _jax.named_scope("update"):
                return (loss_sum + l_k, _jax.tree.map(_jnp.add, grad_sum, gw_k)), gx_k

        init = (_jnp.zeros((), _jnp.float32), _jax.tree.map(_jnp.zeros_like, weights))
        (loss, grad_w), grad_x = _jax.lax.scan(body, init, (per_example, given["loss_target"]))
    with _jax.named_scope("update"):
        delta_w, new_m, new_v = {}, {}, {}
        for n in TWIN_WEIGHTS:
            delta_w[n], new_m[n], new_v[n] = _adamw(weights[n], grad_w[n], given["m_" + n], given["v_" + n])
    return (loss, grad_x, *[grad_w[n] for n in TWIN_WEIGHTS], *[delta_w[n] for n in TWIN_WEIGHTS],
            *[new_m[n] for n in TWIN_WEIGHTS], *[new_v[n] for n in TWIN_WEIGHTS])
```

```python
import functools

import jax
import jax.numpy as jnp
from jax import lax
from jax.experimental import pallas as pl
from jax.experimental.pallas import tpu as pltpu

F32 = jnp.float32
BF16 = jnp.bfloat16

D_MODEL = 1024
ATTN_WIDTH = 512
HEAD_DIM = 64
DILATED_PAIRS = ((128, 1), (512, 4), (2048, 16))
ATTN_BLOCK = 128
ROPE_THETA = 10000.0
HGRN_WIDTH = 512
HGRN_CHUNK = 16
HGRN_HEADS = 4
IN_PROJ_WIDTH = 3584
FFN_HIDDEN = 2816
NORM_EPS = 1e-6
ATTN_SCALE = HEAD_DIM ** -0.5
N_CHIPS = 4
N_DEV = 8

ADAM_LR = 0.001
ADAM_B1 = 0.9
ADAM_B2 = 0.999
ADAM_EPS = 1e-08
ADAM_WD = 0.01
ADAM_STEP = 10

LANES = 128
HGRN_ROWS = 128
ROW_TILE = 256
VMEM_LIMIT = 56 * 1024 * 1024
NEG_BIG = -1e30
MESH_ID = pl.DeviceIdType.MESH


def _cparams(*sem):
    return pltpu.CompilerParams(dimension_semantics=tuple(sem), vmem_limit_bytes=VMEM_LIMIT)


def _dot(a, b):
    return jnp.dot(a, b, preferred_element_type=F32)


def _dot_nt(a, b):
    return lax.dot_general(a, b, (((1,), (1,)), ((), ())), preferred_element_type=F32)


def _dot_tn(a, b):
    return lax.dot_general(a, b, (((0,), (0,)), ((), ())), preferred_element_type=F32)


def _sigmoid(x):
    return 1.0 / (1.0 + jnp.exp(-x))


def _full(shape):
    n = len(shape)
    return pl.BlockSpec(shape, lambda *_: (0,) * n)


def _rows(tm, width):
    return pl.BlockSpec((tm, width), lambda i: (i, 0))


def _swap32(x):
    lane = lax.broadcasted_iota(jnp.int32, x.shape, 1)
    first = (lane % HEAD_DIM) < (HEAD_DIM // 2)
    return jnp.where(first, pltpu.roll(x, LANES - 32, axis=1), pltpu.roll(x, 32, axis=1))


def _rotary_fwd(x, cos, sin_signed):
    parts = []
    for j in range(x.shape[1] // LANES):
        xc = x[:, j * LANES:(j + 1) * LANES]
        parts.append(xc * cos + _swap32(xc) * sin_signed)
    return jnp.concatenate(parts, axis=1)


def _rotary_bwd(dy, cos, sin_signed):
    parts = []
    for j in range(dy.shape[1] // LANES):
        dc = dy[:, j * LANES:(j + 1) * LANES]
        parts.append(dc * cos + _swap32(dc * sin_signed))
    return jnp.concatenate(parts, axis=1)


def _rope_tables(seq):
    half = HEAD_DIM // 2
    inv_freq = ROPE_THETA ** (-jnp.arange(half, dtype=F32) / half)
    ang = jnp.arange(seq, dtype=F32)[:, None] * inv_freq[None, :]
    cos, sin = jnp.cos(ang), jnp.sin(ang)
    cos_t = jnp.tile(cos, (1, LANES // half))
    sin_t = jnp.tile(jnp.concatenate([-sin, sin], axis=1), (1, LANES // HEAD_DIM))
    return cos_t, sin_t


def cast_bf16(w, name):
    r, c = w.shape
    half = r // 2

    def body(w_ref, o_ref):
        o_ref[...] = w_ref[...].astype(BF16)

    return pl.pallas_call(
        body, name=name, grid=(2,),
        in_specs=[pl.BlockSpec((half, c), lambda i: (i, 0))],
        out_specs=pl.BlockSpec((None, half, c), lambda i: (i, 0, 0)),
        out_shape=jax.ShapeDtypeStruct((2, half, c), BF16),
        compiler_params=_cparams("parallel"),
    )(w)


def _mesh_pos():
    return lax.axis_index("x"), lax.axis_index("y"), lax.axis_index("c")


def allgather_halves(halves, name):
    _, r, c = halves.shape

    def body(x_ref, out_ref, send_sems, recv_sems, local_sem):
        x, y, cc = _mesh_pos()
        me, sibling = (x, y, cc), (x, y, 1 - cc)
        chips = [(1 - x, y), (x, 1 - y), (1 - x, 1 - y)]
        mine_src = x_ref.at[cc]

        def rows(px, py, pc):
            return out_ref.at[4 * px + 2 * py + pc]

        def copy(k, block, to, src=None):
            return pltpu.make_async_remote_copy(
                src_ref=rows(*block) if src is None else src, dst_ref=rows(*block),
                send_sem=send_sems.at[k], recv_sem=recv_sems.at[k],
                device_id=to, device_id_type=MESH_ID)

        mine = pltpu.make_async_copy(mine_src, rows(*me), local_sem)
        mine.start()
        first = [copy(0, me, sibling, src=mine_src)]
        first += [copy(1 + j, me, (*chip, cc), src=mine_src) for j, chip in enumerate(chips)]
        for cp in first:
            cp.start()
        passed = [copy(4 + j, (*chip, cc), sibling) for j, chip in enumerate(chips)]
        for j, chip in enumerate(chips):
            copy(1 + j, (*chip, cc), me).wait_recv()
            passed[j].start()
        copy(0, sibling, me).wait_recv()
        for j, chip in enumerate(chips):
            copy(4 + j, (*chip, 1 - cc), me).wait_recv()
        for cp in first + passed:
            cp.wait_send()
        mine.wait()

    return pl.pallas_call(
        body, name=name,
        in_specs=[pl.BlockSpec(memory_space=pl.ANY)],
        out_specs=pl.BlockSpec(memory_space=pl.ANY),
        out_shape=jax.ShapeDtypeStruct((N_DEV, r, c), halves.dtype),
        scratch_shapes=[pltpu.SemaphoreType.DMA((7,)), pltpu.SemaphoreType.DMA((7,)),
                        pltpu.SemaphoreType.DMA],
    )(halves)


def _rms(x):
    return lax.rsqrt(jnp.mean(x * x, axis=-1, keepdims=True) + NORM_EPS)


def in_proj(x, norm1_w, w_in4, cos_t, sin_t):
    seq = x.shape[0]
    tm = ROW_TILE
    cw = w_in4.shape[2]

    def body(x_ref, nw_ref, w_ref, cos_ref, sin_ref, q_ref, k_ref, v_ref, hg_ref, u_ref):
        xv = x_ref[...]
        u = ((xv * _rms(xv)) * nw_ref[...]).astype(BF16)
        u_ref[...] = u
        proj = jnp.concatenate([_dot(u, w_ref[j]) for j in range(N_CHIPS)], axis=1)
        cos, sin = cos_ref[...], sin_ref[...]
        a = ATTN_WIDTH
        q_ref[...] = _rotary_fwd(proj[:, :a], cos, sin).astype(BF16)
        k_ref[...] = _rotary_fwd(proj[:, a:2 * a], cos, sin).astype(BF16)
        v_ref[...] = proj[:, 2 * a:3 * a].astype(BF16)
        hg_ref[...] = proj[:, 3 * a:]

    return pl.pallas_call(
        body, name="in_proj", grid=(seq // tm,),
        in_specs=[_rows(tm, D_MODEL), _full((1, D_MODEL)), _full((N_CHIPS, D_MODEL, cw)),
                  _rows(tm, LANES), _rows(tm, LANES)],
        out_specs=[_rows(tm, ATTN_WIDTH)] * 3 + [_rows(tm, 4 * HGRN_WIDTH), _rows(tm, D_MODEL)],
        out_shape=[jax.ShapeDtypeStruct((seq, ATTN_WIDTH), BF16)] * 3
        + [jax.ShapeDtypeStruct((seq, 4 * HGRN_WIDTH), F32), jax.ShapeDtypeStruct((seq, D_MODEL), BF16)],
        compiler_params=_cparams("parallel"),
    )(x, norm1_w, w_in4, cos_t, sin_t)


def _head_masks():
    lane = lax.broadcasted_iota(jnp.int32, (1, LANES), 1)
    return [(lane // HEAD_DIM) == h for h in range(LANES // HEAD_DIM)]


def _window_valid(first_block):
    qi = lax.broadcasted_iota(jnp.int32, (ATTN_BLOCK, 2 * ATTN_BLOCK), 0)
    kj = lax.broadcasted_iota(jnp.int32, (ATTN_BLOCK, 2 * ATTN_BLOCK), 1)
    valid = (kj >= qi) & (kj <= qi + ATTN_BLOCK)
    return valid & (jnp.logical_not(first_block) | (kj >= ATTN_BLOCK))


def attn_fwd(q, k, v, dilation, name):
    seq = q.shape[0]
    length = seq // dilation
    nb = length // ATTN_BLOCK
    width = ATTN_WIDTH
    shp = (length, dilation * width)
    q, k, v = (t.reshape(shp) for t in (q, k, v))

    def body(q_ref, kc_ref, vc_ref, kp_ref, vp_ref, o_ref, lse_ref):
        valid = _window_valid(pl.program_id(1) == 0)
        masks = _head_masks()
        for hp in range(width // LANES):
            sl = slice(hp * LANES, (hp + 1) * LANES)
            q2 = q_ref[:, sl]
            k2 = jnp.concatenate([kp_ref[:, sl], kc_ref[:, sl]], axis=0)
            v2 = jnp.concatenate([vp_ref[:, sl], vc_ref[:, sl]], axis=0)
            o_acc = jnp.zeros((ATTN_BLOCK, LANES), F32)
            l_acc = jnp.zeros((ATTN_BLOCK, LANES), F32)
            for mh in masks:
                qm = jnp.where(mh, q2, jnp.zeros_like(q2))
                s = jnp.where(valid, _dot_nt(qm, k2) * ATTN_SCALE, NEG_BIG)
                m = jnp.max(s, axis=-1, keepdims=True)
                p = jnp.exp(s - m)
                l = jnp.sum(p, axis=-1, keepdims=True)
                o = _dot(p.astype(BF16), v2) / l
                o_acc = jnp.where(mh, o, o_acc)
                l_acc = jnp.where(mh, m + jnp.log(l), l_acc)
            o_ref[:, sl] = o_acc
            lse_ref[:, sl] = l_acc

    cur = pl.BlockSpec((ATTN_BLOCK, width), lambda r, i: (i, r))
    prev = pl.BlockSpec((ATTN_BLOCK, width), lambda r, i: (jnp.maximum(i - 1, 0), r))
    o, lse = pl.pallas_call(
        body, name=name, grid=(dilation, nb),
        in_specs=[cur, cur, cur, prev, prev],
        out_specs=[cur, cur],
        out_shape=[jax.ShapeDtypeStruct(shp, F32)] * 2,
        compiler_params=_cparams("parallel", "parallel"),
    )(q, k, v, k, v)
    return o.reshape(seq, width), lse.reshape(seq, width)


def _hgrn_constants():
    t = jnp.arange(HGRN_ROWS)[:, None]
    s = jnp.arange(HGRN_ROWS)[None, :]
    same = (t // HGRN_CHUNK) == (s // HGRN_CHUNK)
    cum = same & (s <= t)
    mid = same & ((s % HGRN_CHUNK) <= HGRN_CHUNK // 2 - 1)
    fwd = jnp.concatenate([cum, mid, same], axis=0).astype(BF16)
    bwd = jnp.concatenate([same & (s >= t), same], axis=0).astype(BF16)
    return fwd, bwd


def _exact_rowmix(mat, x):
    hi = x.astype(BF16)
    r1 = x - hi.astype(F32)
    mid = r1.astype(BF16)
    lo = (r1 - mid.astype(F32)).astype(BF16)
    return _dot(mat, hi) + _dot(mat, mid) + _dot(mat, lo)


def _hgrn_prep(hg, lbl, mats):
    w = HGRN_WIDTH
    a0, a1 = lbl[0:1, :], lbl[1:2, :]
    mx = jnp.maximum(a0, a1)
    e0, e1 = jnp.exp(a0 - mx), jnp.exp(a1 - mx)
    lb = e0 / (e0 + e1)
    qb, fb, gb = hg[:, :w], hg[:, w:2 * w], hg[:, 3 * w:]
    sg = _sigmoid(fb)
    f = lb + (1.0 - lb) * sg
    mixed = _exact_rowmix(mats, jnp.log(f))
    b, bmid, btot = mixed[:HGRN_ROWS], mixed[HGRN_ROWS:2 * HGRN_ROWS], mixed[2 * HGRN_ROWS:]
    sq = _sigmoid(qb)
    p = dict(lb=lb, sg=sg, f=f, kk=1.0 - f, sq=sq, qf=qb * sq, gb=gb,
             e_iq=jnp.exp(b - bmid), e_ik=jnp.exp(bmid - b), e_b=jnp.exp(b),
             e_bb=jnp.exp(btot - b), e_tot=jnp.exp(btot))
    p["qi"] = p["qf"] * p["e_iq"]
    p["ki"] = p["kk"] * p["e_ik"]
    p["qs"] = p["qf"] * p["e_b"]
    p["kb"] = p["kk"] * p["e_bb"]
    return p


def _chunk_masks():
    t = lax.broadcasted_iota(jnp.int32, (HGRN_ROWS, HGRN_ROWS), 0)
    s = lax.broadcasted_iota(jnp.int32, (HGRN_ROWS, HGRN_ROWS), 1)
    tril = ((t // HGRN_CHUNK) == (s // HGRN_CHUNK)) & (s <= t)
    n_chunks = HGRN_ROWS // HGRN_CHUNK
    tt = lax.broadcasted_iota(jnp.int32, (HGRN_ROWS, n_chunks * LANES), 0)
    cc = lax.broadcasted_iota(jnp.int32, (HGRN_ROWS, n_chunks * LANES), 1)
    block = (tt // HGRN_CHUNK) == (cc // LANES)
    return tril, block


def _spread(x, block):
    n_chunks = HGRN_ROWS // HGRN_CHUNK
    return jnp.where(block, jnp.tile(x, (1, n_chunks)), jnp.zeros((), x.dtype))


def _fold(x_full, block):
    n_chunks = HGRN_ROWS // HGRN_CHUNK
    z = jnp.where(block, x_full, 0.0)
    acc = z[:, :LANES]
    for n in range(1, n_chunks):
        acc = acc + z[:, n * LANES:(n + 1) * LANES]
    return acc


def hgrn_fwd(hg, lb_logits, hnw):
    seq = hg.shape[0]
    nblk = seq // HGRN_ROWS
    n_chunks = HGRN_ROWS // HGRN_CHUNK
    mats, _ = _hgrn_constants()

    def body(hg_ref, lbl_ref, hnw_ref, mats_ref, yb_ref, o_ref, st0_ref, st_scr):
        @pl.when(pl.program_id(0) == 0)
        def _():
            st_scr[...] = jnp.zeros_like(st_scr)

        hg_v = hg_ref[...]
        p = _hgrn_prep(hg_v, lbl_ref[...], mats_ref[...])
        tril, block = _chunk_masks()
        vv = hg_v[:, 2 * HGRN_WIDTH:3 * HGRN_WIDTH].astype(BF16)
        outs = []
        for h in range(HGRN_HEADS):
            sl = slice(h * LANES, (h + 1) * LANES)
            v_h = vv[:, sl]
            a = jnp.where(tril, _dot_nt(p["qi"][:, sl].astype(BF16), p["ki"][:, sl].astype(BF16)), 0.0)
            o = _dot(a.astype(BF16), v_h)
            upd = _dot_tn(v_h, _spread(p["kb"][:, sl].astype(BF16), block))
            st = st_scr[h]
            st0_ref[h] = st
            parts = []
            for n in range(n_chunks):
                parts.append(st.astype(BF16))
                decay = p["e_tot"][n * HGRN_CHUNK:n * HGRN_CHUNK + 1, sl]
                st = st * decay + upd[:, n * LANES:(n + 1) * LANES]
            st_scr[h] = st
            o = o + _dot_nt(_spread(p["qs"][:, sl].astype(BF16), block), jnp.concatenate(parts, axis=1))
            outs.append(o)
        o_all = jnp.concatenate(outs, axis=1)
        o_ref[...] = o_all
        normed = jnp.concatenate(
            [outs[h] * _rms(outs[h]) for h in range(HGRN_HEADS)], axis=1)
        gb = p["gb"]
        yb_ref[...] = (normed * hnw_ref[...]) * (gb * _sigmoid(gb))

    return pl.pallas_call(
        body, name="hgrn_fwd", grid=(nblk,),
        in_specs=[_rows(HGRN_ROWS, 4 * HGRN_WIDTH), _full((2, HGRN_WIDTH)), _full((1, HGRN_WIDTH)),
                  _full(mats.shape)],
        out_specs=[_rows(HGRN_ROWS, HGRN_WIDTH), _rows(HGRN_ROWS, HGRN_WIDTH),
                   pl.BlockSpec((None, HGRN_HEADS, LANES, LANES), lambda i: (i, 0, 0, 0))],
        out_shape=[jax.ShapeDtypeStruct((seq, HGRN_WIDTH), F32)] * 2
        + [jax.ShapeDtypeStruct((nblk, HGRN_HEADS, LANES, LANES), F32)],
        scratch_shapes=[pltpu.VMEM((HGRN_HEADS, LANES, LANES), F32)],
        compiler_params=_cparams("arbitrary"),
    )(hg, lb_logits, hnw, mats)


def mix_out(outs, lses, yb, x, w_out, norm2_w):
    seq = x.shape[0]
    tm = ROW_TILE

    def body(o1, o2, o3, l1, l2, l3, yb_ref, x_ref, w_ref, nw_ref,
             ya_ref, lse_ref, mixed_ref, h1_ref, u2_ref):
        l1v, l2v, l3v = l1[...], l2[...], l3[...]
        mx = jnp.maximum(jnp.maximum(l1v, l2v), l3v)
        e1, e2, e3 = jnp.exp(l1v - mx), jnp.exp(l2v - mx), jnp.exp(l3v - mx)
        den = e1 + e2 + e3
        ya = (e1 * o1[...] + e2 * o2[...] + e3 * o3[...]) / den
        ya_ref[...] = ya
        lse_ref[...] = mx + jnp.log(den)
        mixed = jnp.concatenate([ya, yb_ref[...]], axis=1).astype(BF16)
        mixed_ref[...] = mixed
        h1 = x_ref[...] + _dot(mixed, w_ref[...])
        h1_ref[...] = h1
        u2_ref[...] = ((h1 * _rms(h1)) * nw_ref[...]).astype(BF16)

    half = _rows(tm, ATTN_WIDTH)
    wide = _rows(tm, D_MODEL)
    return pl.pallas_call(
        body, name="mix_out", grid=(seq // tm,),
        in_specs=[half] * 7 + [wide, _full((D_MODEL, D_MODEL)), _full((1, D_MODEL))],
        out_specs=[half, half, wide, wide, wide],
        out_shape=[jax.ShapeDtypeStruct((seq, ATTN_WIDTH), F32)] * 2
        + [jax.ShapeDtypeStruct((seq, D_MODEL), BF16), jax.ShapeDtypeStruct((seq, D_MODEL), F32),
           jax.ShapeDtypeStruct((seq, D_MODEL), BF16)],
        compiler_params=_cparams("parallel"),
    )(*outs, *lses, yb, x, w_out, norm2_w)


def gate_up(u2, w_gu4):
    seq = u2.shape[0]
    tm = ROW_TILE
    cw = w_gu4.shape[2]

    def body(u_ref, w_ref, g_ref, up_ref, act_ref):
        u = u_ref[...]
        g = jnp.concatenate([_dot(u, w_ref[0]), _dot(u, w_ref[1])], axis=1)
        up = jnp.concatenate([_dot(u, w_ref[2]), _dot(u, w_ref[3])], axis=1)
        g_ref[...] = g.astype(BF16)
        up_ref[...] = up.astype(BF16)
        act_ref[...] = ((g * _sigmoid(g)) * up).astype(BF16)

    ffn = _rows(tm, FFN_HIDDEN)
    return pl.pallas_call(
        body, name="gate_up", grid=(seq // tm,),
        in_specs=[_rows(tm, D_MODEL), _full((N_CHIPS, D_MODEL, cw))],
        out_specs=[ffn] * 3,
        out_shape=[jax.ShapeDtypeStruct((seq, FFN_HIDDEN), BF16)] * 3,
        compiler_params=_cparams("parallel"),
    )(u2, w_gu4)


def down_loss(act, w_down, h1, final_w, target):
    seq = h1.shape[0]
    tm = ROW_TILE
    inv_d = 1.0 / D_MODEL

    def body(act_ref, w_ref, h1_ref, fw_ref, t_ref, dh2_ref, acc_ref):
        @pl.when(pl.program_id(0) == 0)
        def _():
            acc_ref[...] = jnp.zeros_like(acc_ref)

        h2 = h1_ref[...] + _dot(act_ref[...], w_ref[...])
        rf = _rms(h2)
        n = h2 * rf
        fw = fw_ref[...]
        err = n * fw - t_ref[...]
        dy = err * inv_d
        acc_ref[0:1, :] += jnp.sum(dy * n, axis=0, keepdims=True)
        acc_ref[1:2, :] += (0.5 * inv_d) * jnp.sum(err * err, axis=0, keepdims=True)
        dn = dy * fw
        dh2_ref[...] = rf * (dn - n * jnp.mean(dn * n, axis=-1, keepdims=True))

    wide = _rows(tm, D_MODEL)
    return pl.pallas_call(
        body, name="down_loss", grid=(seq // tm,),
        in_specs=[_rows(tm, FFN_HIDDEN), _full((FFN_HIDDEN, D_MODEL)), wide, _full((1, D_MODEL)), wide],
        out_specs=[wide, _full((8, D_MODEL))],
        out_shape=[jax.ShapeDtypeStruct((seq, D_MODEL), F32), jax.ShapeDtypeStruct((8, D_MODEL), F32)],
        compiler_params=_cparams("arbitrary"),
    )(act, w_down, h1, final_w, target)


def down_bwd(dh2, w_down, g, up):
    seq = dh2.shape[0]
    tm = ROW_TILE

    def body(dh_ref, w_ref, g_ref, up_ref, dgu_ref):
        dact = _dot_nt(dh_ref[...].astype(BF16), w_ref[...])
        gv = g_ref[...].astype(F32)
        sg = _sigmoid(gv)
        dgu_ref[:, :FFN_HIDDEN] = (dact * up_ref[...].astype(F32) * (sg * (1.0 + gv * (1.0 - sg)))).astype(BF16)
        dgu_ref[:, FFN_HIDDEN:] = (dact * (gv * sg)).astype(BF16)

    ffn = _rows(tm, FFN_HIDDEN)
    return pl.pallas_call(
        body, name="down_bwd", grid=(seq // tm,),
        in_specs=[_rows(tm, D_MODEL), _full((FFN_HIDDEN, D_MODEL)), ffn, ffn],
        out_specs=_rows(tm, 2 * FFN_HIDDEN),
        out_shape=jax.ShapeDtypeStruct((seq, 2 * FFN_HIDDEN), BF16),
        compiler_params=_cparams("parallel"),
    )(dh2, w_down, g, up)


def _head_sum_matrix():
    i = jnp.arange(ATTN_WIDTH)
    return ((i[:, None] // HEAD_DIM) == (i[None, :] // HEAD_DIM)).astype(BF16)


def gu_bwd(dgu, w_gu4, h1, norm2_w, dh2, w_out, ya):
    seq = h1.shape[0]
    tm = ROW_TILE
    cw = w_gu4.shape[2]
    hsum = _head_sum_matrix()

    def body(dgu_ref, w_ref, h1_ref, nw_ref, dh2_ref, wo_ref, ya_ref, hs_ref,
             dh1_ref, dya_ref, dyb_ref, delta_ref, acc_ref):
        @pl.when(pl.program_id(0) == 0)
        def _():
            acc_ref[...] = jnp.zeros_like(acc_ref)

        du2 = _dot_nt(dgu_ref[:, :cw], w_ref[0])
        for j in range(1, N_CHIPS):
            du2 = du2 + _dot_nt(dgu_ref[:, j * cw:(j + 1) * cw], w_ref[j])
        h1 = h1_ref[...]
        r2 = _rms(h1)
        nh = h1 * r2
        acc_ref[0:1, :] += jnp.sum(du2 * nh, axis=0, keepdims=True)
        dn = du2 * nw_ref[...]
        dh1 = dh2_ref[...] + r2 * (dn - nh * jnp.mean(dn * nh, axis=-1, keepdims=True))
        dh1_ref[...] = dh1
        dmixed = _dot_nt(dh1.astype(BF16), wo_ref[...])
        dya = dmixed[:, :ATTN_WIDTH]
        dya_ref[...] = dya.astype(BF16)
        dyb_ref[...] = dmixed[:, ATTN_WIDTH:]
        prod = dya * ya_ref[...]
        hi = prod.astype(BF16)
        lo = (prod - hi.astype(F32)).astype(BF16)
        delta_ref[...] = _dot(hi, hs_ref[...]) + _dot(lo, hs_ref[...])

    wide = _rows(tm, D_MODEL)
    half = _rows(tm, ATTN_WIDTH)
    return pl.pallas_call(
        body, name="gu_bwd", grid=(seq // tm,),
        in_specs=[_rows(tm, 2 * FFN_HIDDEN), _full((N_CHIPS, D_MODEL, cw)), wide, _full((1, D_MODEL)), wide,
                  _full((D_MODEL, D_MODEL)), half, _full((ATTN_WIDTH, ATTN_WIDTH))],
        out_specs=[wide, half, half, half, _full((8, D_MODEL))],
        out_shape=[jax.ShapeDtypeStruct((seq, D_MODEL), F32), jax.ShapeDtypeStruct((seq, ATTN_WIDTH), BF16),
                   jax.ShapeDtypeStruct((seq, ATTN_WIDTH), F32), jax.ShapeDtypeStruct((seq, ATTN_WIDTH), F32),
                   jax.ShapeDtypeStruct((8, D_MODEL), F32)],
        compiler_params=_cparams("arbitrary"),
    )(dgu, w_gu4, h1, norm2_w, dh2, w_out, ya, hsum)


def attn_bwd(q, k, v, dy, lse, delta, dilation, name):
    seq = q.shape[0]
    length = seq // dilation
    nb = length // ATTN_BLOCK
    width = ATTN_WIDTH
    shp = (length, dilation * width)
    q, k, v, dy, lse, delta = (t.reshape(shp) for t in (q, k, v, dy, lse, delta))

    def body(q_ref, dy_ref, lse_ref, dl_ref, qn_ref, dyn_ref, lsen_ref, dln_ref,
             kc_ref, vc_ref, kp_ref, vp_ref, dq_ref, dk_ref, dv_ref):
        j = pl.program_id(1)
        valid = _window_valid(j == 0)
        qi = lax.broadcasted_iota(jnp.int32, (ATTN_BLOCK, ATTN_BLOCK), 0)
        kj = lax.broadcasted_iota(jnp.int32, (ATTN_BLOCK, ATTN_BLOCK), 1)
        valid_next = (kj >= qi) & (j < nb - 1)
        masks = _head_masks()
        for hp in range(width // LANES):
            sl = slice(hp * LANES, (hp + 1) * LANES)
            q2, dy2, qn2, dyn2 = q_ref[:, sl], dy_ref[:, sl], qn_ref[:, sl], dyn_ref[:, sl]
            kc, vc = kc_ref[:, sl], vc_ref[:, sl]
            k2 = jnp.concatenate([kp_ref[:, sl], kc], axis=0)
            v2 = jnp.concatenate([vp_ref[:, sl], vc], axis=0)
            dq_acc = jnp.zeros((ATTN_BLOCK, LANES), F32)
            dk_acc = jnp.zeros((ATTN_BLOCK, LANES), F32)
            dv_acc = jnp.zeros((ATTN_BLOCK, LANES), F32)
            for h, mh in enumerate(masks):
                c0 = hp * LANES + h * HEAD_DIM
                zero = jnp.zeros_like(q2)
                qm, dym = jnp.where(mh, q2, zero), jnp.where(mh, dy2, zero)
                qnm, dynm = jnp.where(mh, qn2, zero), jnp.where(mh, dyn2, zero)
                s = _dot_nt(qm, k2) * ATTN_SCALE
                p = jnp.where(valid, jnp.exp(s - lse_ref[:, c0:c0 + 1]), 0.0)
                dp = _dot_nt(dym, v2)
                ds = (p * (dp - dl_ref[:, c0:c0 + 1]) * ATTN_SCALE).astype(BF16)
                dq_acc = jnp.where(mh, _dot(ds, k2), dq_acc)
                pb = p.astype(BF16)
                dv_acc = dv_acc + _dot_tn(pb[:, ATTN_BLOCK:], dym)
                dk_acc = dk_acc + _dot_tn(ds[:, ATTN_BLOCK:], qm)
                sn = _dot_nt(qnm, kc) * ATTN_SCALE
                pn = jnp.where(valid_next, jnp.exp(sn - lsen_ref[:, c0:c0 + 1]), 0.0)
                dpn = _dot_nt(dynm, vc)
                dsn = (pn * (dpn - dln_ref[:, c0:c0 + 1]) * ATTN_SCALE).astype(BF16)
                dv_acc = dv_acc + _dot_tn(pn.astype(BF16), dynm)
                dk_acc = dk_acc + _dot_tn(dsn, qnm)
            dq_ref[:, sl] = dq_acc
            dk_ref[:, sl] = dk_acc
            dv_ref[:, sl] = dv_acc

    cur = pl.BlockSpec((ATTN_BLOCK, width), lambda r, i: (i, r))
    prev = pl.BlockSpec((ATTN_BLOCK, width), lambda r, i: (jnp.maximum(i - 1, 0), r))
    nxt = pl.BlockSpec((ATTN_BLOCK, width), lambda r, i: (jnp.minimum(i + 1, nb - 1), r))
    dq, dk, dv = pl.pallas_call(
        body, name=name, grid=(dilation, nb),
        in_specs=[cur] * 4 + [nxt] * 4 + [cur, cur, prev, prev],
        out_specs=[cur] * 3,
        out_shape=[jax.ShapeDtypeStruct(shp, F32)] * 3,
        compiler_params=_cparams("parallel", "parallel"),
    )(q, dy, lse, delta, q, dy, lse, delta, k, v, k, v)
    return tuple(t.reshape(seq, width) for t in (dq, dk, dv))


def hgrn_bwd(hg, lb_logits, hnw, o_pre, st0, dyb):
    seq = hg.shape[0]
    nblk = seq // HGRN_ROWS
    n_chunks = HGRN_ROWS // HGRN_CHUNK
    mats, mats_b = _hgrn_constants()
    w = HGRN_WIDTH

    def body(hg_ref, lbl_ref, hnw_ref, mats_ref, matsb_ref, o_ref, st0_ref, dyb_ref,
             dhg_ref, acc_ref, dst_scr):
        @pl.when(pl.program_id(0) == 0)
        def _():
            dst_scr[...] = jnp.zeros_like(dst_scr)
            acc_ref[...] = jnp.zeros_like(acc_ref)

        hg_v = hg_ref[...]
        p = _hgrn_prep(hg_v, lbl_ref[...], mats_ref[...])
        tril, block = _chunk_masks()
        chunk_of_row = lax.broadcasted_iota(jnp.int32, (HGRN_ROWS, 1), 0) // HGRN_CHUNK
        vv = hg_v[:, 2 * w:3 * w].astype(BF16)
        hnw_v = hnw_ref[...]
        gb = p["gb"]
        sgg = _sigmoid(gb)
        silu_g = gb * sgg
        dyb_v = dyb_ref[...]
        o_v = o_ref[...]

        d_on = dyb_v * hnw_v * silu_g
        on_parts, do_parts = [], []
        for h in range(HGRN_HEADS):
            sl = slice(h * LANES, (h + 1) * LANES)
            rs = _rms(o_v[:, sl])
            on = o_v[:, sl] * rs
            on_parts.append(on)
            do_parts.append(rs * (d_on[:, sl] - on * jnp.mean(d_on[:, sl] * on, axis=-1, keepdims=True)))
        on_all = jnp.concatenate(on_parts, axis=1)
        dgb = dyb_v * on_all * hnw_v * (sgg * (1.0 + gb * (1.0 - sgg)))
        acc_ref[0:1, :] += jnp.sum(dyb_v * on_all * silu_g, axis=0, keepdims=True)

        dqf_parts, dkk_parts, db_parts, dv_parts, dbt_parts, dkbkb_parts = [], [], [], [], [], []
        for h in range(HGRN_HEADS):
            sl = slice(h * LANES, (h + 1) * LANES)
            v_h = vv[:, sl]
            do_h = do_parts[h].astype(BF16)
            qi, ki, qs, kb = p["qi"][:, sl], p["ki"][:, sl], p["qs"][:, sl], p["kb"][:, sl]
            qi_b, ki_b = qi.astype(BF16), ki.astype(BF16)
            kb_cat = _spread(kb.astype(BF16), block)
            qs_cat = _spread(qs.astype(BF16), block)
            upd = _dot_tn(v_h, kb_cat)
            st = st0_ref[h]
            st_parts = []
            for n in range(n_chunks):
                st_parts.append(st)
                decay = p["e_tot"][n * HGRN_CHUNK:n * HGRN_CHUNK + 1, sl]
                st = st * decay + upd[:, n * LANES:(n + 1) * LANES]
            st_cat = jnp.concatenate([s_.astype(BF16) for s_ in st_parts], axis=1)
            wgt = _dot_tn(do_h, qs_cat)
            dst = dst_scr[h]
            dst_parts = [None] * n_chunks
            dbt_rows = [None] * n_chunks
            for n in reversed(range(n_chunks)):
                dst_parts[n] = dst.astype(BF16)
                decay = p["e_tot"][n * HGRN_CHUNK:n * HGRN_CHUNK + 1, sl]
                dbt_rows[n] = jnp.sum(dst * st_parts[n], axis=0, keepdims=True) * decay
                dst = dst * decay + wgt[:, n * LANES:(n + 1) * LANES]
            dst_scr[h] = dst
            dst_cat = jnp.concatenate(dst_parts, axis=1)
            dqs = _fold(_dot(do_h, st_cat), block)
            dkb = _fold(_dot(v_h, dst_cat), block)
            dv_state = _dot_nt(kb_cat, dst_cat)
            a = jnp.where(tril, _dot_nt(qi_b, ki_b), 0.0).astype(BF16)
            da = jnp.where(tril, _dot_nt(do_h, v_h), 0.0).astype(BF16)
            dv_parts.append(_dot_tn(a, do_h) + dv_state)
            dqi = _dot(da, ki_b)
            dki = _dot_tn(da, qi_b)
            dqf_parts.append(dqi * p["e_iq"][:, sl] + dqs * p["e_b"][:, sl])
            dkk_parts.append(dki * p["e_ik"][:, sl] + dkb * p["e_bb"][:, sl])
            dkbkb = dkb * kb
            db_parts.append(dqi * qi - dki * ki + dqs * qs - dkbkb)
            dkbkb_parts.append(dkbkb)
            dbt = jnp.zeros((HGRN_ROWS, LANES), F32)
            for n in range(n_chunks):
                dbt = jnp.where(chunk_of_row == n, dbt_rows[n], dbt)
            dbt_parts.append(dbt)

        cat = lambda parts: jnp.concatenate(parts, axis=1)
        mb = matsb_ref[...]
        dlogf = (_exact_rowmix(mb[:HGRN_ROWS], cat(db_parts))
                 + _exact_rowmix(mb[HGRN_ROWS:], cat(dkbkb_parts)) + cat(dbt_parts))
        sq, qb = p["sq"], hg_v[:, :w]
        dqb = cat(dqf_parts) * (sq * (1.0 + qb * (1.0 - sq)))
        df = dlogf / p["f"] - cat(dkk_parts)
        sg, lb = p["sg"], p["lb"]
        dfb = df * (1.0 - lb) * sg * (1.0 - sg)
        acc_ref[1:2, :] += jnp.sum(df * (1.0 - sg), axis=0, keepdims=True)
        dhg_ref[...] = jnp.concatenate([dqb, dfb, cat(dv_parts), dgb], axis=1)

    rev = lambda i: (nblk - 1 - i, 0)
    return pl.pallas_call(
        body, name="hgrn_bwd", grid=(nblk,),
        in_specs=[pl.BlockSpec((HGRN_ROWS, 4 * w), rev), _full((2, w)), _full((1, w)),
                  _full(mats.shape), _full(mats_b.shape), pl.BlockSpec((HGRN_ROWS, w), rev),
                  pl.BlockSpec((None, HGRN_HEADS, LANES, LANES), lambda i: (nblk - 1 - i, 0, 0, 0)),
                  pl.BlockSpec((HGRN_ROWS, w), rev)],
        out_specs=[pl.BlockSpec((HGRN_ROWS, 4 * w), rev), _full((8, w))],
        out_shape=[jax.ShapeDtypeStruct((seq, 4 * w), F32), jax.ShapeDtypeStruct((8, w), F32)],
        scratch_shapes=[pltpu.VMEM((HGRN_HEADS, LANES, LANES), F32)],
        compiler_params=_cparams("arbitrary"),
    )(hg, lb_logits, hnw, mats, mats_b, o_pre, st0, dyb)


def in_bwd(dqs, dks, dvs, dhg, cos_t, sin_t, w_in4, x, norm1_w, dh1):
    seq = x.shape[0]
    tm = ROW_TILE
    cw = w_in4.shape[2]

    def body(dq1, dq2, dq3, dk1, dk2, dk3, dv1, dv2, dv3, dhg_ref, cos_ref, sin_ref, w_ref,
             x_ref, nw_ref, dh1_ref, dproj_ref, dx_ref, acc_ref):
        @pl.when(pl.program_id(0) == 0)
        def _():
            acc_ref[...] = jnp.zeros_like(acc_ref)

        cos, sin = cos_ref[...], sin_ref[...]
        dqa = _rotary_bwd(dq1[...] + dq2[...] + dq3[...], cos, sin)
        dka = _rotary_bwd(dk1[...] + dk2[...] + dk3[...], cos, sin)
        dva = dv1[...] + dv2[...] + dv3[...]
        dproj = jnp.concatenate([dqa, dka, dva, dhg_ref[...]], axis=1).astype(BF16)
        dproj_ref[...] = dproj
        du = _dot_nt(dproj[:, :cw], w_ref[0])
        for j in range(1, N_CHIPS):
            du = du + _dot_nt(dproj[:, j * cw:(j + 1) * cw], w_ref[j])
        xv = x_ref[...]
        r1 = _rms(xv)
        nx = xv * r1
        acc_ref[0:1, :] += jnp.sum(du * nx, axis=0, keepdims=True)
        dn = du * nw_ref[...]
        dx_ref[...] = dh1_ref[...] + r1 * (dn - nx * jnp.mean(dn * nx, axis=-1, keepdims=True))

    half = _rows(tm, ATTN_WIDTH)
    wide = _rows(tm, D_MODEL)
    return pl.pallas_call(
        body, name="in_bwd", grid=(seq // tm,),
        in_specs=[half] * 9 + [_rows(tm, 4 * HGRN_WIDTH), _rows(tm, LANES), _rows(tm, LANES),
                               _full((N_CHIPS, D_MODEL, cw)), wide, _full((1, D_MODEL)), wide],
        out_specs=[_rows(tm, IN_PROJ_WIDTH), wide, _full((8, D_MODEL))],
        out_shape=[jax.ShapeDtypeStruct((seq, IN_PROJ_WIDTH), BF16), jax.ShapeDtypeStruct((seq, D_MODEL), F32),
                   jax.ShapeDtypeStruct((8, D_MODEL), F32)],
        compiler_params=_cparams("arbitrary"),
    )(*dqs, *dks, *dvs, dhg, cos_t, sin_t, w_in4, x, norm1_w, dh1)


def weight_grad(a, b, col_block, name):
    seq, kdim = a.shape
    ndim = b.shape[1]
    nj = ndim // col_block
    tk = 512

    def body(a_ref, b_ref, o_ref):
        @pl.when(pl.program_id(1) == 0)
        def _():
            o_ref[...] = jnp.zeros_like(o_ref)

        o_ref[...] += _dot_tn(a_ref[...].astype(BF16), b_ref[...].astype(BF16))

    return pl.pallas_call(
        body, name=name, grid=(nj, seq // tk),
        in_specs=[pl.BlockSpec((tk, kdim), lambda j, t: (t, 0)),
                  pl.BlockSpec((tk, col_block), lambda j, t: (t, j))],
        out_specs=pl.BlockSpec((None, kdim, col_block), lambda j, t: (j, 0, 0)),
        out_shape=jax.ShapeDtypeStruct((nj, kdim, col_block), F32),
        compiler_params=_cparams("parallel", "arbitrary"),
    )(a, b)


def exchange_with_sibling(grads):
    n = len(grads)

    def body(*refs):
        g_refs, out_refs = refs[:n], refs[n:2 * n]
        send_sems, recv_sems = refs[2 * n], refs[2 * n + 1]
        x, y, cc = _mesh_pos()
        copies = []
        for i in range(n):
            for j in range(N_CHIPS):
                k = i * N_CHIPS + j
                copies.append(pltpu.make_async_remote_copy(
                    src_ref=g_refs[i].at[j, 1 - cc], dst_ref=out_refs[i].at[j],
                    send_sem=send_sems.at[k], recv_sem=recv_sems.at[k],
                    device_id=(x, y, 1 - cc), device_id_type=MESH_ID))
        for cp in copies:
            cp.start()
        for cp in copies:
            cp.wait_recv()
        for cp in copies:
            cp.wait_send()

    return pl.pallas_call(
        body, name="grad_exchange_sibling",
        in_specs=[pl.BlockSpec(memory_space=pl.ANY)] * n,
        out_specs=[pl.BlockSpec(memory_space=pl.ANY)] * n,
        out_shape=[jax.ShapeDtypeStruct((N_CHIPS,) + g.shape[2:], g.dtype) for g in grads],
        scratch_shapes=[pltpu.SemaphoreType.DMA((n * N_CHIPS,)), pltpu.SemaphoreType.DMA((n * N_CHIPS,))],
    )(*grads)


def add_own_half(grad, recv, name):
    _, _, r, c = grad.shape
    tr = r // 2 if r % 16 == 0 else r

    def body(cc_ref, g_ref, r_ref, o_ref):
        o_ref[...] = g_ref[...] + r_ref[...]

    grid_spec = pltpu.PrefetchScalarGridSpec(
        num_scalar_prefetch=1, grid=(N_CHIPS, r // tr),
        in_specs=[pl.BlockSpec((None, None, tr, c), lambda j, t, cc: (j, cc[0], t, 0)),
                  pl.BlockSpec((None, tr, c), lambda j, t, cc: (j, t, 0))],
        out_specs=pl.BlockSpec((None, tr, c), lambda j, t, cc: (j, t, 0)))
    cc = lax.axis_index("c").astype(jnp.int32).reshape(1)
    return pl.pallas_call(
        body, name=name, grid_spec=grid_spec,
        out_shape=jax.ShapeDtypeStruct((N_CHIPS, r, c), grad.dtype),
        compiler_params=_cparams("parallel", "parallel"),
    )(cc, grad, recv)


def exchange_between_chips(sums):
    n = len(sums)

    def body(*refs):
        s_refs, out_refs = refs[:n], refs[n:2 * n]
        send_sems, recv_sems, local_sems = refs[2 * n], refs[2 * n + 1], refs[2 * n + 2]
        x, y, cc = _mesh_pos()
        my_chip = 2 * x + y
        chips = [(1 - x, y), (x, 1 - y), (1 - x, 1 - y)]
        local = [pltpu.make_async_copy(s_refs[i].at[my_chip], out_refs[i].at[my_chip], local_sems.at[i])
                 for i in range(n)]
        for cp in local:
            cp.start()
        copies = []
        for i in range(n):
            for j, (px, py) in enumerate(chips):
                k = i * 3 + j
                copies.append(pltpu.make_async_remote_copy(
                    src_ref=s_refs[i].at[2 * px + py], dst_ref=out_refs[i].at[my_chip],
                    send_sem=send_sems.at[k], recv_sem=recv_sems.at[k],
                    device_id=(px, py, cc), device_id_type=MESH_ID))
        for cp in copies:
            cp.start()
        for i in range(n):
            for j, (px, py) in enumerate(chips):
                k = i * 3 + j
                pltpu.make_async_remote_copy(
                    src_ref=s_refs[i].at[my_chip], dst_ref=out_refs[i].at[2 * px + py],
                    send_sem=send_sems.at[k], recv_sem=recv_sems.at[k],
                    device_id=(px, py, cc), device_id_type=MESH_ID).wait_recv()
        for cp in copies:
            cp.wait_send()
        for cp in local:
            cp.wait()

    return pl.pallas_call(
        body, name="grad_exchange_chips",
        in_specs=[pl.BlockSpec(memory_space=pl.ANY)] * n,
        out_specs=[pl.BlockSpec(memory_space=pl.ANY)] * n,
        out_shape=[jax.ShapeDtypeStruct(s.shape, s.dtype) for s in sums],
        scratch_shapes=[pltpu.SemaphoreType.DMA((n * 3,)), pltpu.SemaphoreType.DMA((n * 3,)),
                        pltpu.SemaphoreType.DMA((n,))],
    )(*sums)


def sum_chips(parts, name):
    _, r, c = parts.shape
    tr = r // 2 if r % 16 == 0 else r

    def body(p_ref, o_ref):
        o_ref[...] = ((p_ref[0] + p_ref[1]) + p_ref[2]) + p_ref[3]

    return pl.pallas_call(
        body, name=name, grid=(r // tr,),
        in_specs=[pl.BlockSpec((N_CHIPS, tr, c), lambda t: (0, t, 0))],
        out_specs=pl.BlockSpec((tr, c), lambda t: (t, 0)),
        out_shape=jax.ShapeDtypeStruct((r, c), parts.dtype),
        compiler_params=_cparams("parallel"),
    )(parts)


def share_with_sibling(halves):
    n = len(halves)

    def body(*refs):
        h_refs, out_refs = refs[:n], refs[n:2 * n]
        send_sems, recv_sems, local_sems = refs[2 * n], refs[2 * n + 1], refs[2 * n + 2]
        x, y, cc = _mesh_pos()
        local = [pltpu.make_async_copy(h_refs[i], out_refs[i].at[cc], local_sems.at[i]) for i in range(n)]
        copies = [pltpu.make_async_remote_copy(
            src_ref=h_refs[i], dst_ref=out_refs[i].at[cc],
            send_sem=send_sems.at[i], recv_sem=recv_sems.at[i],
            device_id=(x, y, 1 - cc), device_id_type=MESH_ID) for i in range(n)]
        for cp in local + copies:
            cp.start()
        for i in range(n):
            pltpu.make_async_remote_copy(
                src_ref=h_refs[i], dst_ref=out_refs[i].at[1 - cc],
                send_sem=send_sems.at[i], recv_sem=recv_sems.at[i],
                device_id=(x, y, 1 - cc), device_id_type=MESH_ID).wait_recv()
        for cp in copies:
            cp.wait_send()
        for cp in local:
            cp.wait()

    return pl.pallas_call(
        body, name="grad_share_sibling",
        in_specs=[pl.BlockSpec(memory_space=pl.ANY)] * n,
        out_specs=[pl.BlockSpec(memory_space=pl.ANY)] * n,
        out_shape=[jax.ShapeDtypeStruct((2,) + h.shape, h.dtype) for h in halves],
        scratch_shapes=[pltpu.SemaphoreType.DMA((n,)), pltpu.SemaphoreType.DMA((n,)),
                        pltpu.SemaphoreType.DMA((n,))],
    )(*halves)


def _adam_update(w, g, m, v):
    m = ADAM_B1 * m + (1.0 - ADAM_B1) * g
    v = ADAM_B2 * v + (1.0 - ADAM_B2) * (g * g)
    m_hat = m / (1.0 - ADAM_B1 ** ADAM_STEP)
    v_hat = v / (1.0 - ADAM_B2 ** ADAM_STEP)
    delta = -ADAM_LR * (m_hat / (jnp.sqrt(v_hat) + ADAM_EPS) + ADAM_WD * w)
    return delta, m, v


def adamw(w, g, m, v, name):
    r, c = w.shape
    tr = r // 4 if r % 32 == 0 else r

    def body(w_ref, g_ref, m_ref, v_ref, d_ref, nm_ref, nv_ref):
        d, nm, nv = _adam_update(w_ref[...], g_ref[...], m_ref[...], v_ref[...])
        d_ref[...] = d
        nm_ref[...] = nm
        nv_ref[...] = nv

    spec = pl.BlockSpec((tr, c), lambda t: (t, 0))
    return pl.pallas_call(
        body, name=name, grid=(r // tr,),
        in_specs=[spec] * 4, out_specs=[spec] * 3,
        out_shape=[jax.ShapeDtypeStruct((r, c), F32)] * 3,
        compiler_params=_cparams("parallel"),
    )(w, g, m, v)


def small_allreduce(pack):
    def body(p_ref, o_ref, gather, send_sems, recv_sems):
        x, y, cc = _mesh_pos()
        me = 4 * x + 2 * y + cc
        gather[me] = p_ref[...]
        flips = [(fx, fy, fc) for fx in (0, 1) for fy in (0, 1) for fc in (0, 1)][1:]
        copies = []
        for k, (fx, fy, fc) in enumerate(flips):
            copies.append(pltpu.make_async_remote_copy(
                src_ref=p_ref, dst_ref=gather.at[me],
                send_sem=send_sems.at[k], recv_sem=recv_sems.at[k],
                device_id=(x ^ fx, y ^ fy, cc ^ fc), device_id_type=MESH_ID))
        for cp in copies:
            cp.start()
        for k, (fx, fy, fc) in enumerate(flips):
            src = 4 * (x ^ fx) + 2 * (y ^ fy) + (cc ^ fc)
            pltpu.make_async_remote_copy(
                src_ref=p_ref, dst_ref=gather.at[src],
                send_sem=send_sems.at[k], recv_sem=recv_sems.at[k],
                device_id=(x ^ fx, y ^ fy, cc ^ fc), device_id_type=MESH_ID).wait_recv()
        for cp in copies:
            cp.wait_send()
        total = gather[0]
        for d in range(1, N_DEV):
            total = total + gather[d]
        o_ref[...] = total

    return pl.pallas_call(
        body, name="small_allreduce",
        in_specs=[pl.BlockSpec(memory_space=pltpu.VMEM)],
        out_specs=pl.BlockSpec(memory_space=pltpu.VMEM),
        out_shape=jax.ShapeDtypeStruct(pack.shape, pack.dtype),
        scratch_shapes=[pltpu.VMEM((N_DEV,) + pack.shape, pack.dtype),
                        pltpu.SemaphoreType.DMA((7,)), pltpu.SemaphoreType.DMA((7,))],
    )(pack)


def small_update(gsum, wpack, mpack, vpack):
    hw = HGRN_WIDTH

    def body(g_ref, w_ref, m_ref, v_ref, go_ref, d_ref, nm_ref, nv_ref, loss_ref):
        g = g_ref[...]
        wv = w_ref[...]
        a0, a1 = wv[4:5, :hw], wv[4:5, hw:]
        mx = jnp.maximum(a0, a1)
        e0, e1 = jnp.exp(a0 - mx), jnp.exp(a1 - mx)
        lb = e0 / (e0 + e1)
        dl = g[4:5, :hw] * lb * (1.0 - lb)
        row = lax.broadcasted_iota(jnp.int32, g.shape, 0)
        lb_row = jnp.concatenate([dl, -dl], axis=1)
        grads = jnp.where(row == 4, lb_row, jnp.where(row < 4, g, 0.0))
        go_ref[...] = grads
        d, nm, nv = _adam_update(wv, grads, m_ref[...], v_ref[...])
        d_ref[...] = d
        nm_ref[...] = nm
        nv_ref[...] = nv
        loss_ref[...] = jnp.zeros((8, LANES), F32) + jnp.sum(g[5:6, :])

    vm = pl.BlockSpec(memory_space=pltpu.VMEM)
    return pl.pallas_call(
        body, name="small_update",
        in_specs=[vm] * 4, out_specs=[vm] * 5,
        out_shape=[jax.ShapeDtypeStruct(gsum.shape, F32)] * 4 + [jax.ShapeDtypeStruct((8, LANES), F32)],
    )(gsum, wpack, mpack, vpack)


def _pack_small(n1, n2, fn, hn, lbl):
    z = jnp.zeros((1, D_MODEL - HGRN_WIDTH), F32)
    rows = [n1.reshape(1, D_MODEL), n2.reshape(1, D_MODEL), fn.reshape(1, D_MODEL),
            jnp.concatenate([hn.reshape(1, HGRN_WIDTH), z], axis=1), lbl.reshape(1, 2 * HGRN_WIDTH),
            jnp.zeros((3, D_MODEL), F32)]
    return jnp.concatenate(rows, axis=0)


def _unpack_small(pack):
    return (pack[0:1], pack[4].reshape(2, HGRN_WIDTH), pack[3:4, :HGRN_WIDTH], pack[1:2], pack[2])


def kernel(x, norm1_w, w_in, lb_logits, hgrn_norm_w, w_out, norm2_w, w_gate_up, w_down, final_norm_w, loss_target, m_norm1_w, m_w_in, m_lb_logits, m_hgrn_norm_w, m_w_out, m_norm2_w, m_w_gate_up, m_w_down, m_final_norm_w, v_norm1_w, v_w_in, v_lb_logits, v_hgrn_norm_w, v_w_out, v_norm2_w, v_w_gate_up, v_w_down, v_final_norm_w):
    seq = x.shape[1]
    xs = x.reshape(seq, D_MODEL)
    target = loss_target.reshape(seq, D_MODEL)
    shards = {"w_in": w_in[0], "w_out": w_out[0], "w_gu": w_gate_up[0], "w_down": w_down[0]}

    gathered = {k: allgather_halves(cast_bf16(w, "cast_" + k), "gather_" + k) for k, w in shards.items()}
    w_in4 = gathered["w_in"].reshape(N_CHIPS, D_MODEL, -1)
    w_out_f = gathered["w_out"].reshape(D_MODEL, D_MODEL)
    w_gu4 = gathered["w_gu"].reshape(N_CHIPS, D_MODEL, -1)
    w_down_f = gathered["w_down"].reshape(FFN_HIDDEN, D_MODEL)

    cos_t, sin_t = _rope_tables(seq)
    fw = final_norm_w.reshape(1, D_MODEL)

    qr, kr, va, hg, u = in_proj(xs, norm1_w, w_in4, cos_t, sin_t)
    fwd = [attn_fwd(qr, kr, va, d, "attn_fwd_d%d" % d) for _, d in DILATED_PAIRS]
    yb, o_pre, st0 = hgrn_fwd(hg, lb_logits, hgrn_norm_w)
    ya, lse, mixed, h1, u2 = mix_out([f[0] for f in fwd], [f[1] for f in fwd], yb, xs, w_out_f, norm2_w)
    g, up, act = gate_up(u2, w_gu4)
    dh2, acc_fin = down_loss(act, w_down_f, h1, fw, target)

    dgu = down_bwd(dh2, w_down_f, g, up)
    dh1, dya, dyb, delta, acc_n2 = gu_bwd(dgu, w_gu4, h1, norm2_w, dh2, w_out_f, ya)
    bwd = [attn_bwd(qr, kr, va, dya, lse, delta, d, "attn_bwd_d%d" % d) for _, d in DILATED_PAIRS]
    dhg, acc_hg = hgrn_bwd(hg, lb_logits, hgrn_norm_w, o_pre, st0, dyb)
    dproj, dx, acc_n1 = in_bwd([b[0] for b in bwd], [b[1] for b in bwd], [b[2] for b in bwd],
                               dhg, cos_t, sin_t, w_in4, xs, norm1_w, dh1)

    cw_in, cw_gu = w_in4.shape[2], w_gu4.shape[2]
    grads = [
        weight_grad(u, dproj, cw_in, "wgrad_in").reshape(N_CHIPS, 2, D_MODEL // 2, cw_in),
        weight_grad(mixed, dh1, D_MODEL, "wgrad_out").reshape(N_CHIPS, 2, D_MODEL // 8, D_MODEL),
        weight_grad(u2, dgu, cw_gu, "wgrad_gu").reshape(N_CHIPS, 2, D_MODEL // 2, cw_gu),
        weight_grad(act, dh2, D_MODEL, "wgrad_down").reshape(N_CHIPS, 2, FFN_HIDDEN // 8, D_MODEL),
    ]
    names = ["in", "out", "gu", "down"]
    recv = exchange_with_sibling(grads)
    sums = [add_own_half(gr, rc, "add_half_" + nm) for gr, rc, nm in zip(grads, recv, names)]
    parts = exchange_between_chips(sums)
    halves = [sum_chips(p, "sum_chips_" + nm) for p, nm in zip(parts, names)]
    full = share_with_sibling(halves)
    big = {}
    for nm, key, gsh, m_, v_ in zip(names, ["w_in", "w_out", "w_gu", "w_down"], full,
                                    [m_w_in, m_w_out, m_w_gate_up, m_w_down],
                                    [v_w_in, v_w_out, v_w_gate_up, v_w_down]):
        w2 = shards[key]
        g2 = gsh.reshape(w2.shape)
        d2, nm2, nv2 = adamw(w2, g2, m_[0], v_[0], "adamw_" + nm)
        big[key] = tuple(t[None] for t in (g2, d2, nm2, nv2))

    z512 = jnp.zeros((1, D_MODEL - HGRN_WIDTH), F32)
    gpack = jnp.concatenate([
        acc_n1[0:1], acc_n2[0:1], acc_fin[0:1],
        jnp.concatenate([acc_hg[0:1], z512], axis=1), jnp.concatenate([acc_hg[1:2], z512], axis=1),
        acc_fin[1:2], jnp.zeros((2, D_MODEL), F32)], axis=0)
    gsum = small_allreduce(gpack)
    wpack = _pack_small(norm1_w, norm2_w, final_norm_w, hgrn_norm_w, lb_logits)
    mpack = _pack_small(m_norm1_w, m_norm2_w, m_final_norm_w, m_hgrn_norm_w, m_lb_logits)
    vpack = _pack_small(v_norm1_w, v_norm2_w, v_final_norm_w, v_hgrn_norm_w, v_lb_logits)
    gs, ds, nms, nvs, loss8 = small_update(gsum, wpack, mpack, vpack)
    loss = loss8[0, 0]

    def assemble(small_pack, idx):
        n1, lbl, hn, n2, fn = _unpack_small(small_pack)
        return (n1, big["w_in"][idx], lbl, hn, big["w_out"][idx], n2, big["w_gu"][idx], big["w_down"][idx], fn)

    return (loss, dx.reshape(x.shape), *assemble(gs, 0), *assemble(ds, 1), *assemble(nms, 2), *assemble(nvs, 3))
```

```python
import functools

import jax
import jax.numpy as jnp
from jax import lax
from jax.experimental import pallas as pl
from jax.experimental.pallas import tpu as pltpu

F32 = jnp.float32
BF16 = jnp.bfloat16

D_MODEL = 1024
ATTN_WIDTH = 512
HEAD_DIM = 64
DILATED_PAIRS = ((128, 1), (512, 4), (2048, 16))
ATTN_BLOCK = 128
ROPE_THETA = 10000.0
HGRN_WIDTH = 512
HGRN_CHUNK = 16
HGRN_HEADS = 4
IN_PROJ_WIDTH = 3584
FFN_HIDDEN = 2816
NORM_EPS = 1e-6
ATTN_SCALE = HEAD_DIM ** -0.5
N_CHIPS = 4
N_DEV = 8

ADAM_LR = 0.001
ADAM_B1 = 0.9
ADAM_B2 = 0.999
ADAM_EPS = 1e-08
ADAM_WD = 0.01
ADAM_STEP = 10

LANES = 128
HGRN_ROWS = 128
ROW_TILE = 256
ATTN_STEP_ROWS = 2048
VMEM_LIMIT = 56 * 1024 * 1024
NEG_BIG = -1e30
MESH_ID = pl.DeviceIdType.MESH


def _cparams(*sem):
    return pltpu.CompilerParams(dimension_semantics=tuple(sem), vmem_limit_bytes=VMEM_LIMIT)


def _dot(a, b):
    return jnp.dot(a, b, preferred_element_type=F32)


def _dot_nt(a, b):
    return lax.dot_general(a, b, (((1,), (1,)), ((), ())), preferred_element_type=F32)


def _dot_tn(a, b):
    return lax.dot_general(a, b, (((0,), (0,)), ((), ())), preferred_element_type=F32)


def _sigmoid(x):
    return 1.0 / (1.0 + jnp.exp(-x))


def _full(shape):
    n = len(shape)
    return pl.BlockSpec(shape, lambda *_: (0,) * n)


def _rows(tm, width):
    return pl.BlockSpec((tm, width), lambda i: (i, 0))


def _swap32(x):
    lane = lax.broadcasted_iota(jnp.int32, x.shape, 1)
    first = (lane % HEAD_DIM) < (HEAD_DIM // 2)
    return jnp.where(first, pltpu.roll(x, LANES - 32, axis=1), pltpu.roll(x, 32, axis=1))


def _rotary_fwd(x, cos, sin_signed):
    parts = []
    for j in range(x.shape[1] // LANES):
        xc = x[:, j * LANES:(j + 1) * LANES]
        parts.append(xc * cos + _swap32(xc) * sin_signed)
    return jnp.concatenate(parts, axis=1)


def _rotary_bwd(dy, cos, sin_signed):
    parts = []
    for j in range(dy.shape[1] // LANES):
        dc = dy[:, j * LANES:(j + 1) * LANES]
        parts.append(dc * cos + _swap32(dc * sin_signed))
    return jnp.concatenate(parts, axis=1)


def _rope_tables(seq):
    half = HEAD_DIM // 2
    inv_freq = ROPE_THETA ** (-jnp.arange(half, dtype=F32) / half)
    ang = jnp.arange(seq, dtype=F32)[:, None] * inv_freq[None, :]
    cos, sin = jnp.cos(ang), jnp.sin(ang)
    cos_t = jnp.tile(cos, (1, LANES // half))
    sin_t = jnp.tile(jnp.concatenate([-sin, sin], axis=1), (1, LANES // HEAD_DIM))
    return cos_t, sin_t


def cast_bf16(w, name):
    r, c = w.shape
    half = r // 2

    def body(w_ref, o_ref):
        o_ref[...] = w_ref[...].astype(BF16)

    return pl.pallas_call(
        body, name=name, grid=(2,),
        in_specs=[pl.BlockSpec((half, c), lambda i: (i, 0))],
        out_specs=pl.BlockSpec((None, half, c), lambda i: (i, 0, 0)),
        out_shape=jax.ShapeDtypeStruct((2, half, c), BF16),
        compiler_params=_cparams("parallel"),
    )(w)


def _mesh_pos():
    return lax.axis_index("x"), lax.axis_index("y"), lax.axis_index("c")


def allgather_halves(halves, name):
    _, r, c = halves.shape

    def body(x_ref, out_ref, send_sems, recv_sems, local_sem):
        x, y, cc = _mesh_pos()
        me, sibling = (x, y, cc), (x, y, 1 - cc)
        chips = [(1 - x, y), (x, 1 - y), (1 - x, 1 - y)]
        mine_src = x_ref.at[cc]

        def rows(px, py, pc):
            return out_ref.at[4 * px + 2 * py + pc]

        def copy(k, block, to, src=None):
            return pltpu.make_async_remote_copy(
                src_ref=rows(*block) if src is None else src, dst_ref=rows(*block),
                send_sem=send_sems.at[k], recv_sem=recv_sems.at[k],
                device_id=to, device_id_type=MESH_ID)

        mine = pltpu.make_async_copy(mine_src, rows(*me), local_sem)
        mine.start()
        first = [copy(0, me, sibling, src=mine_src)]
        first += [copy(1 + j, me, (*chip, cc), src=mine_src) for j, chip in enumerate(chips)]
        for cp in first:
            cp.start()
        passed = [copy(4 + j, (*chip, cc), sibling) for j, chip in enumerate(chips)]
        for j, chip in enumerate(chips):
            copy(1 + j, (*chip, cc), me).wait_recv()
            passed[j].start()
        copy(0, sibling, me).wait_recv()
        for j, chip in enumerate(chips):
            copy(4 + j, (*chip, 1 - cc), me).wait_recv()
        for cp in first + passed:
            cp.wait_send()
        mine.wait()

    return pl.pallas_call(
        body, name=name,
        in_specs=[pl.BlockSpec(memory_space=pl.ANY)],
        out_specs=pl.BlockSpec(memory_space=pl.ANY),
        out_shape=jax.ShapeDtypeStruct((N_DEV, r, c), halves.dtype),
        scratch_shapes=[pltpu.SemaphoreType.DMA((7,)), pltpu.SemaphoreType.DMA((7,)),
                        pltpu.SemaphoreType.DMA],
    )(halves)


def _rms(x):
    return lax.rsqrt(jnp.mean(x * x, axis=-1, keepdims=True) + NORM_EPS)


def in_proj(x, norm1_w, w_in4, cos_t, sin_t):
    seq = x.shape[0]
    tm = ROW_TILE
    cw = w_in4.shape[2]

    def body(x_ref, nw_ref, w_ref, cos_ref, sin_ref, q_ref, k_ref, v_ref, hg_ref, u_ref):
        xv = x_ref[...]
        u = ((xv * _rms(xv)) * nw_ref[...]).astype(BF16)
        u_ref[...] = u
        proj = jnp.concatenate([_dot(u, w_ref[j]) for j in range(N_CHIPS)], axis=1)
        cos, sin = cos_ref[...], sin_ref[...]
        a = ATTN_WIDTH
        q_ref[...] = _rotary_fwd(proj[:, :a], cos, sin)
        k_ref[...] = _rotary_fwd(proj[:, a:2 * a], cos, sin)
        v_ref[...] = proj[:, 2 * a:3 * a]
        hg_ref[...] = proj[:, 3 * a:]

    return pl.pallas_call(
        body, name="in_proj", grid=(seq // tm,),
        in_specs=[_rows(tm, D_MODEL), _full((1, D_MODEL)), _full((N_CHIPS, D_MODEL, cw)),
                  _rows(tm, LANES), _rows(tm, LANES)],
        out_specs=[_rows(tm, ATTN_WIDTH)] * 3 + [_rows(tm, 4 * HGRN_WIDTH), _rows(tm, D_MODEL)],
        out_shape=[jax.ShapeDtypeStruct((seq, ATTN_WIDTH), F32)] * 3
        + [jax.ShapeDtypeStruct((seq, 4 * HGRN_WIDTH), F32), jax.ShapeDtypeStruct((seq, D_MODEL), BF16)],
        compiler_params=_cparams("parallel"),
    )(x, norm1_w, w_in4, cos_t, sin_t)


def _head_masks():
    lane = lax.broadcasted_iota(jnp.int32, (1, LANES), 1)
    return [(lane // HEAD_DIM) == h for h in range(LANES // HEAD_DIM)]


def _window_valid(no_prev):
    qi = lax.broadcasted_iota(jnp.int32, (ATTN_BLOCK, 2 * ATTN_BLOCK), 0)
    kj = lax.broadcasted_iota(jnp.int32, (ATTN_BLOCK, 2 * ATTN_BLOCK), 1)
    valid = (kj >= qi) & (kj <= qi + ATTN_BLOCK)
    return valid & (jnp.logical_not(no_prev) | (kj >= ATTN_BLOCK))


def _strided_rows(start, dilation):
    if dilation == 1:
        return pl.ds(start, ATTN_BLOCK)
    return pl.ds(start, ATTN_BLOCK, stride=dilation)


def _attn_specs(seq, dilation):
    span = ATTN_BLOCK * dilation
    per_step = ATTN_STEP_ROWS // span
    cur = pl.BlockSpec((ATTN_STEP_ROWS, LANES), lambda hp, j: (j, hp))
    prev = pl.BlockSpec((span, LANES), lambda hp, j: (jnp.maximum(j * per_step - 1, 0), hp))
    return span, per_step, cur, prev


def attn_fwd(q, k, v, dilation, name):
    seq = q.shape[0]
    span, per_step, cur, prev = _attn_specs(seq, dilation)

    def body(q_ref, kc_ref, vc_ref, kp_ref, vp_ref, o_ref, lse_ref):
        first_step = pl.program_id(1) == 0
        masks = _head_masks()

        def block(it, carry):
            t, r = it // dilation, it % dilation
            rows = _strided_rows(r + span * t, dilation)
            before = _strided_rows(r + span * jnp.maximum(t - 1, 0), dilation)
            edge = _strided_rows(r, dilation)
            at_edge = t == 0
            q2 = q_ref[rows, :].astype(BF16)
            kp = jnp.where(at_edge, kp_ref[edge, :], kc_ref[before, :])
            vp = jnp.where(at_edge, vp_ref[edge, :], vc_ref[before, :])
            k2 = jnp.concatenate([kp, kc_ref[rows, :]], axis=0).astype(BF16)
            v2 = jnp.concatenate([vp, vc_ref[rows, :]], axis=0).astype(BF16)
            valid = _window_valid(first_step & at_edge)
            o_acc = jnp.zeros((ATTN_BLOCK, LANES), F32)
            l_acc = jnp.zeros((ATTN_BLOCK, LANES), F32)
            for mh in masks:
                qm = jnp.where(mh, q2, jnp.zeros_like(q2))
                s = jnp.where(valid, _dot_nt(qm, k2) * ATTN_SCALE, NEG_BIG)
                m = jnp.max(s, axis=-1, keepdims=True)
                p = jnp.exp(s - m)
                l = jnp.sum(p, axis=-1, keepdims=True)
                o = _dot(p.astype(BF16), v2) / l
                o_acc = jnp.where(mh, o, o_acc)
                l_acc = jnp.where(mh, m + jnp.log(l), l_acc)
            o_ref[rows, :] = o_acc
            lse_ref[rows, :] = l_acc
            return carry

        lax.fori_loop(0, per_step * dilation, block, 0)

    return pl.pallas_call(
        body, name=name, grid=(ATTN_WIDTH // LANES, seq // ATTN_STEP_ROWS),
        in_specs=[cur, cur, cur, prev, prev],
        out_specs=[cur, cur],
        out_shape=[jax.ShapeDtypeStruct((seq, ATTN_WIDTH), F32)] * 2,
        compiler_params=_cparams("parallel", "parallel"),
    )(q, k, v, k, v)


def _hgrn_constants():
    t = jnp.arange(HGRN_ROWS)[:, None]
    s = jnp.arange(HGRN_ROWS)[None, :]
    same = (t // HGRN_CHUNK) == (s // HGRN_CHUNK)
    cum = same & (s <= t)
    mid = same & ((s % HGRN_CHUNK) <= HGRN_CHUNK // 2 - 1)
    fwd = jnp.concatenate([cum, mid, same], axis=0).astype(BF16)
    bwd = jnp.concatenate([same & (s >= t), same], axis=0).astype(BF16)
    return fwd, bwd


def _exact_rowmix(mat, x):
    hi = x.astype(BF16)
    r1 = x - hi.astype(F32)
    mid = r1.astype(BF16)
    lo = (r1 - mid.astype(F32)).astype(BF16)
    return _dot(mat, hi) + _dot(mat, mid) + _dot(mat, lo)


def _hgrn_prep(hg, lbl, mats):
    w = HGRN_WIDTH
    a0, a1 = lbl[0:1, :], lbl[1:2, :]
    mx = jnp.maximum(a0, a1)
    e0, e1 = jnp.exp(a0 - mx), jnp.exp(a1 - mx)
    lb = e0 / (e0 + e1)
    qb, fb, gb = hg[:, :w], hg[:, w:2 * w], hg[:, 3 * w:]
    sg = _sigmoid(fb)
    f = lb + (1.0 - lb) * sg
    mixed = _exact_rowmix(mats, jnp.log(f))
    b, bmid, btot = mixed[:HGRN_ROWS], mixed[HGRN_ROWS:2 * HGRN_ROWS], mixed[2 * HGRN_ROWS:]
    sq = _sigmoid(qb)
    p = dict(lb=lb, sg=sg, f=f, kk=1.0 - f, sq=sq, qf=qb * sq, gb=gb,
             e_iq=jnp.exp(b - bmid), e_ik=jnp.exp(bmid - b), e_b=jnp.exp(b),
             e_bb=jnp.exp(btot - b), e_tot=jnp.exp(btot))
    p["qi"] = p["qf"] * p["e_iq"]
    p["ki"] = p["kk"] * p["e_ik"]
    p["qs"] = p["qf"] * p["e_b"]
    p["kb"] = p["kk"] * p["e_bb"]
    return p


def _chunk_masks():
    t = lax.broadcasted_iota(jnp.int32, (HGRN_ROWS, HGRN_ROWS), 0)
    s = lax.broadcasted_iota(jnp.int32, (HGRN_ROWS, HGRN_ROWS), 1)
    tril = ((t // HGRN_CHUNK) == (s // HGRN_CHUNK)) & (s <= t)
    n_chunks = HGRN_ROWS // HGRN_CHUNK
    tt = lax.broadcasted_iota(jnp.int32, (HGRN_ROWS, n_chunks * LANES), 0)
    cc = lax.broadcasted_iota(jnp.int32, (HGRN_ROWS, n_chunks * LANES), 1)
    block = (tt // HGRN_CHUNK) == (cc // LANES)
    return tril, block


def _spread(x, block):
    n_chunks = HGRN_ROWS // HGRN_CHUNK
    return jnp.where(block, jnp.tile(x, (1, n_chunks)), jnp.zeros((), x.dtype))


def _fold(x_full, block):
    n_chunks = HGRN_ROWS // HGRN_CHUNK
    z = jnp.where(block, x_full, 0.0)
    acc = z[:, :LANES]
    for n in range(1, n_chunks):
        acc = acc + z[:, n * LANES:(n + 1) * LANES]
    return acc


def hgrn_fwd(hg, lb_logits, hnw):
    seq = hg.shape[0]
    nblk = seq // HGRN_ROWS
    n_chunks = HGRN_ROWS // HGRN_CHUNK
    mats, _ = _hgrn_constants()

    def body(hg_ref, lbl_ref, hnw_ref, mats_ref, yb_ref, o_ref, st0_ref, st_scr):
        @pl.when(pl.program_id(0) == 0)
        def _():
            st_scr[...] = jnp.zeros_like(st_scr)

        hg_v = hg_ref[...]
        p = _hgrn_prep(hg_v, lbl_ref[...], mats_ref[...])
        tril, block = _chunk_masks()
        vv = hg_v[:, 2 * HGRN_WIDTH:3 * HGRN_WIDTH].astype(BF16)
        outs = []
        for h in range(HGRN_HEADS):
            sl = slice(h * LANES, (h + 1) * LANES)
            v_h = vv[:, sl]
            a = jnp.where(tril, _dot_nt(p["qi"][:, sl].astype(BF16), p["ki"][:, sl].astype(BF16)), 0.0)
            o = _dot(a.astype(BF16), v_h)
            upd = _dot_tn(v_h, _spread(p["kb"][:, sl].astype(BF16), block))
            st = st_scr[h]
            st0_ref[h] = st
            parts = []
            for n in range(n_chunks):
                parts.append(st.astype(BF16))
                decay = p["e_tot"][n * HGRN_CHUNK:n * HGRN_CHUNK + 1, sl]
                st = st * decay + upd[:, n * LANES:(n + 1) * LANES]
            st_scr[h] = st
            o = o + _dot_nt(_spread(p["qs"][:, sl].astype(BF16), block), jnp.concatenate(parts, axis=1))
            outs.append(o)
        o_all = jnp.concatenate(outs, axis=1)
        o_ref[...] = o_all
        normed = jnp.concatenate(
            [outs[h] * _rms(outs[h]) for h in range(HGRN_HEADS)], axis=1)
        gb = p["gb"]
        yb_ref[...] = (normed * hnw_ref[...]) * (gb * _sigmoid(gb))

    return pl.pallas_call(
        body, name="hgrn_fwd", grid=(nblk,),
        in_specs=[_rows(HGRN_ROWS, 4 * HGRN_WIDTH), _full((2, HGRN_WIDTH)), _full((1, HGRN_WIDTH)),
                  _full(mats.shape)],
        out_specs=[_rows(HGRN_ROWS, HGRN_WIDTH), _rows(HGRN_ROWS, HGRN_WIDTH),
                   pl.BlockSpec((None, HGRN_HEADS, LANES, LANES), lambda i: (i, 0, 0, 0))],
        out_shape=[jax.ShapeDtypeStruct((seq, HGRN_WIDTH), F32)] * 2
        + [jax.ShapeDtypeStruct((nblk, HGRN_HEADS, LANES, LANES), F32)],
        scratch_shapes=[pltpu.VMEM((HGRN_HEADS, LANES, LANES), F32)],
        compiler_params=_cparams("arbitrary"),
    )(hg, lb_logits, hnw, mats)


def mix_out(outs, lses, yb, x, w_out, norm2_w):
    seq = x.shape[0]
    tm = ROW_TILE

    def body(o1, o2, o3, l1, l2, l3, yb_ref, x_ref, w_ref, nw_ref,
             ya_ref, lse_ref, mixed_ref, h1_ref, u2_ref):
        l1v, l2v, l3v = l1[...], l2[...], l3[...]
        mx = jnp.maximum(jnp.maximum(l1v, l2v), l3v)
        e1, e2, e3 = jnp.exp(l1v - mx), jnp.exp(l2v - mx), jnp.exp(l3v - mx)
        den = e1 + e2 + e3
        ya = (e1 * o1[...] + e2 * o2[...] + e3 * o3[...]) / den
        ya_ref[...] = ya
        lse_ref[...] = mx + jnp.log(den)
        mixed = jnp.concatenate([ya, yb_ref[...]], axis=1).astype(BF16)
        mixed_ref[...] = mixed
        h1 = x_ref[...] + _dot(mixed, w_ref[...])
        h1_ref[...] = h1
        u2_ref[...] = ((h1 * _rms(h1)) * nw_ref[...]).astype(BF16)

    half = _rows(tm, ATTN_WIDTH)
    wide = _rows(tm, D_MODEL)
    return pl.pallas_call(
        body, name="mix_out", grid=(seq // tm,),
        in_specs=[half] * 7 + [wide, _full((D_MODEL, D_MODEL)), _full((1, D_MODEL))],
        out_specs=[half, half, wide, wide, wide],
        out_shape=[jax.ShapeDtypeStruct((seq, ATTN_WIDTH), F32)] * 2
        + [jax.ShapeDtypeStruct((seq, D_MODEL), BF16), jax.ShapeDtypeStruct((seq, D_MODEL), F32),
           jax.ShapeDtypeStruct((seq, D_MODEL), BF16)],
        compiler_params=_cparams("parallel"),
    )(*outs, *lses, yb, x, w_out, norm2_w)


def gate_up(u2, w_gu4):
    seq = u2.shape[0]
    tm = ROW_TILE
    cw = w_gu4.shape[2]

    def body(u_ref, w_ref, g_ref, up_ref, act_ref):
        u = u_ref[...]
        g = jnp.concatenate([_dot(u, w_ref[0]), _dot(u, w_ref[1])], axis=1)
        up = jnp.concatenate([_dot(u, w_ref[2]), _dot(u, w_ref[3])], axis=1)
        g_ref[...] = g.astype(BF16)
        up_ref[...] = up.astype(BF16)
        act_ref[...] = ((g * _sigmoid(g)) * up).astype(BF16)

    ffn = _rows(tm, FFN_HIDDEN)
    return pl.pallas_call(
        body, name="gate_up", grid=(seq // tm,),
        in_specs=[_rows(tm, D_MODEL), _full((N_CHIPS, D_MODEL, cw))],
        out_specs=[ffn] * 3,
        out_shape=[jax.ShapeDtypeStruct((seq, FFN_HIDDEN), BF16)] * 3,
        compiler_params=_cparams("parallel"),
    )(u2, w_gu4)


def down_loss(act, w_down, h1, final_w, target):
    seq = h1.shape[0]
    tm = ROW_TILE
    inv_d = 1.0 / D_MODEL

    def body(act_ref, w_ref, h1_ref, fw_ref, t_ref, dh2_ref, acc_ref):
        @pl.when(pl.program_id(0) == 0)
        def _():
            acc_ref[...] = jnp.zeros_like(acc_ref)

        h2 = h1_ref[...] + _dot(act_ref[...], w_ref[...])
        rf = _rms(h2)
        n = h2 * rf
        fw = fw_ref[...]
        err = n * fw - t_ref[...]
        dy = err * inv_d
        acc_ref[0:1, :] += jnp.sum(dy * n, axis=0, keepdims=True)
        acc_ref[1:2, :] += (0.5 * inv_d) * jnp.sum(err * err, axis=0, keepdims=True)
        dn = dy * fw
        dh2_ref[...] = rf * (dn - n * jnp.mean(dn * n, axis=-1, keepdims=True))

    wide = _rows(tm, D_MODEL)
    return pl.pallas_call(
        body, name="down_loss", grid=(seq // tm,),
        in_specs=[_rows(tm, FFN_HIDDEN), _full((FFN_HIDDEN, D_MODEL)), wide, _full((1, D_MODEL)), wide],
        out_specs=[wide, _full((8, D_MODEL))],
        out_shape=[jax.ShapeDtypeStruct((seq, D_MODEL), F32), jax.ShapeDtypeStruct((8, D_MODEL), F32)],
        compiler_params=_cparams("arbitrary"),
    )(act, w_down, h1, final_w, target)


def down_bwd(dh2, w_down, g, up):
    seq = dh2.shape[0]
    tm = ROW_TILE

    def body(dh_ref, w_ref, g_ref, up_ref, dgu_ref):
        dact = _dot_nt(dh_ref[...].astype(BF16), w_ref[...])
        gv = g_ref[...].astype(F32)
        sg = _sigmoid(gv)
        dgu_ref[:, :FFN_HIDDEN] = (dact * up_ref[...].astype(F32) * (sg * (1.0 + gv * (1.0 - sg)))).astype(BF16)
        dgu_ref[:, FFN_HIDDEN:] = (dact * (gv * sg)).astype(BF16)

    ffn = _rows(tm, FFN_HIDDEN)
    return pl.pallas_call(
        body, name="down_bwd", grid=(seq // tm,),
        in_specs=[_rows(tm, D_MODEL), _full((FFN_HIDDEN, D_MODEL)), ffn, ffn],
        out_specs=_rows(tm, 2 * FFN_HIDDEN),
        out_shape=jax.ShapeDtypeStruct((seq, 2 * FFN_HIDDEN), BF16),
        compiler_params=_cparams("parallel"),
    )(dh2, w_down, g, up)


def _head_sum_matrix():
    i = jnp.arange(ATTN_WIDTH)
    return ((i[:, None] // HEAD_DIM) == (i[None, :] // HEAD_DIM)).astype(BF16)


def gu_bwd(dgu, w_gu4, h1, norm2_w, dh2, w_out, ya):
    seq = h1.shape[0]
    tm = ROW_TILE
    cw = w_gu4.shape[2]
    hsum = _head_sum_matrix()

    def body(dgu_ref, w_ref, h1_ref, nw_ref, dh2_ref, wo_ref, ya_ref, hs_ref,
             dh1_ref, dya_ref, dyb_ref, delta_ref, acc_ref):
        @pl.when(pl.program_id(0) == 0)
        def _():
            acc_ref[...] = jnp.zeros_like(acc_ref)

        du2 = _dot_nt(dgu_ref[:, :cw], w_ref[0])
        for j in range(1, N_CHIPS):
            du2 = du2 + _dot_nt(dgu_ref[:, j * cw:(j + 1) * cw], w_ref[j])
        h1 = h1_ref[...]
        r2 = _rms(h1)
        nh = h1 * r2
        acc_ref[0:1, :] += jnp.sum(du2 * nh, axis=0, keepdims=True)
        dn = du2 * nw_ref[...]
        dh1 = dh2_ref[...] + r2 * (dn - nh * jnp.mean(dn * nh, axis=-1, keepdims=True))
        dh1_ref[...] = dh1
        dmixed = _dot_nt(dh1.astype(BF16), wo_ref[...])
        dya = dmixed[:, :ATTN_WIDTH]
        dya_ref[...] = dya
        dyb_ref[...] = dmixed[:, ATTN_WIDTH:]
        prod = dya * ya_ref[...]
        hi = prod.astype(BF16)
        lo = (prod - hi.astype(F32)).astype(BF16)
        delta_ref[...] = _dot(hi, hs_ref[...]) + _dot(lo, hs_ref[...])

    wide = _rows(tm, D_MODEL)
    half = _rows(tm, ATTN_WIDTH)
    return pl.pallas_call(
        body, name="gu_bwd", grid=(seq // tm,),
        in_specs=[_rows(tm, 2 * FFN_HIDDEN), _full((N_CHIPS, D_MODEL, cw)), wide, _full((1, D_MODEL)), wide,
                  _full((D_MODEL, D_MODEL)), half, _full((ATTN_WIDTH, ATTN_WIDTH))],
        out_specs=[wide, half, half, half, _full((8, D_MODEL))],
        out_shape=[jax.ShapeDtypeStruct((seq, D_MODEL), F32), jax.ShapeDtypeStruct((seq, ATTN_WIDTH), F32),
                   jax.ShapeDtypeStruct((seq, ATTN_WIDTH), F32), jax.ShapeDtypeStruct((seq, ATTN_WIDTH), F32),
                   jax.ShapeDtypeStruct((8, D_MODEL), F32)],
        compiler_params=_cparams("arbitrary"),
    )(dgu, w_gu4, h1, norm2_w, dh2, w_out, ya, hsum)


def attn_bwd(q, k, v, dy, lse, delta, dilation, name):
    seq = q.shape[0]
    span, per_step, cur, prev = _attn_specs(seq, dilation)
    whole = pl.BlockSpec((seq, LANES), lambda hp, j: (0, hp))

    def body(q_ref, dy_ref, lse_ref, dl_ref, kc_ref, vc_ref, kp_ref, vp_ref, dq_ref, dk_ref, dv_ref):
        first_step = pl.program_id(1) == 0
        base = pl.program_id(1) * ATTN_STEP_ROWS
        masks = _head_masks()

        def block(it, carry):
            t, r = it // dilation, it % dilation
            rows = _strided_rows(r + span * t, dilation)
            before = _strided_rows(r + span * jnp.maximum(t - 1, 0), dilation)
            edge = _strided_rows(r, dilation)
            at_edge = t == 0
            q2, dy2 = q_ref[rows, :].astype(BF16), dy_ref[rows, :].astype(BF16)
            lse2, dl2 = lse_ref[rows, :], dl_ref[rows, :]
            kp = jnp.where(at_edge, kp_ref[edge, :], kc_ref[before, :])
            vp = jnp.where(at_edge, vp_ref[edge, :], vc_ref[before, :])
            k2 = jnp.concatenate([kp, kc_ref[rows, :]], axis=0).astype(BF16)
            v2 = jnp.concatenate([vp, vc_ref[rows, :]], axis=0).astype(BF16)
            valid = _window_valid(first_step & at_edge)
            zero = jnp.zeros_like(q2)
            qms, dyms, ps, dss, kms = [], [], [], [], []
            for h, mh in enumerate(masks):
                c0 = h * HEAD_DIM
                qm, dym = jnp.where(mh, q2, zero), jnp.where(mh, dy2, zero)
                s = _dot_nt(qm, k2) * ATTN_SCALE
                p = jnp.where(valid, jnp.exp(s - lse2[:, c0:c0 + 1]), 0.0)
                dp = _dot_nt(dym, v2)
                dss.append((p * (dp - dl2[:, c0:c0 + 1]) * ATTN_SCALE).astype(BF16))
                ps.append(p.astype(BF16))
                qms.append(qm)
                dyms.append(dym)
                kms.append(jnp.where(mh, k2, jnp.zeros_like(k2)))
            dq_ref[rows, :] = _dot(jnp.concatenate(dss, axis=1), jnp.concatenate(kms, axis=0))
            dv_full = _dot_tn(jnp.concatenate(ps, axis=0), jnp.concatenate(dyms, axis=0))
            dk_full = _dot_tn(jnp.concatenate(dss, axis=0), jnp.concatenate(qms, axis=0))
            here = _strided_rows(base + r + span * t, dilation)
            dk_ref[here, :] = dk_full[ATTN_BLOCK:]
            dv_ref[here, :] = dv_full[ATTN_BLOCK:]

            @pl.when(jnp.logical_not(first_step & at_edge))
            def _():
                back = _strided_rows(base + r + span * t - span, dilation)
                dk_ref[back, :] += dk_full[:ATTN_BLOCK]
                dv_ref[back, :] += dv_full[:ATTN_BLOCK]

            return carry

        lax.fori_loop(0, per_step * dilation, block, 0)

    return pl.pallas_call(
        body, name=name, grid=(ATTN_WIDTH // LANES, seq // ATTN_STEP_ROWS),
        in_specs=[cur] * 6 + [prev, prev],
        out_specs=[cur, whole, whole],
        out_shape=[jax.ShapeDtypeStruct((seq, ATTN_WIDTH), F32)] * 3,
        compiler_params=_cparams("parallel", "arbitrary"),
    )(q, dy, lse, delta, k, v, k, v)


def hgrn_bwd(hg, lb_logits, hnw, o_pre, st0, dyb):
    seq = hg.shape[0]
    nblk = seq // HGRN_ROWS
    n_chunks = HGRN_ROWS // HGRN_CHUNK
    mats, mats_b = _hgrn_constants()
    w = HGRN_WIDTH

    def body(hg_ref, lbl_ref, hnw_ref, mats_ref, matsb_ref, o_ref, st0_ref, dyb_ref,
             dhg_ref, acc_ref, dst_scr):
        @pl.when(pl.program_id(0) == 0)
        def _():
            dst_scr[...] = jnp.zeros_like(dst_scr)
            acc_ref[...] = jnp.zeros_like(acc_ref)

        hg_v = hg_ref[...]
        p = _hgrn_prep(hg_v, lbl_ref[...], mats_ref[...])
        tril, block = _chunk_masks()
        chunk_of_row = lax.broadcasted_iota(jnp.int32, (HGRN_ROWS, 1), 0) // HGRN_CHUNK
        vv = hg_v[:, 2 * w:3 * w].astype(BF16)
        hnw_v = hnw_ref[...]
        gb = p["gb"]
        sgg = _sigmoid(gb)
        silu_g = gb * sgg
        dyb_v = dyb_ref[...]
        o_v = o_ref[...]

        d_on = dyb_v * hnw_v * silu_g
        on_parts, do_parts = [], []
        for h in range(HGRN_HEADS):
            sl = slice(h * LANES, (h + 1) * LANES)
            rs = _rms(o_v[:, sl])
            on = o_v[:, sl] * rs
            on_parts.append(on)
            do_parts.append(rs * (d_on[:, sl] - on * jnp.mean(d_on[:, sl] * on, axis=-1, keepdims=True)))
        on_all = jnp.concatenate(on_parts, axis=1)
        dgb = dyb_v * on_all * hnw_v * (sgg * (1.0 + gb * (1.0 - sgg)))
        acc_ref[0:1, :] += jnp.sum(dyb_v * on_all * silu_g, axis=0, keepdims=True)

        dqf_parts, dkk_parts, db_parts, dv_parts, dbt_parts, dkbkb_parts = [], [], [], [], [], []
        for h in range(HGRN_HEADS):
            sl = slice(h * LANES, (h + 1) * LANES)
            v_h = vv[:, sl]
            do_h = do_parts[h].astype(BF16)
            qi, ki, qs, kb = p["qi"][:, sl], p["ki"][:, sl], p["qs"][:, sl], p["kb"][:, sl]
            qi_b, ki_b = qi.astype(BF16), ki.astype(BF16)
            kb_cat = _spread(kb.astype(BF16), block)
            qs_cat = _spread(qs.astype(BF16), block)
            upd = _dot_tn(v_h, kb_cat)
            st = st0_ref[h]
            st_parts = []
            for n in range(n_chunks):
                st_parts.append(st)
                decay = p["e_tot"][n * HGRN_CHUNK:n * HGRN_CHUNK + 1, sl]
                st = st * decay + upd[:, n * LANES:(n + 1) * LANES]
            st_cat = jnp.concatenate([s_.astype(BF16) for s_ in st_parts], axis=1)
            wgt = _dot_tn(do_h, qs_cat)
            dst = dst_scr[h]
            dst_parts = [None] * n_chunks
            dbt_rows = [None] * n_chunks
            for n in reversed(range(n_chunks)):
                dst_parts[n] = dst.astype(BF16)
                decay = p["e_tot"][n * HGRN_CHUNK:n * HGRN_CHUNK + 1, sl]
                dbt_rows[n] = jnp.sum(dst * st_parts[n], axis=0, keepdims=True) * decay
                dst = dst * decay + wgt[:, n * LANES:(n + 1) * LANES]
            dst_scr[h] = dst
            dst_cat = jnp.concatenate(dst_parts, axis=1)
            dqs = _fold(_dot(do_h, st_cat), block)
            dkb = _fold(_dot(v_h, dst_cat), block)
            dv_state = _dot_nt(kb_cat, dst_cat)
            a = jnp.where(tril, _dot_nt(qi_b, ki_b), 0.0).astype(BF16)
            da = jnp.where(tril, _dot_nt(do_h, v_h), 0.0).astype(BF16)
            dv_parts.append(_dot_tn(a, do_h) + dv_state)
            dqi = _dot(da, ki_b)
            dki = _dot_tn(da, qi_b)
            dqf_parts.append(dqi * p["e_iq"][:, sl] + dqs * p["e_b"][:, sl])
            dkk_parts.append(dki * p["e_ik"][:, sl] + dkb * p["e_bb"][:, sl])
            dkbkb = dkb * kb
            db_parts.append(dqi * qi - dki * ki + dqs * qs - dkbkb)
            dkbkb_parts.append(dkbkb)
            dbt = jnp.zeros((HGRN_ROWS, LANES), F32)
            for n in range(n_chunks):
                dbt = jnp.where(chunk_of_row == n, dbt_rows[n], dbt)
            dbt_parts.append(dbt)

        cat = lambda parts: jnp.concatenate(parts, axis=1)
        mb = matsb_ref[...]
        dlogf = (_exact_rowmix(mb[:HGRN_ROWS], cat(db_parts))
                 + _exact_rowmix(mb[HGRN_ROWS:], cat(dkbkb_parts)) + cat(dbt_parts))
        sq, qb = p["sq"], hg_v[:, :w]
        dqb = cat(dqf_parts) * (sq * (1.0 + qb * (1.0 - sq)))
        df = dlogf / p["f"] - cat(dkk_parts)
        sg, lb = p["sg"], p["lb"]
        dfb = df * (1.0 - lb) * sg * (1.0 - sg)
        acc_ref[1:2, :] += jnp.sum(df * (1.0 - sg), axis=0, keepdims=True)
        dhg_ref[...] = jnp.concatenate([dqb, dfb, cat(dv_parts), dgb], axis=1)

    rev = lambda i: (nblk - 1 - i, 0)
    return pl.pallas_call(
        body, name="hgrn_bwd", grid=(nblk,),
        in_specs=[pl.BlockSpec((HGRN_ROWS, 4 * w), rev), _full((2, w)), _full((1, w)),
                  _full(mats.shape), _full(mats_b.shape), pl.BlockSpec((HGRN_ROWS, w), rev),
                  pl.BlockSpec((None, HGRN_HEADS, LANES, LANES), lambda i: (nblk - 1 - i, 0, 0, 0)),
                  pl.BlockSpec((HGRN_ROWS, w), rev)],
        out_specs=[pl.BlockSpec((HGRN_ROWS, 4 * w), rev), _full((8, w))],
        out_shape=[jax.ShapeDtypeStruct((seq, 4 * w), F32), jax.ShapeDtypeStruct((8, w), F32)],
        scratch_shapes=[pltpu.VMEM((HGRN_HEADS, LANES, LANES), F32)],
        compiler_params=_cparams("arbitrary"),
    )(hg, lb_logits, hnw, mats, mats_b, o_pre, st0, dyb)


def in_bwd(dqs, dks, dvs, dhg, cos_t, sin_t, w_in4, x, norm1_w, dh1):
    seq = x.shape[0]
    tm = ROW_TILE
    cw = w_in4.shape[2]

    def body(dq1, dq2, dq3, dk1, dk2, dk3, dv1, dv2, dv3, dhg_ref, cos_ref, sin_ref, w_ref,
             x_ref, nw_ref, dh1_ref, dproj_ref, dx_ref, acc_ref):
        @pl.when(pl.program_id(0) == 0)
        def _():
            acc_ref[...] = jnp.zeros_like(acc_ref)

        cos, sin = cos_ref[...], sin_ref[...]
        dqa = _rotary_bwd(dq1[...] + dq2[...] + dq3[...], cos, sin)
        dka = _rotary_bwd(dk1[...] + dk2[...] + dk3[...], cos, sin)
        dva = dv1[...] + dv2[...] + dv3[...]
        dproj = jnp.concatenate([dqa, dka, dva, dhg_ref[...]], axis=1).astype(BF16)
        dproj_ref[...] = dproj
        du = _dot_nt(dproj[:, :cw], w_ref[0])
        for j in range(1, N_CHIPS):
            du = du + _dot_nt(dproj[:, j * cw:(j + 1) * cw], w_ref[j])
        xv = x_ref[...]
        r1 = _rms(xv)
        nx = xv * r1
        acc_ref[0:1, :] += jnp.sum(du * nx, axis=0, keepdims=True)
        dn = du * nw_ref[...]
        dx_ref[...] = dh1_ref[...] + r1 * (dn - nx * jnp.mean(dn * nx, axis=-1, keepdims=True))

    half = _rows(tm, ATTN_WIDTH)
    wide = _rows(tm, D_MODEL)
    return pl.pallas_call(
        body, name="in_bwd", grid=(seq // tm,),
        in_specs=[half] * 9 + [_rows(tm, 4 * HGRN_WIDTH), _rows(tm, LANES), _rows(tm, LANES),
                               _full((N_CHIPS, D_MODEL, cw)), wide, _full((1, D_MODEL)), wide],
        out_specs=[_rows(tm, IN_PROJ_WIDTH), wide, _full((8, D_MODEL))],
        out_shape=[jax.ShapeDtypeStruct((seq, IN_PROJ_WIDTH), BF16), jax.ShapeDtypeStruct((seq, D_MODEL), F32),
                   jax.ShapeDtypeStruct((8, D_MODEL), F32)],
        compiler_params=_cparams("arbitrary"),
    )(*dqs, *dks, *dvs, dhg, cos_t, sin_t, w_in4, x, norm1_w, dh1)


def weight_grad(a, b, col_block, name):
    seq, kdim = a.shape
    ndim = b.shape[1]
    nj = ndim // col_block
    tk = 512

    def body(a_ref, b_ref, o_ref):
        @pl.when(pl.program_id(1) == 0)
        def _():
            o_ref[...] = jnp.zeros_like(o_ref)

        o_ref[...] += _dot_tn(a_ref[...].astype(BF16), b_ref[...].astype(BF16))

    return pl.pallas_call(
        body, name=name, grid=(nj, seq // tk),
        in_specs=[pl.BlockSpec((tk, kdim), lambda j, t: (t, 0)),
                  pl.BlockSpec((tk, col_block), lambda j, t: (t, j))],
        out_specs=pl.BlockSpec((None, kdim, col_block), lambda j, t: (j, 0, 0)),
        out_shape=jax.ShapeDtypeStruct((nj, kdim, col_block), F32),
        compiler_params=_cparams("parallel", "arbitrary"),
    )(a, b)


def exchange_with_sibling(grads):
    n = len(grads)

    def body(*refs):
        g_refs, out_refs = refs[:n], refs[n:2 * n]
        send_sems, recv_sems = refs[2 * n], refs[2 * n + 1]
        x, y, cc = _mesh_pos()
        copies = []
        for i in range(n):
            for j in range(N_CHIPS):
                k = i * N_CHIPS + j
                copies.append(pltpu.make_async_remote_copy(
                    src_ref=g_refs[i].at[j, 1 - cc], dst_ref=out_refs[i].at[j],
                    send_sem=send_sems.at[k], recv_sem=recv_sems.at[k],
                    device_id=(x, y, 1 - cc), device_id_type=MESH_ID))
        for cp in copies:
            cp.start()
        for cp in copies:
            cp.wait_recv()
        for cp in copies:
            cp.wait_send()

    return pl.pallas_call(
        body, name="grad_exchange_sibling",
        in_specs=[pl.BlockSpec(memory_space=pl.ANY)] * n,
        out_specs=[pl.BlockSpec(memory_space=pl.ANY)] * n,
        out_shape=[jax.ShapeDtypeStruct((N_CHIPS,) + g.shape[2:], g.dtype) for g in grads],
        scratch_shapes=[pltpu.SemaphoreType.DMA((n * N_CHIPS,)), pltpu.SemaphoreType.DMA((n * N_CHIPS,))],
    )(*grads)


def add_own_half(grad, recv, name):
    _, _, r, c = grad.shape
    tr = r // 2 if r % 32 == 0 else r

    def body(cc_ref, g_ref, r_ref, o_ref):
        o_ref[...] = (g_ref[...] + r_ref[...]).astype(BF16)

    grid_spec = pltpu.PrefetchScalarGridSpec(
        num_scalar_prefetch=1, grid=(N_CHIPS, r // tr),
        in_specs=[pl.BlockSpec((None, None, tr, c), lambda j, t, cc: (j, cc[0], t, 0)),
                  pl.BlockSpec((None, tr, c), lambda j, t, cc: (j, t, 0))],
        out_specs=pl.BlockSpec((None, tr, c), lambda j, t, cc: (j, t, 0)))
    cc = lax.axis_index("c").astype(jnp.int32).reshape(1)
    return pl.pallas_call(
        body, name=name, grid_spec=grid_spec,
        out_shape=jax.ShapeDtypeStruct((N_CHIPS, r, c), BF16),
        compiler_params=_cparams("parallel", "parallel"),
    )(cc, grad, recv)


def exchange_between_chips(sums):
    n = len(sums)

    def body(*refs):
        s_refs, out_refs = refs[:n], refs[n:2 * n]
        send_sems, recv_sems = refs[2 * n], refs[2 * n + 1]
        x, y, cc = _mesh_pos()
        my_chip = 2 * x + y
        chips = [(1 - x, y), (x, 1 - y), (1 - x, 1 - y)]
        copies = []
        for i in range(n):
            for j, (px, py) in enumerate(chips):
                k = i * 3 + j
                copies.append(pltpu.make_async_remote_copy(
                    src_ref=s_refs[i].at[2 * px + py], dst_ref=out_refs[i].at[my_chip],
                    send_sem=send_sems.at[k], recv_sem=recv_sems.at[k],
                    device_id=(px, py, cc), device_id_type=MESH_ID))
        for cp in copies:
            cp.start()
        for i in range(n):
            for j, (px, py) in enumerate(chips):
                k = i * 3 + j
                pltpu.make_async_remote_copy(
                    src_ref=s_refs[i].at[my_chip], dst_ref=out_refs[i].at[2 * px + py],
                    send_sem=send_sems.at[k], recv_sem=recv_sems.at[k],
                    device_id=(px, py, cc), device_id_type=MESH_ID).wait_recv()
        for cp in copies:
            cp.wait_send()

    return pl.pallas_call(
        body, name="grad_exchange_chips",
        in_specs=[pl.BlockSpec(memory_space=pl.ANY)] * n,
        out_specs=[pl.BlockSpec(memory_space=pl.ANY)] * n,
        out_shape=[jax.ShapeDtypeStruct(s.shape, s.dtype) for s in sums],
        scratch_shapes=[pltpu.SemaphoreType.DMA((n * 3,)), pltpu.SemaphoreType.DMA((n * 3,))],
    )(*sums)


def sum_chips(sums, parts, name):
    _, r, c = parts.shape
    tr = r // 2 if r % 32 == 0 else r

    def body(idx_ref, s_ref, p1_ref, p2_ref, p3_ref, o_ref):
        o_ref[...] = ((s_ref[...].astype(F32) + p1_ref[...].astype(F32))
                      + p2_ref[...].astype(F32)) + p3_ref[...].astype(F32)

    def pick(k):
        return pl.BlockSpec((None, tr, c), lambda t, idx: (idx[k], t, 0))

    x, y = lax.axis_index("x"), lax.axis_index("y")
    idx = jnp.stack([2 * x + y, 2 * (1 - x) + y, 2 * x + (1 - y), 2 * (1 - x) + (1 - y)]).astype(jnp.int32)
    grid_spec = pltpu.PrefetchScalarGridSpec(
        num_scalar_prefetch=1, grid=(r // tr,),
        in_specs=[pick(0), pick(1), pick(2), pick(3)],
        out_specs=pl.BlockSpec((tr, c), lambda t, idx: (t, 0)))
    return pl.pallas_call(
        body, name=name, grid_spec=grid_spec,
        out_shape=jax.ShapeDtypeStruct((r, c), F32),
        compiler_params=_cparams("parallel"),
    )(idx, sums, parts, parts, parts)


def share_with_sibling(halves):
    n = len(halves)

    def body(*refs):
        h_refs, out_refs = refs[:n], refs[n:2 * n]
        send_sems, recv_sems = refs[2 * n], refs[2 * n + 1]
        x, y, cc = _mesh_pos()
        copies = [pltpu.make_async_remote_copy(
            src_ref=h_refs[i], dst_ref=out_refs[i],
            send_sem=send_sems.at[i], recv_sem=recv_sems.at[i],
            device_id=(x, y, 1 - cc), device_id_type=MESH_ID) for i in range(n)]
        for cp in copies:
            cp.start()
        for cp in copies:
            cp.wait_recv()
        for cp in copies:
            cp.wait_send()

    return pl.pallas_call(
        body, name="grad_share_sibling",
        in_specs=[pl.BlockSpec(memory_space=pl.ANY)] * n,
        out_specs=[pl.BlockSpec(memory_space=pl.ANY)] * n,
        out_shape=[jax.ShapeDtypeStruct(h.shape, h.dtype) for h in halves],
        scratch_shapes=[pltpu.SemaphoreType.DMA((n,)), pltpu.SemaphoreType.DMA((n,))],
    )(*halves)


def _adam_update(w, g, m, v):
    m = ADAM_B1 * m + (1.0 - ADAM_B1) * g
    v = ADAM_B2 * v + (1.0 - ADAM_B2) * (g * g)
    m_hat = m / (1.0 - ADAM_B1 ** ADAM_STEP)
    v_hat = v / (1.0 - ADAM_B2 ** ADAM_STEP)
    delta = -ADAM_LR * (m_hat / (jnp.sqrt(v_hat) + ADAM_EPS) + ADAM_WD * w)
    return delta, m, v


def adamw(w, g_mine, g_sibling, m, v, name):
    r, c = w.shape
    half = r // 2
    tr = half // 2 if half % 16 == 0 else half
    nt = half // tr

    def body(cc_ref, w_ref, ga_ref, gb_ref, m_ref, v_ref, g_ref, d_ref, nm_ref, nv_ref):
        g = jnp.where(pl.program_id(0) == cc_ref[0], ga_ref[...], gb_ref[...])
        g_ref[...] = g
        d, nm, nv = _adam_update(w_ref[...], g, m_ref[...], v_ref[...])
        d_ref[...] = d
        nm_ref[...] = nm
        nv_ref[...] = nv

    full = pl.BlockSpec((tr, c), lambda h, t, cc: (h * nt + t, 0))
    part = pl.BlockSpec((tr, c), lambda h, t, cc: (t, 0))
    grid_spec = pltpu.PrefetchScalarGridSpec(
        num_scalar_prefetch=1, grid=(2, nt),
        in_specs=[full, part, part, full, full], out_specs=[full] * 4)
    cc = lax.axis_index("c").astype(jnp.int32).reshape(1)
    return pl.pallas_call(
        body, name=name, grid_spec=grid_spec,
        out_shape=[jax.ShapeDtypeStruct((r, c), F32)] * 4,
        compiler_params=_cparams("parallel", "parallel"),
    )(cc, w, g_mine, g_sibling, m, v)


def small_allreduce(pack):
    def body(p_ref, o_ref, gather, send_sems, recv_sems):
        x, y, cc = _mesh_pos()
        me = 4 * x + 2 * y + cc
        gather[me] = p_ref[...]
        flips = [(fx, fy, fc) for fx in (0, 1) for fy in (0, 1) for fc in (0, 1)][1:]
        copies = []
        for k, (fx, fy, fc) in enumerate(flips):
            copies.append(pltpu.make_async_remote_copy(
                src_ref=p_ref, dst_ref=gather.at[me],
                send_sem=send_sems.at[k], recv_sem=recv_sems.at[k],
                device_id=(x ^ fx, y ^ fy, cc ^ fc), device_id_type=MESH_ID))
        for cp in copies:
            cp.start()
        for k, (fx, fy, fc) in enumerate(flips):
            src = 4 * (x ^ fx) + 2 * (y ^ fy) + (cc ^ fc)
            pltpu.make_async_remote_copy(
                src_ref=p_ref, dst_ref=gather.at[src],
                send_sem=send_sems.at[k], recv_sem=recv_sems.at[k],
                device_id=(x ^ fx, y ^ fy, cc ^ fc), device_id_type=MESH_ID).wait_recv()
        for cp in copies:
            cp.wait_send()
        total = gather[0]
        for d in range(1, N_DEV):
            total = total + gather[d]
        o_ref[...] = total

    return pl.pallas_call(
        body, name="small_allreduce",
        in_specs=[pl.BlockSpec(memory_space=pltpu.VMEM)],
        out_specs=pl.BlockSpec(memory_space=pltpu.VMEM),
        out_shape=jax.ShapeDtypeStruct(pack.shape, pack.dtype),
        scratch_shapes=[pltpu.VMEM((N_DEV,) + pack.shape, pack.dtype),
                        pltpu.SemaphoreType.DMA((7,)), pltpu.SemaphoreType.DMA((7,))],
    )(pack)


def small_update(gsum, wpack, mpack, vpack):
    hw = HGRN_WIDTH

    def body(g_ref, w_ref, m_ref, v_ref, go_ref, d_ref, nm_ref, nv_ref, loss_ref):
        g = g_ref[...]
        wv = w_ref[...]
        a0, a1 = wv[4:5, :hw], wv[4:5, hw:]
        mx = jnp.maximum(a0, a1)
        e0, e1 = jnp.exp(a0 - mx), jnp.exp(a1 - mx)
        lb = e0 / (e0 + e1)
        dl = g[4:5, :hw] * lb * (1.0 - lb)
        row = lax.broadcasted_iota(jnp.int32, g.shape, 0)
        lb_row = jnp.concatenate([dl, -dl], axis=1)
        grads = jnp.where(row == 4, lb_row, jnp.where(row < 4, g, 0.0))
        go_ref[...] = grads
        d, nm, nv = _adam_update(wv, grads, m_ref[...], v_ref[...])
        d_ref[...] = d
        nm_ref[...] = nm
        nv_ref[...] = nv
        loss_ref[...] = jnp.zeros((8, LANES), F32) + jnp.sum(g[5:6, :])

    vm = pl.BlockSpec(memory_space=pltpu.VMEM)
    return pl.pallas_call(
        body, name="small_update",
        in_specs=[vm] * 4, out_specs=[vm] * 5,
        out_shape=[jax.ShapeDtypeStruct(gsum.shape, F32)] * 4 + [jax.ShapeDtypeStruct((8, LANES), F32)],
    )(gsum, wpack, mpack, vpack)


def _pack_small(n1, n2, fn, hn, lbl):
    z = jnp.zeros((1, D_MODEL - HGRN_WIDTH), F32)
    rows = [n1.reshape(1, D_MODEL), n2.reshape(1, D_MODEL), fn.reshape(1, D_MODEL),
            jnp.concatenate([hn.reshape(1, HGRN_WIDTH), z], axis=1), lbl.reshape(1, 2 * HGRN_WIDTH),
            jnp.zeros((3, D_MODEL), F32)]
    return jnp.concatenate(rows, axis=0)


def _unpack_small(pack):
    return (pack[0:1], pack[4].reshape(2, HGRN_WIDTH), pack[3:4, :HGRN_WIDTH], pack[1:2], pack[2])


def kernel(x, norm1_w, w_in, lb_logits, hgrn_norm_w, w_out, norm2_w, w_gate_up, w_down, final_norm_w, loss_target, m_norm1_w, m_w_in, m_lb_logits, m_hgrn_norm_w, m_w_out, m_norm2_w, m_w_gate_up, m_w_down, m_final_norm_w, v_norm1_w, v_w_in, v_lb_logits, v_hgrn_norm_w, v_w_out, v_norm2_w, v_w_gate_up, v_w_down, v_final_norm_w):
    seq = x.shape[1]
    xs = x.reshape(seq, D_MODEL)
    target = loss_target.reshape(seq, D_MODEL)
    shards = {"w_in": w_in[0], "w_out": w_out[0], "w_gu": w_gate_up[0], "w_down": w_down[0]}

    gathered = {k: allgather_halves(cast_bf16(w, "cast_" + k), "gather_" + k) for k, w in shards.items()}
    w_in4 = gathered["w_in"].reshape(N_CHIPS, D_MODEL, -1)
    w_out_f = gathered["w_out"].reshape(D_MODEL, D_MODEL)
    w_gu4 = gathered["w_gu"].reshape(N_CHIPS, D_MODEL, -1)
    w_down_f = gathered["w_down"].reshape(FFN_HIDDEN, D_MODEL)

    cos_t, sin_t = _rope_tables(seq)
    fw = final_norm_w.reshape(1, D_MODEL)

    qr, kr, va, hg, u = in_proj(xs, norm1_w, w_in4, cos_t, sin_t)
    fwd = [attn_fwd(qr, kr, va, d, "attn_fwd_d%d" % d) for _, d in DILATED_PAIRS]
    yb, o_pre, st0 = hgrn_fwd(hg, lb_logits, hgrn_norm_w)
    ya, lse, mixed, h1, u2 = mix_out([f[0] for f in fwd], [f[1] for f in fwd], yb, xs, w_out_f, norm2_w)
    g, up, act = gate_up(u2, w_gu4)
    dh2, acc_fin = down_loss(act, w_down_f, h1, fw, target)

    dgu = down_bwd(dh2, w_down_f, g, up)
    dh1, dya, dyb, delta, acc_n2 = gu_bwd(dgu, w_gu4, h1, norm2_w, dh2, w_out_f, ya)
    bwd = [attn_bwd(qr, kr, va, dya, lse, delta, d, "attn_bwd_d%d" % d) for _, d in DILATED_PAIRS]
    dhg, acc_hg = hgrn_bwd(hg, lb_logits, hgrn_norm_w, o_pre, st0, dyb)
    dproj, dx, acc_n1 = in_bwd([b[0] for b in bwd], [b[1] for b in bwd], [b[2] for b in bwd],
                               dhg, cos_t, sin_t, w_in4, xs, norm1_w, dh1)

    cw_in, cw_gu = w_in4.shape[2], w_gu4.shape[2]
    grads = [
        weight_grad(u, dproj, cw_in, "wgrad_in").reshape(N_CHIPS, 2, D_MODEL // 2, cw_in),
        weight_grad(mixed, dh1, D_MODEL, "wgrad_out").reshape(N_CHIPS, 2, D_MODEL // 8, D_MODEL),
        weight_grad(u2, dgu, cw_gu, "wgrad_gu").reshape(N_CHIPS, 2, D_MODEL // 2, cw_gu),
        weight_grad(act, dh2, D_MODEL, "wgrad_down").reshape(N_CHIPS, 2, FFN_HIDDEN // 8, D_MODEL),
    ]
    names = ["in", "out", "gu", "down"]
    recv = exchange_with_sibling(grads)
    sums = [add_own_half(gr, rc, "add_half_" + nm) for gr, rc, nm in zip(grads, recv, names)]
    parts = exchange_between_chips(sums)
    halves = [sum_chips(s, p, "sum_chips_" + nm) for s, p, nm in zip(sums, parts, names)]
    others = share_with_sibling(halves)
    big = {}
    for nm, key, mine, other, m_, v_ in zip(names, ["w_in", "w_out", "w_gu", "w_down"], halves, others,
                                            [m_w_in, m_w_out, m_w_gate_up, m_w_down],
                                            [v_w_in, v_w_out, v_w_gate_up, v_w_down]):
        big[key] = tuple(t[None] for t in adamw(shards[key], mine, other, m_[0], v_[0], "adamw_" + nm))

    z512 = jnp.zeros((1, D_MODEL - HGRN_WIDTH), F32)
    gpack = jnp.concatenate([
        acc_n1[0:1], acc_n2[0:1], acc_fin[0:1],
        jnp.concatenate([acc_hg[0:1], z512], axis=1), jnp.concatenate([acc_hg[1:2], z512], axis=1),
        acc_fin[1:2], jnp.zeros((2, D_MODEL), F32)], axis=0)
    gsum = small_allreduce(gpack)
    wpack = _pack_small(norm1_w, norm2_w, final_norm_w, hgrn_norm_w, lb_logits)
    mpack = _pack_small(m_norm1_w, m_norm2_w, m_final_norm_w, m_hgrn_norm_w, m_lb_logits)
    vpack = _pack_small(v_norm1_w, v_norm2_w, v_final_norm_w, v_hgrn_norm_w, v_lb_logits)
    gs, ds, nms, nvs, loss8 = small_update(gsum, wpack, mpack, vpack)
    loss = loss8[0, 0]

    def assemble(small_pack, idx):
        n1, lbl, hn, n2, fn = _unpack_small(small_pack)
        return (n1, big["w_in"][idx], lbl, hn, big["w_out"][idx], n2, big["w_gu"][idx], big["w_down"][idx], fn)

    return (loss, dx.reshape(x.shape), *assemble(gs, 0), *assemble(ds, 1), *assemble(nms, 2), *assemble(nvs, 3))
```

```python
import functools

import jax
import jax.numpy as jnp
from jax import lax
from jax.experimental import pallas as pl
from jax.experimental.pallas import tpu as pltpu

F32 = jnp.float32
BF16 = jnp.bfloat16

D_MODEL = 1024
ATTN_WIDTH = 512
HEAD_DIM = 64
DILATED_PAIRS = ((128, 1), (512, 4), (2048, 16))
ATTN_BLOCK = 128
ROPE_THETA = 10000.0
HGRN_WIDTH = 512
HGRN_CHUNK = 16
HGRN_HEADS = 4
IN_PROJ_WIDTH = 3584
FFN_HIDDEN = 2816
NORM_EPS = 1e-6
ATTN_SCALE = HEAD_DIM ** -0.5
N_CHIPS = 4
N_DEV = 8

ADAM_LR = 0.001
ADAM_B1 = 0.9
ADAM_B2 = 0.999
ADAM_EPS = 1e-08
ADAM_WD = 0.01
ADAM_STEP = 10

LANES = 128
HGRN_ROWS = 128
ROW_TILE = 256
ATTN_STEP_ROWS = 2048
ATTN_FWD_UNROLL = 8
ATTN_BWD_UNROLL = 1
VMEM_LIMIT = 56 * 1024 * 1024
NEG_BIG = -1e30
MESH_ID = pl.DeviceIdType.MESH


def _cparams(*sem):
    return pltpu.CompilerParams(dimension_semantics=tuple(sem), vmem_limit_bytes=VMEM_LIMIT)


def _dot(a, b):
    return jnp.dot(a, b, preferred_element_type=F32)


def _dot_nt(a, b):
    return lax.dot_general(a, b, (((1,), (1,)), ((), ())), preferred_element_type=F32)


def _dot_tn(a, b):
    return lax.dot_general(a, b, (((0,), (0,)), ((), ())), preferred_element_type=F32)


def _sigmoid(x):
    return 1.0 / (1.0 + jnp.exp(-x))


def _full(shape):
    n = len(shape)
    return pl.BlockSpec(shape, lambda *_: (0,) * n)


def _rows(tm, width):
    return pl.BlockSpec((tm, width), lambda i: (i, 0))


def _swap32(x):
    lane = lax.broadcasted_iota(jnp.int32, x.shape, 1)
    first = (lane % HEAD_DIM) < (HEAD_DIM // 2)
    return jnp.where(first, pltpu.roll(x, LANES - 32, axis=1), pltpu.roll(x, 32, axis=1))


def _rotary_fwd(x, cos, sin_signed):
    parts = []
    for j in range(x.shape[1] // LANES):
        xc = x[:, j * LANES:(j + 1) * LANES]
        parts.append(xc * cos + _swap32(xc) * sin_signed)
    return jnp.concatenate(parts, axis=1)


def _rotary_bwd(dy, cos, sin_signed):
    parts = []
    for j in range(dy.shape[1] // LANES):
        dc = dy[:, j * LANES:(j + 1) * LANES]
        parts.append(dc * cos + _swap32(dc * sin_signed))
    return jnp.concatenate(parts, axis=1)


def _rope_tables(seq):
    half = HEAD_DIM // 2
    inv_freq = ROPE_THETA ** (-jnp.arange(half, dtype=F32) / half)
    ang = jnp.arange(seq, dtype=F32)[:, None] * inv_freq[None, :]
    cos, sin = jnp.cos(ang), jnp.sin(ang)
    cos_t = jnp.tile(cos, (1, LANES // half))
    sin_t = jnp.tile(jnp.concatenate([-sin, sin], axis=1), (1, LANES // HEAD_DIM))
    return cos_t, sin_t


def cast_bf16(w, name):
    r, c = w.shape
    half = r // 2

    def body(w_ref, o_ref):
        o_ref[...] = w_ref[...].astype(BF16)

    return pl.pallas_call(
        body, name=name, grid=(2,),
        in_specs=[pl.BlockSpec((half, c), lambda i: (i, 0))],
        out_specs=pl.BlockSpec((None, half, c), lambda i: (i, 0, 0)),
        out_shape=jax.ShapeDtypeStruct((2, half, c), BF16),
        compiler_params=_cparams("parallel"),
    )(w)


def _mesh_pos():
    return lax.axis_index("x"), lax.axis_index("y"), lax.axis_index("c")


def allgather_halves(halves, name):
    _, r, c = halves.shape

    def body(x_ref, out_ref, send_sems, recv_sems, local_sem):
        x, y, cc = _mesh_pos()
        me, sibling = (x, y, cc), (x, y, 1 - cc)
        chips = [(1 - x, y), (x, 1 - y), (1 - x, 1 - y)]
        mine_src = x_ref.at[cc]

        def rows(px, py, pc):
            return out_ref.at[4 * px + 2 * py + pc]

        def copy(k, block, to, src=None):
            return pltpu.make_async_remote_copy(
                src_ref=rows(*block) if src is None else src, dst_ref=rows(*block),
                send_sem=send_sems.at[k], recv_sem=recv_sems.at[k],
                device_id=to, device_id_type=MESH_ID)

        mine = pltpu.make_async_copy(mine_src, rows(*me), local_sem)
        mine.start()
        first = [copy(0, me, sibling, src=mine_src)]
        first += [copy(1 + j, me, (*chip, cc), src=mine_src) for j, chip in enumerate(chips)]
        for cp in first:
            cp.start()
        passed = [copy(4 + j, (*chip, cc), sibling) for j, chip in enumerate(chips)]
        for j, chip in enumerate(chips):
            copy(1 + j, (*chip, cc), me).wait_recv()
            passed[j].start()
        copy(0, sibling, me).wait_recv()
        for j, chip in enumerate(chips):
            copy(4 + j, (*chip, 1 - cc), me).wait_recv()
        for cp in first + passed:
            cp.wait_send()
        mine.wait()

    return pl.pallas_call(
        body, name=name,
        in_specs=[pl.BlockSpec(memory_space=pl.ANY)],
        out_specs=pl.BlockSpec(memory_space=pl.ANY),
        out_shape=jax.ShapeDtypeStruct((N_DEV, r, c), halves.dtype),
        scratch_shapes=[pltpu.SemaphoreType.DMA((7,)), pltpu.SemaphoreType.DMA((7,)),
                        pltpu.SemaphoreType.DMA],
    )(halves)


def _rms(x):
    return lax.rsqrt(jnp.mean(x * x, axis=-1, keepdims=True) + NORM_EPS)


def in_proj(x, norm1_w, w_in4, cos_t, sin_t):
    seq = x.shape[0]
    tm = ROW_TILE
    cw = w_in4.shape[2]

    def body(x_ref, nw_ref, w_ref, cos_ref, sin_ref, q_ref, k_ref, v_ref, hg_ref, u_ref):
        xv = x_ref[...]
        u = ((xv * _rms(xv)) * nw_ref[...]).astype(BF16)
        u_ref[...] = u
        proj = jnp.concatenate([_dot(u, w_ref[j]) for j in range(N_CHIPS)], axis=1)
        cos, sin = cos_ref[...], sin_ref[...]
        a = ATTN_WIDTH
        q_ref[...] = _rotary_fwd(proj[:, :a], cos, sin)
        k_ref[...] = _rotary_fwd(proj[:, a:2 * a], cos, sin)
        v_ref[...] = proj[:, 2 * a:3 * a]
        hg_ref[...] = proj[:, 3 * a:]

    return pl.pallas_call(
        body, name="in_proj", grid=(seq // tm,),
        in_specs=[_rows(tm, D_MODEL), _full((1, D_MODEL)), _full((N_CHIPS, D_MODEL, cw)),
                  _rows(tm, LANES), _rows(tm, LANES)],
        out_specs=[_rows(tm, ATTN_WIDTH)] * 3 + [_rows(tm, 4 * HGRN_WIDTH), _rows(tm, D_MODEL)],
        out_shape=[jax.ShapeDtypeStruct((seq, ATTN_WIDTH), F32)] * 3
        + [jax.ShapeDtypeStruct((seq, 4 * HGRN_WIDTH), F32), jax.ShapeDtypeStruct((seq, D_MODEL), BF16)],
        compiler_params=_cparams("parallel"),
    )(x, norm1_w, w_in4, cos_t, sin_t)


def _head_masks():
    lane = lax.broadcasted_iota(jnp.int32, (1, LANES), 1)
    return [(lane // HEAD_DIM) == h for h in range(LANES // HEAD_DIM)]


def _window_valid(no_prev):
    qi = lax.broadcasted_iota(jnp.int32, (ATTN_BLOCK, 2 * ATTN_BLOCK), 0)
    kj = lax.broadcasted_iota(jnp.int32, (ATTN_BLOCK, 2 * ATTN_BLOCK), 1)
    valid = (kj >= qi) & (kj <= qi + ATTN_BLOCK)
    return valid & (jnp.logical_not(no_prev) | (kj >= ATTN_BLOCK))


def _strided_rows(start, dilation):
    if dilation == 1:
        return pl.ds(start, ATTN_BLOCK)
    return pl.ds(start, ATTN_BLOCK, stride=dilation)


def _block_before(edge_ref, cur_ref, t, r, span, dilation, per_step):
    edge = edge_ref[_strided_rows(r, dilation), :]
    if per_step == 1:
        return edge
    inside = cur_ref[_strided_rows(r + span * jnp.maximum(t - 1, 0), dilation), :]
    return jnp.where(t == 0, edge, inside)


def _attn_specs(seq, dilation):
    span = ATTN_BLOCK * dilation
    per_step = ATTN_STEP_ROWS // span
    cur = pl.BlockSpec((ATTN_STEP_ROWS, LANES), lambda hp, j: (j, hp))
    prev = pl.BlockSpec((span, LANES), lambda hp, j: (jnp.maximum(j * per_step - 1, 0), hp))
    return span, per_step, cur, prev


def attn_fwd(q, k, v, dilation, name):
    seq = q.shape[0]
    span, per_step, cur, prev = _attn_specs(seq, dilation)

    def body(q_ref, kc_ref, vc_ref, kp_ref, vp_ref, o_ref, lse_ref):
        first_step = pl.program_id(1) == 0
        masks = _head_masks()

        def block(it, carry):
            t, r = it // dilation, it % dilation
            rows = _strided_rows(r + span * t, dilation)
            at_edge = t == 0
            q2 = q_ref[rows, :].astype(BF16)
            kp = _block_before(kp_ref, kc_ref, t, r, span, dilation, per_step)
            vp = _block_before(vp_ref, vc_ref, t, r, span, dilation, per_step)
            k2 = jnp.concatenate([kp, kc_ref[rows, :]], axis=0).astype(BF16)
            v2 = jnp.concatenate([vp, vc_ref[rows, :]], axis=0).astype(BF16)
            valid = _window_valid(first_step & at_edge)
            o_acc = jnp.zeros((ATTN_BLOCK, LANES), F32)
            l_acc = jnp.zeros((ATTN_BLOCK, LANES), F32)
            for mh in masks:
                qm = jnp.where(mh, q2, jnp.zeros_like(q2))
                s = jnp.where(valid, _dot_nt(qm, k2) * ATTN_SCALE, NEG_BIG)
                m = jnp.max(s, axis=-1, keepdims=True)
                p = jnp.exp(s - m)
                l = jnp.sum(p, axis=-1, keepdims=True)
                o = _dot(p.astype(BF16), v2) / l
                o_acc = jnp.where(mh, o, o_acc)
                l_acc = jnp.where(mh, m + jnp.log(l), l_acc)
            o_ref[rows, :] = o_acc
            lse_ref[rows, :] = l_acc
            return carry

        lax.fori_loop(0, per_step * dilation, block, 0, unroll=ATTN_FWD_UNROLL)

    return pl.pallas_call(
        body, name=name, grid=(ATTN_WIDTH // LANES, seq // ATTN_STEP_ROWS),
        in_specs=[cur, cur, cur, prev, prev],
        out_specs=[cur, cur],
        out_shape=[jax.ShapeDtypeStruct((seq, ATTN_WIDTH), F32)] * 2,
        compiler_params=_cparams("parallel", "parallel"),
    )(q, k, v, k, v)


def _chunk_cumsum(x, reverse=False):
    rc = lax.broadcasted_iota(jnp.int32, x.shape, 0) % HGRN_CHUNK
    sh = 1
    while sh < HGRN_CHUNK:
        if reverse:
            x = x + jnp.where(rc + sh < HGRN_CHUNK, pltpu.roll(x, x.shape[0] - sh, axis=0), 0.0)
        else:
            x = x + jnp.where(rc >= sh, pltpu.roll(x, sh, axis=0), 0.0)
        sh *= 2
    return x


def _chunk_row(x, row):
    return _chunk_rows([x[n * HGRN_CHUNK + row:n * HGRN_CHUNK + row + 1, :]
                        for n in range(x.shape[0] // HGRN_CHUNK)])


def _chunk_rows(rows):
    return jnp.concatenate([jnp.broadcast_to(r, (HGRN_CHUNK, r.shape[1])) for r in rows], axis=0)


def _hgrn_prep(hg, lbl):
    w = HGRN_WIDTH
    a0, a1 = lbl[0:1, :], lbl[1:2, :]
    mx = jnp.maximum(a0, a1)
    e0, e1 = jnp.exp(a0 - mx), jnp.exp(a1 - mx)
    lb = e0 / (e0 + e1)
    qb, fb, gb = hg[:, :w], hg[:, w:2 * w], hg[:, 3 * w:]
    sg = _sigmoid(fb)
    f = lb + (1.0 - lb) * sg
    b = _chunk_cumsum(jnp.log(f))
    bmid, btot = _chunk_row(b, HGRN_CHUNK // 2 - 1), _chunk_row(b, HGRN_CHUNK - 1)
    sq = _sigmoid(qb)
    p = dict(lb=lb, sg=sg, f=f, kk=1.0 - f, sq=sq, qf=qb * sq, gb=gb,
             e_iq=jnp.exp(b - bmid), e_ik=jnp.exp(bmid - b), e_b=jnp.exp(b),
             e_bb=jnp.exp(btot - b), e_tot=jnp.exp(btot))
    p["qi"] = p["qf"] * p["e_iq"]
    p["ki"] = p["kk"] * p["e_ik"]
    p["qs"] = p["qf"] * p["e_b"]
    p["kb"] = p["kk"] * p["e_bb"]
    return p


def _chunk_masks():
    t = lax.broadcasted_iota(jnp.int32, (HGRN_ROWS, HGRN_ROWS), 0)
    s = lax.broadcasted_iota(jnp.int32, (HGRN_ROWS, HGRN_ROWS), 1)
    tril = ((t // HGRN_CHUNK) == (s // HGRN_CHUNK)) & (s <= t)
    n_chunks = HGRN_ROWS // HGRN_CHUNK
    tt = lax.broadcasted_iota(jnp.int32, (HGRN_ROWS, n_chunks * LANES), 0)
    cc = lax.broadcasted_iota(jnp.int32, (HGRN_ROWS, n_chunks * LANES), 1)
    block = (tt // HGRN_CHUNK) == (cc // LANES)
    return tril, block


def _spread(x, block):
    n_chunks = HGRN_ROWS // HGRN_CHUNK
    return jnp.where(block, jnp.tile(x, (1, n_chunks)), jnp.zeros((), x.dtype))


def _fold(x_full, block):
    n_chunks = HGRN_ROWS // HGRN_CHUNK
    z = jnp.where(block, x_full, 0.0)
    acc = z[:, :LANES]
    for n in range(1, n_chunks):
        acc = acc + z[:, n * LANES:(n + 1) * LANES]
    return acc


def hgrn_fwd(hg, lb_logits, hnw):
    seq = hg.shape[0]
    nblk = seq // HGRN_ROWS
    n_chunks = HGRN_ROWS // HGRN_CHUNK

    def body(hg_ref, lbl_ref, hnw_ref, yb_ref, o_ref, st0_ref, st_scr):
        @pl.when(pl.program_id(0) == 0)
        def _():
            st_scr[...] = jnp.zeros_like(st_scr)

        hg_v = hg_ref[...]
        p = _hgrn_prep(hg_v, lbl_ref[...])
        tril, block = _chunk_masks()
        vv = hg_v[:, 2 * HGRN_WIDTH:3 * HGRN_WIDTH].astype(BF16)
        outs = []
        for h in range(HGRN_HEADS):
            sl = slice(h * LANES, (h + 1) * LANES)
            v_h = vv[:, sl]
            a = jnp.where(tril, _dot_nt(p["qi"][:, sl].astype(BF16), p["ki"][:, sl].astype(BF16)), 0.0)
            o = _dot(a.astype(BF16), v_h)
            upd = _dot_tn(v_h, _spread(p["kb"][:, sl].astype(BF16), block))
            st = st_scr[h]
            st0_ref[h] = st
            parts = []
            for n in range(n_chunks):
                parts.append(st.astype(BF16))
                decay = p["e_tot"][n * HGRN_CHUNK:n * HGRN_CHUNK + 1, sl]
                st = st * decay + upd[:, n * LANES:(n + 1) * LANES]
            st_scr[h] = st
            o = o + _dot_nt(_spread(p["qs"][:, sl].astype(BF16), block), jnp.concatenate(parts, axis=1))
            outs.append(o)
        o_all = jnp.concatenate(outs, axis=1)
        o_ref[...] = o_all
        normed = jnp.concatenate(
            [outs[h] * _rms(outs[h]) for h in range(HGRN_HEADS)], axis=1)
        gb = p["gb"]
        yb_ref[...] = (normed * hnw_ref[...]) * (gb * _sigmoid(gb))

    return pl.pallas_call(
        body, name="hgrn_fwd", grid=(nblk,),
        in_specs=[_rows(HGRN_ROWS, 4 * HGRN_WIDTH), _full((2, HGRN_WIDTH)), _full((1, HGRN_WIDTH))],
        out_specs=[_rows(HGRN_ROWS, HGRN_WIDTH), _rows(HGRN_ROWS, HGRN_WIDTH),
                   pl.BlockSpec((None, HGRN_HEADS, LANES, LANES), lambda i: (i, 0, 0, 0))],
        out_shape=[jax.ShapeDtypeStruct((seq, HGRN_WIDTH), F32)] * 2
        + [jax.ShapeDtypeStruct((nblk, HGRN_HEADS, LANES, LANES), F32)],
        scratch_shapes=[pltpu.VMEM((HGRN_HEADS, LANES, LANES), F32)],
        compiler_params=_cparams("arbitrary"),
    )(hg, lb_logits, hnw)


def mix_out(outs, lses, yb, x, w_out, norm2_w):
    seq = x.shape[0]
    tm = ROW_TILE

    def body(o1, o2, o3, l1, l2, l3, yb_ref, x_ref, w_ref, nw_ref,
             ya_ref, lse_ref, mixed_ref, h1_ref, u2_ref):
        l1v, l2v, l3v = l1[...], l2[...], l3[...]
        mx = jnp.maximum(jnp.maximum(l1v, l2v), l3v)
        e1, e2, e3 = jnp.exp(l1v - mx), jnp.exp(l2v - mx), jnp.exp(l3v - mx)
        den = e1 + e2 + e3
        ya = (e1 * o1[...] + e2 * o2[...] + e3 * o3[...]) / den
        ya_ref[...] = ya
        lse_ref[...] = mx + jnp.log(den)
        mixed = jnp.concatenate([ya, yb_ref[...]], axis=1).astype(BF16)
        mixed_ref[...] = mixed
        h1 = x_ref[...] + _dot(mixed, w_ref[...])
        h1_ref[...] = h1
        u2_ref[...] = ((h1 * _rms(h1)) * nw_ref[...]).astype(BF16)

    half = _rows(tm, ATTN_WIDTH)
    wide = _rows(tm, D_MODEL)
    return pl.pallas_call(
        body, name="mix_out", grid=(seq // tm,),
        in_specs=[half] * 7 + [wide, _full((D_MODEL, D_MODEL)), _full((1, D_MODEL))],
        out_specs=[half, half, wide, wide, wide],
        out_shape=[jax.ShapeDtypeStruct((seq, ATTN_WIDTH), F32)] * 2
        + [jax.ShapeDtypeStruct((seq, D_MODEL), BF16), jax.ShapeDtypeStruct((seq, D_MODEL), F32),
           jax.ShapeDtypeStruct((seq, D_MODEL), BF16)],
        compiler_params=_cparams("parallel"),
    )(*outs, *lses, yb, x, w_out, norm2_w)


def gate_up(u2, w_gu4):
    seq = u2.shape[0]
    tm = ROW_TILE
    cw = w_gu4.shape[2]

    def body(u_ref, w_ref, g_ref, up_ref, act_ref):
        u = u_ref[...]
        g = jnp.concatenate([_dot(u, w_ref[0]), _dot(u, w_ref[1])], axis=1)
        up = jnp.concatenate([_dot(u, w_ref[2]), _dot(u, w_ref[3])], axis=1)
        g_ref[...] = g.astype(BF16)
        up_ref[...] = up.astype(BF16)
        act_ref[...] = ((g * _sigmoid(g)) * up).astype(BF16)

    ffn = _rows(tm, FFN_HIDDEN)
    return pl.pallas_call(
        body, name="gate_up", grid=(seq // tm,),
        in_specs=[_rows(tm, D_MODEL), _full((N_CHIPS, D_MODEL, cw))],
        out_specs=[ffn] * 3,
        out_shape=[jax.ShapeDtypeStruct((seq, FFN_HIDDEN), BF16)] * 3,
        compiler_params=_cparams("parallel"),
    )(u2, w_gu4)


def down_loss(act, w_down, h1, final_w, target):
    seq = h1.shape[0]
    tm = ROW_TILE
    inv_d = 1.0 / D_MODEL

    def body(act_ref, w_ref, h1_ref, fw_ref, t_ref, dh2_ref, acc_ref):
        @pl.when(pl.program_id(0) == 0)
        def _():
            acc_ref[...] = jnp.zeros_like(acc_ref)

        h2 = h1_ref[...] + _dot(act_ref[...], w_ref[...])
        rf = _rms(h2)
        n = h2 * rf
        fw = fw_ref[...]
        err = n * fw - t_ref[...]
        dy = err * inv_d
        acc_ref[0:1, :] += jnp.sum(dy * n, axis=0, keepdims=True)
        acc_ref[1:2, :] += (0.5 * inv_d) * jnp.sum(err * err, axis=0, keepdims=True)
        dn = dy * fw
        dh2_ref[...] = rf * (dn - n * jnp.mean(dn * n, axis=-1, keepdims=True))

    wide = _rows(tm, D_MODEL)
    return pl.pallas_call(
        body, name="down_loss", grid=(seq // tm,),
        in_specs=[_rows(tm, FFN_HIDDEN), _full((FFN_HIDDEN, D_MODEL)), wide, _full((1, D_MODEL)), wide],
        out_specs=[wide, _full((8, D_MODEL))],
        out_shape=[jax.ShapeDtypeStruct((seq, D_MODEL), F32), jax.ShapeDtypeStruct((8, D_MODEL), F32)],
        compiler_params=_cparams("arbitrary"),
    )(act, w_down, h1, final_w, target)


def down_bwd(dh2, w_down, g, up):
    seq = dh2.shape[0]
    tm = ROW_TILE

    def body(dh_ref, w_ref, g_ref, up_ref, dgu_ref):
        dact = _dot_nt(dh_ref[...].astype(BF16), w_ref[...])
        gv = g_ref[...].astype(F32)
        sg = _sigmoid(gv)
        dgu_ref[:, :FFN_HIDDEN] = (dact * up_ref[...].astype(F32) * (sg * (1.0 + gv * (1.0 - sg)))).astype(BF16)
        dgu_ref[:, FFN_HIDDEN:] = (dact * (gv * sg)).astype(BF16)

    ffn = _rows(tm, FFN_HIDDEN)
    return pl.pallas_call(
        body, name="down_bwd", grid=(seq // tm,),
        in_specs=[_rows(tm, D_MODEL), _full((FFN_HIDDEN, D_MODEL)), ffn, ffn],
        out_specs=_rows(tm, 2 * FFN_HIDDEN),
        out_shape=jax.ShapeDtypeStruct((seq, 2 * FFN_HIDDEN), BF16),
        compiler_params=_cparams("parallel"),
    )(dh2, w_down, g, up)


def _head_sum_matrix():
    i = jnp.arange(ATTN_WIDTH)
    return ((i[:, None] // HEAD_DIM) == (i[None, :] // HEAD_DIM)).astype(BF16)


def gu_bwd(dgu, w_gu4, h1, norm2_w, dh2, w_out, ya):
    seq = h1.shape[0]
    tm = ROW_TILE
    cw = w_gu4.shape[2]
    hsum = _head_sum_matrix()

    def body(dgu_ref, w_ref, h1_ref, nw_ref, dh2_ref, wo_ref, ya_ref, hs_ref,
             dh1_ref, dya_ref, dyb_ref, delta_ref, acc_ref):
        @pl.when(pl.program_id(0) == 0)
        def _():
            acc_ref[...] = jnp.zeros_like(acc_ref)

        du2 = _dot_nt(dgu_ref[:, :cw], w_ref[0])
        for j in range(1, N_CHIPS):
            du2 = du2 + _dot_nt(dgu_ref[:, j * cw:(j + 1) * cw], w_ref[j])
        h1 = h1_ref[...]
        r2 = _rms(h1)
        nh = h1 * r2
        acc_ref[0:1, :] += jnp.sum(du2 * nh, axis=0, keepdims=True)
        dn = du2 * nw_ref[...]
        dh1 = dh2_ref[...] + r2 * (dn - nh * jnp.mean(dn * nh, axis=-1, keepdims=True))
        dh1_ref[...] = dh1
        dmixed = _dot_nt(dh1.astype(BF16), wo_ref[...])
        dya = dmixed[:, :ATTN_WIDTH]
        dya_ref[...] = dya
        dyb_ref[...] = dmixed[:, ATTN_WIDTH:]
        prod = dya * ya_ref[...]
        hi = prod.astype(BF16)
        lo = (prod - hi.astype(F32)).astype(BF16)
        delta_ref[...] = _dot(hi, hs_ref[...]) + _dot(lo, hs_ref[...])

    wide = _rows(tm, D_MODEL)
    half = _rows(tm, ATTN_WIDTH)
    return pl.pallas_call(
        body, name="gu_bwd", grid=(seq // tm,),
        in_specs=[_rows(tm, 2 * FFN_HIDDEN), _full((N_CHIPS, D_MODEL, cw)), wide, _full((1, D_MODEL)), wide,
                  _full((D_MODEL, D_MODEL)), half, _full((ATTN_WIDTH, ATTN_WIDTH))],
        out_specs=[wide, half, half, half, _full((8, D_MODEL))],
        out_shape=[jax.ShapeDtypeStruct((seq, D_MODEL), F32), jax.ShapeDtypeStruct((seq, ATTN_WIDTH), F32),
                   jax.ShapeDtypeStruct((seq, ATTN_WIDTH), F32), jax.ShapeDtypeStruct((seq, ATTN_WIDTH), F32),
                   jax.ShapeDtypeStruct((8, D_MODEL), F32)],
        compiler_params=_cparams("arbitrary"),
    )(dgu, w_gu4, h1, norm2_w, dh2, w_out, ya, hsum)


def attn_bwd(q, k, v, dy, lse, delta, dilation, name):
    seq = q.shape[0]
    span, per_step, cur, prev = _attn_specs(seq, dilation)
    whole = pl.BlockSpec((seq, LANES), lambda hp, j: (0, hp))

    def body(q_ref, dy_ref, lse_ref, dl_ref, kc_ref, vc_ref, kp_ref, vp_ref, dq_ref, dk_ref, dv_ref):
        first_step = pl.program_id(1) == 0
        base = pl.program_id(1) * ATTN_STEP_ROWS
        masks = _head_masks()

        def block(it, carry):
            t, r = it // dilation, it % dilation
            rows = _strided_rows(r + span * t, dilation)
            at_edge = t == 0
            q2, dy2 = q_ref[rows, :].astype(BF16), dy_ref[rows, :].astype(BF16)
            lse2, dl2 = lse_ref[rows, :], dl_ref[rows, :]
            kp = _block_before(kp_ref, kc_ref, t, r, span, dilation, per_step)
            vp = _block_before(vp_ref, vc_ref, t, r, span, dilation, per_step)
            k2 = jnp.concatenate([kp, kc_ref[rows, :]], axis=0).astype(BF16)
            v2 = jnp.concatenate([vp, vc_ref[rows, :]], axis=0).astype(BF16)
            valid = _window_valid(first_step & at_edge)
            zero = jnp.zeros_like(q2)
            qms, dyms, ps, dss, kms = [], [], [], [], []
            for h, mh in enumerate(masks):
                c0 = h * HEAD_DIM
                qm, dym = jnp.where(mh, q2, zero), jnp.where(mh, dy2, zero)
                s = _dot_nt(qm, k2) * ATTN_SCALE
                p = jnp.where(valid, jnp.exp(s - lse2[:, c0:c0 + 1]), 0.0)
                dp = _dot_nt(dym, v2)
                dss.append((p * (dp - dl2[:, c0:c0 + 1]) * ATTN_SCALE).astype(BF16))
                ps.append(p.astype(BF16))
                qms.append(qm)
                dyms.append(dym)
                kms.append(jnp.where(mh, k2, jnp.zeros_like(k2)))
            dq_ref[rows, :] = _dot(jnp.concatenate(dss, axis=1), jnp.concatenate(kms, axis=0))
            dv_full = _dot_tn(jnp.concatenate(ps, axis=0), jnp.concatenate(dyms, axis=0))
            dk_full = _dot_tn(jnp.concatenate(dss, axis=0), jnp.concatenate(qms, axis=0))
            here = _strided_rows(base + r + span * t, dilation)
            dk_ref[here, :] = dk_full[ATTN_BLOCK:]
            dv_ref[here, :] = dv_full[ATTN_BLOCK:]

            @pl.when(jnp.logical_not(first_step & at_edge))
            def _():
                back = _strided_rows(base + r + span * t - span, dilation)
                dk_ref[back, :] += dk_full[:ATTN_BLOCK]
                dv_ref[back, :] += dv_full[:ATTN_BLOCK]

            return carry

        lax.fori_loop(0, per_step * dilation, block, 0, unroll=ATTN_BWD_UNROLL)

    return pl.pallas_call(
        body, name=name, grid=(ATTN_WIDTH // LANES, seq // ATTN_STEP_ROWS),
        in_specs=[cur] * 6 + [prev, prev],
        out_specs=[cur, whole, whole],
        out_shape=[jax.ShapeDtypeStruct((seq, ATTN_WIDTH), F32)] * 3,
        compiler_params=_cparams("parallel", "arbitrary"),
    )(q, dy, lse, delta, k, v, k, v)


def hgrn_bwd(hg, lb_logits, hnw, o_pre, st0, dyb):
    seq = hg.shape[0]
    nblk = seq // HGRN_ROWS
    n_chunks = HGRN_ROWS // HGRN_CHUNK
    w = HGRN_WIDTH

    def body(hg_ref, lbl_ref, hnw_ref, o_ref, st0_ref, dyb_ref, dhg_ref, acc_ref, dst_scr):
        @pl.when(pl.program_id(0) == 0)
        def _():
            dst_scr[...] = jnp.zeros_like(dst_scr)
            acc_ref[...] = jnp.zeros_like(acc_ref)

        hg_v = hg_ref[...]
        p = _hgrn_prep(hg_v, lbl_ref[...])
        tril, block = _chunk_masks()
        vv = hg_v[:, 2 * w:3 * w].astype(BF16)
        hnw_v = hnw_ref[...]
        gb = p["gb"]
        sgg = _sigmoid(gb)
        silu_g = gb * sgg
        dyb_v = dyb_ref[...]
        o_v = o_ref[...]

        d_on = dyb_v * hnw_v * silu_g
        on_parts, do_parts = [], []
        for h in range(HGRN_HEADS):
            sl = slice(h * LANES, (h + 1) * LANES)
            rs = _rms(o_v[:, sl])
            on = o_v[:, sl] * rs
            on_parts.append(on)
            do_parts.append(rs * (d_on[:, sl] - on * jnp.mean(d_on[:, sl] * on, axis=-1, keepdims=True)))
        on_all = jnp.concatenate(on_parts, axis=1)
        dgb = dyb_v * on_all * hnw_v * (sgg * (1.0 + gb * (1.0 - sgg)))
        acc_ref[0:1, :] += jnp.sum(dyb_v * on_all * silu_g, axis=0, keepdims=True)

        dqf_parts, dkk_parts, db_parts, dv_parts, dbt_parts, dkbkb_parts = [], [], [], [], [], []
        for h in range(HGRN_HEADS):
            sl = slice(h * LANES, (h + 1) * LANES)
            v_h = vv[:, sl]
            do_h = do_parts[h].astype(BF16)
            qi, ki, qs, kb = p["qi"][:, sl], p["ki"][:, sl], p["qs"][:, sl], p["kb"][:, sl]
            qi_b, ki_b = qi.astype(BF16), ki.astype(BF16)
            kb_cat = _spread(kb.astype(BF16), block)
            qs_cat = _spread(qs.astype(BF16), block)
            upd = _dot_tn(v_h, kb_cat)
            st = st0_ref[h]
            st_parts = []
            for n in range(n_chunks):
                st_parts.append(st)
                decay = p["e_tot"][n * HGRN_CHUNK:n * HGRN_CHUNK + 1, sl]
                st = st * decay + upd[:, n * LANES:(n + 1) * LANES]
            st_cat = jnp.concatenate([s_.astype(BF16) for s_ in st_parts], axis=1)
            wgt = _dot_tn(do_h, qs_cat)
            dst = dst_scr[h]
            dst_parts = [None] * n_chunks
            dbt_rows = [None] * n_chunks
            for n in reversed(range(n_chunks)):
                dst_parts[n] = dst.astype(BF16)
                decay = p["e_tot"][n * HGRN_CHUNK:n * HGRN_CHUNK + 1, sl]
                dbt_rows[n] = jnp.sum(dst * st_parts[n], axis=0, keepdims=True) * decay
                dst = dst * decay + wgt[:, n * LANES:(n + 1) * LANES]
            dst_scr[h] = dst
            dst_cat = jnp.concatenate(dst_parts, axis=1)
            dqs = _fold(_dot(do_h, st_cat), block)
            dkb = _fold(_dot(v_h, dst_cat), block)
            dv_state = _dot_nt(kb_cat, dst_cat)
            a = jnp.where(tril, _dot_nt(qi_b, ki_b), 0.0).astype(BF16)
            da = jnp.where(tril, _dot_nt(do_h, v_h), 0.0).astype(BF16)
            dv_parts.append(_dot_tn(a, do_h) + dv_state)
            dqi = _dot(da, ki_b)
            dki = _dot_tn(da, qi_b)
            dqf_parts.append(dqi * p["e_iq"][:, sl] + dqs * p["e_b"][:, sl])
            dkk_parts.append(dki * p["e_ik"][:, sl] + dkb * p["e_bb"][:, sl])
            dkbkb = dkb * kb
            db_parts.append(dqi * qi - dki * ki + dqs * qs - dkbkb)
            dkbkb_parts.append(dkbkb)
            dbt_parts.append(_chunk_rows(dbt_rows))

        cat = lambda parts: jnp.concatenate(parts, axis=1)
        dlogf = (_chunk_cumsum(cat(db_parts), reverse=True)
                 + _chunk_row(_chunk_cumsum(cat(dkbkb_parts)), HGRN_CHUNK - 1) + cat(dbt_parts))
        sq, qb = p["sq"], hg_v[:, :w]
        dqb = cat(dqf_parts) * (sq * (1.0 + qb * (1.0 - sq)))
        df = dlogf / p["f"] - cat(dkk_parts)
        sg, lb = p["sg"], p["lb"]
        dfb = df * (1.0 - lb) * sg * (1.0 - sg)
        acc_ref[1:2, :] += jnp.sum(df * (1.0 - sg), axis=0, keepdims=True)
        dhg_ref[...] = jnp.concatenate([dqb, dfb, cat(dv_parts), dgb], axis=1)

    rev = lambda i: (nblk - 1 - i, 0)
    return pl.pallas_call(
        body, name="hgrn_bwd", grid=(nblk,),
        in_specs=[pl.BlockSpec((HGRN_ROWS, 4 * w), rev), _full((2, w)), _full((1, w)),
                  pl.BlockSpec((HGRN_ROWS, w), rev),
                  pl.BlockSpec((None, HGRN_HEADS, LANES, LANES), lambda i: (nblk - 1 - i, 0, 0, 0)),
                  pl.BlockSpec((HGRN_ROWS, w), rev)],
        out_specs=[pl.BlockSpec((HGRN_ROWS, 4 * w), rev), _full((8, w))],
        out_shape=[jax.ShapeDtypeStruct((seq, 4 * w), F32), jax.ShapeDtypeStruct((8, w), F32)],
        scratch_shapes=[pltpu.VMEM((HGRN_HEADS, LANES, LANES), F32)],
        compiler_params=_cparams("arbitrary"),
    )(hg, lb_logits, hnw, o_pre, st0, dyb)


def in_bwd(dqs, dks, dvs, dhg, cos_t, sin_t, w_in4, x, norm1_w, dh1):
    seq = x.shape[0]
    tm = ROW_TILE
    cw = w_in4.shape[2]

    def body(dq1, dq2, dq3, dk1, dk2, dk3, dv1, dv2, dv3, dhg_ref, cos_ref, sin_ref, w_ref,
             x_ref, nw_ref, dh1_ref, dproj_ref, dx_ref, acc_ref):
        @pl.when(pl.program_id(0) == 0)
        def _():
            acc_ref[...] = jnp.zeros_like(acc_ref)

        cos, sin = cos_ref[...], sin_ref[...]
        dqa = _rotary_bwd(dq1[...] + dq2[...] + dq3[...], cos, sin)
        dka = _rotary_bwd(dk1[...] + dk2[...] + dk3[...], cos, sin)
        dva = dv1[...] + dv2[...] + dv3[...]
        dproj = jnp.concatenate([dqa, dka, dva, dhg_ref[...]], axis=1).astype(BF16)
        dproj_ref[...] = dproj
        du = _dot_nt(dproj[:, :cw], w_ref[0])
        for j in range(1, N_CHIPS):
            du = du + _dot_nt(dproj[:, j * cw:(j + 1) * cw], w_ref[j])
        xv = x_ref[...]
        r1 = _rms(xv)
        nx = xv * r1
        acc_ref[0:1, :] += jnp.sum(du * nx, axis=0, keepdims=True)
        dn = du * nw_ref[...]
        dx_ref[...] = dh1_ref[...] + r1 * (dn - nx * jnp.mean(dn * nx, axis=-1, keepdims=True))

    half = _rows(tm, ATTN_WIDTH)
    wide = _rows(tm, D_MODEL)
    return pl.pallas_call(
        body, name="in_bwd", grid=(seq // tm,),
        in_specs=[half] * 9 + [_rows(tm, 4 * HGRN_WIDTH), _rows(tm, LANES), _rows(tm, LANES),
                               _full((N_CHIPS, D_MODEL, cw)), wide, _full((1, D_MODEL)), wide],
        out_specs=[_rows(tm, IN_PROJ_WIDTH), wide, _full((8, D_MODEL))],
        out_shape=[jax.ShapeDtypeStruct((seq, IN_PROJ_WIDTH), BF16), jax.ShapeDtypeStruct((seq, D_MODEL), F32),
                   jax.ShapeDtypeStruct((8, D_MODEL), F32)],
        compiler_params=_cparams("arbitrary"),
    )(*dqs, *dks, *dvs, dhg, cos_t, sin_t, w_in4, x, norm1_w, dh1)


def weight_grad(a, b, col_block, name):
    seq, kdim = a.shape
    ndim = b.shape[1]
    nj = ndim // col_block
    tk = 512

    def body(a_ref, b_ref, o_ref):
        @pl.when(pl.program_id(1) == 0)
        def _():
            o_ref[...] = jnp.zeros_like(o_ref)

        o_ref[...] += _dot_tn(a_ref[...].astype(BF16), b_ref[...].astype(BF16))

    return pl.pallas_call(
        body, name=name, grid=(nj, seq // tk),
        in_specs=[pl.BlockSpec((tk, kdim), lambda j, t: (t, 0)),
                  pl.BlockSpec((tk, col_block), lambda j, t: (t, j))],
        out_specs=pl.BlockSpec((None, kdim, col_block), lambda j, t: (j, 0, 0)),
        out_shape=jax.ShapeDtypeStruct((nj, kdim, col_block), F32),
        compiler_params=_cparams("parallel", "arbitrary"),
    )(a, b)


def exchange_with_sibling(grads):
    n = len(grads)

    def body(*refs):
        g_refs, out_refs = refs[:n], refs[n:2 * n]
        send_sems, recv_sems = refs[2 * n], refs[2 * n + 1]
        x, y, cc = _mesh_pos()
        copies = []
        for i in range(n):
            for j in range(N_CHIPS):
                k = i * N_CHIPS + j
                copies.append(pltpu.make_async_remote_copy(
                    src_ref=g_refs[i].at[j, 1 - cc], dst_ref=out_refs[i].at[j],
                    send_sem=send_sems.at[k], recv_sem=recv_sems.at[k],
                    device_id=(x, y, 1 - cc), device_id_type=MESH_ID))
        for cp in copies:
            cp.start()
        for cp in copies:
            cp.wait_recv()
        for cp in copies:
            cp.wait_send()

    return pl.pallas_call(
        body, name="grad_exchange_sibling",
        in_specs=[pl.BlockSpec(memory_space=pl.ANY)] * n,
        out_specs=[pl.BlockSpec(memory_space=pl.ANY)] * n,
        out_shape=[jax.ShapeDtypeStruct((N_CHIPS,) + g.shape[2:], g.dtype) for g in grads],
        scratch_shapes=[pltpu.SemaphoreType.DMA((n * N_CHIPS,)), pltpu.SemaphoreType.DMA((n * N_CHIPS,))],
    )(*grads)


def add_own_half(grad, recv, name):
    _, _, r, c = grad.shape
    tr = r // 2 if r % 32 == 0 else r

    def body(cc_ref, g_ref, r_ref, o_ref):
        o_ref[...] = (g_ref[...] + r_ref[...]).astype(BF16)

    grid_spec = pltpu.PrefetchScalarGridSpec(
        num_scalar_prefetch=1, grid=(N_CHIPS, r // tr),
        in_specs=[pl.BlockSpec((None, None, tr, c), lambda j, t, cc: (j, cc[0], t, 0)),
                  pl.BlockSpec((None, tr, c), lambda j, t, cc: (j, t, 0))],
        out_specs=pl.BlockSpec((None, tr, c), lambda j, t, cc: (j, t, 0)))
    cc = lax.axis_index("c").astype(jnp.int32).reshape(1)
    return pl.pallas_call(
        body, name=name, grid_spec=grid_spec,
        out_shape=jax.ShapeDtypeStruct((N_CHIPS, r, c), BF16),
        compiler_params=_cparams("parallel", "parallel"),
    )(cc, grad, recv)


def exchange_between_chips(sums):
    n = len(sums)

    def body(*refs):
        s_refs, out_refs = refs[:n], refs[n:2 * n]
        send_sems, recv_sems = refs[2 * n], refs[2 * n + 1]
        x, y, cc = _mesh_pos()
        my_chip = 2 * x + y
        chips = [(1 - x, y), (x, 1 - y), (1 - x, 1 - y)]
        copies = []
        for i in range(n):
            for j, (px, py) in enumerate(chips):
                k = i * 3 + j
                copies.append(pltpu.make_async_remote_copy(
                    src_ref=s_refs[i].at[2 * px + py], dst_ref=out_refs[i].at[my_chip],
                    send_sem=send_sems.at[k], recv_sem=recv_sems.at[k],
                    device_id=(px, py, cc), device_id_type=MESH_ID))
        for cp in copies:
            cp.start()
        for i in range(n):
            for j, (px, py) in enumerate(chips):
                k = i * 3 + j
                pltpu.make_async_remote_copy(
                    src_ref=s_refs[i].at[my_chip], dst_ref=out_refs[i].at[2 * px + py],
                    send_sem=send_sems.at[k], recv_sem=recv_sems.at[k],
                    device_id=(px, py, cc), device_id_type=MESH_ID).wait_recv()
        for cp in copies:
            cp.wait_send()

    return pl.pallas_call(
        body, name="grad_exchange_chips",
        in_specs=[pl.BlockSpec(memory_space=pl.ANY)] * n,
        out_specs=[pl.BlockSpec(memory_space=pl.ANY)] * n,
        out_shape=[jax.ShapeDtypeStruct(s.shape, s.dtype) for s in sums],
        scratch_shapes=[pltpu.SemaphoreType.DMA((n * 3,)), pltpu.SemaphoreType.DMA((n * 3,))],
    )(*sums)


def sum_chips(sums, parts, name):
    _, r, c = parts.shape
    tr = r // 2 if r % 32 == 0 else r

    def body(idx_ref, s_ref, p1_ref, p2_ref, p3_ref, o_ref):
        o_ref[...] = ((s_ref[...].astype(F32) + p1_ref[...].astype(F32))
                      + p2_ref[...].astype(F32)) + p3_ref[...].astype(F32)

    def pick(k):
        return pl.BlockSpec((None, tr, c), lambda t, idx: (idx[k], t, 0))

    x, y = lax.axis_index("x"), lax.axis_index("y")
    idx = jnp.stack([2 * x + y, 2 * (1 - x) + y, 2 * x + (1 - y), 2 * (1 - x) + (1 - y)]).astype(jnp.int32)
    grid_spec = pltpu.PrefetchScalarGridSpec(
        num_scalar_prefetch=1, grid=(r // tr,),
        in_specs=[pick(0), pick(1), pick(2), pick(3)],
        out_specs=pl.BlockSpec((tr, c), lambda t, idx: (t, 0)))
    return pl.pallas_call(
        body, name=name, grid_spec=grid_spec,
        out_shape=jax.ShapeDtypeStruct((r, c), F32),
        compiler_params=_cparams("parallel"),
    )(idx, sums, parts, parts, parts)


def share_with_sibling(halves):
    n = len(halves)

    def body(*refs):
        h_refs, out_refs = refs[:n], refs[n:2 * n]
        send_sems, recv_sems = refs[2 * n], refs[2 * n + 1]
        x, y, cc = _mesh_pos()
        copies = [pltpu.make_async_remote_copy(
            src_ref=h_refs[i], dst_ref=out_refs[i],
            send_sem=send_sems.at[i], recv_sem=recv_sems.at[i],
            device_id=(x, y, 1 - cc), device_id_type=MESH_ID) for i in range(n)]
        for cp in copies:
            cp.start()
        for cp in copies:
            cp.wait_recv()
        for cp in copies:
            cp.wait_send()

    return pl.pallas_call(
        body, name="grad_share_sibling",
        in_specs=[pl.BlockSpec(memory_space=pl.ANY)] * n,
        out_specs=[pl.BlockSpec(memory_space=pl.ANY)] * n,
        out_shape=[jax.ShapeDtypeStruct(h.shape, h.dtype) for h in halves],
        scratch_shapes=[pltpu.SemaphoreType.DMA((n,)), pltpu.SemaphoreType.DMA((n,))],
    )(*halves)


def _adam_update(w, g, m, v):
    m = ADAM_B1 * m + (1.0 - ADAM_B1) * g
    v = ADAM_B2 * v + (1.0 - ADAM_B2) * (g * g)
    m_hat = m / (1.0 - ADAM_B1 ** ADAM_STEP)
    v_hat = v / (1.0 - ADAM_B2 ** ADAM_STEP)
    delta = -ADAM_LR * (m_hat / (jnp.sqrt(v_hat) + ADAM_EPS) + ADAM_WD * w)
    return delta, m, v


def adamw(w, g_mine, g_sibling, m, v, name):
    r, c = w.shape
    half = r // 2
    tr = half // 2 if half % 16 == 0 else half
    nt = half // tr

    def body(cc_ref, w_ref, ga_ref, gb_ref, m_ref, v_ref, g_ref, d_ref, nm_ref, nv_ref):
        g = jnp.where(pl.program_id(0) == cc_ref[0], ga_ref[...], gb_ref[...])
        g_ref[...] = g
        d, nm, nv = _adam_update(w_ref[...], g, m_ref[...], v_ref[...])
        d_ref[...] = d
        nm_ref[...] = nm
        nv_ref[...] = nv

    full = pl.BlockSpec((tr, c), lambda h, t, cc: (h * nt + t, 0))
    part = pl.BlockSpec((tr, c), lambda h, t, cc: (t, 0))
    grid_spec = pltpu.PrefetchScalarGridSpec(
        num_scalar_prefetch=1, grid=(2, nt),
        in_specs=[full, part, part, full, full], out_specs=[full] * 4)
    cc = lax.axis_index("c").astype(jnp.int32).reshape(1)
    return pl.pallas_call(
        body, name=name, grid_spec=grid_spec,
        out_shape=[jax.ShapeDtypeStruct((r, c), F32)] * 4,
        compiler_params=_cparams("parallel", "parallel"),
    )(cc, w, g_mine, g_sibling, m, v)


def small_allreduce(pack):
    def body(p_ref, o_ref, gather, send_sems, recv_sems):
        x, y, cc = _mesh_pos()
        me = 4 * x + 2 * y + cc
        gather[me] = p_ref[...]
        flips = [(fx, fy, fc) for fx in (0, 1) for fy in (0, 1) for fc in (0, 1)][1:]
        copies = []
        for k, (fx, fy, fc) in enumerate(flips):
            copies.append(pltpu.make_async_remote_copy(
                src_ref=p_ref, dst_ref=gather.at[me],
                send_sem=send_sems.at[k], recv_sem=recv_sems.at[k],
                device_id=(x ^ fx, y ^ fy, cc ^ fc), device_id_type=MESH_ID))
        for cp in copies:
            cp.start()
        for k, (fx, fy, fc) in enumerate(flips):
            src = 4 * (x ^ fx) + 2 * (y ^ fy) + (cc ^ fc)
            pltpu.make_async_remote_copy(
                src_ref=p_ref, dst_ref=gather.at[src],
                send_sem=send_sems.at[k], recv_sem=recv_sems.at[k],
                device_id=(x ^ fx, y ^ fy, cc ^ fc), device_id_type=MESH_ID).wait_recv()
        for cp in copies:
            cp.wait_send()
        total = gather[0]
        for d in range(1, N_DEV):
            total = total + gather[d]
        o_ref[...] = total

    return pl.pallas_call(
        body, name="small_allreduce",
        in_specs=[pl.BlockSpec(memory_space=pltpu.VMEM)],
        out_specs=pl.BlockSpec(memory_space=pltpu.VMEM),
        out_shape=jax.ShapeDtypeStruct(pack.shape, pack.dtype),
        scratch_shapes=[pltpu.VMEM((N_DEV,) + pack.shape, pack.dtype),
                        pltpu.SemaphoreType.DMA((7,)), pltpu.SemaphoreType.DMA((7,))],
    )(pack)


def small_update(gsum, wpack, mpack, vpack):
    hw = HGRN_WIDTH

    def body(g_ref, w_ref, m_ref, v_ref, go_ref, d_ref, nm_ref, nv_ref, loss_ref):
        g = g_ref[...]
        wv = w_ref[...]
        a0, a1 = wv[4:5, :hw], wv[4:5, hw:]
        mx = jnp.maximum(a0, a1)
        e0, e1 = jnp.exp(a0 - mx), jnp.exp(a1 - mx)
        lb = e0 / (e0 + e1)
        dl = g[4:5, :hw] * lb * (1.0 - lb)
        row = lax.broadcasted_iota(jnp.int32, g.shape, 0)
        lb_row = jnp.concatenate([dl, -dl], axis=1)
        grads = jnp.where(row == 4, lb_row, jnp.where(row < 4, g, 0.0))
        go_ref[...] = grads
        d, nm, nv = _adam_update(wv, grads, m_ref[...], v_ref[...])
        d_ref[...] = d
        nm_ref[...] = nm
        nv_ref[...] = nv
        loss_ref[...] = jnp.zeros((8, LANES), F32) + jnp.sum(g[5:6, :])

    vm = pl.BlockSpec(memory_space=pltpu.VMEM)
    return pl.pallas_call(
        body, name="small_update",
        in_specs=[vm] * 4, out_specs=[vm] * 5,
        out_shape=[jax.ShapeDtypeStruct(gsum.shape, F32)] * 4 + [jax.ShapeDtypeStruct((8, LANES), F32)],
    )(gsum, wpack, mpack, vpack)


def _pack_small(n1, n2, fn, hn, lbl):
    z = jnp.zeros((1, D_MODEL - HGRN_WIDTH), F32)
    rows = [n1.reshape(1, D_MODEL), n2.reshape(1, D_MODEL), fn.reshape(1, D_MODEL),
            jnp.concatenate([hn.reshape(1, HGRN_WIDTH), z], axis=1), lbl.reshape(1, 2 * HGRN_WIDTH),
            jnp.zeros((3, D_MODEL), F32)]
    return jnp.concatenate(rows, axis=0)


def _unpack_small(pack):
    return (pack[0:1], pack[4].reshape(2, HGRN_WIDTH), pack[3:4, :HGRN_WIDTH], pack[1:2], pack[2])


def kernel(x, norm1_w, w_in, lb_logits, hgrn_norm_w, w_out, norm2_w, w_gate_up, w_down, final_norm_w, loss_target, m_norm1_w, m_w_in, m_lb_logits, m_hgrn_norm_w, m_w_out, m_norm2_w, m_w_gate_up, m_w_down, m_final_norm_w, v_norm1_w, v_w_in, v_lb_logits, v_hgrn_norm_w, v_w_out, v_norm2_w, v_w_gate_up, v_w_down, v_final_norm_w):
    seq = x.shape[1]
    xs = x.reshape(seq, D_MODEL)
    target = loss_target.reshape(seq, D_MODEL)
    shards = {"w_in": w_in[0], "w_out": w_out[0], "w_gu": w_gate_up[0], "w_down": w_down[0]}

    gathered = {k: allgather_halves(cast_bf16(w, "cast_" + k), "gather_" + k) for k, w in shards.items()}
    w_in4 = gathered["w_in"].reshape(N_CHIPS, D_MODEL, -1)
    w_out_f = gathered["w_out"].reshape(D_MODEL, D_MODEL)
    w_gu4 = gathered["w_gu"].reshape(N_CHIPS, D_MODEL, -1)
    w_down_f = gathered["w_down"].reshape(FFN_HIDDEN, D_MODEL)

    cos_t, sin_t = _rope_tables(seq)
    fw = final_norm_w.reshape(1, D_MODEL)

    qr, kr, va, hg, u = in_proj(xs, norm1_w, w_in4, cos_t, sin_t)
    fwd = [attn_fwd(qr, kr, va, d, "attn_fwd_d%d" % d) for _, d in DILATED_PAIRS]
    yb, o_pre, st0 = hgrn_fwd(hg, lb_logits, hgrn_norm_w)
    ya, lse, mixed, h1, u2 = mix_out([f[0] for f in fwd], [f[1] for f in fwd], yb, xs, w_out_f, norm2_w)
    g, up, act = gate_up(u2, w_gu4)
    dh2, acc_fin = down_loss(act, w_down_f, h1, fw, target)

    dgu = down_bwd(dh2, w_down_f, g, up)
    dh1, dya, dyb, delta, acc_n2 = gu_bwd(dgu, w_gu4, h1, norm2_w, dh2, w_out_f, ya)
    bwd = [attn_bwd(qr, kr, va, dya, lse, delta, d, "attn_bwd_d%d" % d) for _, d in DILATED_PAIRS]
    dhg, acc_hg = hgrn_bwd(hg, lb_logits, hgrn_norm_w, o_pre, st0, dyb)
    dproj, dx, acc_n1 = in_bwd([b[0] for b in bwd], [b[1] for b in bwd], [b[2] for b in bwd],
                               dhg, cos_t, sin_t, w_in4, xs, norm1_w, dh1)

    cw_in, cw_gu = w_in4.shape[2], w_gu4.shape[2]
    grads = [
        weight_grad(u, dproj, cw_in, "wgrad_in").reshape(N_CHIPS, 2, D_MODEL // 2, cw_in),
        weight_grad(mixed, dh1, D_MODEL, "wgrad_out").reshape(N_CHIPS, 2, D_MODEL // 8, D_MODEL),
        weight_grad(u2, dgu, cw_gu, "wgrad_gu").reshape(N_CHIPS, 2, D_MODEL // 2, cw_gu),
        weight_grad(act, dh2, D_MODEL, "wgrad_down").reshape(N_CHIPS, 2, FFN_HIDDEN // 8, D_MODEL),
    ]
    names = ["in", "out", "gu", "down"]
    recv = exchange_with_sibling(grads)
    sums = [add_own_half(gr, rc, "add_half_" + nm) for gr, rc, nm in zip(grads, recv, names)]
    parts = exchange_between_chips(sums)
    halves = [sum_chips(s, p, "sum_chips_" + nm) for s, p, nm in zip(sums, parts, names)]
    others = share_with_sibling(halves)
    big = {}
    for nm, key, mine, other, m_, v_ in zip(names, ["w_in", "w_out", "w_gu", "w_down"], halves, others,
                                            [m_w_in, m_w_out, m_w_gate_up, m_w_down],
                                            [v_w_in, v_w_out, v_w_gate_up, v_w_down]):
        big[key] = tuple(t[None] for t in adamw(shards[key], mine, other, m_[0], v_[0], "adamw_" + nm))

    z512 = jnp.zeros((1, D_MODEL - HGRN_WIDTH), F32)
    gpack = jnp.concatenate([
        acc_n1[0:1], acc_n2[0:1], acc_fin[0:1],
        jnp.concatenate([acc_hg[0:1], z512], axis=1), jnp.concatenate([acc_hg[1:2], z512], axis=1),
        acc_fin[1:2], jnp.zeros((2, D_MODEL), F32)], axis=0)
    gsum = small_allreduce(gpack)
    wpack = _pack_small(norm1_w, norm2_w, final_norm_w, hgrn_norm_w, lb_logits)
    mpack = _pack_small(m_norm1_w, m_norm2_w, m_final_norm_w, m_hgrn_norm_w, m_lb_logits)
    vpack = _pack_small(v_norm1_w, v_norm2_w, v_final_norm_w, v_hgrn_norm_w, v_lb_logits)
    gs, ds, nms, nvs, loss8 = small_update(gsum, wpack, mpack, vpack)
    loss = loss8[0, 0]

    def assemble(small_pack, idx):
        n1, lbl, hn, n2, fn = _unpack_small(small_pack)
        return (n1, big["w_in"][idx], lbl, hn, big["w_out"][idx], n2, big["w_gu"][idx], big["w_down"][idx], fn)

    return (loss, dx.reshape(x.shape), *assemble(gs, 0), *assemble(ds, 1), *assemble(nms, 2), *assemble(nvs, 3))
```

```python
import functools

import jax
import jax.numpy as jnp
from jax import lax
from jax.experimental import pallas as pl
from jax.experimental.pallas import tpu as pltpu

F32 = jnp.float32
BF16 = jnp.bfloat16

D_MODEL = 1024
ATTN_WIDTH = 512
HEAD_DIM = 64
DILATED_PAIRS = ((128, 1), (512, 4), (2048, 16))
ATTN_BLOCK = 128
ROPE_THETA = 10000.0
HGRN_WIDTH = 512
HGRN_CHUNK = 16
HGRN_HEADS = 4
IN_PROJ_WIDTH = 3584
FFN_HIDDEN = 2816
NORM_EPS = 1e-6
ATTN_SCALE = HEAD_DIM ** -0.5
N_CHIPS = 4
N_DEV = 8

ADAM_LR = 0.001
ADAM_B1 = 0.9
ADAM_B2 = 0.999
ADAM_EPS = 1e-08
ADAM_WD = 0.01
ADAM_STEP = 10

LANES = 128
HGRN_ROWS = 128
ROW_TILE = 256
ATTN_STEP_ROWS = 2048
ATTN_FWD_UNROLL = 8
ATTN_BWD_UNROLL = 8
VMEM_LIMIT = 56 * 1024 * 1024
NEG_BIG = -1e30
MESH_ID = pl.DeviceIdType.MESH


def _cparams(*sem):
    return pltpu.CompilerParams(dimension_semantics=tuple(sem), vmem_limit_bytes=VMEM_LIMIT)


def _dot(a, b):
    return jnp.dot(a, b, preferred_element_type=F32)


def _dot_nt(a, b):
    return lax.dot_general(a, b, (((1,), (1,)), ((), ())), preferred_element_type=F32)


def _dot_tn(a, b):
    return lax.dot_general(a, b, (((0,), (0,)), ((), ())), preferred_element_type=F32)


def _sigmoid(x):
    return 1.0 / (1.0 + jnp.exp(-x))


def _full(shape):
    n = len(shape)
    return pl.BlockSpec(shape, lambda *_: (0,) * n)


def _rows(tm, width):
    return pl.BlockSpec((tm, width), lambda i: (i, 0))


def _swap32(x):
    lane = lax.broadcasted_iota(jnp.int32, x.shape, 1)
    first = (lane % HEAD_DIM) < (HEAD_DIM // 2)
    return jnp.where(first, pltpu.roll(x, LANES - 32, axis=1), pltpu.roll(x, 32, axis=1))


def _rotary_fwd(x, cos, sin_signed):
    parts = []
    for j in range(x.shape[1] // LANES):
        xc = x[:, j * LANES:(j + 1) * LANES]
        parts.append(xc * cos + _swap32(xc) * sin_signed)
    return jnp.concatenate(parts, axis=1)


def _rotary_bwd(dy, cos, sin_signed):
    parts = []
    for j in range(dy.shape[1] // LANES):
        dc = dy[:, j * LANES:(j + 1) * LANES]
        parts.append(dc * cos + _swap32(dc * sin_signed))
    return jnp.concatenate(parts, axis=1)


def _rope_tables(seq):
    half = HEAD_DIM // 2
    inv_freq = ROPE_THETA ** (-jnp.arange(half, dtype=F32) / half)
    ang = jnp.arange(seq, dtype=F32)[:, None] * inv_freq[None, :]
    cos, sin = jnp.cos(ang), jnp.sin(ang)
    cos_t = jnp.tile(cos, (1, LANES // half))
    sin_t = jnp.tile(jnp.concatenate([-sin, sin], axis=1), (1, LANES // HEAD_DIM))
    return cos_t, sin_t


def cast_bf16(w, name):
    r, c = w.shape
    half = r // 2

    def body(w_ref, o_ref):
        o_ref[...] = w_ref[...].astype(BF16)

    return pl.pallas_call(
        body, name=name, grid=(2,),
        in_specs=[pl.BlockSpec((half, c), lambda i: (i, 0))],
        out_specs=pl.BlockSpec((None, half, c), lambda i: (i, 0, 0)),
        out_shape=jax.ShapeDtypeStruct((2, half, c), BF16),
        compiler_params=_cparams("parallel"),
    )(w)


def _mesh_pos():
    return lax.axis_index("x"), lax.axis_index("y"), lax.axis_index("c")


def allgather_halves(halves, name):
    _, r, c = halves.shape

    def body(x_ref, out_ref, send_sems, recv_sems, local_sem):
        x, y, cc = _mesh_pos()
        me, sibling = (x, y, cc), (x, y, 1 - cc)
        chips = [(1 - x, y), (x, 1 - y), (1 - x, 1 - y)]
        mine_src = x_ref.at[cc]

        def rows(px, py, pc):
            return out_ref.at[4 * px + 2 * py + pc]

        def copy(k, block, to, src=None):
            return pltpu.make_async_remote_copy(
                src_ref=rows(*block) if src is None else src, dst_ref=rows(*block),
                send_sem=send_sems.at[k], recv_sem=recv_sems.at[k],
                device_id=to, device_id_type=MESH_ID)

        mine = pltpu.make_async_copy(mine_src, rows(*me), local_sem)
        mine.start()
        first = [copy(0, me, sibling, src=mine_src)]
        first += [copy(1 + j, me, (*chip, cc), src=mine_src) for j, chip in enumerate(chips)]
        for cp in first:
            cp.start()
        passed = [copy(4 + j, (*chip, cc), sibling) for j, chip in enumerate(chips)]
        for j, chip in enumerate(chips):
            copy(1 + j, (*chip, cc), me).wait_recv()
            passed[j].start()
        copy(0, sibling, me).wait_recv()
        for j, chip in enumerate(chips):
            copy(4 + j, (*chip, 1 - cc), me).wait_recv()
        for cp in first + passed:
            cp.wait_send()
        mine.wait()

    return pl.pallas_call(
        body, name=name,
        in_specs=[pl.BlockSpec(memory_space=pl.ANY)],
        out_specs=pl.BlockSpec(memory_space=pl.ANY),
        out_shape=jax.ShapeDtypeStruct((N_DEV, r, c), halves.dtype),
        scratch_shapes=[pltpu.SemaphoreType.DMA((7,)), pltpu.SemaphoreType.DMA((7,)),
                        pltpu.SemaphoreType.DMA],
    )(halves)


def _rms(x):
    return lax.rsqrt(jnp.mean(x * x, axis=-1, keepdims=True) + NORM_EPS)


def in_proj(x, norm1_w, w_in4, cos_t, sin_t):
    seq = x.shape[0]
    tm = ROW_TILE
    cw = w_in4.shape[2]

    def body(x_ref, nw_ref, w_ref, cos_ref, sin_ref, q_ref, k_ref, v_ref, hg_ref, u_ref):
        xv = x_ref[...]
        u = ((xv * _rms(xv)) * nw_ref[...]).astype(BF16)
        u_ref[...] = u
        proj = jnp.concatenate([_dot(u, w_ref[j]) for j in range(N_CHIPS)], axis=1)
        cos, sin = cos_ref[...], sin_ref[...]
        a = ATTN_WIDTH
        q_ref[...] = _rotary_fwd(proj[:, :a], cos, sin)
        k_ref[...] = _rotary_fwd(proj[:, a:2 * a], cos, sin)
        v_ref[...] = proj[:, 2 * a:3 * a]
        hg_ref[...] = proj[:, 3 * a:]

    return pl.pallas_call(
        body, name="in_proj", grid=(seq // tm,),
        in_specs=[_rows(tm, D_MODEL), _full((1, D_MODEL)), _full((N_CHIPS, D_MODEL, cw)),
                  _rows(tm, LANES), _rows(tm, LANES)],
        out_specs=[_rows(tm, ATTN_WIDTH)] * 3 + [_rows(tm, 4 * HGRN_WIDTH), _rows(tm, D_MODEL)],
        out_shape=[jax.ShapeDtypeStruct((seq, ATTN_WIDTH), F32)] * 3
        + [jax.ShapeDtypeStruct((seq, 4 * HGRN_WIDTH), F32), jax.ShapeDtypeStruct((seq, D_MODEL), BF16)],
        compiler_params=_cparams("parallel"),
    )(x, norm1_w, w_in4, cos_t, sin_t)


def _head_masks():
    lane = lax.broadcasted_iota(jnp.int32, (1, LANES), 1)
    return [(lane // HEAD_DIM) == h for h in range(LANES // HEAD_DIM)]


def _window_valid(no_prev):
    qi = lax.broadcasted_iota(jnp.int32, (ATTN_BLOCK, 2 * ATTN_BLOCK), 0)
    kj = lax.broadcasted_iota(jnp.int32, (ATTN_BLOCK, 2 * ATTN_BLOCK), 1)
    valid = (kj >= qi) & (kj <= qi + ATTN_BLOCK)
    return valid & (jnp.logical_not(no_prev) | (kj >= ATTN_BLOCK))


def _strided_rows(start, dilation):
    if dilation == 1:
        return pl.ds(start, ATTN_BLOCK)
    return pl.ds(start, ATTN_BLOCK, stride=dilation)


def _block_before(edge_ref, cur_ref, t, r, span, dilation, per_step):
    edge = edge_ref[_strided_rows(r, dilation), :]
    if per_step == 1:
        return edge
    inside = cur_ref[_strided_rows(r + span * jnp.maximum(t - 1, 0), dilation), :]
    return jnp.where(t == 0, edge, inside)


def _attn_specs(seq, dilation):
    span = ATTN_BLOCK * dilation
    per_step = ATTN_STEP_ROWS // span
    cur = pl.BlockSpec((ATTN_STEP_ROWS, LANES), lambda hp, j: (j, hp))
    prev = pl.BlockSpec((span, LANES), lambda hp, j: (jnp.maximum(j * per_step - 1, 0), hp))
    return span, per_step, cur, prev


def attn_fwd(q, k, v, dilation, name):
    seq = q.shape[0]
    span, per_step, cur, prev = _attn_specs(seq, dilation)

    def body(q_ref, kc_ref, vc_ref, kp_ref, vp_ref, o_ref, lse_ref):
        first_step = pl.program_id(1) == 0
        masks = _head_masks()

        def block(it, carry):
            t, r = it // dilation, it % dilation
            rows = _strided_rows(r + span * t, dilation)
            at_edge = t == 0
            q2 = q_ref[rows, :].astype(BF16)
            kp = _block_before(kp_ref, kc_ref, t, r, span, dilation, per_step)
            vp = _block_before(vp_ref, vc_ref, t, r, span, dilation, per_step)
            k2 = jnp.concatenate([kp, kc_ref[rows, :]], axis=0).astype(BF16)
            v2 = jnp.concatenate([vp, vc_ref[rows, :]], axis=0).astype(BF16)
            valid = _window_valid(first_step & at_edge)
            o_acc = jnp.zeros((ATTN_BLOCK, LANES), F32)
            l_acc = jnp.zeros((ATTN_BLOCK, LANES), F32)
            for mh in masks:
                qm = jnp.where(mh, q2, jnp.zeros_like(q2))
                s = jnp.where(valid, _dot_nt(qm, k2) * ATTN_SCALE, NEG_BIG)
                m = jnp.max(s, axis=-1, keepdims=True)
                p = jnp.exp(s - m)
                l = jnp.sum(p, axis=-1, keepdims=True)
                o = _dot(p.astype(BF16), v2) / l
                o_acc = jnp.where(mh, o, o_acc)
                l_acc = jnp.where(mh, m + jnp.log(l), l_acc)
            o_ref[rows, :] = o_acc
            lse_ref[rows, :] = l_acc
            return carry

        lax.fori_loop(0, per_step * dilation, block, 0, unroll=ATTN_FWD_UNROLL)

    return pl.pallas_call(
        body, name=name, grid=(ATTN_WIDTH // LANES, seq // ATTN_STEP_ROWS),
        in_specs=[cur, cur, cur, prev, prev],
        out_specs=[cur, cur],
        out_shape=[jax.ShapeDtypeStruct((seq, ATTN_WIDTH), F32)] * 2,
        compiler_params=_cparams("parallel", "parallel"),
    )(q, k, v, k, v)


def _chunk_cumsum(x, reverse=False):
    rc = lax.broadcasted_iota(jnp.int32, x.shape, 0) % HGRN_CHUNK
    sh = 1
    while sh < HGRN_CHUNK:
        if reverse:
            x = x + jnp.where(rc + sh < HGRN_CHUNK, pltpu.roll(x, x.shape[0] - sh, axis=0), 0.0)
        else:
            x = x + jnp.where(rc >= sh, pltpu.roll(x, sh, axis=0), 0.0)
        sh *= 2
    return x


def _chunk_row(x, row):
    return _chunk_rows([x[n * HGRN_CHUNK + row:n * HGRN_CHUNK + row + 1, :]
                        for n in range(x.shape[0] // HGRN_CHUNK)])


def _chunk_rows(rows):
    return jnp.concatenate([jnp.broadcast_to(r, (HGRN_CHUNK, r.shape[1])) for r in rows], axis=0)


def _hgrn_prep(hg, lbl):
    w = HGRN_WIDTH
    a0, a1 = lbl[0:1, :], lbl[1:2, :]
    mx = jnp.maximum(a0, a1)
    e0, e1 = jnp.exp(a0 - mx), jnp.exp(a1 - mx)
    lb = e0 / (e0 + e1)
    qb, fb, gb = hg[:, :w], hg[:, w:2 * w], hg[:, 3 * w:]
    sg = _sigmoid(fb)
    f = lb + (1.0 - lb) * sg
    b = _chunk_cumsum(jnp.log(f))
    bmid, btot = _chunk_row(b, HGRN_CHUNK // 2 - 1), _chunk_row(b, HGRN_CHUNK - 1)
    sq = _sigmoid(qb)
    p = dict(lb=lb, sg=sg, f=f, kk=1.0 - f, sq=sq, qf=qb * sq, gb=gb,
             e_iq=jnp.exp(b - bmid), e_ik=jnp.exp(bmid - b), e_b=jnp.exp(b),
             e_bb=jnp.exp(btot - b), e_tot=jnp.exp(btot))
    p["qi"] = p["qf"] * p["e_iq"]
    p["ki"] = p["kk"] * p["e_ik"]
    p["qs"] = p["qf"] * p["e_b"]
    p["kb"] = p["kk"] * p["e_bb"]
    return p


def _chunk_masks():
    t = lax.broadcasted_iota(jnp.int32, (HGRN_ROWS, HGRN_ROWS), 0)
    s = lax.broadcasted_iota(jnp.int32, (HGRN_ROWS, HGRN_ROWS), 1)
    tril = ((t // HGRN_CHUNK) == (s // HGRN_CHUNK)) & (s <= t)
    n_chunks = HGRN_ROWS // HGRN_CHUNK
    tt = lax.broadcasted_iota(jnp.int32, (HGRN_ROWS, n_chunks * LANES), 0)
    cc = lax.broadcasted_iota(jnp.int32, (HGRN_ROWS, n_chunks * LANES), 1)
    block = (tt // HGRN_CHUNK) == (cc // LANES)
    return tril, block


def _spread(x, block):
    n_chunks = HGRN_ROWS // HGRN_CHUNK
    return jnp.where(block, jnp.tile(x, (1, n_chunks)), jnp.zeros((), x.dtype))


def _fold(x_full, block):
    n_chunks = HGRN_ROWS // HGRN_CHUNK
    z = jnp.where(block, x_full, 0.0)
    acc = z[:, :LANES]
    for n in range(1, n_chunks):
        acc = acc + z[:, n * LANES:(n + 1) * LANES]
    return acc


def hgrn_fwd(hg, lb_logits, hnw):
    seq = hg.shape[0]
    nblk = seq // HGRN_ROWS
    n_chunks = HGRN_ROWS // HGRN_CHUNK

    def body(hg_ref, lbl_ref, hnw_ref, yb_ref, o_ref, st0_ref, st_scr):
        @pl.when(pl.program_id(0) == 0)
        def _():
            st_scr[...] = jnp.zeros_like(st_scr)

        hg_v = hg_ref[...]
        p = _hgrn_prep(hg_v, lbl_ref[...])
        tril, block = _chunk_masks()
        vv = hg_v[:, 2 * HGRN_WIDTH:3 * HGRN_WIDTH].astype(BF16)
        outs = []
        for h in range(HGRN_HEADS):
            sl = slice(h * LANES, (h + 1) * LANES)
            v_h = vv[:, sl]
            a = jnp.where(tril, _dot_nt(p["qi"][:, sl].astype(BF16), p["ki"][:, sl].astype(BF16)), 0.0)
            o = _dot(a.astype(BF16), v_h)
            upd = _dot_tn(v_h, _spread(p["kb"][:, sl].astype(BF16), block))
            st = st_scr[h]
            st0_ref[h] = st
            parts = []
            for n in range(n_chunks):
                parts.append(st.astype(BF16))
                decay = p["e_tot"][n * HGRN_CHUNK:n * HGRN_CHUNK + 1, sl]
                st = st * decay + upd[:, n * LANES:(n + 1) * LANES]
            st_scr[h] = st
            o = o + _dot_nt(_spread(p["qs"][:, sl].astype(BF16), block), jnp.concatenate(parts, axis=1))
            outs.append(o)
        o_all = jnp.concatenate(outs, axis=1)
        o_ref[...] = o_all
        normed = jnp.concatenate(
            [outs[h] * _rms(outs[h]) for h in range(HGRN_HEADS)], axis=1)
        gb = p["gb"]
        yb_ref[...] = (normed * hnw_ref[...]) * (gb * _sigmoid(gb))

    return pl.pallas_call(
        body, name="hgrn_fwd", grid=(nblk,),
        in_specs=[_rows(HGRN_ROWS, 4 * HGRN_WIDTH), _full((2, HGRN_WIDTH)), _full((1, HGRN_WIDTH))],
        out_specs=[_rows(HGRN_ROWS, HGRN_WIDTH), _rows(HGRN_ROWS, HGRN_WIDTH),
                   pl.BlockSpec((None, HGRN_HEADS, LANES, LANES), lambda i: (i, 0, 0, 0))],
        out_shape=[jax.ShapeDtypeStruct((seq, HGRN_WIDTH), F32)] * 2
        + [jax.ShapeDtypeStruct((nblk, HGRN_HEADS, LANES, LANES), F32)],
        scratch_shapes=[pltpu.VMEM((HGRN_HEADS, LANES, LANES), F32)],
        compiler_params=_cparams("arbitrary"),
    )(hg, lb_logits, hnw)


def mix_out(outs, lses, yb, x, w_out, norm2_w):
    seq = x.shape[0]
    tm = ROW_TILE

    def body(o1, o2, o3, l1, l2, l3, yb_ref, x_ref, w_ref, nw_ref,
             ya_ref, lse_ref, mixed_ref, h1_ref, u2_ref):
        l1v, l2v, l3v = l1[...], l2[...], l3[...]
        mx = jnp.maximum(jnp.maximum(l1v, l2v), l3v)
        e1, e2, e3 = jnp.exp(l1v - mx), jnp.exp(l2v - mx), jnp.exp(l3v - mx)
        den = e1 + e2 + e3
        ya = (e1 * o1[...] + e2 * o2[...] + e3 * o3[...]) / den
        ya_ref[...] = ya
        lse_ref[...] = mx + jnp.log(den)
        mixed = jnp.concatenate([ya, yb_ref[...]], axis=1).astype(BF16)
        mixed_ref[...] = mixed
        h1 = x_ref[...] + _dot(mixed, w_ref[...])
        h1_ref[...] = h1
        u2_ref[...] = ((h1 * _rms(h1)) * nw_ref[...]).astype(BF16)

    half = _rows(tm, ATTN_WIDTH)
    wide = _rows(tm, D_MODEL)
    return pl.pallas_call(
        body, name="mix_out", grid=(seq // tm,),
        in_specs=[half] * 7 + [wide, _full((D_MODEL, D_MODEL)), _full((1, D_MODEL))],
        out_specs=[half, half, wide, wide, wide],
        out_shape=[jax.ShapeDtypeStruct((seq, ATTN_WIDTH), F32)] * 2
        + [jax.ShapeDtypeStruct((seq, D_MODEL), BF16), jax.ShapeDtypeStruct((seq, D_MODEL), F32),
           jax.ShapeDtypeStruct((seq, D_MODEL), BF16)],
        compiler_params=_cparams("parallel"),
    )(*outs, *lses, yb, x, w_out, norm2_w)


def gate_up(u2, w_gu4):
    seq = u2.shape[0]
    tm = ROW_TILE
    cw = w_gu4.shape[2]

    def body(u_ref, w_ref, g_ref, up_ref, act_ref):
        u = u_ref[...]
        g = jnp.concatenate([_dot(u, w_ref[0]), _dot(u, w_ref[1])], axis=1)
        up = jnp.concatenate([_dot(u, w_ref[2]), _dot(u, w_ref[3])], axis=1)
        g_ref[...] = g.astype(BF16)
        up_ref[...] = up.astype(BF16)
        act_ref[...] = ((g * _sigmoid(g)) * up).astype(BF16)

    ffn = _rows(tm, FFN_HIDDEN)
    return pl.pallas_call(
        body, name="gate_up", grid=(seq // tm,),
        in_specs=[_rows(tm, D_MODEL), _full((N_CHIPS, D_MODEL, cw))],
        out_specs=[ffn] * 3,
        out_shape=[jax.ShapeDtypeStruct((seq, FFN_HIDDEN), BF16)] * 3,
        compiler_params=_cparams("parallel"),
    )(u2, w_gu4)


def down_loss(act, w_down, h1, final_w, target):
    seq = h1.shape[0]
    tm = ROW_TILE
    inv_d = 1.0 / D_MODEL

    def body(act_ref, w_ref, h1_ref, fw_ref, t_ref, dh2_ref, acc_ref):
        @pl.when(pl.program_id(0) == 0)
        def _():
            acc_ref[...] = jnp.zeros_like(acc_ref)

        h2 = h1_ref[...] + _dot(act_ref[...], w_ref[...])
        rf = _rms(h2)
        n = h2 * rf
        fw = fw_ref[...]
        err = n * fw - t_ref[...]
        dy = err * inv_d
        acc_ref[0:1, :] += jnp.sum(dy * n, axis=0, keepdims=True)
        acc_ref[1:2, :] += (0.5 * inv_d) * jnp.sum(err * err, axis=0, keepdims=True)
        dn = dy * fw
        dh2_ref[...] = rf * (dn - n * jnp.mean(dn * n, axis=-1, keepdims=True))

    wide = _rows(tm, D_MODEL)
    return pl.pallas_call(
        body, name="down_loss", grid=(seq // tm,),
        in_specs=[_rows(tm, FFN_HIDDEN), _full((FFN_HIDDEN, D_MODEL)), wide, _full((1, D_MODEL)), wide],
        out_specs=[wide, _full((8, D_MODEL))],
        out_shape=[jax.ShapeDtypeStruct((seq, D_MODEL), F32), jax.ShapeDtypeStruct((8, D_MODEL), F32)],
        compiler_params=_cparams("arbitrary"),
    )(act, w_down, h1, final_w, target)


def down_bwd(dh2, w_down, g, up):
    seq = dh2.shape[0]
    tm = ROW_TILE

    def body(dh_ref, w_ref, g_ref, up_ref, dgu_ref):
        dact = _dot_nt(dh_ref[...].astype(BF16), w_ref[...])
        gv = g_ref[...].astype(F32)
        sg = _sigmoid(gv)
        dgu_ref[:, :FFN_HIDDEN] = (dact * up_ref[...].astype(F32) * (sg * (1.0 + gv * (1.0 - sg)))).astype(BF16)
        dgu_ref[:, FFN_HIDDEN:] = (dact * (gv * sg)).astype(BF16)

    ffn = _rows(tm, FFN_HIDDEN)
    return pl.pallas_call(
        body, name="down_bwd", grid=(seq // tm,),
        in_specs=[_rows(tm, D_MODEL), _full((FFN_HIDDEN, D_MODEL)), ffn, ffn],
        out_specs=_rows(tm, 2 * FFN_HIDDEN),
        out_shape=jax.ShapeDtypeStruct((seq, 2 * FFN_HIDDEN), BF16),
        compiler_params=_cparams("parallel"),
    )(dh2, w_down, g, up)


def _head_sum_matrix():
    i = jnp.arange(ATTN_WIDTH)
    return ((i[:, None] // HEAD_DIM) == (i[None, :] // HEAD_DIM)).astype(BF16)


def gu_bwd(dgu, w_gu4, h1, norm2_w, dh2, w_out, ya):
    seq = h1.shape[0]
    tm = ROW_TILE
    cw = w_gu4.shape[2]
    hsum = _head_sum_matrix()

    def body(dgu_ref, w_ref, h1_ref, nw_ref, dh2_ref, wo_ref, ya_ref, hs_ref,
             dh1_ref, dya_ref, dyb_ref, delta_ref, acc_ref):
        @pl.when(pl.program_id(0) == 0)
        def _():
            acc_ref[...] = jnp.zeros_like(acc_ref)

        du2 = _dot_nt(dgu_ref[:, :cw], w_ref[0])
        for j in range(1, N_CHIPS):
            du2 = du2 + _dot_nt(dgu_ref[:, j * cw:(j + 1) * cw], w_ref[j])
        h1 = h1_ref[...]
        r2 = _rms(h1)
        nh = h1 * r2
        acc_ref[0:1, :] += jnp.sum(du2 * nh, axis=0, keepdims=True)
        dn = du2 * nw_ref[...]
        dh1 = dh2_ref[...] + r2 * (dn - nh * jnp.mean(dn * nh, axis=-1, keepdims=True))
        dh1_ref[...] = dh1
        dmixed = _dot_nt(dh1.astype(BF16), wo_ref[...])
        dya = dmixed[:, :ATTN_WIDTH]
        dya_ref[...] = dya
        dyb_ref[...] = dmixed[:, ATTN_WIDTH:]
        prod = dya * ya_ref[...]
        hi = prod.astype(BF16)
        lo = (prod - hi.astype(F32)).astype(BF16)
        delta_ref[...] = _dot(hi, hs_ref[...]) + _dot(lo, hs_ref[...])

    wide = _rows(tm, D_MODEL)
    half = _rows(tm, ATTN_WIDTH)
    return pl.pallas_call(
        body, name="gu_bwd", grid=(seq // tm,),
        in_specs=[_rows(tm, 2 * FFN_HIDDEN), _full((N_CHIPS, D_MODEL, cw)), wide, _full((1, D_MODEL)), wide,
                  _full((D_MODEL, D_MODEL)), half, _full((ATTN_WIDTH, ATTN_WIDTH))],
        out_specs=[wide, half, half, half, _full((8, D_MODEL))],
        out_shape=[jax.ShapeDtypeStruct((seq, D_MODEL), F32), jax.ShapeDtypeStruct((seq, ATTN_WIDTH), F32),
                   jax.ShapeDtypeStruct((seq, ATTN_WIDTH), F32), jax.ShapeDtypeStruct((seq, ATTN_WIDTH), F32),
                   jax.ShapeDtypeStruct((8, D_MODEL), F32)],
        compiler_params=_cparams("arbitrary"),
    )(dgu, w_gu4, h1, norm2_w, dh2, w_out, ya, hsum)


def attn_bwd(q, k, v, dy, lse, delta, dilation, name):
    seq = q.shape[0]
    span, per_step, cur, prev = _attn_specs(seq, dilation)
    whole = pl.BlockSpec((seq, LANES), lambda hp, j: (0, hp))

    def body(q_ref, dy_ref, lse_ref, dl_ref, kc_ref, vc_ref, kp_ref, vp_ref, dq_ref, dk_ref, dv_ref):
        first_step = pl.program_id(1) == 0
        base = pl.program_id(1) * ATTN_STEP_ROWS
        masks = _head_masks()

        def block(it, carry):
            t, r = it // dilation, it % dilation
            rows = _strided_rows(r + span * t, dilation)
            at_edge = t == 0
            q2, dy2 = q_ref[rows, :].astype(BF16), dy_ref[rows, :].astype(BF16)
            lse2, dl2 = lse_ref[rows, :], dl_ref[rows, :]
            kp = _block_before(kp_ref, kc_ref, t, r, span, dilation, per_step)
            vp = _block_before(vp_ref, vc_ref, t, r, span, dilation, per_step)
            k2 = jnp.concatenate([kp, kc_ref[rows, :]], axis=0).astype(BF16)
            v2 = jnp.concatenate([vp, vc_ref[rows, :]], axis=0).astype(BF16)
            valid = _window_valid(first_step & at_edge)
            zero = jnp.zeros_like(q2)
            qms, dyms, ps, dss, kms = [], [], [], [], []
            for h, mh in enumerate(masks):
                c0 = h * HEAD_DIM
                qm, dym = jnp.where(mh, q2, zero), jnp.where(mh, dy2, zero)
                s = _dot_nt(qm, k2) * ATTN_SCALE
                p = jnp.where(valid, jnp.exp(s - lse2[:, c0:c0 + 1]), 0.0)
                dp = _dot_nt(dym, v2)
                dss.append((p * (dp - dl2[:, c0:c0 + 1]) * ATTN_SCALE).astype(BF16))
                ps.append(p.astype(BF16))
                qms.append(qm)
                dyms.append(dym)
                kms.append(jnp.where(mh, k2, jnp.zeros_like(k2)))
            dq_ref[rows, :] = _dot(jnp.concatenate(dss, axis=1), jnp.concatenate(kms, axis=0))
            dv_full = _dot_tn(jnp.concatenate(ps, axis=0), jnp.concatenate(dyms, axis=0))
            dk_full = _dot_tn(jnp.concatenate(dss, axis=0), jnp.concatenate(qms, axis=0))
            here = _strided_rows(base + r + span * t, dilation)
            dk_ref[here, :] = dk_full[ATTN_BLOCK:]
            dv_ref[here, :] = dv_full[ATTN_BLOCK:]

            back = _strided_rows(jnp.maximum(base + r + span * t - span, r), dilation)
            dk_ref[back, :] += dk_full[:ATTN_BLOCK]
            dv_ref[back, :] += dv_full[:ATTN_BLOCK]
            return carry

        lax.fori_loop(0, per_step * dilation, block, 0, unroll=ATTN_BWD_UNROLL)

    return pl.pallas_call(
        body, name=name, grid=(ATTN_WIDTH // LANES, seq // ATTN_STEP_ROWS),
        in_specs=[cur] * 6 + [prev, prev],
        out_specs=[cur, whole, whole],
        out_shape=[jax.ShapeDtypeStruct((seq, ATTN_WIDTH), F32)] * 3,
        compiler_params=_cparams("parallel", "arbitrary"),
    )(q, dy, lse, delta, k, v, k, v)


def hgrn_bwd(hg, lb_logits, hnw, o_pre, st0, dyb):
    seq = hg.shape[0]
    nblk = seq // HGRN_ROWS
    n_chunks = HGRN_ROWS // HGRN_CHUNK
    w = HGRN_WIDTH

    def body(hg_ref, lbl_ref, hnw_ref, o_ref, st0_ref, dyb_ref, dhg_ref, acc_ref, dst_scr):
        @pl.when(pl.program_id(0) == 0)
        def _():
            dst_scr[...] = jnp.zeros_like(dst_scr)
            acc_ref[...] = jnp.zeros_like(acc_ref)

        hg_v = hg_ref[...]
        p = _hgrn_prep(hg_v, lbl_ref[...])
        tril, block = _chunk_masks()
        vv = hg_v[:, 2 * w:3 * w].astype(BF16)
        hnw_v = hnw_ref[...]
        gb = p["gb"]
        sgg = _sigmoid(gb)
        silu_g = gb * sgg
        dyb_v = dyb_ref[...]
        o_v = o_ref[...]

        d_on = dyb_v * hnw_v * silu_g
        on_parts, do_parts = [], []
        for h in range(HGRN_HEADS):
            sl = slice(h * LANES, (h + 1) * LANES)
            rs = _rms(o_v[:, sl])
            on = o_v[:, sl] * rs
            on_parts.append(on)
            do_parts.append(rs * (d_on[:, sl] - on * jnp.mean(d_on[:, sl] * on, axis=-1, keepdims=True)))
        on_all = jnp.concatenate(on_parts, axis=1)
        dgb = dyb_v * on_all * hnw_v * (sgg * (1.0 + gb * (1.0 - sgg)))
        acc_ref[0:1, :] += jnp.sum(dyb_v * on_all * silu_g, axis=0, keepdims=True)

        dqf_parts, dkk_parts, db_parts, dv_parts, dbt_parts, dkbkb_parts = [], [], [], [], [], []
        for h in range(HGRN_HEADS):
            sl = slice(h * LANES, (h + 1) * LANES)
            v_h = vv[:, sl]
            do_h = do_parts[h].astype(BF16)
            qi, ki, qs, kb = p["qi"][:, sl], p["ki"][:, sl], p["qs"][:, sl], p["kb"][:, sl]
            qi_b, ki_b = qi.astype(BF16), ki.astype(BF16)
            kb_cat = _spread(kb.astype(BF16), block)
            qs_cat = _spread(qs.astype(BF16), block)
            upd = _dot_tn(v_h, kb_cat)
            st = st0_ref[h]
            st_parts = []
            for n in range(n_chunks):
                st_parts.append(st)
                decay = p["e_tot"][n * HGRN_CHUNK:n * HGRN_CHUNK + 1, sl]
                st = st * decay + upd[:, n * LANES:(n + 1) * LANES]
            st_cat = jnp.concatenate([s_.astype(BF16) for s_ in st_parts], axis=1)
            wgt = _dot_tn(do_h, qs_cat)
            dst = dst_scr[h]
            dst_parts = [None] * n_chunks
            dbt_rows = [None] * n_chunks
            for n in reversed(range(n_chunks)):
                dst_parts[n] = dst.astype(BF16)
                decay = p["e_tot"][n * HGRN_CHUNK:n * HGRN_CHUNK + 1, sl]
                dbt_rows[n] = jnp.sum(dst * st_parts[n], axis=0, keepdims=True) * decay
                dst = dst * decay + wgt[:, n * LANES:(n + 1) * LANES]
            dst_scr[h] = dst
            dst_cat = jnp.concatenate(dst_parts, axis=1)
            dqs = _fold(_dot(do_h, st_cat), block)
            dkb = _fold(_dot(v_h, dst_cat), block)
            dv_state = _dot_nt(kb_cat, dst_cat)
            a = jnp.where(tril, _dot_nt(qi_b, ki_b), 0.0).astype(BF16)
            da = jnp.where(tril, _dot_nt(do_h, v_h), 0.0).astype(BF16)
            dv_parts.append(_dot_tn(a, do_h) + dv_state)
            dqi = _dot(da, ki_b)
            dki = _dot_tn(da, qi_b)
            dqf_parts.append(dqi * p["e_iq"][:, sl] + dqs * p["e_b"][:, sl])
            dkk_parts.append(dki * p["e_ik"][:, sl] + dkb * p["e_bb"][:, sl])
            dkbkb = dkb * kb
            db_parts.append(dqi * qi - dki * ki + dqs * qs - dkbkb)
            dkbkb_parts.append(dkbkb)
            dbt_parts.append(_chunk_rows(dbt_rows))

        cat = lambda parts: jnp.concatenate(parts, axis=1)
        dlogf = (_chunk_cumsum(cat(db_parts), reverse=True)
                 + _chunk_row(_chunk_cumsum(cat(dkbkb_parts)), HGRN_CHUNK - 1) + cat(dbt_parts))
        sq, qb = p["sq"], hg_v[:, :w]
        dqb = cat(dqf_parts) * (sq * (1.0 + qb * (1.0 - sq)))
        df = dlogf / p["f"] - cat(dkk_parts)
        sg, lb = p["sg"], p["lb"]
        dfb = df * (1.0 - lb) * sg * (1.0 - sg)
        acc_ref[1:2, :] += jnp.sum(df * (1.0 - sg), axis=0, keepdims=True)
        dhg_ref[...] = jnp.concatenate([dqb, dfb, cat(dv_parts), dgb], axis=1)

    rev = lambda i: (nblk - 1 - i, 0)
    return pl.pallas_call(
        body, name="hgrn_bwd", grid=(nblk,),
        in_specs=[pl.BlockSpec((HGRN_ROWS, 4 * w), rev), _full((2, w)), _full((1, w)),
                  pl.BlockSpec((HGRN_ROWS, w), rev),
                  pl.BlockSpec((None, HGRN_HEADS, LANES, LANES), lambda i: (nblk - 1 - i, 0, 0, 0)),
                  pl.BlockSpec((HGRN_ROWS, w), rev)],
        out_specs=[pl.BlockSpec((HGRN_ROWS, 4 * w), rev), _full((8, w))],
        out_shape=[jax.ShapeDtypeStruct((seq, 4 * w), F32), jax.ShapeDtypeStruct((8, w), F32)],
        scratch_shapes=[pltpu.VMEM((HGRN_HEADS, LANES, LANES), F32)],
        compiler_params=_cparams("arbitrary"),
    )(hg, lb_logits, hnw, o_pre, st0, dyb)


def in_bwd(dqs, dks, dvs, dhg, cos_t, sin_t, w_in4, x, norm1_w, dh1):
    seq = x.shape[0]
    tm = ROW_TILE
    cw = w_in4.shape[2]

    def body(dq1, dq2, dq3, dk1, dk2, dk3, dv1, dv2, dv3, dhg_ref, cos_ref, sin_ref, w_ref,
             x_ref, nw_ref, dh1_ref, dproj_ref, dx_ref, acc_ref):
        @pl.when(pl.program_id(0) == 0)
        def _():
            acc_ref[...] = jnp.zeros_like(acc_ref)

        cos, sin = cos_ref[...], sin_ref[...]
        dqa = _rotary_bwd(dq1[...] + dq2[...] + dq3[...], cos, sin)
        dka = _rotary_bwd(dk1[...] + dk2[...] + dk3[...], cos, sin)
        dva = dv1[...] + dv2[...] + dv3[...]
        dproj = jnp.concatenate([dqa, dka, dva, dhg_ref[...]], axis=1).astype(BF16)
        dproj_ref[...] = dproj
        du = _dot_nt(dproj[:, :cw], w_ref[0])
        for j in range(1, N_CHIPS):
            du = du + _dot_nt(dproj[:, j * cw:(j + 1) * cw], w_ref[j])
        xv = x_ref[...]
        r1 = _rms(xv)
        nx = xv * r1
        acc_ref[0:1, :] += jnp.sum(du * nx, axis=0, keepdims=True)
        dn = du * nw_ref[...]
        dx_ref[...] = dh1_ref[...] + r1 * (dn - nx * jnp.mean(dn * nx, axis=-1, keepdims=True))

    half = _rows(tm, ATTN_WIDTH)
    wide = _rows(tm, D_MODEL)
    return pl.pallas_call(
        body, name="in_bwd", grid=(seq // tm,),
        in_specs=[half] * 9 + [_rows(tm, 4 * HGRN_WIDTH), _rows(tm, LANES), _rows(tm, LANES),
                               _full((N_CHIPS, D_MODEL, cw)), wide, _full((1, D_MODEL)), wide],
        out_specs=[_rows(tm, IN_PROJ_WIDTH), wide, _full((8, D_MODEL))],
        out_shape=[jax.ShapeDtypeStruct((seq, IN_PROJ_WIDTH), BF16), jax.ShapeDtypeStruct((seq, D_MODEL), F32),
                   jax.ShapeDtypeStruct((8, D_MODEL), F32)],
        compiler_params=_cparams("arbitrary"),
    )(*dqs, *dks, *dvs, dhg, cos_t, sin_t, w_in4, x, norm1_w, dh1)


def weight_grad(a, b, col_block, name):
    seq, kdim = a.shape
    ndim = b.shape[1]
    nj = ndim // col_block
    tk = 512

    def body(a_ref, b_ref, o_ref):
        @pl.when(pl.program_id(1) == 0)
        def _():
            o_ref[...] = jnp.zeros_like(o_ref)

        o_ref[...] += _dot_tn(a_ref[...].astype(BF16), b_ref[...].astype(BF16))

    return pl.pallas_call(
        body, name=name, grid=(nj, seq // tk),
        in_specs=[pl.BlockSpec((tk, kdim), lambda j, t: (t, 0)),
                  pl.BlockSpec((tk, col_block), lambda j, t: (t, j))],
        out_specs=pl.BlockSpec((None, kdim, col_block), lambda j, t: (j, 0, 0)),
        out_shape=jax.ShapeDtypeStruct((nj, kdim, col_block), F32),
        compiler_params=_cparams("parallel", "arbitrary"),
    )(a, b)


def exchange_with_sibling(grads):
    n = len(grads)

    def body(*refs):
        g_refs, out_refs = refs[:n], refs[n:2 * n]
        send_sems, recv_sems = refs[2 * n], refs[2 * n + 1]
        x, y, cc = _mesh_pos()
        copies = []
        for i in range(n):
            for j in range(N_CHIPS):
                k = i * N_CHIPS + j
                copies.append(pltpu.make_async_remote_copy(
                    src_ref=g_refs[i].at[j, 1 - cc], dst_ref=out_refs[i].at[j],
                    send_sem=send_sems.at[k], recv_sem=recv_sems.at[k],
                    device_id=(x, y, 1 - cc), device_id_type=MESH_ID))
        for cp in copies:
            cp.start()
        for cp in copies:
            cp.wait_recv()
        for cp in copies:
            cp.wait_send()

    return pl.pallas_call(
        body, name="grad_exchange_sibling",
        in_specs=[pl.BlockSpec(memory_space=pl.ANY)] * n,
        out_specs=[pl.BlockSpec(memory_space=pl.ANY)] * n,
        out_shape=[jax.ShapeDtypeStruct((N_CHIPS,) + g.shape[2:], g.dtype) for g in grads],
        scratch_shapes=[pltpu.SemaphoreType.DMA((n * N_CHIPS,)), pltpu.SemaphoreType.DMA((n * N_CHIPS,))],
    )(*grads)


def add_own_half(grad, recv, name):
    _, _, r, c = grad.shape
    tr = r // 2 if r % 32 == 0 else r

    def body(cc_ref, g_ref, r_ref, o_ref):
        o_ref[...] = (g_ref[...] + r_ref[...]).astype(BF16)

    grid_spec = pltpu.PrefetchScalarGridSpec(
        num_scalar_prefetch=1, grid=(N_CHIPS, r // tr),
        in_specs=[pl.BlockSpec((None, None, tr, c), lambda j, t, cc: (j, cc[0], t, 0)),
                  pl.BlockSpec((None, tr, c), lambda j, t, cc: (j, t, 0))],
        out_specs=pl.BlockSpec((None, tr, c), lambda j, t, cc: (j, t, 0)))
    cc = lax.axis_index("c").astype(jnp.int32).reshape(1)
    return pl.pallas_call(
        body, name=name, grid_spec=grid_spec,
        out_shape=jax.ShapeDtypeStruct((N_CHIPS, r, c), BF16),
        compiler_params=_cparams("parallel", "parallel"),
    )(cc, grad, recv)


def exchange_between_chips(sums):
    n = len(sums)

    def body(*refs):
        s_refs, out_refs = refs[:n], refs[n:2 * n]
        send_sems, recv_sems = refs[2 * n], refs[2 * n + 1]
        x, y, cc = _mesh_pos()
        my_chip = 2 * x + y
        chips = [(1 - x, y), (x, 1 - y), (1 - x, 1 - y)]
        copies = []
        for i in range(n):
            for j, (px, py) in enumerate(chips):
                k = i * 3 + j
                copies.append(pltpu.make_async_remote_copy(
                    src_ref=s_refs[i].at[2 * px + py], dst_ref=out_refs[i].at[my_chip],
                    send_sem=send_sems.at[k], recv_sem=recv_sems.at[k],
                    device_id=(px, py, cc), device_id_type=MESH_ID))
        for cp in copies:
            cp.start()
        for i in range(n):
            for j, (px, py) in enumerate(chips):
                k = i * 3 + j
                pltpu.make_async_remote_copy(
                    src_ref=s_refs[i].at[my_chip], dst_ref=out_refs[i].at[2 * px + py],
                    send_sem=send_sems.at[k], recv_sem=recv_sems.at[k],
                    device_id=(px, py, cc), device_id_type=MESH_ID).wait_recv()
        for cp in copies:
            cp.wait_send()

    return pl.pallas_call(
        body, name="grad_exchange_chips",
        in_specs=[pl.BlockSpec(memory_space=pl.ANY)] * n,
        out_specs=[pl.BlockSpec(memory_space=pl.ANY)] * n,
        out_shape=[jax.ShapeDtypeStruct(s.shape, s.dtype) for s in sums],
        scratch_shapes=[pltpu.SemaphoreType.DMA((n * 3,)), pltpu.SemaphoreType.DMA((n * 3,))],
    )(*sums)


def sum_chips(sums, parts, name):
    _, r, c = parts.shape
    tr = r // 2 if r % 32 == 0 else r

    def body(idx_ref, s_ref, p1_ref, p2_ref, p3_ref, o_ref):
        o_ref[...] = ((s_ref[...].astype(F32) + p1_ref[...].astype(F32))
                      + p2_ref[...].astype(F32)) + p3_ref[...].astype(F32)

    def pick(k):
        return pl.BlockSpec((None, tr, c), lambda t, idx: (idx[k], t, 0))

    x, y = lax.axis_index("x"), lax.axis_index("y")
    idx = jnp.stack([2 * x + y, 2 * (1 - x) + y, 2 * x + (1 - y), 2 * (1 - x) + (1 - y)]).astype(jnp.int32)
    grid_spec = pltpu.PrefetchScalarGridSpec(
        num_scalar_prefetch=1, grid=(r // tr,),
        in_specs=[pick(0), pick(1), pick(2), pick(3)],
        out_specs=pl.BlockSpec((tr, c), lambda t, idx: (t, 0)))
    return pl.pallas_call(
        body, name=name, grid_spec=grid_spec,
        out_shape=jax.ShapeDtypeStruct((r, c), F32),
        compiler_params=_cparams("parallel"),
    )(idx, sums, parts, parts, parts)


def share_with_sibling(halves):
    n = len(halves)

    def body(*refs):
        h_refs, out_refs = refs[:n], refs[n:2 * n]
        send_sems, recv_sems = refs[2 * n], refs[2 * n + 1]
        x, y, cc = _mesh_pos()
        copies = [pltpu.make_async_remote_copy(
            src_ref=h_refs[i], dst_ref=out_refs[i],
            send_sem=send_sems.at[i], recv_sem=recv_sems.at[i],
            device_id=(x, y, 1 - cc), device_id_type=MESH_ID) for i in range(n)]
        for cp in copies:
            cp.start()
        for cp in copies:
            cp.wait_recv()
        for cp in copies:
            cp.wait_send()

    return pl.pallas_call(
        body, name="grad_share_sibling",
        in_specs=[pl.BlockSpec(memory_space=pl.ANY)] * n,
        out_specs=[pl.BlockSpec(memory_space=pl.ANY)] * n,
        out_shape=[jax.ShapeDtypeStruct(h.shape, h.dtype) for h in halves],
        scratch_shapes=[pltpu.SemaphoreType.DMA((n,)), pltpu.SemaphoreType.DMA((n,))],
    )(*halves)


def _adam_update(w, g, m, v):
    m = ADAM_B1 * m + (1.0 - ADAM_B1) * g
    v = ADAM_B2 * v + (1.0 - ADAM_B2) * (g * g)
    m_hat = m / (1.0 - ADAM_B1 ** ADAM_STEP)
    v_hat = v / (1.0 - ADAM_B2 ** ADAM_STEP)
    delta = -ADAM_LR * (m_hat / (jnp.sqrt(v_hat) + ADAM_EPS) + ADAM_WD * w)
    return delta, m, v


def adamw(w, g_mine, g_sibling, m, v, name):
    r, c = w.shape
    half = r // 2
    tr = half // 2 if half % 16 == 0 else half
    nt = half // tr

    def body(cc_ref, w_ref, ga_ref, gb_ref, m_ref, v_ref, g_ref, d_ref, nm_ref, nv_ref):
        g = jnp.where(pl.program_id(0) == cc_ref[0], ga_ref[...], gb_ref[...])
        g_ref[...] = g
        d, nm, nv = _adam_update(w_ref[...], g, m_ref[...], v_ref[...])
        d_ref[...] = d
        nm_ref[...] = nm
        nv_ref[...] = nv

    full = pl.BlockSpec((tr, c), lambda h, t, cc: (h * nt + t, 0))
    part = pl.BlockSpec((tr, c), lambda h, t, cc: (t, 0))
    grid_spec = pltpu.PrefetchScalarGridSpec(
        num_scalar_prefetch=1, grid=(2, nt),
        in_specs=[full, part, part, full, full], out_specs=[full] * 4)
    cc = lax.axis_index("c").astype(jnp.int32).reshape(1)
    return pl.pallas_call(
        body, name=name, grid_spec=grid_spec,
        out_shape=[jax.ShapeDtypeStruct((r, c), F32)] * 4,
        compiler_params=_cparams("parallel", "parallel"),
    )(cc, w, g_mine, g_sibling, m, v)


def small_allreduce(pack):
    def body(p_ref, o_ref, gather, send_sems, recv_sems):
        x, y, cc = _mesh_pos()
        me = 4 * x + 2 * y + cc
        gather[me] = p_ref[...]
        flips = [(fx, fy, fc) for fx in (0, 1) for fy in (0, 1) for fc in (0, 1)][1:]
        copies = []
        for k, (fx, fy, fc) in enumerate(flips):
            copies.append(pltpu.make_async_remote_copy(
                src_ref=p_ref, dst_ref=gather.at[me],
                send_sem=send_sems.at[k], recv_sem=recv_sems.at[k],
                device_id=(x ^ fx, y ^ fy, cc ^ fc), device_id_type=MESH_ID))
        for cp in copies:
            cp.start()
        for k, (fx, fy, fc) in enumerate(flips):
            src = 4 * (x ^ fx) + 2 * (y ^ fy) + (cc ^ fc)
            pltpu.make_async_remote_copy(
                src_ref=p_ref, dst_ref=gather.at[src],
                send_sem=send_sems.at[k], recv_sem=recv_sems.at[k],
                device_id=(x ^ fx, y ^ fy, cc ^ fc), device_id_type=MESH_ID).wait_recv()
        for cp in copies:
            cp.wait_send()
        total = gather[0]
        for d in range(1, N_DEV):
            total = total + gather[d]
        o_ref[...] = total

    return pl.pallas_call(
        body, name="small_allreduce",
        in_specs=[pl.BlockSpec(memory_space=pltpu.VMEM)],
        out_specs=pl.BlockSpec(memory_space=pltpu.VMEM),
        out_shape=jax.ShapeDtypeStruct(pack.shape, pack.dtype),
        scratch_shapes=[pltpu.VMEM((N_DEV,) + pack.shape, pack.dtype),
                        pltpu.SemaphoreType.DMA((7,)), pltpu.SemaphoreType.DMA((7,))],
    )(pack)


def small_update(gsum, wpack, mpack, vpack):
    hw = HGRN_WIDTH

    def body(g_ref, w_ref, m_ref, v_ref, go_ref, d_ref, nm_ref, nv_ref, loss_ref):
        g = g_ref[...]
        wv = w_ref[...]
        a0, a1 = wv[4:5, :hw], wv[4:5, hw:]
        mx = jnp.maximum(a0, a1)
        e0, e1 = jnp.exp(a0 - mx), jnp.exp(a1 - mx)
        lb = e0 / (e0 + e1)
        dl = g[4:5, :hw] * lb * (1.0 - lb)
        row = lax.broadcasted_iota(jnp.int32, g.shape, 0)
        lb_row = jnp.concatenate([dl, -dl], axis=1)
        grads = jnp.where(row == 4, lb_row, jnp.where(row < 4, g, 0.0))
        go_ref[...] = grads
        d, nm, nv = _adam_update(wv, grads, m_ref[...], v_ref[...])
        d_ref[...] = d
        nm_ref[...] = nm
        nv_ref[...] = nv
        loss_ref[...] = jnp.zeros((8, LANES), F32) + jnp.sum(g[5:6, :])

    vm = pl.BlockSpec(memory_space=pltpu.VMEM)
    return pl.pallas_call(
        body, name="small_update",
        in_specs=[vm] * 4, out_specs=[vm] * 5,
        out_shape=[jax.ShapeDtypeStruct(gsum.shape, F32)] * 4 + [jax.ShapeDtypeStruct((8, LANES), F32)],
    )(gsum, wpack, mpack, vpack)


def _pack_small(n1, n2, fn, hn, lbl):
    z = jnp.zeros((1, D_MODEL - HGRN_WIDTH), F32)
    rows = [n1.reshape(1, D_MODEL), n2.reshape(1, D_MODEL), fn.reshape(1, D_MODEL),
            jnp.concatenate([hn.reshape(1, HGRN_WIDTH), z], axis=1), lbl.reshape(1, 2 * HGRN_WIDTH),
            jnp.zeros((3, D_MODEL), F32)]
    return jnp.concatenate(rows, axis=0)


def _unpack_small(pack):
    return (pack[0:1], pack[4].reshape(2, HGRN_WIDTH), pack[3:4, :HGRN_WIDTH], pack[1:2], pack[2])


def kernel(x, norm1_w, w_in, lb_logits, hgrn_norm_w, w_out, norm2_w, w_gate_up, w_down, final_norm_w, loss_target, m_norm1_w, m_w_in, m_lb_logits, m_hgrn_norm_w, m_w_out, m_norm2_w, m_w_gate_up, m_w_down, m_final_norm_w, v_norm1_w, v_w_in, v_lb_logits, v_hgrn_norm_w, v_w_out, v_norm2_w, v_w_gate_up, v_w_down, v_final_norm_w):
    seq = x.shape[1]
    xs = x.reshape(seq, D_MODEL)
    target = loss_target.reshape(seq, D_MODEL)
    shards = {"w_in": w_in[0], "w_out": w_out[0], "w_gu": w_gate_up[0], "w_down": w_down[0]}

    gathered = {k: allgather_halves(cast_bf16(w, "cast_" + k), "gather_" + k) for k, w in shards.items()}
    w_in4 = gathered["w_in"].reshape(N_CHIPS, D_MODEL, -1)
    w_out_f = gathered["w_out"].reshape(D_MODEL, D_MODEL)
    w_gu4 = gathered["w_gu"].reshape(N_CHIPS, D_MODEL, -1)
    w_down_f = gathered["w_down"].reshape(FFN_HIDDEN, D_MODEL)

    cos_t, sin_t = _rope_tables(seq)
    fw = final_norm_w.reshape(1, D_MODEL)

    qr, kr, va, hg, u = in_proj(xs, norm1_w, w_in4, cos_t, sin_t)
    fwd = [attn_fwd(qr, kr, va, d, "attn_fwd_d%d" % d) for _, d in DILATED_PAIRS]
    yb, o_pre, st0 = hgrn_fwd(hg, lb_logits, hgrn_norm_w)
    ya, lse, mixed, h1, u2 = mix_out([f[0] for f in fwd], [f[1] for f in fwd], yb, xs, w_out_f, norm2_w)
    g, up, act = gate_up(u2, w_gu4)
    dh2, acc_fin = down_loss(act, w_down_f, h1, fw, target)

    dgu = down_bwd(dh2, w_down_f, g, up)
    dh1, dya, dyb, delta, acc_n2 = gu_bwd(dgu, w_gu4, h1, norm2_w, dh2, w_out_f, ya)
    bwd = [attn_bwd(qr, kr, va, dya, lse, delta, d, "attn_bwd_d%d" % d) for _, d in DILATED_PAIRS]
    dhg, acc_hg = hgrn_bwd(hg, lb_logits, hgrn_norm_w, o_pre, st0, dyb)
    dproj, dx, acc_n1 = in_bwd([b[0] for b in bwd], [b[1] for b in bwd], [b[2] for b in bwd],
                               dhg, cos_t, sin_t, w_in4, xs, norm1_w, dh1)

    cw_in, cw_gu = w_in4.shape[2], w_gu4.shape[2]
    grads = [
        weight_grad(u, dproj, cw_in, "wgrad_in").reshape(N_CHIPS, 2, D_MODEL // 2, cw_in),
        weight_grad(mixed, dh1, D_MODEL, "wgrad_out").reshape(N_CHIPS, 2, D_MODEL // 8, D_MODEL),
        weight_grad(u2, dgu, cw_gu, "wgrad_gu").reshape(N_CHIPS, 2, D_MODEL // 2, cw_gu),
        weight_grad(act, dh2, D_MODEL, "wgrad_down").reshape(N_CHIPS, 2, FFN_HIDDEN // 8, D_MODEL),
    ]
    names = ["in", "out", "gu", "down"]
    recv = exchange_with_sibling(grads)
    sums = [add_own_half(gr, rc, "add_half_" + nm) for gr, rc, nm in zip(grads, recv, names)]
    parts = exchange_between_chips(sums)
    halves = [sum_chips(s, p, "sum_chips_" + nm) for s, p, nm in zip(sums, parts, names)]
    others = share_with_sibling(halves)
    big = {}
    for nm, key, mine, other, m_, v_ in zip(names, ["w_in", "w_out", "w_gu", "w_down"], halves, others,
                                            [m_w_in, m_w_out, m_w_gate_up, m_w_down],
                                            [v_w_in, v_w_out, v_w_gate_up, v_w_down]):
        big[key] = tuple(t[None] for t in adamw(shards[key], mine, other, m_[0], v_[0], "adamw_" + nm))

    z512 = jnp.zeros((1, D_MODEL - HGRN_WIDTH), F32)
    gpack = jnp.concatenate([
        acc_n1[0:1], acc_n2[0:1], acc_fin[0:1],
        jnp.concatenate([acc_hg[0:1], z512], axis=1), jnp.concatenate([acc_hg[1:2], z512], axis=1),
        acc_fin[1:2], jnp.zeros((2, D_MODEL), F32)], axis=0)
    gsum = small_allreduce(gpack)
    wpack = _pack_small(norm1_w, norm2_w, final_norm_w, hgrn_norm_w, lb_logits)
    mpack = _pack_small(m_norm1_w, m_norm2_w, m_final_norm_w, m_hgrn_norm_w, m_lb_logits)
    vpack = _pack_small(v_norm1_w, v_norm2_w, v_final_norm_w, v_hgrn_norm_w, v_lb_logits)
    gs, ds, nms, nvs, loss8 = small_update(gsum, wpack, mpack, vpack)
    loss = loss8[0, 0]

    def assemble(small_pack, idx):
        n1, lbl, hn, n2, fn = _unpack_small(small_pack)
        return (n1, big["w_in"][idx], lbl, hn, big["w_out"][idx], n2, big["w_gu"][idx], big["w_down"][idx], fn)

    return (loss, dx.reshape(x.shape), *assemble(gs, 0), *assemble(ds, 1), *assemble(nms, 2), *assemble(nvs, 3))
```

```python
import functools

import jax
import jax.numpy as jnp
from jax import lax
from jax.experimental import pallas as pl
from jax.experimental.pallas import tpu as pltpu

F32 = jnp.float32
BF16 = jnp.bfloat16

D_MODEL = 1024
ATTN_WIDTH = 512
HEAD_DIM = 64
DILATED_PAIRS = ((128, 1), (512, 4), (2048, 16))
ATTN_BLOCK = 128
ROPE_THETA = 10000.0
HGRN_WIDTH = 512
HGRN_CHUNK = 16
HGRN_HEADS = 4
IN_PROJ_WIDTH = 3584
FFN_HIDDEN = 2816
NORM_EPS = 1e-6
ATTN_SCALE = HEAD_DIM ** -0.5
N_CHIPS = 4
N_DEV = 8

ADAM_LR = 0.001
ADAM_B1 = 0.9
ADAM_B2 = 0.999
ADAM_EPS = 1e-08
ADAM_WD = 0.01
ADAM_STEP = 10

LANES = 128
HGRN_ROWS = 128
ROW_TILE = 256
ATTN_STEP_ROWS = 2048
ATTN_FWD_UNROLL = 8
ATTN_BWD_UNROLL = 8
VMEM_LIMIT = 56 * 1024 * 1024
NEG_BIG = -1e30
MESH_ID = pl.DeviceIdType.MESH


def _cparams(*sem):
    return pltpu.CompilerParams(dimension_semantics=tuple(sem), vmem_limit_bytes=VMEM_LIMIT)


def _dot(a, b):
    return jnp.dot(a, b, preferred_element_type=F32)


def _dot_nt(a, b):
    return lax.dot_general(a, b, (((1,), (1,)), ((), ())), preferred_element_type=F32)


def _dot_tn(a, b):
    return lax.dot_general(a, b, (((0,), (0,)), ((), ())), preferred_element_type=F32)


def _sigmoid(x):
    return 1.0 / (1.0 + jnp.exp(-x))


def _full(shape):
    n = len(shape)
    return pl.BlockSpec(shape, lambda *_: (0,) * n)


def _rows(tm, width):
    return pl.BlockSpec((tm, width), lambda i: (i, 0))


def _swap32(x):
    lane = lax.broadcasted_iota(jnp.int32, x.shape, 1)
    first = (lane % HEAD_DIM) < (HEAD_DIM // 2)
    return jnp.where(first, pltpu.roll(x, LANES - 32, axis=1), pltpu.roll(x, 32, axis=1))


def _rotary_fwd(x, cos, sin_signed):
    parts = []
    for j in range(x.shape[1] // LANES):
        xc = x[:, j * LANES:(j + 1) * LANES]
        parts.append(xc * cos + _swap32(xc) * sin_signed)
    return jnp.concatenate(parts, axis=1)


def _rotary_bwd(dy, cos, sin_signed):
    parts = []
    for j in range(dy.shape[1] // LANES):
        dc = dy[:, j * LANES:(j + 1) * LANES]
        parts.append(dc * cos + _swap32(dc * sin_signed))
    return jnp.concatenate(parts, axis=1)


def _rope_tables(seq):
    half = HEAD_DIM // 2
    inv_freq = ROPE_THETA ** (-jnp.arange(half, dtype=F32) / half)
    ang = jnp.arange(seq, dtype=F32)[:, None] * inv_freq[None, :]
    cos, sin = jnp.cos(ang), jnp.sin(ang)
    cos_t = jnp.tile(cos, (1, LANES // half))
    sin_t = jnp.tile(jnp.concatenate([-sin, sin], axis=1), (1, LANES // HEAD_DIM))
    return cos_t, sin_t


def cast_bf16(w, name):
    r, c = w.shape
    half = r // 2

    def body(w_ref, o_ref):
        o_ref[...] = w_ref[...].astype(BF16)

    return pl.pallas_call(
        body, name=name, grid=(2,),
        in_specs=[pl.BlockSpec((half, c), lambda i: (i, 0))],
        out_specs=pl.BlockSpec((None, half, c), lambda i: (i, 0, 0)),
        out_shape=jax.ShapeDtypeStruct((2, half, c), BF16),
        compiler_params=_cparams("parallel"),
    )(w)


def _mesh_pos():
    return lax.axis_index("x"), lax.axis_index("y"), lax.axis_index("c")


GATHER_COPIES = 7


def _gather_phases(x_refs, out_refs, send_sems, recv_sems, local_sems):
    n = len(x_refs)
    x, y, cc = _mesh_pos()
    me, sibling = (x, y, cc), (x, y, 1 - cc)
    chips = [(1 - x, y), (x, 1 - y), (1 - x, 1 - y)]

    def rows(i, px, py, pc):
        return out_refs[i].at[4 * px + 2 * py + pc]

    def copy(i, k, block, to, src=None):
        return pltpu.make_async_remote_copy(
            src_ref=rows(i, *block) if src is None else src, dst_ref=rows(i, *block),
            send_sem=send_sems.at[GATHER_COPIES * i + k], recv_sem=recv_sems.at[GATHER_COPIES * i + k],
            device_id=to, device_id_type=MESH_ID)

    def local(i):
        return pltpu.make_async_copy(x_refs[i].at[cc], rows(i, *me), local_sems.at[i])

    def first(i):
        mine = x_refs[i].at[cc]
        return [copy(i, 0, me, sibling, src=mine)] + [
            copy(i, 1 + j, me, (*chip, cc), src=mine) for j, chip in enumerate(chips)]

    def passed(i):
        return [copy(i, 4 + j, (*chip, cc), sibling) for j, chip in enumerate(chips)]

    def start():
        for i in range(n):
            local(i).start()
            for cp in first(i):
                cp.start()

    def forward():
        for i in range(n):
            onward = passed(i)
            for j, chip in enumerate(chips):
                copy(i, 1 + j, (*chip, cc), me).wait_recv()
                onward[j].start()

    def finish():
        for i in range(n):
            copy(i, 0, sibling, me).wait_recv()
            for j, chip in enumerate(chips):
                copy(i, 4 + j, (*chip, 1 - cc), me).wait_recv()
            for cp in first(i) + passed(i):
                cp.wait_send()
            local(i).wait()

    return start, forward, finish


def _gather_scratch(n):
    return [pltpu.SemaphoreType.DMA((GATHER_COPIES * n,)), pltpu.SemaphoreType.DMA((GATHER_COPIES * n,)),
            pltpu.SemaphoreType.DMA((n,))]


def _gathered_shape(halves):
    return jax.ShapeDtypeStruct((N_DEV,) + halves.shape[1:], halves.dtype)


def allgather_halves(halves, name):
    def body(x_ref, out_ref, send_sems, recv_sems, local_sems):
        start, forward, finish = _gather_phases([x_ref], [out_ref], send_sems, recv_sems, local_sems)
        start()
        forward()
        finish()

    return pl.pallas_call(
        body, name=name,
        in_specs=[pl.BlockSpec(memory_space=pl.ANY)],
        out_specs=pl.BlockSpec(memory_space=pl.ANY),
        out_shape=_gathered_shape(halves),
        scratch_shapes=_gather_scratch(1),
    )(halves)


def _rms(x):
    return lax.rsqrt(jnp.mean(x * x, axis=-1, keepdims=True) + NORM_EPS)


def in_proj(x, norm1_w, w_in4, cos_t, sin_t):
    seq = x.shape[0]
    tm = ROW_TILE
    cw = w_in4.shape[2]

    def body(x_ref, nw_ref, w_ref, cos_ref, sin_ref, q_ref, k_ref, v_ref, hg_ref, u_ref):
        xv = x_ref[...]
        u = ((xv * _rms(xv)) * nw_ref[...]).astype(BF16)
        u_ref[...] = u
        proj = jnp.concatenate([_dot(u, w_ref[j]) for j in range(N_CHIPS)], axis=1)
        cos, sin = cos_ref[...], sin_ref[...]
        a = ATTN_WIDTH
        q_ref[...] = _rotary_fwd(proj[:, :a], cos, sin)
        k_ref[...] = _rotary_fwd(proj[:, a:2 * a], cos, sin)
        v_ref[...] = proj[:, 2 * a:3 * a]
        hg_ref[...] = proj[:, 3 * a:]

    return pl.pallas_call(
        body, name="in_proj", grid=(seq // tm,),
        in_specs=[_rows(tm, D_MODEL), _full((1, D_MODEL)), _full((N_CHIPS, D_MODEL, cw)),
                  _rows(tm, LANES), _rows(tm, LANES)],
        out_specs=[_rows(tm, ATTN_WIDTH)] * 3 + [_rows(tm, 4 * HGRN_WIDTH), _rows(tm, D_MODEL)],
        out_shape=[jax.ShapeDtypeStruct((seq, ATTN_WIDTH), F32)] * 3
        + [jax.ShapeDtypeStruct((seq, 4 * HGRN_WIDTH), F32), jax.ShapeDtypeStruct((seq, D_MODEL), BF16)],
        compiler_params=_cparams("parallel"),
    )(x, norm1_w, w_in4, cos_t, sin_t)


def _head_masks():
    lane = lax.broadcasted_iota(jnp.int32, (1, LANES), 1)
    return [(lane // HEAD_DIM) == h for h in range(LANES // HEAD_DIM)]


def _window_valid(no_prev):
    qi = lax.broadcasted_iota(jnp.int32, (ATTN_BLOCK, 2 * ATTN_BLOCK), 0)
    kj = lax.broadcasted_iota(jnp.int32, (ATTN_BLOCK, 2 * ATTN_BLOCK), 1)
    valid = (kj >= qi) & (kj <= qi + ATTN_BLOCK)
    return valid & (jnp.logical_not(no_prev) | (kj >= ATTN_BLOCK))


def _strided_rows(start, dilation):
    if dilation == 1:
        return pl.ds(start, ATTN_BLOCK)
    return pl.ds(start, ATTN_BLOCK, stride=dilation)


def _block_before(edge_ref, cur_ref, t, r, span, dilation, per_step):
    edge = edge_ref[_strided_rows(r, dilation), :]
    if per_step == 1:
        return edge
    inside = cur_ref[_strided_rows(r + span * jnp.maximum(t - 1, 0), dilation), :]
    return jnp.where(t == 0, edge, inside)


def _attn_specs(seq, dilation):
    span = ATTN_BLOCK * dilation
    per_step = ATTN_STEP_ROWS // span
    cur = pl.BlockSpec((ATTN_STEP_ROWS, LANES), lambda hp, j: (j, hp))
    prev = pl.BlockSpec((span, LANES), lambda hp, j: (jnp.maximum(j * per_step - 1, 0), hp))
    return span, per_step, cur, prev


def attn_fwd(q, k, v, dilation, name):
    seq = q.shape[0]
    span, per_step, cur, prev = _attn_specs(seq, dilation)

    def body(q_ref, kc_ref, vc_ref, kp_ref, vp_ref, o_ref, lse_ref):
        first_step = pl.program_id(1) == 0
        masks = _head_masks()

        def block(it, carry):
            t, r = it // dilation, it % dilation
            rows = _strided_rows(r + span * t, dilation)
            at_edge = t == 0
            q2 = q_ref[rows, :].astype(BF16)
            kp = _block_before(kp_ref, kc_ref, t, r, span, dilation, per_step)
            vp = _block_before(vp_ref, vc_ref, t, r, span, dilation, per_step)
            k2 = jnp.concatenate([kp, kc_ref[rows, :]], axis=0).astype(BF16)
            v2 = jnp.concatenate([vp, vc_ref[rows, :]], axis=0).astype(BF16)
            valid = _window_valid(first_step & at_edge)
            o_acc = jnp.zeros((ATTN_BLOCK, LANES), F32)
            l_acc = jnp.zeros((ATTN_BLOCK, LANES), F32)
            for mh in masks:
                qm = jnp.where(mh, q2, jnp.zeros_like(q2))
                s = jnp.where(valid, _dot_nt(qm, k2) * ATTN_SCALE, NEG_BIG)
                m = jnp.max(s, axis=-1, keepdims=True)
                p = jnp.exp(s - m)
                l = jnp.sum(p, axis=-1, keepdims=True)
                o = _dot(p.astype(BF16), v2) / l
                o_acc = jnp.where(mh, o, o_acc)
                l_acc = jnp.where(mh, m + jnp.log(l), l_acc)
            o_ref[rows, :] = o_acc
            lse_ref[rows, :] = l_acc
            return carry

        lax.fori_loop(0, per_step * dilation, block, 0, unroll=ATTN_FWD_UNROLL)

    return pl.pallas_call(
        body, name=name, grid=(ATTN_WIDTH // LANES, seq // ATTN_STEP_ROWS),
        in_specs=[cur, cur, cur, prev, prev],
        out_specs=[cur, cur],
        out_shape=[jax.ShapeDtypeStruct((seq, ATTN_WIDTH), F32)] * 2,
        compiler_params=_cparams("parallel", "parallel"),
    )(q, k, v, k, v)


def _chunk_cumsum(x, reverse=False):
    rc = lax.broadcasted_iota(jnp.int32, x.shape, 0) % HGRN_CHUNK
    sh = 1
    while sh < HGRN_CHUNK:
        if reverse:
            x = x + jnp.where(rc + sh < HGRN_CHUNK, pltpu.roll(x, x.shape[0] - sh, axis=0), 0.0)
        else:
            x = x + jnp.where(rc >= sh, pltpu.roll(x, sh, axis=0), 0.0)
        sh *= 2
    return x


def _chunk_row(x, row):
    return _chunk_rows([x[n * HGRN_CHUNK + row:n * HGRN_CHUNK + row + 1, :]
                        for n in range(x.shape[0] // HGRN_CHUNK)])


def _chunk_rows(rows):
    return jnp.concatenate([jnp.broadcast_to(r, (HGRN_CHUNK, r.shape[1])) for r in rows], axis=0)


def _hgrn_prep(hg, lbl):
    w = HGRN_WIDTH
    a0, a1 = lbl[0:1, :], lbl[1:2, :]
    mx = jnp.maximum(a0, a1)
    e0, e1 = jnp.exp(a0 - mx), jnp.exp(a1 - mx)
    lb = e0 / (e0 + e1)
    qb, fb, gb = hg[:, :w], hg[:, w:2 * w], hg[:, 3 * w:]
    sg = _sigmoid(fb)
    f = lb + (1.0 - lb) * sg
    b = _chunk_cumsum(jnp.log(f))
    bmid, btot = _chunk_row(b, HGRN_CHUNK // 2 - 1), _chunk_row(b, HGRN_CHUNK - 1)
    sq = _sigmoid(qb)
    p = dict(lb=lb, sg=sg, f=f, kk=1.0 - f, sq=sq, qf=qb * sq, gb=gb,
             e_iq=jnp.exp(b - bmid), e_ik=jnp.exp(bmid - b), e_b=jnp.exp(b),
             e_bb=jnp.exp(btot - b), e_tot=jnp.exp(btot))
    p["qi"] = p["qf"] * p["e_iq"]
    p["ki"] = p["kk"] * p["e_ik"]
    p["qs"] = p["qf"] * p["e_b"]
    p["kb"] = p["kk"] * p["e_bb"]
    return p


def _chunk_masks():
    t = lax.broadcasted_iota(jnp.int32, (HGRN_ROWS, HGRN_ROWS), 0)
    s = lax.broadcasted_iota(jnp.int32, (HGRN_ROWS, HGRN_ROWS), 1)
    tril = ((t // HGRN_CHUNK) == (s // HGRN_CHUNK)) & (s <= t)
    n_chunks = HGRN_ROWS // HGRN_CHUNK
    tt = lax.broadcasted_iota(jnp.int32, (HGRN_ROWS, n_chunks * LANES), 0)
    cc = lax.broadcasted_iota(jnp.int32, (HGRN_ROWS, n_chunks * LANES), 1)
    block = (tt // HGRN_CHUNK) == (cc // LANES)
    return tril, block


def _spread(x, block):
    n_chunks = HGRN_ROWS // HGRN_CHUNK
    return jnp.where(block, jnp.tile(x, (1, n_chunks)), jnp.zeros((), x.dtype))


def _fold(x_full, block):
    n_chunks = HGRN_ROWS // HGRN_CHUNK
    z = jnp.where(block, x_full, 0.0)
    acc = z[:, :LANES]
    for n in range(1, n_chunks):
        acc = acc + z[:, n * LANES:(n + 1) * LANES]
    return acc


def hgrn_fwd(hg, lb_logits, hnw, weight_halves=()):
    seq = hg.shape[0]
    nblk = seq // HGRN_ROWS
    n_chunks = HGRN_ROWS // HGRN_CHUNK
    n_w = len(weight_halves)

    def body(*refs):
        hg_ref, lbl_ref, hnw_ref = refs[:3]
        w_refs = refs[3:3 + n_w]
        yb_ref, o_ref, st0_ref = refs[3 + n_w:6 + n_w]
        g_refs = refs[6 + n_w:6 + 2 * n_w]
        st_scr = refs[6 + 2 * n_w]
        step = pl.program_id(0)
        if n_w:
            start, forward, finish = _gather_phases(w_refs, g_refs, *refs[7 + 2 * n_w:])
            pl.when(step == 0)(start)
            pl.when(step == (3 * nblk) // 4)(forward)

        @pl.when(step == 0)
        def _():
            st_scr[...] = jnp.zeros_like(st_scr)

        hg_v = hg_ref[...]
        p = _hgrn_prep(hg_v, lbl_ref[...])
        tril, block = _chunk_masks()
        vv = hg_v[:, 2 * HGRN_WIDTH:3 * HGRN_WIDTH].astype(BF16)
        outs = []
        for h in range(HGRN_HEADS):
            sl = slice(h * LANES, (h + 1) * LANES)
            v_h = vv[:, sl]
            a = jnp.where(tril, _dot_nt(p["qi"][:, sl].astype(BF16), p["ki"][:, sl].astype(BF16)), 0.0)
            o = _dot(a.astype(BF16), v_h)
            upd = _dot_tn(v_h, _spread(p["kb"][:, sl].astype(BF16), block))
            st = st_scr[h]
            st0_ref[h] = st
            parts = []
            for n in range(n_chunks):
                parts.append(st.astype(BF16))
                decay = p["e_tot"][n * HGRN_CHUNK:n * HGRN_CHUNK + 1, sl]
                st = st * decay + upd[:, n * LANES:(n + 1) * LANES]
            st_scr[h] = st
            o = o + _dot_nt(_spread(p["qs"][:, sl].astype(BF16), block), jnp.concatenate(parts, axis=1))
            outs.append(o)
        o_all = jnp.concatenate(outs, axis=1)
        o_ref[...] = o_all
        normed = jnp.concatenate(
            [outs[h] * _rms(outs[h]) for h in range(HGRN_HEADS)], axis=1)
        gb = p["gb"]
        yb_ref[...] = (normed * hnw_ref[...]) * (gb * _sigmoid(gb))
        if n_w:
            pl.when(step == nblk - 1)(finish)

    anywhere = pl.BlockSpec(memory_space=pl.ANY)
    return pl.pallas_call(
        body, name="hgrn_fwd", grid=(nblk,),
        in_specs=[_rows(HGRN_ROWS, 4 * HGRN_WIDTH), _full((2, HGRN_WIDTH)), _full((1, HGRN_WIDTH))]
        + [anywhere] * n_w,
        out_specs=[_rows(HGRN_ROWS, HGRN_WIDTH), _rows(HGRN_ROWS, HGRN_WIDTH),
                   pl.BlockSpec((None, HGRN_HEADS, LANES, LANES), lambda i: (i, 0, 0, 0))] + [anywhere] * n_w,
        out_shape=[jax.ShapeDtypeStruct((seq, HGRN_WIDTH), F32)] * 2
        + [jax.ShapeDtypeStruct((nblk, HGRN_HEADS, LANES, LANES), F32)]
        + [_gathered_shape(h) for h in weight_halves],
        scratch_shapes=[pltpu.VMEM((HGRN_HEADS, LANES, LANES), F32)] + (_gather_scratch(n_w) if n_w else []),
        compiler_params=_cparams("arbitrary"),
    )(hg, lb_logits, hnw, *weight_halves)


def mix_out(outs, lses, yb, x, w_out, norm2_w):
    seq = x.shape[0]
    tm = ROW_TILE

    def body(o1, o2, o3, l1, l2, l3, yb_ref, x_ref, w_ref, nw_ref,
             ya_ref, lse_ref, mixed_ref, h1_ref, u2_ref):
        l1v, l2v, l3v = l1[...], l2[...], l3[...]
        mx = jnp.maximum(jnp.maximum(l1v, l2v), l3v)
        e1, e2, e3 = jnp.exp(l1v - mx), jnp.exp(l2v - mx), jnp.exp(l3v - mx)
        den = e1 + e2 + e3
        ya = (e1 * o1[...] + e2 * o2[...] + e3 * o3[...]) / den
        ya_ref[...] = ya
        lse_ref[...] = mx + jnp.log(den)
        mixed = jnp.concatenate([ya, yb_ref[...]], axis=1).astype(BF16)
        mixed_ref[...] = mixed
        h1 = x_ref[...] + _dot(mixed, w_ref[...])
        h1_ref[...] = h1
        u2_ref[...] = ((h1 * _rms(h1)) * nw_ref[...]).astype(BF16)

    half = _rows(tm, ATTN_WIDTH)
    wide = _rows(tm, D_MODEL)
    return pl.pallas_call(
        body, name="mix_out", grid=(seq // tm,),
        in_specs=[half] * 7 + [wide, _full((D_MODEL, D_MODEL)), _full((1, D_MODEL))],
        out_specs=[half, half, wide, wide, wide],
        out_shape=[jax.ShapeDtypeStruct((seq, ATTN_WIDTH), F32)] * 2
        + [jax.ShapeDtypeStruct((seq, D_MODEL), BF16), jax.ShapeDtypeStruct((seq, D_MODEL), F32),
           jax.ShapeDtypeStruct((seq, D_MODEL), BF16)],
        compiler_params=_cparams("parallel"),
    )(*outs, *lses, yb, x, w_out, norm2_w)


def gate_up(u2, w_gu4):
    seq = u2.shape[0]
    tm = ROW_TILE
    cw = w_gu4.shape[2]

    def body(u_ref, w_ref, g_ref, up_ref, act_ref):
        u = u_ref[...]
        g = jnp.concatenate([_dot(u, w_ref[0]), _dot(u, w_ref[1])], axis=1)
        up = jnp.concatenate([_dot(u, w_ref[2]), _dot(u, w_ref[3])], axis=1)
        g_ref[...] = g.astype(BF16)
        up_ref[...] = up.astype(BF16)
        act_ref[...] = ((g * _sigmoid(g)) * up).astype(BF16)

    ffn = _rows(tm, FFN_HIDDEN)
    return pl.pallas_call(
        body, name="gate_up", grid=(seq // tm,),
        in_specs=[_rows(tm, D_MODEL), _full((N_CHIPS, D_MODEL, cw))],
        out_specs=[ffn] * 3,
        out_shape=[jax.ShapeDtypeStruct((seq, FFN_HIDDEN), BF16)] * 3,
        compiler_params=_cparams("parallel"),
    )(u2, w_gu4)


def down_loss(act, w_down, h1, final_w, target):
    seq = h1.shape[0]
    tm = ROW_TILE
    inv_d = 1.0 / D_MODEL

    def body(act_ref, w_ref, h1_ref, fw_ref, t_ref, dh2_ref, acc_ref):
        @pl.when(pl.program_id(0) == 0)
        def _():
            acc_ref[...] = jnp.zeros_like(acc_ref)

        h2 = h1_ref[...] + _dot(act_ref[...], w_ref[...])
        rf = _rms(h2)
        n = h2 * rf
        fw = fw_ref[...]
        err = n * fw - t_ref[...]
        dy = err * inv_d
        acc_ref[0:1, :] += jnp.sum(dy * n, axis=0, keepdims=True)
        acc_ref[1:2, :] += (0.5 * inv_d) * jnp.sum(err * err, axis=0, keepdims=True)
        dn = dy * fw
        dh2_ref[...] = rf * (dn - n * jnp.mean(dn * n, axis=-1, keepdims=True))

    wide = _rows(tm, D_MODEL)
    return pl.pallas_call(
        body, name="down_loss", grid=(seq // tm,),
        in_specs=[_rows(tm, FFN_HIDDEN), _full((FFN_HIDDEN, D_MODEL)), wide, _full((1, D_MODEL)), wide],
        out_specs=[wide, _full((8, D_MODEL))],
        out_shape=[jax.ShapeDtypeStruct((seq, D_MODEL), F32), jax.ShapeDtypeStruct((8, D_MODEL), F32)],
        compiler_params=_cparams("arbitrary"),
    )(act, w_down, h1, final_w, target)


def down_bwd(dh2, w_down, g, up):
    seq = dh2.shape[0]
    tm = ROW_TILE

    def body(dh_ref, w_ref, g_ref, up_ref, dgu_ref):
        dact = _dot_nt(dh_ref[...].astype(BF16), w_ref[...])
        gv = g_ref[...].astype(F32)
        sg = _sigmoid(gv)
        dgu_ref[:, :FFN_HIDDEN] = (dact * up_ref[...].astype(F32) * (sg * (1.0 + gv * (1.0 - sg)))).astype(BF16)
        dgu_ref[:, FFN_HIDDEN:] = (dact * (gv * sg)).astype(BF16)

    ffn = _rows(tm, FFN_HIDDEN)
    return pl.pallas_call(
        body, name="down_bwd", grid=(seq // tm,),
        in_specs=[_rows(tm, D_MODEL), _full((FFN_HIDDEN, D_MODEL)), ffn, ffn],
        out_specs=_rows(tm, 2 * FFN_HIDDEN),
        out_shape=jax.ShapeDtypeStruct((seq, 2 * FFN_HIDDEN), BF16),
        compiler_params=_cparams("parallel"),
    )(dh2, w_down, g, up)


def _head_sum_matrix():
    i = jnp.arange(ATTN_WIDTH)
    return ((i[:, None] // HEAD_DIM) == (i[None, :] // HEAD_DIM)).astype(BF16)


def gu_bwd(dgu, w_gu4, h1, norm2_w, dh2, w_out, ya):
    seq = h1.shape[0]
    tm = ROW_TILE
    cw = w_gu4.shape[2]
    hsum = _head_sum_matrix()

    def body(dgu_ref, w_ref, h1_ref, nw_ref, dh2_ref, wo_ref, ya_ref, hs_ref,
             dh1_ref, dya_ref, dyb_ref, delta_ref, acc_ref):
        @pl.when(pl.program_id(0) == 0)
        def _():
            acc_ref[...] = jnp.zeros_like(acc_ref)

        du2 = _dot_nt(dgu_ref[:, :cw], w_ref[0])
        for j in range(1, N_CHIPS):
            du2 = du2 + _dot_nt(dgu_ref[:, j * cw:(j + 1) * cw], w_ref[j])
        h1 = h1_ref[...]
        r2 = _rms(h1)
        nh = h1 * r2
        acc_ref[0:1, :] += jnp.sum(du2 * nh, axis=0, keepdims=True)
        dn = du2 * nw_ref[...]
        dh1 = dh2_ref[...] + r2 * (dn - nh * jnp.mean(dn * nh, axis=-1, keepdims=True))
        dh1_ref[...] = dh1
        dmixed = _dot_nt(dh1.astype(BF16), wo_ref[...])
        dya = dmixed[:, :ATTN_WIDTH]
        dya_ref[...] = dya
        dyb_ref[...] = dmixed[:, ATTN_WIDTH:]
        prod = dya * ya_ref[...]
        hi = prod.astype(BF16)
        lo = (prod - hi.astype(F32)).astype(BF16)
        delta_ref[...] = _dot(hi, hs_ref[...]) + _dot(lo, hs_ref[...])

    wide = _rows(tm, D_MODEL)
    half = _rows(tm, ATTN_WIDTH)
    return pl.pallas_call(
        body, name="gu_bwd", grid=(seq // tm,),
        in_specs=[_rows(tm, 2 * FFN_HIDDEN), _full((N_CHIPS, D_MODEL, cw)), wide, _full((1, D_MODEL)), wide,
                  _full((D_MODEL, D_MODEL)), half, _full((ATTN_WIDTH, ATTN_WIDTH))],
        out_specs=[wide, half, half, half, _full((8, D_MODEL))],
        out_shape=[jax.ShapeDtypeStruct((seq, D_MODEL), F32), jax.ShapeDtypeStruct((seq, ATTN_WIDTH), F32),
                   jax.ShapeDtypeStruct((seq, ATTN_WIDTH), F32), jax.ShapeDtypeStruct((seq, ATTN_WIDTH), F32),
                   jax.ShapeDtypeStruct((8, D_MODEL), F32)],
        compiler_params=_cparams("arbitrary"),
    )(dgu, w_gu4, h1, norm2_w, dh2, w_out, ya, hsum)


def attn_bwd(q, k, v, dy, lse, delta, dilation, name):
    seq = q.shape[0]
    span, per_step, cur, prev = _attn_specs(seq, dilation)
    whole = pl.BlockSpec((seq, LANES), lambda hp, j: (0, hp))

    def body(q_ref, dy_ref, lse_ref, dl_ref, kc_ref, vc_ref, kp_ref, vp_ref, dq_ref, dk_ref, dv_ref):
        first_step = pl.program_id(1) == 0
        base = pl.program_id(1) * ATTN_STEP_ROWS
        masks = _head_masks()

        def block(it, carry):
            t, r = it // dilation, it % dilation
            rows = _strided_rows(r + span * t, dilation)
            at_edge = t == 0
            q2, dy2 = q_ref[rows, :].astype(BF16), dy_ref[rows, :].astype(BF16)
            lse2, dl2 = lse_ref[rows, :], dl_ref[rows, :]
            kp = _block_before(kp_ref, kc_ref, t, r, span, dilation, per_step)
            vp = _block_before(vp_ref, vc_ref, t, r, span, dilation, per_step)
            k2 = jnp.concatenate([kp, kc_ref[rows, :]], axis=0).astype(BF16)
            v2 = jnp.concatenate([vp, vc_ref[rows, :]], axis=0).astype(BF16)
            valid = _window_valid(first_step & at_edge)
            zero = jnp.zeros_like(q2)
            qms, dyms, ps, dss, kms = [], [], [], [], []
            for h, mh in enumerate(masks):
                c0 = h * HEAD_DIM
                qm, dym = jnp.where(mh, q2, zero), jnp.where(mh, dy2, zero)
                s = _dot_nt(qm, k2) * ATTN_SCALE
                p = jnp.where(valid, jnp.exp(s - lse2[:, c0:c0 + 1]), 0.0)
                dp = _dot_nt(dym, v2)
                dss.append((p * (dp - dl2[:, c0:c0 + 1]) * ATTN_SCALE).astype(BF16))
                ps.append(p.astype(BF16))
                qms.append(qm)
                dyms.append(dym)
                kms.append(jnp.where(mh, k2, jnp.zeros_like(k2)))
            dq_ref[rows, :] = _dot(jnp.concatenate(dss, axis=1), jnp.concatenate(kms, axis=0))
            dv_full = _dot_tn(jnp.concatenate(ps, axis=0), jnp.concatenate(dyms, axis=0))
            dk_full = _dot_tn(jnp.concatenate(dss, axis=0), jnp.concatenate(qms, axis=0))
            here = _strided_rows(base + r + span * t, dilation)
            dk_ref[here, :] = dk_full[ATTN_BLOCK:]
            dv_ref[here, :] = dv_full[ATTN_BLOCK:]

            back = _strided_rows(jnp.maximum(base + r + span * t - span, r), dilation)
            dk_ref[back, :] += dk_full[:ATTN_BLOCK]
            dv_ref[back, :] += dv_full[:ATTN_BLOCK]
            return carry

        lax.fori_loop(0, per_step * dilation, block, 0, unroll=ATTN_BWD_UNROLL)

    return pl.pallas_call(
        body, name=name, grid=(ATTN_WIDTH // LANES, seq // ATTN_STEP_ROWS),
        in_specs=[cur] * 6 + [prev, prev],
        out_specs=[cur, whole, whole],
        out_shape=[jax.ShapeDtypeStruct((seq, ATTN_WIDTH), F32)] * 3,
        compiler_params=_cparams("parallel", "arbitrary"),
    )(q, dy, lse, delta, k, v, k, v)


def hgrn_bwd(hg, lb_logits, hnw, o_pre, st0, dyb, chip_sums=()):
    seq = hg.shape[0]
    nblk = seq // HGRN_ROWS
    n_chunks = HGRN_ROWS // HGRN_CHUNK
    w = HGRN_WIDTH
    n_s = len(chip_sums)

    def body(*refs):
        hg_ref, lbl_ref, hnw_ref, o_ref, st0_ref, dyb_ref = refs[:6]
        dhg_ref, acc_ref = refs[6 + n_s:8 + n_s]
        dst_scr = refs[8 + 2 * n_s]
        step = pl.program_id(0)
        if n_s:
            start, finish = _chip_exchange_phases(refs[6:6 + n_s], refs[8 + n_s:8 + 2 * n_s], *refs[9 + 2 * n_s:])
            pl.when(step == 0)(start)

        @pl.when(step == 0)
        def _():
            dst_scr[...] = jnp.zeros_like(dst_scr)
            acc_ref[...] = jnp.zeros_like(acc_ref)

        hg_v = hg_ref[...]
        p = _hgrn_prep(hg_v, lbl_ref[...])
        tril, block = _chunk_masks()
        vv = hg_v[:, 2 * w:3 * w].astype(BF16)
        hnw_v = hnw_ref[...]
        gb = p["gb"]
        sgg = _sigmoid(gb)
        silu_g = gb * sgg
        dyb_v = dyb_ref[...]
        o_v = o_ref[...]

        d_on = dyb_v * hnw_v * silu_g
        on_parts, do_parts = [], []
        for h in range(HGRN_HEADS):
            sl = slice(h * LANES, (h + 1) * LANES)
            rs = _rms(o_v[:, sl])
            on = o_v[:, sl] * rs
            on_parts.append(on)
            do_parts.append(rs * (d_on[:, sl] - on * jnp.mean(d_on[:, sl] * on, axis=-1, keepdims=True)))
        on_all = jnp.concatenate(on_parts, axis=1)
        dgb = dyb_v * on_all * hnw_v * (sgg * (1.0 + gb * (1.0 - sgg)))
        acc_ref[0:1, :] += jnp.sum(dyb_v * on_all * silu_g, axis=0, keepdims=True)

        dqf_parts, dkk_parts, db_parts, dv_parts, dbt_parts, dkbkb_parts = [], [], [], [], [], []
        for h in range(HGRN_HEADS):
            sl = slice(h * LANES, (h + 1) * LANES)
            v_h = vv[:, sl]
            do_h = do_parts[h].astype(BF16)
            qi, ki, qs, kb = p["qi"][:, sl], p["ki"][:, sl], p["qs"][:, sl], p["kb"][:, sl]
            qi_b, ki_b = qi.astype(BF16), ki.astype(BF16)
            kb_cat = _spread(kb.astype(BF16), block)
            qs_cat = _spread(qs.astype(BF16), block)
            upd = _dot_tn(v_h, kb_cat)
            st = st0_ref[h]
            st_parts = []
            for n in range(n_chunks):
                st_parts.append(st)
                decay = p["e_tot"][n * HGRN_CHUNK:n * HGRN_CHUNK + 1, sl]
                st = st * decay + upd[:, n * LANES:(n + 1) * LANES]
            st_cat = jnp.concatenate([s_.astype(BF16) for s_ in st_parts], axis=1)
            wgt = _dot_tn(do_h, qs_cat)
            dst = dst_scr[h]
            dst_parts = [None] * n_chunks
            dbt_rows = [None] * n_chunks
            for n in reversed(range(n_chunks)):
                dst_parts[n] = dst.astype(BF16)
                decay = p["e_tot"][n * HGRN_CHUNK:n * HGRN_CHUNK + 1, sl]
                dbt_rows[n] = jnp.sum(dst * st_parts[n], axis=0, keepdims=True) * decay
                dst = dst * decay + wgt[:, n * LANES:(n + 1) * LANES]
            dst_scr[h] = dst
            dst_cat = jnp.concatenate(dst_parts, axis=1)
            dqs = _fold(_dot(do_h, st_cat), block)
            dkb = _fold(_dot(v_h, dst_cat), block)
            dv_state = _dot_nt(kb_cat, dst_cat)
            a = jnp.where(tril, _dot_nt(qi_b, ki_b), 0.0).astype(BF16)
            da = jnp.where(tril, _dot_nt(do_h, v_h), 0.0).astype(BF16)
            dv_parts.append(_dot_tn(a, do_h) + dv_state)
            dqi = _dot(da, ki_b)
            dki = _dot_tn(da, qi_b)
            dqf_parts.append(dqi * p["e_iq"][:, sl] + dqs * p["e_b"][:, sl])
            dkk_parts.append(dki * p["e_ik"][:, sl] + dkb * p["e_bb"][:, sl])
            dkbkb = dkb * kb
            db_parts.append(dqi * qi - dki * ki + dqs * qs - dkbkb)
            dkbkb_parts.append(dkbkb)
            dbt_parts.append(_chunk_rows(dbt_rows))

        cat = lambda parts: jnp.concatenate(parts, axis=1)
        dlogf = (_chunk_cumsum(cat(db_parts), reverse=True)
                 + _chunk_row(_chunk_cumsum(cat(dkbkb_parts)), HGRN_CHUNK - 1) + cat(dbt_parts))
        sq, qb = p["sq"], hg_v[:, :w]
        dqb = cat(dqf_parts) * (sq * (1.0 + qb * (1.0 - sq)))
        df = dlogf / p["f"] - cat(dkk_parts)
        sg, lb = p["sg"], p["lb"]
        dfb = df * (1.0 - lb) * sg * (1.0 - sg)
        acc_ref[1:2, :] += jnp.sum(df * (1.0 - sg), axis=0, keepdims=True)
        dhg_ref[...] = jnp.concatenate([dqb, dfb, cat(dv_parts), dgb], axis=1)
        if n_s:
            pl.when(step == nblk - 1)(finish)

    rev = lambda i: (nblk - 1 - i, 0)
    anywhere = pl.BlockSpec(memory_space=pl.ANY)
    return pl.pallas_call(
        body, name="hgrn_bwd", grid=(nblk,),
        in_specs=[pl.BlockSpec((HGRN_ROWS, 4 * w), rev), _full((2, w)), _full((1, w)),
                  pl.BlockSpec((HGRN_ROWS, w), rev),
                  pl.BlockSpec((None, HGRN_HEADS, LANES, LANES), lambda i: (nblk - 1 - i, 0, 0, 0)),
                  pl.BlockSpec((HGRN_ROWS, w), rev)] + [anywhere] * n_s,
        out_specs=[pl.BlockSpec((HGRN_ROWS, 4 * w), rev), _full((8, w))] + [anywhere] * n_s,
        out_shape=[jax.ShapeDtypeStruct((seq, 4 * w), F32), jax.ShapeDtypeStruct((8, w), F32)]
        + [jax.ShapeDtypeStruct(s.shape, s.dtype) for s in chip_sums],
        scratch_shapes=[pltpu.VMEM((HGRN_HEADS, LANES, LANES), F32)] + (_chip_exchange_scratch(n_s) if n_s else []),
        compiler_params=_cparams("arbitrary"),
    )(hg, lb_logits, hnw, o_pre, st0, dyb, *chip_sums)


def in_bwd(dqs, dks, dvs, dhg, cos_t, sin_t, w_in4, x, norm1_w, dh1):
    seq = x.shape[0]
    tm = ROW_TILE
    cw = w_in4.shape[2]

    def body(dq1, dq2, dq3, dk1, dk2, dk3, dv1, dv2, dv3, dhg_ref, cos_ref, sin_ref, w_ref,
             x_ref, nw_ref, dh1_ref, dproj_ref, dx_ref, acc_ref):
        @pl.when(pl.program_id(0) == 0)
        def _():
            acc_ref[...] = jnp.zeros_like(acc_ref)

        cos, sin = cos_ref[...], sin_ref[...]
        dqa = _rotary_bwd(dq1[...] + dq2[...] + dq3[...], cos, sin)
        dka = _rotary_bwd(dk1[...] + dk2[...] + dk3[...], cos, sin)
        dva = dv1[...] + dv2[...] + dv3[...]
        dproj = jnp.concatenate([dqa, dka, dva, dhg_ref[...]], axis=1).astype(BF16)
        dproj_ref[...] = dproj
        du = _dot_nt(dproj[:, :cw], w_ref[0])
        for j in range(1, N_CHIPS):
            du = du + _dot_nt(dproj[:, j * cw:(j + 1) * cw], w_ref[j])
        xv = x_ref[...]
        r1 = _rms(xv)
        nx = xv * r1
        acc_ref[0:1, :] += jnp.sum(du * nx, axis=0, keepdims=True)
        dn = du * nw_ref[...]
        dx_ref[...] = dh1_ref[...] + r1 * (dn - nx * jnp.mean(dn * nx, axis=-1, keepdims=True))

    half = _rows(tm, ATTN_WIDTH)
    wide = _rows(tm, D_MODEL)
    return pl.pallas_call(
        body, name="in_bwd", grid=(seq // tm,),
        in_specs=[half] * 9 + [_rows(tm, 4 * HGRN_WIDTH), _rows(tm, LANES), _rows(tm, LANES),
                               _full((N_CHIPS, D_MODEL, cw)), wide, _full((1, D_MODEL)), wide],
        out_specs=[_rows(tm, IN_PROJ_WIDTH), wide, _full((8, D_MODEL))],
        out_shape=[jax.ShapeDtypeStruct((seq, IN_PROJ_WIDTH), BF16), jax.ShapeDtypeStruct((seq, D_MODEL), F32),
                   jax.ShapeDtypeStruct((8, D_MODEL), F32)],
        compiler_params=_cparams("arbitrary"),
    )(*dqs, *dks, *dvs, dhg, cos_t, sin_t, w_in4, x, norm1_w, dh1)


def weight_grad(a, b, col_block, name):
    seq, kdim = a.shape
    ndim = b.shape[1]
    nj = ndim // col_block
    tk = 512

    def body(a_ref, b_ref, o_ref):
        @pl.when(pl.program_id(1) == 0)
        def _():
            o_ref[...] = jnp.zeros_like(o_ref)

        o_ref[...] += _dot_tn(a_ref[...].astype(BF16), b_ref[...].astype(BF16))

    return pl.pallas_call(
        body, name=name, grid=(nj, seq // tk),
        in_specs=[pl.BlockSpec((tk, kdim), lambda j, t: (t, 0)),
                  pl.BlockSpec((tk, col_block), lambda j, t: (t, j))],
        out_specs=pl.BlockSpec((None, kdim, col_block), lambda j, t: (j, 0, 0)),
        out_shape=jax.ShapeDtypeStruct((nj, kdim, col_block), F32),
        compiler_params=_cparams("parallel", "arbitrary"),
    )(a, b)


def exchange_with_sibling(grads, name):
    n = len(grads)

    def body(*refs):
        g_refs, out_refs = refs[:n], refs[n:2 * n]
        send_sems, recv_sems = refs[2 * n], refs[2 * n + 1]
        x, y, cc = _mesh_pos()
        copies = []
        for i in range(n):
            for j in range(N_CHIPS):
                k = i * N_CHIPS + j
                copies.append(pltpu.make_async_remote_copy(
                    src_ref=g_refs[i].at[j, 1 - cc], dst_ref=out_refs[i].at[j],
                    send_sem=send_sems.at[k], recv_sem=recv_sems.at[k],
                    device_id=(x, y, 1 - cc), device_id_type=MESH_ID))
        for cp in copies:
            cp.start()
        for cp in copies:
            cp.wait_recv()
        for cp in copies:
            cp.wait_send()

    return pl.pallas_call(
        body, name=name,
        in_specs=[pl.BlockSpec(memory_space=pl.ANY)] * n,
        out_specs=[pl.BlockSpec(memory_space=pl.ANY)] * n,
        out_shape=[jax.ShapeDtypeStruct((N_CHIPS,) + g.shape[2:], g.dtype) for g in grads],
        scratch_shapes=[pltpu.SemaphoreType.DMA((n * N_CHIPS,)), pltpu.SemaphoreType.DMA((n * N_CHIPS,))],
    )(*grads)


def add_own_half(grad, recv, name):
    _, _, r, c = grad.shape
    tr = r // 2 if r % 32 == 0 else r

    def body(cc_ref, g_ref, r_ref, o_ref):
        o_ref[...] = (g_ref[...] + r_ref[...]).astype(BF16)

    grid_spec = pltpu.PrefetchScalarGridSpec(
        num_scalar_prefetch=1, grid=(N_CHIPS, r // tr),
        in_specs=[pl.BlockSpec((None, None, tr, c), lambda j, t, cc: (j, cc[0], t, 0)),
                  pl.BlockSpec((None, tr, c), lambda j, t, cc: (j, t, 0))],
        out_specs=pl.BlockSpec((None, tr, c), lambda j, t, cc: (j, t, 0)))
    cc = lax.axis_index("c").astype(jnp.int32).reshape(1)
    return pl.pallas_call(
        body, name=name, grid_spec=grid_spec,
        out_shape=jax.ShapeDtypeStruct((N_CHIPS, r, c), BF16),
        compiler_params=_cparams("parallel", "parallel"),
    )(cc, grad, recv)


def _chip_exchange_phases(s_refs, out_refs, send_sems, recv_sems):
    n = len(s_refs)
    x, y, cc = _mesh_pos()
    my_chip = 2 * x + y
    chips = [(1 - x, y), (x, 1 - y), (1 - x, 1 - y)]

    def outgoing():
        return [pltpu.make_async_remote_copy(
            src_ref=s_refs[i].at[2 * px + py], dst_ref=out_refs[i].at[my_chip],
            send_sem=send_sems.at[3 * i + j], recv_sem=recv_sems.at[3 * i + j],
            device_id=(px, py, cc), device_id_type=MESH_ID)
            for i in range(n) for j, (px, py) in enumerate(chips)]

    def start():
        for cp in outgoing():
            cp.start()

    def finish():
        for i in range(n):
            for j, (px, py) in enumerate(chips):
                pltpu.make_async_remote_copy(
                    src_ref=s_refs[i].at[my_chip], dst_ref=out_refs[i].at[2 * px + py],
                    send_sem=send_sems.at[3 * i + j], recv_sem=recv_sems.at[3 * i + j],
                    device_id=(px, py, cc), device_id_type=MESH_ID).wait_recv()
        for cp in outgoing():
            cp.wait_send()

    return start, finish


def _chip_exchange_scratch(n):
    return [pltpu.SemaphoreType.DMA((3 * n,)), pltpu.SemaphoreType.DMA((3 * n,))]


def exchange_between_chips(sums):
    n = len(sums)

    def body(*refs):
        start, finish = _chip_exchange_phases(refs[:n], refs[n:2 * n], refs[2 * n], refs[2 * n + 1])
        start()
        finish()

    return pl.pallas_call(
        body, name="grad_exchange_chips",
        in_specs=[pl.BlockSpec(memory_space=pl.ANY)] * n,
        out_specs=[pl.BlockSpec(memory_space=pl.ANY)] * n,
        out_shape=[jax.ShapeDtypeStruct(s.shape, s.dtype) for s in sums],
        scratch_shapes=_chip_exchange_scratch(n),
    )(*sums)


def sum_chips(sums, parts, name):
    _, r, c = parts.shape
    tr = r // 2 if r % 32 == 0 else r

    def body(idx_ref, s_ref, p1_ref, p2_ref, p3_ref, o_ref):
        o_ref[...] = ((s_ref[...].astype(F32) + p1_ref[...].astype(F32))
                      + p2_ref[...].astype(F32)) + p3_ref[...].astype(F32)

    def pick(k):
        return pl.BlockSpec((None, tr, c), lambda t, idx: (idx[k], t, 0))

    x, y = lax.axis_index("x"), lax.axis_index("y")
    idx = jnp.stack([2 * x + y, 2 * (1 - x) + y, 2 * x + (1 - y), 2 * (1 - x) + (1 - y)]).astype(jnp.int32)
    grid_spec = pltpu.PrefetchScalarGridSpec(
        num_scalar_prefetch=1, grid=(r // tr,),
        in_specs=[pick(0), pick(1), pick(2), pick(3)],
        out_specs=pl.BlockSpec((tr, c), lambda t, idx: (t, 0)))
    return pl.pallas_call(
        body, name=name, grid_spec=grid_spec,
        out_shape=jax.ShapeDtypeStruct((r, c), F32),
        compiler_params=_cparams("parallel"),
    )(idx, sums, parts, parts, parts)


def share_with_sibling(halves):
    n = len(halves)

    def body(*refs):
        h_refs, out_refs = refs[:n], refs[n:2 * n]
        send_sems, recv_sems = refs[2 * n], refs[2 * n + 1]
        x, y, cc = _mesh_pos()
        copies = [pltpu.make_async_remote_copy(
            src_ref=h_refs[i], dst_ref=out_refs[i],
            send_sem=send_sems.at[i], recv_sem=recv_sems.at[i],
            device_id=(x, y, 1 - cc), device_id_type=MESH_ID) for i in range(n)]
        for cp in copies:
            cp.start()
        for cp in copies:
            cp.wait_recv()
        for cp in copies:
            cp.wait_send()

    return pl.pallas_call(
        body, name="grad_share_sibling",
        in_specs=[pl.BlockSpec(memory_space=pl.ANY)] * n,
        out_specs=[pl.BlockSpec(memory_space=pl.ANY)] * n,
        out_shape=[jax.ShapeDtypeStruct(h.shape, h.dtype) for h in halves],
        scratch_shapes=[pltpu.SemaphoreType.DMA((n,)), pltpu.SemaphoreType.DMA((n,))],
    )(*halves)


def _adam_update(w, g, m, v):
    m = ADAM_B1 * m + (1.0 - ADAM_B1) * g
    v = ADAM_B2 * v + (1.0 - ADAM_B2) * (g * g)
    m_hat = m / (1.0 - ADAM_B1 ** ADAM_STEP)
    v_hat = v / (1.0 - ADAM_B2 ** ADAM_STEP)
    delta = -ADAM_LR * (m_hat / (jnp.sqrt(v_hat) + ADAM_EPS) + ADAM_WD * w)
    return delta, m, v


def adamw(w, g_mine, g_sibling, m, v, name):
    r, c = w.shape
    half = r // 2
    tr = half // 2 if half % 16 == 0 else half
    nt = half // tr

    def body(cc_ref, w_ref, ga_ref, gb_ref, m_ref, v_ref, g_ref, d_ref, nm_ref, nv_ref):
        g = jnp.where(pl.program_id(0) == cc_ref[0], ga_ref[...], gb_ref[...])
        g_ref[...] = g
        d, nm, nv = _adam_update(w_ref[...], g, m_ref[...], v_ref[...])
        d_ref[...] = d
        nm_ref[...] = nm
        nv_ref[...] = nv

    full = pl.BlockSpec((tr, c), lambda h, t, cc: (h * nt + t, 0))
    part = pl.BlockSpec((tr, c), lambda h, t, cc: (t, 0))
    grid_spec = pltpu.PrefetchScalarGridSpec(
        num_scalar_prefetch=1, grid=(2, nt),
        in_specs=[full, part, part, full, full], out_specs=[full] * 4)
    cc = lax.axis_index("c").astype(jnp.int32).reshape(1)
    return pl.pallas_call(
        body, name=name, grid_spec=grid_spec,
        out_shape=[jax.ShapeDtypeStruct((r, c), F32)] * 4,
        compiler_params=_cparams("parallel", "parallel"),
    )(cc, w, g_mine, g_sibling, m, v)


def small_allreduce(pack):
    def body(p_ref, o_ref, gather, send_sems, recv_sems):
        x, y, cc = _mesh_pos()
        me = 4 * x + 2 * y + cc
        gather[me] = p_ref[...]
        flips = [(fx, fy, fc) for fx in (0, 1) for fy in (0, 1) for fc in (0, 1)][1:]
        copies = []
        for k, (fx, fy, fc) in enumerate(flips):
            copies.append(pltpu.make_async_remote_copy(
                src_ref=p_ref, dst_ref=gather.at[me],
                send_sem=send_sems.at[k], recv_sem=recv_sems.at[k],
                device_id=(x ^ fx, y ^ fy, cc ^ fc), device_id_type=MESH_ID))
        for cp in copies:
            cp.start()
        for k, (fx, fy, fc) in enumerate(flips):
            src = 4 * (x ^ fx) + 2 * (y ^ fy) + (cc ^ fc)
            pltpu.make_async_remote_copy(
                src_ref=p_ref, dst_ref=gather.at[src],
                send_sem=send_sems.at[k], recv_sem=recv_sems.at[k],
                device_id=(x ^ fx, y ^ fy, cc ^ fc), device_id_type=MESH_ID).wait_recv()
        for cp in copies:
            cp.wait_send()
        total = gather[0]
        for d in range(1, N_DEV):
            total = total + gather[d]
        o_ref[...] = total

    return pl.pallas_call(
        body, name="small_allreduce",
        in_specs=[pl.BlockSpec(memory_space=pltpu.VMEM)],
        out_specs=pl.BlockSpec(memory_space=pltpu.VMEM),
        out_shape=jax.ShapeDtypeStruct(pack.shape, pack.dtype),
        scratch_shapes=[pltpu.VMEM((N_DEV,) + pack.shape, pack.dtype),
                        pltpu.SemaphoreType.DMA((7,)), pltpu.SemaphoreType.DMA((7,))],
    )(pack)


def small_update(gsum, wpack, mpack, vpack):
    hw = HGRN_WIDTH

    def body(g_ref, w_ref, m_ref, v_ref, go_ref, d_ref, nm_ref, nv_ref, loss_ref):
        g = g_ref[...]
        wv = w_ref[...]
        a0, a1 = wv[4:5, :hw], wv[4:5, hw:]
        mx = jnp.maximum(a0, a1)
        e0, e1 = jnp.exp(a0 - mx), jnp.exp(a1 - mx)
        lb = e0 / (e0 + e1)
        dl = g[4:5, :hw] * lb * (1.0 - lb)
        row = lax.broadcasted_iota(jnp.int32, g.shape, 0)
        lb_row = jnp.concatenate([dl, -dl], axis=1)
        grads = jnp.where(row == 4, lb_row, jnp.where(row < 4, g, 0.0))
        go_ref[...] = grads
        d, nm, nv = _adam_update(wv, grads, m_ref[...], v_ref[...])
        d_ref[...] = d
        nm_ref[...] = nm
        nv_ref[...] = nv
        loss_ref[...] = jnp.zeros((8, LANES), F32) + jnp.sum(g[5:6, :])

    vm = pl.BlockSpec(memory_space=pltpu.VMEM)
    return pl.pallas_call(
        body, name="small_update",
        in_specs=[vm] * 4, out_specs=[vm] * 5,
        out_shape=[jax.ShapeDtypeStruct(gsum.shape, F32)] * 4 + [jax.ShapeDtypeStruct((8, LANES), F32)],
    )(gsum, wpack, mpack, vpack)


def _pack_small(n1, n2, fn, hn, lbl):
    z = jnp.zeros((1, D_MODEL - HGRN_WIDTH), F32)
    rows = [n1.reshape(1, D_MODEL), n2.reshape(1, D_MODEL), fn.reshape(1, D_MODEL),
            jnp.concatenate([hn.reshape(1, HGRN_WIDTH), z], axis=1), lbl.reshape(1, 2 * HGRN_WIDTH),
            jnp.zeros((3, D_MODEL), F32)]
    return jnp.concatenate(rows, axis=0)


def _unpack_small(pack):
    return (pack[0:1], pack[4].reshape(2, HGRN_WIDTH), pack[3:4, :HGRN_WIDTH], pack[1:2], pack[2])


def kernel(x, norm1_w, w_in, lb_logits, hgrn_norm_w, w_out, norm2_w, w_gate_up, w_down, final_norm_w, loss_target, m_norm1_w, m_w_in, m_lb_logits, m_hgrn_norm_w, m_w_out, m_norm2_w, m_w_gate_up, m_w_down, m_final_norm_w, v_norm1_w, v_w_in, v_lb_logits, v_hgrn_norm_w, v_w_out, v_norm2_w, v_w_gate_up, v_w_down, v_final_norm_w):
    seq = x.shape[1]
    xs = x.reshape(seq, D_MODEL)
    target = loss_target.reshape(seq, D_MODEL)
    shards = {"w_in": w_in[0], "w_out": w_out[0], "w_gu": w_gate_up[0], "w_down": w_down[0]}

    cast = {k: cast_bf16(w, "cast_" + k) for k, w in shards.items()}
    w_in4 = allgather_halves(cast["w_in"], "gather_w_in").reshape(N_CHIPS, D_MODEL, -1)

    cos_t, sin_t = _rope_tables(seq)
    fw = final_norm_w.reshape(1, D_MODEL)

    qr, kr, va, hg, u = in_proj(xs, norm1_w, w_in4, cos_t, sin_t)
    fwd = [attn_fwd(qr, kr, va, d, "attn_fwd_d%d" % d) for _, d in DILATED_PAIRS]
    yb, o_pre, st0, g_out, g_gu, g_down = hgrn_fwd(
        hg, lb_logits, hgrn_norm_w, [cast["w_out"], cast["w_gu"], cast["w_down"]])
    w_out_f = g_out.reshape(D_MODEL, D_MODEL)
    w_gu4 = g_gu.reshape(N_CHIPS, D_MODEL, -1)
    w_down_f = g_down.reshape(FFN_HIDDEN, D_MODEL)
    ya, lse, mixed, h1, u2 = mix_out([f[0] for f in fwd], [f[1] for f in fwd], yb, xs, w_out_f, norm2_w)
    g, up, act = gate_up(u2, w_gu4)
    dh2, acc_fin = down_loss(act, w_down_f, h1, fw, target)

    cw_in, cw_gu = w_in4.shape[2], w_gu4.shape[2]
    dgu = down_bwd(dh2, w_down_f, g, up)
    dh1, dya, dyb, delta, acc_n2 = gu_bwd(dgu, w_gu4, h1, norm2_w, dh2, w_out_f, ya)
    early = [
        weight_grad(mixed, dh1, D_MODEL, "wgrad_out").reshape(N_CHIPS, 2, D_MODEL // 8, D_MODEL),
        weight_grad(u2, dgu, cw_gu, "wgrad_gu").reshape(N_CHIPS, 2, D_MODEL // 2, cw_gu),
        weight_grad(act, dh2, D_MODEL, "wgrad_down").reshape(N_CHIPS, 2, FFN_HIDDEN // 8, D_MODEL),
    ]
    early_names = ["out", "gu", "down"]
    early_recv = exchange_with_sibling(early, "grad_exchange_sibling_early")
    early_sums = [add_own_half(gr, rc, "add_half_" + nm) for gr, rc, nm in zip(early, early_recv, early_names)]
    dhg, acc_hg, *early_parts = hgrn_bwd(hg, lb_logits, hgrn_norm_w, o_pre, st0, dyb, early_sums)
    bwd = [attn_bwd(qr, kr, va, dya, lse, delta, d, "attn_bwd_d%d" % d) for _, d in DILATED_PAIRS]
    dproj, dx, acc_n1 = in_bwd([b[0] for b in bwd], [b[1] for b in bwd], [b[2] for b in bwd],
                               dhg, cos_t, sin_t, w_in4, xs, norm1_w, dh1)
    late = [weight_grad(u, dproj, cw_in, "wgrad_in").reshape(N_CHIPS, 2, D_MODEL // 2, cw_in)]
    late_recv = exchange_with_sibling(late, "grad_exchange_sibling_late")
    late_sums = [add_own_half(late[0], late_recv[0], "add_half_in")]
    late_parts = exchange_between_chips(late_sums)
    names = ["in"] + early_names
    sums, parts = late_sums + early_sums, list(late_parts) + list(early_parts)
    halves = [sum_chips(s, p, "sum_chips_" + nm) for s, p, nm in zip(sums, parts, names)]
    others = share_with_sibling(halves)
    big = {}
    for nm, key, mine, other, m_, v_ in zip(names, ["w_in", "w_out", "w_gu", "w_down"], halves, others,
                                            [m_w_in, m_w_out, m_w_gate_up, m_w_down],
                                            [v_w_in, v_w_out, v_w_gate_up, v_w_down]):
        big[key] = tuple(t[None] for t in adamw(shards[key], mine, other, m_[0], v_[0], "adamw_" + nm))

    z512 = jnp.zeros((1, D_MODEL - HGRN_WIDTH), F32)
    gpack = jnp.concatenate([
        acc_n1[0:1], acc_n2[0:1], acc_fin[0:1],
        jnp.concatenate([acc_hg[0:1], z512], axis=1), jnp.concatenate([acc_hg[1:2], z512], axis=1),
        acc_fin[1:2], jnp.zeros((2, D_MODEL), F32)], axis=0)
    gsum = small_allreduce(gpack)
    wpack = _pack_small(norm1_w, norm2_w, final_norm_w, hgrn_norm_w, lb_logits)
    mpack = _pack_small(m_norm1_w, m_norm2_w, m_final_norm_w, m_hgrn_norm_w, m_lb_logits)
    vpack = _pack_small(v_norm1_w, v_norm2_w, v_final_norm_w, v_hgrn_norm_w, v_lb_logits)
    gs, ds, nms, nvs, loss8 = small_update(gsum, wpack, mpack, vpack)
    loss = loss8[0, 0]

    def assemble(small_pack, idx):
        n1, lbl, hn, n2, fn = _unpack_small(small_pack)
        return (n1, big["w_in"][idx], lbl, hn, big["w_out"][idx], n2, big["w_gu"][idx], big["w_down"][idx], fn)

    return (loss, dx.reshape(x.shape), *assemble(gs, 0), *assemble(ds, 1), *assemble(nms, 2), *assemble(nvs, 3))
```

```python
import functools

import jax
import jax.numpy as jnp
from jax import lax
from jax.experimental import pallas as pl
from jax.experimental.pallas import tpu as pltpu

F32 = jnp.float32
BF16 = jnp.bfloat16

D_MODEL = 1024
ATTN_WIDTH = 512
HEAD_DIM = 64
DILATED_PAIRS = ((128, 1), (512, 4), (2048, 16))
ATTN_BLOCK = 128
ROPE_THETA = 10000.0
HGRN_WIDTH = 512
HGRN_CHUNK = 16
HGRN_HEADS = 4
IN_PROJ_WIDTH = 3584
FFN_HIDDEN = 2816
NORM_EPS = 1e-6
ATTN_SCALE = HEAD_DIM ** -0.5
N_CHIPS = 4
N_DEV = 8

ADAM_LR = 0.001
ADAM_B1 = 0.9
ADAM_B2 = 0.999
ADAM_EPS = 1e-08
ADAM_WD = 0.01
ADAM_STEP = 10

LANES = 128
HGRN_ROWS = 128
ROW_TILE = 256
ATTN_STEP_ROWS = 2048
ATTN_FWD_UNROLL = 8
ATTN_BWD_UNROLL = 8
VMEM_LIMIT = 56 * 1024 * 1024
NEG_BIG = -1e30
MESH_ID = pl.DeviceIdType.MESH


def _cparams(*sem):
    return pltpu.CompilerParams(dimension_semantics=tuple(sem), vmem_limit_bytes=VMEM_LIMIT)


def _dot(a, b):
    return jnp.dot(a, b, preferred_element_type=F32)


def _dot_nt(a, b):
    return lax.dot_general(a, b, (((1,), (1,)), ((), ())), preferred_element_type=F32)


def _dot_tn(a, b):
    return lax.dot_general(a, b, (((0,), (0,)), ((), ())), preferred_element_type=F32)


def _sigmoid(x):
    return 1.0 / (1.0 + jnp.exp(-x))


def _full(shape):
    n = len(shape)
    return pl.BlockSpec(shape, lambda *_: (0,) * n)


def _weight(shape):
    n = len(shape)
    return pl.BlockSpec(shape, lambda *_: (0,) * n, pipeline_mode=pl.Buffered(1))


def _rows(tm, width):
    return pl.BlockSpec((tm, width), lambda i: (i, 0))


def _swap32(x):
    lane = lax.broadcasted_iota(jnp.int32, x.shape, 1)
    first = (lane % HEAD_DIM) < (HEAD_DIM // 2)
    return jnp.where(first, pltpu.roll(x, LANES - 32, axis=1), pltpu.roll(x, 32, axis=1))


def _rotary_fwd(x, cos, sin_signed):
    parts = []
    for j in range(x.shape[1] // LANES):
        xc = x[:, j * LANES:(j + 1) * LANES]
        parts.append(xc * cos + _swap32(xc) * sin_signed)
    return jnp.concatenate(parts, axis=1)


def _rotary_bwd(dy, cos, sin_signed):
    parts = []
    for j in range(dy.shape[1] // LANES):
        dc = dy[:, j * LANES:(j + 1) * LANES]
        parts.append(dc * cos + _swap32(dc * sin_signed))
    return jnp.concatenate(parts, axis=1)


def _rope_tables(seq):
    half = HEAD_DIM // 2
    inv_freq = ROPE_THETA ** (-jnp.arange(half, dtype=F32) / half)
    ang = jnp.arange(seq, dtype=F32)[:, None] * inv_freq[None, :]
    cos, sin = jnp.cos(ang), jnp.sin(ang)
    cos_t = jnp.tile(cos, (1, LANES // half))
    sin_t = jnp.tile(jnp.concatenate([-sin, sin], axis=1), (1, LANES // HEAD_DIM))
    return cos_t, sin_t


def cast_bf16(w, name):
    r, c = w.shape
    half = r // 2

    def body(w_ref, o_ref):
        o_ref[...] = w_ref[...].astype(BF16)

    return pl.pallas_call(
        body, name=name, grid=(2,),
        in_specs=[pl.BlockSpec((half, c), lambda i: (i, 0))],
        out_specs=pl.BlockSpec((None, half, c), lambda i: (i, 0, 0)),
        out_shape=jax.ShapeDtypeStruct((2, half, c), BF16),
        compiler_params=_cparams("parallel"),
    )(w)


def _mesh_pos():
    return lax.axis_index("x"), lax.axis_index("y"), lax.axis_index("c")


GATHER_COPIES = 7


def _gather_phases(x_refs, out_refs, send_sems, recv_sems, local_sems):
    n = len(x_refs)
    x, y, cc = _mesh_pos()
    me, sibling = (x, y, cc), (x, y, 1 - cc)
    chips = [(1 - x, y), (x, 1 - y), (1 - x, 1 - y)]

    def rows(i, px, py, pc):
        return out_refs[i].at[4 * px + 2 * py + pc]

    def copy(i, k, block, to, src=None):
        return pltpu.make_async_remote_copy(
            src_ref=rows(i, *block) if src is None else src, dst_ref=rows(i, *block),
            send_sem=send_sems.at[GATHER_COPIES * i + k], recv_sem=recv_sems.at[GATHER_COPIES * i + k],
            device_id=to, device_id_type=MESH_ID)

    def local(i):
        return pltpu.make_async_copy(x_refs[i].at[cc], rows(i, *me), local_sems.at[i])

    def first(i):
        mine = x_refs[i].at[cc]
        return [copy(i, 0, me, sibling, src=mine)] + [
            copy(i, 1 + j, me, (*chip, cc), src=mine) for j, chip in enumerate(chips)]

    def passed(i):
        return [copy(i, 4 + j, (*chip, cc), sibling) for j, chip in enumerate(chips)]

    def start():
        for i in range(n):
            local(i).start()
            for cp in first(i):
                cp.start()

    def forward():
        for i in range(n):
            onward = passed(i)
            for j, chip in enumerate(chips):
                copy(i, 1 + j, (*chip, cc), me).wait_recv()
                onward[j].start()

    def finish():
        for i in range(n):
            copy(i, 0, sibling, me).wait_recv()
            for j, chip in enumerate(chips):
                copy(i, 4 + j, (*chip, 1 - cc), me).wait_recv()
            for cp in first(i) + passed(i):
                cp.wait_send()
            local(i).wait()

    return start, forward, finish


def _gather_scratch(n):
    return [pltpu.SemaphoreType.DMA((GATHER_COPIES * n,)), pltpu.SemaphoreType.DMA((GATHER_COPIES * n,)),
            pltpu.SemaphoreType.DMA((n,))]


def _gathered_shape(halves):
    return jax.ShapeDtypeStruct((N_DEV,) + halves.shape[1:], halves.dtype)


def allgather_halves(halves, name):
    def body(x_ref, out_ref, send_sems, recv_sems, local_sems):
        start, forward, finish = _gather_phases([x_ref], [out_ref], send_sems, recv_sems, local_sems)
        start()
        forward()
        finish()

    return pl.pallas_call(
        body, name=name,
        in_specs=[pl.BlockSpec(memory_space=pl.ANY)],
        out_specs=pl.BlockSpec(memory_space=pl.ANY),
        out_shape=_gathered_shape(halves),
        scratch_shapes=_gather_scratch(1),
    )(halves)


def _rms(x):
    return lax.rsqrt(jnp.mean(x * x, axis=-1, keepdims=True) + NORM_EPS)


def in_proj(x, norm1_w, w_in4, cos_t, sin_t):
    seq = x.shape[0]
    tm = ROW_TILE
    cw = w_in4.shape[2]

    def body(x_ref, nw_ref, w_ref, cos_ref, sin_ref, q_ref, k_ref, v_ref, hg_ref, u_ref):
        xv = x_ref[...]
        u = ((xv * _rms(xv)) * nw_ref[...]).astype(BF16)
        u_ref[...] = u
        proj = jnp.concatenate([_dot(u, w_ref[j]) for j in range(N_CHIPS)], axis=1)
        cos, sin = cos_ref[...], sin_ref[...]
        a = ATTN_WIDTH
        q_ref[...] = _rotary_fwd(proj[:, :a], cos, sin)
        k_ref[...] = _rotary_fwd(proj[:, a:2 * a], cos, sin)
        v_ref[...] = proj[:, 2 * a:3 * a]
        hg_ref[...] = proj[:, 3 * a:]

    return pl.pallas_call(
        body, name="in_proj", grid=(seq // tm,),
        in_specs=[_rows(tm, D_MODEL), _full((1, D_MODEL)), _weight((N_CHIPS, D_MODEL, cw)),
                  _rows(tm, LANES), _rows(tm, LANES)],
        out_specs=[_rows(tm, ATTN_WIDTH)] * 3 + [_rows(tm, 4 * HGRN_WIDTH), _rows(tm, D_MODEL)],
        out_shape=[jax.ShapeDtypeStruct((seq, ATTN_WIDTH), F32)] * 3
        + [jax.ShapeDtypeStruct((seq, 4 * HGRN_WIDTH), F32), jax.ShapeDtypeStruct((seq, D_MODEL), BF16)],
        compiler_params=_cparams("parallel"),
    )(x, norm1_w, w_in4, cos_t, sin_t)


def _head_masks():
    lane = lax.broadcasted_iota(jnp.int32, (1, LANES), 1)
    return [(lane // HEAD_DIM) == h for h in range(LANES // HEAD_DIM)]


def _window_valid(no_prev):
    qi = lax.broadcasted_iota(jnp.int32, (ATTN_BLOCK, 2 * ATTN_BLOCK), 0)
    kj = lax.broadcasted_iota(jnp.int32, (ATTN_BLOCK, 2 * ATTN_BLOCK), 1)
    valid = (kj >= qi) & (kj <= qi + ATTN_BLOCK)
    return valid & (jnp.logical_not(no_prev) | (kj >= ATTN_BLOCK))


def _strided_rows(start, dilation):
    if dilation == 1:
        return pl.ds(start, ATTN_BLOCK)
    return pl.ds(start, ATTN_BLOCK, stride=dilation)


def _block_before(edge_ref, cur_ref, t, r, span, dilation, per_step):
    edge = edge_ref[_strided_rows(r, dilation), :]
    if per_step == 1:
        return edge
    inside = cur_ref[_strided_rows(r + span * jnp.maximum(t - 1, 0), dilation), :]
    return jnp.where(t == 0, edge, inside)


def _attn_specs(seq, dilation):
    span = ATTN_BLOCK * dilation
    per_step = ATTN_STEP_ROWS // span
    cur = pl.BlockSpec((ATTN_STEP_ROWS, LANES), lambda hp, j: (j, hp))
    prev = pl.BlockSpec((span, LANES), lambda hp, j: (jnp.maximum(j * per_step - 1, 0), hp))
    return span, per_step, cur, prev


def attn_fwd(q, k, v, dilation, name):
    seq = q.shape[0]
    span, per_step, cur, prev = _attn_specs(seq, dilation)

    def body(q_ref, kc_ref, vc_ref, kp_ref, vp_ref, o_ref, lse_ref):
        first_step = pl.program_id(1) == 0
        masks = _head_masks()

        def block(it, carry):
            t, r = it // dilation, it % dilation
            rows = _strided_rows(r + span * t, dilation)
            at_edge = t == 0
            q2 = q_ref[rows, :].astype(BF16)
            kp = _block_before(kp_ref, kc_ref, t, r, span, dilation, per_step)
            vp = _block_before(vp_ref, vc_ref, t, r, span, dilation, per_step)
            k2 = jnp.concatenate([kp, kc_ref[rows, :]], axis=0).astype(BF16)
            v2 = jnp.concatenate([vp, vc_ref[rows, :]], axis=0).astype(BF16)
            valid = _window_valid(first_step & at_edge)
            o_acc = jnp.zeros((ATTN_BLOCK, LANES), F32)
            l_acc = jnp.zeros((ATTN_BLOCK, LANES), F32)
            for mh in masks:
                qm = jnp.where(mh, q2, jnp.zeros_like(q2))
                s = jnp.where(valid, _dot_nt(qm, k2) * ATTN_SCALE, NEG_BIG)
                m = jnp.max(s, axis=-1, keepdims=True)
                p = jnp.exp(s - m)
                l = jnp.sum(p, axis=-1, keepdims=True)
                o = _dot(p.astype(BF16), v2) / l
                o_acc = jnp.where(mh, o, o_acc)
                l_acc = jnp.where(mh, m + jnp.log(l), l_acc)
            o_ref[rows, :] = o_acc
            lse_ref[rows, :] = l_acc
            return carry

        lax.fori_loop(0, per_step * dilation, block, 0, unroll=ATTN_FWD_UNROLL)

    return pl.pallas_call(
        body, name=name, grid=(ATTN_WIDTH // LANES, seq // ATTN_STEP_ROWS),
        in_specs=[cur, cur, cur, prev, prev],
        out_specs=[cur, cur],
        out_shape=[jax.ShapeDtypeStruct((seq, ATTN_WIDTH), F32)] * 2,
        compiler_params=_cparams("parallel", "parallel"),
    )(q, k, v, k, v)


def _chunk_cumsum(x, reverse=False):
    rc = lax.broadcasted_iota(jnp.int32, x.shape, 0) % HGRN_CHUNK
    sh = 1
    while sh < HGRN_CHUNK:
        if reverse:
            x = x + jnp.where(rc + sh < HGRN_CHUNK, pltpu.roll(x, x.shape[0] - sh, axis=0), 0.0)
        else:
            x = x + jnp.where(rc >= sh, pltpu.roll(x, sh, axis=0), 0.0)
        sh *= 2
    return x


def _chunk_row(x, row):
    return _chunk_rows([x[n * HGRN_CHUNK + row:n * HGRN_CHUNK + row + 1, :]
                        for n in range(x.shape[0] // HGRN_CHUNK)])


def _chunk_rows(rows):
    return jnp.concatenate([jnp.broadcast_to(r, (HGRN_CHUNK, r.shape[1])) for r in rows], axis=0)


def _hgrn_prep(hg, lbl):
    w = HGRN_WIDTH
    a0, a1 = lbl[0:1, :], lbl[1:2, :]
    mx = jnp.maximum(a0, a1)
    e0, e1 = jnp.exp(a0 - mx), jnp.exp(a1 - mx)
    lb = e0 / (e0 + e1)
    qb, fb, gb = hg[:, :w], hg[:, w:2 * w], hg[:, 3 * w:]
    sg = _sigmoid(fb)
    f = lb + (1.0 - lb) * sg
    b = _chunk_cumsum(jnp.log(f))
    bmid, btot = _chunk_row(b, HGRN_CHUNK // 2 - 1), _chunk_row(b, HGRN_CHUNK - 1)
    sq = _sigmoid(qb)
    p = dict(lb=lb, sg=sg, f=f, kk=1.0 - f, sq=sq, qf=qb * sq, gb=gb,
             e_iq=jnp.exp(b - bmid), e_ik=jnp.exp(bmid - b), e_b=jnp.exp(b),
             e_bb=jnp.exp(btot - b), e_tot=jnp.exp(btot))
    p["qi"] = p["qf"] * p["e_iq"]
    p["ki"] = p["kk"] * p["e_ik"]
    p["qs"] = p["qf"] * p["e_b"]
    p["kb"] = p["kk"] * p["e_bb"]
    return p


def _chunk_masks():
    t = lax.broadcasted_iota(jnp.int32, (HGRN_ROWS, HGRN_ROWS), 0)
    s = lax.broadcasted_iota(jnp.int32, (HGRN_ROWS, HGRN_ROWS), 1)
    tril = ((t // HGRN_CHUNK) == (s // HGRN_CHUNK)) & (s <= t)
    n_chunks = HGRN_ROWS // HGRN_CHUNK
    tt = lax.broadcasted_iota(jnp.int32, (HGRN_ROWS, n_chunks * LANES), 0)
    cc = lax.broadcasted_iota(jnp.int32, (HGRN_ROWS, n_chunks * LANES), 1)
    block = (tt // HGRN_CHUNK) == (cc // LANES)
    return tril, block


def _spread(x, block):
    n_chunks = HGRN_ROWS // HGRN_CHUNK
    return jnp.where(block, jnp.tile(x, (1, n_chunks)), jnp.zeros((), x.dtype))


def _fold(x_full, block):
    n_chunks = HGRN_ROWS // HGRN_CHUNK
    z = jnp.where(block, x_full, 0.0)
    acc = z[:, :LANES]
    for n in range(1, n_chunks):
        acc = acc + z[:, n * LANES:(n + 1) * LANES]
    return acc


def hgrn_fwd(hg, lb_logits, hnw, weight_halves=()):
    seq = hg.shape[0]
    nblk = seq // HGRN_ROWS
    n_chunks = HGRN_ROWS // HGRN_CHUNK
    n_w = len(weight_halves)

    def body(*refs):
        hg_ref, lbl_ref, hnw_ref = refs[:3]
        w_refs = refs[3:3 + n_w]
        yb_ref, o_ref, st0_ref = refs[3 + n_w:6 + n_w]
        g_refs = refs[6 + n_w:6 + 2 * n_w]
        st_scr = refs[6 + 2 * n_w]
        step = pl.program_id(0)
        if n_w:
            start, forward, finish = _gather_phases(w_refs, g_refs, *refs[7 + 2 * n_w:])
            pl.when(step == 0)(start)
            pl.when(step == (3 * nblk) // 4)(forward)

        @pl.when(step == 0)
        def _():
            st_scr[...] = jnp.zeros_like(st_scr)

        hg_v = hg_ref[...]
        p = _hgrn_prep(hg_v, lbl_ref[...])
        tril, block = _chunk_masks()
        vv = hg_v[:, 2 * HGRN_WIDTH:3 * HGRN_WIDTH].astype(BF16)
        outs = []
        for h in range(HGRN_HEADS):
            sl = slice(h * LANES, (h + 1) * LANES)
            v_h = vv[:, sl]
            a = jnp.where(tril, _dot_nt(p["qi"][:, sl].astype(BF16), p["ki"][:, sl].astype(BF16)), 0.0)
            o = _dot(a.astype(BF16), v_h)
            upd = _dot_tn(v_h, _spread(p["kb"][:, sl].astype(BF16), block))
            st = st_scr[h]
            st0_ref[h] = st
            parts = []
            for n in range(n_chunks):
                parts.append(st.astype(BF16))
                decay = p["e_tot"][n * HGRN_CHUNK:n * HGRN_CHUNK + 1, sl]
                st = st * decay + upd[:, n * LANES:(n + 1) * LANES]
            st_scr[h] = st
            o = o + _dot_nt(_spread(p["qs"][:, sl].astype(BF16), block), jnp.concatenate(parts, axis=1))
            outs.append(o)
        o_all = jnp.concatenate(outs, axis=1)
        o_ref[...] = o_all
        normed = jnp.concatenate(
            [outs[h] * _rms(outs[h]) for h in range(HGRN_HEADS)], axis=1)
        gb = p["gb"]
        yb_ref[...] = (normed * hnw_ref[...]) * (gb * _sigmoid(gb))
        if n_w:
            pl.when(step == nblk - 1)(finish)

    anywhere = pl.BlockSpec(memory_space=pl.ANY)
    return pl.pallas_call(
        body, name="hgrn_fwd", grid=(nblk,),
        in_specs=[_rows(HGRN_ROWS, 4 * HGRN_WIDTH), _full((2, HGRN_WIDTH)), _full((1, HGRN_WIDTH))]
        + [anywhere] * n_w,
        out_specs=[_rows(HGRN_ROWS, HGRN_WIDTH), _rows(HGRN_ROWS, HGRN_WIDTH),
                   pl.BlockSpec((None, HGRN_HEADS, LANES, LANES), lambda i: (i, 0, 0, 0))] + [anywhere] * n_w,
        out_shape=[jax.ShapeDtypeStruct((seq, HGRN_WIDTH), F32)] * 2
        + [jax.ShapeDtypeStruct((nblk, HGRN_HEADS, LANES, LANES), F32)]
        + [_gathered_shape(h) for h in weight_halves],
        scratch_shapes=[pltpu.VMEM((HGRN_HEADS, LANES, LANES), F32)] + (_gather_scratch(n_w) if n_w else []),
        compiler_params=_cparams("arbitrary"),
    )(hg, lb_logits, hnw, *weight_halves)


def ffn_fwd(outs, lses, yb, x, w_out, norm2_w, w_gu4, w_down, final_w, target):
    seq = x.shape[0]
    tm = ROW_TILE
    cw = w_gu4.shape[2]
    inv_d = 1.0 / D_MODEL

    def body(o1, o2, o3, l1, l2, l3, yb_ref, x_ref, wo_ref, nw_ref, wgu_ref, wd_ref, fw_ref, t_ref,
             ya_ref, lse_ref, mixed_ref, h1_ref, u2_ref, g_ref, up_ref, act_ref, dh2_ref, acc_ref):
        @pl.when(pl.program_id(0) == 0)
        def _():
            acc_ref[...] = jnp.zeros_like(acc_ref)

        l1v, l2v, l3v = l1[...], l2[...], l3[...]
        mx = jnp.maximum(jnp.maximum(l1v, l2v), l3v)
        e1, e2, e3 = jnp.exp(l1v - mx), jnp.exp(l2v - mx), jnp.exp(l3v - mx)
        den = e1 + e2 + e3
        ya = (e1 * o1[...] + e2 * o2[...] + e3 * o3[...]) / den
        ya_ref[...] = ya
        lse_ref[...] = mx + jnp.log(den)
        mixed = jnp.concatenate([ya, yb_ref[...]], axis=1).astype(BF16)
        mixed_ref[...] = mixed
        h1 = x_ref[...] + _dot(mixed, wo_ref[...])
        h1_ref[...] = h1
        u2 = ((h1 * _rms(h1)) * nw_ref[...]).astype(BF16)
        u2_ref[...] = u2
        g = jnp.concatenate([_dot(u2, wgu_ref[0]), _dot(u2, wgu_ref[1])], axis=1)
        up = jnp.concatenate([_dot(u2, wgu_ref[2]), _dot(u2, wgu_ref[3])], axis=1)
        g_ref[...] = g.astype(BF16)
        up_ref[...] = up.astype(BF16)
        act = ((g * _sigmoid(g)) * up).astype(BF16)
        act_ref[...] = act
        h2 = h1 + _dot(act, wd_ref[...])
        rf = _rms(h2)
        n = h2 * rf
        fw = fw_ref[...]
        err = n * fw - t_ref[...]
        dy = err * inv_d
        acc_ref[0:1, :] += jnp.sum(dy * n, axis=0, keepdims=True)
        acc_ref[1:2, :] += (0.5 * inv_d) * jnp.sum(err * err, axis=0, keepdims=True)
        dn = dy * fw
        dh2_ref[...] = rf * (dn - n * jnp.mean(dn * n, axis=-1, keepdims=True))

    half = _rows(tm, ATTN_WIDTH)
    wide = _rows(tm, D_MODEL)
    ffn = _rows(tm, FFN_HIDDEN)
    return pl.pallas_call(
        body, name="ffn_fwd", grid=(seq // tm,),
        in_specs=[half] * 7 + [wide, _weight((D_MODEL, D_MODEL)), _full((1, D_MODEL)),
                               _weight((N_CHIPS, D_MODEL, cw)), _weight((FFN_HIDDEN, D_MODEL)),
                               _full((1, D_MODEL)), wide],
        out_specs=[half, half, wide, wide, wide, ffn, ffn, ffn, wide, _full((8, D_MODEL))],
        out_shape=[jax.ShapeDtypeStruct((seq, ATTN_WIDTH), F32)] * 2
        + [jax.ShapeDtypeStruct((seq, D_MODEL), BF16), jax.ShapeDtypeStruct((seq, D_MODEL), F32),
           jax.ShapeDtypeStruct((seq, D_MODEL), BF16)]
        + [jax.ShapeDtypeStruct((seq, FFN_HIDDEN), BF16)] * 3
        + [jax.ShapeDtypeStruct((seq, D_MODEL), F32), jax.ShapeDtypeStruct((8, D_MODEL), F32)],
        compiler_params=_cparams("arbitrary"),
    )(*outs, *lses, yb, x, w_out, norm2_w, w_gu4, w_down, final_w, target)


def _head_sum_matrix():
    i = jnp.arange(ATTN_WIDTH)
    return ((i[:, None] // HEAD_DIM) == (i[None, :] // HEAD_DIM)).astype(BF16)


def ffn_bwd(dh2, w_down, g, up, w_gu4, h1, norm2_w, w_out, ya):
    seq = h1.shape[0]
    tm = ROW_TILE
    cw = w_gu4.shape[2]
    hsum = _head_sum_matrix()

    def body(dh2_ref, wd_ref, g_ref, up_ref, w_ref, h1_ref, nw_ref, wo_ref, ya_ref, hs_ref,
             dgu_ref, dh1_ref, dya_ref, dyb_ref, delta_ref, acc_ref):
        @pl.when(pl.program_id(0) == 0)
        def _():
            acc_ref[...] = jnp.zeros_like(acc_ref)

        dact = _dot_nt(dh2_ref[...].astype(BF16), wd_ref[...])
        gv = g_ref[...].astype(F32)
        sg = _sigmoid(gv)
        dgu_ref[:, :FFN_HIDDEN] = (dact * up_ref[...].astype(F32) * (sg * (1.0 + gv * (1.0 - sg)))).astype(BF16)
        dgu_ref[:, FFN_HIDDEN:] = (dact * (gv * sg)).astype(BF16)
        du2 = _dot_nt(dgu_ref[:, :cw], w_ref[0])
        for j in range(1, N_CHIPS):
            du2 = du2 + _dot_nt(dgu_ref[:, j * cw:(j + 1) * cw], w_ref[j])
        h1 = h1_ref[...]
        r2 = _rms(h1)
        nh = h1 * r2
        acc_ref[0:1, :] += jnp.sum(du2 * nh, axis=0, keepdims=True)
        dn = du2 * nw_ref[...]
        dh1 = dh2_ref[...] + r2 * (dn - nh * jnp.mean(dn * nh, axis=-1, keepdims=True))
        dh1_ref[...] = dh1
        dmixed = _dot_nt(dh1.astype(BF16), wo_ref[...])
        dya = dmixed[:, :ATTN_WIDTH]
        dya_ref[...] = dya
        dyb_ref[...] = dmixed[:, ATTN_WIDTH:]
        prod = dya * ya_ref[...]
        hi = prod.astype(BF16)
        lo = (prod - hi.astype(F32)).astype(BF16)
        delta_ref[...] = _dot(hi, hs_ref[...]) + _dot(lo, hs_ref[...])

    wide = _rows(tm, D_MODEL)
    half = _rows(tm, ATTN_WIDTH)
    ffn = _rows(tm, FFN_HIDDEN)
    return pl.pallas_call(
        body, name="ffn_bwd", grid=(seq // tm,),
        in_specs=[wide, _weight((FFN_HIDDEN, D_MODEL)), ffn, ffn, _weight((N_CHIPS, D_MODEL, cw)), wide,
                  _full((1, D_MODEL)), _weight((D_MODEL, D_MODEL)), half, _full((ATTN_WIDTH, ATTN_WIDTH))],
        out_specs=[_rows(tm, 2 * FFN_HIDDEN), wide, half, half, half, _full((8, D_MODEL))],
        out_shape=[jax.ShapeDtypeStruct((seq, 2 * FFN_HIDDEN), BF16), jax.ShapeDtypeStruct((seq, D_MODEL), F32)]
        + [jax.ShapeDtypeStruct((seq, ATTN_WIDTH), F32)] * 3 + [jax.ShapeDtypeStruct((8, D_MODEL), F32)],
        compiler_params=_cparams("arbitrary"),
    )(dh2, w_down, g, up, w_gu4, h1, norm2_w, w_out, ya, hsum)


def attn_bwd(q, k, v, dy, lse, delta, dilation, name):
    seq = q.shape[0]
    span, per_step, cur, prev = _attn_specs(seq, dilation)
    whole = pl.BlockSpec((seq, LANES), lambda hp, j: (0, hp))

    def body(q_ref, dy_ref, lse_ref, dl_ref, kc_ref, vc_ref, kp_ref, vp_ref, dq_ref, dk_ref, dv_ref):
        first_step = pl.program_id(1) == 0
        base = pl.program_id(1) * ATTN_STEP_ROWS
        masks = _head_masks()

        def block(it, carry):
            t, r = it // dilation, it % dilation
            rows = _strided_rows(r + span * t, dilation)
            at_edge = t == 0
            q2, dy2 = q_ref[rows, :].astype(BF16), dy_ref[rows, :].astype(BF16)
            lse2, dl2 = lse_ref[rows, :], dl_ref[rows, :]
            kp = _block_before(kp_ref, kc_ref, t, r, span, dilation, per_step)
            vp = _block_before(vp_ref, vc_ref, t, r, span, dilation, per_step)
            k2 = jnp.concatenate([kp, kc_ref[rows, :]], axis=0).astype(BF16)
            v2 = jnp.concatenate([vp, vc_ref[rows, :]], axis=0).astype(BF16)
            valid = _window_valid(first_step & at_edge)
            zero = jnp.zeros_like(q2)
            qms, dyms, ps, dss, kms = [], [], [], [], []
            for h, mh in enumerate(masks):
                c0 = h * HEAD_DIM
                qm, dym = jnp.where(mh, q2, zero), jnp.where(mh, dy2, zero)
                s = _dot_nt(qm, k2) * ATTN_SCALE
                p = jnp.where(valid, jnp.exp(s - lse2[:, c0:c0 + 1]), 0.0)
                dp = _dot_nt(dym, v2)
                dss.append((p * (dp - dl2[:, c0:c0 + 1]) * ATTN_SCALE).astype(BF16))
                ps.append(p.astype(BF16))
                qms.append(qm)
                dyms.append(dym)
                kms.append(jnp.where(mh, k2, jnp.zeros_like(k2)))
            dq_ref[rows, :] = _dot(jnp.concatenate(dss, axis=1), jnp.concatenate(kms, axis=0))
            dv_full = _dot_tn(jnp.concatenate(ps, axis=0), jnp.concatenate(dyms, axis=0))
            dk_full = _dot_tn(jnp.concatenate(dss, axis=0), jnp.concatenate(qms, axis=0))
            here = _strided_rows(base + r + span * t, dilation)
            dk_ref[here, :] = dk_full[ATTN_BLOCK:]
            dv_ref[here, :] = dv_full[ATTN_BLOCK:]

            back = _strided_rows(jnp.maximum(base + r + span * t - span, r), dilation)
            dk_ref[back, :] += dk_full[:ATTN_BLOCK]
            dv_ref[back, :] += dv_full[:ATTN_BLOCK]
            return carry

        lax.fori_loop(0, per_step * dilation, block, 0, unroll=ATTN_BWD_UNROLL)

    return pl.pallas_call(
        body, name=name, grid=(ATTN_WIDTH // LANES, seq // ATTN_STEP_ROWS),
        in_specs=[cur] * 6 + [prev, prev],
        out_specs=[cur, whole, whole],
        out_shape=[jax.ShapeDtypeStruct((seq, ATTN_WIDTH), F32)] * 3,
        compiler_params=_cparams("parallel", "arbitrary"),
    )(q, dy, lse, delta, k, v, k, v)


def hgrn_bwd(hg, lb_logits, hnw, o_pre, st0, dyb, chip_sums=()):
    seq = hg.shape[0]
    nblk = seq // HGRN_ROWS
    n_chunks = HGRN_ROWS // HGRN_CHUNK
    w = HGRN_WIDTH
    n_s = len(chip_sums)

    def body(*refs):
        hg_ref, lbl_ref, hnw_ref, o_ref, st0_ref, dyb_ref = refs[:6]
        dhg_ref, acc_ref = refs[6 + n_s:8 + n_s]
        dst_scr = refs[8 + 2 * n_s]
        step = pl.program_id(0)
        if n_s:
            start, finish = _chip_exchange_phases(refs[6:6 + n_s], refs[8 + n_s:8 + 2 * n_s], *refs[9 + 2 * n_s:])
            pl.when(step == 0)(start)

        @pl.when(step == 0)
        def _():
            dst_scr[...] = jnp.zeros_like(dst_scr)
            acc_ref[...] = jnp.zeros_like(acc_ref)

        hg_v = hg_ref[...]
        p = _hgrn_prep(hg_v, lbl_ref[...])
        tril, block = _chunk_masks()
        vv = hg_v[:, 2 * w:3 * w].astype(BF16)
        hnw_v = hnw_ref[...]
        gb = p["gb"]
        sgg = _sigmoid(gb)
        silu_g = gb * sgg
        dyb_v = dyb_ref[...]
        o_v = o_ref[...]

        d_on = dyb_v * hnw_v * silu_g
        on_parts, do_parts = [], []
        for h in range(HGRN_HEADS):
            sl = slice(h * LANES, (h + 1) * LANES)
            rs = _rms(o_v[:, sl])
            on = o_v[:, sl] * rs
            on_parts.append(on)
            do_parts.append(rs * (d_on[:, sl] - on * jnp.mean(d_on[:, sl] * on, axis=-1, keepdims=True)))
        on_all = jnp.concatenate(on_parts, axis=1)
        dgb = dyb_v * on_all * hnw_v * (sgg * (1.0 + gb * (1.0 - sgg)))
        acc_ref[0:1, :] += jnp.sum(dyb_v * on_all * silu_g, axis=0, keepdims=True)

        dqf_parts, dkk_parts, db_parts, dv_parts, dbt_parts, dkbkb_parts = [], [], [], [], [], []
        for h in range(HGRN_HEADS):
            sl = slice(h * LANES, (h + 1) * LANES)
            v_h = vv[:, sl]
            do_h = do_parts[h].astype(BF16)
            qi, ki, qs, kb = p["qi"][:, sl], p["ki"][:, sl], p["qs"][:, sl], p["kb"][:, sl]
            qi_b, ki_b = qi.astype(BF16), ki.astype(BF16)
            kb_cat = _spread(kb.astype(BF16), block)
            qs_cat = _spread(qs.astype(BF16), block)
            upd = _dot_tn(v_h, kb_cat)
            st = st0_ref[h]
            st_parts = []
            for n in range(n_chunks):
                st_parts.append(st)
                decay = p["e_tot"][n * HGRN_CHUNK:n * HGRN_CHUNK + 1, sl]
                st = st * decay + upd[:, n * LANES:(n + 1) * LANES]
            st_cat = jnp.concatenate([s_.astype(BF16) for s_ in st_parts], axis=1)
            wgt = _dot_tn(do_h, qs_cat)
            dst = dst_scr[h]
            dst_parts = [None] * n_chunks
            dbt_rows = [None] * n_chunks
            for n in reversed(range(n_chunks)):
                dst_parts[n] = dst.astype(BF16)
                decay = p["e_tot"][n * HGRN_CHUNK:n * HGRN_CHUNK + 1, sl]
                dbt_rows[n] = jnp.sum(dst * st_parts[n], axis=0, keepdims=True) * decay
                dst = dst * decay + wgt[:, n * LANES:(n + 1) * LANES]
            dst_scr[h] = dst
            dst_cat = jnp.concatenate(dst_parts, axis=1)
            dqs = _fold(_dot(do_h, st_cat), block)
            dkb = _fold(_dot(v_h, dst_cat), block)
            dv_state = _dot_nt(kb_cat, dst_cat)
            a = jnp.where(tril, _dot_nt(qi_b, ki_b), 0.0).astype(BF16)
            da = jnp.where(tril, _dot_nt(do_h, v_h), 0.0).astype(BF16)
            dv_parts.append(_dot_tn(a, do_h) + dv_state)
            dqi = _dot(da, ki_b)
            dki = _dot_tn(da, qi_b)
            dqf_parts.append(dqi * p["e_iq"][:, sl] + dqs * p["e_b"][:, sl])
            dkk_parts.append(dki * p["e_ik"][:, sl] + dkb * p["e_bb"][:, sl])
            dkbkb = dkb * kb
            db_parts.append(dqi * qi - dki * ki + dqs * qs - dkbkb)
            dkbkb_parts.append(dkbkb)
            dbt_parts.append(_chunk_rows(dbt_rows))

        cat = lambda parts: jnp.concatenate(parts, axis=1)
        dlogf = (_chunk_cumsum(cat(db_parts), reverse=True)
                 + _chunk_row(_chunk_cumsum(cat(dkbkb_parts)), HGRN_CHUNK - 1) + cat(dbt_parts))
        sq, qb = p["sq"], hg_v[:, :w]
        dqb = cat(dqf_parts) * (sq * (1.0 + qb * (1.0 - sq)))
        df = dlogf / p["f"] - cat(dkk_parts)
        sg, lb = p["sg"], p["lb"]
        dfb = df * (1.0 - lb) * sg * (1.0 - sg)
        acc_ref[1:2, :] += jnp.sum(df * (1.0 - sg), axis=0, keepdims=True)
        dhg_ref[...] = jnp.concatenate([dqb, dfb, cat(dv_parts), dgb], axis=1)
        if n_s:
            pl.when(step == nblk - 1)(finish)

    rev = lambda i: (nblk - 1 - i, 0)
    anywhere = pl.BlockSpec(memory_space=pl.ANY)
    return pl.pallas_call(
        body, name="hgrn_bwd", grid=(nblk,),
        in_specs=[pl.BlockSpec((HGRN_ROWS, 4 * w), rev), _full((2, w)), _full((1, w)),
                  pl.BlockSpec((HGRN_ROWS, w), rev),
                  pl.BlockSpec((None, HGRN_HEADS, LANES, LANES), lambda i: (nblk - 1 - i, 0, 0, 0)),
                  pl.BlockSpec((HGRN_ROWS, w), rev)] + [anywhere] * n_s,
        out_specs=[pl.BlockSpec((HGRN_ROWS, 4 * w), rev), _full((8, w))] + [anywhere] * n_s,
        out_shape=[jax.ShapeDtypeStruct((seq, 4 * w), F32), jax.ShapeDtypeStruct((8, w), F32)]
        + [jax.ShapeDtypeStruct(s.shape, s.dtype) for s in chip_sums],
        scratch_shapes=[pltpu.VMEM((HGRN_HEADS, LANES, LANES), F32)] + (_chip_exchange_scratch(n_s) if n_s else []),
        compiler_params=_cparams("arbitrary"),
    )(hg, lb_logits, hnw, o_pre, st0, dyb, *chip_sums)


def in_bwd(dqs, dks, dvs, dhg, cos_t, sin_t, w_in4, x, norm1_w, dh1):
    seq = x.shape[0]
    tm = ROW_TILE
    cw = w_in4.shape[2]

    def body(dq1, dq2, dq3, dk1, dk2, dk3, dv1, dv2, dv3, dhg_ref, cos_ref, sin_ref, w_ref,
             x_ref, nw_ref, dh1_ref, dproj_ref, dx_ref, acc_ref):
        @pl.when(pl.program_id(0) == 0)
        def _():
            acc_ref[...] = jnp.zeros_like(acc_ref)

        cos, sin = cos_ref[...], sin_ref[...]
        dqa = _rotary_bwd(dq1[...] + dq2[...] + dq3[...], cos, sin)
        dka = _rotary_bwd(dk1[...] + dk2[...] + dk3[...], cos, sin)
        dva = dv1[...] + dv2[...] + dv3[...]
        dproj = jnp.concatenate([dqa, dka, dva, dhg_ref[...]], axis=1).astype(BF16)
        dproj_ref[...] = dproj
        du = _dot_nt(dproj[:, :cw], w_ref[0])
        for j in range(1, N_CHIPS):
            du = du + _dot_nt(dproj[:, j * cw:(j + 1) * cw], w_ref[j])
        xv = x_ref[...]
        r1 = _rms(xv)
        nx = xv * r1
        acc_ref[0:1, :] += jnp.sum(du * nx, axis=0, keepdims=True)
        dn = du * nw_ref[...]
        dx_ref[...] = dh1_ref[...] + r1 * (dn - nx * jnp.mean(dn * nx, axis=-1, keepdims=True))

    half = _rows(tm, ATTN_WIDTH)
    wide = _rows(tm, D_MODEL)
    return pl.pallas_call(
        body, name="in_bwd", grid=(seq // tm,),
        in_specs=[half] * 9 + [_rows(tm, 4 * HGRN_WIDTH), _rows(tm, LANES), _rows(tm, LANES),
                               _weight((N_CHIPS, D_MODEL, cw)), wide, _full((1, D_MODEL)), wide],
        out_specs=[_rows(tm, IN_PROJ_WIDTH), wide, _full((8, D_MODEL))],
        out_shape=[jax.ShapeDtypeStruct((seq, IN_PROJ_WIDTH), BF16), jax.ShapeDtypeStruct((seq, D_MODEL), F32),
                   jax.ShapeDtypeStruct((8, D_MODEL), F32)],
        compiler_params=_cparams("arbitrary"),
    )(*dqs, *dks, *dvs, dhg, cos_t, sin_t, w_in4, x, norm1_w, dh1)


def weight_grad(a, b, col_block, name):
    seq, kdim = a.shape
    ndim = b.shape[1]
    nj = ndim // col_block
    tk = 512

    def body(a_ref, b_ref, o_ref):
        @pl.when(pl.program_id(1) == 0)
        def _():
            o_ref[...] = jnp.zeros_like(o_ref)

        o_ref[...] += _dot_tn(a_ref[...].astype(BF16), b_ref[...].astype(BF16))

    return pl.pallas_call(
        body, name=name, grid=(nj, seq // tk),
        in_specs=[pl.BlockSpec((tk, kdim), lambda j, t: (t, 0)),
                  pl.BlockSpec((tk, col_block), lambda j, t: (t, j))],
        out_specs=pl.BlockSpec((None, kdim, col_block), lambda j, t: (j, 0, 0)),
        out_shape=jax.ShapeDtypeStruct((nj, kdim, col_block), F32),
        compiler_params=_cparams("parallel", "arbitrary"),
    )(a, b)


def exchange_with_sibling(grads, name):
    n = len(grads)

    def body(*refs):
        g_refs, out_refs = refs[:n], refs[n:2 * n]
        send_sems, recv_sems = refs[2 * n], refs[2 * n + 1]
        x, y, cc = _mesh_pos()
        copies = []
        for i in range(n):
            for j in range(N_CHIPS):
                k = i * N_CHIPS + j
                copies.append(pltpu.make_async_remote_copy(
                    src_ref=g_refs[i].at[j, 1 - cc], dst_ref=out_refs[i].at[j],
                    send_sem=send_sems.at[k], recv_sem=recv_sems.at[k],
                    device_id=(x, y, 1 - cc), device_id_type=MESH_ID))
        for cp in copies:
            cp.start()
        for cp in copies:
            cp.wait_recv()
        for cp in copies:
            cp.wait_send()

    return pl.pallas_call(
        body, name=name,
        in_specs=[pl.BlockSpec(memory_space=pl.ANY)] * n,
        out_specs=[pl.BlockSpec(memory_space=pl.ANY)] * n,
        out_shape=[jax.ShapeDtypeStruct((N_CHIPS,) + g.shape[2:], g.dtype) for g in grads],
        scratch_shapes=[pltpu.SemaphoreType.DMA((n * N_CHIPS,)), pltpu.SemaphoreType.DMA((n * N_CHIPS,))],
    )(*grads)


def add_own_half(grad, recv, name):
    _, _, r, c = grad.shape
    tr = r // 2 if r % 32 == 0 else r

    def body(cc_ref, g_ref, r_ref, o_ref):
        o_ref[...] = (g_ref[...] + r_ref[...]).astype(BF16)

    grid_spec = pltpu.PrefetchScalarGridSpec(
        num_scalar_prefetch=1, grid=(N_CHIPS, r // tr),
        in_specs=[pl.BlockSpec((None, None, tr, c), lambda j, t, cc: (j, cc[0], t, 0)),
                  pl.BlockSpec((None, tr, c), lambda j, t, cc: (j, t, 0))],
        out_specs=pl.BlockSpec((None, tr, c), lambda j, t, cc: (j, t, 0)))
    cc = lax.axis_index("c").astype(jnp.int32).reshape(1)
    return pl.pallas_call(
        body, name=name, grid_spec=grid_spec,
        out_shape=jax.ShapeDtypeStruct((N_CHIPS, r, c), BF16),
        compiler_params=_cparams("parallel", "parallel"),
    )(cc, grad, recv)


def _chip_exchange_phases(s_refs, out_refs, send_sems, recv_sems):
    n = len(s_refs)
    x, y, cc = _mesh_pos()
    my_chip = 2 * x + y
    chips = [(1 - x, y), (x, 1 - y), (1 - x, 1 - y)]

    def outgoing():
        return [pltpu.make_async_remote_copy(
            src_ref=s_refs[i].at[2 * px + py], dst_ref=out_refs[i].at[my_chip],
            send_sem=send_sems.at[3 * i + j], recv_sem=recv_sems.at[3 * i + j],
            device_id=(px, py, cc), device_id_type=MESH_ID)
            for i in range(n) for j, (px, py) in enumerate(chips)]

    def start():
        for cp in outgoing():
            cp.start()

    def finish():
        for i in range(n):
            for j, (px, py) in enumerate(chips):
                pltpu.make_async_remote_copy(
                    src_ref=s_refs[i].at[my_chip], dst_ref=out_refs[i].at[2 * px + py],
                    send_sem=send_sems.at[3 * i + j], recv_sem=recv_sems.at[3 * i + j],
                    device_id=(px, py, cc), device_id_type=MESH_ID).wait_recv()
        for cp in outgoing():
            cp.wait_send()

    return start, finish


def _chip_exchange_scratch(n):
    return [pltpu.SemaphoreType.DMA((3 * n,)), pltpu.SemaphoreType.DMA((3 * n,))]


def exchange_between_chips(sums):
    n = len(sums)

    def body(*refs):
        start, finish = _chip_exchange_phases(refs[:n], refs[n:2 * n], refs[2 * n], refs[2 * n + 1])
        start()
        finish()

    return pl.pallas_call(
        body, name="grad_exchange_chips",
        in_specs=[pl.BlockSpec(memory_space=pl.ANY)] * n,
        out_specs=[pl.BlockSpec(memory_space=pl.ANY)] * n,
        out_shape=[jax.ShapeDtypeStruct(s.shape, s.dtype) for s in sums],
        scratch_shapes=_chip_exchange_scratch(n),
    )(*sums)


def sum_chips(sums, parts, name):
    _, r, c = parts.shape
    tr = r // 2 if r % 32 == 0 else r

    def body(idx_ref, s_ref, p1_ref, p2_ref, p3_ref, o_ref):
        o_ref[...] = ((s_ref[...].astype(F32) + p1_ref[...].astype(F32))
                      + p2_ref[...].astype(F32)) + p3_ref[...].astype(F32)

    def pick(k):
        return pl.BlockSpec((None, tr, c), lambda t, idx: (idx[k], t, 0))

    x, y = lax.axis_index("x"), lax.axis_index("y")
    idx = jnp.stack([2 * x + y, 2 * (1 - x) + y, 2 * x + (1 - y), 2 * (1 - x) + (1 - y)]).astype(jnp.int32)
    grid_spec = pltpu.PrefetchScalarGridSpec(
        num_scalar_prefetch=1, grid=(r // tr,),
        in_specs=[pick(0), pick(1), pick(2), pick(3)],
        out_specs=pl.BlockSpec((tr, c), lambda t, idx: (t, 0)))
    return pl.pallas_call(
        body, name=name, grid_spec=grid_spec,
        out_shape=jax.ShapeDtypeStruct((r, c), F32),
        compiler_params=_cparams("parallel"),
    )(idx, sums, parts, parts, parts)


def share_with_sibling(halves):
    n = len(halves)

    def body(*refs):
        h_refs, out_refs = refs[:n], refs[n:2 * n]
        send_sems, recv_sems = refs[2 * n], refs[2 * n + 1]
        x, y, cc = _mesh_pos()
        copies = [pltpu.make_async_remote_copy(
            src_ref=h_refs[i], dst_ref=out_refs[i],
            send_sem=send_sems.at[i], recv_sem=recv_sems.at[i],
            device_id=(x, y, 1 - cc), device_id_type=MESH_ID) for i in range(n)]
        for cp in copies:
            cp.start()
        for cp in copies:
            cp.wait_recv()
        for cp in copies:
            cp.wait_send()

    return pl.pallas_call(
        body, name="grad_share_sibling",
        in_specs=[pl.BlockSpec(memory_space=pl.ANY)] * n,
        out_specs=[pl.BlockSpec(memory_space=pl.ANY)] * n,
        out_shape=[jax.ShapeDtypeStruct(h.shape, h.dtype) for h in halves],
        scratch_shapes=[pltpu.SemaphoreType.DMA((n,)), pltpu.SemaphoreType.DMA((n,))],
    )(*halves)


def _adam_update(w, g, m, v):
    m = ADAM_B1 * m + (1.0 - ADAM_B1) * g
    v = ADAM_B2 * v + (1.0 - ADAM_B2) * (g * g)
    m_hat = m / (1.0 - ADAM_B1 ** ADAM_STEP)
    v_hat = v / (1.0 - ADAM_B2 ** ADAM_STEP)
    delta = -ADAM_LR * (m_hat / (jnp.sqrt(v_hat) + ADAM_EPS) + ADAM_WD * w)
    return delta, m, v


def adamw(w, g_mine, g_sibling, m, v, name):
    r, c = w.shape
    half = r // 2
    tr = half // 2 if half % 16 == 0 else half
    nt = half // tr

    def body(cc_ref, w_ref, ga_ref, gb_ref, m_ref, v_ref, g_ref, d_ref, nm_ref, nv_ref):
        g = jnp.where(pl.program_id(0) == cc_ref[0], ga_ref[...], gb_ref[...])
        g_ref[...] = g
        d, nm, nv = _adam_update(w_ref[...], g, m_ref[...], v_ref[...])
        d_ref[...] = d
        nm_ref[...] = nm
        nv_ref[...] = nv

    full = pl.BlockSpec((tr, c), lambda h, t, cc: (h * nt + t, 0))
    part = pl.BlockSpec((tr, c), lambda h, t, cc: (t, 0))
    grid_spec = pltpu.PrefetchScalarGridSpec(
        num_scalar_prefetch=1, grid=(2, nt),
        in_specs=[full, part, part, full, full], out_specs=[full] * 4)
    cc = lax.axis_index("c").astype(jnp.int32).reshape(1)
    return pl.pallas_call(
        body, name=name, grid_spec=grid_spec,
        out_shape=[jax.ShapeDtypeStruct((r, c), F32)] * 4,
        compiler_params=_cparams("parallel", "parallel"),
    )(cc, w, g_mine, g_sibling, m, v)


def small_allreduce(pack):
    def body(p_ref, o_ref, gather, send_sems, recv_sems):
        x, y, cc = _mesh_pos()
        me = 4 * x + 2 * y + cc
        gather[me] = p_ref[...]
        flips = [(fx, fy, fc) for fx in (0, 1) for fy in (0, 1) for fc in (0, 1)][1:]
        copies = []
        for k, (fx, fy, fc) in enumerate(flips):
            copies.append(pltpu.make_async_remote_copy(
                src_ref=p_ref, dst_ref=gather.at[me],
                send_sem=send_sems.at[k], recv_sem=recv_sems.at[k],
                device_id=(x ^ fx, y ^ fy, cc ^ fc), device_id_type=MESH_ID))
        for cp in copies:
            cp.start()
        for k, (fx, fy, fc) in enumerate(flips):
            src = 4 * (x ^ fx) + 2 * (y ^ fy) + (cc ^ fc)
            pltpu.make_async_remote_copy(
                src_ref=p_ref, dst_ref=gather.at[src],
                send_sem=send_sems.at[k], recv_sem=recv_sems.at[k],
                device_id=(x ^ fx, y ^ fy, cc ^ fc), device_id_type=MESH_ID).wait_recv()
        for cp in copies:
            cp.wait_send()
        total = gather[0]
        for d in range(1, N_DEV):
            total = total + gather[d]
        o_ref[...] = total

    return pl.pallas_call(
        body, name="small_allreduce",
        in_specs=[pl.BlockSpec(memory_space=pltpu.VMEM)],
        out_specs=pl.BlockSpec(memory_space=pltpu.VMEM),
        out_shape=jax.ShapeDtypeStruct(pack.shape, pack.dtype),
        scratch_shapes=[pltpu.VMEM((N_DEV,) + pack.shape, pack.dtype),
                        pltpu.SemaphoreType.DMA((7,)), pltpu.SemaphoreType.DMA((7,))],
    )(pack)


def small_update(gsum, wpack, mpack, vpack):
    hw = HGRN_WIDTH

    def body(g_ref, w_ref, m_ref, v_ref, go_ref, d_ref, nm_ref, nv_ref, loss_ref):
        g = g_ref[...]
        wv = w_ref[...]
        a0, a1 = wv[4:5, :hw], wv[4:5, hw:]
        mx = jnp.maximum(a0, a1)
        e0, e1 = jnp.exp(a0 - mx), jnp.exp(a1 - mx)
        lb = e0 / (e0 + e1)
        dl = g[4:5, :hw] * lb * (1.0 - lb)
        row = lax.broadcasted_iota(jnp.int32, g.shape, 0)
        lb_row = jnp.concatenate([dl, -dl], axis=1)
        grads = jnp.where(row == 4, lb_row, jnp.where(row < 4, g, 0.0))
        go_ref[...] = grads
        d, nm, nv = _adam_update(wv, grads, m_ref[...], v_ref[...])
        d_ref[...] = d
        nm_ref[...] = nm
        nv_ref[...] = nv
        loss_ref[...] = jnp.zeros((8, LANES), F32) + jnp.sum(g[5:6, :])

    vm = pl.BlockSpec(memory_space=pltpu.VMEM)
    return pl.pallas_call(
        body, name="small_update",
        in_specs=[vm] * 4, out_specs=[vm] * 5,
        out_shape=[jax.ShapeDtypeStruct(gsum.shape, F32)] * 4 + [jax.ShapeDtypeStruct((8, LANES), F32)],
    )(gsum, wpack, mpack, vpack)


def _pack_small(n1, n2, fn, hn, lbl):
    z = jnp.zeros((1, D_MODEL - HGRN_WIDTH), F32)
    rows = [n1.reshape(1, D_MODEL), n2.reshape(1, D_MODEL), fn.reshape(1, D_MODEL),
            jnp.concatenate([hn.reshape(1, HGRN_WIDTH), z], axis=1), lbl.reshape(1, 2 * HGRN_WIDTH),
            jnp.zeros((3, D_MODEL), F32)]
    return jnp.concatenate(rows, axis=0)


def _unpack_small(pack):
    return (pack[0:1], pack[4].reshape(2, HGRN_WIDTH), pack[3:4, :HGRN_WIDTH], pack[1:2], pack[2])


def kernel(x, norm1_w, w_in, lb_logits, hgrn_norm_w, w_out, norm2_w, w_gate_up, w_down, final_norm_w, loss_target, m_norm1_w, m_w_in, m_lb_logits, m_hgrn_norm_w, m_w_out, m_norm2_w, m_w_gate_up, m_w_down, m_final_norm_w, v_norm1_w, v_w_in, v_lb_logits, v_hgrn_norm_w, v_w_out, v_norm2_w, v_w_gate_up, v_w_down, v_final_norm_w):
    seq = x.shape[1]
    xs = x.reshape(seq, D_MODEL)
    target = loss_target.reshape(seq, D_MODEL)
    shards = {"w_in": w_in[0], "w_out": w_out[0], "w_gu": w_gate_up[0], "w_down": w_down[0]}

    cast = {k: cast_bf16(w, "cast_" + k) for k, w in shards.items()}
    w_in4 = allgather_halves(cast["w_in"], "gather_w_in").reshape(N_CHIPS, D_MODEL, -1)

    cos_t, sin_t = _rope_tables(seq)
    fw = final_norm_w.reshape(1, D_MODEL)

    qr, kr, va, hg, u = in_proj(xs, norm1_w, w_in4, cos_t, sin_t)
    fwd = [attn_fwd(qr, kr, va, d, "attn_fwd_d%d" % d) for _, d in DILATED_PAIRS]
    yb, o_pre, st0, g_out, g_gu, g_down = hgrn_fwd(
        hg, lb_logits, hgrn_norm_w, [cast["w_out"], cast["w_gu"], cast["w_down"]])
    w_out_f = g_out.reshape(D_MODEL, D_MODEL)
    w_gu4 = g_gu.reshape(N_CHIPS, D_MODEL, -1)
    w_down_f = g_down.reshape(FFN_HIDDEN, D_MODEL)
    ya, lse, mixed, h1, u2, g, up, act, dh2, acc_fin = ffn_fwd(
        [f[0] for f in fwd], [f[1] for f in fwd], yb, xs, w_out_f, norm2_w, w_gu4, w_down_f, fw, target)

    cw_in, cw_gu = w_in4.shape[2], w_gu4.shape[2]
    dgu, dh1, dya, dyb, delta, acc_n2 = ffn_bwd(dh2, w_down_f, g, up, w_gu4, h1, norm2_w, w_out_f, ya)
    early = [
        weight_grad(mixed, dh1, D_MODEL, "wgrad_out").reshape(N_CHIPS, 2, D_MODEL // 8, D_MODEL),
        weight_grad(u2, dgu, cw_gu, "wgrad_gu").reshape(N_CHIPS, 2, D_MODEL // 2, cw_gu),
        weight_grad(act, dh2, D_MODEL, "wgrad_down").reshape(N_CHIPS, 2, FFN_HIDDEN // 8, D_MODEL),
    ]
    early_names = ["out", "gu", "down"]
    early_recv = exchange_with_sibling(early, "grad_exchange_sibling_early")
    early_sums = [add_own_half(gr, rc, "add_half_" + nm) for gr, rc, nm in zip(early, early_recv, early_names)]
    dhg, acc_hg, *early_parts = hgrn_bwd(hg, lb_logits, hgrn_norm_w, o_pre, st0, dyb, early_sums)
    bwd = [attn_bwd(qr, kr, va, dya, lse, delta, d, "attn_bwd_d%d" % d) for _, d in DILATED_PAIRS]
    dproj, dx, acc_n1 = in_bwd([b[0] for b in bwd], [b[1] for b in bwd], [b[2] for b in bwd],
                               dhg, cos_t, sin_t, w_in4, xs, norm1_w, dh1)
    late = [weight_grad(u, dproj, cw_in, "wgrad_in").reshape(N_CHIPS, 2, D_MODEL // 2, cw_in)]
    late_recv = exchange_with_sibling(late, "grad_exchange_sibling_late")
    late_sums = [add_own_half(late[0], late_recv[0], "add_half_in")]
    late_parts = exchange_between_chips(late_sums)
    names = ["in"] + early_names
    sums, parts = late_sums + early_sums, list(late_parts) + list(early_parts)
    halves = [sum_chips(s, p, "sum_chips_" + nm) for s, p, nm in zip(sums, parts, names)]
    others = share_with_sibling(halves)
    big = {}
    for nm, key, mine, other, m_, v_ in zip(names, ["w_in", "w_out", "w_gu", "w_down"], halves, others,
                                            [m_w_in, m_w_out, m_w_gate_up, m_w_down],
                                            [v_w_in, v_w_out, v_w_gate_up, v_w_down]):
        big[key] = tuple(t[None] for t in adamw(shards[key], mine, other, m_[0], v_[0], "adamw_" + nm))

    z512 = jnp.zeros((1, D_MODEL - HGRN_WIDTH), F32)
    gpack = jnp.concatenate([
        acc_n1[0:1], acc_n2[0:1], acc_fin[0:1],
        jnp.concatenate([acc_hg[0:1], z512], axis=1), jnp.concatenate([acc_hg[1:2], z512], axis=1),
        acc_fin[1:2], jnp.zeros((2, D_MODEL), F32)], axis=0)
    gsum = small_allreduce(gpack)
    wpack = _pack_small(norm1_w, norm2_w, final_norm_w, hgrn_norm_w, lb_logits)
    mpack = _pack_small(m_norm1_w, m_norm2_w, m_final_norm_w, m_hgrn_norm_w, m_lb_logits)
    vpack = _pack_small(v_norm1_w, v_norm2_w, v_final_norm_w, v_hgrn_norm_w, v_lb_logits)
    gs, ds, nms, nvs, loss8 = small_update(gsum, wpack, mpack, vpack)
    loss = loss8[0, 0]

    def assemble(small_pack, idx):
        n1, lbl, hn, n2, fn = _unpack_small(small_pack)
        return (n1, big["w_in"][idx], lbl, hn, big["w_out"][idx], n2, big["w_gu"][idx], big["w_down"][idx], fn)

    return (loss, dx.reshape(x.shape), *assemble(gs, 0), *assemble(ds, 1), *assemble(nms, 2), *assemble(nvs, 3))
```

```python
import functools

import jax
import jax.numpy as jnp
from jax import lax
from jax.experimental import pallas as pl
from jax.experimental.pallas import tpu as pltpu

F32 = jnp.float32
BF16 = jnp.bfloat16

D_MODEL = 1024
ATTN_WIDTH = 512
HEAD_DIM = 64
DILATED_PAIRS = ((128, 1), (512, 4), (2048, 16))
ATTN_BLOCK = 128
ROPE_THETA = 10000.0
HGRN_WIDTH = 512
HGRN_CHUNK = 16
HGRN_HEADS = 4
IN_PROJ_WIDTH = 3584
FFN_HIDDEN = 2816
NORM_EPS = 1e-6
ATTN_SCALE = HEAD_DIM ** -0.5
N_CHIPS = 4
N_DEV = 8

ADAM_LR = 0.001
ADAM_B1 = 0.9
ADAM_B2 = 0.999
ADAM_EPS = 1e-08
ADAM_WD = 0.01
ADAM_STEP = 10

LANES = 128
HGRN_ROWS = 128
ROW_TILE = 256
ATTN_STEP_ROWS = 2048
ATTN_FWD_UNROLL = 8
ATTN_BWD_UNROLL = 8
VMEM_LIMIT = 56 * 1024 * 1024
NEG_BIG = -1e30
MESH_ID = pl.DeviceIdType.MESH


def _cparams(*sem):
    return pltpu.CompilerParams(dimension_semantics=tuple(sem), vmem_limit_bytes=VMEM_LIMIT)


def _dot(a, b):
    return jnp.dot(a, b, preferred_element_type=F32)


def _dot_nt(a, b):
    return lax.dot_general(a, b, (((1,), (1,)), ((), ())), preferred_element_type=F32)


def _dot_tn(a, b):
    return lax.dot_general(a, b, (((0,), (0,)), ((), ())), preferred_element_type=F32)


def _sigmoid(x):
    return 1.0 / (1.0 + jnp.exp(-x))


def _full(shape):
    n = len(shape)
    return pl.BlockSpec(shape, lambda *_: (0,) * n)


def _weight(shape):
    n = len(shape)
    return pl.BlockSpec(shape, lambda *_: (0,) * n, pipeline_mode=pl.Buffered(1))


def _rows(tm, width):
    return pl.BlockSpec((tm, width), lambda i: (i, 0))


def _swap32(x):
    lane = lax.broadcasted_iota(jnp.int32, x.shape, 1)
    first = (lane % HEAD_DIM) < (HEAD_DIM // 2)
    return jnp.where(first, pltpu.roll(x, LANES - 32, axis=1), pltpu.roll(x, 32, axis=1))


def _rotary_fwd(x, cos, sin_signed):
    parts = []
    for j in range(x.shape[1] // LANES):
        xc = x[:, j * LANES:(j + 1) * LANES]
        parts.append(xc * cos + _swap32(xc) * sin_signed)
    return jnp.concatenate(parts, axis=1)


def _rotary_bwd(dy, cos, sin_signed):
    parts = []
    for j in range(dy.shape[1] // LANES):
        dc = dy[:, j * LANES:(j + 1) * LANES]
        parts.append(dc * cos + _swap32(dc * sin_signed))
    return jnp.concatenate(parts, axis=1)


def _rope_tables(seq):
    half = HEAD_DIM // 2
    inv_freq = ROPE_THETA ** (-jnp.arange(half, dtype=F32) / half)
    ang = jnp.arange(seq, dtype=F32)[:, None] * inv_freq[None, :]
    cos, sin = jnp.cos(ang), jnp.sin(ang)
    cos_t = jnp.tile(cos, (1, LANES // half))
    sin_t = jnp.tile(jnp.concatenate([-sin, sin], axis=1), (1, LANES // HEAD_DIM))
    return cos_t, sin_t


def cast_bf16(w, name):
    r, c = w.shape
    half = r // 2

    def body(w_ref, o_ref):
        o_ref[...] = w_ref[...].astype(BF16)

    return pl.pallas_call(
        body, name=name, grid=(2,),
        in_specs=[pl.BlockSpec((half, c), lambda i: (i, 0))],
        out_specs=pl.BlockSpec((None, half, c), lambda i: (i, 0, 0)),
        out_shape=jax.ShapeDtypeStruct((2, half, c), BF16),
        compiler_params=_cparams("parallel"),
    )(w)


def _mesh_pos():
    return lax.axis_index("x"), lax.axis_index("y"), lax.axis_index("c")


GATHER_COPIES = 7


def _gather_phases(x_refs, out_refs, send_sems, recv_sems, local_sems):
    n = len(x_refs)
    x, y, cc = _mesh_pos()
    me, sibling = (x, y, cc), (x, y, 1 - cc)
    chips = [(1 - x, y), (x, 1 - y), (1 - x, 1 - y)]

    def rows(i, px, py, pc):
        return out_refs[i].at[4 * px + 2 * py + pc]

    def copy(i, k, block, to, src=None):
        return pltpu.make_async_remote_copy(
            src_ref=rows(i, *block) if src is None else src, dst_ref=rows(i, *block),
            send_sem=send_sems.at[GATHER_COPIES * i + k], recv_sem=recv_sems.at[GATHER_COPIES * i + k],
            device_id=to, device_id_type=MESH_ID)

    def local(i):
        return pltpu.make_async_copy(x_refs[i].at[cc], rows(i, *me), local_sems.at[i])

    def first(i):
        mine = x_refs[i].at[cc]
        return [copy(i, 0, me, sibling, src=mine)] + [
            copy(i, 1 + j, me, (*chip, cc), src=mine) for j, chip in enumerate(chips)]

    def passed(i):
        return [copy(i, 4 + j, (*chip, cc), sibling) for j, chip in enumerate(chips)]

    def start():
        for i in range(n):
            local(i).start()
            for cp in first(i):
                cp.start()

    def forward():
        for i in range(n):
            onward = passed(i)
            for j, chip in enumerate(chips):
                copy(i, 1 + j, (*chip, cc), me).wait_recv()
                onward[j].start()

    def finish():
        for i in range(n):
            copy(i, 0, sibling, me).wait_recv()
            for j, chip in enumerate(chips):
                copy(i, 4 + j, (*chip, 1 - cc), me).wait_recv()
            for cp in first(i) + passed(i):
                cp.wait_send()
            local(i).wait()

    return start, forward, finish


def _gather_scratch(n):
    return [pltpu.SemaphoreType.DMA((GATHER_COPIES * n,)), pltpu.SemaphoreType.DMA((GATHER_COPIES * n,)),
            pltpu.SemaphoreType.DMA((n,))]


def _gathered_shape(halves):
    return jax.ShapeDtypeStruct((N_DEV,) + halves.shape[1:], halves.dtype)


def allgather_halves(halves, name):
    def body(x_ref, out_ref, send_sems, recv_sems, local_sems):
        start, forward, finish = _gather_phases([x_ref], [out_ref], send_sems, recv_sems, local_sems)
        start()
        forward()
        finish()

    return pl.pallas_call(
        body, name=name,
        in_specs=[pl.BlockSpec(memory_space=pl.ANY)],
        out_specs=pl.BlockSpec(memory_space=pl.ANY),
        out_shape=_gathered_shape(halves),
        scratch_shapes=_gather_scratch(1),
    )(halves)


def _rms(x):
    return lax.rsqrt(jnp.mean(x * x, axis=-1, keepdims=True) + NORM_EPS)


def in_proj(x, norm1_w, w_in4, cos_t, sin_t, weight_halves=()):
    seq = x.shape[0]
    tm = ROW_TILE
    cw = w_in4.shape[2]
    n_w = len(weight_halves)
    steps = seq // tm

    def body(*refs):
        x_ref, nw_ref, w_ref, cos_ref, sin_ref = refs[:5]
        q_ref, k_ref, v_ref, hg_ref, u_ref = refs[5 + n_w:10 + n_w]
        step = pl.program_id(0)
        if n_w:
            start, forward, finish = _gather_phases(
                refs[5:5 + n_w], refs[10 + n_w:10 + 2 * n_w], *refs[10 + 2 * n_w:])
            pl.when(step == 0)(start)
            pl.when(step == (3 * steps) // 4)(forward)
        xv = x_ref[...]
        u = ((xv * _rms(xv)) * nw_ref[...]).astype(BF16)
        u_ref[...] = u
        proj = jnp.concatenate([_dot(u, w_ref[j]) for j in range(N_CHIPS)], axis=1)
        cos, sin = cos_ref[...], sin_ref[...]
        a = ATTN_WIDTH
        q_ref[...] = _rotary_fwd(proj[:, :a], cos, sin)
        k_ref[...] = _rotary_fwd(proj[:, a:2 * a], cos, sin)
        v_ref[...] = proj[:, 2 * a:3 * a]
        hg_ref[...] = proj[:, 3 * a:]
        if n_w:
            pl.when(step == steps - 1)(finish)

    anywhere = pl.BlockSpec(memory_space=pl.ANY)
    return pl.pallas_call(
        body, name="in_proj", grid=(steps,),
        in_specs=[_rows(tm, D_MODEL), _full((1, D_MODEL)), _weight((N_CHIPS, D_MODEL, cw)),
                  _rows(tm, LANES), _rows(tm, LANES)] + [anywhere] * n_w,
        out_specs=[_rows(tm, ATTN_WIDTH)] * 3 + [_rows(tm, 4 * HGRN_WIDTH), _rows(tm, D_MODEL)]
        + [anywhere] * n_w,
        out_shape=[jax.ShapeDtypeStruct((seq, ATTN_WIDTH), F32)] * 3
        + [jax.ShapeDtypeStruct((seq, 4 * HGRN_WIDTH), F32), jax.ShapeDtypeStruct((seq, D_MODEL), BF16)]
        + [_gathered_shape(h) for h in weight_halves],
        scratch_shapes=_gather_scratch(n_w) if n_w else [],
        compiler_params=_cparams("arbitrary"),
    )(x, norm1_w, w_in4, cos_t, sin_t, *weight_halves)


def _head_masks():
    lane = lax.broadcasted_iota(jnp.int32, (1, LANES), 1)
    return [(lane // HEAD_DIM) == h for h in range(LANES // HEAD_DIM)]


def _window_valid(no_prev):
    qi = lax.broadcasted_iota(jnp.int32, (ATTN_BLOCK, 2 * ATTN_BLOCK), 0)
    kj = lax.broadcasted_iota(jnp.int32, (ATTN_BLOCK, 2 * ATTN_BLOCK), 1)
    valid = (kj >= qi) & (kj <= qi + ATTN_BLOCK)
    return valid & (jnp.logical_not(no_prev) | (kj >= ATTN_BLOCK))


def _strided_rows(start, dilation):
    if dilation == 1:
        return pl.ds(start, ATTN_BLOCK)
    return pl.ds(start, ATTN_BLOCK, stride=dilation)


def _block_before(edge_ref, cur_ref, t, r, span, dilation, per_step):
    edge = edge_ref[_strided_rows(r, dilation), :]
    if per_step == 1:
        return edge
    inside = cur_ref[_strided_rows(r + span * jnp.maximum(t - 1, 0), dilation), :]
    return jnp.where(t == 0, edge, inside)


def _attn_specs(seq, dilation):
    span = ATTN_BLOCK * dilation
    per_step = ATTN_STEP_ROWS // span
    cur = pl.BlockSpec((ATTN_STEP_ROWS, LANES), lambda hp, j: (j, hp))
    prev = pl.BlockSpec((span, LANES), lambda hp, j: (jnp.maximum(j * per_step - 1, 0), hp))
    return span, per_step, cur, prev


def attn_fwd(q, k, v, dilation, name):
    seq = q.shape[0]
    span, per_step, cur, prev = _attn_specs(seq, dilation)

    def body(q_ref, kc_ref, vc_ref, kp_ref, vp_ref, o_ref, lse_ref):
        first_step = pl.program_id(1) == 0
        masks = _head_masks()

        def block(it, carry):
            t, r = it // dilation, it % dilation
            rows = _strided_rows(r + span * t, dilation)
            at_edge = t == 0
            q2 = q_ref[rows, :].astype(BF16)
            kp = _block_before(kp_ref, kc_ref, t, r, span, dilation, per_step)
            vp = _block_before(vp_ref, vc_ref, t, r, span, dilation, per_step)
            k2 = jnp.concatenate([kp, kc_ref[rows, :]], axis=0).astype(BF16)
            v2 = jnp.concatenate([vp, vc_ref[rows, :]], axis=0).astype(BF16)
            valid = _window_valid(first_step & at_edge)
            o_acc = jnp.zeros((ATTN_BLOCK, LANES), F32)
            l_acc = jnp.zeros((ATTN_BLOCK, LANES), F32)
            for mh in masks:
                qm = jnp.where(mh, q2, jnp.zeros_like(q2))
                s = jnp.where(valid, _dot_nt(qm, k2) * ATTN_SCALE, NEG_BIG)
                m = jnp.max(s, axis=-1, keepdims=True)
                p = jnp.exp(s - m)
                l = jnp.sum(p, axis=-1, keepdims=True)
                o = _dot(p.astype(BF16), v2) / l
                o_acc = jnp.where(mh, o, o_acc)
                l_acc = jnp.where(mh, m + jnp.log(l), l_acc)
            o_ref[rows, :] = o_acc
            lse_ref[rows, :] = l_acc
            return carry

        lax.fori_loop(0, per_step * dilation, block, 0, unroll=ATTN_FWD_UNROLL)

    return pl.pallas_call(
        body, name=name, grid=(ATTN_WIDTH // LANES, seq // ATTN_STEP_ROWS),
        in_specs=[cur, cur, cur, prev, prev],
        out_specs=[cur, cur],
        out_shape=[jax.ShapeDtypeStruct((seq, ATTN_WIDTH), F32)] * 2,
        compiler_params=_cparams("parallel", "parallel"),
    )(q, k, v, k, v)


def _chunk_cumsum(x, reverse=False):
    rc = lax.broadcasted_iota(jnp.int32, x.shape, 0) % HGRN_CHUNK
    sh = 1
    while sh < HGRN_CHUNK:
        if reverse:
            x = x + jnp.where(rc + sh < HGRN_CHUNK, pltpu.roll(x, x.shape[0] - sh, axis=0), 0.0)
        else:
            x = x + jnp.where(rc >= sh, pltpu.roll(x, sh, axis=0), 0.0)
        sh *= 2
    return x


def _chunk_row(x, row):
    return _chunk_rows([x[n * HGRN_CHUNK + row:n * HGRN_CHUNK + row + 1, :]
                        for n in range(x.shape[0] // HGRN_CHUNK)])


def _chunk_rows(rows):
    return jnp.concatenate([jnp.broadcast_to(r, (HGRN_CHUNK, r.shape[1])) for r in rows], axis=0)


def _hgrn_prep(hg, lbl):
    w = HGRN_WIDTH
    a0, a1 = lbl[0:1, :], lbl[1:2, :]
    mx = jnp.maximum(a0, a1)
    e0, e1 = jnp.exp(a0 - mx), jnp.exp(a1 - mx)
    lb = e0 / (e0 + e1)
    qb, fb, gb = hg[:, :w], hg[:, w:2 * w], hg[:, 3 * w:]
    sg = _sigmoid(fb)
    f = lb + (1.0 - lb) * sg
    b = _chunk_cumsum(jnp.log(f))
    bmid, btot = _chunk_row(b, HGRN_CHUNK // 2 - 1), _chunk_row(b, HGRN_CHUNK - 1)
    sq = _sigmoid(qb)
    p = dict(lb=lb, sg=sg, f=f, kk=1.0 - f, sq=sq, qf=qb * sq, gb=gb,
             e_iq=jnp.exp(b - bmid), e_ik=jnp.exp(bmid - b), e_b=jnp.exp(b),
             e_bb=jnp.exp(btot - b), e_tot=jnp.exp(btot))
    p["qi"] = p["qf"] * p["e_iq"]
    p["ki"] = p["kk"] * p["e_ik"]
    p["qs"] = p["qf"] * p["e_b"]
    p["kb"] = p["kk"] * p["e_bb"]
    return p


def _chunk_masks():
    t = lax.broadcasted_iota(jnp.int32, (HGRN_ROWS, HGRN_ROWS), 0)
    s = lax.broadcasted_iota(jnp.int32, (HGRN_ROWS, HGRN_ROWS), 1)
    tril = ((t // HGRN_CHUNK) == (s // HGRN_CHUNK)) & (s <= t)
    n_chunks = HGRN_ROWS // HGRN_CHUNK
    tt = lax.broadcasted_iota(jnp.int32, (HGRN_ROWS, n_chunks * LANES), 0)
    cc = lax.broadcasted_iota(jnp.int32, (HGRN_ROWS, n_chunks * LANES), 1)
    block = (tt // HGRN_CHUNK) == (cc // LANES)
    return tril, block


def _spread(x, block):
    n_chunks = HGRN_ROWS // HGRN_CHUNK
    return jnp.where(block, jnp.tile(x, (1, n_chunks)), jnp.zeros((), x.dtype))


def _fold(x_full, block):
    n_chunks = HGRN_ROWS // HGRN_CHUNK
    z = jnp.where(block, x_full, 0.0)
    acc = z[:, :LANES]
    for n in range(1, n_chunks):
        acc = acc + z[:, n * LANES:(n + 1) * LANES]
    return acc


def hgrn_fwd(hg, lb_logits, hnw, weight_halves=()):
    seq = hg.shape[0]
    nblk = seq // HGRN_ROWS
    n_chunks = HGRN_ROWS // HGRN_CHUNK
    n_w = len(weight_halves)

    def body(*refs):
        hg_ref, lbl_ref, hnw_ref = refs[:3]
        w_refs = refs[3:3 + n_w]
        yb_ref, o_ref, st0_ref = refs[3 + n_w:6 + n_w]
        g_refs = refs[6 + n_w:6 + 2 * n_w]
        st_scr = refs[6 + 2 * n_w]
        step = pl.program_id(0)
        if n_w:
            start, forward, finish = _gather_phases(w_refs, g_refs, *refs[7 + 2 * n_w:])
            pl.when(step == 0)(start)
            pl.when(step == (3 * nblk) // 4)(forward)

        @pl.when(step == 0)
        def _():
            st_scr[...] = jnp.zeros_like(st_scr)

        hg_v = hg_ref[...]
        p = _hgrn_prep(hg_v, lbl_ref[...])
        tril, block = _chunk_masks()
        vv = hg_v[:, 2 * HGRN_WIDTH:3 * HGRN_WIDTH].astype(BF16)
        outs = []
        for h in range(HGRN_HEADS):
            sl = slice(h * LANES, (h + 1) * LANES)
            v_h = vv[:, sl]
            a = jnp.where(tril, _dot_nt(p["qi"][:, sl].astype(BF16), p["ki"][:, sl].astype(BF16)), 0.0)
            o = _dot(a.astype(BF16), v_h)
            upd = _dot_tn(v_h, _spread(p["kb"][:, sl].astype(BF16), block))
            st = st_scr[h]
            st0_ref[h] = st
            parts = []
            for n in range(n_chunks):
                parts.append(st.astype(BF16))
                decay = p["e_tot"][n * HGRN_CHUNK:n * HGRN_CHUNK + 1, sl]
                st = st * decay + upd[:, n * LANES:(n + 1) * LANES]
            st_scr[h] = st
            o = o + _dot_nt(_spread(p["qs"][:, sl].astype(BF16), block), jnp.concatenate(parts, axis=1))
            outs.append(o)
        o_all = jnp.concatenate(outs, axis=1)
        o_ref[...] = o_all
        normed = jnp.concatenate(
            [outs[h] * _rms(outs[h]) for h in range(HGRN_HEADS)], axis=1)
        gb = p["gb"]
        yb_ref[...] = (normed * hnw_ref[...]) * (gb * _sigmoid(gb))
        if n_w:
            pl.when(step == nblk - 1)(finish)

    anywhere = pl.BlockSpec(memory_space=pl.ANY)
    return pl.pallas_call(
        body, name="hgrn_fwd", grid=(nblk,),
        in_specs=[_rows(HGRN_ROWS, 4 * HGRN_WIDTH), _full((2, HGRN_WIDTH)), _full((1, HGRN_WIDTH))]
        + [anywhere] * n_w,
        out_specs=[_rows(HGRN_ROWS, HGRN_WIDTH), _rows(HGRN_ROWS, HGRN_WIDTH),
                   pl.BlockSpec((None, HGRN_HEADS, LANES, LANES), lambda i: (i, 0, 0, 0))] + [anywhere] * n_w,
        out_shape=[jax.ShapeDtypeStruct((seq, HGRN_WIDTH), F32)] * 2
        + [jax.ShapeDtypeStruct((nblk, HGRN_HEADS, LANES, LANES), F32)]
        + [_gathered_shape(h) for h in weight_halves],
        scratch_shapes=[pltpu.VMEM((HGRN_HEADS, LANES, LANES), F32)] + (_gather_scratch(n_w) if n_w else []),
        compiler_params=_cparams("arbitrary"),
    )(hg, lb_logits, hnw, *weight_halves)


def ffn_fwd(outs, lses, yb, x, w_out, norm2_w, w_gu4, w_down, final_w, target):
    seq = x.shape[0]
    tm = ROW_TILE
    cw = w_gu4.shape[2]
    inv_d = 1.0 / D_MODEL

    def body(o1, o2, o3, l1, l2, l3, yb_ref, x_ref, wo_ref, nw_ref, wgu_ref, wd_ref, fw_ref, t_ref,
             ya_ref, lse_ref, mixed_ref, h1_ref, u2_ref, g_ref, up_ref, act_ref, dh2_ref, acc_ref):
        @pl.when(pl.program_id(0) == 0)
        def _():
            acc_ref[...] = jnp.zeros_like(acc_ref)

        l1v, l2v, l3v = l1[...], l2[...], l3[...]
        mx = jnp.maximum(jnp.maximum(l1v, l2v), l3v)
        e1, e2, e3 = jnp.exp(l1v - mx), jnp.exp(l2v - mx), jnp.exp(l3v - mx)
        den = e1 + e2 + e3
        ya = (e1 * o1[...] + e2 * o2[...] + e3 * o3[...]) / den
        ya_ref[...] = ya
        lse_ref[...] = mx + jnp.log(den)
        mixed = jnp.concatenate([ya, yb_ref[...]], axis=1).astype(BF16)
        mixed_ref[...] = mixed
        h1 = x_ref[...] + _dot(mixed, wo_ref[...])
        h1_ref[...] = h1
        u2 = ((h1 * _rms(h1)) * nw_ref[...]).astype(BF16)
        u2_ref[...] = u2
        g = jnp.concatenate([_dot(u2, wgu_ref[0]), _dot(u2, wgu_ref[1])], axis=1)
        up = jnp.concatenate([_dot(u2, wgu_ref[2]), _dot(u2, wgu_ref[3])], axis=1)
        g_ref[...] = g.astype(BF16)
        up_ref[...] = up.astype(BF16)
        act = ((g * _sigmoid(g)) * up).astype(BF16)
        act_ref[...] = act
        h2 = h1 + _dot(act, wd_ref[...])
        rf = _rms(h2)
        n = h2 * rf
        fw = fw_ref[...]
        err = n * fw - t_ref[...]
        dy = err * inv_d
        acc_ref[0:1, :] += jnp.sum(dy * n, axis=0, keepdims=True)
        acc_ref[1:2, :] += (0.5 * inv_d) * jnp.sum(err * err, axis=0, keepdims=True)
        dn = dy * fw
        dh2_ref[...] = rf * (dn - n * jnp.mean(dn * n, axis=-1, keepdims=True))

    half = _rows(tm, ATTN_WIDTH)
    wide = _rows(tm, D_MODEL)
    ffn = _rows(tm, FFN_HIDDEN)
    return pl.pallas_call(
        body, name="ffn_fwd", grid=(seq // tm,),
        in_specs=[half] * 7 + [wide, _weight((D_MODEL, D_MODEL)), _full((1, D_MODEL)),
                               _weight((N_CHIPS, D_MODEL, cw)), _weight((FFN_HIDDEN, D_MODEL)),
                               _full((1, D_MODEL)), wide],
        out_specs=[half, half, wide, wide, wide, ffn, ffn, ffn, wide, _full((8, D_MODEL))],
        out_shape=[jax.ShapeDtypeStruct((seq, ATTN_WIDTH), F32)] * 2
        + [jax.ShapeDtypeStruct((seq, D_MODEL), BF16), jax.ShapeDtypeStruct((seq, D_MODEL), F32),
           jax.ShapeDtypeStruct((seq, D_MODEL), BF16)]
        + [jax.ShapeDtypeStruct((seq, FFN_HIDDEN), BF16)] * 3
        + [jax.ShapeDtypeStruct((seq, D_MODEL), F32), jax.ShapeDtypeStruct((8, D_MODEL), F32)],
        compiler_params=_cparams("arbitrary"),
    )(*outs, *lses, yb, x, w_out, norm2_w, w_gu4, w_down, final_w, target)


def _head_sum_matrix():
    i = jnp.arange(ATTN_WIDTH)
    return ((i[:, None] // HEAD_DIM) == (i[None, :] // HEAD_DIM)).astype(BF16)


def ffn_bwd(dh2, w_down, g, up, w_gu4, h1, norm2_w, w_out, ya):
    seq = h1.shape[0]
    tm = ROW_TILE
    cw = w_gu4.shape[2]
    hsum = _head_sum_matrix()

    def body(dh2_ref, wd_ref, g_ref, up_ref, w_ref, h1_ref, nw_ref, wo_ref, ya_ref, hs_ref,
             dgu_ref, dh1_ref, dya_ref, dyb_ref, delta_ref, acc_ref):
        @pl.when(pl.program_id(0) == 0)
        def _():
            acc_ref[...] = jnp.zeros_like(acc_ref)

        dact = _dot_nt(dh2_ref[...].astype(BF16), wd_ref[...])
        gv = g_ref[...].astype(F32)
        sg = _sigmoid(gv)
        dgu_ref[:, :FFN_HIDDEN] = (dact * up_ref[...].astype(F32) * (sg * (1.0 + gv * (1.0 - sg)))).astype(BF16)
        dgu_ref[:, FFN_HIDDEN:] = (dact * (gv * sg)).astype(BF16)
        du2 = _dot_nt(dgu_ref[:, :cw], w_ref[0])
        for j in range(1, N_CHIPS):
            du2 = du2 + _dot_nt(dgu_ref[:, j * cw:(j + 1) * cw], w_ref[j])
        h1 = h1_ref[...]
        r2 = _rms(h1)
        nh = h1 * r2
        acc_ref[0:1, :] += jnp.sum(du2 * nh, axis=0, keepdims=True)
        dn = du2 * nw_ref[...]
        dh1 = dh2_ref[...] + r2 * (dn - nh * jnp.mean(dn * nh, axis=-1, keepdims=True))
        dh1_ref[...] = dh1
        dmixed = _dot_nt(dh1.astype(BF16), wo_ref[...])
        dya = dmixed[:, :ATTN_WIDTH]
        dya_ref[...] = dya
        dyb_ref[...] = dmixed[:, ATTN_WIDTH:]
        prod = dya * ya_ref[...]
        hi = prod.astype(BF16)
        lo = (prod - hi.astype(F32)).astype(BF16)
        delta_ref[...] = _dot(hi, hs_ref[...]) + _dot(lo, hs_ref[...])

    wide = _rows(tm, D_MODEL)
    half = _rows(tm, ATTN_WIDTH)
    ffn = _rows(tm, FFN_HIDDEN)
    return pl.pallas_call(
        body, name="ffn_bwd", grid=(seq // tm,),
        in_specs=[wide, _weight((FFN_HIDDEN, D_MODEL)), ffn, ffn, _weight((N_CHIPS, D_MODEL, cw)), wide,
                  _full((1, D_MODEL)), _weight((D_MODEL, D_MODEL)), half, _full((ATTN_WIDTH, ATTN_WIDTH))],
        out_specs=[_rows(tm, 2 * FFN_HIDDEN), wide, half, half, half, _full((8, D_MODEL))],
        out_shape=[jax.ShapeDtypeStruct((seq, 2 * FFN_HIDDEN), BF16), jax.ShapeDtypeStruct((seq, D_MODEL), F32)]
        + [jax.ShapeDtypeStruct((seq, ATTN_WIDTH), F32)] * 3 + [jax.ShapeDtypeStruct((8, D_MODEL), F32)],
        compiler_params=_cparams("arbitrary"),
    )(dh2, w_down, g, up, w_gu4, h1, norm2_w, w_out, ya, hsum)


def attn_bwd(q, k, v, dy, lse, delta, dilation, name):
    seq = q.shape[0]
    span, per_step, cur, prev = _attn_specs(seq, dilation)
    whole = pl.BlockSpec((seq, LANES), lambda hp, j: (0, hp))

    def body(q_ref, dy_ref, lse_ref, dl_ref, kc_ref, vc_ref, kp_ref, vp_ref, dq_ref, dk_ref, dv_ref):
        first_step = pl.program_id(1) == 0
        base = pl.program_id(1) * ATTN_STEP_ROWS
        masks = _head_masks()

        def block(it, carry):
            t, r = it // dilation, it % dilation
            rows = _strided_rows(r + span * t, dilation)
            at_edge = t == 0
            q2, dy2 = q_ref[rows, :].astype(BF16), dy_ref[rows, :].astype(BF16)
            lse2, dl2 = lse_ref[rows, :], dl_ref[rows, :]
            kp = _block_before(kp_ref, kc_ref, t, r, span, dilation, per_step)
            vp = _block_before(vp_ref, vc_ref, t, r, span, dilation, per_step)
            k2 = jnp.concatenate([kp, kc_ref[rows, :]], axis=0).astype(BF16)
            v2 = jnp.concatenate([vp, vc_ref[rows, :]], axis=0).astype(BF16)
            valid = _window_valid(first_step & at_edge)
            zero = jnp.zeros_like(q2)
            qms, dyms, ps, dss, kms = [], [], [], [], []
            for h, mh in enumerate(masks):
                c0 = h * HEAD_DIM
                qm, dym = jnp.where(mh, q2, zero), jnp.where(mh, dy2, zero)
                s = _dot_nt(qm, k2) * ATTN_SCALE
                p = jnp.where(valid, jnp.exp(s - lse2[:, c0:c0 + 1]), 0.0)
                dp = _dot_nt(dym, v2)
                dss.append((p * (dp - dl2[:, c0:c0 + 1]) * ATTN_SCALE).astype(BF16))
                ps.append(p.astype(BF16))
                qms.append(qm)
                dyms.append(dym)
                kms.append(jnp.where(mh, k2, jnp.zeros_like(k2)))
            dq_ref[rows, :] = _dot(jnp.concatenate(dss, axis=1), jnp.concatenate(kms, axis=0))
            dv_full = _dot_tn(jnp.concatenate(ps, axis=0), jnp.concatenate(dyms, axis=0))
            dk_full = _dot_tn(jnp.concatenate(dss, axis=0), jnp.concatenate(qms, axis=0))
            here = _strided_rows(base + r + span * t, dilation)
            dk_ref[here, :] = dk_full[ATTN_BLOCK:]
            dv_ref[here, :] = dv_full[ATTN_BLOCK:]

            back = _strided_rows(jnp.maximum(base + r + span * t - span, r), dilation)
            dk_ref[back, :] += dk_full[:ATTN_BLOCK]
            dv_ref[back, :] += dv_full[:ATTN_BLOCK]
            return carry

        lax.fori_loop(0, per_step * dilation, block, 0, unroll=ATTN_BWD_UNROLL)

    return pl.pallas_call(
        body, name=name, grid=(ATTN_WIDTH // LANES, seq // ATTN_STEP_ROWS),
        in_specs=[cur] * 6 + [prev, prev],
        out_specs=[cur, whole, whole],
        out_shape=[jax.ShapeDtypeStruct((seq, ATTN_WIDTH), F32)] * 3,
        compiler_params=_cparams("parallel", "arbitrary"),
    )(q, dy, lse, delta, k, v, k, v)


def hgrn_bwd(hg, lb_logits, hnw, o_pre, st0, dyb, chip_sums=()):
    seq = hg.shape[0]
    nblk = seq // HGRN_ROWS
    n_chunks = HGRN_ROWS // HGRN_CHUNK
    w = HGRN_WIDTH
    n_s = len(chip_sums)

    def body(*refs):
        hg_ref, lbl_ref, hnw_ref, o_ref, st0_ref, dyb_ref = refs[:6]
        dhg_ref, acc_ref = refs[6 + n_s:8 + n_s]
        dst_scr = refs[8 + 2 * n_s]
        step = pl.program_id(0)
        if n_s:
            start, finish = _chip_exchange_phases(refs[6:6 + n_s], refs[8 + n_s:8 + 2 * n_s], *refs[9 + 2 * n_s:])
            pl.when(step == 0)(start)

        @pl.when(step == 0)
        def _():
            dst_scr[...] = jnp.zeros_like(dst_scr)
            acc_ref[...] = jnp.zeros_like(acc_ref)

        hg_v = hg_ref[...]
        p = _hgrn_prep(hg_v, lbl_ref[...])
        tril, block = _chunk_masks()
        vv = hg_v[:, 2 * w:3 * w].astype(BF16)
        hnw_v = hnw_ref[...]
        gb = p["gb"]
        sgg = _sigmoid(gb)
        silu_g = gb * sgg
        dyb_v = dyb_ref[...]
        o_v = o_ref[...]

        d_on = dyb_v * hnw_v * silu_g
        on_parts, do_parts = [], []
        for h in range(HGRN_HEADS):
            sl = slice(h * LANES, (h + 1) * LANES)
            rs = _rms(o_v[:, sl])
            on = o_v[:, sl] * rs
            on_parts.append(on)
            do_parts.append(rs * (d_on[:, sl] - on * jnp.mean(d_on[:, sl] * on, axis=-1, keepdims=True)))
        on_all = jnp.concatenate(on_parts, axis=1)
        dgb = dyb_v * on_all * hnw_v * (sgg * (1.0 + gb * (1.0 - sgg)))
        acc_ref[0:1, :] += jnp.sum(dyb_v * on_all * silu_g, axis=0, keepdims=True)

        dqf_parts, dkk_parts, db_parts, dv_parts, dbt_parts, dkbkb_parts = [], [], [], [], [], []
        for h in range(HGRN_HEADS):
            sl = slice(h * LANES, (h + 1) * LANES)
            v_h = vv[:, sl]
            do_h = do_parts[h].astype(BF16)
            qi, ki, qs, kb = p["qi"][:, sl], p["ki"][:, sl], p["qs"][:, sl], p["kb"][:, sl]
            qi_b, ki_b = qi.astype(BF16), ki.astype(BF16)
            kb_cat = _spread(kb.astype(BF16), block)
            qs_cat = _spread(qs.astype(BF16), block)
            upd = _dot_tn(v_h, kb_cat)
            st = st0_ref[h]
            st_parts = []
            for n in range(n_chunks):
                st_parts.append(st)
                decay = p["e_tot"][n * HGRN_CHUNK:n * HGRN_CHUNK + 1, sl]
                st = st * decay + upd[:, n * LANES:(n + 1) * LANES]
            st_cat = jnp.concatenate([s_.astype(BF16) for s_ in st_parts], axis=1)
            wgt = _dot_tn(do_h, qs_cat)
            dst = dst_scr[h]
            dst_parts = [None] * n_chunks
            dbt_rows = [None] * n_chunks
            for n in reversed(range(n_chunks)):
                dst_parts[n] = dst.astype(BF16)
                decay = p["e_tot"][n * HGRN_CHUNK:n * HGRN_CHUNK + 1, sl]
                dbt_rows[n] = jnp.sum(dst * st_parts[n], axis=0, keepdims=True) * decay
                dst = dst * decay + wgt[:, n * LANES:(n + 1) * LANES]
            dst_scr[h] = dst
            dst_cat = jnp.concatenate(dst_parts, axis=1)
            dqs = _fold(_dot(do_h, st_cat), block)
            dkb = _fold(_dot(v_h, dst_cat), block)
            dv_state = _dot_nt(kb_cat, dst_cat)
            a = jnp.where(tril, _dot_nt(qi_b, ki_b), 0.0).astype(BF16)
            da = jnp.where(tril, _dot_nt(do_h, v_h), 0.0).astype(BF16)
            dv_parts.append(_dot_tn(a, do_h) + dv_state)
            dqi = _dot(da, ki_b)
            dki = _dot_tn(da, qi_b)
            dqf_parts.append(dqi * p["e_iq"][:, sl] + dqs * p["e_b"][:, sl])
            dkk_parts.append(dki * p["e_ik"][:, sl] + dkb * p["e_bb"][:, sl])
            dkbkb = dkb * kb
            db_parts.append(dqi * qi - dki * ki + dqs * qs - dkbkb)
            dkbkb_parts.append(dkbkb)
            dbt_parts.append(_chunk_rows(dbt_rows))

        cat = lambda parts: jnp.concatenate(parts, axis=1)
        dlogf = (_chunk_cumsum(cat(db_parts), reverse=True)
                 + _chunk_row(_chunk_cumsum(cat(dkbkb_parts)), HGRN_CHUNK - 1) + cat(dbt_parts))
        sq, qb = p["sq"], hg_v[:, :w]
        dqb = cat(dqf_parts) * (sq * (1.0 + qb * (1.0 - sq)))
        df = dlogf / p["f"] - cat(dkk_parts)
        sg, lb = p["sg"], p["lb"]
        dfb = df * (1.0 - lb) * sg * (1.0 - sg)
        acc_ref[1:2, :] += jnp.sum(df * (1.0 - sg), axis=0, keepdims=True)
        dhg_ref[...] = jnp.concatenate([dqb, dfb, cat(dv_parts), dgb], axis=1)
        if n_s:
            pl.when(step == nblk - 1)(finish)

    rev = lambda i: (nblk - 1 - i, 0)
    anywhere = pl.BlockSpec(memory_space=pl.ANY)
    return pl.pallas_call(
        body, name="hgrn_bwd", grid=(nblk,),
        in_specs=[pl.BlockSpec((HGRN_ROWS, 4 * w), rev), _full((2, w)), _full((1, w)),
                  pl.BlockSpec((HGRN_ROWS, w), rev),
                  pl.BlockSpec((None, HGRN_HEADS, LANES, LANES), lambda i: (nblk - 1 - i, 0, 0, 0)),
                  pl.BlockSpec((HGRN_ROWS, w), rev)] + [anywhere] * n_s,
        out_specs=[pl.BlockSpec((HGRN_ROWS, 4 * w), rev), _full((8, w))] + [anywhere] * n_s,
        out_shape=[jax.ShapeDtypeStruct((seq, 4 * w), F32), jax.ShapeDtypeStruct((8, w), F32)]
        + [jax.ShapeDtypeStruct(s.shape, s.dtype) for s in chip_sums],
        scratch_shapes=[pltpu.VMEM((HGRN_HEADS, LANES, LANES), F32)] + (_chip_exchange_scratch(n_s) if n_s else []),
        compiler_params=_cparams("arbitrary"),
    )(hg, lb_logits, hnw, o_pre, st0, dyb, *chip_sums)


def in_bwd(dqs, dks, dvs, dhg, cos_t, sin_t, w_in4, x, norm1_w, dh1):
    seq = x.shape[0]
    tm = ROW_TILE
    cw = w_in4.shape[2]

    def body(dq1, dq2, dq3, dk1, dk2, dk3, dv1, dv2, dv3, dhg_ref, cos_ref, sin_ref, w_ref,
             x_ref, nw_ref, dh1_ref, dproj_ref, dx_ref, acc_ref):
        @pl.when(pl.program_id(0) == 0)
        def _():
            acc_ref[...] = jnp.zeros_like(acc_ref)

        cos, sin = cos_ref[...], sin_ref[...]
        dqa = _rotary_bwd(dq1[...] + dq2[...] + dq3[...], cos, sin)
        dka = _rotary_bwd(dk1[...] + dk2[...] + dk3[...], cos, sin)
        dva = dv1[...] + dv2[...] + dv3[...]
        dproj = jnp.concatenate([dqa, dka, dva, dhg_ref[...]], axis=1).astype(BF16)
        dproj_ref[...] = dproj
        du = _dot_nt(dproj[:, :cw], w_ref[0])
        for j in range(1, N_CHIPS):
            du = du + _dot_nt(dproj[:, j * cw:(j + 1) * cw], w_ref[j])
        xv = x_ref[...]
        r1 = _rms(xv)
        nx = xv * r1
        acc_ref[0:1, :] += jnp.sum(du * nx, axis=0, keepdims=True)
        dn = du * nw_ref[...]
        dx_ref[...] = dh1_ref[...] + r1 * (dn - nx * jnp.mean(dn * nx, axis=-1, keepdims=True))

    half = _rows(tm, ATTN_WIDTH)
    wide = _rows(tm, D_MODEL)
    return pl.pallas_call(
        body, name="in_bwd", grid=(seq // tm,),
        in_specs=[half] * 9 + [_rows(tm, 4 * HGRN_WIDTH), _rows(tm, LANES), _rows(tm, LANES),
                               _weight((N_CHIPS, D_MODEL, cw)), wide, _full((1, D_MODEL)), wide],
        out_specs=[_rows(tm, IN_PROJ_WIDTH), wide, _full((8, D_MODEL))],
        out_shape=[jax.ShapeDtypeStruct((seq, IN_PROJ_WIDTH), BF16), jax.ShapeDtypeStruct((seq, D_MODEL), F32),
                   jax.ShapeDtypeStruct((8, D_MODEL), F32)],
        compiler_params=_cparams("arbitrary"),
    )(*dqs, *dks, *dvs, dhg, cos_t, sin_t, w_in4, x, norm1_w, dh1)


def weight_grad(a, b, col_block, name, group=1):
    seq, kdim = a.shape
    ndim = b.shape[1]
    nj = ndim // col_block
    tk = 512

    def body(a_ref, b_ref, o_ref):
        @pl.when(pl.program_id(1) == 0)
        def _():
            o_ref[...] = jnp.zeros_like(o_ref)

        acc = _dot_tn(a_ref[...].astype(BF16), b_ref[...].astype(BF16))
        for i in range(group):
            o_ref[i] += acc[:, i * col_block:(i + 1) * col_block]

    return pl.pallas_call(
        body, name=name, grid=(nj // group, seq // tk),
        in_specs=[pl.BlockSpec((tk, kdim), lambda j, t: (t, 0)),
                  pl.BlockSpec((tk, group * col_block), lambda j, t: (t, j))],
        out_specs=pl.BlockSpec((group, kdim, col_block), lambda j, t: (j, 0, 0)),
        out_shape=jax.ShapeDtypeStruct((nj, kdim, col_block), F32),
        compiler_params=_cparams("parallel", "arbitrary"),
    )(a, b)


def exchange_with_sibling(grads, name):
    n = len(grads)

    def body(*refs):
        g_refs, out_refs = refs[:n], refs[n:2 * n]
        send_sems, recv_sems = refs[2 * n], refs[2 * n + 1]
        x, y, cc = _mesh_pos()
        copies = []
        for i in range(n):
            for j in range(N_CHIPS):
                k = i * N_CHIPS + j
                copies.append(pltpu.make_async_remote_copy(
                    src_ref=g_refs[i].at[j, 1 - cc], dst_ref=out_refs[i].at[j],
                    send_sem=send_sems.at[k], recv_sem=recv_sems.at[k],
                    device_id=(x, y, 1 - cc), device_id_type=MESH_ID))
        for cp in copies:
            cp.start()
        for cp in copies:
            cp.wait_recv()
        for cp in copies:
            cp.wait_send()

    return pl.pallas_call(
        body, name=name,
        in_specs=[pl.BlockSpec(memory_space=pl.ANY)] * n,
        out_specs=[pl.BlockSpec(memory_space=pl.ANY)] * n,
        out_shape=[jax.ShapeDtypeStruct((N_CHIPS,) + g.shape[2:], g.dtype) for g in grads],
        scratch_shapes=[pltpu.SemaphoreType.DMA((n * N_CHIPS,)), pltpu.SemaphoreType.DMA((n * N_CHIPS,))],
    )(*grads)


def add_own_half(grad, recv, name):
    _, _, r, c = grad.shape
    tr = r // 2 if r % 32 == 0 else r

    def body(cc_ref, g_ref, r_ref, o_ref):
        o_ref[...] = (g_ref[...] + r_ref[...]).astype(BF16)

    grid_spec = pltpu.PrefetchScalarGridSpec(
        num_scalar_prefetch=1, grid=(N_CHIPS, r // tr),
        in_specs=[pl.BlockSpec((None, None, tr, c), lambda j, t, cc: (j, cc[0], t, 0)),
                  pl.BlockSpec((None, tr, c), lambda j, t, cc: (j, t, 0))],
        out_specs=pl.BlockSpec((None, tr, c), lambda j, t, cc: (j, t, 0)))
    cc = lax.axis_index("c").astype(jnp.int32).reshape(1)
    return pl.pallas_call(
        body, name=name, grid_spec=grid_spec,
        out_shape=jax.ShapeDtypeStruct((N_CHIPS, r, c), BF16),
        compiler_params=_cparams("parallel", "parallel"),
    )(cc, grad, recv)


def _chip_exchange_phases(s_refs, out_refs, send_sems, recv_sems):
    n = len(s_refs)
    x, y, cc = _mesh_pos()
    my_chip = 2 * x + y
    chips = [(1 - x, y), (x, 1 - y), (1 - x, 1 - y)]

    def outgoing():
        return [pltpu.make_async_remote_copy(
            src_ref=s_refs[i].at[2 * px + py], dst_ref=out_refs[i].at[my_chip],
            send_sem=send_sems.at[3 * i + j], recv_sem=recv_sems.at[3 * i + j],
            device_id=(px, py, cc), device_id_type=MESH_ID)
            for i in range(n) for j, (px, py) in enumerate(chips)]

    def start():
        for cp in outgoing():
            cp.start()

    def finish():
        for i in range(n):
            for j, (px, py) in enumerate(chips):
                pltpu.make_async_remote_copy(
                    src_ref=s_refs[i].at[my_chip], dst_ref=out_refs[i].at[2 * px + py],
                    send_sem=send_sems.at[3 * i + j], recv_sem=recv_sems.at[3 * i + j],
                    device_id=(px, py, cc), device_id_type=MESH_ID).wait_recv()
        for cp in outgoing():
            cp.wait_send()

    return start, finish


def _chip_exchange_scratch(n):
    return [pltpu.SemaphoreType.DMA((3 * n,)), pltpu.SemaphoreType.DMA((3 * n,))]


def exchange_between_chips(sums):
    n = len(sums)

    def body(*refs):
        start, finish = _chip_exchange_phases(refs[:n], refs[n:2 * n], refs[2 * n], refs[2 * n + 1])
        start()
        finish()

    return pl.pallas_call(
        body, name="grad_exchange_chips",
        in_specs=[pl.BlockSpec(memory_space=pl.ANY)] * n,
        out_specs=[pl.BlockSpec(memory_space=pl.ANY)] * n,
        out_shape=[jax.ShapeDtypeStruct(s.shape, s.dtype) for s in sums],
        scratch_shapes=_chip_exchange_scratch(n),
    )(*sums)


def sum_chips(sums, parts, name):
    _, r, c = parts.shape
    tr = r // 2 if r % 32 == 0 else r

    def body(idx_ref, s_ref, p1_ref, p2_ref, p3_ref, o_ref):
        o_ref[...] = ((s_ref[...].astype(F32) + p1_ref[...].astype(F32))
                      + p2_ref[...].astype(F32)) + p3_ref[...].astype(F32)

    def pick(k):
        return pl.BlockSpec((None, tr, c), lambda t, idx: (idx[k], t, 0))

    x, y = lax.axis_index("x"), lax.axis_index("y")
    idx = jnp.stack([2 * x + y, 2 * (1 - x) + y, 2 * x + (1 - y), 2 * (1 - x) + (1 - y)]).astype(jnp.int32)
    grid_spec = pltpu.PrefetchScalarGridSpec(
        num_scalar_prefetch=1, grid=(r // tr,),
        in_specs=[pick(0), pick(1), pick(2), pick(3)],
        out_specs=pl.BlockSpec((tr, c), lambda t, idx: (t, 0)))
    return pl.pallas_call(
        body, name=name, grid_spec=grid_spec,
        out_shape=jax.ShapeDtypeStruct((r, c), F32),
        compiler_params=_cparams("parallel"),
    )(idx, sums, parts, parts, parts)


def share_with_sibling(halves):
    n = len(halves)

    def body(*refs):
        h_refs, out_refs = refs[:n], refs[n:2 * n]
        send_sems, recv_sems = refs[2 * n], refs[2 * n + 1]
        x, y, cc = _mesh_pos()
        copies = [pltpu.make_async_remote_copy(
            src_ref=h_refs[i], dst_ref=out_refs[i],
            send_sem=send_sems.at[i], recv_sem=recv_sems.at[i],
            device_id=(x, y, 1 - cc), device_id_type=MESH_ID) for i in range(n)]
        for cp in copies:
            cp.start()
        for cp in copies:
            cp.wait_recv()
        for cp in copies:
            cp.wait_send()

    return pl.pallas_call(
        body, name="grad_share_sibling",
        in_specs=[pl.BlockSpec(memory_space=pl.ANY)] * n,
        out_specs=[pl.BlockSpec(memory_space=pl.ANY)] * n,
        out_shape=[jax.ShapeDtypeStruct(h.shape, h.dtype) for h in halves],
        scratch_shapes=[pltpu.SemaphoreType.DMA((n,)), pltpu.SemaphoreType.DMA((n,))],
    )(*halves)


def _adam_update(w, g, m, v):
    m = ADAM_B1 * m + (1.0 - ADAM_B1) * g
    v = ADAM_B2 * v + (1.0 - ADAM_B2) * (g * g)
    m_hat = m / (1.0 - ADAM_B1 ** ADAM_STEP)
    v_hat = v / (1.0 - ADAM_B2 ** ADAM_STEP)
    delta = -ADAM_LR * (m_hat / (jnp.sqrt(v_hat) + ADAM_EPS) + ADAM_WD * w)
    return delta, m, v


def adamw(w, g_mine, g_sibling, m, v, name):
    r, c = w.shape
    half = r // 2
    tr = half // 2 if half % 16 == 0 else half
    nt = half // tr

    def body(cc_ref, w_ref, ga_ref, gb_ref, m_ref, v_ref, g_ref, d_ref, nm_ref, nv_ref):
        g = jnp.where(pl.program_id(0) == cc_ref[0], ga_ref[...], gb_ref[...])
        g_ref[...] = g
        d, nm, nv = _adam_update(w_ref[...], g, m_ref[...], v_ref[...])
        d_ref[...] = d
        nm_ref[...] = nm
        nv_ref[...] = nv

    full = pl.BlockSpec((tr, c), lambda h, t, cc: (h * nt + t, 0))
    part = pl.BlockSpec((tr, c), lambda h, t, cc: (t, 0))
    grid_spec = pltpu.PrefetchScalarGridSpec(
        num_scalar_prefetch=1, grid=(2, nt),
        in_specs=[full, part, part, full, full], out_specs=[full] * 4)
    cc = lax.axis_index("c").astype(jnp.int32).reshape(1)
    return pl.pallas_call(
        body, name=name, grid_spec=grid_spec,
        out_shape=[jax.ShapeDtypeStruct((r, c), F32)] * 4,
        compiler_params=_cparams("parallel", "parallel"),
    )(cc, w, g_mine, g_sibling, m, v)


def small_allreduce(pack):
    def body(p_ref, o_ref, gather, send_sems, recv_sems):
        x, y, cc = _mesh_pos()
        me = 4 * x + 2 * y + cc
        gather[me] = p_ref[...]
        flips = [(fx, fy, fc) for fx in (0, 1) for fy in (0, 1) for fc in (0, 1)][1:]
        copies = []
        for k, (fx, fy, fc) in enumerate(flips):
            copies.append(pltpu.make_async_remote_copy(
                src_ref=p_ref, dst_ref=gather.at[me],
                send_sem=send_sems.at[k], recv_sem=recv_sems.at[k],
                device_id=(x ^ fx, y ^ fy, cc ^ fc), device_id_type=MESH_ID))
        for cp in copies:
            cp.start()
        for k, (fx, fy, fc) in enumerate(flips):
            src = 4 * (x ^ fx) + 2 * (y ^ fy) + (cc ^ fc)
            pltpu.make_async_remote_copy(
                src_ref=p_ref, dst_ref=gather.at[src],
                send_sem=send_sems.at[k], recv_sem=recv_sems.at[k],
                device_id=(x ^ fx, y ^ fy, cc ^ fc), device_id_type=MESH_ID).wait_recv()
        for cp in copies:
            cp.wait_send()
        total = gather[0]
        for d in range(1, N_DEV):
            total = total + gather[d]
        o_ref[...] = total

    return pl.pallas_call(
        body, name="small_allreduce",
        in_specs=[pl.BlockSpec(memory_space=pltpu.VMEM)],
        out_specs=pl.BlockSpec(memory_space=pltpu.VMEM),
        out_shape=jax.ShapeDtypeStruct(pack.shape, pack.dtype),
        scratch_shapes=[pltpu.VMEM((N_DEV,) + pack.shape, pack.dtype),
                        pltpu.SemaphoreType.DMA((7,)), pltpu.SemaphoreType.DMA((7,))],
    )(pack)


def small_update(gsum, wpack, mpack, vpack):
    hw = HGRN_WIDTH

    def body(g_ref, w_ref, m_ref, v_ref, go_ref, d_ref, nm_ref, nv_ref, loss_ref):
        g = g_ref[...]
        wv = w_ref[...]
        a0, a1 = wv[4:5, :hw], wv[4:5, hw:]
        mx = jnp.maximum(a0, a1)
        e0, e1 = jnp.exp(a0 - mx), jnp.exp(a1 - mx)
        lb = e0 / (e0 + e1)
        dl = g[4:5, :hw] * lb * (1.0 - lb)
        row = lax.broadcasted_iota(jnp.int32, g.shape, 0)
        lb_row = jnp.concatenate([dl, -dl], axis=1)
        grads = jnp.where(row == 4, lb_row, jnp.where(row < 4, g, 0.0))
        go_ref[...] = grads
        d, nm, nv = _adam_update(wv, grads, m_ref[...], v_ref[...])
        d_ref[...] = d
        nm_ref[...] = nm
        nv_ref[...] = nv
        loss_ref[...] = jnp.zeros((8, LANES), F32) + jnp.sum(g[5:6, :])

    vm = pl.BlockSpec(memory_space=pltpu.VMEM)
    return pl.pallas_call(
        body, name="small_update",
        in_specs=[vm] * 4, out_specs=[vm] * 5,
        out_shape=[jax.ShapeDtypeStruct(gsum.shape, F32)] * 4 + [jax.ShapeDtypeStruct((8, LANES), F32)],
    )(gsum, wpack, mpack, vpack)


def _pack_small(n1, n2, fn, hn, lbl):
    z = jnp.zeros((1, D_MODEL - HGRN_WIDTH), F32)
    rows = [n1.reshape(1, D_MODEL), n2.reshape(1, D_MODEL), fn.reshape(1, D_MODEL),
            jnp.concatenate([hn.reshape(1, HGRN_WIDTH), z], axis=1), lbl.reshape(1, 2 * HGRN_WIDTH),
            jnp.zeros((3, D_MODEL), F32)]
    return jnp.concatenate(rows, axis=0)


def _unpack_small(pack):
    return (pack[0:1], pack[4].reshape(2, HGRN_WIDTH), pack[3:4, :HGRN_WIDTH], pack[1:2], pack[2])


def kernel(x, norm1_w, w_in, lb_logits, hgrn_norm_w, w_out, norm2_w, w_gate_up, w_down, final_norm_w, loss_target, m_norm1_w, m_w_in, m_lb_logits, m_hgrn_norm_w, m_w_out, m_norm2_w, m_w_gate_up, m_w_down, m_final_norm_w, v_norm1_w, v_w_in, v_lb_logits, v_hgrn_norm_w, v_w_out, v_norm2_w, v_w_gate_up, v_w_down, v_final_norm_w):
    seq = x.shape[1]
    xs = x.reshape(seq, D_MODEL)
    target = loss_target.reshape(seq, D_MODEL)
    shards = {"w_in": w_in[0], "w_out": w_out[0], "w_gu": w_gate_up[0], "w_down": w_down[0]}

    cast = {k: cast_bf16(w, "cast_" + k) for k, w in shards.items()}
    w_in4 = allgather_halves(cast["w_in"], "gather_w_in").reshape(N_CHIPS, D_MODEL, -1)

    cos_t, sin_t = _rope_tables(seq)
    fw = final_norm_w.reshape(1, D_MODEL)

    qr, kr, va, hg, u, g_out, g_down = in_proj(
        xs, norm1_w, w_in4, cos_t, sin_t, [cast["w_out"], cast["w_down"]])
    fwd = [attn_fwd(qr, kr, va, d, "attn_fwd_d%d" % d) for _, d in DILATED_PAIRS]
    yb, o_pre, st0, g_gu = hgrn_fwd(hg, lb_logits, hgrn_norm_w, [cast["w_gu"]])
    w_out_f = g_out.reshape(D_MODEL, D_MODEL)
    w_gu4 = g_gu.reshape(N_CHIPS, D_MODEL, -1)
    w_down_f = g_down.reshape(FFN_HIDDEN, D_MODEL)
    ya, lse, mixed, h1, u2, g, up, act, dh2, acc_fin = ffn_fwd(
        [f[0] for f in fwd], [f[1] for f in fwd], yb, xs, w_out_f, norm2_w, w_gu4, w_down_f, fw, target)

    cw_in, cw_gu = w_in4.shape[2], w_gu4.shape[2]
    dgu, dh1, dya, dyb, delta, acc_n2 = ffn_bwd(dh2, w_down_f, g, up, w_gu4, h1, norm2_w, w_out_f, ya)
    early = [
        weight_grad(mixed, dh1, D_MODEL, "wgrad_out").reshape(N_CHIPS, 2, D_MODEL // 8, D_MODEL),
        weight_grad(u2, dgu, cw_gu, "wgrad_gu", group=2).reshape(N_CHIPS, 2, D_MODEL // 2, cw_gu),
        weight_grad(act, dh2, D_MODEL, "wgrad_down").reshape(N_CHIPS, 2, FFN_HIDDEN // 8, D_MODEL),
    ]
    early_names = ["out", "gu", "down"]
    early_recv = exchange_with_sibling(early, "grad_exchange_sibling_early")
    early_sums = [add_own_half(gr, rc, "add_half_" + nm) for gr, rc, nm in zip(early, early_recv, early_names)]
    dhg, acc_hg, *early_parts = hgrn_bwd(hg, lb_logits, hgrn_norm_w, o_pre, st0, dyb, early_sums)
    bwd = [attn_bwd(qr, kr, va, dya, lse, delta, d, "attn_bwd_d%d" % d) for _, d in DILATED_PAIRS]
    dproj, dx, acc_n1 = in_bwd([b[0] for b in bwd], [b[1] for b in bwd], [b[2] for b in bwd],
                               dhg, cos_t, sin_t, w_in4, xs, norm1_w, dh1)
    late = [weight_grad(u, dproj, cw_in, "wgrad_in", group=2).reshape(N_CHIPS, 2, D_MODEL // 2, cw_in)]
    late_recv = exchange_with_sibling(late, "grad_exchange_sibling_late")
    late_sums = [add_own_half(late[0], late_recv[0], "add_half_in")]
    late_parts = exchange_between_chips(late_sums)
    names = ["in"] + early_names
    sums, parts = late_sums + early_sums, list(late_parts) + list(early_parts)
    halves = [sum_chips(s, p, "sum_chips_" + nm) for s, p, nm in zip(sums, parts, names)]
    others = share_with_sibling(halves)
    big = {}
    for nm, key, mine, other, m_, v_ in zip(names, ["w_in", "w_out", "w_gu", "w_down"], halves, others,
                                            [m_w_in, m_w_out, m_w_gate_up, m_w_down],
                                            [v_w_in, v_w_out, v_w_gate_up, v_w_down]):
        big[key] = tuple(t[None] for t in adamw(shards[key], mine, other, m_[0], v_[0], "adamw_" + nm))

    z512 = jnp.zeros((1, D_MODEL - HGRN_WIDTH), F32)
    gpack = jnp.concatenate([
        acc_n1[0:1], acc_n2[0:1], acc_fin[0:1],
        jnp.concatenate([acc_hg[0:1], z512], axis=1), jnp.concatenate([acc_hg[1:2], z512], axis=1),
        acc_fin[1:2], jnp.zeros((2, D_MODEL), F32)], axis=0)
    gsum = small_allreduce(gpack)
    wpack = _pack_small(norm1_w, norm2_w, final_norm_w, hgrn_norm_w, lb_logits)
    mpack = _pack_small(m_norm1_w, m_norm2_w, m_final_norm_w, m_hgrn_norm_w, m_lb_logits)
    vpack = _pack_small(v_norm1_w, v_norm2_w, v_final_norm_w, v_hgrn_norm_w, v_lb_logits)
    gs, ds, nms, nvs, loss8 = small_update(gsum, wpack, mpack, vpack)
    loss = loss8[0, 0]

    def assemble(small_pack, idx):
        n1, lbl, hn, n2, fn = _unpack_small(small_pack)
        return (n1, big["w_in"][idx], lbl, hn, big["w_out"][idx], n2, big["w_gu"][idx], big["w_down"][idx], fn)

    return (loss, dx.reshape(x.shape), *assemble(gs, 0), *assemble(ds, 1), *assemble(nms, 2), *assemble(nvs, 3))
```

```python
import functools

import jax
import jax.numpy as jnp
from jax import lax
from jax.experimental import pallas as pl
from jax.experimental.pallas import tpu as pltpu

F32 = jnp.float32
BF16 = jnp.bfloat16

D_MODEL = 1024
ATTN_WIDTH = 512
HEAD_DIM = 64
DILATED_PAIRS = ((128, 1), (512, 4), (2048, 16))
ATTN_BLOCK = 128
ROPE_THETA = 10000.0
HGRN_WIDTH = 512
HGRN_CHUNK = 16
HGRN_HEADS = 4
IN_PROJ_WIDTH = 3584
FFN_HIDDEN = 2816
NORM_EPS = 1e-6
ATTN_SCALE = HEAD_DIM ** -0.5
N_CHIPS = 4
N_DEV = 8

ADAM_LR = 0.001
ADAM_B1 = 0.9
ADAM_B2 = 0.999
ADAM_EPS = 1e-08
ADAM_WD = 0.01
ADAM_STEP = 10

LANES = 128
HGRN_ROWS = 128
ROW_TILE = 256
ATTN_STEP_ROWS = 2048
ATTN_FWD_UNROLL = 8
ATTN_BWD_UNROLL = 8
VMEM_LIMIT = 56 * 1024 * 1024
NEG_BIG = -1e30
MESH_ID = pl.DeviceIdType.MESH


def _cparams(*sem):
    return pltpu.CompilerParams(dimension_semantics=tuple(sem), vmem_limit_bytes=VMEM_LIMIT)


def _dot(a, b):
    return jnp.dot(a, b, preferred_element_type=F32)


def _dot_nt(a, b):
    return lax.dot_general(a, b, (((1,), (1,)), ((), ())), preferred_element_type=F32)


def _dot_tn(a, b):
    return lax.dot_general(a, b, (((0,), (0,)), ((), ())), preferred_element_type=F32)


def _sigmoid(x):
    return 1.0 / (1.0 + jnp.exp(-x))


def _full(shape):
    n = len(shape)
    return pl.BlockSpec(shape, lambda *_: (0,) * n)


def _weight(shape):
    n = len(shape)
    return pl.BlockSpec(shape, lambda *_: (0,) * n, pipeline_mode=pl.Buffered(1))


def _rows(tm, width):
    return pl.BlockSpec((tm, width), lambda i: (i, 0))


def _swap32(x):
    lane = lax.broadcasted_iota(jnp.int32, x.shape, 1)
    first = (lane % HEAD_DIM) < (HEAD_DIM // 2)
    return jnp.where(first, pltpu.roll(x, LANES - 32, axis=1), pltpu.roll(x, 32, axis=1))


def _rotary_fwd(x, cos, sin_signed):
    parts = []
    for j in range(x.shape[1] // LANES):
        xc = x[:, j * LANES:(j + 1) * LANES]
        parts.append(xc * cos + _swap32(xc) * sin_signed)
    return jnp.concatenate(parts, axis=1)


def _rotary_bwd(dy, cos, sin_signed):
    parts = []
    for j in range(dy.shape[1] // LANES):
        dc = dy[:, j * LANES:(j + 1) * LANES]
        parts.append(dc * cos + _swap32(dc * sin_signed))
    return jnp.concatenate(parts, axis=1)


def _rope_tables(seq):
    half = HEAD_DIM // 2
    inv_freq = ROPE_THETA ** (-jnp.arange(half, dtype=F32) / half)
    ang = jnp.arange(seq, dtype=F32)[:, None] * inv_freq[None, :]
    cos, sin = jnp.cos(ang), jnp.sin(ang)
    cos_t = jnp.tile(cos, (1, LANES // half))
    sin_t = jnp.tile(jnp.concatenate([-sin, sin], axis=1), (1, LANES // HEAD_DIM))
    return cos_t, sin_t


def cast_bf16(w, name):
    r, c = w.shape
    half = r // 2

    def body(w_ref, o_ref):
        o_ref[...] = w_ref[...].astype(BF16)

    return pl.pallas_call(
        body, name=name, grid=(2,),
        in_specs=[pl.BlockSpec((half, c), lambda i: (i, 0))],
        out_specs=pl.BlockSpec((None, half, c), lambda i: (i, 0, 0)),
        out_shape=jax.ShapeDtypeStruct((2, half, c), BF16),
        compiler_params=_cparams("parallel"),
    )(w)


def _mesh_pos():
    return lax.axis_index("x"), lax.axis_index("y"), lax.axis_index("c")


GATHER_COPIES = 7


def _gather_phases(x_refs, out_refs, send_sems, recv_sems, local_sems):
    n = len(x_refs)
    x, y, cc = _mesh_pos()
    me, sibling = (x, y, cc), (x, y, 1 - cc)
    chips = [(1 - x, y), (x, 1 - y), (1 - x, 1 - y)]

    def rows(i, px, py, pc):
        return out_refs[i].at[4 * px + 2 * py + pc]

    def copy(i, k, block, to, src=None):
        return pltpu.make_async_remote_copy(
            src_ref=rows(i, *block) if src is None else src, dst_ref=rows(i, *block),
            send_sem=send_sems.at[GATHER_COPIES * i + k], recv_sem=recv_sems.at[GATHER_COPIES * i + k],
            device_id=to, device_id_type=MESH_ID)

    def local(i):
        return pltpu.make_async_copy(x_refs[i].at[cc], rows(i, *me), local_sems.at[i])

    def first(i):
        mine = x_refs[i].at[cc]
        return [copy(i, 0, me, sibling, src=mine)] + [
            copy(i, 1 + j, me, (*chip, cc), src=mine) for j, chip in enumerate(chips)]

    def passed(i):
        return [copy(i, 4 + j, (*chip, cc), sibling) for j, chip in enumerate(chips)]

    def start():
        for i in range(n):
            local(i).start()
            for cp in first(i):
                cp.start()

    def forward():
        for i in range(n):
            onward = passed(i)
            for j, chip in enumerate(chips):
                copy(i, 1 + j, (*chip, cc), me).wait_recv()
                onward[j].start()

    def finish():
        for i in range(n):
            copy(i, 0, sibling, me).wait_recv()
            for j, chip in enumerate(chips):
                copy(i, 4 + j, (*chip, 1 - cc), me).wait_recv()
            for cp in first(i) + passed(i):
                cp.wait_send()
            local(i).wait()

    return start, forward, finish


def _gather_scratch(n):
    return [pltpu.SemaphoreType.DMA((GATHER_COPIES * n,)), pltpu.SemaphoreType.DMA((GATHER_COPIES * n,)),
            pltpu.SemaphoreType.DMA((n,))]


def _gathered_shape(halves):
    return jax.ShapeDtypeStruct((N_DEV,) + halves.shape[1:], halves.dtype)


def allgather_halves(halves, name):
    def body(x_ref, out_ref, send_sems, recv_sems, local_sems):
        start, forward, finish = _gather_phases([x_ref], [out_ref], send_sems, recv_sems, local_sems)
        start()
        forward()
        finish()

    return pl.pallas_call(
        body, name=name,
        in_specs=[pl.BlockSpec(memory_space=pl.ANY)],
        out_specs=pl.BlockSpec(memory_space=pl.ANY),
        out_shape=_gathered_shape(halves),
        scratch_shapes=_gather_scratch(1),
    )(halves)


def _rms(x):
    return lax.rsqrt(jnp.mean(x * x, axis=-1, keepdims=True) + NORM_EPS)


def in_proj(x, norm1_w, w_in4, cos_t, sin_t, weight_halves=()):
    seq = x.shape[0]
    tm = ROW_TILE
    cw = w_in4.shape[2]
    n_w = len(weight_halves)
    steps = seq // tm

    def body(*refs):
        x_ref, nw_ref, w_ref, cos_ref, sin_ref = refs[:5]
        q_ref, k_ref, v_ref, hg_ref, u_ref = refs[5 + n_w:10 + n_w]
        step = pl.program_id(0)
        if n_w:
            start, forward, finish = _gather_phases(
                refs[5:5 + n_w], refs[10 + n_w:10 + 2 * n_w], *refs[10 + 2 * n_w:])
            pl.when(step == 0)(start)
            pl.when(step == (3 * steps) // 4)(forward)
        xv = x_ref[...]
        u = ((xv * _rms(xv)) * nw_ref[...]).astype(BF16)
        u_ref[...] = u
        proj = jnp.concatenate([_dot(u, w_ref[j]) for j in range(N_CHIPS)], axis=1)
        cos, sin = cos_ref[...], sin_ref[...]
        a = ATTN_WIDTH
        q_ref[...] = _rotary_fwd(proj[:, :a], cos, sin)
        k_ref[...] = _rotary_fwd(proj[:, a:2 * a], cos, sin)
        v_ref[...] = proj[:, 2 * a:3 * a]
        hg_ref[...] = proj[:, 3 * a:]
        if n_w:
            pl.when(step == steps - 1)(finish)

    anywhere = pl.BlockSpec(memory_space=pl.ANY)
    return pl.pallas_call(
        body, name="in_proj", grid=(steps,),
        in_specs=[_rows(tm, D_MODEL), _full((1, D_MODEL)), _weight((N_CHIPS, D_MODEL, cw)),
                  _rows(tm, LANES), _rows(tm, LANES)] + [anywhere] * n_w,
        out_specs=[_rows(tm, ATTN_WIDTH)] * 3 + [_rows(tm, 4 * HGRN_WIDTH), _rows(tm, D_MODEL)]
        + [anywhere] * n_w,
        out_shape=[jax.ShapeDtypeStruct((seq, ATTN_WIDTH), F32)] * 3
        + [jax.ShapeDtypeStruct((seq, 4 * HGRN_WIDTH), F32), jax.ShapeDtypeStruct((seq, D_MODEL), BF16)]
        + [_gathered_shape(h) for h in weight_halves],
        scratch_shapes=_gather_scratch(n_w) if n_w else [],
        compiler_params=_cparams("arbitrary"),
    )(x, norm1_w, w_in4, cos_t, sin_t, *weight_halves)


def _head_masks():
    lane = lax.broadcasted_iota(jnp.int32, (1, LANES), 1)
    return [(lane // HEAD_DIM) == h for h in range(LANES // HEAD_DIM)]


def _window_valid(no_prev):
    qi = lax.broadcasted_iota(jnp.int32, (ATTN_BLOCK, 2 * ATTN_BLOCK), 0)
    kj = lax.broadcasted_iota(jnp.int32, (ATTN_BLOCK, 2 * ATTN_BLOCK), 1)
    valid = (kj >= qi) & (kj <= qi + ATTN_BLOCK)
    return valid & (jnp.logical_not(no_prev) | (kj >= ATTN_BLOCK))


def _strided_rows(start, dilation):
    if dilation == 1:
        return pl.ds(start, ATTN_BLOCK)
    return pl.ds(start, ATTN_BLOCK, stride=dilation)


def _block_before(edge_ref, cur_ref, t, r, span, dilation, per_step):
    edge = edge_ref[_strided_rows(ATTN_STEP_ROWS - span + r, dilation), :]
    if per_step == 1:
        return edge
    inside = cur_ref[_strided_rows(r + span * jnp.maximum(t - 1, 0), dilation), :]
    return jnp.where(t == 0, edge, inside)


def _attn_specs():
    cur = pl.BlockSpec((ATTN_STEP_ROWS, LANES), lambda hp, j: (j, hp))
    prev = pl.BlockSpec((ATTN_STEP_ROWS, LANES), lambda hp, j: (jnp.maximum(j - 1, 0), hp))
    return cur, prev


def _for_each_block(dilation, unroll, block):
    span = ATTN_BLOCK * dilation
    per_step = ATTN_STEP_ROWS // span

    def trip(it, carry):
        block(it // dilation, it % dilation, span, per_step)
        return carry

    lax.fori_loop(0, per_step * dilation, trip, 0, unroll=unroll)


def attn_fwd(q, k, v):
    seq = q.shape[0]
    cur, prev = _attn_specs()

    def body(q_ref, kc_ref, vc_ref, kp_ref, vp_ref, y_ref, lse_ref):
        first_step = pl.program_id(1) == 0
        masks = _head_masks()
        for index, (_, dilation) in enumerate(DILATED_PAIRS):
            def block(t, r, span, per_step, dilation=dilation, merge=index > 0):
                rows = _strided_rows(r + span * t, dilation)
                q2 = q_ref[rows, :].astype(BF16)
                kp = _block_before(kp_ref, kc_ref, t, r, span, dilation, per_step)
                vp = _block_before(vp_ref, vc_ref, t, r, span, dilation, per_step)
                k2 = jnp.concatenate([kp, kc_ref[rows, :]], axis=0).astype(BF16)
                v2 = jnp.concatenate([vp, vc_ref[rows, :]], axis=0).astype(BF16)
                valid = _window_valid(first_step & (t == 0))
                o_acc = jnp.zeros((ATTN_BLOCK, LANES), F32)
                l_acc = jnp.zeros((ATTN_BLOCK, LANES), F32)
                for mh in masks:
                    qm = jnp.where(mh, q2, jnp.zeros_like(q2))
                    s = jnp.where(valid, _dot_nt(qm, k2) * ATTN_SCALE, NEG_BIG)
                    m = jnp.max(s, axis=-1, keepdims=True)
                    p = jnp.exp(s - m)
                    l = jnp.sum(p, axis=-1, keepdims=True)
                    o = _dot(p.astype(BF16), v2) / l
                    o_acc = jnp.where(mh, o, o_acc)
                    l_acc = jnp.where(mh, m + jnp.log(l), l_acc)
                if merge:
                    y_old, l_old = y_ref[rows, :], lse_ref[rows, :]
                    mx = jnp.maximum(l_old, l_acc)
                    e_old, e_new = jnp.exp(l_old - mx), jnp.exp(l_acc - mx)
                    den = e_old + e_new
                    o_acc = (y_old * e_old + o_acc * e_new) / den
                    l_acc = mx + jnp.log(den)
                y_ref[rows, :] = o_acc
                lse_ref[rows, :] = l_acc

            _for_each_block(dilation, ATTN_FWD_UNROLL, block)

    return pl.pallas_call(
        body, name="attn_fwd", grid=(ATTN_WIDTH // LANES, seq // ATTN_STEP_ROWS),
        in_specs=[cur, cur, cur, prev, prev],
        out_specs=[cur, cur],
        out_shape=[jax.ShapeDtypeStruct((seq, ATTN_WIDTH), F32)] * 2,
        compiler_params=_cparams("parallel", "parallel"),
    )(q, k, v, k, v)


def _chunk_cumsum(x, reverse=False):
    rc = lax.broadcasted_iota(jnp.int32, x.shape, 0) % HGRN_CHUNK
    sh = 1
    while sh < HGRN_CHUNK:
        if reverse:
            x = x + jnp.where(rc + sh < HGRN_CHUNK, pltpu.roll(x, x.shape[0] - sh, axis=0), 0.0)
        else:
            x = x + jnp.where(rc >= sh, pltpu.roll(x, sh, axis=0), 0.0)
        sh *= 2
    return x


def _chunk_row(x, row):
    return _chunk_rows([x[n * HGRN_CHUNK + row:n * HGRN_CHUNK + row + 1, :]
                        for n in range(x.shape[0] // HGRN_CHUNK)])


def _chunk_rows(rows):
    return jnp.concatenate([jnp.broadcast_to(r, (HGRN_CHUNK, r.shape[1])) for r in rows], axis=0)


def _hgrn_prep(hg, lbl):
    w = HGRN_WIDTH
    a0, a1 = lbl[0:1, :], lbl[1:2, :]
    mx = jnp.maximum(a0, a1)
    e0, e1 = jnp.exp(a0 - mx), jnp.exp(a1 - mx)
    lb = e0 / (e0 + e1)
    qb, fb, gb = hg[:, :w], hg[:, w:2 * w], hg[:, 3 * w:]
    sg = _sigmoid(fb)
    f = lb + (1.0 - lb) * sg
    b = _chunk_cumsum(jnp.log(f))
    bmid, btot = _chunk_row(b, HGRN_CHUNK // 2 - 1), _chunk_row(b, HGRN_CHUNK - 1)
    sq = _sigmoid(qb)
    p = dict(lb=lb, sg=sg, f=f, kk=1.0 - f, sq=sq, qf=qb * sq, gb=gb,
             e_iq=jnp.exp(b - bmid), e_ik=jnp.exp(bmid - b), e_b=jnp.exp(b),
             e_bb=jnp.exp(btot - b), e_tot=jnp.exp(btot))
    p["qi"] = p["qf"] * p["e_iq"]
    p["ki"] = p["kk"] * p["e_ik"]
    p["qs"] = p["qf"] * p["e_b"]
    p["kb"] = p["kk"] * p["e_bb"]
    return p


def _chunk_masks():
    t = lax.broadcasted_iota(jnp.int32, (HGRN_ROWS, HGRN_ROWS), 0)
    s = lax.broadcasted_iota(jnp.int32, (HGRN_ROWS, HGRN_ROWS), 1)
    tril = ((t // HGRN_CHUNK) == (s // HGRN_CHUNK)) & (s <= t)
    n_chunks = HGRN_ROWS // HGRN_CHUNK
    tt = lax.broadcasted_iota(jnp.int32, (HGRN_ROWS, n_chunks * LANES), 0)
    cc = lax.broadcasted_iota(jnp.int32, (HGRN_ROWS, n_chunks * LANES), 1)
    block = (tt // HGRN_CHUNK) == (cc // LANES)
    return tril, block


def _spread(x, block):
    n_chunks = HGRN_ROWS // HGRN_CHUNK
    return jnp.where(block, jnp.tile(x, (1, n_chunks)), jnp.zeros((), x.dtype))


def _fold(x_full, block):
    n_chunks = HGRN_ROWS // HGRN_CHUNK
    z = jnp.where(block, x_full, 0.0)
    acc = z[:, :LANES]
    for n in range(1, n_chunks):
        acc = acc + z[:, n * LANES:(n + 1) * LANES]
    return acc


def hgrn_fwd(hg, lb_logits, hnw, weight_halves=()):
    seq = hg.shape[0]
    nblk = seq // HGRN_ROWS
    n_chunks = HGRN_ROWS // HGRN_CHUNK
    n_w = len(weight_halves)

    def body(*refs):
        hg_ref, lbl_ref, hnw_ref = refs[:3]
        w_refs = refs[3:3 + n_w]
        yb_ref, o_ref, st0_ref = refs[3 + n_w:6 + n_w]
        g_refs = refs[6 + n_w:6 + 2 * n_w]
        st_scr = refs[6 + 2 * n_w]
        step = pl.program_id(0)
        if n_w:
            start, forward, finish = _gather_phases(w_refs, g_refs, *refs[7 + 2 * n_w:])
            pl.when(step == 0)(start)
            pl.when(step == (3 * nblk) // 4)(forward)

        @pl.when(step == 0)
        def _():
            st_scr[...] = jnp.zeros_like(st_scr)

        hg_v = hg_ref[...]
        p = _hgrn_prep(hg_v, lbl_ref[...])
        tril, block = _chunk_masks()
        vv = hg_v[:, 2 * HGRN_WIDTH:3 * HGRN_WIDTH].astype(BF16)
        outs = []
        for h in range(HGRN_HEADS):
            sl = slice(h * LANES, (h + 1) * LANES)
            v_h = vv[:, sl]
            a = jnp.where(tril, _dot_nt(p["qi"][:, sl].astype(BF16), p["ki"][:, sl].astype(BF16)), 0.0)
            o = _dot(a.astype(BF16), v_h)
            upd = _dot_tn(v_h, _spread(p["kb"][:, sl].astype(BF16), block))
            st = st_scr[h]
            st0_ref[h] = st
            parts = []
            for n in range(n_chunks):
                parts.append(st.astype(BF16))
                decay = p["e_tot"][n * HGRN_CHUNK:n * HGRN_CHUNK + 1, sl]
                st = st * decay + upd[:, n * LANES:(n + 1) * LANES]
            st_scr[h] = st
            o = o + _dot_nt(_spread(p["qs"][:, sl].astype(BF16), block), jnp.concatenate(parts, axis=1))
            outs.append(o)
        o_all = jnp.concatenate(outs, axis=1)
        o_ref[...] = o_all
        normed = jnp.concatenate(
            [outs[h] * _rms(outs[h]) for h in range(HGRN_HEADS)], axis=1)
        gb = p["gb"]
        yb_ref[...] = (normed * hnw_ref[...]) * (gb * _sigmoid(gb))
        if n_w:
            pl.when(step == nblk - 1)(finish)

    anywhere = pl.BlockSpec(memory_space=pl.ANY)
    return pl.pallas_call(
        body, name="hgrn_fwd", grid=(nblk,),
        in_specs=[_rows(HGRN_ROWS, 4 * HGRN_WIDTH), _full((2, HGRN_WIDTH)), _full((1, HGRN_WIDTH))]
        + [anywhere] * n_w,
        out_specs=[_rows(HGRN_ROWS, HGRN_WIDTH), _rows(HGRN_ROWS, HGRN_WIDTH),
                   pl.BlockSpec((None, HGRN_HEADS, LANES, LANES), lambda i: (i, 0, 0, 0))] + [anywhere] * n_w,
        out_shape=[jax.ShapeDtypeStruct((seq, HGRN_WIDTH), F32)] * 2
        + [jax.ShapeDtypeStruct((nblk, HGRN_HEADS, LANES, LANES), F32)]
        + [_gathered_shape(h) for h in weight_halves],
        scratch_shapes=[pltpu.VMEM((HGRN_HEADS, LANES, LANES), F32)] + (_gather_scratch(n_w) if n_w else []),
        compiler_params=_cparams("arbitrary"),
    )(hg, lb_logits, hnw, *weight_halves)


def ffn_fwd(ya, yb, x, w_out, norm2_w, w_gu4, w_down, final_w, target):
    seq = x.shape[0]
    tm = ROW_TILE
    cw = w_gu4.shape[2]
    inv_d = 1.0 / D_MODEL

    def body(ya_ref, yb_ref, x_ref, wo_ref, nw_ref, wgu_ref, wd_ref, fw_ref, t_ref,
             mixed_ref, h1_ref, u2_ref, g_ref, up_ref, act_ref, dh2_ref, acc_ref):
        @pl.when(pl.program_id(0) == 0)
        def _():
            acc_ref[...] = jnp.zeros_like(acc_ref)

        mixed = jnp.concatenate([ya_ref[...], yb_ref[...]], axis=1).astype(BF16)
        mixed_ref[...] = mixed
        h1 = x_ref[...] + _dot(mixed, wo_ref[...])
        h1_ref[...] = h1
        u2 = ((h1 * _rms(h1)) * nw_ref[...]).astype(BF16)
        u2_ref[...] = u2
        g = jnp.concatenate([_dot(u2, wgu_ref[0]), _dot(u2, wgu_ref[1])], axis=1)
        up = jnp.concatenate([_dot(u2, wgu_ref[2]), _dot(u2, wgu_ref[3])], axis=1)
        g_ref[...] = g.astype(BF16)
        up_ref[...] = up.astype(BF16)
        act = ((g * _sigmoid(g)) * up).astype(BF16)
        act_ref[...] = act
        h2 = h1 + _dot(act, wd_ref[...])
        rf = _rms(h2)
        n = h2 * rf
        fw = fw_ref[...]
        err = n * fw - t_ref[...]
        dy = err * inv_d
        acc_ref[0:1, :] += jnp.sum(dy * n, axis=0, keepdims=True)
        acc_ref[1:2, :] += (0.5 * inv_d) * jnp.sum(err * err, axis=0, keepdims=True)
        dn = dy * fw
        dh2_ref[...] = rf * (dn - n * jnp.mean(dn * n, axis=-1, keepdims=True))

    half = _rows(tm, ATTN_WIDTH)
    wide = _rows(tm, D_MODEL)
    ffn = _rows(tm, FFN_HIDDEN)
    return pl.pallas_call(
        body, name="ffn_fwd", grid=(seq // tm,),
        in_specs=[half, half, wide, _weight((D_MODEL, D_MODEL)), _full((1, D_MODEL)),
                  _weight((N_CHIPS, D_MODEL, cw)), _weight((FFN_HIDDEN, D_MODEL)), _full((1, D_MODEL)), wide],
        out_specs=[wide, wide, wide, ffn, ffn, ffn, wide, _full((8, D_MODEL))],
        out_shape=[jax.ShapeDtypeStruct((seq, D_MODEL), BF16), jax.ShapeDtypeStruct((seq, D_MODEL), F32),
                   jax.ShapeDtypeStruct((seq, D_MODEL), BF16)]
        + [jax.ShapeDtypeStruct((seq, FFN_HIDDEN), BF16)] * 3
        + [jax.ShapeDtypeStruct((seq, D_MODEL), F32), jax.ShapeDtypeStruct((8, D_MODEL), F32)],
        compiler_params=_cparams("arbitrary"),
    )(ya, yb, x, w_out, norm2_w, w_gu4, w_down, final_w, target)


def _head_sum_matrix():
    i = jnp.arange(ATTN_WIDTH)
    return ((i[:, None] // HEAD_DIM) == (i[None, :] // HEAD_DIM)).astype(BF16)


def ffn_bwd(dh2, w_down, g, up, w_gu4, h1, norm2_w, w_out, ya):
    seq = h1.shape[0]
    tm = ROW_TILE
    cw = w_gu4.shape[2]
    hsum = _head_sum_matrix()

    def body(dh2_ref, wd_ref, g_ref, up_ref, w_ref, h1_ref, nw_ref, wo_ref, ya_ref, hs_ref,
             dgu_ref, dh1_ref, dya_ref, dyb_ref, delta_ref, acc_ref):
        @pl.when(pl.program_id(0) == 0)
        def _():
            acc_ref[...] = jnp.zeros_like(acc_ref)

        dh2_b = dh2_ref[...].astype(BF16)
        du2 = jnp.zeros((tm, D_MODEL), F32)
        for j in range(N_CHIPS // 2):
            dact = _dot_nt(dh2_b, wd_ref[j * cw:(j + 1) * cw, :])
            gv = g_ref[:, j * cw:(j + 1) * cw].astype(F32)
            sg = _sigmoid(gv)
            dg = (dact * up_ref[:, j * cw:(j + 1) * cw].astype(F32) * (sg * (1.0 + gv * (1.0 - sg)))).astype(BF16)
            dup = (dact * (gv * sg)).astype(BF16)
            dgu_ref[:, j * cw:(j + 1) * cw] = dg
            dgu_ref[:, FFN_HIDDEN + j * cw:FFN_HIDDEN + (j + 1) * cw] = dup
            du2 = du2 + _dot_nt(dg, w_ref[j]) + _dot_nt(dup, w_ref[N_CHIPS // 2 + j])
        h1 = h1_ref[...]
        r2 = _rms(h1)
        nh = h1 * r2
        acc_ref[0:1, :] += jnp.sum(du2 * nh, axis=0, keepdims=True)
        dn = du2 * nw_ref[...]
        dh1 = dh2_ref[...] + r2 * (dn - nh * jnp.mean(dn * nh, axis=-1, keepdims=True))
        dh1_ref[...] = dh1
        dmixed = _dot_nt(dh1.astype(BF16), wo_ref[...])
        dya = dmixed[:, :ATTN_WIDTH]
        dya_ref[...] = dya
        dyb_ref[...] = dmixed[:, ATTN_WIDTH:]
        prod = dya * ya_ref[...]
        hi = prod.astype(BF16)
        lo = (prod - hi.astype(F32)).astype(BF16)
        delta_ref[...] = _dot(hi, hs_ref[...]) + _dot(lo, hs_ref[...])

    wide = _rows(tm, D_MODEL)
    half = _rows(tm, ATTN_WIDTH)
    ffn = _rows(tm, FFN_HIDDEN)
    return pl.pallas_call(
        body, name="ffn_bwd", grid=(seq // tm,),
        in_specs=[wide, _weight((FFN_HIDDEN, D_MODEL)), ffn, ffn, _weight((N_CHIPS, D_MODEL, cw)), wide,
                  _full((1, D_MODEL)), _weight((D_MODEL, D_MODEL)), half, _full((ATTN_WIDTH, ATTN_WIDTH))],
        out_specs=[_rows(tm, 2 * FFN_HIDDEN), wide, half, half, half, _full((8, D_MODEL))],
        out_shape=[jax.ShapeDtypeStruct((seq, 2 * FFN_HIDDEN), BF16), jax.ShapeDtypeStruct((seq, D_MODEL), F32)]
        + [jax.ShapeDtypeStruct((seq, ATTN_WIDTH), F32)] * 3 + [jax.ShapeDtypeStruct((8, D_MODEL), F32)],
        compiler_params=_cparams("arbitrary"),
    )(dh2, w_down, g, up, w_gu4, h1, norm2_w, w_out, ya, hsum)


def attn_bwd(q, k, v, dy, lse, delta):
    seq = q.shape[0]
    cur, prev = _attn_specs()
    whole = pl.BlockSpec((seq, LANES), lambda hp, j: (0, hp))

    def body(q_ref, dy_ref, lse_ref, dl_ref, kc_ref, vc_ref, kp_ref, vp_ref, dq_ref, dk_ref, dv_ref):
        first_step = pl.program_id(1) == 0
        base = pl.program_id(1) * ATTN_STEP_ROWS
        masks = _head_masks()

        def block(t, r, span, per_step, dilation, add):
            rows = _strided_rows(r + span * t, dilation)
            at_edge = t == 0
            q2, dy2 = q_ref[rows, :].astype(BF16), dy_ref[rows, :].astype(BF16)
            lse2, dl2 = lse_ref[rows, :], dl_ref[rows, :]
            kp = _block_before(kp_ref, kc_ref, t, r, span, dilation, per_step)
            vp = _block_before(vp_ref, vc_ref, t, r, span, dilation, per_step)
            k2 = jnp.concatenate([kp, kc_ref[rows, :]], axis=0).astype(BF16)
            v2 = jnp.concatenate([vp, vc_ref[rows, :]], axis=0).astype(BF16)
            valid = _window_valid(first_step & at_edge)
            zero = jnp.zeros_like(q2)
            qms, dyms, ps, dss, kms = [], [], [], [], []
            for h, mh in enumerate(masks):
                c0 = h * HEAD_DIM
                qm, dym = jnp.where(mh, q2, zero), jnp.where(mh, dy2, zero)
                s = _dot_nt(qm, k2) * ATTN_SCALE
                p = jnp.where(valid, jnp.exp(s - lse2[:, c0:c0 + 1]), 0.0)
                dp = _dot_nt(dym, v2)
                dss.append((p * (dp - dl2[:, c0:c0 + 1]) * ATTN_SCALE).astype(BF16))
                ps.append(p.astype(BF16))
                qms.append(qm)
                dyms.append(dym)
                kms.append(jnp.where(mh, k2, jnp.zeros_like(k2)))
            dq = _dot(jnp.concatenate(dss, axis=1), jnp.concatenate(kms, axis=0))
            dv_full = _dot_tn(jnp.concatenate(ps, axis=0), jnp.concatenate(dyms, axis=0))
            dk_full = _dot_tn(jnp.concatenate(dss, axis=0), jnp.concatenate(qms, axis=0))
            here = _strided_rows(base + r + span * t, dilation)
            if add:
                dq_ref[rows, :] += dq
                dk_ref[here, :] += dk_full[ATTN_BLOCK:]
                dv_ref[here, :] += dv_full[ATTN_BLOCK:]
            else:
                dq_ref[rows, :] = dq
                dk_ref[here, :] = dk_full[ATTN_BLOCK:]
                dv_ref[here, :] = dv_full[ATTN_BLOCK:]
            back = _strided_rows(jnp.maximum(base + r + span * t - span, r), dilation)
            dk_ref[back, :] += dk_full[:ATTN_BLOCK]
            dv_ref[back, :] += dv_full[:ATTN_BLOCK]

        for index, (_, dilation) in enumerate(DILATED_PAIRS):
            _for_each_block(dilation, ATTN_BWD_UNROLL,
                            functools.partial(block, dilation=dilation, add=index > 0))

    return pl.pallas_call(
        body, name="attn_bwd", grid=(ATTN_WIDTH // LANES, seq // ATTN_STEP_ROWS),
        in_specs=[cur] * 6 + [prev, prev],
        out_specs=[cur, whole, whole],
        out_shape=[jax.ShapeDtypeStruct((seq, ATTN_WIDTH), F32)] * 3,
        compiler_params=_cparams("parallel", "arbitrary"),
    )(q, dy, lse, delta, k, v, k, v)


def hgrn_bwd(hg, lb_logits, hnw, o_pre, st0, dyb, chip_sums=()):
    seq = hg.shape[0]
    nblk = seq // HGRN_ROWS
    n_chunks = HGRN_ROWS // HGRN_CHUNK
    w = HGRN_WIDTH
    n_s = len(chip_sums)

    def body(*refs):
        hg_ref, lbl_ref, hnw_ref, o_ref, st0_ref, dyb_ref = refs[:6]
        dhg_ref, acc_ref = refs[6 + n_s:8 + n_s]
        dst_scr = refs[8 + 2 * n_s]
        step = pl.program_id(0)
        if n_s:
            start, finish = _chip_exchange_phases(refs[6:6 + n_s], refs[8 + n_s:8 + 2 * n_s], *refs[9 + 2 * n_s:])
            pl.when(step == 0)(start)

        @pl.when(step == 0)
        def _():
            dst_scr[...] = jnp.zeros_like(dst_scr)
            acc_ref[...] = jnp.zeros_like(acc_ref)

        hg_v = hg_ref[...]
        p = _hgrn_prep(hg_v, lbl_ref[...])
        tril, block = _chunk_masks()
        vv = hg_v[:, 2 * w:3 * w].astype(BF16)
        hnw_v = hnw_ref[...]
        gb = p["gb"]
        sgg = _sigmoid(gb)
        silu_g = gb * sgg
        dyb_v = dyb_ref[...]
        o_v = o_ref[...]

        d_on = dyb_v * hnw_v * silu_g
        on_parts, do_parts = [], []
        for h in range(HGRN_HEADS):
            sl = slice(h * LANES, (h + 1) * LANES)
            rs = _rms(o_v[:, sl])
            on = o_v[:, sl] * rs
            on_parts.append(on)
            do_parts.append(rs * (d_on[:, sl] - on * jnp.mean(d_on[:, sl] * on, axis=-1, keepdims=True)))
        on_all = jnp.concatenate(on_parts, axis=1)
        dgb = dyb_v * on_all * hnw_v * (sgg * (1.0 + gb * (1.0 - sgg)))
        acc_ref[0:1, :] += jnp.sum(dyb_v * on_all * silu_g, axis=0, keepdims=True)

        dqf_parts, dkk_parts, db_parts, dv_parts, dbt_parts, dkbkb_parts = [], [], [], [], [], []
        for h in range(HGRN_HEADS):
            sl = slice(h * LANES, (h + 1) * LANES)
            v_h = vv[:, sl]
            do_h = do_parts[h].astype(BF16)
            qi, ki, qs, kb = p["qi"][:, sl], p["ki"][:, sl], p["qs"][:, sl], p["kb"][:, sl]
            qi_b, ki_b = qi.astype(BF16), ki.astype(BF16)
            kb_cat = _spread(kb.astype(BF16), block)
            qs_cat = _spread(qs.astype(BF16), block)
            upd = _dot_tn(v_h, kb_cat)
            st = st0_ref[h]
            st_parts = []
            for n in range(n_chunks):
                st_parts.append(st)
                decay = p["e_tot"][n * HGRN_CHUNK:n * HGRN_CHUNK + 1, sl]
                st = st * decay + upd[:, n * LANES:(n + 1) * LANES]
            st_cat = jnp.concatenate([s_.astype(BF16) for s_ in st_parts], axis=1)
            wgt = _dot_tn(do_h, qs_cat)
            dst = dst_scr[h]
            dst_parts = [None] * n_chunks
            dbt_rows = [None] * n_chunks
            for n in reversed(range(n_chunks)):
                dst_parts[n] = dst.astype(BF16)
                decay = p["e_tot"][n * HGRN_CHUNK:n * HGRN_CHUNK + 1, sl]
                dbt_rows[n] = jnp.sum(dst * st_parts[n], axis=0, keepdims=True) * decay
                dst = dst * decay + wgt[:, n * LANES:(n + 1) * LANES]
            dst_scr[h] = dst
            dst_cat = jnp.concatenate(dst_parts, axis=1)
            dqs = _fold(_dot(do_h, st_cat), block)
            dkb = _fold(_dot(v_h, dst_cat), block)
            dv_state = _dot_nt(kb_cat, dst_cat)
            a = jnp.where(tril, _dot_nt(qi_b, ki_b), 0.0).astype(BF16)
            da = jnp.where(tril, _dot_nt(do_h, v_h), 0.0).astype(BF16)
            dv_parts.append(_dot_tn(a, do_h) + dv_state)
            dqi = _dot(da, ki_b)
            dki = _dot_tn(da, qi_b)
            dqf_parts.append(dqi * p["e_iq"][:, sl] + dqs * p["e_b"][:, sl])
            dkk_parts.append(dki * p["e_ik"][:, sl] + dkb * p["e_bb"][:, sl])
            dkbkb = dkb * kb
            db_parts.append(dqi * qi - dki * ki + dqs * qs - dkbkb)
            dkbkb_parts.append(dkbkb)
            dbt_parts.append(_chunk_rows(dbt_rows))

        cat = lambda parts: jnp.concatenate(parts, axis=1)
        dlogf = (_chunk_cumsum(cat(db_parts), reverse=True)
                 + _chunk_row(_chunk_cumsum(cat(dkbkb_parts)), HGRN_CHUNK - 1) + cat(dbt_parts))
        sq, qb = p["sq"], hg_v[:, :w]
        dqb = cat(dqf_parts) * (sq * (1.0 + qb * (1.0 - sq)))
        df = dlogf / p["f"] - cat(dkk_parts)
        sg, lb = p["sg"], p["lb"]
        dfb = df * (1.0 - lb) * sg * (1.0 - sg)
        acc_ref[1:2, :] += jnp.sum(df * (1.0 - sg), axis=0, keepdims=True)
        dhg_ref[...] = jnp.concatenate([dqb, dfb, cat(dv_parts), dgb], axis=1)
        if n_s:
            pl.when(step == nblk - 1)(finish)

    rev = lambda i: (nblk - 1 - i, 0)
    anywhere = pl.BlockSpec(memory_space=pl.ANY)
    return pl.pallas_call(
        body, name="hgrn_bwd", grid=(nblk,),
        in_specs=[pl.BlockSpec((HGRN_ROWS, 4 * w), rev), _full((2, w)), _full((1, w)),
                  pl.BlockSpec((HGRN_ROWS, w), rev),
                  pl.BlockSpec((None, HGRN_HEADS, LANES, LANES), lambda i: (nblk - 1 - i, 0, 0, 0)),
                  pl.BlockSpec((HGRN_ROWS, w), rev)] + [anywhere] * n_s,
        out_specs=[pl.BlockSpec((HGRN_ROWS, 4 * w), rev), _full((8, w))] + [anywhere] * n_s,
        out_shape=[jax.ShapeDtypeStruct((seq, 4 * w), F32), jax.ShapeDtypeStruct((8, w), F32)]
        + [jax.ShapeDtypeStruct(s.shape, s.dtype) for s in chip_sums],
        scratch_shapes=[pltpu.VMEM((HGRN_HEADS, LANES, LANES), F32)] + (_chip_exchange_scratch(n_s) if n_s else []),
        compiler_params=_cparams("arbitrary"),
    )(hg, lb_logits, hnw, o_pre, st0, dyb, *chip_sums)


def in_bwd(dq, dk, dv, dhg, cos_t, sin_t, w_in4, x, norm1_w, dh1):
    seq = x.shape[0]
    tm = ROW_TILE
    cw = w_in4.shape[2]

    def body(dq_ref, dk_ref, dv_ref, dhg_ref, cos_ref, sin_ref, w_ref,
             x_ref, nw_ref, dh1_ref, dproj_ref, dx_ref, acc_ref):
        @pl.when(pl.program_id(0) == 0)
        def _():
            acc_ref[...] = jnp.zeros_like(acc_ref)

        cos, sin = cos_ref[...], sin_ref[...]
        dqa = _rotary_bwd(dq_ref[...], cos, sin)
        dka = _rotary_bwd(dk_ref[...], cos, sin)
        dproj = jnp.concatenate([dqa, dka, dv_ref[...], dhg_ref[...]], axis=1).astype(BF16)
        dproj_ref[...] = dproj
        du = _dot_nt(dproj[:, :cw], w_ref[0])
        for j in range(1, N_CHIPS):
            du = du + _dot_nt(dproj[:, j * cw:(j + 1) * cw], w_ref[j])
        xv = x_ref[...]
        r1 = _rms(xv)
        nx = xv * r1
        acc_ref[0:1, :] += jnp.sum(du * nx, axis=0, keepdims=True)
        dn = du * nw_ref[...]
        dx_ref[...] = dh1_ref[...] + r1 * (dn - nx * jnp.mean(dn * nx, axis=-1, keepdims=True))

    half = _rows(tm, ATTN_WIDTH)
    wide = _rows(tm, D_MODEL)
    return pl.pallas_call(
        body, name="in_bwd", grid=(seq // tm,),
        in_specs=[half] * 3 + [_rows(tm, 4 * HGRN_WIDTH), _rows(tm, LANES), _rows(tm, LANES),
                               _weight((N_CHIPS, D_MODEL, cw)), wide, _full((1, D_MODEL)), wide],
        out_specs=[_rows(tm, IN_PROJ_WIDTH), wide, _full((8, D_MODEL))],
        out_shape=[jax.ShapeDtypeStruct((seq, IN_PROJ_WIDTH), BF16), jax.ShapeDtypeStruct((seq, D_MODEL), F32),
                   jax.ShapeDtypeStruct((8, D_MODEL), F32)],
        compiler_params=_cparams("arbitrary"),
    )(dq, dk, dv, dhg, cos_t, sin_t, w_in4, x, norm1_w, dh1)


def weight_grad(a, b, col_block, name, group=1):
    seq, kdim = a.shape
    ndim = b.shape[1]
    nj = ndim // col_block
    tk = 512

    def body(a_ref, b_ref, o_ref):
        @pl.when(pl.program_id(1) == 0)
        def _():
            o_ref[...] = jnp.zeros_like(o_ref)

        acc = _dot_tn(a_ref[...].astype(BF16), b_ref[...].astype(BF16))
        for i in range(group):
            o_ref[i] += acc[:, i * col_block:(i + 1) * col_block]

    return pl.pallas_call(
        body, name=name, grid=(nj // group, seq // tk),
        in_specs=[pl.BlockSpec((tk, kdim), lambda j, t: (t, 0)),
                  pl.BlockSpec((tk, group * col_block), lambda j, t: (t, j))],
        out_specs=pl.BlockSpec((group, kdim, col_block), lambda j, t: (j, 0, 0)),
        out_shape=jax.ShapeDtypeStruct((nj, kdim, col_block), F32),
        compiler_params=_cparams("parallel", "arbitrary"),
    )(a, b)


def exchange_with_sibling(grads, name):
    n = len(grads)

    def body(*refs):
        g_refs, out_refs = refs[:n], refs[n:2 * n]
        send_sems, recv_sems = refs[2 * n], refs[2 * n + 1]
        x, y, cc = _mesh_pos()
        copies = []
        for i in range(n):
            for j in range(N_CHIPS):
                k = i * N_CHIPS + j
                copies.append(pltpu.make_async_remote_copy(
                    src_ref=g_refs[i].at[j, 1 - cc], dst_ref=out_refs[i].at[j],
                    send_sem=send_sems.at[k], recv_sem=recv_sems.at[k],
                    device_id=(x, y, 1 - cc), device_id_type=MESH_ID))
        for cp in copies:
            cp.start()
        for cp in copies:
            cp.wait_recv()
        for cp in copies:
            cp.wait_send()

    return pl.pallas_call(
        body, name=name,
        in_specs=[pl.BlockSpec(memory_space=pl.ANY)] * n,
        out_specs=[pl.BlockSpec(memory_space=pl.ANY)] * n,
        out_shape=[jax.ShapeDtypeStruct((N_CHIPS,) + g.shape[2:], g.dtype) for g in grads],
        scratch_shapes=[pltpu.SemaphoreType.DMA((n * N_CHIPS,)), pltpu.SemaphoreType.DMA((n * N_CHIPS,))],
    )(*grads)


def add_own_half(grad, recv, name):
    _, _, r, c = grad.shape
    tr = r // 2 if r % 32 == 0 else r

    def body(cc_ref, g_ref, r_ref, o_ref):
        o_ref[...] = (g_ref[...] + r_ref[...]).astype(BF16)

    grid_spec = pltpu.PrefetchScalarGridSpec(
        num_scalar_prefetch=1, grid=(N_CHIPS, r // tr),
        in_specs=[pl.BlockSpec((None, None, tr, c), lambda j, t, cc: (j, cc[0], t, 0)),
                  pl.BlockSpec((None, tr, c), lambda j, t, cc: (j, t, 0))],
        out_specs=pl.BlockSpec((None, tr, c), lambda j, t, cc: (j, t, 0)))
    cc = lax.axis_index("c").astype(jnp.int32).reshape(1)
    return pl.pallas_call(
        body, name=name, grid_spec=grid_spec,
        out_shape=jax.ShapeDtypeStruct((N_CHIPS, r, c), BF16),
        compiler_params=_cparams("parallel", "parallel"),
    )(cc, grad, recv)


def _chip_exchange_phases(s_refs, out_refs, send_sems, recv_sems):
    n = len(s_refs)
    x, y, cc = _mesh_pos()
    my_chip = 2 * x + y
    chips = [(1 - x, y), (x, 1 - y), (1 - x, 1 - y)]

    def outgoing():
        return [pltpu.make_async_remote_copy(
            src_ref=s_refs[i].at[2 * px + py], dst_ref=out_refs[i].at[my_chip],
            send_sem=send_sems.at[3 * i + j], recv_sem=recv_sems.at[3 * i + j],
            device_id=(px, py, cc), device_id_type=MESH_ID)
            for i in range(n) for j, (px, py) in enumerate(chips)]

    def start():
        for cp in outgoing():
            cp.start()

    def finish():
        for i in range(n):
            for j, (px, py) in enumerate(chips):
                pltpu.make_async_remote_copy(
                    src_ref=s_refs[i].at[my_chip], dst_ref=out_refs[i].at[2 * px + py],
                    send_sem=send_sems.at[3 * i + j], recv_sem=recv_sems.at[3 * i + j],
                    device_id=(px, py, cc), device_id_type=MESH_ID).wait_recv()
        for cp in outgoing():
            cp.wait_send()

    return start, finish


def _chip_exchange_scratch(n):
    return [pltpu.SemaphoreType.DMA((3 * n,)), pltpu.SemaphoreType.DMA((3 * n,))]


def exchange_between_chips(sums):
    n = len(sums)

    def body(*refs):
        start, finish = _chip_exchange_phases(refs[:n], refs[n:2 * n], refs[2 * n], refs[2 * n + 1])
        start()
        finish()

    return pl.pallas_call(
        body, name="grad_exchange_chips",
        in_specs=[pl.BlockSpec(memory_space=pl.ANY)] * n,
        out_specs=[pl.BlockSpec(memory_space=pl.ANY)] * n,
        out_shape=[jax.ShapeDtypeStruct(s.shape, s.dtype) for s in sums],
        scratch_shapes=_chip_exchange_scratch(n),
    )(*sums)


def sum_chips(sums, parts, name):
    _, r, c = parts.shape
    tr = r // 2 if r % 32 == 0 else r

    def body(idx_ref, s_ref, p1_ref, p2_ref, p3_ref, o_ref):
        o_ref[...] = ((s_ref[...].astype(F32) + p1_ref[...].astype(F32))
                      + p2_ref[...].astype(F32)) + p3_ref[...].astype(F32)

    def pick(k):
        return pl.BlockSpec((None, tr, c), lambda t, idx: (idx[k], t, 0))

    x, y = lax.axis_index("x"), lax.axis_index("y")
    idx = jnp.stack([2 * x + y, 2 * (1 - x) + y, 2 * x + (1 - y), 2 * (1 - x) + (1 - y)]).astype(jnp.int32)
    grid_spec = pltpu.PrefetchScalarGridSpec(
        num_scalar_prefetch=1, grid=(r // tr,),
        in_specs=[pick(0), pick(1), pick(2), pick(3)],
        out_specs=pl.BlockSpec((tr, c), lambda t, idx: (t, 0)))
    return pl.pallas_call(
        body, name=name, grid_spec=grid_spec,
        out_shape=jax.ShapeDtypeStruct((r, c), F32),
        compiler_params=_cparams("parallel"),
    )(idx, sums, parts, parts, parts)


def share_with_sibling(halves):
    n = len(halves)

    def body(*refs):
        h_refs, out_refs = refs[:n], refs[n:2 * n]
        send_sems, recv_sems = refs[2 * n], refs[2 * n + 1]
        x, y, cc = _mesh_pos()
        copies = [pltpu.make_async_remote_copy(
            src_ref=h_refs[i], dst_ref=out_refs[i],
            send_sem=send_sems.at[i], recv_sem=recv_sems.at[i],
            device_id=(x, y, 1 - cc), device_id_type=MESH_ID) for i in range(n)]
        for cp in copies:
            cp.start()
        for cp in copies:
            cp.wait_recv()
        for cp in copies:
            cp.wait_send()

    return pl.pallas_call(
        body, name="grad_share_sibling",
        in_specs=[pl.BlockSpec(memory_space=pl.ANY)] * n,
        out_specs=[pl.BlockSpec(memory_space=pl.ANY)] * n,
        out_shape=[jax.ShapeDtypeStruct(h.shape, h.dtype) for h in halves],
        scratch_shapes=[pltpu.SemaphoreType.DMA((n,)), pltpu.SemaphoreType.DMA((n,))],
    )(*halves)


def _adam_update(w, g, m, v):
    m = ADAM_B1 * m + (1.0 - ADAM_B1) * g
    v = ADAM_B2 * v + (1.0 - ADAM_B2) * (g * g)
    m_hat = m / (1.0 - ADAM_B1 ** ADAM_STEP)
    v_hat = v / (1.0 - ADAM_B2 ** ADAM_STEP)
    delta = -ADAM_LR * (m_hat / (jnp.sqrt(v_hat) + ADAM_EPS) + ADAM_WD * w)
    return delta, m, v


def adamw(w, g_mine, g_sibling, m, v, name):
    r, c = w.shape
    half = r // 2
    tr = half // 2 if half % 16 == 0 else half
    nt = half // tr

    def body(cc_ref, w_ref, ga_ref, gb_ref, m_ref, v_ref, g_ref, d_ref, nm_ref, nv_ref):
        g = jnp.where(pl.program_id(0) == cc_ref[0], ga_ref[...], gb_ref[...])
        g_ref[...] = g
        d, nm, nv = _adam_update(w_ref[...], g, m_ref[...], v_ref[...])
        d_ref[...] = d
        nm_ref[...] = nm
        nv_ref[...] = nv

    full = pl.BlockSpec((tr, c), lambda h, t, cc: (h * nt + t, 0))
    part = pl.BlockSpec((tr, c), lambda h, t, cc: (t, 0))
    grid_spec = pltpu.PrefetchScalarGridSpec(
        num_scalar_prefetch=1, grid=(2, nt),
        in_specs=[full, part, part, full, full], out_specs=[full] * 4)
    cc = lax.axis_index("c").astype(jnp.int32).reshape(1)
    return pl.pallas_call(
        body, name=name, grid_spec=grid_spec,
        out_shape=[jax.ShapeDtypeStruct((r, c), F32)] * 4,
        compiler_params=_cparams("parallel", "parallel"),
    )(cc, w, g_mine, g_sibling, m, v)


def small_allreduce(pack):
    def body(p_ref, o_ref, gather, send_sems, recv_sems):
        x, y, cc = _mesh_pos()
        me = 4 * x + 2 * y + cc
        gather[me] = p_ref[...]
        flips = [(fx, fy, fc) for fx in (0, 1) for fy in (0, 1) for fc in (0, 1)][1:]
        copies = []
        for k, (fx, fy, fc) in enumerate(flips):
            copies.append(pltpu.make_async_remote_copy(
                src_ref=p_ref, dst_ref=gather.at[me],
                send_sem=send_sems.at[k], recv_sem=recv_sems.at[k],
                device_id=(x ^ fx, y ^ fy, cc ^ fc), device_id_type=MESH_ID))
        for cp in copies:
            cp.start()
        for k, (fx, fy, fc) in enumerate(flips):
            src = 4 * (x ^ fx) + 2 * (y ^ fy) + (cc ^ fc)
            pltpu.make_async_remote_copy(
                src_ref=p_ref, dst_ref=gather.at[src],
                send_sem=send_sems.at[k], recv_sem=recv_sems.at[k],
                device_id=(x ^ fx, y ^ fy, cc ^ fc), device_id_type=MESH_ID).wait_recv()
        for cp in copies:
            cp.wait_send()
        total = gather[0]
        for d in range(1, N_DEV):
            total = total + gather[d]
        o_ref[...] = total

    return pl.pallas_call(
        body, name="small_allreduce",
        in_specs=[pl.BlockSpec(memory_space=pltpu.VMEM)],
        out_specs=pl.BlockSpec(memory_space=pltpu.VMEM),
        out_shape=jax.ShapeDtypeStruct(pack.shape, pack.dtype),
        scratch_shapes=[pltpu.VMEM((N_DEV,) + pack.shape, pack.dtype),
                        pltpu.SemaphoreType.DMA((7,)), pltpu.SemaphoreType.DMA((7,))],
    )(pack)


def small_update(gsum, wpack, mpack, vpack):
    hw = HGRN_WIDTH

    def body(g_ref, w_ref, m_ref, v_ref, go_ref, d_ref, nm_ref, nv_ref, loss_ref):
        g = g_ref[...]
        wv = w_ref[...]
        a0, a1 = wv[4:5, :hw], wv[4:5, hw:]
        mx = jnp.maximum(a0, a1)
        e0, e1 = jnp.exp(a0 - mx), jnp.exp(a1 - mx)
        lb = e0 / (e0 + e1)
        dl = g[4:5, :hw] * lb * (1.0 - lb)
        row = lax.broadcasted_iota(jnp.int32, g.shape, 0)
        lb_row = jnp.concatenate([dl, -dl], axis=1)
        grads = jnp.where(row == 4, lb_row, jnp.where(row < 4, g, 0.0))
        go_ref[...] = grads
        d, nm, nv = _adam_update(wv, grads, m_ref[...], v_ref[...])
        d_ref[...] = d
        nm_ref[...] = nm
        nv_ref[...] = nv
        loss_ref[...] = jnp.zeros((8, LANES), F32) + jnp.sum(g[5:6, :])

    vm = pl.BlockSpec(memory_space=pltpu.VMEM)
    return pl.pallas_call(
        body, name="small_update",
        in_specs=[vm] * 4, out_specs=[vm] * 5,
        out_shape=[jax.ShapeDtypeStruct(gsum.shape, F32)] * 4 + [jax.ShapeDtypeStruct((8, LANES), F32)],
    )(gsum, wpack, mpack, vpack)


def _pack_small(n1, n2, fn, hn, lbl):
    z = jnp.zeros((1, D_MODEL - HGRN_WIDTH), F32)
    rows = [n1.reshape(1, D_MODEL), n2.reshape(1, D_MODEL), fn.reshape(1, D_MODEL),
            jnp.concatenate([hn.reshape(1, HGRN_WIDTH), z], axis=1), lbl.reshape(1, 2 * HGRN_WIDTH),
            jnp.zeros((3, D_MODEL), F32)]
    return jnp.concatenate(rows, axis=0)


def _unpack_small(pack):
    return (pack[0:1], pack[4].reshape(2, HGRN_WIDTH), pack[3:4, :HGRN_WIDTH], pack[1:2], pack[2])


def kernel(x, norm1_w, w_in, lb_logits, hgrn_norm_w, w_out, norm2_w, w_gate_up, w_down, final_norm_w, loss_target, m_norm1_w, m_w_in, m_lb_logits, m_hgrn_norm_w, m_w_out, m_norm2_w, m_w_gate_up, m_w_down, m_final_norm_w, v_norm1_w, v_w_in, v_lb_logits, v_hgrn_norm_w, v_w_out, v_norm2_w, v_w_gate_up, v_w_down, v_final_norm_w):
    seq = x.shape[1]
    xs = x.reshape(seq, D_MODEL)
    target = loss_target.reshape(seq, D_MODEL)
    shards = {"w_in": w_in[0], "w_out": w_out[0], "w_gu": w_gate_up[0], "w_down": w_down[0]}

    cast = {k: cast_bf16(w, "cast_" + k) for k, w in shards.items()}
    w_in4 = allgather_halves(cast["w_in"], "gather_w_in").reshape(N_CHIPS, D_MODEL, -1)

    cos_t, sin_t = _rope_tables(seq)
    fw = final_norm_w.reshape(1, D_MODEL)

    qr, kr, va, hg, u, g_out, g_down = in_proj(
        xs, norm1_w, w_in4, cos_t, sin_t, [cast["w_out"], cast["w_down"]])
    ya, lse = attn_fwd(qr, kr, va)
    yb, o_pre, st0, g_gu = hgrn_fwd(hg, lb_logits, hgrn_norm_w, [cast["w_gu"]])
    w_out_f = g_out.reshape(D_MODEL, D_MODEL)
    w_gu4 = g_gu.reshape(N_CHIPS, D_MODEL, -1)
    w_down_f = g_down.reshape(FFN_HIDDEN, D_MODEL)
    mixed, h1, u2, g, up, act, dh2, acc_fin = ffn_fwd(
        ya, yb, xs, w_out_f, norm2_w, w_gu4, w_down_f, fw, target)

    cw_in, cw_gu = w_in4.shape[2], w_gu4.shape[2]
    dgu, dh1, dya, dyb, delta, acc_n2 = ffn_bwd(dh2, w_down_f, g, up, w_gu4, h1, norm2_w, w_out_f, ya)
    early = [
        weight_grad(mixed, dh1, D_MODEL, "wgrad_out").reshape(N_CHIPS, 2, D_MODEL // 8, D_MODEL),
        weight_grad(u2, dgu, cw_gu, "wgrad_gu", group=2).reshape(N_CHIPS, 2, D_MODEL // 2, cw_gu),
        weight_grad(act, dh2, D_MODEL, "wgrad_down").reshape(N_CHIPS, 2, FFN_HIDDEN // 8, D_MODEL),
    ]
    early_names = ["out", "gu", "down"]
    early_recv = exchange_with_sibling(early, "grad_exchange_sibling_early")
    early_sums = [add_own_half(gr, rc, "add_half_" + nm) for gr, rc, nm in zip(early, early_recv, early_names)]
    dhg, acc_hg, *early_parts = hgrn_bwd(hg, lb_logits, hgrn_norm_w, o_pre, st0, dyb, early_sums)
    dq, dk, dv = attn_bwd(qr, kr, va, dya, lse, delta)
    dproj, dx, acc_n1 = in_bwd(dq, dk, dv, dhg, cos_t, sin_t, w_in4, xs, norm1_w, dh1)
    late = [weight_grad(u, dproj, cw_in, "wgrad_in", group=2).reshape(N_CHIPS, 2, D_MODEL // 2, cw_in)]
    late_recv = exchange_with_sibling(late, "grad_exchange_sibling_late")
    late_sums = [add_own_half(late[0], late_recv[0], "add_half_in")]
    late_parts = exchange_between_chips(late_sums)
    names = ["in"] + early_names
    sums, parts = late_sums + early_sums, list(late_parts) + list(early_parts)
    halves = [sum_chips(s, p, "sum_chips_" + nm) for s, p, nm in zip(sums, parts, names)]
    others = share_with_sibling(halves)
    big = {}
    for nm, key, mine, other, m_, v_ in zip(names, ["w_in", "w_out", "w_gu", "w_down"], halves, others,
                                            [m_w_in, m_w_out, m_w_gate_up, m_w_down],
                                            [v_w_in, v_w_out, v_w_gate_up, v_w_down]):
        big[key] = tuple(t[None] for t in adamw(shards[key], mine, other, m_[0], v_[0], "adamw_" + nm))

    z512 = jnp.zeros((1, D_MODEL - HGRN_WIDTH), F32)
    gpack = jnp.concatenate([
        acc_n1[0:1], acc_n2[0:1], acc_fin[0:1],
        jnp.concatenate([acc_hg[0:1], z512], axis=1), jnp.concatenate([acc_hg[1:2], z512], axis=1),
        acc_fin[1:2], jnp.zeros((2, D_MODEL), F32)], axis=0)
    gsum = small_allreduce(gpack)
    wpack = _pack_small(norm1_w, norm2_w, final_norm_w, hgrn_norm_w, lb_logits)
    mpack = _pack_small(m_norm1_w, m_norm2_w, m_final_norm_w, m_hgrn_norm_w, m_lb_logits)
    vpack = _pack_small(v_norm1_w, v_norm2_w, v_final_norm_w, v_hgrn_norm_w, v_lb_logits)
    gs, ds, nms, nvs, loss8 = small_update(gsum, wpack, mpack, vpack)
    loss = loss8[0, 0]

    def assemble(small_pack, idx):
        n1, lbl, hn, n2, fn = _unpack_small(small_pack)
        return (n1, big["w_in"][idx], lbl, hn, big["w_out"][idx], n2, big["w_gu"][idx], big["w_down"][idx], fn)

    return (loss, dx.reshape(x.shape), *assemble(gs, 0), *assemble(ds, 1), *assemble(nms, 2), *assemble(nvs, 3))
```

```python
import functools

import jax
import jax.numpy as jnp
from jax import lax
from jax.experimental import pallas as pl
from jax.experimental.pallas import tpu as pltpu

F32 = jnp.float32
BF16 = jnp.bfloat16

D_MODEL = 1024
ATTN_WIDTH = 512
HEAD_DIM = 64
DILATED_PAIRS = ((128, 1), (512, 4), (2048, 16))
ATTN_BLOCK = 128
ROPE_THETA = 10000.0
HGRN_WIDTH = 512
HGRN_CHUNK = 16
HGRN_HEADS = 4
IN_PROJ_WIDTH = 3584
FFN_HIDDEN = 2816
NORM_EPS = 1e-6
ATTN_SCALE = HEAD_DIM ** -0.5
N_CHIPS = 4
N_DEV = 8

ADAM_LR = 0.001
ADAM_B1 = 0.9
ADAM_B2 = 0.999
ADAM_EPS = 1e-08
ADAM_WD = 0.01
ADAM_STEP = 10

LANES = 128
HGRN_ROWS = 128
ROW_TILE = 256
ATTN_STEP_ROWS = 2048
ATTN_FWD_UNROLL = 8
ATTN_BWD_UNROLL = 8
VMEM_LIMIT = 56 * 1024 * 1024
NEG_BIG = -1e30
MESH_ID = pl.DeviceIdType.MESH


def _cparams(*sem):
    return pltpu.CompilerParams(dimension_semantics=tuple(sem), vmem_limit_bytes=VMEM_LIMIT)


def _dot(a, b):
    return jnp.dot(a, b, preferred_element_type=F32)


def _dot_nt(a, b):
    return lax.dot_general(a, b, (((1,), (1,)), ((), ())), preferred_element_type=F32)


def _dot_tn(a, b):
    return lax.dot_general(a, b, (((0,), (0,)), ((), ())), preferred_element_type=F32)


def _sigmoid(x):
    return 1.0 / (1.0 + jnp.exp(-x))


def _full(shape):
    n = len(shape)
    return pl.BlockSpec(shape, lambda *_: (0,) * n)


def _weight(shape):
    n = len(shape)
    return pl.BlockSpec(shape, lambda *_: (0,) * n, pipeline_mode=pl.Buffered(1))


def _rows(tm, width):
    return pl.BlockSpec((tm, width), lambda i: (i, 0))


def _swap32(x):
    lane = lax.broadcasted_iota(jnp.int32, x.shape, 1)
    first = (lane % HEAD_DIM) < (HEAD_DIM // 2)
    return jnp.where(first, pltpu.roll(x, LANES - 32, axis=1), pltpu.roll(x, 32, axis=1))


def _rotary_fwd(x, cos, sin_signed):
    parts = []
    for j in range(x.shape[1] // LANES):
        xc = x[:, j * LANES:(j + 1) * LANES]
        parts.append(xc * cos + _swap32(xc) * sin_signed)
    return jnp.concatenate(parts, axis=1)


def _rotary_bwd(dy, cos, sin_signed):
    parts = []
    for j in range(dy.shape[1] // LANES):
        dc = dy[:, j * LANES:(j + 1) * LANES]
        parts.append(dc * cos + _swap32(dc * sin_signed))
    return jnp.concatenate(parts, axis=1)


def _rope_tables(seq):
    half = HEAD_DIM // 2
    inv_freq = ROPE_THETA ** (-jnp.arange(half, dtype=F32) / half)
    ang = jnp.arange(seq, dtype=F32)[:, None] * inv_freq[None, :]
    cos, sin = jnp.cos(ang), jnp.sin(ang)
    cos_t = jnp.tile(cos, (1, LANES // half))
    sin_t = jnp.tile(jnp.concatenate([-sin, sin], axis=1), (1, LANES // HEAD_DIM))
    return cos_t, sin_t


def cast_bf16(w, name):
    r, c = w.shape
    half = r // 2

    def body(w_ref, o_ref):
        o_ref[...] = w_ref[...].astype(BF16)

    return pl.pallas_call(
        body, name=name, grid=(2,),
        in_specs=[pl.BlockSpec((half, c), lambda i: (i, 0))],
        out_specs=pl.BlockSpec((None, half, c), lambda i: (i, 0, 0)),
        out_shape=jax.ShapeDtypeStruct((2, half, c), BF16),
        compiler_params=_cparams("parallel"),
    )(w)


def _mesh_pos():
    return lax.axis_index("x"), lax.axis_index("y"), lax.axis_index("c")


GATHER_COPIES = 7


def _gather_phases(x_refs, out_refs, send_sems, recv_sems, local_sems):
    n = len(x_refs)
    x, y, cc = _mesh_pos()
    me, sibling = (x, y, cc), (x, y, 1 - cc)
    chips = [(1 - x, y), (x, 1 - y), (1 - x, 1 - y)]

    def rows(i, px, py, pc):
        return out_refs[i].at[4 * px + 2 * py + pc]

    def copy(i, k, block, to, src=None):
        return pltpu.make_async_remote_copy(
            src_ref=rows(i, *block) if src is None else src, dst_ref=rows(i, *block),
            send_sem=send_sems.at[GATHER_COPIES * i + k], recv_sem=recv_sems.at[GATHER_COPIES * i + k],
            device_id=to, device_id_type=MESH_ID)

    def local(i):
        return pltpu.make_async_copy(x_refs[i].at[cc], rows(i, *me), local_sems.at[i])

    def first(i):
        mine = x_refs[i].at[cc]
        return [copy(i, 0, me, sibling, src=mine)] + [
            copy(i, 1 + j, me, (*chip, cc), src=mine) for j, chip in enumerate(chips)]

    def passed(i):
        return [copy(i, 4 + j, (*chip, cc), sibling) for j, chip in enumerate(chips)]

    def start():
        for i in range(n):
            local(i).start()
            for cp in first(i):
                cp.start()

    def forward():
        for i in range(n):
            onward = passed(i)
            for j, chip in enumerate(chips):
                copy(i, 1 + j, (*chip, cc), me).wait_recv()
                onward[j].start()

    def finish():
        for i in range(n):
            copy(i, 0, sibling, me).wait_recv()
            for j, chip in enumerate(chips):
                copy(i, 4 + j, (*chip, 1 - cc), me).wait_recv()
            for cp in first(i) + passed(i):
                cp.wait_send()
            local(i).wait()

    return start, forward, finish


def _gather_scratch(n):
    return [pltpu.SemaphoreType.DMA((GATHER_COPIES * n,)), pltpu.SemaphoreType.DMA((GATHER_COPIES * n,)),
            pltpu.SemaphoreType.DMA((n,))]


def _gathered_shape(halves):
    return jax.ShapeDtypeStruct((N_DEV,) + halves.shape[1:], halves.dtype)


def allgather_halves(halves, name):
    def body(x_ref, out_ref, send_sems, recv_sems, local_sems):
        start, forward, finish = _gather_phases([x_ref], [out_ref], send_sems, recv_sems, local_sems)
        start()
        forward()
        finish()

    return pl.pallas_call(
        body, name=name,
        in_specs=[pl.BlockSpec(memory_space=pl.ANY)],
        out_specs=pl.BlockSpec(memory_space=pl.ANY),
        out_shape=_gathered_shape(halves),
        scratch_shapes=_gather_scratch(1),
    )(halves)


def _rms(x):
    return lax.rsqrt(jnp.mean(x * x, axis=-1, keepdims=True) + NORM_EPS)


def in_proj(x, norm1_w, w_in4, cos_t, sin_t, weight_halves=()):
    seq = x.shape[0]
    tm = ROW_TILE
    cw = w_in4.shape[2]
    n_w = len(weight_halves)
    steps = seq // tm

    def body(*refs):
        x_ref, nw_ref, w_ref, cos_ref, sin_ref = refs[:5]
        q_ref, k_ref, v_ref, hg_ref, u_ref = refs[5 + n_w:10 + n_w]
        step = pl.program_id(0)
        if n_w:
            start, forward, finish = _gather_phases(
                refs[5:5 + n_w], refs[10 + n_w:10 + 2 * n_w], *refs[10 + 2 * n_w:])
            pl.when(step == 0)(start)
            pl.when(step == (3 * steps) // 4)(forward)
        xv = x_ref[...]
        u = ((xv * _rms(xv)) * nw_ref[...]).astype(BF16)
        u_ref[...] = u
        proj = jnp.concatenate([_dot(u, w_ref[j]) for j in range(N_CHIPS)], axis=1)
        cos, sin = cos_ref[...], sin_ref[...]
        a = ATTN_WIDTH
        q_ref[...] = _rotary_fwd(proj[:, :a], cos, sin)
        k_ref[...] = _rotary_fwd(proj[:, a:2 * a], cos, sin)
        v_ref[...] = proj[:, 2 * a:3 * a]
        hg_ref[...] = proj[:, 3 * a:]
        if n_w:
            pl.when(step == steps - 1)(finish)

    anywhere = pl.BlockSpec(memory_space=pl.ANY)
    return pl.pallas_call(
        body, name="in_proj", grid=(steps,),
        in_specs=[_rows(tm, D_MODEL), _full((1, D_MODEL)), _weight((N_CHIPS, D_MODEL, cw)),
                  _rows(tm, LANES), _rows(tm, LANES)] + [anywhere] * n_w,
        out_specs=[_rows(tm, ATTN_WIDTH)] * 3 + [_rows(tm, 4 * HGRN_WIDTH), _rows(tm, D_MODEL)]
        + [anywhere] * n_w,
        out_shape=[jax.ShapeDtypeStruct((seq, ATTN_WIDTH), F32)] * 3
        + [jax.ShapeDtypeStruct((seq, 4 * HGRN_WIDTH), F32), jax.ShapeDtypeStruct((seq, D_MODEL), BF16)]
        + [_gathered_shape(h) for h in weight_halves],
        scratch_shapes=_gather_scratch(n_w) if n_w else [],
        compiler_params=_cparams("arbitrary"),
    )(x, norm1_w, w_in4, cos_t, sin_t, *weight_halves)


def _head_masks():
    lane = lax.broadcasted_iota(jnp.int32, (1, LANES), 1)
    return [(lane // HEAD_DIM) == h for h in range(LANES // HEAD_DIM)]


def _window_valid(no_prev):
    qi = lax.broadcasted_iota(jnp.int32, (ATTN_BLOCK, 2 * ATTN_BLOCK), 0)
    kj = lax.broadcasted_iota(jnp.int32, (ATTN_BLOCK, 2 * ATTN_BLOCK), 1)
    valid = (kj >= qi) & (kj <= qi + ATTN_BLOCK)
    return valid & (jnp.logical_not(no_prev) | (kj >= ATTN_BLOCK))


def _strided_rows(start, dilation):
    if dilation == 1:
        return pl.ds(start, ATTN_BLOCK)
    return pl.ds(start, ATTN_BLOCK, stride=dilation)


def _block_before(edge_ref, cur_ref, t, r, span, dilation, per_step):
    edge = edge_ref[_strided_rows(ATTN_STEP_ROWS - span + r, dilation), :]
    if per_step == 1:
        return edge
    inside = cur_ref[_strided_rows(r + span * jnp.maximum(t - 1, 0), dilation), :]
    return jnp.where(t == 0, edge, inside)


def _attn_specs():
    cur = pl.BlockSpec((ATTN_STEP_ROWS, LANES), lambda hp, j: (j, hp))
    prev = pl.BlockSpec((ATTN_STEP_ROWS, LANES), lambda hp, j: (jnp.maximum(j - 1, 0), hp))
    return cur, prev


def _for_each_block(dilation, unroll, block):
    span = ATTN_BLOCK * dilation
    per_step = ATTN_STEP_ROWS // span

    def trip(it, carry):
        block(it // dilation, it % dilation, span, per_step)
        return carry

    lax.fori_loop(0, per_step * dilation, trip, 0, unroll=unroll)


def attn_fwd(q, k, v):
    seq = q.shape[0]
    cur, prev = _attn_specs()

    def body(q_ref, kc_ref, vc_ref, kp_ref, vp_ref, y_ref, lse_ref):
        first_step = pl.program_id(1) == 0
        masks = _head_masks()
        for index, (_, dilation) in enumerate(DILATED_PAIRS):
            def block(t, r, span, per_step, dilation=dilation, merge=index > 0):
                rows = _strided_rows(r + span * t, dilation)
                q2 = q_ref[rows, :].astype(BF16)
                kp = _block_before(kp_ref, kc_ref, t, r, span, dilation, per_step)
                vp = _block_before(vp_ref, vc_ref, t, r, span, dilation, per_step)
                k2 = jnp.concatenate([kp, kc_ref[rows, :]], axis=0).astype(BF16)
                v2 = jnp.concatenate([vp, vc_ref[rows, :]], axis=0).astype(BF16)
                valid = _window_valid(first_step & (t == 0))
                o_acc = jnp.zeros((ATTN_BLOCK, LANES), F32)
                l_acc = jnp.zeros((ATTN_BLOCK, LANES), F32)
                for mh in masks:
                    qm = jnp.where(mh, q2, jnp.zeros_like(q2))
                    s = jnp.where(valid, _dot_nt(qm, k2) * ATTN_SCALE, NEG_BIG)
                    m = jnp.max(s, axis=-1, keepdims=True)
                    p = jnp.exp(s - m)
                    l = jnp.sum(p, axis=-1, keepdims=True)
                    o = _dot(p.astype(BF16), v2) / l
                    o_acc = jnp.where(mh, o, o_acc)
                    l_acc = jnp.where(mh, m + jnp.log(l), l_acc)
                if merge:
                    y_old, l_old = y_ref[rows, :], lse_ref[rows, :]
                    mx = jnp.maximum(l_old, l_acc)
                    e_old, e_new = jnp.exp(l_old - mx), jnp.exp(l_acc - mx)
                    den = e_old + e_new
                    o_acc = (y_old * e_old + o_acc * e_new) / den
                    l_acc = mx + jnp.log(den)
                y_ref[rows, :] = o_acc
                lse_ref[rows, :] = l_acc

            _for_each_block(dilation, ATTN_FWD_UNROLL, block)

    return pl.pallas_call(
        body, name="attn_fwd", grid=(ATTN_WIDTH // LANES, seq // ATTN_STEP_ROWS),
        in_specs=[cur, cur, cur, prev, prev],
        out_specs=[cur, cur],
        out_shape=[jax.ShapeDtypeStruct((seq, ATTN_WIDTH), F32)] * 2,
        compiler_params=_cparams("parallel", "parallel"),
    )(q, k, v, k, v)


def _chunk_cumsum(x, reverse=False):
    rc = lax.broadcasted_iota(jnp.int32, x.shape, 0) % HGRN_CHUNK
    sh = 1
    while sh < HGRN_CHUNK:
        if reverse:
            x = x + jnp.where(rc + sh < HGRN_CHUNK, pltpu.roll(x, x.shape[0] - sh, axis=0), 0.0)
        else:
            x = x + jnp.where(rc >= sh, pltpu.roll(x, sh, axis=0), 0.0)
        sh *= 2
    return x


def _chunk_row(x, row):
    return _chunk_rows([x[n * HGRN_CHUNK + row:n * HGRN_CHUNK + row + 1, :]
                        for n in range(x.shape[0] // HGRN_CHUNK)])


def _chunk_rows(rows):
    return jnp.concatenate([jnp.broadcast_to(r, (HGRN_CHUNK, r.shape[1])) for r in rows], axis=0)


def _hgrn_prep(hg, lbl):
    w = HGRN_WIDTH
    a0, a1 = lbl[0:1, :], lbl[1:2, :]
    mx = jnp.maximum(a0, a1)
    e0, e1 = jnp.exp(a0 - mx), jnp.exp(a1 - mx)
    lb = e0 / (e0 + e1)
    qb, fb, gb = hg[:, :w], hg[:, w:2 * w], hg[:, 3 * w:]
    sg = _sigmoid(fb)
    f = lb + (1.0 - lb) * sg
    b = _chunk_cumsum(jnp.log(f))
    bmid, btot = _chunk_row(b, HGRN_CHUNK // 2 - 1), _chunk_row(b, HGRN_CHUNK - 1)
    sq = _sigmoid(qb)
    p = dict(lb=lb, sg=sg, f=f, kk=1.0 - f, sq=sq, qf=qb * sq, gb=gb,
             e_iq=jnp.exp(b - bmid), e_ik=jnp.exp(bmid - b), e_b=jnp.exp(b),
             e_bb=jnp.exp(btot - b), e_tot=jnp.exp(btot))
    p["qi"] = p["qf"] * p["e_iq"]
    p["ki"] = p["kk"] * p["e_ik"]
    p["qs"] = p["qf"] * p["e_b"]
    p["kb"] = p["kk"] * p["e_bb"]
    return p


def _chunk_masks():
    t = lax.broadcasted_iota(jnp.int32, (HGRN_ROWS, HGRN_ROWS), 0)
    s = lax.broadcasted_iota(jnp.int32, (HGRN_ROWS, HGRN_ROWS), 1)
    tril = ((t // HGRN_CHUNK) == (s // HGRN_CHUNK)) & (s <= t)
    n_chunks = HGRN_ROWS // HGRN_CHUNK
    tt = lax.broadcasted_iota(jnp.int32, (HGRN_ROWS, n_chunks * LANES), 0)
    cc = lax.broadcasted_iota(jnp.int32, (HGRN_ROWS, n_chunks * LANES), 1)
    block = (tt // HGRN_CHUNK) == (cc // LANES)
    return tril, block


def _spread(x, block):
    n_chunks = HGRN_ROWS // HGRN_CHUNK
    return jnp.where(block, jnp.tile(x, (1, n_chunks)), jnp.zeros((), x.dtype))


def _fold(x_full, block):
    n_chunks = HGRN_ROWS // HGRN_CHUNK
    z = jnp.where(block, x_full, 0.0)
    acc = z[:, :LANES]
    for n in range(1, n_chunks):
        acc = acc + z[:, n * LANES:(n + 1) * LANES]
    return acc


def hgrn_fwd(hg, lb_logits, hnw, weight_halves=()):
    seq = hg.shape[0]
    nblk = seq // HGRN_ROWS
    n_chunks = HGRN_ROWS // HGRN_CHUNK
    n_w = len(weight_halves)

    def body(*refs):
        hg_ref, lbl_ref, hnw_ref = refs[:3]
        w_refs = refs[3:3 + n_w]
        yb_ref, o_ref, st0_ref = refs[3 + n_w:6 + n_w]
        g_refs = refs[6 + n_w:6 + 2 * n_w]
        st_scr = refs[6 + 2 * n_w]
        step = pl.program_id(0)
        if n_w:
            start, forward, finish = _gather_phases(w_refs, g_refs, *refs[7 + 2 * n_w:])
            pl.when(step == 0)(start)
            pl.when(step == (3 * nblk) // 4)(forward)

        @pl.when(step == 0)
        def _():
            st_scr[...] = jnp.zeros_like(st_scr)

        hg_v = hg_ref[...]
        p = _hgrn_prep(hg_v, lbl_ref[...])
        tril, block = _chunk_masks()
        vv = hg_v[:, 2 * HGRN_WIDTH:3 * HGRN_WIDTH].astype(BF16)
        outs = []
        for h in range(HGRN_HEADS):
            sl = slice(h * LANES, (h + 1) * LANES)
            v_h = vv[:, sl]
            a = jnp.where(tril, _dot_nt(p["qi"][:, sl].astype(BF16), p["ki"][:, sl].astype(BF16)), 0.0)
            o = _dot(a.astype(BF16), v_h)
            upd = _dot_tn(v_h, _spread(p["kb"][:, sl].astype(BF16), block))
            st = st_scr[h]
            st0_ref[h] = st
            parts = []
            for n in range(n_chunks):
                parts.append(st.astype(BF16))
                decay = p["e_tot"][n * HGRN_CHUNK:n * HGRN_CHUNK + 1, sl]
                st = st * decay + upd[:, n * LANES:(n + 1) * LANES]
            st_scr[h] = st
            o = o + _dot_nt(_spread(p["qs"][:, sl].astype(BF16), block), jnp.concatenate(parts, axis=1))
            outs.append(o)
        o_all = jnp.concatenate(outs, axis=1)
        o_ref[...] = o_all
        normed = jnp.concatenate(
            [outs[h] * _rms(outs[h]) for h in range(HGRN_HEADS)], axis=1)
        gb = p["gb"]
        yb_ref[...] = (normed * hnw_ref[...]) * (gb * _sigmoid(gb))
        if n_w:
            pl.when(step == nblk - 1)(finish)

    anywhere = pl.BlockSpec(memory_space=pl.ANY)
    return pl.pallas_call(
        body, name="hgrn_fwd", grid=(nblk,),
        in_specs=[_rows(HGRN_ROWS, 4 * HGRN_WIDTH), _full((2, HGRN_WIDTH)), _full((1, HGRN_WIDTH))]
        + [anywhere] * n_w,
        out_specs=[_rows(HGRN_ROWS, HGRN_WIDTH), _rows(HGRN_ROWS, HGRN_WIDTH),
                   pl.BlockSpec((None, HGRN_HEADS, LANES, LANES), lambda i: (i, 0, 0, 0))] + [anywhere] * n_w,
        out_shape=[jax.ShapeDtypeStruct((seq, HGRN_WIDTH), F32)] * 2
        + [jax.ShapeDtypeStruct((nblk, HGRN_HEADS, LANES, LANES), F32)]
        + [_gathered_shape(h) for h in weight_halves],
        scratch_shapes=[pltpu.VMEM((HGRN_HEADS, LANES, LANES), F32)] + (_gather_scratch(n_w) if n_w else []),
        compiler_params=_cparams("arbitrary"),
    )(hg, lb_logits, hnw, *weight_halves)


def ffn_fwd(ya, yb, x, w_out, norm2_w, w_gu4, w_down, final_w, target):
    seq = x.shape[0]
    tm = ROW_TILE
    cw = w_gu4.shape[2]
    inv_d = 1.0 / D_MODEL

    def body(ya_ref, yb_ref, x_ref, wo_ref, nw_ref, wgu_ref, wd_ref, fw_ref, t_ref,
             mixed_ref, h1_ref, u2_ref, g_ref, up_ref, act_ref, dh2_ref, acc_ref):
        @pl.when(pl.program_id(0) == 0)
        def _():
            acc_ref[...] = jnp.zeros_like(acc_ref)

        mixed = jnp.concatenate([ya_ref[...], yb_ref[...]], axis=1).astype(BF16)
        mixed_ref[...] = mixed
        h1 = x_ref[...] + _dot(mixed, wo_ref[...])
        h1_ref[...] = h1
        u2 = ((h1 * _rms(h1)) * nw_ref[...]).astype(BF16)
        u2_ref[...] = u2
        g = jnp.concatenate([_dot(u2, wgu_ref[0]), _dot(u2, wgu_ref[1])], axis=1)
        up = jnp.concatenate([_dot(u2, wgu_ref[2]), _dot(u2, wgu_ref[3])], axis=1)
        g_ref[...] = g.astype(BF16)
        up_ref[...] = up.astype(BF16)
        act = ((g * _sigmoid(g)) * up).astype(BF16)
        act_ref[...] = act
        h2 = h1 + _dot(act, wd_ref[...])
        rf = _rms(h2)
        n = h2 * rf
        fw = fw_ref[...]
        err = n * fw - t_ref[...]
        dy = err * inv_d
        acc_ref[0:1, :] += jnp.sum(dy * n, axis=0, keepdims=True)
        acc_ref[1:2, :] += (0.5 * inv_d) * jnp.sum(err * err, axis=0, keepdims=True)
        dn = dy * fw
        dh2_ref[...] = rf * (dn - n * jnp.mean(dn * n, axis=-1, keepdims=True))

    half = _rows(tm, ATTN_WIDTH)
    wide = _rows(tm, D_MODEL)
    ffn = _rows(tm, FFN_HIDDEN)
    return pl.pallas_call(
        body, name="ffn_fwd", grid=(seq // tm,),
        in_specs=[half, half, wide, _weight((D_MODEL, D_MODEL)), _full((1, D_MODEL)),
                  _weight((N_CHIPS, D_MODEL, cw)), _weight((FFN_HIDDEN, D_MODEL)), _full((1, D_MODEL)), wide],
        out_specs=[wide, wide, wide, ffn, ffn, ffn, wide, _full((8, D_MODEL))],
        out_shape=[jax.ShapeDtypeStruct((seq, D_MODEL), BF16), jax.ShapeDtypeStruct((seq, D_MODEL), F32),
                   jax.ShapeDtypeStruct((seq, D_MODEL), BF16)]
        + [jax.ShapeDtypeStruct((seq, FFN_HIDDEN), BF16)] * 3
        + [jax.ShapeDtypeStruct((seq, D_MODEL), F32), jax.ShapeDtypeStruct((8, D_MODEL), F32)],
        compiler_params=_cparams("arbitrary"),
    )(ya, yb, x, w_out, norm2_w, w_gu4, w_down, final_w, target)


def _head_sum_matrix():
    i = jnp.arange(ATTN_WIDTH)
    return ((i[:, None] // HEAD_DIM) == (i[None, :] // HEAD_DIM)).astype(BF16)


def ffn_bwd(dh2, w_down, g, up, w_gu4, h1, norm2_w, w_out, ya):
    seq = h1.shape[0]
    tm = ROW_TILE
    cw = w_gu4.shape[2]
    hsum = _head_sum_matrix()

    def body(dh2_ref, wd_ref, g_ref, up_ref, w_ref, h1_ref, nw_ref, wo_ref, ya_ref, hs_ref,
             dgu_ref, dh1_ref, dya_ref, dyb_ref, delta_ref, acc_ref):
        @pl.when(pl.program_id(0) == 0)
        def _():
            acc_ref[...] = jnp.zeros_like(acc_ref)

        dh2_b = dh2_ref[...].astype(BF16)
        du2 = jnp.zeros((tm, D_MODEL), F32)
        for j in range(N_CHIPS // 2):
            dact = _dot_nt(dh2_b, wd_ref[j * cw:(j + 1) * cw, :])
            gv = g_ref[:, j * cw:(j + 1) * cw].astype(F32)
            sg = _sigmoid(gv)
            dg = (dact * up_ref[:, j * cw:(j + 1) * cw].astype(F32) * (sg * (1.0 + gv * (1.0 - sg)))).astype(BF16)
            dup = (dact * (gv * sg)).astype(BF16)
            dgu_ref[:, j * cw:(j + 1) * cw] = dg
            dgu_ref[:, FFN_HIDDEN + j * cw:FFN_HIDDEN + (j + 1) * cw] = dup
            du2 = du2 + _dot_nt(dg, w_ref[j]) + _dot_nt(dup, w_ref[N_CHIPS // 2 + j])
        h1 = h1_ref[...]
        r2 = _rms(h1)
        nh = h1 * r2
        acc_ref[0:1, :] += jnp.sum(du2 * nh, axis=0, keepdims=True)
        dn = du2 * nw_ref[...]
        dh1 = dh2_ref[...] + r2 * (dn - nh * jnp.mean(dn * nh, axis=-1, keepdims=True))
        dh1_ref[...] = dh1
        dmixed = _dot_nt(dh1.astype(BF16), wo_ref[...])
        dya = dmixed[:, :ATTN_WIDTH]
        dya_ref[...] = dya
        dyb_ref[...] = dmixed[:, ATTN_WIDTH:]
        prod = dya * ya_ref[...]
        hi = prod.astype(BF16)
        lo = (prod - hi.astype(F32)).astype(BF16)
        delta_ref[...] = _dot(hi, hs_ref[...]) + _dot(lo, hs_ref[...])

    wide = _rows(tm, D_MODEL)
    half = _rows(tm, ATTN_WIDTH)
    ffn = _rows(tm, FFN_HIDDEN)
    return pl.pallas_call(
        body, name="ffn_bwd", grid=(seq // tm,),
        in_specs=[wide, _weight((FFN_HIDDEN, D_MODEL)), ffn, ffn, _weight((N_CHIPS, D_MODEL, cw)), wide,
                  _full((1, D_MODEL)), _weight((D_MODEL, D_MODEL)), half, _full((ATTN_WIDTH, ATTN_WIDTH))],
        out_specs=[_rows(tm, 2 * FFN_HIDDEN), wide, half, half, half, _full((8, D_MODEL))],
        out_shape=[jax.ShapeDtypeStruct((seq, 2 * FFN_HIDDEN), BF16), jax.ShapeDtypeStruct((seq, D_MODEL), F32)]
        + [jax.ShapeDtypeStruct((seq, ATTN_WIDTH), F32)] * 3 + [jax.ShapeDtypeStruct((8, D_MODEL), F32)],
        compiler_params=_cparams("arbitrary"),
    )(dh2, w_down, g, up, w_gu4, h1, norm2_w, w_out, ya, hsum)


def attn_bwd(q, k, v, dy, lse, delta, sibling_grads=()):
    seq = q.shape[0]
    cur, prev = _attn_specs()
    whole = pl.BlockSpec((seq, LANES), lambda hp, j: (0, hp))
    n_g = len(sibling_grads)
    n_hp, n_steps = ATTN_WIDTH // LANES, seq // ATTN_STEP_ROWS

    def body(*refs):
        q_ref, dy_ref, lse_ref, dl_ref, kc_ref, vc_ref, kp_ref, vp_ref = refs[:8]
        dq_ref, dk_ref, dv_ref = refs[8 + n_g:11 + n_g]
        first_step = pl.program_id(1) == 0
        base = pl.program_id(1) * ATTN_STEP_ROWS
        masks = _head_masks()
        if n_g:
            start, finish = _sibling_exchange_phases(
                refs[8:8 + n_g], refs[11 + n_g:11 + 2 * n_g], *refs[11 + 2 * n_g:])
            pl.when((pl.program_id(0) == 0) & first_step)(start)

        def block(t, r, span, per_step, dilation, add):
            rows = _strided_rows(r + span * t, dilation)
            at_edge = t == 0
            q2, dy2 = q_ref[rows, :].astype(BF16), dy_ref[rows, :].astype(BF16)
            lse2, dl2 = lse_ref[rows, :], dl_ref[rows, :]
            kp = _block_before(kp_ref, kc_ref, t, r, span, dilation, per_step)
            vp = _block_before(vp_ref, vc_ref, t, r, span, dilation, per_step)
            k2 = jnp.concatenate([kp, kc_ref[rows, :]], axis=0).astype(BF16)
            v2 = jnp.concatenate([vp, vc_ref[rows, :]], axis=0).astype(BF16)
            valid = _window_valid(first_step & at_edge)
            zero = jnp.zeros_like(q2)
            qms, dyms, ps, dss, kms = [], [], [], [], []
            for h, mh in enumerate(masks):
                c0 = h * HEAD_DIM
                qm, dym = jnp.where(mh, q2, zero), jnp.where(mh, dy2, zero)
                s = _dot_nt(qm, k2) * ATTN_SCALE
                p = jnp.where(valid, jnp.exp(s - lse2[:, c0:c0 + 1]), 0.0)
                dp = _dot_nt(dym, v2)
                dss.append((p * (dp - dl2[:, c0:c0 + 1]) * ATTN_SCALE).astype(BF16))
                ps.append(p.astype(BF16))
                qms.append(qm)
                dyms.append(dym)
                kms.append(jnp.where(mh, k2, jnp.zeros_like(k2)))
            dq = _dot(jnp.concatenate(dss, axis=1), jnp.concatenate(kms, axis=0))
            dv_full = _dot_tn(jnp.concatenate(ps, axis=0), jnp.concatenate(dyms, axis=0))
            dk_full = _dot_tn(jnp.concatenate(dss, axis=0), jnp.concatenate(qms, axis=0))
            here = _strided_rows(base + r + span * t, dilation)
            if add:
                dq_ref[rows, :] += dq
                dk_ref[here, :] += dk_full[ATTN_BLOCK:]
                dv_ref[here, :] += dv_full[ATTN_BLOCK:]
            else:
                dq_ref[rows, :] = dq
                dk_ref[here, :] = dk_full[ATTN_BLOCK:]
                dv_ref[here, :] = dv_full[ATTN_BLOCK:]
            back = _strided_rows(jnp.maximum(base + r + span * t - span, r), dilation)
            dk_ref[back, :] += dk_full[:ATTN_BLOCK]
            dv_ref[back, :] += dv_full[:ATTN_BLOCK]

        for index, (_, dilation) in enumerate(DILATED_PAIRS):
            _for_each_block(dilation, ATTN_BWD_UNROLL,
                            functools.partial(block, dilation=dilation, add=index > 0))
        if n_g:
            pl.when((pl.program_id(0) == n_hp - 1) & (pl.program_id(1) == n_steps - 1))(finish)

    anywhere = pl.BlockSpec(memory_space=pl.ANY)
    return pl.pallas_call(
        body, name="attn_bwd", grid=(n_hp, n_steps),
        in_specs=[cur] * 6 + [prev, prev] + [anywhere] * n_g,
        out_specs=[cur, whole, whole] + [anywhere] * n_g,
        out_shape=[jax.ShapeDtypeStruct((seq, ATTN_WIDTH), F32)] * 3 + _sibling_exchange_shapes(sibling_grads),
        scratch_shapes=_sibling_exchange_scratch(n_g) if n_g else [],
        compiler_params=_cparams("arbitrary", "arbitrary"),
    )(q, dy, lse, delta, k, v, k, v, *sibling_grads)


def hgrn_bwd(hg, lb_logits, hnw, o_pre, st0, dyb, chip_sums=()):
    seq = hg.shape[0]
    nblk = seq // HGRN_ROWS
    n_chunks = HGRN_ROWS // HGRN_CHUNK
    w = HGRN_WIDTH
    n_s = len(chip_sums)

    def body(*refs):
        hg_ref, lbl_ref, hnw_ref, o_ref, st0_ref, dyb_ref = refs[:6]
        dhg_ref, acc_ref = refs[6 + n_s:8 + n_s]
        dst_scr = refs[8 + 2 * n_s]
        step = pl.program_id(0)
        if n_s:
            start, finish = _chip_exchange_phases(refs[6:6 + n_s], refs[8 + n_s:8 + 2 * n_s], *refs[9 + 2 * n_s:])
            pl.when(step == 0)(start)

        @pl.when(step == 0)
        def _():
            dst_scr[...] = jnp.zeros_like(dst_scr)
            acc_ref[...] = jnp.zeros_like(acc_ref)

        hg_v = hg_ref[...]
        p = _hgrn_prep(hg_v, lbl_ref[...])
        tril, block = _chunk_masks()
        vv = hg_v[:, 2 * w:3 * w].astype(BF16)
        hnw_v = hnw_ref[...]
        gb = p["gb"]
        sgg = _sigmoid(gb)
        silu_g = gb * sgg
        dyb_v = dyb_ref[...]
        o_v = o_ref[...]

        d_on = dyb_v * hnw_v * silu_g
        on_parts, do_parts = [], []
        for h in range(HGRN_HEADS):
            sl = slice(h * LANES, (h + 1) * LANES)
            rs = _rms(o_v[:, sl])
            on = o_v[:, sl] * rs
            on_parts.append(on)
            do_parts.append(rs * (d_on[:, sl] - on * jnp.mean(d_on[:, sl] * on, axis=-1, keepdims=True)))
        on_all = jnp.concatenate(on_parts, axis=1)
        dgb = dyb_v * on_all * hnw_v * (sgg * (1.0 + gb * (1.0 - sgg)))
        acc_ref[0:1, :] += jnp.sum(dyb_v * on_all * silu_g, axis=0, keepdims=True)

        dqf_parts, dkk_parts, db_parts, dv_parts, dbt_parts, dkbkb_parts = [], [], [], [], [], []
        for h in range(HGRN_HEADS):
            sl = slice(h * LANES, (h + 1) * LANES)
            v_h = vv[:, sl]
            do_h = do_parts[h].astype(BF16)
            qi, ki, qs, kb = p["qi"][:, sl], p["ki"][:, sl], p["qs"][:, sl], p["kb"][:, sl]
            qi_b, ki_b = qi.astype(BF16), ki.astype(BF16)
            kb_cat = _spread(kb.astype(BF16), block)
            qs_cat = _spread(qs.astype(BF16), block)
            upd = _dot_tn(v_h, kb_cat)
            st = st0_ref[h]
            st_parts = []
            for n in range(n_chunks):
                st_parts.append(st)
                decay = p["e_tot"][n * HGRN_CHUNK:n * HGRN_CHUNK + 1, sl]
                st = st * decay + upd[:, n * LANES:(n + 1) * LANES]
            st_cat = jnp.concatenate([s_.astype(BF16) for s_ in st_parts], axis=1)
            wgt = _dot_tn(do_h, qs_cat)
            dst = dst_scr[h]
            dst_parts = [None] * n_chunks
            dbt_rows = [None] * n_chunks
            for n in reversed(range(n_chunks)):
                dst_parts[n] = dst.astype(BF16)
                decay = p["e_tot"][n * HGRN_CHUNK:n * HGRN_CHUNK + 1, sl]
                dbt_rows[n] = jnp.sum(dst * st_parts[n], axis=0, keepdims=True) * decay
                dst = dst * decay + wgt[:, n * LANES:(n + 1) * LANES]
            dst_scr[h] = dst
            dst_cat = jnp.concatenate(dst_parts, axis=1)
            dqs = _fold(_dot(do_h, st_cat), block)
            dkb = _fold(_dot(v_h, dst_cat), block)
            dv_state = _dot_nt(kb_cat, dst_cat)
            a = jnp.where(tril, _dot_nt(qi_b, ki_b), 0.0).astype(BF16)
            da = jnp.where(tril, _dot_nt(do_h, v_h), 0.0).astype(BF16)
            dv_parts.append(_dot_tn(a, do_h) + dv_state)
            dqi = _dot(da, ki_b)
            dki = _dot_tn(da, qi_b)
            dqf_parts.append(dqi * p["e_iq"][:, sl] + dqs * p["e_b"][:, sl])
            dkk_parts.append(dki * p["e_ik"][:, sl] + dkb * p["e_bb"][:, sl])
            dkbkb = dkb * kb
            db_parts.append(dqi * qi - dki * ki + dqs * qs - dkbkb)
            dkbkb_parts.append(dkbkb)
            dbt_parts.append(_chunk_rows(dbt_rows))

        cat = lambda parts: jnp.concatenate(parts, axis=1)
        dlogf = (_chunk_cumsum(cat(db_parts), reverse=True)
                 + _chunk_row(_chunk_cumsum(cat(dkbkb_parts)), HGRN_CHUNK - 1) + cat(dbt_parts))
        sq, qb = p["sq"], hg_v[:, :w]
        dqb = cat(dqf_parts) * (sq * (1.0 + qb * (1.0 - sq)))
        df = dlogf / p["f"] - cat(dkk_parts)
        sg, lb = p["sg"], p["lb"]
        dfb = df * (1.0 - lb) * sg * (1.0 - sg)
        acc_ref[1:2, :] += jnp.sum(df * (1.0 - sg), axis=0, keepdims=True)
        dhg_ref[...] = jnp.concatenate([dqb, dfb, cat(dv_parts), dgb], axis=1)
        if n_s:
            pl.when(step == nblk - 1)(finish)

    rev = lambda i: (nblk - 1 - i, 0)
    anywhere = pl.BlockSpec(memory_space=pl.ANY)
    return pl.pallas_call(
        body, name="hgrn_bwd", grid=(nblk,),
        in_specs=[pl.BlockSpec((HGRN_ROWS, 4 * w), rev), _full((2, w)), _full((1, w)),
                  pl.BlockSpec((HGRN_ROWS, w), rev),
                  pl.BlockSpec((None, HGRN_HEADS, LANES, LANES), lambda i: (nblk - 1 - i, 0, 0, 0)),
                  pl.BlockSpec((HGRN_ROWS, w), rev)] + [anywhere] * n_s,
        out_specs=[pl.BlockSpec((HGRN_ROWS, 4 * w), rev), _full((8, w))] + [anywhere] * n_s,
        out_shape=[jax.ShapeDtypeStruct((seq, 4 * w), F32), jax.ShapeDtypeStruct((8, w), F32)]
        + [jax.ShapeDtypeStruct(s.shape, s.dtype) for s in chip_sums],
        scratch_shapes=[pltpu.VMEM((HGRN_HEADS, LANES, LANES), F32)] + (_chip_exchange_scratch(n_s) if n_s else []),
        compiler_params=_cparams("arbitrary"),
    )(hg, lb_logits, hnw, o_pre, st0, dyb, *chip_sums)


def in_bwd(dq, dk, dv, dhg, cos_t, sin_t, w_in4, x, norm1_w, dh1):
    seq = x.shape[0]
    tm = ROW_TILE
    cw = w_in4.shape[2]

    def body(dq_ref, dk_ref, dv_ref, dhg_ref, cos_ref, sin_ref, w_ref,
             x_ref, nw_ref, dh1_ref, dproj_ref, dx_ref, acc_ref):
        @pl.when(pl.program_id(0) == 0)
        def _():
            acc_ref[...] = jnp.zeros_like(acc_ref)

        cos, sin = cos_ref[...], sin_ref[...]
        dqa = _rotary_bwd(dq_ref[...], cos, sin)
        dka = _rotary_bwd(dk_ref[...], cos, sin)
        dproj = jnp.concatenate([dqa, dka, dv_ref[...], dhg_ref[...]], axis=1).astype(BF16)
        dproj_ref[...] = dproj
        du = _dot_nt(dproj[:, :cw], w_ref[0])
        for j in range(1, N_CHIPS):
            du = du + _dot_nt(dproj[:, j * cw:(j + 1) * cw], w_ref[j])
        xv = x_ref[...]
        r1 = _rms(xv)
        nx = xv * r1
        acc_ref[0:1, :] += jnp.sum(du * nx, axis=0, keepdims=True)
        dn = du * nw_ref[...]
        dx_ref[...] = dh1_ref[...] + r1 * (dn - nx * jnp.mean(dn * nx, axis=-1, keepdims=True))

    half = _rows(tm, ATTN_WIDTH)
    wide = _rows(tm, D_MODEL)
    return pl.pallas_call(
        body, name="in_bwd", grid=(seq // tm,),
        in_specs=[half] * 3 + [_rows(tm, 4 * HGRN_WIDTH), _rows(tm, LANES), _rows(tm, LANES),
                               _weight((N_CHIPS, D_MODEL, cw)), wide, _full((1, D_MODEL)), wide],
        out_specs=[_rows(tm, IN_PROJ_WIDTH), wide, _full((8, D_MODEL))],
        out_shape=[jax.ShapeDtypeStruct((seq, IN_PROJ_WIDTH), BF16), jax.ShapeDtypeStruct((seq, D_MODEL), F32),
                   jax.ShapeDtypeStruct((8, D_MODEL), F32)],
        compiler_params=_cparams("arbitrary"),
    )(dq, dk, dv, dhg, cos_t, sin_t, w_in4, x, norm1_w, dh1)


def weight_grad(a, b, col_block, name, group=1):
    seq, kdim = a.shape
    ndim = b.shape[1]
    nj = ndim // col_block
    tk = 512

    def body(a_ref, b_ref, o_ref):
        @pl.when(pl.program_id(1) == 0)
        def _():
            o_ref[...] = jnp.zeros_like(o_ref)

        acc = _dot_tn(a_ref[...].astype(BF16), b_ref[...].astype(BF16))
        for i in range(group):
            o_ref[i] += acc[:, i * col_block:(i + 1) * col_block]

    return pl.pallas_call(
        body, name=name, grid=(nj // group, seq // tk),
        in_specs=[pl.BlockSpec((tk, kdim), lambda j, t: (t, 0)),
                  pl.BlockSpec((tk, group * col_block), lambda j, t: (t, j))],
        out_specs=pl.BlockSpec((group, kdim, col_block), lambda j, t: (j, 0, 0)),
        out_shape=jax.ShapeDtypeStruct((nj, kdim, col_block), F32),
        compiler_params=_cparams("parallel", "arbitrary"),
    )(a, b)


def _sibling_exchange_phases(g_refs, out_refs, send_sems, recv_sems):
    x, y, cc = _mesh_pos()

    def copies():
        return [pltpu.make_async_remote_copy(
            src_ref=g_refs[i].at[j, 1 - cc], dst_ref=out_refs[i].at[j],
            send_sem=send_sems.at[i * N_CHIPS + j], recv_sem=recv_sems.at[i * N_CHIPS + j],
            device_id=(x, y, 1 - cc), device_id_type=MESH_ID)
            for i in range(len(g_refs)) for j in range(N_CHIPS)]

    def start():
        for cp in copies():
            cp.start()

    def finish():
        for cp in copies():
            cp.wait_recv()
        for cp in copies():
            cp.wait_send()

    return start, finish


def _sibling_exchange_scratch(n):
    return [pltpu.SemaphoreType.DMA((n * N_CHIPS,)), pltpu.SemaphoreType.DMA((n * N_CHIPS,))]


def _sibling_exchange_shapes(grads):
    return [jax.ShapeDtypeStruct((N_CHIPS,) + g.shape[2:], g.dtype) for g in grads]


def exchange_with_sibling(grads, name):
    n = len(grads)

    def body(*refs):
        start, finish = _sibling_exchange_phases(refs[:n], refs[n:2 * n], refs[2 * n], refs[2 * n + 1])
        start()
        finish()

    return pl.pallas_call(
        body, name=name,
        in_specs=[pl.BlockSpec(memory_space=pl.ANY)] * n,
        out_specs=[pl.BlockSpec(memory_space=pl.ANY)] * n,
        out_shape=_sibling_exchange_shapes(grads),
        scratch_shapes=_sibling_exchange_scratch(n),
    )(*grads)


def add_own_half(grad, recv, name):
    _, _, r, c = grad.shape
    tr = r // 2 if r % 32 == 0 else r

    def body(cc_ref, g_ref, r_ref, o_ref):
        o_ref[...] = (g_ref[...] + r_ref[...]).astype(BF16)

    grid_spec = pltpu.PrefetchScalarGridSpec(
        num_scalar_prefetch=1, grid=(N_CHIPS, r // tr),
        in_specs=[pl.BlockSpec((None, None, tr, c), lambda j, t, cc: (j, cc[0], t, 0)),
                  pl.BlockSpec((None, tr, c), lambda j, t, cc: (j, t, 0))],
        out_specs=pl.BlockSpec((None, tr, c), lambda j, t, cc: (j, t, 0)))
    cc = lax.axis_index("c").astype(jnp.int32).reshape(1)
    return pl.pallas_call(
        body, name=name, grid_spec=grid_spec,
        out_shape=jax.ShapeDtypeStruct((N_CHIPS, r, c), BF16),
        compiler_params=_cparams("parallel", "parallel"),
    )(cc, grad, recv)


def _chip_exchange_phases(s_refs, out_refs, send_sems, recv_sems):
    n = len(s_refs)
    x, y, cc = _mesh_pos()
    my_chip = 2 * x + y
    chips = [(1 - x, y), (x, 1 - y), (1 - x, 1 - y)]

    def outgoing():
        return [pltpu.make_async_remote_copy(
            src_ref=s_refs[i].at[2 * px + py], dst_ref=out_refs[i].at[my_chip],
            send_sem=send_sems.at[3 * i + j], recv_sem=recv_sems.at[3 * i + j],
            device_id=(px, py, cc), device_id_type=MESH_ID)
            for i in range(n) for j, (px, py) in enumerate(chips)]

    def start():
        for cp in outgoing():
            cp.start()

    def finish():
        for i in range(n):
            for j, (px, py) in enumerate(chips):
                pltpu.make_async_remote_copy(
                    src_ref=s_refs[i].at[my_chip], dst_ref=out_refs[i].at[2 * px + py],
                    send_sem=send_sems.at[3 * i + j], recv_sem=recv_sems.at[3 * i + j],
                    device_id=(px, py, cc), device_id_type=MESH_ID).wait_recv()
        for cp in outgoing():
            cp.wait_send()

    return start, finish


def _chip_exchange_scratch(n):
    return [pltpu.SemaphoreType.DMA((3 * n,)), pltpu.SemaphoreType.DMA((3 * n,))]


def exchange_between_chips(sums):
    n = len(sums)

    def body(*refs):
        start, finish = _chip_exchange_phases(refs[:n], refs[n:2 * n], refs[2 * n], refs[2 * n + 1])
        start()
        finish()

    return pl.pallas_call(
        body, name="grad_exchange_chips",
        in_specs=[pl.BlockSpec(memory_space=pl.ANY)] * n,
        out_specs=[pl.BlockSpec(memory_space=pl.ANY)] * n,
        out_shape=[jax.ShapeDtypeStruct(s.shape, s.dtype) for s in sums],
        scratch_shapes=_chip_exchange_scratch(n),
    )(*sums)


def sum_chips(sums, parts, name):
    _, r, c = parts.shape
    tr = r // 2 if r % 32 == 0 else r

    def body(idx_ref, s_ref, p1_ref, p2_ref, p3_ref, o_ref):
        o_ref[...] = ((s_ref[...].astype(F32) + p1_ref[...].astype(F32))
                      + p2_ref[...].astype(F32)) + p3_ref[...].astype(F32)

    def pick(k):
        return pl.BlockSpec((None, tr, c), lambda t, idx: (idx[k], t, 0))

    x, y = lax.axis_index("x"), lax.axis_index("y")
    idx = jnp.stack([2 * x + y, 2 * (1 - x) + y, 2 * x + (1 - y), 2 * (1 - x) + (1 - y)]).astype(jnp.int32)
    grid_spec = pltpu.PrefetchScalarGridSpec(
        num_scalar_prefetch=1, grid=(r // tr,),
        in_specs=[pick(0), pick(1), pick(2), pick(3)],
        out_specs=pl.BlockSpec((tr, c), lambda t, idx: (t, 0)))
    return pl.pallas_call(
        body, name=name, grid_spec=grid_spec,
        out_shape=jax.ShapeDtypeStruct((r, c), F32),
        compiler_params=_cparams("parallel"),
    )(idx, sums, parts, parts, parts)


def share_with_sibling(halves):
    n = len(halves)

    def body(*refs):
        h_refs, out_refs = refs[:n], refs[n:2 * n]
        send_sems, recv_sems = refs[2 * n], refs[2 * n + 1]
        x, y, cc = _mesh_pos()
        copies = [pltpu.make_async_remote_copy(
            src_ref=h_refs[i], dst_ref=out_refs[i],
            send_sem=send_sems.at[i], recv_sem=recv_sems.at[i],
            device_id=(x, y, 1 - cc), device_id_type=MESH_ID) for i in range(n)]
        for cp in copies:
            cp.start()
        for cp in copies:
            cp.wait_recv()
        for cp in copies:
            cp.wait_send()

    return pl.pallas_call(
        body, name="grad_share_sibling",
        in_specs=[pl.BlockSpec(memory_space=pl.ANY)] * n,
        out_specs=[pl.BlockSpec(memory_space=pl.ANY)] * n,
        out_shape=[jax.ShapeDtypeStruct(h.shape, h.dtype) for h in halves],
        scratch_shapes=[pltpu.SemaphoreType.DMA((n,)), pltpu.SemaphoreType.DMA((n,))],
    )(*halves)


def _adam_update(w, g, m, v):
    m = ADAM_B1 * m + (1.0 - ADAM_B1) * g
    v = ADAM_B2 * v + (1.0 - ADAM_B2) * (g * g)
    m_hat = m / (1.0 - ADAM_B1 ** ADAM_STEP)
    v_hat = v / (1.0 - ADAM_B2 ** ADAM_STEP)
    delta = -ADAM_LR * (m_hat / (jnp.sqrt(v_hat) + ADAM_EPS) + ADAM_WD * w)
    return delta, m, v


def adamw(w, g_mine, g_sibling, m, v, name):
    r, c = w.shape
    half = r // 2
    tr = half // 2 if half % 16 == 0 else half
    nt = half // tr

    def body(cc_ref, w_ref, ga_ref, gb_ref, m_ref, v_ref, g_ref, d_ref, nm_ref, nv_ref):
        g = jnp.where(pl.program_id(0) == cc_ref[0], ga_ref[...], gb_ref[...])
        g_ref[...] = g
        d, nm, nv = _adam_update(w_ref[...], g, m_ref[...], v_ref[...])
        d_ref[...] = d
        nm_ref[...] = nm
        nv_ref[...] = nv

    full = pl.BlockSpec((tr, c), lambda h, t, cc: (h * nt + t, 0))
    part = pl.BlockSpec((tr, c), lambda h, t, cc: (t, 0))
    grid_spec = pltpu.PrefetchScalarGridSpec(
        num_scalar_prefetch=1, grid=(2, nt),
        in_specs=[full, part, part, full, full], out_specs=[full] * 4)
    cc = lax.axis_index("c").astype(jnp.int32).reshape(1)
    return pl.pallas_call(
        body, name=name, grid_spec=grid_spec,
        out_shape=[jax.ShapeDtypeStruct((r, c), F32)] * 4,
        compiler_params=_cparams("parallel", "parallel"),
    )(cc, w, g_mine, g_sibling, m, v)


def small_allreduce(pack):
    def body(p_ref, o_ref, gather, send_sems, recv_sems):
        x, y, cc = _mesh_pos()
        me = 4 * x + 2 * y + cc
        gather[me] = p_ref[...]
        flips = [(fx, fy, fc) for fx in (0, 1) for fy in (0, 1) for fc in (0, 1)][1:]
        copies = []
        for k, (fx, fy, fc) in enumerate(flips):
            copies.append(pltpu.make_async_remote_copy(
                src_ref=p_ref, dst_ref=gather.at[me],
                send_sem=send_sems.at[k], recv_sem=recv_sems.at[k],
                device_id=(x ^ fx, y ^ fy, cc ^ fc), device_id_type=MESH_ID))
        for cp in copies:
            cp.start()
        for k, (fx, fy, fc) in enumerate(flips):
            src = 4 * (x ^ fx) + 2 * (y ^ fy) + (cc ^ fc)
            pltpu.make_async_remote_copy(
                src_ref=p_ref, dst_ref=gather.at[src],
                send_sem=send_sems.at[k], recv_sem=recv_sems.at[k],
                device_id=(x ^ fx, y ^ fy, cc ^ fc), device_id_type=MESH_ID).wait_recv()
        for cp in copies:
            cp.wait_send()
        total = gather[0]
        for d in range(1, N_DEV):
            total = total + gather[d]
        o_ref[...] = total

    return pl.pallas_call(
        body, name="small_allreduce",
        in_specs=[pl.BlockSpec(memory_space=pltpu.VMEM)],
        out_specs=pl.BlockSpec(memory_space=pltpu.VMEM),
        out_shape=jax.ShapeDtypeStruct(pack.shape, pack.dtype),
        scratch_shapes=[pltpu.VMEM((N_DEV,) + pack.shape, pack.dtype),
                        pltpu.SemaphoreType.DMA((7,)), pltpu.SemaphoreType.DMA((7,))],
    )(pack)


def small_update(gsum, wpack, mpack, vpack):
    hw = HGRN_WIDTH

    def body(g_ref, w_ref, m_ref, v_ref, go_ref, d_ref, nm_ref, nv_ref, loss_ref):
        g = g_ref[...]
        wv = w_ref[...]
        a0, a1 = wv[4:5, :hw], wv[4:5, hw:]
        mx = jnp.maximum(a0, a1)
        e0, e1 = jnp.exp(a0 - mx), jnp.exp(a1 - mx)
        lb = e0 / (e0 + e1)
        dl = g[4:5, :hw] * lb * (1.0 - lb)
        row = lax.broadcasted_iota(jnp.int32, g.shape, 0)
        lb_row = jnp.concatenate([dl, -dl], axis=1)
        grads = jnp.where(row == 4, lb_row, jnp.where(row < 4, g, 0.0))
        go_ref[...] = grads
        d, nm, nv = _adam_update(wv, grads, m_ref[...], v_ref[...])
        d_ref[...] = d
        nm_ref[...] = nm
        nv_ref[...] = nv
        loss_ref[...] = jnp.zeros((8, LANES), F32) + jnp.sum(g[5:6, :])

    vm = pl.BlockSpec(memory_space=pltpu.VMEM)
    return pl.pallas_call(
        body, name="small_update",
        in_specs=[vm] * 4, out_specs=[vm] * 5,
        out_shape=[jax.ShapeDtypeStruct(gsum.shape, F32)] * 4 + [jax.ShapeDtypeStruct((8, LANES), F32)],
    )(gsum, wpack, mpack, vpack)


def _pack_small(n1, n2, fn, hn, lbl):
    z = jnp.zeros((1, D_MODEL - HGRN_WIDTH), F32)
    rows = [n1.reshape(1, D_MODEL), n2.reshape(1, D_MODEL), fn.reshape(1, D_MODEL),
            jnp.concatenate([hn.reshape(1, HGRN_WIDTH), z], axis=1), lbl.reshape(1, 2 * HGRN_WIDTH),
            jnp.zeros((3, D_MODEL), F32)]
    return jnp.concatenate(rows, axis=0)


def _unpack_small(pack):
    return (pack[0:1], pack[4].reshape(2, HGRN_WIDTH), pack[3:4, :HGRN_WIDTH], pack[1:2], pack[2])


def kernel(x, norm1_w, w_in, lb_logits, hgrn_norm_w, w_out, norm2_w, w_gate_up, w_down, final_norm_w, loss_target, m_norm1_w, m_w_in, m_lb_logits, m_hgrn_norm_w, m_w_out, m_norm2_w, m_w_gate_up, m_w_down, m_final_norm_w, v_norm1_w, v_w_in, v_lb_logits, v_hgrn_norm_w, v_w_out, v_norm2_w, v_w_gate_up, v_w_down, v_final_norm_w):
    seq = x.shape[1]
    xs = x.reshape(seq, D_MODEL)
    target = loss_target.reshape(seq, D_MODEL)
    shards = {"w_in": w_in[0], "w_out": w_out[0], "w_gu": w_gate_up[0], "w_down": w_down[0]}

    cast = {k: cast_bf16(w, "cast_" + k) for k, w in shards.items()}
    w_in4 = allgather_halves(cast["w_in"], "gather_w_in").reshape(N_CHIPS, D_MODEL, -1)

    cos_t, sin_t = _rope_tables(seq)
    fw = final_norm_w.reshape(1, D_MODEL)

    qr, kr, va, hg, u, g_out, g_down = in_proj(
        xs, norm1_w, w_in4, cos_t, sin_t, [cast["w_out"], cast["w_down"]])
    ya, lse = attn_fwd(qr, kr, va)
    yb, o_pre, st0, g_gu = hgrn_fwd(hg, lb_logits, hgrn_norm_w, [cast["w_gu"]])
    w_out_f = g_out.reshape(D_MODEL, D_MODEL)
    w_gu4 = g_gu.reshape(N_CHIPS, D_MODEL, -1)
    w_down_f = g_down.reshape(FFN_HIDDEN, D_MODEL)
    mixed, h1, u2, g, up, act, dh2, acc_fin = ffn_fwd(
        ya, yb, xs, w_out_f, norm2_w, w_gu4, w_down_f, fw, target)

    cw_in, cw_gu = w_in4.shape[2], w_gu4.shape[2]
    dgu, dh1, dya, dyb, delta, acc_n2 = ffn_bwd(dh2, w_down_f, g, up, w_gu4, h1, norm2_w, w_out_f, ya)
    early = [
        weight_grad(mixed, dh1, D_MODEL, "wgrad_out").reshape(N_CHIPS, 2, D_MODEL // 8, D_MODEL),
        weight_grad(u2, dgu, cw_gu, "wgrad_gu", group=2).reshape(N_CHIPS, 2, D_MODEL // 2, cw_gu),
        weight_grad(act, dh2, D_MODEL, "wgrad_down").reshape(N_CHIPS, 2, FFN_HIDDEN // 8, D_MODEL),
    ]
    early_names = ["out", "gu", "down"]
    dq, dk, dv, *early_recv = attn_bwd(qr, kr, va, dya, lse, delta, early)
    early_sums = [add_own_half(gr, rc, "add_half_" + nm) for gr, rc, nm in zip(early, early_recv, early_names)]
    dhg, acc_hg, *early_parts = hgrn_bwd(hg, lb_logits, hgrn_norm_w, o_pre, st0, dyb, early_sums)
    dproj, dx, acc_n1 = in_bwd(dq, dk, dv, dhg, cos_t, sin_t, w_in4, xs, norm1_w, dh1)
    late = [weight_grad(u, dproj, cw_in, "wgrad_in", group=2).reshape(N_CHIPS, 2, D_MODEL // 2, cw_in)]
    late_recv = exchange_with_sibling(late, "grad_exchange_sibling_late")
    late_sums = [add_own_half(late[0], late_recv[0], "add_half_in")]
    late_parts = exchange_between_chips(late_sums)
    names = ["in"] + early_names
    sums, parts = late_sums + early_sums, list(late_parts) + list(early_parts)
    halves = [sum_chips(s, p, "sum_chips_" + nm) for s, p, nm in zip(sums, parts, names)]
    others = share_with_sibling(halves)
    big = {}
    for nm, key, mine, other, m_, v_ in zip(names, ["w_in", "w_out", "w_gu", "w_down"], halves, others,
                                            [m_w_in, m_w_out, m_w_gate_up, m_w_down],
                                            [v_w_in, v_w_out, v_w_gate_up, v_w_down]):
        big[key] = tuple(t[None] for t in adamw(shards[key], mine, other, m_[0], v_[0], "adamw_" + nm))

    z512 = jnp.zeros((1, D_MODEL - HGRN_WIDTH), F32)
    gpack = jnp.concatenate([
        acc_n1[0:1], acc_n2[0:1], acc_fin[0:1],
        jnp.concatenate([acc_hg[0:1], z512], axis=1), jnp.concatenate([acc_hg[1:2], z512], axis=1),
        acc_fin[1:2], jnp.zeros((2, D_MODEL), F32)], axis=0)
    gsum = small_allreduce(gpack)
    wpack = _pack_small(norm1_w, norm2_w, final_norm_w, hgrn_norm_w, lb_logits)
    mpack = _pack_small(m_norm1_w, m_norm2_w, m_final_norm_w, m_hgrn_norm_w, m_lb_logits)
    vpack = _pack_small(v_norm1_w, v_norm2_w, v_final_norm_w, v_hgrn_norm_w, v_lb_logits)
    gs, ds, nms, nvs, loss8 = small_update(gsum, wpack, mpack, vpack)
    loss = loss8[0, 0]

    def assemble(small_pack, idx):
        n1, lbl, hn, n2, fn = _unpack_small(small_pack)
        return (n1, big["w_in"][idx], lbl, hn, big["w_out"][idx], n2, big["w_gu"][idx], big["w_down"][idx], fn)

    return (loss, dx.reshape(x.shape), *assemble(gs, 0), *assemble(ds, 1), *assemble(nms, 2), *assemble(nvs, 3))
```

```python
import functools

import jax
import jax.numpy as jnp
from jax import lax
from jax.experimental import pallas as pl
from jax.experimental.pallas import tpu as pltpu

F32 = jnp.float32
BF16 = jnp.bfloat16

D_MODEL = 1024
ATTN_WIDTH = 512
HEAD_DIM = 64
DILATED_PAIRS = ((128, 1), (512, 4), (2048, 16))
ATTN_BLOCK = 128
ROPE_THETA = 10000.0
HGRN_WIDTH = 512
HGRN_CHUNK = 16
HGRN_HEADS = 4
IN_PROJ_WIDTH = 3584
FFN_HIDDEN = 2816
NORM_EPS = 1e-6
ATTN_SCALE = HEAD_DIM ** -0.5
N_CHIPS = 4
N_DEV = 8

ADAM_LR = 0.001
ADAM_B1 = 0.9
ADAM_B2 = 0.999
ADAM_EPS = 1e-08
ADAM_WD = 0.01
ADAM_STEP = 10

LANES = 128
HGRN_ROWS = 128
ROW_TILE = 256
ATTN_STEP_ROWS = 2048
ATTN_FWD_UNROLL = 8
ATTN_BWD_UNROLL = 8
VMEM_LIMIT = 56 * 1024 * 1024
NEG_BIG = -1e30
MESH_ID = pl.DeviceIdType.MESH


def _cparams(*sem):
    return pltpu.CompilerParams(dimension_semantics=tuple(sem), vmem_limit_bytes=VMEM_LIMIT)


def _dot(a, b):
    return jnp.dot(a, b, preferred_element_type=F32)


def _dot_nt(a, b):
    return lax.dot_general(a, b, (((1,), (1,)), ((), ())), preferred_element_type=F32)


def _dot_tn(a, b):
    return lax.dot_general(a, b, (((0,), (0,)), ((), ())), preferred_element_type=F32)


def _sigmoid(x):
    return 1.0 / (1.0 + jnp.exp(-x))


def _full(shape):
    n = len(shape)
    return pl.BlockSpec(shape, lambda *_: (0,) * n)


def _weight(shape):
    n = len(shape)
    return pl.BlockSpec(shape, lambda *_: (0,) * n, pipeline_mode=pl.Buffered(1))


def _rows(tm, width):
    return pl.BlockSpec((tm, width), lambda i: (i, 0))


def _swap32(x):
    lane = lax.broadcasted_iota(jnp.int32, x.shape, 1)
    first = (lane % HEAD_DIM) < (HEAD_DIM // 2)
    return jnp.where(first, pltpu.roll(x, LANES - 32, axis=1), pltpu.roll(x, 32, axis=1))


def _rotary_fwd(x, cos, sin_signed):
    parts = []
    for j in range(x.shape[1] // LANES):
        xc = x[:, j * LANES:(j + 1) * LANES]
        parts.append(xc * cos + _swap32(xc) * sin_signed)
    return jnp.concatenate(parts, axis=1)


def _rotary_bwd(dy, cos, sin_signed):
    parts = []
    for j in range(dy.shape[1] // LANES):
        dc = dy[:, j * LANES:(j + 1) * LANES]
        parts.append(dc * cos + _swap32(dc * sin_signed))
    return jnp.concatenate(parts, axis=1)


def _rope_tables(seq):
    half = HEAD_DIM // 2
    inv_freq = ROPE_THETA ** (-jnp.arange(half, dtype=F32) / half)
    ang = jnp.arange(seq, dtype=F32)[:, None] * inv_freq[None, :]
    cos, sin = jnp.cos(ang), jnp.sin(ang)
    cos_t = jnp.tile(cos, (1, LANES // half))
    sin_t = jnp.tile(jnp.concatenate([-sin, sin], axis=1), (1, LANES // HEAD_DIM))
    return cos_t, sin_t


def cast_bf16(w, name):
    r, c = w.shape
    half = r // 2

    def body(w_ref, o_ref):
        o_ref[...] = w_ref[...].astype(BF16)

    return pl.pallas_call(
        body, name=name, grid=(2,),
        in_specs=[pl.BlockSpec((half, c), lambda i: (i, 0))],
        out_specs=pl.BlockSpec((None, half, c), lambda i: (i, 0, 0)),
        out_shape=jax.ShapeDtypeStruct((2, half, c), BF16),
        compiler_params=_cparams("parallel"),
    )(w)


def _mesh_pos():
    return lax.axis_index("x"), lax.axis_index("y"), lax.axis_index("c")


GATHER_COPIES = 7


def _gather_phases(x_refs, out_refs, send_sems, recv_sems, local_sems):
    n = len(x_refs)
    x, y, cc = _mesh_pos()
    me, sibling = (x, y, cc), (x, y, 1 - cc)
    chips = [(1 - x, y), (x, 1 - y), (1 - x, 1 - y)]

    def rows(i, px, py, pc):
        return out_refs[i].at[4 * px + 2 * py + pc]

    def copy(i, k, block, to, src=None):
        return pltpu.make_async_remote_copy(
            src_ref=rows(i, *block) if src is None else src, dst_ref=rows(i, *block),
            send_sem=send_sems.at[GATHER_COPIES * i + k], recv_sem=recv_sems.at[GATHER_COPIES * i + k],
            device_id=to, device_id_type=MESH_ID)

    def local(i):
        return pltpu.make_async_copy(x_refs[i].at[cc], rows(i, *me), local_sems.at[i])

    def first(i):
        mine = x_refs[i].at[cc]
        return [copy(i, 0, me, sibling, src=mine)] + [
            copy(i, 1 + j, me, (*chip, cc), src=mine) for j, chip in enumerate(chips)]

    def passed(i):
        return [copy(i, 4 + j, (*chip, cc), sibling) for j, chip in enumerate(chips)]

    def start():
        for i in range(n):
            local(i).start()
            for cp in first(i):
                cp.start()

    def forward():
        for i in range(n):
            onward = passed(i)
            for j, chip in enumerate(chips):
                copy(i, 1 + j, (*chip, cc), me).wait_recv()
                onward[j].start()

    def finish():
        for i in range(n):
            copy(i, 0, sibling, me).wait_recv()
            for j, chip in enumerate(chips):
                copy(i, 4 + j, (*chip, 1 - cc), me).wait_recv()
            for cp in first(i) + passed(i):
                cp.wait_send()
            local(i).wait()

    return start, forward, finish


def _gather_scratch(n):
    return [pltpu.SemaphoreType.DMA((GATHER_COPIES * n,)), pltpu.SemaphoreType.DMA((GATHER_COPIES * n,)),
            pltpu.SemaphoreType.DMA((n,))]


def _gathered_shape(halves):
    return jax.ShapeDtypeStruct((N_DEV,) + halves.shape[1:], halves.dtype)


def allgather_halves(halves, name):
    def body(x_ref, out_ref, send_sems, recv_sems, local_sems):
        start, forward, finish = _gather_phases([x_ref], [out_ref], send_sems, recv_sems, local_sems)
        start()
        forward()
        finish()

    return pl.pallas_call(
        body, name=name,
        in_specs=[pl.BlockSpec(memory_space=pl.ANY)],
        out_specs=pl.BlockSpec(memory_space=pl.ANY),
        out_shape=_gathered_shape(halves),
        scratch_shapes=_gather_scratch(1),
    )(halves)


def _rms(x):
    return lax.rsqrt(jnp.mean(x * x, axis=-1, keepdims=True) + NORM_EPS)


def in_proj(x, norm1_w, w_in4, cos_t, sin_t, weight_halves=()):
    seq = x.shape[0]
    tm = ROW_TILE
    cw = w_in4.shape[2]
    n_w = len(weight_halves)
    steps = seq // tm

    def body(*refs):
        x_ref, nw_ref, w_ref, cos_ref, sin_ref = refs[:5]
        q_ref, k_ref, v_ref, hg_ref, u_ref = refs[5 + n_w:10 + n_w]
        step = pl.program_id(0)
        if n_w:
            start, forward, finish = _gather_phases(
                refs[5:5 + n_w], refs[10 + n_w:10 + 2 * n_w], *refs[10 + 2 * n_w:])
            pl.when(step == 0)(start)
            pl.when(step == (3 * steps) // 4)(forward)
        xv = x_ref[...]
        u = ((xv * _rms(xv)) * nw_ref[...]).astype(BF16)
        u_ref[...] = u
        proj = jnp.concatenate([_dot(u, w_ref[j]) for j in range(N_CHIPS)], axis=1)
        cos, sin = cos_ref[...], sin_ref[...]
        a = ATTN_WIDTH
        q_ref[...] = _rotary_fwd(proj[:, :a], cos, sin)
        k_ref[...] = _rotary_fwd(proj[:, a:2 * a], cos, sin)
        v_ref[...] = proj[:, 2 * a:3 * a]
        hg_ref[...] = proj[:, 3 * a:]
        if n_w:
            pl.when(step == steps - 1)(finish)

    anywhere = pl.BlockSpec(memory_space=pl.ANY)
    return pl.pallas_call(
        body, name="in_proj", grid=(steps,),
        in_specs=[_rows(tm, D_MODEL), _full((1, D_MODEL)), _weight((N_CHIPS, D_MODEL, cw)),
                  _rows(tm, LANES), _rows(tm, LANES)] + [anywhere] * n_w,
        out_specs=[_rows(tm, ATTN_WIDTH)] * 3 + [_rows(tm, 4 * HGRN_WIDTH), _rows(tm, D_MODEL)]
        + [anywhere] * n_w,
        out_shape=[jax.ShapeDtypeStruct((seq, ATTN_WIDTH), F32)] * 3
        + [jax.ShapeDtypeStruct((seq, 4 * HGRN_WIDTH), F32), jax.ShapeDtypeStruct((seq, D_MODEL), BF16)]
        + [_gathered_shape(h) for h in weight_halves],
        scratch_shapes=_gather_scratch(n_w) if n_w else [],
        compiler_params=_cparams("arbitrary"),
    )(x, norm1_w, w_in4, cos_t, sin_t, *weight_halves)


def _head_masks():
    lane = lax.broadcasted_iota(jnp.int32, (1, LANES), 1)
    return [(lane // HEAD_DIM) == h for h in range(LANES // HEAD_DIM)]


def _window_valid(no_prev):
    qi = lax.broadcasted_iota(jnp.int32, (ATTN_BLOCK, 2 * ATTN_BLOCK), 0)
    kj = lax.broadcasted_iota(jnp.int32, (ATTN_BLOCK, 2 * ATTN_BLOCK), 1)
    valid = (kj >= qi) & (kj <= qi + ATTN_BLOCK)
    return valid & (jnp.logical_not(no_prev) | (kj >= ATTN_BLOCK))


def _strided_rows(start, dilation):
    if dilation == 1:
        return pl.ds(start, ATTN_BLOCK)
    return pl.ds(start, ATTN_BLOCK, stride=dilation)


def _block_before(edge_ref, cur_ref, t, r, span, dilation, per_step):
    edge = edge_ref[_strided_rows(ATTN_STEP_ROWS - span + r, dilation), :]
    if per_step == 1:
        return edge
    inside = cur_ref[_strided_rows(r + span * jnp.maximum(t - 1, 0), dilation), :]
    return jnp.where(t == 0, edge, inside)


def _attn_specs():
    cur = pl.BlockSpec((ATTN_STEP_ROWS, LANES), lambda hp, j: (j, hp))
    prev = pl.BlockSpec((ATTN_STEP_ROWS, LANES), lambda hp, j: (jnp.maximum(j - 1, 0), hp))
    return cur, prev


def _for_each_block(dilation, unroll, block):
    span = ATTN_BLOCK * dilation
    per_step = ATTN_STEP_ROWS // span

    def trip(it, carry):
        block(it // dilation, it % dilation, span, per_step)
        return carry

    lax.fori_loop(0, per_step * dilation, trip, 0, unroll=unroll)


def attn_fwd(q, k, v):
    seq = q.shape[0]
    cur, prev = _attn_specs()

    def body(q_ref, kc_ref, vc_ref, kp_ref, vp_ref, y_ref, lse_ref):
        first_step = pl.program_id(1) == 0
        masks = _head_masks()
        for index, (_, dilation) in enumerate(DILATED_PAIRS):
            def block(t, r, span, per_step, dilation=dilation, merge=index > 0):
                rows = _strided_rows(r + span * t, dilation)
                q2 = q_ref[rows, :].astype(BF16)
                kp = _block_before(kp_ref, kc_ref, t, r, span, dilation, per_step)
                vp = _block_before(vp_ref, vc_ref, t, r, span, dilation, per_step)
                k2 = jnp.concatenate([kp, kc_ref[rows, :]], axis=0).astype(BF16)
                v2 = jnp.concatenate([vp, vc_ref[rows, :]], axis=0).astype(BF16)
                valid = _window_valid(first_step & (t == 0))
                o_acc = jnp.zeros((ATTN_BLOCK, LANES), F32)
                l_acc = jnp.zeros((ATTN_BLOCK, LANES), F32)
                for mh in masks:
                    qm = jnp.where(mh, q2, jnp.zeros_like(q2))
                    s = jnp.where(valid, _dot_nt(qm, k2) * ATTN_SCALE, NEG_BIG)
                    m = jnp.max(s, axis=-1, keepdims=True)
                    p = jnp.exp(s - m)
                    l = jnp.sum(p, axis=-1, keepdims=True)
                    o = _dot(p.astype(BF16), v2) / l
                    o_acc = jnp.where(mh, o, o_acc)
                    l_acc = jnp.where(mh, m + jnp.log(l), l_acc)
                if merge:
                    y_old, l_old = y_ref[rows, :], lse_ref[rows, :]
                    mx = jnp.maximum(l_old, l_acc)
                    e_old, e_new = jnp.exp(l_old - mx), jnp.exp(l_acc - mx)
                    den = e_old + e_new
                    o_acc = (y_old * e_old + o_acc * e_new) / den
                    l_acc = mx + jnp.log(den)
                y_ref[rows, :] = o_acc
                lse_ref[rows, :] = l_acc

            _for_each_block(dilation, ATTN_FWD_UNROLL, block)

    return pl.pallas_call(
        body, name="attn_fwd", grid=(ATTN_WIDTH // LANES, seq // ATTN_STEP_ROWS),
        in_specs=[cur, cur, cur, prev, prev],
        out_specs=[cur, cur],
        out_shape=[jax.ShapeDtypeStruct((seq, ATTN_WIDTH), F32)] * 2,
        compiler_params=_cparams("parallel", "parallel"),
    )(q, k, v, k, v)


def _chunk_cumsum(x, reverse=False):
    rc = lax.broadcasted_iota(jnp.int32, x.shape, 0) % HGRN_CHUNK
    sh = 1
    while sh < HGRN_CHUNK:
        if reverse:
            x = x + jnp.where(rc + sh < HGRN_CHUNK, pltpu.roll(x, x.shape[0] - sh, axis=0), 0.0)
        else:
            x = x + jnp.where(rc >= sh, pltpu.roll(x, sh, axis=0), 0.0)
        sh *= 2
    return x


def _chunk_row(x, row):
    return _chunk_rows([x[n * HGRN_CHUNK + row:n * HGRN_CHUNK + row + 1, :]
                        for n in range(x.shape[0] // HGRN_CHUNK)])


def _chunk_rows(rows):
    return jnp.concatenate([jnp.broadcast_to(r, (HGRN_CHUNK, r.shape[1])) for r in rows], axis=0)


def _hgrn_prep(hg, lbl):
    w = HGRN_WIDTH
    a0, a1 = lbl[0:1, :], lbl[1:2, :]
    mx = jnp.maximum(a0, a1)
    e0, e1 = jnp.exp(a0 - mx), jnp.exp(a1 - mx)
    lb = e0 / (e0 + e1)
    qb, fb, gb = hg[:, :w], hg[:, w:2 * w], hg[:, 3 * w:]
    sg = _sigmoid(fb)
    f = lb + (1.0 - lb) * sg
    b = _chunk_cumsum(jnp.log(f))
    bmid, btot = _chunk_row(b, HGRN_CHUNK // 2 - 1), _chunk_row(b, HGRN_CHUNK - 1)
    sq = _sigmoid(qb)
    p = dict(lb=lb, sg=sg, f=f, kk=1.0 - f, sq=sq, qf=qb * sq, gb=gb,
             e_iq=jnp.exp(b - bmid), e_ik=jnp.exp(bmid - b), e_b=jnp.exp(b),
             e_bb=jnp.exp(btot - b), e_tot=jnp.exp(btot))
    p["qi"] = p["qf"] * p["e_iq"]
    p["ki"] = p["kk"] * p["e_ik"]
    p["qs"] = p["qf"] * p["e_b"]
    p["kb"] = p["kk"] * p["e_bb"]
    return p


def _chunk_masks():
    t = lax.broadcasted_iota(jnp.int32, (HGRN_ROWS, HGRN_ROWS), 0)
    s = lax.broadcasted_iota(jnp.int32, (HGRN_ROWS, HGRN_ROWS), 1)
    tril = ((t // HGRN_CHUNK) == (s // HGRN_CHUNK)) & (s <= t)
    n_chunks = HGRN_ROWS // HGRN_CHUNK
    tt = lax.broadcasted_iota(jnp.int32, (HGRN_ROWS, n_chunks * LANES), 0)
    cc = lax.broadcasted_iota(jnp.int32, (HGRN_ROWS, n_chunks * LANES), 1)
    block = (tt // HGRN_CHUNK) == (cc // LANES)
    return tril, block


def _spread(x, block):
    n_chunks = HGRN_ROWS // HGRN_CHUNK
    return jnp.where(block, jnp.tile(x, (1, n_chunks)), jnp.zeros((), x.dtype))


def _fold(x_full, block):
    n_chunks = HGRN_ROWS // HGRN_CHUNK
    z = jnp.where(block, x_full, 0.0)
    acc = z[:, :LANES]
    for n in range(1, n_chunks):
        acc = acc + z[:, n * LANES:(n + 1) * LANES]
    return acc


def hgrn_fwd(hg, lb_logits, hnw, weight_halves=()):
    seq = hg.shape[0]
    nblk = seq // HGRN_ROWS
    n_chunks = HGRN_ROWS // HGRN_CHUNK
    n_w = len(weight_halves)

    def body(*refs):
        hg_ref, lbl_ref, hnw_ref = refs[:3]
        w_refs = refs[3:3 + n_w]
        yb_ref, o_ref, st0_ref = refs[3 + n_w:6 + n_w]
        g_refs = refs[6 + n_w:6 + 2 * n_w]
        st_scr = refs[6 + 2 * n_w]
        step = pl.program_id(0)
        if n_w:
            start, forward, finish = _gather_phases(w_refs, g_refs, *refs[7 + 2 * n_w:])
            pl.when(step == 0)(start)
            pl.when(step == (3 * nblk) // 4)(forward)

        @pl.when(step == 0)
        def _():
            st_scr[...] = jnp.zeros_like(st_scr)

        hg_v = hg_ref[...]
        p = _hgrn_prep(hg_v, lbl_ref[...])
        tril, block = _chunk_masks()
        vv = hg_v[:, 2 * HGRN_WIDTH:3 * HGRN_WIDTH].astype(BF16)
        outs = []
        for h in range(HGRN_HEADS):
            sl = slice(h * LANES, (h + 1) * LANES)
            v_h = vv[:, sl]
            a = jnp.where(tril, _dot_nt(p["qi"][:, sl].astype(BF16), p["ki"][:, sl].astype(BF16)), 0.0)
            o = _dot(a.astype(BF16), v_h)
            upd = _dot_tn(v_h, _spread(p["kb"][:, sl].astype(BF16), block))
            st = st_scr[h]
            st0_ref[h] = st
            parts = []
            for n in range(n_chunks):
                parts.append(st.astype(BF16))
                decay = p["e_tot"][n * HGRN_CHUNK:n * HGRN_CHUNK + 1, sl]
                st = st * decay + upd[:, n * LANES:(n + 1) * LANES]
            st_scr[h] = st
            o = o + _dot_nt(_spread(p["qs"][:, sl].astype(BF16), block), jnp.concatenate(parts, axis=1))
            outs.append(o)
        o_all = jnp.concatenate(outs, axis=1)
        o_ref[...] = o_all
        normed = jnp.concatenate(
            [outs[h] * _rms(outs[h]) for h in range(HGRN_HEADS)], axis=1)
        gb = p["gb"]
        yb_ref[...] = (normed * hnw_ref[...]) * (gb * _sigmoid(gb))
        if n_w:
            pl.when(step == nblk - 1)(finish)

    anywhere = pl.BlockSpec(memory_space=pl.ANY)
    return pl.pallas_call(
        body, name="hgrn_fwd", grid=(nblk,),
        in_specs=[_rows(HGRN_ROWS, 4 * HGRN_WIDTH), _full((2, HGRN_WIDTH)), _full((1, HGRN_WIDTH))]
        + [anywhere] * n_w,
        out_specs=[_rows(HGRN_ROWS, HGRN_WIDTH), _rows(HGRN_ROWS, HGRN_WIDTH),
                   pl.BlockSpec((None, HGRN_HEADS, LANES, LANES), lambda i: (i, 0, 0, 0))] + [anywhere] * n_w,
        out_shape=[jax.ShapeDtypeStruct((seq, HGRN_WIDTH), F32)] * 2
        + [jax.ShapeDtypeStruct((nblk, HGRN_HEADS, LANES, LANES), F32)]
        + [_gathered_shape(h) for h in weight_halves],
        scratch_shapes=[pltpu.VMEM((HGRN_HEADS, LANES, LANES), F32)] + (_gather_scratch(n_w) if n_w else []),
        compiler_params=_cparams("arbitrary"),
    )(hg, lb_logits, hnw, *weight_halves)


def ffn_fwd(ya, yb, x, w_out, norm2_w, w_gu4, w_down, final_w, target):
    seq = x.shape[0]
    tm = ROW_TILE
    cw = w_gu4.shape[2]
    inv_d = 1.0 / D_MODEL

    def body(ya_ref, yb_ref, x_ref, wo_ref, nw_ref, wgu_ref, wd_ref, fw_ref, t_ref,
             mixed_ref, h1_ref, u2_ref, g_ref, up_ref, act_ref, dh2_ref, acc_ref):
        @pl.when(pl.program_id(0) == 0)
        def _():
            acc_ref[...] = jnp.zeros_like(acc_ref)

        mixed = jnp.concatenate([ya_ref[...], yb_ref[...]], axis=1).astype(BF16)
        mixed_ref[...] = mixed
        h1 = x_ref[...] + _dot(mixed, wo_ref[...])
        h1_ref[...] = h1
        u2 = ((h1 * _rms(h1)) * nw_ref[...]).astype(BF16)
        u2_ref[...] = u2
        g = jnp.concatenate([_dot(u2, wgu_ref[0]), _dot(u2, wgu_ref[1])], axis=1)
        up = jnp.concatenate([_dot(u2, wgu_ref[2]), _dot(u2, wgu_ref[3])], axis=1)
        g_ref[...] = g.astype(BF16)
        up_ref[...] = up.astype(BF16)
        act = ((g * _sigmoid(g)) * up).astype(BF16)
        act_ref[...] = act
        h2 = h1 + _dot(act, wd_ref[...])
        rf = _rms(h2)
        n = h2 * rf
        fw = fw_ref[...]
        err = n * fw - t_ref[...]
        dy = err * inv_d
        acc_ref[0:1, :] += jnp.sum(dy * n, axis=0, keepdims=True)
        acc_ref[1:2, :] += (0.5 * inv_d) * jnp.sum(err * err, axis=0, keepdims=True)
        dn = dy * fw
        dh2_ref[...] = rf * (dn - n * jnp.mean(dn * n, axis=-1, keepdims=True))

    half = _rows(tm, ATTN_WIDTH)
    wide = _rows(tm, D_MODEL)
    ffn = _rows(tm, FFN_HIDDEN)
    return pl.pallas_call(
        body, name="ffn_fwd", grid=(seq // tm,),
        in_specs=[half, half, wide, _weight((D_MODEL, D_MODEL)), _full((1, D_MODEL)),
                  _weight((N_CHIPS, D_MODEL, cw)), _weight((FFN_HIDDEN, D_MODEL)), _full((1, D_MODEL)), wide],
        out_specs=[wide, wide, wide, ffn, ffn, ffn, wide, _full((8, D_MODEL))],
        out_shape=[jax.ShapeDtypeStruct((seq, D_MODEL), BF16), jax.ShapeDtypeStruct((seq, D_MODEL), F32),
                   jax.ShapeDtypeStruct((seq, D_MODEL), BF16)]
        + [jax.ShapeDtypeStruct((seq, FFN_HIDDEN), BF16)] * 3
        + [jax.ShapeDtypeStruct((seq, D_MODEL), F32), jax.ShapeDtypeStruct((8, D_MODEL), F32)],
        compiler_params=_cparams("arbitrary"),
    )(ya, yb, x, w_out, norm2_w, w_gu4, w_down, final_w, target)


def _head_sum_matrix():
    i = jnp.arange(ATTN_WIDTH)
    return ((i[:, None] // HEAD_DIM) == (i[None, :] // HEAD_DIM)).astype(BF16)


def ffn_bwd(dh2, w_down, g, up, w_gu4, h1, norm2_w, w_out, ya):
    seq = h1.shape[0]
    tm = ROW_TILE
    cw = w_gu4.shape[2]
    hsum = _head_sum_matrix()

    def body(dh2_ref, wd_ref, g_ref, up_ref, w_ref, h1_ref, nw_ref, wo_ref, ya_ref, hs_ref,
             dgu_ref, dh1_ref, dya_ref, dyb_ref, delta_ref, acc_ref):
        @pl.when(pl.program_id(0) == 0)
        def _():
            acc_ref[...] = jnp.zeros_like(acc_ref)

        dh2_b = dh2_ref[...].astype(BF16)
        du2 = jnp.zeros((tm, D_MODEL), F32)
        for j in range(N_CHIPS // 2):
            dact = _dot_nt(dh2_b, wd_ref[j * cw:(j + 1) * cw, :])
            gv = g_ref[:, j * cw:(j + 1) * cw].astype(F32)
            sg = _sigmoid(gv)
            dg = (dact * up_ref[:, j * cw:(j + 1) * cw].astype(F32) * (sg * (1.0 + gv * (1.0 - sg)))).astype(BF16)
            dup = (dact * (gv * sg)).astype(BF16)
            dgu_ref[:, j * cw:(j + 1) * cw] = dg
            dgu_ref[:, FFN_HIDDEN + j * cw:FFN_HIDDEN + (j + 1) * cw] = dup
            du2 = du2 + _dot_nt(dg, w_ref[j]) + _dot_nt(dup, w_ref[N_CHIPS // 2 + j])
        h1 = h1_ref[...]
        r2 = _rms(h1)
        nh = h1 * r2
        acc_ref[0:1, :] += jnp.sum(du2 * nh, axis=0, keepdims=True)
        dn = du2 * nw_ref[...]
        dh1 = dh2_ref[...] + r2 * (dn - nh * jnp.mean(dn * nh, axis=-1, keepdims=True))
        dh1_ref[...] = dh1
        dmixed = _dot_nt(dh1.astype(BF16), wo_ref[...])
        dya = dmixed[:, :ATTN_WIDTH]
        dya_ref[...] = dya
        dyb_ref[...] = dmixed[:, ATTN_WIDTH:]
        prod = dya * ya_ref[...]
        hi = prod.astype(BF16)
        lo = (prod - hi.astype(F32)).astype(BF16)
        delta_ref[...] = _dot(hi, hs_ref[...]) + _dot(lo, hs_ref[...])

    wide = _rows(tm, D_MODEL)
    half = _rows(tm, ATTN_WIDTH)
    ffn = _rows(tm, FFN_HIDDEN)
    return pl.pallas_call(
        body, name="ffn_bwd", grid=(seq // tm,),
        in_specs=[wide, _weight((FFN_HIDDEN, D_MODEL)), ffn, ffn, _weight((N_CHIPS, D_MODEL, cw)), wide,
                  _full((1, D_MODEL)), _weight((D_MODEL, D_MODEL)), half, _full((ATTN_WIDTH, ATTN_WIDTH))],
        out_specs=[_rows(tm, 2 * FFN_HIDDEN), wide, half, half, half, _full((8, D_MODEL))],
        out_shape=[jax.ShapeDtypeStruct((seq, 2 * FFN_HIDDEN), BF16), jax.ShapeDtypeStruct((seq, D_MODEL), F32)]
        + [jax.ShapeDtypeStruct((seq, ATTN_WIDTH), F32)] * 3 + [jax.ShapeDtypeStruct((8, D_MODEL), F32)],
        compiler_params=_cparams("arbitrary"),
    )(dh2, w_down, g, up, w_gu4, h1, norm2_w, w_out, ya, hsum)


def attn_bwd(q, k, v, dy, lse, delta, sibling_grads=()):
    seq = q.shape[0]
    cur, prev = _attn_specs()
    whole = pl.BlockSpec((seq, LANES), lambda hp, j: (0, hp))
    n_g = len(sibling_grads)
    n_hp, n_steps = ATTN_WIDTH // LANES, seq // ATTN_STEP_ROWS

    def body(*refs):
        q_ref, dy_ref, lse_ref, dl_ref, kc_ref, vc_ref, kp_ref, vp_ref = refs[:8]
        dq_ref, dk_ref, dv_ref = refs[8 + n_g:11 + n_g]
        first_step = pl.program_id(1) == 0
        base = pl.program_id(1) * ATTN_STEP_ROWS
        masks = _head_masks()
        if n_g:
            start, finish = _sibling_exchange_phases(
                refs[8:8 + n_g], refs[11 + n_g:11 + 2 * n_g], *refs[11 + 2 * n_g:])
            pl.when((pl.program_id(0) == 0) & first_step)(start)

        def block(t, r, span, per_step, dilation, add):
            rows = _strided_rows(r + span * t, dilation)
            at_edge = t == 0
            q2, dy2 = q_ref[rows, :].astype(BF16), dy_ref[rows, :].astype(BF16)
            lse2, dl2 = lse_ref[rows, :], dl_ref[rows, :]
            kp = _block_before(kp_ref, kc_ref, t, r, span, dilation, per_step)
            vp = _block_before(vp_ref, vc_ref, t, r, span, dilation, per_step)
            k2 = jnp.concatenate([kp, kc_ref[rows, :]], axis=0).astype(BF16)
            v2 = jnp.concatenate([vp, vc_ref[rows, :]], axis=0).astype(BF16)
            valid = _window_valid(first_step & at_edge)
            zero = jnp.zeros_like(q2)
            qms, dyms, ps, dss, kms = [], [], [], [], []
            for h, mh in enumerate(masks):
                c0 = h * HEAD_DIM
                qm, dym = jnp.where(mh, q2, zero), jnp.where(mh, dy2, zero)
                s = _dot_nt(qm, k2) * ATTN_SCALE
                p = jnp.where(valid, jnp.exp(s - lse2[:, c0:c0 + 1]), 0.0)
                dp = _dot_nt(dym, v2)
                dss.append((p * (dp - dl2[:, c0:c0 + 1]) * ATTN_SCALE).astype(BF16))
                ps.append(p.astype(BF16))
                qms.append(qm)
                dyms.append(dym)
                kms.append(jnp.where(mh, k2, jnp.zeros_like(k2)))
            dq = _dot(jnp.concatenate(dss, axis=1), jnp.concatenate(kms, axis=0))
            dv_full = _dot_tn(jnp.concatenate(ps, axis=0), jnp.concatenate(dyms, axis=0))
            dk_full = _dot_tn(jnp.concatenate(dss, axis=0), jnp.concatenate(qms, axis=0))
            here = _strided_rows(base + r + span * t, dilation)
            if add:
                dq_ref[rows, :] += dq
                dk_ref[here, :] += dk_full[ATTN_BLOCK:]
                dv_ref[here, :] += dv_full[ATTN_BLOCK:]
            else:
                dq_ref[rows, :] = dq
                dk_ref[here, :] = dk_full[ATTN_BLOCK:]
                dv_ref[here, :] = dv_full[ATTN_BLOCK:]
            back = _strided_rows(jnp.maximum(base + r + span * t - span, r), dilation)
            dk_ref[back, :] += dk_full[:ATTN_BLOCK]
            dv_ref[back, :] += dv_full[:ATTN_BLOCK]

        for index, (_, dilation) in enumerate(DILATED_PAIRS):
            _for_each_block(dilation, ATTN_BWD_UNROLL,
                            functools.partial(block, dilation=dilation, add=index > 0))
        if n_g:
            pl.when((pl.program_id(0) == n_hp - 1) & (pl.program_id(1) == n_steps - 1))(finish)

    anywhere = pl.BlockSpec(memory_space=pl.ANY)
    return pl.pallas_call(
        body, name="attn_bwd", grid=(n_hp, n_steps),
        in_specs=[cur] * 6 + [prev, prev] + [anywhere] * n_g,
        out_specs=[cur, whole, whole] + [anywhere] * n_g,
        out_shape=[jax.ShapeDtypeStruct((seq, ATTN_WIDTH), F32)] * 3 + _sibling_exchange_shapes(sibling_grads),
        scratch_shapes=_sibling_exchange_scratch(n_g) if n_g else [],
        compiler_params=_cparams("arbitrary", "arbitrary"),
    )(q, dy, lse, delta, k, v, k, v, *sibling_grads)


def hgrn_bwd(hg, lb_logits, hnw, o_pre, st0, dyb, chip_sums=()):
    seq = hg.shape[0]
    nblk = seq // HGRN_ROWS
    n_chunks = HGRN_ROWS // HGRN_CHUNK
    w = HGRN_WIDTH
    n_s = len(chip_sums)

    def body(*refs):
        hg_ref, lbl_ref, hnw_ref, o_ref, st0_ref, dyb_ref = refs[:6]
        dhg_ref, acc_ref = refs[6 + n_s:8 + n_s]
        dst_scr = refs[8 + 2 * n_s]
        step = pl.program_id(0)
        if n_s:
            start, finish = _chip_exchange_phases(refs[6:6 + n_s], refs[8 + n_s:8 + 2 * n_s], *refs[9 + 2 * n_s:])
            pl.when(step == 0)(start)

        @pl.when(step == 0)
        def _():
            dst_scr[...] = jnp.zeros_like(dst_scr)
            acc_ref[...] = jnp.zeros_like(acc_ref)

        hg_v = hg_ref[...]
        p = _hgrn_prep(hg_v, lbl_ref[...])
        tril, block = _chunk_masks()
        vv = hg_v[:, 2 * w:3 * w].astype(BF16)
        hnw_v = hnw_ref[...]
        gb = p["gb"]
        sgg = _sigmoid(gb)
        silu_g = gb * sgg
        dyb_v = dyb_ref[...]
        o_v = o_ref[...]

        d_on = dyb_v * hnw_v * silu_g
        on_parts, do_parts = [], []
        for h in range(HGRN_HEADS):
            sl = slice(h * LANES, (h + 1) * LANES)
            rs = _rms(o_v[:, sl])
            on = o_v[:, sl] * rs
            on_parts.append(on)
            do_parts.append(rs * (d_on[:, sl] - on * jnp.mean(d_on[:, sl] * on, axis=-1, keepdims=True)))
        on_all = jnp.concatenate(on_parts, axis=1)
        dgb = dyb_v * on_all * hnw_v * (sgg * (1.0 + gb * (1.0 - sgg)))
        acc_ref[0:1, :] += jnp.sum(dyb_v * on_all * silu_g, axis=0, keepdims=True)

        dqf_parts, dkk_parts, db_parts, dv_parts, dbt_parts, dkbkb_parts = [], [], [], [], [], []
        for h in range(HGRN_HEADS):
            sl = slice(h * LANES, (h + 1) * LANES)
            v_h = vv[:, sl]
            do_h = do_parts[h].astype(BF16)
            qi, ki, qs, kb = p["qi"][:, sl], p["ki"][:, sl], p["qs"][:, sl], p["kb"][:, sl]
            qi_b, ki_b = qi.astype(BF16), ki.astype(BF16)
            kb_cat = _spread(kb.astype(BF16), block)
            qs_cat = _spread(qs.astype(BF16), block)
            upd = _dot_tn(v_h, kb_cat)
            st = st0_ref[h]
            st_parts = []
            for n in range(n_chunks):
                st_parts.append(st)
                decay = p["e_tot"][n * HGRN_CHUNK:n * HGRN_CHUNK + 1, sl]
                st = st * decay + upd[:, n * LANES:(n + 1) * LANES]
            st_cat = jnp.concatenate([s_.astype(BF16) for s_ in st_parts], axis=1)
            wgt = _dot_tn(do_h, qs_cat)
            dst = dst_scr[h]
            dst_parts = [None] * n_chunks
            dbt_rows = [None] * n_chunks
            for n in reversed(range(n_chunks)):
                dst_parts[n] = dst.astype(BF16)
                decay = p["e_tot"][n * HGRN_CHUNK:n * HGRN_CHUNK + 1, sl]
                dbt_rows[n] = jnp.sum(dst * st_parts[n], axis=0, keepdims=True) * decay
                dst = dst * decay + wgt[:, n * LANES:(n + 1) * LANES]
            dst_scr[h] = dst
            dst_cat = jnp.concatenate(dst_parts, axis=1)
            dqs = _fold(_dot(do_h, st_cat), block)
            dkb = _fold(_dot(v_h, dst_cat), block)
            dv_state = _dot_nt(kb_cat, dst_cat)
            a = jnp.where(tril, _dot_nt(qi_b, ki_b), 0.0).astype(BF16)
            da = jnp.where(tril, _dot_nt(do_h, v_h), 0.0).astype(BF16)
            dv_parts.append(_dot_tn(a, do_h) + dv_state)
            dqi = _dot(da, ki_b)
            dki = _dot_tn(da, qi_b)
            dqf_parts.append(dqi * p["e_iq"][:, sl] + dqs * p["e_b"][:, sl])
            dkk_parts.append(dki * p["e_ik"][:, sl] + dkb * p["e_bb"][:, sl])
            dkbkb = dkb * kb
            db_parts.append(dqi * qi - dki * ki + dqs * qs - dkbkb)
            dkbkb_parts.append(dkbkb)
            dbt_parts.append(_chunk_rows(dbt_rows))

        cat = lambda parts: jnp.concatenate(parts, axis=1)
        dlogf = (_chunk_cumsum(cat(db_parts), reverse=True)
                 + _chunk_row(_chunk_cumsum(cat(dkbkb_parts)), HGRN_CHUNK - 1) + cat(dbt_parts))
        sq, qb = p["sq"], hg_v[:, :w]
        dqb = cat(dqf_parts) * (sq * (1.0 + qb * (1.0 - sq)))
        df = dlogf / p["f"] - cat(dkk_parts)
        sg, lb = p["sg"], p["lb"]
        dfb = df * (1.0 - lb) * sg * (1.0 - sg)
        acc_ref[1:2, :] += jnp.sum(df * (1.0 - sg), axis=0, keepdims=True)
        dhg_ref[...] = jnp.concatenate([dqb, dfb, cat(dv_parts), dgb], axis=1)
        if n_s:
            pl.when(step == nblk - 1)(finish)

    rev = lambda i: (nblk - 1 - i, 0)
    anywhere = pl.BlockSpec(memory_space=pl.ANY)
    return pl.pallas_call(
        body, name="hgrn_bwd", grid=(nblk,),
        in_specs=[pl.BlockSpec((HGRN_ROWS, 4 * w), rev), _full((2, w)), _full((1, w)),
                  pl.BlockSpec((HGRN_ROWS, w), rev),
                  pl.BlockSpec((None, HGRN_HEADS, LANES, LANES), lambda i: (nblk - 1 - i, 0, 0, 0)),
                  pl.BlockSpec((HGRN_ROWS, w), rev)] + [anywhere] * n_s,
        out_specs=[pl.BlockSpec((HGRN_ROWS, 4 * w), rev), _full((8, w))] + [anywhere] * n_s,
        out_shape=[jax.ShapeDtypeStruct((seq, 4 * w), F32), jax.ShapeDtypeStruct((8, w), F32)]
        + [jax.ShapeDtypeStruct(s.shape, s.dtype) for s in chip_sums],
        scratch_shapes=[pltpu.VMEM((HGRN_HEADS, LANES, LANES), F32)] + (_chip_exchange_scratch(n_s) if n_s else []),
        compiler_params=_cparams("arbitrary"),
    )(hg, lb_logits, hnw, o_pre, st0, dyb, *chip_sums)


def in_bwd(dq, dk, dv, dhg, cos_t, sin_t, w_in4, x, norm1_w, dh1):
    seq = x.shape[0]
    tm = ROW_TILE
    cw = w_in4.shape[2]

    def body(dq_ref, dk_ref, dv_ref, dhg_ref, cos_ref, sin_ref, w_ref,
             x_ref, nw_ref, dh1_ref, dproj_ref, dx_ref, acc_ref):
        @pl.when(pl.program_id(0) == 0)
        def _():
            acc_ref[...] = jnp.zeros_like(acc_ref)

        cos, sin = cos_ref[...], sin_ref[...]
        dqa = _rotary_bwd(dq_ref[...], cos, sin)
        dka = _rotary_bwd(dk_ref[...], cos, sin)
        dproj = jnp.concatenate([dqa, dka, dv_ref[...], dhg_ref[...]], axis=1).astype(BF16)
        dproj_ref[...] = dproj
        du = _dot_nt(dproj[:, :cw], w_ref[0])
        for j in range(1, N_CHIPS):
            du = du + _dot_nt(dproj[:, j * cw:(j + 1) * cw], w_ref[j])
        xv = x_ref[...]
        r1 = _rms(xv)
        nx = xv * r1
        acc_ref[0:1, :] += jnp.sum(du * nx, axis=0, keepdims=True)
        dn = du * nw_ref[...]
        dx_ref[...] = dh1_ref[...] + r1 * (dn - nx * jnp.mean(dn * nx, axis=-1, keepdims=True))

    half = _rows(tm, ATTN_WIDTH)
    wide = _rows(tm, D_MODEL)
    return pl.pallas_call(
        body, name="in_bwd", grid=(seq // tm,),
        in_specs=[half] * 3 + [_rows(tm, 4 * HGRN_WIDTH), _rows(tm, LANES), _rows(tm, LANES),
                               _weight((N_CHIPS, D_MODEL, cw)), wide, _full((1, D_MODEL)), wide],
        out_specs=[_rows(tm, IN_PROJ_WIDTH), wide, _full((8, D_MODEL))],
        out_shape=[jax.ShapeDtypeStruct((seq, IN_PROJ_WIDTH), BF16), jax.ShapeDtypeStruct((seq, D_MODEL), F32),
                   jax.ShapeDtypeStruct((8, D_MODEL), F32)],
        compiler_params=_cparams("arbitrary"),
    )(dq, dk, dv, dhg, cos_t, sin_t, w_in4, x, norm1_w, dh1)


def weight_grad(a, b, col_block, name, group=1, small_pack=None):
    seq, kdim = a.shape
    ndim = b.shape[1]
    nj = ndim // col_block
    tk = min(1024, seq)
    hosting = small_pack is not None
    n_j, n_t = nj // group, seq // tk

    def body(*refs):
        a_ref, b_ref = refs[:2]
        o_ref = refs[3] if hosting else refs[2]
        if hosting:
            start, finish = _pack_gather_phases(refs[2], refs[4], *refs[5:])
            pl.when((pl.program_id(0) == 0) & (pl.program_id(1) == 0))(start)

        @pl.when(pl.program_id(1) == 0)
        def _():
            o_ref[...] = jnp.zeros_like(o_ref)

        acc = _dot_tn(a_ref[...].astype(BF16), b_ref[...].astype(BF16))
        for i in range(group):
            o_ref[i] += acc[:, i * col_block:(i + 1) * col_block]
        if hosting:
            pl.when((pl.program_id(0) == n_j - 1) & (pl.program_id(1) == n_t - 1))(finish)

    anywhere = pl.BlockSpec(memory_space=pl.ANY)
    out = pl.pallas_call(
        body, name=name, grid=(n_j, n_t),
        in_specs=[pl.BlockSpec((tk, kdim), lambda j, t: (t, 0)),
                  pl.BlockSpec((tk, group * col_block), lambda j, t: (t, j))] + [anywhere] * hosting,
        out_specs=[pl.BlockSpec((group, kdim, col_block), lambda j, t: (j, 0, 0))] + [anywhere] * hosting,
        out_shape=[jax.ShapeDtypeStruct((nj, kdim, col_block), F32)]
        + ([jax.ShapeDtypeStruct((N_DEV,) + small_pack.shape, F32)] if hosting else []),
        scratch_shapes=[pltpu.SemaphoreType.DMA((N_DEV - 1,)), pltpu.SemaphoreType.DMA((N_DEV - 1,)),
                        pltpu.SemaphoreType.DMA] if hosting else [],
        compiler_params=_cparams("arbitrary", "arbitrary"),
    )(a, b, *([small_pack] if hosting else []))
    return out if hosting else out[0]


def _sibling_exchange_phases(g_refs, out_refs, send_sems, recv_sems):
    x, y, cc = _mesh_pos()

    def copies():
        return [pltpu.make_async_remote_copy(
            src_ref=g_refs[i].at[j, 1 - cc], dst_ref=out_refs[i].at[j],
            send_sem=send_sems.at[i * N_CHIPS + j], recv_sem=recv_sems.at[i * N_CHIPS + j],
            device_id=(x, y, 1 - cc), device_id_type=MESH_ID)
            for i in range(len(g_refs)) for j in range(N_CHIPS)]

    def start():
        for cp in copies():
            cp.start()

    def finish():
        for cp in copies():
            cp.wait_recv()
        for cp in copies():
            cp.wait_send()

    return start, finish


def _sibling_exchange_scratch(n):
    return [pltpu.SemaphoreType.DMA((n * N_CHIPS,)), pltpu.SemaphoreType.DMA((n * N_CHIPS,))]


def _sibling_exchange_shapes(grads):
    return [jax.ShapeDtypeStruct((N_CHIPS,) + g.shape[2:], g.dtype) for g in grads]


def exchange_with_sibling(grads, name):
    n = len(grads)

    def body(*refs):
        start, finish = _sibling_exchange_phases(refs[:n], refs[n:2 * n], refs[2 * n], refs[2 * n + 1])
        start()
        finish()

    return pl.pallas_call(
        body, name=name,
        in_specs=[pl.BlockSpec(memory_space=pl.ANY)] * n,
        out_specs=[pl.BlockSpec(memory_space=pl.ANY)] * n,
        out_shape=_sibling_exchange_shapes(grads),
        scratch_shapes=_sibling_exchange_scratch(n),
    )(*grads)


def add_own_half(grad, recv, name):
    _, _, r, c = grad.shape
    tr = r // 2 if r % 32 == 0 else r

    def body(cc_ref, g_ref, r_ref, o_ref):
        o_ref[...] = (g_ref[...] + r_ref[...]).astype(BF16)

    grid_spec = pltpu.PrefetchScalarGridSpec(
        num_scalar_prefetch=1, grid=(N_CHIPS, r // tr),
        in_specs=[pl.BlockSpec((None, None, tr, c), lambda j, t, cc: (j, cc[0], t, 0)),
                  pl.BlockSpec((None, tr, c), lambda j, t, cc: (j, t, 0))],
        out_specs=pl.BlockSpec((None, tr, c), lambda j, t, cc: (j, t, 0)))
    cc = lax.axis_index("c").astype(jnp.int32).reshape(1)
    return pl.pallas_call(
        body, name=name, grid_spec=grid_spec,
        out_shape=jax.ShapeDtypeStruct((N_CHIPS, r, c), BF16),
        compiler_params=_cparams("parallel", "parallel"),
    )(cc, grad, recv)


def _chip_exchange_phases(s_refs, out_refs, send_sems, recv_sems):
    n = len(s_refs)
    x, y, cc = _mesh_pos()
    my_chip = 2 * x + y
    chips = [(1 - x, y), (x, 1 - y), (1 - x, 1 - y)]

    def outgoing():
        return [pltpu.make_async_remote_copy(
            src_ref=s_refs[i].at[2 * px + py], dst_ref=out_refs[i].at[my_chip],
            send_sem=send_sems.at[3 * i + j], recv_sem=recv_sems.at[3 * i + j],
            device_id=(px, py, cc), device_id_type=MESH_ID)
            for i in range(n) for j, (px, py) in enumerate(chips)]

    def start():
        for cp in outgoing():
            cp.start()

    def finish():
        for i in range(n):
            for j, (px, py) in enumerate(chips):
                pltpu.make_async_remote_copy(
                    src_ref=s_refs[i].at[my_chip], dst_ref=out_refs[i].at[2 * px + py],
                    send_sem=send_sems.at[3 * i + j], recv_sem=recv_sems.at[3 * i + j],
                    device_id=(px, py, cc), device_id_type=MESH_ID).wait_recv()
        for cp in outgoing():
            cp.wait_send()

    return start, finish


def _chip_exchange_scratch(n):
    return [pltpu.SemaphoreType.DMA((3 * n,)), pltpu.SemaphoreType.DMA((3 * n,))]


def sum_chips(sums, parts, name):
    _, r, c = parts.shape
    tr = r // 2 if r % 32 == 0 else r

    def body(idx_ref, s_ref, p1_ref, p2_ref, p3_ref, o_ref):
        o_ref[...] = ((s_ref[...].astype(F32) + p1_ref[...].astype(F32))
                      + p2_ref[...].astype(F32)) + p3_ref[...].astype(F32)

    def pick(k):
        return pl.BlockSpec((None, tr, c), lambda t, idx: (idx[k], t, 0))

    x, y = lax.axis_index("x"), lax.axis_index("y")
    idx = jnp.stack([2 * x + y, 2 * (1 - x) + y, 2 * x + (1 - y), 2 * (1 - x) + (1 - y)]).astype(jnp.int32)
    grid_spec = pltpu.PrefetchScalarGridSpec(
        num_scalar_prefetch=1, grid=(r // tr,),
        in_specs=[pick(0), pick(1), pick(2), pick(3)],
        out_specs=pl.BlockSpec((tr, c), lambda t, idx: (t, 0)))
    return pl.pallas_call(
        body, name=name, grid_spec=grid_spec,
        out_shape=jax.ShapeDtypeStruct((r, c), F32),
        compiler_params=_cparams("parallel"),
    )(idx, sums, parts, parts, parts)


def share_with_sibling(halves, name):
    n = len(halves)

    def body(*refs):
        h_refs, out_refs = refs[:n], refs[n:2 * n]
        send_sems, recv_sems = refs[2 * n], refs[2 * n + 1]
        x, y, cc = _mesh_pos()
        copies = [pltpu.make_async_remote_copy(
            src_ref=h_refs[i], dst_ref=out_refs[i],
            send_sem=send_sems.at[i], recv_sem=recv_sems.at[i],
            device_id=(x, y, 1 - cc), device_id_type=MESH_ID) for i in range(n)]
        for cp in copies:
            cp.start()
        for cp in copies:
            cp.wait_recv()
        for cp in copies:
            cp.wait_send()

    return pl.pallas_call(
        body, name=name,
        in_specs=[pl.BlockSpec(memory_space=pl.ANY)] * n,
        out_specs=[pl.BlockSpec(memory_space=pl.ANY)] * n,
        out_shape=[jax.ShapeDtypeStruct(h.shape, h.dtype) for h in halves],
        scratch_shapes=[pltpu.SemaphoreType.DMA((n,)), pltpu.SemaphoreType.DMA((n,))],
    )(*halves)


def _adam_update(w, g, m, v):
    m = ADAM_B1 * m + (1.0 - ADAM_B1) * g
    v = ADAM_B2 * v + (1.0 - ADAM_B2) * (g * g)
    m_hat = m / (1.0 - ADAM_B1 ** ADAM_STEP)
    v_hat = v / (1.0 - ADAM_B2 ** ADAM_STEP)
    delta = -ADAM_LR * (m_hat / (jnp.sqrt(v_hat) + ADAM_EPS) + ADAM_WD * w)
    return delta, m, v


ADAMW_STEPS = 8


def adamw(ws, g_mine, g_sibling, ms, vs, name, chip_sums=()):
    n, n_s = len(ws), len(chip_sums)
    per_half = ADAMW_STEPS // 2

    def body(*refs):
        cc_ref = refs[0]
        ins = refs[1:1 + 5 * n]
        outs = refs[1 + 5 * n + n_s:1 + 9 * n + n_s]
        step = pl.program_id(0)
        if n_s:
            start, finish = _chip_exchange_phases(
                refs[1 + 5 * n:1 + 5 * n + n_s], refs[1 + 9 * n + n_s:1 + 9 * n + 2 * n_s],
                *refs[1 + 9 * n + 2 * n_s:])
            pl.when(step == 0)(start)
        mine = (step // per_half) == cc_ref[0]
        for i in range(n):
            w_ref, ga_ref, gb_ref, m_ref, v_ref = ins[5 * i:5 * i + 5]
            g_ref, d_ref, nm_ref, nv_ref = outs[4 * i:4 * i + 4]
            g = jnp.where(mine, ga_ref[...], gb_ref[...])
            g_ref[...] = g
            d, nm, nv = _adam_update(w_ref[...], g, m_ref[...], v_ref[...])
            d_ref[...] = d
            nm_ref[...] = nm
            nv_ref[...] = nv
        if n_s:
            pl.when(step == ADAMW_STEPS - 1)(finish)

    in_specs, out_specs, out_shape, operands = [], [], [], []
    for w, ga, gb, m, v in zip(ws, g_mine, g_sibling, ms, vs):
        r, c = w.shape
        tr = r // ADAMW_STEPS
        full = pl.BlockSpec((tr, c), lambda t, cc: (t, 0))
        part = pl.BlockSpec((tr, c), lambda t, cc: (t % per_half, 0))
        in_specs += [full, part, part, full, full]
        out_specs += [full] * 4
        out_shape += [jax.ShapeDtypeStruct((r, c), F32)] * 4
        operands += [w, ga, gb, m, v]
    anywhere = pl.BlockSpec(memory_space=pl.ANY)
    grid_spec = pltpu.PrefetchScalarGridSpec(
        num_scalar_prefetch=1, grid=(ADAMW_STEPS,),
        in_specs=in_specs + [anywhere] * n_s, out_specs=out_specs + [anywhere] * n_s,
        scratch_shapes=_chip_exchange_scratch(n_s) if n_s else [])
    cc = lax.axis_index("c").astype(jnp.int32).reshape(1)
    res = pl.pallas_call(
        body, name=name, grid_spec=grid_spec,
        out_shape=out_shape + [jax.ShapeDtypeStruct(s.shape, s.dtype) for s in chip_sums],
        compiler_params=_cparams("arbitrary"),
    )(cc, *operands, *chip_sums)
    per_shard = [tuple(res[4 * i:4 * i + 4]) for i in range(n)]
    return (per_shard, list(res[4 * n:])) if n_s else per_shard


def _pack_gather_phases(p_ref, out_ref, send_sems, recv_sems, local_sem):
    x, y, cc = _mesh_pos()
    me = 4 * x + 2 * y + cc
    flips = [(fx, fy, fc) for fx in (0, 1) for fy in (0, 1) for fc in (0, 1)][1:]

    def copy(k, row):
        fx, fy, fc = flips[k]
        return pltpu.make_async_remote_copy(
            src_ref=p_ref, dst_ref=out_ref.at[row],
            send_sem=send_sems.at[k], recv_sem=recv_sems.at[k],
            device_id=(x ^ fx, y ^ fy, cc ^ fc), device_id_type=MESH_ID)

    def local():
        return pltpu.make_async_copy(p_ref, out_ref.at[me], local_sem)

    def start():
        local().start()
        for k in range(len(flips)):
            copy(k, me).start()

    def finish():
        for k, (fx, fy, fc) in enumerate(flips):
            copy(k, 4 * (x ^ fx) + 2 * (y ^ fy) + (cc ^ fc)).wait_recv()
        for k in range(len(flips)):
            copy(k, me).wait_send()
        local().wait()

    return start, finish


def small_update(gathered, wpack, mpack, vpack):
    hw = HGRN_WIDTH

    def body(g_ref, w_ref, m_ref, v_ref, go_ref, d_ref, nm_ref, nv_ref, loss_ref):
        g = g_ref[0]
        for d in range(1, N_DEV):
            g = g + g_ref[d]
        wv = w_ref[...]
        a0, a1 = wv[4:5, :hw], wv[4:5, hw:]
        mx = jnp.maximum(a0, a1)
        e0, e1 = jnp.exp(a0 - mx), jnp.exp(a1 - mx)
        lb = e0 / (e0 + e1)
        dl = g[4:5, :hw] * lb * (1.0 - lb)
        row = lax.broadcasted_iota(jnp.int32, g.shape, 0)
        lb_row = jnp.concatenate([dl, -dl], axis=1)
        grads = jnp.where(row == 4, lb_row, jnp.where(row < 4, g, 0.0))
        go_ref[...] = grads
        d, nm, nv = _adam_update(wv, grads, m_ref[...], v_ref[...])
        d_ref[...] = d
        nm_ref[...] = nm
        nv_ref[...] = nv
        loss_ref[...] = jnp.zeros((8, LANES), F32) + jnp.sum(g[5:6, :])

    vm = pl.BlockSpec(memory_space=pltpu.VMEM)
    return pl.pallas_call(
        body, name="small_update",
        in_specs=[vm] * 4, out_specs=[vm] * 5,
        out_shape=[jax.ShapeDtypeStruct(wpack.shape, F32)] * 4 + [jax.ShapeDtypeStruct((8, LANES), F32)],
    )(gathered, wpack, mpack, vpack)


def _pack_small(n1, n2, fn, hn, lbl):
    z = jnp.zeros((1, D_MODEL - HGRN_WIDTH), F32)
    rows = [n1.reshape(1, D_MODEL), n2.reshape(1, D_MODEL), fn.reshape(1, D_MODEL),
            jnp.concatenate([hn.reshape(1, HGRN_WIDTH), z], axis=1), lbl.reshape(1, 2 * HGRN_WIDTH),
            jnp.zeros((3, D_MODEL), F32)]
    return jnp.concatenate(rows, axis=0)


def _unpack_small(pack):
    return (pack[0:1], pack[4].reshape(2, HGRN_WIDTH), pack[3:4, :HGRN_WIDTH], pack[1:2], pack[2])


def kernel(x, norm1_w, w_in, lb_logits, hgrn_norm_w, w_out, norm2_w, w_gate_up, w_down, final_norm_w, loss_target, m_norm1_w, m_w_in, m_lb_logits, m_hgrn_norm_w, m_w_out, m_norm2_w, m_w_gate_up, m_w_down, m_final_norm_w, v_norm1_w, v_w_in, v_lb_logits, v_hgrn_norm_w, v_w_out, v_norm2_w, v_w_gate_up, v_w_down, v_final_norm_w):
    seq = x.shape[1]
    xs = x.reshape(seq, D_MODEL)
    target = loss_target.reshape(seq, D_MODEL)
    shards = {"w_in": w_in[0], "w_out": w_out[0], "w_gu": w_gate_up[0], "w_down": w_down[0]}

    cast = {k: cast_bf16(w, "cast_" + k) for k, w in shards.items()}
    w_in4 = allgather_halves(cast["w_in"], "gather_w_in").reshape(N_CHIPS, D_MODEL, -1)

    cos_t, sin_t = _rope_tables(seq)
    fw = final_norm_w.reshape(1, D_MODEL)

    qr, kr, va, hg, u, g_out, g_down = in_proj(
        xs, norm1_w, w_in4, cos_t, sin_t, [cast["w_out"], cast["w_down"]])
    ya, lse = attn_fwd(qr, kr, va)
    yb, o_pre, st0, g_gu = hgrn_fwd(hg, lb_logits, hgrn_norm_w, [cast["w_gu"]])
    w_out_f = g_out.reshape(D_MODEL, D_MODEL)
    w_gu4 = g_gu.reshape(N_CHIPS, D_MODEL, -1)
    w_down_f = g_down.reshape(FFN_HIDDEN, D_MODEL)
    mixed, h1, u2, g, up, act, dh2, acc_fin = ffn_fwd(
        ya, yb, xs, w_out_f, norm2_w, w_gu4, w_down_f, fw, target)

    cw_in, cw_gu = w_in4.shape[2], w_gu4.shape[2]
    dgu, dh1, dya, dyb, delta, acc_n2 = ffn_bwd(dh2, w_down_f, g, up, w_gu4, h1, norm2_w, w_out_f, ya)
    early = [
        weight_grad(mixed, dh1, D_MODEL, "wgrad_out").reshape(N_CHIPS, 2, D_MODEL // 8, D_MODEL),
        weight_grad(u2, dgu, cw_gu, "wgrad_gu", group=2).reshape(N_CHIPS, 2, D_MODEL // 2, cw_gu),
        weight_grad(act, dh2, D_MODEL, "wgrad_down").reshape(N_CHIPS, 2, FFN_HIDDEN // 8, D_MODEL),
    ]
    early_names = ["out", "gu", "down"]
    dq, dk, dv, *early_recv = attn_bwd(qr, kr, va, dya, lse, delta, early)
    early_sums = [add_own_half(gr, rc, "add_half_" + nm) for gr, rc, nm in zip(early, early_recv, early_names)]
    dhg, acc_hg, *early_parts = hgrn_bwd(hg, lb_logits, hgrn_norm_w, o_pre, st0, dyb, early_sums)
    dproj, dx, acc_n1 = in_bwd(dq, dk, dv, dhg, cos_t, sin_t, w_in4, xs, norm1_w, dh1)
    z512 = jnp.zeros((1, D_MODEL - HGRN_WIDTH), F32)
    gpack = jnp.concatenate([
        acc_n1[0:1], acc_n2[0:1], acc_fin[0:1],
        jnp.concatenate([acc_hg[0:1], z512], axis=1), jnp.concatenate([acc_hg[1:2], z512], axis=1),
        acc_fin[1:2], jnp.zeros((2, D_MODEL), F32)], axis=0)
    g_in, gathered_packs = weight_grad(u, dproj, cw_in, "wgrad_in", group=2, small_pack=gpack)
    late = [g_in.reshape(N_CHIPS, 2, D_MODEL // 2, cw_in)]
    late_recv = exchange_with_sibling(late, "grad_exchange_sibling_late")
    late_sums = [add_own_half(late[0], late_recv[0], "add_half_in")]

    early_halves = [sum_chips(s, p, "sum_chips_" + nm) for s, p, nm in zip(early_sums, early_parts, early_names)]
    early_others = share_with_sibling(early_halves, "grad_share_sibling_early")
    early_keys = ["w_out", "w_gu", "w_down"]
    moments = {"w_in": (m_w_in, v_w_in), "w_out": (m_w_out, v_w_out),
               "w_gu": (m_w_gate_up, v_w_gate_up), "w_down": (m_w_down, v_w_down)}
    early_updates, late_parts = adamw(
        [shards[k] for k in early_keys], early_halves, early_others,
        [moments[k][0][0] for k in early_keys], [moments[k][1][0] for k in early_keys],
        "adamw_early", chip_sums=late_sums)
    late_halves = [sum_chips(late_sums[0], late_parts[0], "sum_chips_in")]
    late_others = share_with_sibling(late_halves, "grad_share_sibling_late")
    late_updates = adamw([shards["w_in"]], late_halves, late_others,
                         [moments["w_in"][0][0]], [moments["w_in"][1][0]], "adamw_in")
    big = {k: tuple(t[None] for t in upd) for k, upd in zip(early_keys + ["w_in"], early_updates + late_updates)}

    wpack = _pack_small(norm1_w, norm2_w, final_norm_w, hgrn_norm_w, lb_logits)
    mpack = _pack_small(m_norm1_w, m_norm2_w, m_final_norm_w, m_hgrn_norm_w, m_lb_logits)
    vpack = _pack_small(v_norm1_w, v_norm2_w, v_final_norm_w, v_hgrn_norm_w, v_lb_logits)
    gs, ds, nms, nvs, loss8 = small_update(gathered_packs, wpack, mpack, vpack)
    loss = loss8[0, 0]

    def assemble(small_pack, idx):
        n1, lbl, hn, n2, fn = _unpack_small(small_pack)
        return (n1, big["w_in"][idx], lbl, hn, big["w_out"][idx], n2, big["w_gu"][idx], big["w_down"][idx], fn)

    return (loss, dx.reshape(x.shape), *assemble(gs, 0), *assemble(ds, 1), *assemble(nms, 2), *assemble(nvs, 3))
```

```python
import functools

import jax
import jax.numpy as jnp
from jax import lax
from jax.experimental import pallas as pl
from jax.experimental.pallas import tpu as pltpu

F32 = jnp.float32
BF16 = jnp.bfloat16

D_MODEL = 1024
ATTN_WIDTH = 512
HEAD_DIM = 64
DILATED_PAIRS = ((128, 1), (512, 4), (2048, 16))
ATTN_BLOCK = 128
ROPE_THETA = 10000.0
HGRN_WIDTH = 512
HGRN_CHUNK = 16
HGRN_HEADS = 4
IN_PROJ_WIDTH = 3584
FFN_HIDDEN = 2816
NORM_EPS = 1e-6
ATTN_SCALE = HEAD_DIM ** -0.5
N_CHIPS = 4
N_DEV = 8

ADAM_LR = 0.001
ADAM_B1 = 0.9
ADAM_B2 = 0.999
ADAM_EPS = 1e-08
ADAM_WD = 0.01
ADAM_STEP = 10

LANES = 128
HGRN_ROWS = 128
ROW_TILE = 256
ATTN_STEP_ROWS = 2048
ATTN_FWD_UNROLL = 8
ATTN_BWD_UNROLL = 8
ATTN_TRANSPOSED_BELOW = 16
VMEM_LIMIT = 56 * 1024 * 1024
NEG_BIG = -1e30
MESH_ID = pl.DeviceIdType.MESH


def _cparams(*sem):
    return pltpu.CompilerParams(dimension_semantics=tuple(sem), vmem_limit_bytes=VMEM_LIMIT)


def _dot(a, b):
    return jnp.dot(a, b, preferred_element_type=F32)


def _dot_nt(a, b):
    return lax.dot_general(a, b, (((1,), (1,)), ((), ())), preferred_element_type=F32)


def _dot_tn(a, b):
    return lax.dot_general(a, b, (((0,), (0,)), ((), ())), preferred_element_type=F32)


def _sigmoid(x):
    return 1.0 / (1.0 + jnp.exp(-x))


def _full(shape):
    n = len(shape)
    return pl.BlockSpec(shape, lambda *_: (0,) * n)


def _weight(shape):
    n = len(shape)
    return pl.BlockSpec(shape, lambda *_: (0,) * n, pipeline_mode=pl.Buffered(1))


def _rows(tm, width):
    return pl.BlockSpec((tm, width), lambda i: (i, 0))


def _swap32(x):
    lane = lax.broadcasted_iota(jnp.int32, x.shape, 1)
    first = (lane % HEAD_DIM) < (HEAD_DIM // 2)
    return jnp.where(first, pltpu.roll(x, LANES - 32, axis=1), pltpu.roll(x, 32, axis=1))


def _rotary_fwd(x, cos, sin_signed):
    parts = []
    for j in range(x.shape[1] // LANES):
        xc = x[:, j * LANES:(j + 1) * LANES]
        parts.append(xc * cos + _swap32(xc) * sin_signed)
    return jnp.concatenate(parts, axis=1)


def _rotary_bwd(dy, cos, sin_signed):
    parts = []
    for j in range(dy.shape[1] // LANES):
        dc = dy[:, j * LANES:(j + 1) * LANES]
        parts.append(dc * cos + _swap32(dc * sin_signed))
    return jnp.concatenate(parts, axis=1)


def _rope_tables(seq):
    half = HEAD_DIM // 2
    inv_freq = ROPE_THETA ** (-jnp.arange(half, dtype=F32) / half)
    ang = jnp.arange(seq, dtype=F32)[:, None] * inv_freq[None, :]
    cos, sin = jnp.cos(ang), jnp.sin(ang)
    cos_t = jnp.tile(cos, (1, LANES // half))
    sin_t = jnp.tile(jnp.concatenate([-sin, sin], axis=1), (1, LANES // HEAD_DIM))
    return cos_t, sin_t


def cast_bf16(w, name):
    r, c = w.shape
    half = r // 2

    def body(w_ref, o_ref):
        o_ref[...] = w_ref[...].astype(BF16)

    return pl.pallas_call(
        body, name=name, grid=(2,),
        in_specs=[pl.BlockSpec((half, c), lambda i: (i, 0))],
        out_specs=pl.BlockSpec((None, half, c), lambda i: (i, 0, 0)),
        out_shape=jax.ShapeDtypeStruct((2, half, c), BF16),
        compiler_params=_cparams("parallel"),
    )(w)


def _mesh_pos():
    return lax.axis_index("x"), lax.axis_index("y"), lax.axis_index("c")


GATHER_COPIES = 7


def _gather_phases(x_refs, out_refs, send_sems, recv_sems, local_sems):
    n = len(x_refs)
    x, y, cc = _mesh_pos()
    me, sibling = (x, y, cc), (x, y, 1 - cc)
    chips = [(1 - x, y), (x, 1 - y), (1 - x, 1 - y)]

    def rows(i, px, py, pc):
        return out_refs[i].at[4 * px + 2 * py + pc]

    def copy(i, k, block, to, src=None):
        return pltpu.make_async_remote_copy(
            src_ref=rows(i, *block) if src is None else src, dst_ref=rows(i, *block),
            send_sem=send_sems.at[GATHER_COPIES * i + k], recv_sem=recv_sems.at[GATHER_COPIES * i + k],
            device_id=to, device_id_type=MESH_ID)

    def local(i):
        return pltpu.make_async_copy(x_refs[i].at[cc], rows(i, *me), local_sems.at[i])

    def first(i):
        mine = x_refs[i].at[cc]
        return [copy(i, 0, me, sibling, src=mine)] + [
            copy(i, 1 + j, me, (*chip, cc), src=mine) for j, chip in enumerate(chips)]

    def passed(i):
        return [copy(i, 4 + j, (*chip, cc), sibling) for j, chip in enumerate(chips)]

    def start():
        for i in range(n):
            local(i).start()
            for cp in first(i):
                cp.start()

    def forward():
        for i in range(n):
            onward = passed(i)
            for j, chip in enumerate(chips):
                copy(i, 1 + j, (*chip, cc), me).wait_recv()
                onward[j].start()

    def finish():
        for i in range(n):
            copy(i, 0, sibling, me).wait_recv()
            for j, chip in enumerate(chips):
                copy(i, 4 + j, (*chip, 1 - cc), me).wait_recv()
            for cp in first(i) + passed(i):
                cp.wait_send()
            local(i).wait()

    return start, forward, finish


def _gather_scratch(n):
    return [pltpu.SemaphoreType.DMA((GATHER_COPIES * n,)), pltpu.SemaphoreType.DMA((GATHER_COPIES * n,)),
            pltpu.SemaphoreType.DMA((n,))]


def _gathered_shape(halves):
    return jax.ShapeDtypeStruct((N_DEV,) + halves.shape[1:], halves.dtype)


def allgather_halves(halves, name):
    def body(x_ref, out_ref, send_sems, recv_sems, local_sems):
        start, forward, finish = _gather_phases([x_ref], [out_ref], send_sems, recv_sems, local_sems)
        start()
        forward()
        finish()

    return pl.pallas_call(
        body, name=name,
        in_specs=[pl.BlockSpec(memory_space=pl.ANY)],
        out_specs=pl.BlockSpec(memory_space=pl.ANY),
        out_shape=_gathered_shape(halves),
        scratch_shapes=_gather_scratch(1),
    )(halves)


def _rms(x):
    return lax.rsqrt(jnp.mean(x * x, axis=-1, keepdims=True) + NORM_EPS)


def in_proj(x, norm1_w, w_in4, cos_t, sin_t, weight_halves=()):
    seq = x.shape[0]
    tm = ROW_TILE
    cw = w_in4.shape[2]
    n_w = len(weight_halves)
    steps = seq // tm

    def body(*refs):
        x_ref, nw_ref, w_ref, cos_ref, sin_ref = refs[:5]
        q_ref, k_ref, v_ref, hg_ref, u_ref = refs[5 + n_w:10 + n_w]
        step = pl.program_id(0)
        if n_w:
            start, forward, finish = _gather_phases(
                refs[5:5 + n_w], refs[10 + n_w:10 + 2 * n_w], *refs[10 + 2 * n_w:])
            pl.when(step == 0)(start)
            pl.when(step == (3 * steps) // 4)(forward)
        xv = x_ref[...]
        u = ((xv * _rms(xv)) * nw_ref[...]).astype(BF16)
        u_ref[...] = u
        proj = jnp.concatenate([_dot(u, w_ref[j]) for j in range(N_CHIPS)], axis=1)
        cos, sin = cos_ref[...], sin_ref[...]
        a = ATTN_WIDTH
        q_ref[...] = _rotary_fwd(proj[:, :a], cos, sin)
        k_ref[...] = _rotary_fwd(proj[:, a:2 * a], cos, sin)
        v_ref[...] = proj[:, 2 * a:3 * a]
        hg_ref[...] = proj[:, 3 * a:]
        if n_w:
            pl.when(step == steps - 1)(finish)

    anywhere = pl.BlockSpec(memory_space=pl.ANY)
    return pl.pallas_call(
        body, name="in_proj", grid=(steps,),
        in_specs=[_rows(tm, D_MODEL), _full((1, D_MODEL)), _weight((N_CHIPS, D_MODEL, cw)),
                  _rows(tm, LANES), _rows(tm, LANES)] + [anywhere] * n_w,
        out_specs=[_rows(tm, ATTN_WIDTH)] * 3 + [_rows(tm, 4 * HGRN_WIDTH), _rows(tm, D_MODEL)]
        + [anywhere] * n_w,
        out_shape=[jax.ShapeDtypeStruct((seq, ATTN_WIDTH), F32)] * 3
        + [jax.ShapeDtypeStruct((seq, 4 * HGRN_WIDTH), F32), jax.ShapeDtypeStruct((seq, D_MODEL), BF16)]
        + [_gathered_shape(h) for h in weight_halves],
        scratch_shapes=_gather_scratch(n_w) if n_w else [],
        compiler_params=_cparams("arbitrary"),
    )(x, norm1_w, w_in4, cos_t, sin_t, *weight_halves)


def _head_masks():
    lane = lax.broadcasted_iota(jnp.int32, (1, LANES), 1)
    return [(lane // HEAD_DIM) == h for h in range(LANES // HEAD_DIM)]


def _window_valid(no_prev):
    qi = lax.broadcasted_iota(jnp.int32, (ATTN_BLOCK, 2 * ATTN_BLOCK), 0)
    kj = lax.broadcasted_iota(jnp.int32, (ATTN_BLOCK, 2 * ATTN_BLOCK), 1)
    valid = (kj >= qi) & (kj <= qi + ATTN_BLOCK)
    return valid & (jnp.logical_not(no_prev) | (kj >= ATTN_BLOCK))


def _strided_rows(start, dilation):
    if dilation == 1:
        return pl.ds(start, ATTN_BLOCK)
    return pl.ds(start, ATTN_BLOCK, stride=dilation)


def _block_before(edge_ref, cur_ref, t, r, span, dilation, per_step):
    edge = edge_ref[_strided_rows(ATTN_STEP_ROWS - span + r, dilation), :]
    if per_step == 1:
        return edge
    inside = cur_ref[_strided_rows(r + span * jnp.maximum(t - 1, 0), dilation), :]
    return jnp.where(t == 0, edge, inside)


def _attn_specs():
    cur = pl.BlockSpec((ATTN_STEP_ROWS, LANES), lambda hp, j: (j, hp))
    prev = pl.BlockSpec((ATTN_STEP_ROWS, LANES), lambda hp, j: (jnp.maximum(j - 1, 0), hp))
    return cur, prev


def _for_each_block(dilation, unroll, block):
    span = ATTN_BLOCK * dilation
    per_step = ATTN_STEP_ROWS // span

    def trip(it, carry):
        block(it // dilation, it % dilation, span, per_step)
        return carry

    lax.fori_loop(0, per_step * dilation, trip, 0, unroll=unroll)


def attn_fwd(q, k, v):
    seq = q.shape[0]
    cur, prev = _attn_specs()

    def body(q_ref, kc_ref, vc_ref, kp_ref, vp_ref, y_ref, lse_ref):
        first_step = pl.program_id(1) == 0
        masks = _head_masks()
        for index, (_, dilation) in enumerate(DILATED_PAIRS):
            def block(t, r, span, per_step, dilation=dilation, merge=index > 0):
                rows = _strided_rows(r + span * t, dilation)
                q2 = (q_ref[rows, :] * ATTN_SCALE).astype(BF16)
                kp = _block_before(kp_ref, kc_ref, t, r, span, dilation, per_step)
                vp = _block_before(vp_ref, vc_ref, t, r, span, dilation, per_step)
                k2 = jnp.concatenate([kp, kc_ref[rows, :]], axis=0).astype(BF16)
                v2 = jnp.concatenate([vp, vc_ref[rows, :]], axis=0).astype(BF16)
                valid = _window_valid(first_step & (t == 0))
                o_acc = jnp.zeros((ATTN_BLOCK, LANES), F32)
                l_acc = jnp.zeros((ATTN_BLOCK, LANES), F32)
                for mh in masks:
                    qm = jnp.where(mh, q2, jnp.zeros_like(q2))
                    s = jnp.where(valid, _dot_nt(qm, k2), NEG_BIG)
                    m = jnp.max(s, axis=-1, keepdims=True)
                    p = jnp.exp(s - m)
                    l = jnp.sum(p, axis=-1, keepdims=True)
                    o = _dot(p.astype(BF16), v2) / l
                    o_acc = jnp.where(mh, o, o_acc)
                    l_acc = jnp.where(mh, m + jnp.log(l), l_acc)
                if merge:
                    y_old, l_old = y_ref[rows, :], lse_ref[rows, :]
                    mx = jnp.maximum(l_old, l_acc)
                    e_old, e_new = jnp.exp(l_old - mx), jnp.exp(l_acc - mx)
                    den = e_old + e_new
                    o_acc = (y_old * e_old + o_acc * e_new) / den
                    l_acc = mx + jnp.log(den)
                y_ref[rows, :] = o_acc
                lse_ref[rows, :] = l_acc

            _for_each_block(dilation, ATTN_FWD_UNROLL, block)

    return pl.pallas_call(
        body, name="attn_fwd", grid=(ATTN_WIDTH // LANES, seq // ATTN_STEP_ROWS),
        in_specs=[cur, cur, cur, prev, prev],
        out_specs=[cur, cur],
        out_shape=[jax.ShapeDtypeStruct((seq, ATTN_WIDTH), F32)] * 2,
        compiler_params=_cparams("parallel", "parallel"),
    )(q, k, v, k, v)


def _chunk_cumsum(x, reverse=False):
    rc = lax.broadcasted_iota(jnp.int32, x.shape, 0) % HGRN_CHUNK
    sh = 1
    while sh < HGRN_CHUNK:
        if reverse:
            x = x + jnp.where(rc + sh < HGRN_CHUNK, pltpu.roll(x, x.shape[0] - sh, axis=0), 0.0)
        else:
            x = x + jnp.where(rc >= sh, pltpu.roll(x, sh, axis=0), 0.0)
        sh *= 2
    return x


def _chunk_row(x, row):
    return _chunk_rows([x[n * HGRN_CHUNK + row:n * HGRN_CHUNK + row + 1, :]
                        for n in range(x.shape[0] // HGRN_CHUNK)])


def _chunk_rows(rows):
    return jnp.concatenate([jnp.broadcast_to(r, (HGRN_CHUNK, r.shape[1])) for r in rows], axis=0)


def _hgrn_prep(hg, lbl):
    w = HGRN_WIDTH
    a0, a1 = lbl[0:1, :], lbl[1:2, :]
    mx = jnp.maximum(a0, a1)
    e0, e1 = jnp.exp(a0 - mx), jnp.exp(a1 - mx)
    lb = e0 / (e0 + e1)
    qb, fb, gb = hg[:, :w], hg[:, w:2 * w], hg[:, 3 * w:]
    sg = _sigmoid(fb)
    f = lb + (1.0 - lb) * sg
    b = _chunk_cumsum(jnp.log(f))
    bmid, btot = _chunk_row(b, HGRN_CHUNK // 2 - 1), _chunk_row(b, HGRN_CHUNK - 1)
    sq = _sigmoid(qb)
    p = dict(lb=lb, sg=sg, f=f, kk=1.0 - f, sq=sq, qf=qb * sq, gb=gb,
             e_iq=jnp.exp(b - bmid), e_ik=jnp.exp(bmid - b), e_b=jnp.exp(b),
             e_bb=jnp.exp(btot - b), e_tot=jnp.exp(btot))
    p["qi"] = p["qf"] * p["e_iq"]
    p["ki"] = p["kk"] * p["e_ik"]
    p["qs"] = p["qf"] * p["e_b"]
    p["kb"] = p["kk"] * p["e_bb"]
    return p


def _chunk_masks():
    t = lax.broadcasted_iota(jnp.int32, (HGRN_ROWS, HGRN_ROWS), 0)
    s = lax.broadcasted_iota(jnp.int32, (HGRN_ROWS, HGRN_ROWS), 1)
    tril = ((t // HGRN_CHUNK) == (s // HGRN_CHUNK)) & (s <= t)
    n_chunks = HGRN_ROWS // HGRN_CHUNK
    tt = lax.broadcasted_iota(jnp.int32, (HGRN_ROWS, n_chunks * LANES), 0)
    cc = lax.broadcasted_iota(jnp.int32, (HGRN_ROWS, n_chunks * LANES), 1)
    block = (tt // HGRN_CHUNK) == (cc // LANES)
    return tril, block


def _spread(x, block):
    n_chunks = HGRN_ROWS // HGRN_CHUNK
    return jnp.where(block, jnp.tile(x, (1, n_chunks)), jnp.zeros((), x.dtype))


def _fold(x_full, block):
    n_chunks = HGRN_ROWS // HGRN_CHUNK
    z = jnp.where(block, x_full, 0.0)
    acc = z[:, :LANES]
    for n in range(1, n_chunks):
        acc = acc + z[:, n * LANES:(n + 1) * LANES]
    return acc


def hgrn_fwd(hg, lb_logits, hnw, weight_halves=()):
    seq = hg.shape[0]
    nblk = seq // HGRN_ROWS
    n_chunks = HGRN_ROWS // HGRN_CHUNK
    n_w = len(weight_halves)

    def body(*refs):
        hg_ref, lbl_ref, hnw_ref = refs[:3]
        w_refs = refs[3:3 + n_w]
        yb_ref, o_ref, st0_ref = refs[3 + n_w:6 + n_w]
        g_refs = refs[6 + n_w:6 + 2 * n_w]
        st_scr = refs[6 + 2 * n_w]
        step = pl.program_id(0)
        if n_w:
            start, forward, finish = _gather_phases(w_refs, g_refs, *refs[7 + 2 * n_w:])
            pl.when(step == 0)(start)
            pl.when(step == (3 * nblk) // 4)(forward)

        @pl.when(step == 0)
        def _():
            st_scr[...] = jnp.zeros_like(st_scr)

        hg_v = hg_ref[...]
        p = _hgrn_prep(hg_v, lbl_ref[...])
        tril, block = _chunk_masks()
        vv = hg_v[:, 2 * HGRN_WIDTH:3 * HGRN_WIDTH].astype(BF16)
        outs = []
        for h in range(HGRN_HEADS):
            sl = slice(h * LANES, (h + 1) * LANES)
            v_h = vv[:, sl]
            a = jnp.where(tril, _dot_nt(p["qi"][:, sl].astype(BF16), p["ki"][:, sl].astype(BF16)), 0.0)
            o = _dot(a.astype(BF16), v_h)
            upd = _dot_tn(v_h, _spread(p["kb"][:, sl].astype(BF16), block))
            st = st_scr[h]
            st0_ref[h] = st
            parts = []
            for n in range(n_chunks):
                parts.append(st.astype(BF16))
                decay = p["e_tot"][n * HGRN_CHUNK:n * HGRN_CHUNK + 1, sl]
                st = st * decay + upd[:, n * LANES:(n + 1) * LANES]
            st_scr[h] = st
            o = o + _dot_nt(_spread(p["qs"][:, sl].astype(BF16), block), jnp.concatenate(parts, axis=1))
            outs.append(o)
        o_all = jnp.concatenate(outs, axis=1)
        o_ref[...] = o_all
        normed = jnp.concatenate(
            [outs[h] * _rms(outs[h]) for h in range(HGRN_HEADS)], axis=1)
        gb = p["gb"]
        yb_ref[...] = (normed * hnw_ref[...]) * (gb * _sigmoid(gb))
        if n_w:
            pl.when(step == nblk - 1)(finish)

    anywhere = pl.BlockSpec(memory_space=pl.ANY)
    return pl.pallas_call(
        body, name="hgrn_fwd", grid=(nblk,),
        in_specs=[_rows(HGRN_ROWS, 4 * HGRN_WIDTH), _full((2, HGRN_WIDTH)), _full((1, HGRN_WIDTH))]
        + [anywhere] * n_w,
        out_specs=[_rows(HGRN_ROWS, HGRN_WIDTH), _rows(HGRN_ROWS, HGRN_WIDTH),
                   pl.BlockSpec((None, HGRN_HEADS, LANES, LANES), lambda i: (i, 0, 0, 0))] + [anywhere] * n_w,
        out_shape=[jax.ShapeDtypeStruct((seq, HGRN_WIDTH), F32)] * 2
        + [jax.ShapeDtypeStruct((nblk, HGRN_HEADS, LANES, LANES), F32)]
        + [_gathered_shape(h) for h in weight_halves],
        scratch_shapes=[pltpu.VMEM((HGRN_HEADS, LANES, LANES), F32)] + (_gather_scratch(n_w) if n_w else []),
        compiler_params=_cparams("arbitrary"),
    )(hg, lb_logits, hnw, *weight_halves)


def ffn_fwd(ya, yb, x, w_out, norm2_w, w_gu4, w_down, final_w, target):
    seq = x.shape[0]
    tm = ROW_TILE
    cw = w_gu4.shape[2]
    inv_d = 1.0 / D_MODEL

    def body(ya_ref, yb_ref, x_ref, wo_ref, nw_ref, wgu_ref, wd_ref, fw_ref, t_ref,
             mixed_ref, h1_ref, u2_ref, g_ref, up_ref, act_ref, dh2_ref, acc_ref):
        @pl.when(pl.program_id(0) == 0)
        def _():
            acc_ref[...] = jnp.zeros_like(acc_ref)

        mixed = jnp.concatenate([ya_ref[...], yb_ref[...]], axis=1).astype(BF16)
        mixed_ref[...] = mixed
        h1 = x_ref[...] + _dot(mixed, wo_ref[...])
        h1_ref[...] = h1
        u2 = ((h1 * _rms(h1)) * nw_ref[...]).astype(BF16)
        u2_ref[...] = u2
        g = jnp.concatenate([_dot(u2, wgu_ref[0]), _dot(u2, wgu_ref[1])], axis=1)
        up = jnp.concatenate([_dot(u2, wgu_ref[2]), _dot(u2, wgu_ref[3])], axis=1)
        g_ref[...] = g.astype(BF16)
        up_ref[...] = up.astype(BF16)
        act = ((g * _sigmoid(g)) * up).astype(BF16)
        act_ref[...] = act
        h2 = h1 + _dot(act, wd_ref[...])
        rf = _rms(h2)
        n = h2 * rf
        fw = fw_ref[...]
        err = n * fw - t_ref[...]
        dy = err * inv_d
        acc_ref[0:1, :] += jnp.sum(dy * n, axis=0, keepdims=True)
        acc_ref[1:2, :] += (0.5 * inv_d) * jnp.sum(err * err, axis=0, keepdims=True)
        dn = dy * fw
        dh2_ref[...] = rf * (dn - n * jnp.mean(dn * n, axis=-1, keepdims=True))

    half = _rows(tm, ATTN_WIDTH)
    wide = _rows(tm, D_MODEL)
    ffn = _rows(tm, FFN_HIDDEN)
    return pl.pallas_call(
        body, name="ffn_fwd", grid=(seq // tm,),
        in_specs=[half, half, wide, _weight((D_MODEL, D_MODEL)), _full((1, D_MODEL)),
                  _weight((N_CHIPS, D_MODEL, cw)), _weight((FFN_HIDDEN, D_MODEL)), _full((1, D_MODEL)), wide],
        out_specs=[wide, wide, wide, ffn, ffn, ffn, wide, _full((8, D_MODEL))],
        out_shape=[jax.ShapeDtypeStruct((seq, D_MODEL), BF16), jax.ShapeDtypeStruct((seq, D_MODEL), F32),
                   jax.ShapeDtypeStruct((seq, D_MODEL), BF16)]
        + [jax.ShapeDtypeStruct((seq, FFN_HIDDEN), BF16)] * 3
        + [jax.ShapeDtypeStruct((seq, D_MODEL), F32), jax.ShapeDtypeStruct((8, D_MODEL), F32)],
        compiler_params=_cparams("arbitrary"),
    )(ya, yb, x, w_out, norm2_w, w_gu4, w_down, final_w, target)


def _head_sum_matrix():
    i = jnp.arange(ATTN_WIDTH)
    return ((i[:, None] // HEAD_DIM) == (i[None, :] // HEAD_DIM)).astype(BF16)


def ffn_bwd(dh2, w_down, g, up, w_gu4, h1, norm2_w, w_out, ya):
    seq = h1.shape[0]
    tm = ROW_TILE
    cw = w_gu4.shape[2]
    hsum = _head_sum_matrix()

    def body(dh2_ref, wd_ref, g_ref, up_ref, w_ref, h1_ref, nw_ref, wo_ref, ya_ref, hs_ref,
             dgu_ref, dh1_ref, dya_ref, dyb_ref, delta_ref, acc_ref):
        @pl.when(pl.program_id(0) == 0)
        def _():
            acc_ref[...] = jnp.zeros_like(acc_ref)

        dh2_b = dh2_ref[...].astype(BF16)
        du2 = jnp.zeros((tm, D_MODEL), F32)
        for j in range(N_CHIPS // 2):
            dact = _dot_nt(dh2_b, wd_ref[j * cw:(j + 1) * cw, :])
            gv = g_ref[:, j * cw:(j + 1) * cw].astype(F32)
            sg = _sigmoid(gv)
            dg = (dact * up_ref[:, j * cw:(j + 1) * cw].astype(F32) * (sg * (1.0 + gv * (1.0 - sg)))).astype(BF16)
            dup = (dact * (gv * sg)).astype(BF16)
            dgu_ref[:, j * cw:(j + 1) * cw] = dg
            dgu_ref[:, FFN_HIDDEN + j * cw:FFN_HIDDEN + (j + 1) * cw] = dup
            du2 = du2 + _dot_nt(dg, w_ref[j]) + _dot_nt(dup, w_ref[N_CHIPS // 2 + j])
        h1 = h1_ref[...]
        r2 = _rms(h1)
        nh = h1 * r2
        acc_ref[0:1, :] += jnp.sum(du2 * nh, axis=0, keepdims=True)
        dn = du2 * nw_ref[...]
        dh1 = dh2_ref[...] + r2 * (dn - nh * jnp.mean(dn * nh, axis=-1, keepdims=True))
        dh1_ref[...] = dh1
        dmixed = _dot_nt(dh1.astype(BF16), wo_ref[...])
        dya = dmixed[:, :ATTN_WIDTH]
        dya_ref[...] = dya
        dyb_ref[...] = dmixed[:, ATTN_WIDTH:]
        prod = dya * ya_ref[...]
        hi = prod.astype(BF16)
        lo = (prod - hi.astype(F32)).astype(BF16)
        delta_ref[...] = _dot(hi, hs_ref[...]) + _dot(lo, hs_ref[...])

    wide = _rows(tm, D_MODEL)
    half = _rows(tm, ATTN_WIDTH)
    ffn = _rows(tm, FFN_HIDDEN)
    return pl.pallas_call(
        body, name="ffn_bwd", grid=(seq // tm,),
        in_specs=[wide, _weight((FFN_HIDDEN, D_MODEL)), ffn, ffn, _weight((N_CHIPS, D_MODEL, cw)), wide,
                  _full((1, D_MODEL)), _weight((D_MODEL, D_MODEL)), half, _full((ATTN_WIDTH, ATTN_WIDTH))],
        out_specs=[_rows(tm, 2 * FFN_HIDDEN), wide, half, half, half, _full((8, D_MODEL))],
        out_shape=[jax.ShapeDtypeStruct((seq, 2 * FFN_HIDDEN), BF16), jax.ShapeDtypeStruct((seq, D_MODEL), F32)]
        + [jax.ShapeDtypeStruct((seq, ATTN_WIDTH), F32)] * 3 + [jax.ShapeDtypeStruct((8, D_MODEL), F32)],
        compiler_params=_cparams("arbitrary"),
    )(dh2, w_down, g, up, w_gu4, h1, norm2_w, w_out, ya, hsum)


def attn_bwd(q, k, v, dy, lse, delta, sibling_grads=()):
    seq = q.shape[0]
    cur, prev = _attn_specs()
    whole = pl.BlockSpec((seq, LANES), lambda hp, j: (0, hp))
    n_g = len(sibling_grads)
    n_hp, n_steps = ATTN_WIDTH // LANES, seq // ATTN_STEP_ROWS

    def body(*refs):
        q_ref, dy_ref, lse_ref, dl_ref, kc_ref, vc_ref, kp_ref, vp_ref = refs[:8]
        dq_ref, dk_ref, dv_ref = refs[8 + n_g:11 + n_g]
        first_step = pl.program_id(1) == 0
        base = pl.program_id(1) * ATTN_STEP_ROWS
        masks = _head_masks()
        if n_g:
            start, finish = _sibling_exchange_phases(
                refs[8:8 + n_g], refs[11 + n_g:11 + 2 * n_g], *refs[11 + 2 * n_g:])
            pl.when((pl.program_id(0) == 0) & first_step)(start)

        def block(t, r, span, per_step, dilation, add):
            rows = _strided_rows(r + span * t, dilation)
            at_edge = t == 0
            q2, dy2 = (q_ref[rows, :] * ATTN_SCALE).astype(BF16), dy_ref[rows, :].astype(BF16)
            lse2, dl2 = lse_ref[rows, :], dl_ref[rows, :]
            kp = _block_before(kp_ref, kc_ref, t, r, span, dilation, per_step)
            vp = _block_before(vp_ref, vc_ref, t, r, span, dilation, per_step)
            k2 = jnp.concatenate([kp, kc_ref[rows, :]], axis=0).astype(BF16)
            v2 = jnp.concatenate([vp, vc_ref[rows, :]], axis=0).astype(BF16)
            valid = _window_valid(first_step & at_edge)
            zero = jnp.zeros_like(q2)
            qms, dyms, ps, dss, kms = [], [], [], [], []
            for h, mh in enumerate(masks):
                c0 = h * HEAD_DIM
                qm, dym = jnp.where(mh, q2, zero), jnp.where(mh, dy2, zero)
                s = _dot_nt(qm, k2)
                p = jnp.where(valid, jnp.exp(s - lse2[:, c0:c0 + 1]), 0.0)
                dp = _dot_nt(dym, v2)
                dss.append((p * (dp - dl2[:, c0:c0 + 1])).astype(BF16))
                ps.append(p.astype(BF16))
                qms.append(qm)
                dyms.append(dym)
                kms.append(jnp.where(mh, k2, jnp.zeros_like(k2)))
            dq = _dot(jnp.concatenate(dss, axis=1), jnp.concatenate(kms, axis=0)) * ATTN_SCALE
            p_all, ds_all = jnp.concatenate(ps, axis=0), jnp.concatenate(dss, axis=0)
            dy_all, q_all = jnp.concatenate(dyms, axis=0), jnp.concatenate(qms, axis=0)
            if dilation < ATTN_TRANSPOSED_BELOW:
                dv_full, dk_full = _dot_tn(dy_all, p_all).T, _dot_tn(q_all, ds_all).T
            else:
                dv_full, dk_full = _dot_tn(p_all, dy_all), _dot_tn(ds_all, q_all)
            here = _strided_rows(base + r + span * t, dilation)
            if add:
                dq_ref[rows, :] += dq
                dk_ref[here, :] += dk_full[ATTN_BLOCK:]
                dv_ref[here, :] += dv_full[ATTN_BLOCK:]
            else:
                dq_ref[rows, :] = dq
                dk_ref[here, :] = dk_full[ATTN_BLOCK:]
                dv_ref[here, :] = dv_full[ATTN_BLOCK:]
            back = _strided_rows(jnp.maximum(base + r + span * t - span, r), dilation)
            dk_ref[back, :] += dk_full[:ATTN_BLOCK]
            dv_ref[back, :] += dv_full[:ATTN_BLOCK]

        for index, (_, dilation) in enumerate(DILATED_PAIRS):
            _for_each_block(dilation, ATTN_BWD_UNROLL,
                            functools.partial(block, dilation=dilation, add=index > 0))
        if n_g:
            pl.when((pl.program_id(0) == n_hp - 1) & (pl.program_id(1) == n_steps - 1))(finish)

    anywhere = pl.BlockSpec(memory_space=pl.ANY)
    return pl.pallas_call(
        body, name="attn_bwd", grid=(n_hp, n_steps),
        in_specs=[cur] * 6 + [prev, prev] + [anywhere] * n_g,
        out_specs=[cur, whole, whole] + [anywhere] * n_g,
        out_shape=[jax.ShapeDtypeStruct((seq, ATTN_WIDTH), F32)] * 3 + _sibling_exchange_shapes(sibling_grads),
        scratch_shapes=_sibling_exchange_scratch(n_g) if n_g else [],
        compiler_params=_cparams("arbitrary", "arbitrary"),
    )(q, dy, lse, delta, k, v, k, v, *sibling_grads)


def hgrn_bwd(hg, lb_logits, hnw, o_pre, st0, dyb, chip_sums=()):
    seq = hg.shape[0]
    nblk = seq // HGRN_ROWS
    n_chunks = HGRN_ROWS // HGRN_CHUNK
    w = HGRN_WIDTH
    n_s = len(chip_sums)

    def body(*refs):
        hg_ref, lbl_ref, hnw_ref, o_ref, st0_ref, dyb_ref = refs[:6]
        dhg_ref, acc_ref = refs[6 + n_s:8 + n_s]
        dst_scr = refs[8 + 2 * n_s]
        step = pl.program_id(0)
        if n_s:
            start, finish = _chip_exchange_phases(refs[6:6 + n_s], refs[8 + n_s:8 + 2 * n_s], *refs[9 + 2 * n_s:])
            pl.when(step == 0)(start)

        @pl.when(step == 0)
        def _():
            dst_scr[...] = jnp.zeros_like(dst_scr)
            acc_ref[...] = jnp.zeros_like(acc_ref)

        hg_v = hg_ref[...]
        p = _hgrn_prep(hg_v, lbl_ref[...])
        tril, block = _chunk_masks()
        vv = hg_v[:, 2 * w:3 * w].astype(BF16)
        hnw_v = hnw_ref[...]
        gb = p["gb"]
        sgg = _sigmoid(gb)
        silu_g = gb * sgg
        dyb_v = dyb_ref[...]
        o_v = o_ref[...]

        d_on = dyb_v * hnw_v * silu_g
        on_parts, do_parts = [], []
        for h in range(HGRN_HEADS):
            sl = slice(h * LANES, (h + 1) * LANES)
            rs = _rms(o_v[:, sl])
            on = o_v[:, sl] * rs
            on_parts.append(on)
            do_parts.append(rs * (d_on[:, sl] - on * jnp.mean(d_on[:, sl] * on, axis=-1, keepdims=True)))
        on_all = jnp.concatenate(on_parts, axis=1)
        dgb = dyb_v * on_all * hnw_v * (sgg * (1.0 + gb * (1.0 - sgg)))
        acc_ref[0:1, :] += jnp.sum(dyb_v * on_all * silu_g, axis=0, keepdims=True)

        dqf_parts, dkk_parts, db_parts, dv_parts, dbt_parts, dkbkb_parts = [], [], [], [], [], []
        for h in range(HGRN_HEADS):
            sl = slice(h * LANES, (h + 1) * LANES)
            v_h = vv[:, sl]
            do_h = do_parts[h].astype(BF16)
            qi, ki, qs, kb = p["qi"][:, sl], p["ki"][:, sl], p["qs"][:, sl], p["kb"][:, sl]
            qi_b, ki_b = qi.astype(BF16), ki.astype(BF16)
            kb_cat = _spread(kb.astype(BF16), block)
            qs_cat = _spread(qs.astype(BF16), block)
            upd = _dot_tn(v_h, kb_cat)
            st = st0_ref[h]
            st_parts = []
            for n in range(n_chunks):
                st_parts.append(st)
                decay = p["e_tot"][n * HGRN_CHUNK:n * HGRN_CHUNK + 1, sl]
                st = st * decay + upd[:, n * LANES:(n + 1) * LANES]
            st_cat = jnp.concatenate([s_.astype(BF16) for s_ in st_parts], axis=1)
            wgt = _dot_tn(do_h, qs_cat)
            dst = dst_scr[h]
            dst_parts = [None] * n_chunks
            dbt_rows = [None] * n_chunks
            for n in reversed(range(n_chunks)):
                dst_parts[n] = dst.astype(BF16)
                decay = p["e_tot"][n * HGRN_CHUNK:n * HGRN_CHUNK + 1, sl]
                dbt_rows[n] = jnp.sum(dst * st_parts[n], axis=0, keepdims=True) * decay
                dst = dst * decay + wgt[:, n * LANES:(n + 1) * LANES]
            dst_scr[h] = dst
            dst_cat = jnp.concatenate(dst_parts, axis=1)
            dqs = _fold(_dot(do_h, st_cat), block)
            dkb = _fold(_dot(v_h, dst_cat), block)
            dv_state = _dot_nt(kb_cat, dst_cat)
            a = jnp.where(tril, _dot_nt(qi_b, ki_b), 0.0).astype(BF16)
            da = jnp.where(tril, _dot_nt(do_h, v_h), 0.0).astype(BF16)
            dv_parts.append(_dot_tn(a, do_h) + dv_state)
            dqi = _dot(da, ki_b)
            dki = _dot_tn(da, qi_b)
            dqf_parts.append(dqi * p["e_iq"][:, sl] + dqs * p["e_b"][:, sl])
            dkk_parts.append(dki * p["e_ik"][:, sl] + dkb * p["e_bb"][:, sl])
            dkbkb = dkb * kb
            db_parts.append(dqi * qi - dki * ki + dqs * qs - dkbkb)
            dkbkb_parts.append(dkbkb)
            dbt_parts.append(_chunk_rows(dbt_rows))

        cat = lambda parts: jnp.concatenate(parts, axis=1)
        dlogf = (_chunk_cumsum(cat(db_parts), reverse=True)
                 + _chunk_row(_chunk_cumsum(cat(dkbkb_parts)), HGRN_CHUNK - 1) + cat(dbt_parts))
        sq, qb = p["sq"], hg_v[:, :w]
        dqb = cat(dqf_parts) * (sq * (1.0 + qb * (1.0 - sq)))
        df = dlogf / p["f"] - cat(dkk_parts)
        sg, lb = p["sg"], p["lb"]
        dfb = df * (1.0 - lb) * sg * (1.0 - sg)
        acc_ref[1:2, :] += jnp.sum(df * (1.0 - sg), axis=0, keepdims=True)
        dhg_ref[...] = jnp.concatenate([dqb, dfb, cat(dv_parts), dgb], axis=1).astype(BF16)
        if n_s:
            pl.when(step == nblk - 1)(finish)

    rev = lambda i: (nblk - 1 - i, 0)
    anywhere = pl.BlockSpec(memory_space=pl.ANY)
    return pl.pallas_call(
        body, name="hgrn_bwd", grid=(nblk,),
        in_specs=[pl.BlockSpec((HGRN_ROWS, 4 * w), rev), _full((2, w)), _full((1, w)),
                  pl.BlockSpec((HGRN_ROWS, w), rev),
                  pl.BlockSpec((None, HGRN_HEADS, LANES, LANES), lambda i: (nblk - 1 - i, 0, 0, 0)),
                  pl.BlockSpec((HGRN_ROWS, w), rev)] + [anywhere] * n_s,
        out_specs=[pl.BlockSpec((HGRN_ROWS, 4 * w), rev), _full((8, w))] + [anywhere] * n_s,
        out_shape=[jax.ShapeDtypeStruct((seq, 4 * w), BF16), jax.ShapeDtypeStruct((8, w), F32)]
        + [jax.ShapeDtypeStruct(s.shape, s.dtype) for s in chip_sums],
        scratch_shapes=[pltpu.VMEM((HGRN_HEADS, LANES, LANES), F32)] + (_chip_exchange_scratch(n_s) if n_s else []),
        compiler_params=_cparams("arbitrary"),
    )(hg, lb_logits, hnw, o_pre, st0, dyb, *chip_sums)


def in_bwd(dq, dk, dv, dhg, cos_t, sin_t, w_in4, x, norm1_w, dh1):
    seq = x.shape[0]
    tm = ROW_TILE
    cw = w_in4.shape[2]

    def body(dq_ref, dk_ref, dv_ref, dhg_ref, cos_ref, sin_ref, w_ref,
             x_ref, nw_ref, dh1_ref, dproj_ref, dx_ref, acc_ref):
        @pl.when(pl.program_id(0) == 0)
        def _():
            acc_ref[...] = jnp.zeros_like(acc_ref)

        cos, sin = cos_ref[...], sin_ref[...]
        dqa = _rotary_bwd(dq_ref[...], cos, sin)
        dka = _rotary_bwd(dk_ref[...], cos, sin)
        dproj = jnp.concatenate(
            [jnp.concatenate([dqa, dka, dv_ref[...]], axis=1).astype(BF16), dhg_ref[...]], axis=1)
        dproj_ref[...] = dproj
        du = _dot_nt(dproj[:, :cw], w_ref[0])
        for j in range(1, N_CHIPS):
            du = du + _dot_nt(dproj[:, j * cw:(j + 1) * cw], w_ref[j])
        xv = x_ref[...]
        r1 = _rms(xv)
        nx = xv * r1
        acc_ref[0:1, :] += jnp.sum(du * nx, axis=0, keepdims=True)
        dn = du * nw_ref[...]
        dx_ref[...] = dh1_ref[...] + r1 * (dn - nx * jnp.mean(dn * nx, axis=-1, keepdims=True))

    half = _rows(tm, ATTN_WIDTH)
    wide = _rows(tm, D_MODEL)
    return pl.pallas_call(
        body, name="in_bwd", grid=(seq // tm,),
        in_specs=[half] * 3 + [_rows(tm, 4 * HGRN_WIDTH), _rows(tm, LANES), _rows(tm, LANES),
                               _weight((N_CHIPS, D_MODEL, cw)), wide, _full((1, D_MODEL)), wide],
        out_specs=[_rows(tm, IN_PROJ_WIDTH), wide, _full((8, D_MODEL))],
        out_shape=[jax.ShapeDtypeStruct((seq, IN_PROJ_WIDTH), BF16), jax.ShapeDtypeStruct((seq, D_MODEL), F32),
                   jax.ShapeDtypeStruct((8, D_MODEL), F32)],
        compiler_params=_cparams("arbitrary"),
    )(dq, dk, dv, dhg, cos_t, sin_t, w_in4, x, norm1_w, dh1)


def weight_grad(a, b, col_block, name, group=1, small_pack=None):
    seq, kdim = a.shape
    ndim = b.shape[1]
    nj = ndim // col_block
    tk = min(1024, seq)
    hosting = small_pack is not None
    n_j, n_t = nj // group, seq // tk

    def body(*refs):
        a_ref, b_ref = refs[:2]
        o_ref = refs[3] if hosting else refs[2]
        if hosting:
            start, finish = _pack_gather_phases(refs[2], refs[4], *refs[5:])
            pl.when((pl.program_id(0) == 0) & (pl.program_id(1) == 0))(start)

        @pl.when(pl.program_id(1) == 0)
        def _():
            o_ref[...] = jnp.zeros_like(o_ref)

        acc = _dot_tn(a_ref[...].astype(BF16), b_ref[...].astype(BF16))
        for i in range(group):
            o_ref[i] += acc[:, i * col_block:(i + 1) * col_block]
        if hosting:
            pl.when((pl.program_id(0) == n_j - 1) & (pl.program_id(1) == n_t - 1))(finish)

    anywhere = pl.BlockSpec(memory_space=pl.ANY)
    out = pl.pallas_call(
        body, name=name, grid=(n_j, n_t),
        in_specs=[pl.BlockSpec((tk, kdim), lambda j, t: (t, 0)),
                  pl.BlockSpec((tk, group * col_block), lambda j, t: (t, j))] + [anywhere] * hosting,
        out_specs=[pl.BlockSpec((group, kdim, col_block), lambda j, t: (j, 0, 0))] + [anywhere] * hosting,
        out_shape=[jax.ShapeDtypeStruct((nj, kdim, col_block), F32)]
        + ([jax.ShapeDtypeStruct((N_DEV,) + small_pack.shape, F32)] if hosting else []),
        scratch_shapes=[pltpu.SemaphoreType.DMA((N_DEV - 1,)), pltpu.SemaphoreType.DMA((N_DEV - 1,)),
                        pltpu.SemaphoreType.DMA] if hosting else [],
        compiler_params=_cparams("arbitrary", "arbitrary"),
    )(a, b, *([small_pack] if hosting else []))
    return out if hosting else out[0]


def _sibling_exchange_phases(g_refs, out_refs, send_sems, recv_sems):
    x, y, cc = _mesh_pos()

    def copies():
        return [pltpu.make_async_remote_copy(
            src_ref=g_refs[i].at[j, 1 - cc], dst_ref=out_refs[i].at[j],
            send_sem=send_sems.at[i * N_CHIPS + j], recv_sem=recv_sems.at[i * N_CHIPS + j],
            device_id=(x, y, 1 - cc), device_id_type=MESH_ID)
            for i in range(len(g_refs)) for j in range(N_CHIPS)]

    def start():
        for cp in copies():
            cp.start()

    def finish():
        for cp in copies():
            cp.wait_recv()
        for cp in copies():
            cp.wait_send()

    return start, finish


def _sibling_exchange_scratch(n):
    return [pltpu.SemaphoreType.DMA((n * N_CHIPS,)), pltpu.SemaphoreType.DMA((n * N_CHIPS,))]


def _sibling_exchange_shapes(grads):
    return [jax.ShapeDtypeStruct((N_CHIPS,) + g.shape[2:], g.dtype) for g in grads]


def exchange_with_sibling(grads, name):
    n = len(grads)

    def body(*refs):
        start, finish = _sibling_exchange_phases(refs[:n], refs[n:2 * n], refs[2 * n], refs[2 * n + 1])
        start()
        finish()

    return pl.pallas_call(
        body, name=name,
        in_specs=[pl.BlockSpec(memory_space=pl.ANY)] * n,
        out_specs=[pl.BlockSpec(memory_space=pl.ANY)] * n,
        out_shape=_sibling_exchange_shapes(grads),
        scratch_shapes=_sibling_exchange_scratch(n),
    )(*grads)


def add_own_half(grad, recv, name):
    _, _, r, c = grad.shape
    tr = r // 2 if r % 32 == 0 else r

    def body(cc_ref, g_ref, r_ref, o_ref):
        o_ref[...] = (g_ref[...] + r_ref[...]).astype(BF16)

    grid_spec = pltpu.PrefetchScalarGridSpec(
        num_scalar_prefetch=1, grid=(N_CHIPS, r // tr),
        in_specs=[pl.BlockSpec((None, None, tr, c), lambda j, t, cc: (j, cc[0], t, 0)),
                  pl.BlockSpec((None, tr, c), lambda j, t, cc: (j, t, 0))],
        out_specs=pl.BlockSpec((None, tr, c), lambda j, t, cc: (j, t, 0)))
    cc = lax.axis_index("c").astype(jnp.int32).reshape(1)
    return pl.pallas_call(
        body, name=name, grid_spec=grid_spec,
        out_shape=jax.ShapeDtypeStruct((N_CHIPS, r, c), BF16),
        compiler_params=_cparams("parallel", "parallel"),
    )(cc, grad, recv)


def _chip_exchange_phases(s_refs, out_refs, send_sems, recv_sems):
    n = len(s_refs)
    x, y, cc = _mesh_pos()
    my_chip = 2 * x + y
    chips = [(1 - x, y), (x, 1 - y), (1 - x, 1 - y)]

    def outgoing():
        return [pltpu.make_async_remote_copy(
            src_ref=s_refs[i].at[2 * px + py], dst_ref=out_refs[i].at[my_chip],
            send_sem=send_sems.at[3 * i + j], recv_sem=recv_sems.at[3 * i + j],
            device_id=(px, py, cc), device_id_type=MESH_ID)
            for i in range(n) for j, (px, py) in enumerate(chips)]

    def start():
        for cp in outgoing():
            cp.start()

    def finish():
        for i in range(n):
            for j, (px, py) in enumerate(chips):
                pltpu.make_async_remote_copy(
                    src_ref=s_refs[i].at[my_chip], dst_ref=out_refs[i].at[2 * px + py],
                    send_sem=send_sems.at[3 * i + j], recv_sem=recv_sems.at[3 * i + j],
                    device_id=(px, py, cc), device_id_type=MESH_ID).wait_recv()
        for cp in outgoing():
            cp.wait_send()

    return start, finish


def _chip_exchange_scratch(n):
    return [pltpu.SemaphoreType.DMA((3 * n,)), pltpu.SemaphoreType.DMA((3 * n,))]


def sum_chips(sums, parts, name):
    _, r, c = parts.shape
    tr = r // 2 if r % 32 == 0 else r

    def body(idx_ref, s_ref, p1_ref, p2_ref, p3_ref, o_ref):
        o_ref[...] = ((s_ref[...].astype(F32) + p1_ref[...].astype(F32))
                      + p2_ref[...].astype(F32)) + p3_ref[...].astype(F32)

    def pick(k):
        return pl.BlockSpec((None, tr, c), lambda t, idx: (idx[k], t, 0))

    x, y = lax.axis_index("x"), lax.axis_index("y")
    idx = jnp.stack([2 * x + y, 2 * (1 - x) + y, 2 * x + (1 - y), 2 * (1 - x) + (1 - y)]).astype(jnp.int32)
    grid_spec = pltpu.PrefetchScalarGridSpec(
        num_scalar_prefetch=1, grid=(r // tr,),
        in_specs=[pick(0), pick(1), pick(2), pick(3)],
        out_specs=pl.BlockSpec((tr, c), lambda t, idx: (t, 0)))
    return pl.pallas_call(
        body, name=name, grid_spec=grid_spec,
        out_shape=jax.ShapeDtypeStruct((r, c), F32),
        compiler_params=_cparams("parallel"),
    )(idx, sums, parts, parts, parts)


def share_with_sibling(halves, name):
    n = len(halves)

    def body(*refs):
        h_refs, out_refs = refs[:n], refs[n:2 * n]
        send_sems, recv_sems = refs[2 * n], refs[2 * n + 1]
        x, y, cc = _mesh_pos()
        copies = [pltpu.make_async_remote_copy(
            src_ref=h_refs[i], dst_ref=out_refs[i],
            send_sem=send_sems.at[i], recv_sem=recv_sems.at[i],
            device_id=(x, y, 1 - cc), device_id_type=MESH_ID) for i in range(n)]
        for cp in copies:
            cp.start()
        for cp in copies:
            cp.wait_recv()
        for cp in copies:
            cp.wait_send()

    return pl.pallas_call(
        body, name=name,
        in_specs=[pl.BlockSpec(memory_space=pl.ANY)] * n,
        out_specs=[pl.BlockSpec(memory_space=pl.ANY)] * n,
        out_shape=[jax.ShapeDtypeStruct(h.shape, h.dtype) for h in halves],
        scratch_shapes=[pltpu.SemaphoreType.DMA((n,)), pltpu.SemaphoreType.DMA((n,))],
    )(*halves)


def _adam_update(w, g, m, v):
    m = ADAM_B1 * m + (1.0 - ADAM_B1) * g
    v = ADAM_B2 * v + (1.0 - ADAM_B2) * (g * g)
    m_hat = m / (1.0 - ADAM_B1 ** ADAM_STEP)
    v_hat = v / (1.0 - ADAM_B2 ** ADAM_STEP)
    delta = -ADAM_LR * (m_hat / (jnp.sqrt(v_hat) + ADAM_EPS) + ADAM_WD * w)
    return delta, m, v


ADAMW_STEPS = 8


def adamw(ws, g_mine, g_sibling, ms, vs, name, chip_sums=()):
    n, n_s = len(ws), len(chip_sums)
    per_half = ADAMW_STEPS // 2

    def body(*refs):
        cc_ref = refs[0]
        ins = refs[1:1 + 5 * n]
        outs = refs[1 + 5 * n + n_s:1 + 9 * n + n_s]
        step = pl.program_id(0)
        if n_s:
            start, finish = _chip_exchange_phases(
                refs[1 + 5 * n:1 + 5 * n + n_s], refs[1 + 9 * n + n_s:1 + 9 * n + 2 * n_s],
                *refs[1 + 9 * n + 2 * n_s:])
            pl.when(step == 0)(start)
        mine = (step // per_half) == cc_ref[0]
        for i in range(n):
            w_ref, ga_ref, gb_ref, m_ref, v_ref = ins[5 * i:5 * i + 5]
            g_ref, d_ref, nm_ref, nv_ref = outs[4 * i:4 * i + 4]
            g = jnp.where(mine, ga_ref[...], gb_ref[...])
            g_ref[...] = g
            d, nm, nv = _adam_update(w_ref[...], g, m_ref[...], v_ref[...])
            d_ref[...] = d
            nm_ref[...] = nm
            nv_ref[...] = nv
        if n_s:
            pl.when(step == ADAMW_STEPS - 1)(finish)

    in_specs, out_specs, out_shape, operands = [], [], [], []
    for w, ga, gb, m, v in zip(ws, g_mine, g_sibling, ms, vs):
        r, c = w.shape
        tr = r // ADAMW_STEPS
        full = pl.BlockSpec((tr, c), lambda t, cc: (t, 0))
        part = pl.BlockSpec((tr, c), lambda t, cc: (t % per_half, 0))
        in_specs += [full, part, part, full, full]
        out_specs += [full] * 4
        out_shape += [jax.ShapeDtypeStruct((r, c), F32)] * 4
        operands += [w, ga, gb, m, v]
    anywhere = pl.BlockSpec(memory_space=pl.ANY)
    grid_spec = pltpu.PrefetchScalarGridSpec(
        num_scalar_prefetch=1, grid=(ADAMW_STEPS,),
        in_specs=in_specs + [anywhere] * n_s, out_specs=out_specs + [anywhere] * n_s,
        scratch_shapes=_chip_exchange_scratch(n_s) if n_s else [])
    cc = lax.axis_index("c").astype(jnp.int32).reshape(1)
    res = pl.pallas_call(
        body, name=name, grid_spec=grid_spec,
        out_shape=out_shape + [jax.ShapeDtypeStruct(s.shape, s.dtype) for s in chip_sums],
        compiler_params=_cparams("arbitrary"),
    )(cc, *operands, *chip_sums)
    per_shard = [tuple(res[4 * i:4 * i + 4]) for i in range(n)]
    return (per_shard, list(res[4 * n:])) if n_s else per_shard


def _pack_gather_phases(p_ref, out_ref, send_sems, recv_sems, local_sem):
    x, y, cc = _mesh_pos()
    me = 4 * x + 2 * y + cc
    flips = [(fx, fy, fc) for fx in (0, 1) for fy in (0, 1) for fc in (0, 1)][1:]

    def copy(k, row):
        fx, fy, fc = flips[k]
        return pltpu.make_async_remote_copy(
            src_ref=p_ref, dst_ref=out_ref.at[row],
            send_sem=send_sems.at[k], recv_sem=recv_sems.at[k],
            device_id=(x ^ fx, y ^ fy, cc ^ fc), device_id_type=MESH_ID)

    def local():
        return pltpu.make_async_copy(p_ref, out_ref.at[me], local_sem)

    def start():
        local().start()
        for k in range(len(flips)):
            copy(k, me).start()

    def finish():
        for k, (fx, fy, fc) in enumerate(flips):
            copy(k, 4 * (x ^ fx) + 2 * (y ^ fy) + (cc ^ fc)).wait_recv()
        for k in range(len(flips)):
            copy(k, me).wait_send()
        local().wait()

    return start, finish


def small_update(gathered, wpack, mpack, vpack):
    hw = HGRN_WIDTH

    def body(g_ref, w_ref, m_ref, v_ref, go_ref, d_ref, nm_ref, nv_ref, loss_ref):
        g = g_ref[0]
        for d in range(1, N_DEV):
            g = g + g_ref[d]
        wv = w_ref[...]
        a0, a1 = wv[4:5, :hw], wv[4:5, hw:]
        mx = jnp.maximum(a0, a1)
        e0, e1 = jnp.exp(a0 - mx), jnp.exp(a1 - mx)
        lb = e0 / (e0 + e1)
        dl = g[4:5, :hw] * lb * (1.0 - lb)
        row = lax.broadcasted_iota(jnp.int32, g.shape, 0)
        lb_row = jnp.concatenate([dl, -dl], axis=1)
        grads = jnp.where(row == 4, lb_row, jnp.where(row < 4, g, 0.0))
        go_ref[...] = grads
        d, nm, nv = _adam_update(wv, grads, m_ref[...], v_ref[...])
        d_ref[...] = d
        nm_ref[...] = nm
        nv_ref[...] = nv
        loss_ref[...] = jnp.zeros((8, LANES), F32) + jnp.sum(g[5:6, :])

    vm = pl.BlockSpec(memory_space=pltpu.VMEM)
    return pl.pallas_call(
        body, name="small_update",
        in_specs=[vm] * 4, out_specs=[vm] * 5,
        out_shape=[jax.ShapeDtypeStruct(wpack.shape, F32)] * 4 + [jax.ShapeDtypeStruct((8, LANES), F32)],
    )(gathered, wpack, mpack, vpack)


def _pack_small(n1, n2, fn, hn, lbl):
    z = jnp.zeros((1, D_MODEL - HGRN_WIDTH), F32)
    rows = [n1.reshape(1, D_MODEL), n2.reshape(1, D_MODEL), fn.reshape(1, D_MODEL),
            jnp.concatenate([hn.reshape(1, HGRN_WIDTH), z], axis=1), lbl.reshape(1, 2 * HGRN_WIDTH),
            jnp.zeros((3, D_MODEL), F32)]
    return jnp.concatenate(rows, axis=0)


def _unpack_small(pack):
    return (pack[0:1], pack[4].reshape(2, HGRN_WIDTH), pack[3:4, :HGRN_WIDTH], pack[1:2], pack[2])


def kernel(x, norm1_w, w_in, lb_logits, hgrn_norm_w, w_out, norm2_w, w_gate_up, w_down, final_norm_w, loss_target, m_norm1_w, m_w_in, m_lb_logits, m_hgrn_norm_w, m_w_out, m_norm2_w, m_w_gate_up, m_w_down, m_final_norm_w, v_norm1_w, v_w_in, v_lb_logits, v_hgrn_norm_w, v_w_out, v_norm2_w, v_w_gate_up, v_w_down, v_final_norm_w):
    seq = x.shape[1]
    xs = x.reshape(seq, D_MODEL)
    target = loss_target.reshape(seq, D_MODEL)
    shards = {"w_in": w_in[0], "w_out": w_out[0], "w_gu": w_gate_up[0], "w_down": w_down[0]}

    cast = {k: cast_bf16(w, "cast_" + k) for k, w in shards.items()}
    w_in4 = allgather_halves(cast["w_in"], "gather_w_in").reshape(N_CHIPS, D_MODEL, -1)

    cos_t, sin_t = _rope_tables(seq)
    fw = final_norm_w.reshape(1, D_MODEL)

    qr, kr, va, hg, u, g_out, g_down = in_proj(
        xs, norm1_w, w_in4, cos_t, sin_t, [cast["w_out"], cast["w_down"]])
    ya, lse = attn_fwd(qr, kr, va)
    yb, o_pre, st0, g_gu = hgrn_fwd(hg, lb_logits, hgrn_norm_w, [cast["w_gu"]])
    w_out_f = g_out.reshape(D_MODEL, D_MODEL)
    w_gu4 = g_gu.reshape(N_CHIPS, D_MODEL, -1)
    w_down_f = g_down.reshape(FFN_HIDDEN, D_MODEL)
    mixed, h1, u2, g, up, act, dh2, acc_fin = ffn_fwd(
        ya, yb, xs, w_out_f, norm2_w, w_gu4, w_down_f, fw, target)

    cw_in, cw_gu = w_in4.shape[2], w_gu4.shape[2]
    dgu, dh1, dya, dyb, delta, acc_n2 = ffn_bwd(dh2, w_down_f, g, up, w_gu4, h1, norm2_w, w_out_f, ya)
    early = [
        weight_grad(mixed, dh1, D_MODEL, "wgrad_out").reshape(N_CHIPS, 2, D_MODEL // 8, D_MODEL),
        weight_grad(u2, dgu, cw_gu, "wgrad_gu", group=2).reshape(N_CHIPS, 2, D_MODEL // 2, cw_gu),
        weight_grad(act, dh2, D_MODEL, "wgrad_down").reshape(N_CHIPS, 2, FFN_HIDDEN // 8, D_MODEL),
    ]
    early_names = ["out", "gu", "down"]
    dq, dk, dv, *early_recv = attn_bwd(qr, kr, va, dya, lse, delta, early)
    early_sums = [add_own_half(gr, rc, "add_half_" + nm) for gr, rc, nm in zip(early, early_recv, early_names)]
    dhg, acc_hg, *early_parts = hgrn_bwd(hg, lb_logits, hgrn_norm_w, o_pre, st0, dyb, early_sums)
    dproj, dx, acc_n1 = in_bwd(dq, dk, dv, dhg, cos_t, sin_t, w_in4, xs, norm1_w, dh1)
    z512 = jnp.zeros((1, D_MODEL - HGRN_WIDTH), F32)
    gpack = jnp.concatenate([
        acc_n1[0:1], acc_n2[0:1], acc_fin[0:1],
        jnp.concatenate([acc_hg[0:1], z512], axis=1), jnp.concatenate([acc_hg[1:2], z512], axis=1),
        acc_fin[1:2], jnp.zeros((2, D_MODEL), F32)], axis=0)
    g_in, gathered_packs = weight_grad(u, dproj, cw_in, "wgrad_in", group=2, small_pack=gpack)
    late = [g_in.reshape(N_CHIPS, 2, D_MODEL // 2, cw_in)]
    late_recv = exchange_with_sibling(late, "grad_exchange_sibling_late")
    late_sums = [add_own_half(late[0], late_recv[0], "add_half_in")]

    early_halves = [sum_chips(s, p, "sum_chips_" + nm) for s, p, nm in zip(early_sums, early_parts, early_names)]
    early_others = share_with_sibling(early_halves, "grad_share_sibling_early")
    early_keys = ["w_out", "w_gu", "w_down"]
    moments = {"w_in": (m_w_in, v_w_in), "w_out": (m_w_out, v_w_out),
               "w_gu": (m_w_gate_up, v_w_gate_up), "w_down": (m_w_down, v_w_down)}
    early_updates, late_parts = adamw(
        [shards[k] for k in early_keys], early_halves, early_others,
        [moments[k][0][0] for k in early_keys], [moments[k][1][0] for k in early_keys],
        "adamw_early", chip_sums=late_sums)
    late_halves = [sum_chips(late_sums[0], late_parts[0], "sum_chips_in")]
    late_others = share_with_sibling(late_halves, "grad_share_sibling_late")
    late_updates = adamw([shards["w_in"]], late_halves, late_others,
                         [moments["w_in"][0][0]], [moments["w_in"][1][0]], "adamw_in")
    big = {k: tuple(t[None] for t in upd) for k, upd in zip(early_keys + ["w_in"], early_updates + late_updates)}

    wpack = _pack_small(norm1_w, norm2_w, final_norm_w, hgrn_norm_w, lb_logits)
    mpack = _pack_small(m_norm1_w, m_norm2_w, m_final_norm_w, m_hgrn_norm_w, m_lb_logits)
    vpack = _pack_small(v_norm1_w, v_norm2_w, v_final_norm_w, v_hgrn_norm_w, v_lb_logits)
    gs, ds, nms, nvs, loss8 = small_update(gathered_packs, wpack, mpack, vpack)
    loss = loss8[0, 0]

    def assemble(small_pack, idx):
        n1, lbl, hn, n2, fn = _unpack_small(small_pack)
        return (n1, big["w_in"][idx], lbl, hn, big["w_out"][idx], n2, big["w_gu"][idx], big["w_down"][idx], fn)

    return (loss, dx.reshape(x.shape), *assemble(gs, 0), *assemble(ds, 1), *assemble(nms, 2), *assemble(nvs, 3))
```

```python
import functools

import jax
import jax.numpy as jnp
from jax import lax
from jax.experimental import pallas as pl
from jax.experimental.pallas import tpu as pltpu

F32 = jnp.float32
BF16 = jnp.bfloat16

D_MODEL = 1024
ATTN_WIDTH = 512
HEAD_DIM = 64
DILATED_PAIRS = ((128, 1), (512, 4), (2048, 16))
ATTN_BLOCK = 128
ROPE_THETA = 10000.0
HGRN_WIDTH = 512
HGRN_CHUNK = 16
HGRN_HEADS = 4
IN_PROJ_WIDTH = 3584
FFN_HIDDEN = 2816
NORM_EPS = 1e-6
ATTN_SCALE = HEAD_DIM ** -0.5
N_CHIPS = 4
N_DEV = 8

ADAM_LR = 0.001
ADAM_B1 = 0.9
ADAM_B2 = 0.999
ADAM_EPS = 1e-08
ADAM_WD = 0.01
ADAM_STEP = 10

LANES = 128
HGRN_ROWS = 128
HGRN_STEP_BLOCKS = 2
ROW_TILE = 256
ATTN_STEP_ROWS = 2048
ATTN_FWD_UNROLL = 8
ATTN_BWD_UNROLL = 8
ATTN_TRANSPOSED_BELOW = 16
VMEM_LIMIT = 56 * 1024 * 1024
NEG_BIG = -1e30
MESH_ID = pl.DeviceIdType.MESH


def _cparams(*sem):
    return pltpu.CompilerParams(dimension_semantics=tuple(sem), vmem_limit_bytes=VMEM_LIMIT)


def _dot(a, b):
    return jnp.dot(a, b, preferred_element_type=F32)


def _dot_nt(a, b):
    return lax.dot_general(a, b, (((1,), (1,)), ((), ())), preferred_element_type=F32)


def _dot_tn(a, b):
    return lax.dot_general(a, b, (((0,), (0,)), ((), ())), preferred_element_type=F32)


def _sigmoid(x):
    return 1.0 / (1.0 + jnp.exp(-x))


def _full(shape):
    n = len(shape)
    return pl.BlockSpec(shape, lambda *_: (0,) * n)


def _weight(shape):
    n = len(shape)
    return pl.BlockSpec(shape, lambda *_: (0,) * n, pipeline_mode=pl.Buffered(1))


def _rows(tm, width):
    return pl.BlockSpec((tm, width), lambda i: (i, 0))


def _swap32(x):
    lane = lax.broadcasted_iota(jnp.int32, x.shape, 1)
    first = (lane % HEAD_DIM) < (HEAD_DIM // 2)
    return jnp.where(first, pltpu.roll(x, LANES - 32, axis=1), pltpu.roll(x, 32, axis=1))


def _rotary_fwd(x, cos, sin_signed):
    parts = []
    for j in range(x.shape[1] // LANES):
        xc = x[:, j * LANES:(j + 1) * LANES]
        parts.append(xc * cos + _swap32(xc) * sin_signed)
    return jnp.concatenate(parts, axis=1)


def _rotary_bwd(dy, cos, sin_signed):
    parts = []
    for j in range(dy.shape[1] // LANES):
        dc = dy[:, j * LANES:(j + 1) * LANES]
        parts.append(dc * cos + _swap32(dc * sin_signed))
    return jnp.concatenate(parts, axis=1)


def _rope_tables(seq):
    half = HEAD_DIM // 2
    inv_freq = ROPE_THETA ** (-jnp.arange(half, dtype=F32) / half)
    ang = jnp.arange(seq, dtype=F32)[:, None] * inv_freq[None, :]
    cos, sin = jnp.cos(ang), jnp.sin(ang)
    cos_t = jnp.tile(cos, (1, LANES // half))
    sin_t = jnp.tile(jnp.concatenate([-sin, sin], axis=1), (1, LANES // HEAD_DIM))
    return cos_t, sin_t


def cast_bf16(w, name):
    r, c = w.shape
    half = r // 2

    def body(w_ref, o_ref):
        o_ref[...] = w_ref[...].astype(BF16)

    return pl.pallas_call(
        body, name=name, grid=(2,),
        in_specs=[pl.BlockSpec((half, c), lambda i: (i, 0))],
        out_specs=pl.BlockSpec((None, half, c), lambda i: (i, 0, 0)),
        out_shape=jax.ShapeDtypeStruct((2, half, c), BF16),
        compiler_params=_cparams("parallel"),
    )(w)


def _mesh_pos():
    return lax.axis_index("x"), lax.axis_index("y"), lax.axis_index("c")


GATHER_COPIES = 7


def _gather_phases(x_refs, out_refs, send_sems, recv_sems, local_sems):
    n = len(x_refs)
    x, y, cc = _mesh_pos()
    me, sibling = (x, y, cc), (x, y, 1 - cc)
    chips = [(1 - x, y), (x, 1 - y), (1 - x, 1 - y)]

    def rows(i, px, py, pc):
        return out_refs[i].at[4 * px + 2 * py + pc]

    def copy(i, k, block, to, src=None):
        return pltpu.make_async_remote_copy(
            src_ref=rows(i, *block) if src is None else src, dst_ref=rows(i, *block),
            send_sem=send_sems.at[GATHER_COPIES * i + k], recv_sem=recv_sems.at[GATHER_COPIES * i + k],
            device_id=to, device_id_type=MESH_ID)

    def local(i):
        return pltpu.make_async_copy(x_refs[i].at[cc], rows(i, *me), local_sems.at[i])

    def first(i):
        mine = x_refs[i].at[cc]
        return [copy(i, 0, me, sibling, src=mine)] + [
            copy(i, 1 + j, me, (*chip, cc), src=mine) for j, chip in enumerate(chips)]

    def passed(i):
        return [copy(i, 4 + j, (*chip, cc), sibling) for j, chip in enumerate(chips)]

    def start():
        for i in range(n):
            local(i).start()
            for cp in first(i):
                cp.start()

    def forward():
        for i in range(n):
            onward = passed(i)
            for j, chip in enumerate(chips):
                copy(i, 1 + j, (*chip, cc), me).wait_recv()
                onward[j].start()

    def finish():
        for i in range(n):
            copy(i, 0, sibling, me).wait_recv()
            for j, chip in enumerate(chips):
                copy(i, 4 + j, (*chip, 1 - cc), me).wait_recv()
            for cp in first(i) + passed(i):
                cp.wait_send()
            local(i).wait()

    return start, forward, finish


def _gather_scratch(n):
    return [pltpu.SemaphoreType.DMA((GATHER_COPIES * n,)), pltpu.SemaphoreType.DMA((GATHER_COPIES * n,)),
            pltpu.SemaphoreType.DMA((n,))]


def _gathered_shape(halves):
    return jax.ShapeDtypeStruct((N_DEV,) + halves.shape[1:], halves.dtype)


def allgather_halves(halves, name):
    def body(x_ref, out_ref, send_sems, recv_sems, local_sems):
        start, forward, finish = _gather_phases([x_ref], [out_ref], send_sems, recv_sems, local_sems)
        start()
        forward()
        finish()

    return pl.pallas_call(
        body, name=name,
        in_specs=[pl.BlockSpec(memory_space=pl.ANY)],
        out_specs=pl.BlockSpec(memory_space=pl.ANY),
        out_shape=_gathered_shape(halves),
        scratch_shapes=_gather_scratch(1),
    )(halves)


def _rms(x):
    return lax.rsqrt(jnp.mean(x * x, axis=-1, keepdims=True) + NORM_EPS)


def in_proj(x, norm1_w, w_in4, cos_t, sin_t, weight_halves=()):
    seq = x.shape[0]
    tm = ROW_TILE
    cw = w_in4.shape[2]
    n_w = len(weight_halves)
    steps = seq // tm

    def body(*refs):
        x_ref, nw_ref, w_ref, cos_ref, sin_ref = refs[:5]
        q_ref, k_ref, v_ref, hg_ref, u_ref = refs[5 + n_w:10 + n_w]
        step = pl.program_id(0)
        if n_w:
            start, forward, finish = _gather_phases(
                refs[5:5 + n_w], refs[10 + n_w:10 + 2 * n_w], *refs[10 + 2 * n_w:])
            pl.when(step == 0)(start)
            pl.when(step == (3 * steps) // 4)(forward)
        xv = x_ref[...]
        u = ((xv * _rms(xv)) * nw_ref[...]).astype(BF16)
        u_ref[...] = u
        proj = jnp.concatenate([_dot(u, w_ref[j]) for j in range(N_CHIPS)], axis=1)
        cos, sin = cos_ref[...], sin_ref[...]
        a = ATTN_WIDTH
        q_ref[...] = _rotary_fwd(proj[:, :a], cos, sin)
        k_ref[...] = _rotary_fwd(proj[:, a:2 * a], cos, sin)
        v_ref[...] = proj[:, 2 * a:3 * a]
        hg_ref[...] = proj[:, 3 * a:]
        if n_w:
            pl.when(step == steps - 1)(finish)

    anywhere = pl.BlockSpec(memory_space=pl.ANY)
    return pl.pallas_call(
        body, name="in_proj", grid=(steps,),
        in_specs=[_rows(tm, D_MODEL), _full((1, D_MODEL)), _weight((N_CHIPS, D_MODEL, cw)),
                  _rows(tm, LANES), _rows(tm, LANES)] + [anywhere] * n_w,
        out_specs=[_rows(tm, ATTN_WIDTH)] * 3 + [_rows(tm, 4 * HGRN_WIDTH), _rows(tm, D_MODEL)]
        + [anywhere] * n_w,
        out_shape=[jax.ShapeDtypeStruct((seq, ATTN_WIDTH), F32)] * 3
        + [jax.ShapeDtypeStruct((seq, 4 * HGRN_WIDTH), F32), jax.ShapeDtypeStruct((seq, D_MODEL), BF16)]
        + [_gathered_shape(h) for h in weight_halves],
        scratch_shapes=_gather_scratch(n_w) if n_w else [],
        compiler_params=_cparams("arbitrary"),
    )(x, norm1_w, w_in4, cos_t, sin_t, *weight_halves)


def _head_masks():
    lane = lax.broadcasted_iota(jnp.int32, (1, LANES), 1)
    return [(lane // HEAD_DIM) == h for h in range(LANES // HEAD_DIM)]


def _window_valid(no_prev):
    qi = lax.broadcasted_iota(jnp.int32, (ATTN_BLOCK, 2 * ATTN_BLOCK), 0)
    kj = lax.broadcasted_iota(jnp.int32, (ATTN_BLOCK, 2 * ATTN_BLOCK), 1)
    valid = (kj >= qi) & (kj <= qi + ATTN_BLOCK)
    return valid & (jnp.logical_not(no_prev) | (kj >= ATTN_BLOCK))


def _strided_rows(start, dilation):
    if dilation == 1:
        return pl.ds(start, ATTN_BLOCK)
    return pl.ds(start, ATTN_BLOCK, stride=dilation)


def _block_before(edge_ref, cur_ref, t, r, span, dilation, per_step):
    edge = edge_ref[_strided_rows(ATTN_STEP_ROWS - span + r, dilation), :]
    if per_step == 1:
        return edge
    inside = cur_ref[_strided_rows(r + span * jnp.maximum(t - 1, 0), dilation), :]
    return jnp.where(t == 0, edge, inside)


def _attn_specs():
    cur = pl.BlockSpec((ATTN_STEP_ROWS, LANES), lambda hp, j: (j, hp))
    prev = pl.BlockSpec((ATTN_STEP_ROWS, LANES), lambda hp, j: (jnp.maximum(j - 1, 0), hp))
    return cur, prev


def _for_each_block(dilation, unroll, block):
    span = ATTN_BLOCK * dilation
    per_step = ATTN_STEP_ROWS // span

    def trip(it, carry):
        block(it // dilation, it % dilation, span, per_step)
        return carry

    lax.fori_loop(0, per_step * dilation, trip, 0, unroll=unroll)


def attn_fwd(q, k, v):
    seq = q.shape[0]
    cur, prev = _attn_specs()

    def body(q_ref, kc_ref, vc_ref, kp_ref, vp_ref, y_ref, lse_ref):
        first_step = pl.program_id(1) == 0
        masks = _head_masks()
        for index, (_, dilation) in enumerate(DILATED_PAIRS):
            def block(t, r, span, per_step, dilation=dilation, merge=index > 0):
                rows = _strided_rows(r + span * t, dilation)
                q2 = (q_ref[rows, :] * ATTN_SCALE).astype(BF16)
                kp = _block_before(kp_ref, kc_ref, t, r, span, dilation, per_step)
                vp = _block_before(vp_ref, vc_ref, t, r, span, dilation, per_step)
                k2 = jnp.concatenate([kp, kc_ref[rows, :]], axis=0).astype(BF16)
                v2 = jnp.concatenate([vp, vc_ref[rows, :]], axis=0).astype(BF16)
                valid = _window_valid(first_step & (t == 0))
                o_acc = jnp.zeros((ATTN_BLOCK, LANES), F32)
                l_acc = jnp.zeros((ATTN_BLOCK, LANES), F32)
                for mh in masks:
                    qm = jnp.where(mh, q2, jnp.zeros_like(q2))
                    s = jnp.where(valid, _dot_nt(qm, k2), NEG_BIG)
                    m = jnp.max(s, axis=-1, keepdims=True)
                    p = jnp.exp(s - m)
                    l = jnp.sum(p, axis=-1, keepdims=True)
                    o = _dot(p.astype(BF16), v2) / l
                    o_acc = jnp.where(mh, o, o_acc)
                    l_acc = jnp.where(mh, m + jnp.log(l), l_acc)
                if merge:
                    y_old, l_old = y_ref[rows, :], lse_ref[rows, :]
                    mx = jnp.maximum(l_old, l_acc)
                    e_old, e_new = jnp.exp(l_old - mx), jnp.exp(l_acc - mx)
                    den = e_old + e_new
                    o_acc = (y_old * e_old + o_acc * e_new) / den
                    l_acc = mx + jnp.log(den)
                y_ref[rows, :] = o_acc
                lse_ref[rows, :] = l_acc

            _for_each_block(dilation, ATTN_FWD_UNROLL, block)

    return pl.pallas_call(
        body, name="attn_fwd", grid=(ATTN_WIDTH // LANES, seq // ATTN_STEP_ROWS),
        in_specs=[cur, cur, cur, prev, prev],
        out_specs=[cur, cur],
        out_shape=[jax.ShapeDtypeStruct((seq, ATTN_WIDTH), F32)] * 2,
        compiler_params=_cparams("parallel", "parallel"),
    )(q, k, v, k, v)


def _chunk_cumsum(x, reverse=False):
    rc = lax.broadcasted_iota(jnp.int32, x.shape, 0) % HGRN_CHUNK
    sh = 1
    while sh < HGRN_CHUNK:
        if reverse:
            x = x + jnp.where(rc + sh < HGRN_CHUNK, pltpu.roll(x, x.shape[0] - sh, axis=0), 0.0)
        else:
            x = x + jnp.where(rc >= sh, pltpu.roll(x, sh, axis=0), 0.0)
        sh *= 2
    return x


def _chunk_row(x, row):
    return _chunk_rows([x[n * HGRN_CHUNK + row:n * HGRN_CHUNK + row + 1, :]
                        for n in range(x.shape[0] // HGRN_CHUNK)])


def _chunk_rows(rows):
    return jnp.concatenate([jnp.broadcast_to(r, (HGRN_CHUNK, r.shape[1])) for r in rows], axis=0)


def _hgrn_prep(hg, lbl):
    w = HGRN_WIDTH
    a0, a1 = lbl[0:1, :], lbl[1:2, :]
    mx = jnp.maximum(a0, a1)
    e0, e1 = jnp.exp(a0 - mx), jnp.exp(a1 - mx)
    lb = e0 / (e0 + e1)
    qb, fb, gb = hg[:, :w], hg[:, w:2 * w], hg[:, 3 * w:]
    sg = _sigmoid(fb)
    f = lb + (1.0 - lb) * sg
    b = _chunk_cumsum(jnp.log(f))
    bmid, btot = _chunk_row(b, HGRN_CHUNK // 2 - 1), _chunk_row(b, HGRN_CHUNK - 1)
    sq = _sigmoid(qb)
    p = dict(lb=lb, sg=sg, f=f, kk=1.0 - f, sq=sq, qf=qb * sq, gb=gb,
             e_iq=jnp.exp(b - bmid), e_ik=jnp.exp(bmid - b), e_b=jnp.exp(b),
             e_bb=jnp.exp(btot - b), e_tot=jnp.exp(btot))
    p["qi"] = p["qf"] * p["e_iq"]
    p["ki"] = p["kk"] * p["e_ik"]
    p["qs"] = p["qf"] * p["e_b"]
    p["kb"] = p["kk"] * p["e_bb"]
    return p


def _chunk_masks():
    t = lax.broadcasted_iota(jnp.int32, (HGRN_ROWS, HGRN_ROWS), 0)
    s = lax.broadcasted_iota(jnp.int32, (HGRN_ROWS, HGRN_ROWS), 1)
    tril = ((t // HGRN_CHUNK) == (s // HGRN_CHUNK)) & (s <= t)
    n_chunks = HGRN_ROWS // HGRN_CHUNK
    tt = lax.broadcasted_iota(jnp.int32, (HGRN_ROWS, n_chunks * LANES), 0)
    cc = lax.broadcasted_iota(jnp.int32, (HGRN_ROWS, n_chunks * LANES), 1)
    block = (tt // HGRN_CHUNK) == (cc // LANES)
    return tril, block


def _spread(x, block):
    n_chunks = HGRN_ROWS // HGRN_CHUNK
    return jnp.where(block, jnp.tile(x, (1, n_chunks)), jnp.zeros((), x.dtype))


def _fold(x_full, block):
    n_chunks = HGRN_ROWS // HGRN_CHUNK
    z = jnp.where(block, x_full, 0.0)
    acc = z[:, :LANES]
    for n in range(1, n_chunks):
        acc = acc + z[:, n * LANES:(n + 1) * LANES]
    return acc


def hgrn_fwd(hg, lb_logits, hnw, weight_halves=()):
    seq = hg.shape[0]
    nblk = seq // HGRN_ROWS
    n_steps = nblk // HGRN_STEP_BLOCKS
    step_rows = HGRN_ROWS * HGRN_STEP_BLOCKS
    n_chunks = HGRN_ROWS // HGRN_CHUNK
    n_w = len(weight_halves)

    def body(*refs):
        hg_ref, lbl_ref, hnw_ref = refs[:3]
        w_refs = refs[3:3 + n_w]
        yb_ref, o_ref, st0_ref = refs[3 + n_w:6 + n_w]
        g_refs = refs[6 + n_w:6 + 2 * n_w]
        st_scr = refs[6 + 2 * n_w]
        step = pl.program_id(0)
        if n_w:
            start, forward, finish = _gather_phases(w_refs, g_refs, *refs[7 + 2 * n_w:])
            pl.when(step == 0)(start)
            pl.when(step == (3 * n_steps) // 4)(forward)

        @pl.when(step == 0)
        def _():
            st_scr[...] = jnp.zeros_like(st_scr)

        tril, block = _chunk_masks()
        for sub in range(HGRN_STEP_BLOCKS):
            rows = slice(sub * HGRN_ROWS, (sub + 1) * HGRN_ROWS)
            hg_v = hg_ref[rows, :]
            p = _hgrn_prep(hg_v, lbl_ref[...])
            vv = hg_v[:, 2 * HGRN_WIDTH:3 * HGRN_WIDTH].astype(BF16)
            outs = []
            for h in range(HGRN_HEADS):
                sl = slice(h * LANES, (h + 1) * LANES)
                v_h = vv[:, sl]
                a = jnp.where(tril, _dot_nt(p["qi"][:, sl].astype(BF16), p["ki"][:, sl].astype(BF16)), 0.0)
                o = _dot(a.astype(BF16), v_h)
                upd = _dot_tn(v_h, _spread(p["kb"][:, sl].astype(BF16), block))
                st = st_scr[h]
                st0_ref[sub, h] = st
                parts = []
                for n in range(n_chunks):
                    parts.append(st.astype(BF16))
                    decay = p["e_tot"][n * HGRN_CHUNK:n * HGRN_CHUNK + 1, sl]
                    st = st * decay + upd[:, n * LANES:(n + 1) * LANES]
                st_scr[h] = st
                o = o + _dot_nt(_spread(p["qs"][:, sl].astype(BF16), block), jnp.concatenate(parts, axis=1))
                outs.append(o)
            o_ref[rows, :] = jnp.concatenate(outs, axis=1)
            normed = jnp.concatenate(
                [outs[h] * _rms(outs[h]) for h in range(HGRN_HEADS)], axis=1)
            gb = p["gb"]
            yb_ref[rows, :] = (normed * hnw_ref[...]) * (gb * _sigmoid(gb))
        if n_w:
            pl.when(step == n_steps - 1)(finish)

    anywhere = pl.BlockSpec(memory_space=pl.ANY)
    return pl.pallas_call(
        body, name="hgrn_fwd", grid=(n_steps,),
        in_specs=[_rows(step_rows, 4 * HGRN_WIDTH), _full((2, HGRN_WIDTH)), _full((1, HGRN_WIDTH))]
        + [anywhere] * n_w,
        out_specs=[_rows(step_rows, HGRN_WIDTH), _rows(step_rows, HGRN_WIDTH),
                   pl.BlockSpec((HGRN_STEP_BLOCKS, HGRN_HEADS, LANES, LANES), lambda i: (i, 0, 0, 0))]
        + [anywhere] * n_w,
        out_shape=[jax.ShapeDtypeStruct((seq, HGRN_WIDTH), F32)] * 2
        + [jax.ShapeDtypeStruct((nblk, HGRN_HEADS, LANES, LANES), F32)]
        + [_gathered_shape(h) for h in weight_halves],
        scratch_shapes=[pltpu.VMEM((HGRN_HEADS, LANES, LANES), F32)] + (_gather_scratch(n_w) if n_w else []),
        compiler_params=_cparams("arbitrary"),
    )(hg, lb_logits, hnw, *weight_halves)


def ffn_fwd(ya, yb, x, w_out, norm2_w, w_gu4, w_down, final_w, target):
    seq = x.shape[0]
    tm = ROW_TILE
    cw = w_gu4.shape[2]
    inv_d = 1.0 / D_MODEL

    def body(ya_ref, yb_ref, x_ref, wo_ref, nw_ref, wgu_ref, wd_ref, fw_ref, t_ref,
             mixed_ref, h1_ref, u2_ref, g_ref, up_ref, act_ref, dh2_ref, acc_ref):
        @pl.when(pl.program_id(0) == 0)
        def _():
            acc_ref[...] = jnp.zeros_like(acc_ref)

        mixed = jnp.concatenate([ya_ref[...], yb_ref[...]], axis=1).astype(BF16)
        mixed_ref[...] = mixed
        h1 = x_ref[...] + _dot(mixed, wo_ref[...])
        h1_ref[...] = h1
        u2 = ((h1 * _rms(h1)) * nw_ref[...]).astype(BF16)
        u2_ref[...] = u2
        g = jnp.concatenate([_dot(u2, wgu_ref[0]), _dot(u2, wgu_ref[1])], axis=1)
        up = jnp.concatenate([_dot(u2, wgu_ref[2]), _dot(u2, wgu_ref[3])], axis=1)
        g_ref[...] = g.astype(BF16)
        up_ref[...] = up.astype(BF16)
        act = ((g * _sigmoid(g)) * up).astype(BF16)
        act_ref[...] = act
        h2 = h1 + _dot(act, wd_ref[...])
        rf = _rms(h2)
        n = h2 * rf
        fw = fw_ref[...]
        err = n * fw - t_ref[...]
        dy = err * inv_d
        acc_ref[0:1, :] += jnp.sum(dy * n, axis=0, keepdims=True)
        acc_ref[1:2, :] += (0.5 * inv_d) * jnp.sum(err * err, axis=0, keepdims=True)
        dn = dy * fw
        dh2_ref[...] = rf * (dn - n * jnp.mean(dn * n, axis=-1, keepdims=True))

    half = _rows(tm, ATTN_WIDTH)
    wide = _rows(tm, D_MODEL)
    ffn = _rows(tm, FFN_HIDDEN)
    return pl.pallas_call(
        body, name="ffn_fwd", grid=(seq // tm,),
        in_specs=[half, half, wide, _weight((D_MODEL, D_MODEL)), _full((1, D_MODEL)),
                  _weight((N_CHIPS, D_MODEL, cw)), _weight((FFN_HIDDEN, D_MODEL)), _full((1, D_MODEL)), wide],
        out_specs=[wide, wide, wide, ffn, ffn, ffn, wide, _full((8, D_MODEL))],
        out_shape=[jax.ShapeDtypeStruct((seq, D_MODEL), BF16), jax.ShapeDtypeStruct((seq, D_MODEL), F32),
                   jax.ShapeDtypeStruct((seq, D_MODEL), BF16)]
        + [jax.ShapeDtypeStruct((seq, FFN_HIDDEN), BF16)] * 3
        + [jax.ShapeDtypeStruct((seq, D_MODEL), F32), jax.ShapeDtypeStruct((8, D_MODEL), F32)],
        compiler_params=_cparams("arbitrary"),
    )(ya, yb, x, w_out, norm2_w, w_gu4, w_down, final_w, target)


def _head_sum_matrix():
    i = jnp.arange(ATTN_WIDTH)
    return ((i[:, None] // HEAD_DIM) == (i[None, :] // HEAD_DIM)).astype(BF16)


def ffn_bwd(dh2, w_down, g, up, w_gu4, h1, norm2_w, w_out, ya):
    seq = h1.shape[0]
    tm = ROW_TILE
    cw = w_gu4.shape[2]
    hsum = _head_sum_matrix()

    def body(dh2_ref, wd_ref, g_ref, up_ref, w_ref, h1_ref, nw_ref, wo_ref, ya_ref, hs_ref,
             dgu_ref, dh1_ref, dya_ref, dyb_ref, delta_ref, acc_ref):
        @pl.when(pl.program_id(0) == 0)
        def _():
            acc_ref[...] = jnp.zeros_like(acc_ref)

        dh2_b = dh2_ref[...].astype(BF16)
        du2 = jnp.zeros((tm, D_MODEL), F32)
        for j in range(N_CHIPS // 2):
            dact = _dot_nt(dh2_b, wd_ref[j * cw:(j + 1) * cw, :])
            gv = g_ref[:, j * cw:(j + 1) * cw].astype(F32)
            sg = _sigmoid(gv)
            dg = (dact * up_ref[:, j * cw:(j + 1) * cw].astype(F32) * (sg * (1.0 + gv * (1.0 - sg)))).astype(BF16)
            dup = (dact * (gv * sg)).astype(BF16)
            dgu_ref[:, j * cw:(j + 1) * cw] = dg
            dgu_ref[:, FFN_HIDDEN + j * cw:FFN_HIDDEN + (j + 1) * cw] = dup
            du2 = du2 + _dot_nt(dg, w_ref[j]) + _dot_nt(dup, w_ref[N_CHIPS // 2 + j])
        h1 = h1_ref[...]
        r2 = _rms(h1)
        nh = h1 * r2
        acc_ref[0:1, :] += jnp.sum(du2 * nh, axis=0, keepdims=True)
        dn = du2 * nw_ref[...]
        dh1 = dh2_ref[...] + r2 * (dn - nh * jnp.mean(dn * nh, axis=-1, keepdims=True))
        dh1_ref[...] = dh1
        dmixed = _dot_nt(dh1.astype(BF16), wo_ref[...])
        dya = dmixed[:, :ATTN_WIDTH]
        dya_ref[...] = dya
        dyb_ref[...] = dmixed[:, ATTN_WIDTH:]
        prod = dya * ya_ref[...]
        hi = prod.astype(BF16)
        lo = (prod - hi.astype(F32)).astype(BF16)
        delta_ref[...] = _dot(hi, hs_ref[...]) + _dot(lo, hs_ref[...])

    wide = _rows(tm, D_MODEL)
    half = _rows(tm, ATTN_WIDTH)
    ffn = _rows(tm, FFN_HIDDEN)
    return pl.pallas_call(
        body, name="ffn_bwd", grid=(seq // tm,),
        in_specs=[wide, _weight((FFN_HIDDEN, D_MODEL)), ffn, ffn, _weight((N_CHIPS, D_MODEL, cw)), wide,
                  _full((1, D_MODEL)), _weight((D_MODEL, D_MODEL)), half, _full((ATTN_WIDTH, ATTN_WIDTH))],
        out_specs=[_rows(tm, 2 * FFN_HIDDEN), wide, half, half, half, _full((8, D_MODEL))],
        out_shape=[jax.ShapeDtypeStruct((seq, 2 * FFN_HIDDEN), BF16), jax.ShapeDtypeStruct((seq, D_MODEL), F32)]
        + [jax.ShapeDtypeStruct((seq, ATTN_WIDTH), F32)] * 3 + [jax.ShapeDtypeStruct((8, D_MODEL), F32)],
        compiler_params=_cparams("arbitrary"),
    )(dh2, w_down, g, up, w_gu4, h1, norm2_w, w_out, ya, hsum)


def attn_bwd(q, k, v, dy, lse, delta, sibling_grads=()):
    seq = q.shape[0]
    cur, prev = _attn_specs()
    whole = pl.BlockSpec((seq, LANES), lambda hp, j: (0, hp))
    n_g = len(sibling_grads)
    n_hp, n_steps = ATTN_WIDTH // LANES, seq // ATTN_STEP_ROWS

    def body(*refs):
        q_ref, dy_ref, lse_ref, dl_ref, kc_ref, vc_ref, kp_ref, vp_ref = refs[:8]
        dq_ref, dk_ref, dv_ref = refs[8 + n_g:11 + n_g]
        first_step = pl.program_id(1) == 0
        base = pl.program_id(1) * ATTN_STEP_ROWS
        masks = _head_masks()
        if n_g:
            start, finish = _sibling_exchange_phases(
                refs[8:8 + n_g], refs[11 + n_g:11 + 2 * n_g], *refs[11 + 2 * n_g:])
            pl.when((pl.program_id(0) == 0) & first_step)(start)

        def block(t, r, span, per_step, dilation, add):
            rows = _strided_rows(r + span * t, dilation)
            at_edge = t == 0
            q2, dy2 = (q_ref[rows, :] * ATTN_SCALE).astype(BF16), dy_ref[rows, :].astype(BF16)
            lse2, dl2 = lse_ref[rows, :], dl_ref[rows, :]
            kp = _block_before(kp_ref, kc_ref, t, r, span, dilation, per_step)
            vp = _block_before(vp_ref, vc_ref, t, r, span, dilation, per_step)
            k2 = jnp.concatenate([kp, kc_ref[rows, :]], axis=0).astype(BF16)
            v2 = jnp.concatenate([vp, vc_ref[rows, :]], axis=0).astype(BF16)
            valid = _window_valid(first_step & at_edge)
            zero = jnp.zeros_like(q2)
            qms, dyms, ps, dss, kms = [], [], [], [], []
            for h, mh in enumerate(masks):
                c0 = h * HEAD_DIM
                qm, dym = jnp.where(mh, q2, zero), jnp.where(mh, dy2, zero)
                s = _dot_nt(qm, k2)
                p = jnp.where(valid, jnp.exp(s - lse2[:, c0:c0 + 1]), 0.0)
                dp = _dot_nt(dym, v2)
                dss.append((p * (dp - dl2[:, c0:c0 + 1])).astype(BF16))
                ps.append(p.astype(BF16))
                qms.append(qm)
                dyms.append(dym)
                kms.append(jnp.where(mh, k2, jnp.zeros_like(k2)))
            dq = _dot(jnp.concatenate(dss, axis=1), jnp.concatenate(kms, axis=0)) * ATTN_SCALE
            p_all, ds_all = jnp.concatenate(ps, axis=0), jnp.concatenate(dss, axis=0)
            dy_all, q_all = jnp.concatenate(dyms, axis=0), jnp.concatenate(qms, axis=0)
            if dilation < ATTN_TRANSPOSED_BELOW:
                dv_full, dk_full = _dot_tn(dy_all, p_all).T, _dot_tn(q_all, ds_all).T
            else:
                dv_full, dk_full = _dot_tn(p_all, dy_all), _dot_tn(ds_all, q_all)
            here = _strided_rows(base + r + span * t, dilation)
            if add:
                dq_ref[rows, :] += dq
                dk_ref[here, :] += dk_full[ATTN_BLOCK:]
                dv_ref[here, :] += dv_full[ATTN_BLOCK:]
            else:
                dq_ref[rows, :] = dq
                dk_ref[here, :] = dk_full[ATTN_BLOCK:]
                dv_ref[here, :] = dv_full[ATTN_BLOCK:]
            back = _strided_rows(jnp.maximum(base + r + span * t - span, r), dilation)
            dk_ref[back, :] += dk_full[:ATTN_BLOCK]
            dv_ref[back, :] += dv_full[:ATTN_BLOCK]

        for index, (_, dilation) in enumerate(DILATED_PAIRS):
            _for_each_block(dilation, ATTN_BWD_UNROLL,
                            functools.partial(block, dilation=dilation, add=index > 0))
        if n_g:
            pl.when((pl.program_id(0) == n_hp - 1) & (pl.program_id(1) == n_steps - 1))(finish)

    anywhere = pl.BlockSpec(memory_space=pl.ANY)
    return pl.pallas_call(
        body, name="attn_bwd", grid=(n_hp, n_steps),
        in_specs=[cur] * 6 + [prev, prev] + [anywhere] * n_g,
        out_specs=[cur, whole, whole] + [anywhere] * n_g,
        out_shape=[jax.ShapeDtypeStruct((seq, ATTN_WIDTH), F32)] * 3 + _sibling_exchange_shapes(sibling_grads),
        scratch_shapes=_sibling_exchange_scratch(n_g) if n_g else [],
        compiler_params=_cparams("arbitrary", "arbitrary"),
    )(q, dy, lse, delta, k, v, k, v, *sibling_grads)


def hgrn_bwd(hg, lb_logits, hnw, o_pre, st0, dyb, chip_sums=()):
    seq = hg.shape[0]
    step_rows = HGRN_ROWS * HGRN_STEP_BLOCKS
    n_steps = seq // step_rows
    n_chunks = HGRN_ROWS // HGRN_CHUNK
    w = HGRN_WIDTH
    n_s = len(chip_sums)

    def body(*refs):
        hg_ref, lbl_ref, hnw_ref, o_ref, st0_ref, dyb_ref = refs[:6]
        dhg_ref, acc_ref = refs[6 + n_s:8 + n_s]
        dst_scr = refs[8 + 2 * n_s]
        step = pl.program_id(0)
        if n_s:
            start, finish = _chip_exchange_phases(refs[6:6 + n_s], refs[8 + n_s:8 + 2 * n_s], *refs[9 + 2 * n_s:])
            pl.when(step == 0)(start)

        @pl.when(step == 0)
        def _():
            dst_scr[...] = jnp.zeros_like(dst_scr)
            acc_ref[...] = jnp.zeros_like(acc_ref)

        tril, block = _chunk_masks()
        for sub in reversed(range(HGRN_STEP_BLOCKS)):
            rows = slice(sub * HGRN_ROWS, (sub + 1) * HGRN_ROWS)
            hg_v = hg_ref[rows, :]
            p = _hgrn_prep(hg_v, lbl_ref[...])
            vv = hg_v[:, 2 * w:3 * w].astype(BF16)
            hnw_v = hnw_ref[...]
            gb = p["gb"]
            sgg = _sigmoid(gb)
            silu_g = gb * sgg
            dyb_v = dyb_ref[rows, :]
            o_v = o_ref[rows, :]

            d_on = dyb_v * hnw_v * silu_g
            on_parts, do_parts = [], []
            for h in range(HGRN_HEADS):
                sl = slice(h * LANES, (h + 1) * LANES)
                rs = _rms(o_v[:, sl])
                on = o_v[:, sl] * rs
                on_parts.append(on)
                do_parts.append(rs * (d_on[:, sl] - on * jnp.mean(d_on[:, sl] * on, axis=-1, keepdims=True)))
            on_all = jnp.concatenate(on_parts, axis=1)
            dgb = dyb_v * on_all * hnw_v * (sgg * (1.0 + gb * (1.0 - sgg)))
            acc_ref[0:1, :] += jnp.sum(dyb_v * on_all * silu_g, axis=0, keepdims=True)

            dqf_parts, dkk_parts, db_parts, dv_parts, dbt_parts, dkbkb_parts = [], [], [], [], [], []
            for h in range(HGRN_HEADS):
                sl = slice(h * LANES, (h + 1) * LANES)
                v_h = vv[:, sl]
                do_h = do_parts[h].astype(BF16)
                qi, ki, qs, kb = p["qi"][:, sl], p["ki"][:, sl], p["qs"][:, sl], p["kb"][:, sl]
                qi_b, ki_b = qi.astype(BF16), ki.astype(BF16)
                kb_cat = _spread(kb.astype(BF16), block)
                qs_cat = _spread(qs.astype(BF16), block)
                upd = _dot_tn(v_h, kb_cat)
                st = st0_ref[sub, h]
                st_parts = []
                for n in range(n_chunks):
                    st_parts.append(st)
                    decay = p["e_tot"][n * HGRN_CHUNK:n * HGRN_CHUNK + 1, sl]
                    st = st * decay + upd[:, n * LANES:(n + 1) * LANES]
                st_cat = jnp.concatenate([s_.astype(BF16) for s_ in st_parts], axis=1)
                wgt = _dot_tn(do_h, qs_cat)
                dst = dst_scr[h]
                dst_parts = [None] * n_chunks
                dbt_rows = [None] * n_chunks
                for n in reversed(range(n_chunks)):
                    dst_parts[n] = dst.astype(BF16)
                    decay = p["e_tot"][n * HGRN_CHUNK:n * HGRN_CHUNK + 1, sl]
                    dbt_rows[n] = jnp.sum(dst * st_parts[n], axis=0, keepdims=True) * decay
                    dst = dst * decay + wgt[:, n * LANES:(n + 1) * LANES]
                dst_scr[h] = dst
                dst_cat = jnp.concatenate(dst_parts, axis=1)
                dqs = _fold(_dot(do_h, st_cat), block)
                dkb = _fold(_dot(v_h, dst_cat), block)
                dv_state = _dot_nt(kb_cat, dst_cat)
                a = jnp.where(tril, _dot_nt(qi_b, ki_b), 0.0).astype(BF16)
                da = jnp.where(tril, _dot_nt(do_h, v_h), 0.0).astype(BF16)
                dv_parts.append(_dot_tn(a, do_h) + dv_state)
                dqi = _dot(da, ki_b)
                dki = _dot_tn(da, qi_b)
                dqf_parts.append(dqi * p["e_iq"][:, sl] + dqs * p["e_b"][:, sl])
                dkk_parts.append(dki * p["e_ik"][:, sl] + dkb * p["e_bb"][:, sl])
                dkbkb = dkb * kb
                db_parts.append(dqi * qi - dki * ki + dqs * qs - dkbkb)
                dkbkb_parts.append(dkbkb)
                dbt_parts.append(_chunk_rows(dbt_rows))

            cat = lambda parts: jnp.concatenate(parts, axis=1)
            dlogf = (_chunk_cumsum(cat(db_parts), reverse=True)
                     + _chunk_row(_chunk_cumsum(cat(dkbkb_parts)), HGRN_CHUNK - 1) + cat(dbt_parts))
            sq, qb = p["sq"], hg_v[:, :w]
            dqb = cat(dqf_parts) * (sq * (1.0 + qb * (1.0 - sq)))
            df = dlogf / p["f"] - cat(dkk_parts)
            sg, lb = p["sg"], p["lb"]
            dfb = df * (1.0 - lb) * sg * (1.0 - sg)
            acc_ref[1:2, :] += jnp.sum(df * (1.0 - sg), axis=0, keepdims=True)
            dhg_ref[rows, :] = jnp.concatenate([dqb, dfb, cat(dv_parts), dgb], axis=1).astype(BF16)
        if n_s:
            pl.when(step == n_steps - 1)(finish)

    rev = lambda i: (n_steps - 1 - i, 0)
    anywhere = pl.BlockSpec(memory_space=pl.ANY)
    return pl.pallas_call(
        body, name="hgrn_bwd", grid=(n_steps,),
        in_specs=[pl.BlockSpec((step_rows, 4 * w), rev), _full((2, w)), _full((1, w)),
                  pl.BlockSpec((step_rows, w), rev),
                  pl.BlockSpec((HGRN_STEP_BLOCKS, HGRN_HEADS, LANES, LANES), lambda i: (n_steps - 1 - i, 0, 0, 0)),
                  pl.BlockSpec((step_rows, w), rev)] + [anywhere] * n_s,
        out_specs=[pl.BlockSpec((step_rows, 4 * w), rev), _full((8, w))] + [anywhere] * n_s,
        out_shape=[jax.ShapeDtypeStruct((seq, 4 * w), BF16), jax.ShapeDtypeStruct((8, w), F32)]
        + [jax.ShapeDtypeStruct(s.shape, s.dtype) for s in chip_sums],
        scratch_shapes=[pltpu.VMEM((HGRN_HEADS, LANES, LANES), F32)] + (_chip_exchange_scratch(n_s) if n_s else []),
        compiler_params=_cparams("arbitrary"),
    )(hg, lb_logits, hnw, o_pre, st0, dyb, *chip_sums)


def in_bwd(dq, dk, dv, dhg, cos_t, sin_t, w_in4, x, norm1_w, dh1):
    seq = x.shape[0]
    tm = ROW_TILE
    cw = w_in4.shape[2]

    def body(dq_ref, dk_ref, dv_ref, dhg_ref, cos_ref, sin_ref, w_ref,
             x_ref, nw_ref, dh1_ref, dproj_ref, dx_ref, acc_ref):
        @pl.when(pl.program_id(0) == 0)
        def _():
            acc_ref[...] = jnp.zeros_like(acc_ref)

        cos, sin = cos_ref[...], sin_ref[...]
        dqa = _rotary_bwd(dq_ref[...], cos, sin)
        dka = _rotary_bwd(dk_ref[...], cos, sin)
        dproj = jnp.concatenate(
            [jnp.concatenate([dqa, dka, dv_ref[...]], axis=1).astype(BF16), dhg_ref[...]], axis=1)
        dproj_ref[...] = dproj
        du = _dot_nt(dproj[:, :cw], w_ref[0])
        for j in range(1, N_CHIPS):
            du = du + _dot_nt(dproj[:, j * cw:(j + 1) * cw], w_ref[j])
        xv = x_ref[...]
        r1 = _rms(xv)
        nx = xv * r1
        acc_ref[0:1, :] += jnp.sum(du * nx, axis=0, keepdims=True)
        dn = du * nw_ref[...]
        dx_ref[...] = dh1_ref[...] + r1 * (dn - nx * jnp.mean(dn * nx, axis=-1, keepdims=True))

    half = _rows(tm, ATTN_WIDTH)
    wide = _rows(tm, D_MODEL)
    return pl.pallas_call(
        body, name="in_bwd", grid=(seq // tm,),
        in_specs=[half] * 3 + [_rows(tm, 4 * HGRN_WIDTH), _rows(tm, LANES), _rows(tm, LANES),
                               _weight((N_CHIPS, D_MODEL, cw)), wide, _full((1, D_MODEL)), wide],
        out_specs=[_rows(tm, IN_PROJ_WIDTH), wide, _full((8, D_MODEL))],
        out_shape=[jax.ShapeDtypeStruct((seq, IN_PROJ_WIDTH), BF16), jax.ShapeDtypeStruct((seq, D_MODEL), F32),
                   jax.ShapeDtypeStruct((8, D_MODEL), F32)],
        compiler_params=_cparams("arbitrary"),
    )(dq, dk, dv, dhg, cos_t, sin_t, w_in4, x, norm1_w, dh1)


def weight_grad(a, b, col_block, name, group=1, small_pack=None):
    seq, kdim = a.shape
    ndim = b.shape[1]
    nj = ndim // col_block
    tk = min(1024, seq)
    hosting = small_pack is not None
    n_j, n_t = nj // group, seq // tk

    def body(*refs):
        a_ref, b_ref = refs[:2]
        o_ref = refs[3] if hosting else refs[2]
        if hosting:
            start, finish = _pack_gather_phases(refs[2], refs[4], *refs[5:])
            pl.when((pl.program_id(0) == 0) & (pl.program_id(1) == 0))(start)

        @pl.when(pl.program_id(1) == 0)
        def _():
            o_ref[...] = jnp.zeros_like(o_ref)

        acc = _dot_tn(a_ref[...].astype(BF16), b_ref[...].astype(BF16))
        for i in range(group):
            o_ref[i] += acc[:, i * col_block:(i + 1) * col_block]
        if hosting:
            pl.when((pl.program_id(0) == n_j - 1) & (pl.program_id(1) == n_t - 1))(finish)

    anywhere = pl.BlockSpec(memory_space=pl.ANY)
    out = pl.pallas_call(
        body, name=name, grid=(n_j, n_t),
        in_specs=[pl.BlockSpec((tk, kdim), lambda j, t: (t, 0)),
                  pl.BlockSpec((tk, group * col_block), lambda j, t: (t, j))] + [anywhere] * hosting,
        out_specs=[pl.BlockSpec((group, kdim, col_block), lambda j, t: (j, 0, 0))] + [anywhere] * hosting,
        out_shape=[jax.ShapeDtypeStruct((nj, kdim, col_block), F32)]
        + ([jax.ShapeDtypeStruct((N_DEV,) + small_pack.shape, F32)] if hosting else []),
        scratch_shapes=[pltpu.SemaphoreType.DMA((N_DEV - 1,)), pltpu.SemaphoreType.DMA((N_DEV - 1,)),
                        pltpu.SemaphoreType.DMA] if hosting else [],
        compiler_params=_cparams("arbitrary", "arbitrary"),
    )(a, b, *([small_pack] if hosting else []))
    return out if hosting else out[0]


def _sibling_exchange_phases(g_refs, out_refs, send_sems, recv_sems):
    x, y, cc = _mesh_pos()

    def copies():
        return [pltpu.make_async_remote_copy(
            src_ref=g_refs[i].at[j, 1 - cc], dst_ref=out_refs[i].at[j],
            send_sem=send_sems.at[i * N_CHIPS + j], recv_sem=recv_sems.at[i * N_CHIPS + j],
            device_id=(x, y, 1 - cc), device_id_type=MESH_ID)
            for i in range(len(g_refs)) for j in range(N_CHIPS)]

    def start():
        for cp in copies():
            cp.start()

    def finish():
        for cp in copies():
            cp.wait_recv()
        for cp in copies():
            cp.wait_send()

    return start, finish


def _sibling_exchange_scratch(n):
    return [pltpu.SemaphoreType.DMA((n * N_CHIPS,)), pltpu.SemaphoreType.DMA((n * N_CHIPS,))]


def _sibling_exchange_shapes(grads):
    return [jax.ShapeDtypeStruct((N_CHIPS,) + g.shape[2:], g.dtype) for g in grads]


def exchange_with_sibling(grads, name):
    n = len(grads)

    def body(*refs):
        start, finish = _sibling_exchange_phases(refs[:n], refs[n:2 * n], refs[2 * n], refs[2 * n + 1])
        start()
        finish()

    return pl.pallas_call(
        body, name=name,
        in_specs=[pl.BlockSpec(memory_space=pl.ANY)] * n,
        out_specs=[pl.BlockSpec(memory_space=pl.ANY)] * n,
        out_shape=_sibling_exchange_shapes(grads),
        scratch_shapes=_sibling_exchange_scratch(n),
    )(*grads)


def add_own_half(grad, recv, name):
    _, _, r, c = grad.shape
    tr = r // 2 if r % 32 == 0 else r

    def body(cc_ref, g_ref, r_ref, o_ref):
        o_ref[...] = (g_ref[...] + r_ref[...]).astype(BF16)

    grid_spec = pltpu.PrefetchScalarGridSpec(
        num_scalar_prefetch=1, grid=(N_CHIPS, r // tr),
        in_specs=[pl.BlockSpec((None, None, tr, c), lambda j, t, cc: (j, cc[0], t, 0)),
                  pl.BlockSpec((None, tr, c), lambda j, t, cc: (j, t, 0))],
        out_specs=pl.BlockSpec((None, tr, c), lambda j, t, cc: (j, t, 0)))
    cc = lax.axis_index("c").astype(jnp.int32).reshape(1)
    return pl.pallas_call(
        body, name=name, grid_spec=grid_spec,
        out_shape=jax.ShapeDtypeStruct((N_CHIPS, r, c), BF16),
        compiler_params=_cparams("parallel", "parallel"),
    )(cc, grad, recv)


def _chip_exchange_phases(s_refs, out_refs, send_sems, recv_sems):
    n = len(s_refs)
    x, y, cc = _mesh_pos()
    my_chip = 2 * x + y
    chips = [(1 - x, y), (x, 1 - y), (1 - x, 1 - y)]

    def outgoing():
        return [pltpu.make_async_remote_copy(
            src_ref=s_refs[i].at[2 * px + py], dst_ref=out_refs[i].at[my_chip],
            send_sem=send_sems.at[3 * i + j], recv_sem=recv_sems.at[3 * i + j],
            device_id=(px, py, cc), device_id_type=MESH_ID)
            for i in range(n) for j, (px, py) in enumerate(chips)]

    def start():
        for cp in outgoing():
            cp.start()

    def finish():
        for i in range(n):
            for j, (px, py) in enumerate(chips):
                pltpu.make_async_remote_copy(
                    src_ref=s_refs[i].at[my_chip], dst_ref=out_refs[i].at[2 * px + py],
                    send_sem=send_sems.at[3 * i + j], recv_sem=recv_sems.at[3 * i + j],
                    device_id=(px, py, cc), device_id_type=MESH_ID).wait_recv()
        for cp in outgoing():
            cp.wait_send()

    return start, finish


def _chip_exchange_scratch(n):
    return [pltpu.SemaphoreType.DMA((3 * n,)), pltpu.SemaphoreType.DMA((3 * n,))]


def sum_chips(sums, parts, name):
    _, r, c = parts.shape
    tr = r // 2 if r % 32 == 0 else r

    def body(idx_ref, s_ref, p1_ref, p2_ref, p3_ref, o_ref):
        o_ref[...] = ((s_ref[...].astype(F32) + p1_ref[...].astype(F32))
                      + p2_ref[...].astype(F32)) + p3_ref[...].astype(F32)

    def pick(k):
        return pl.BlockSpec((None, tr, c), lambda t, idx: (idx[k], t, 0))

    x, y = lax.axis_index("x"), lax.axis_index("y")
    idx = jnp.stack([2 * x + y, 2 * (1 - x) + y, 2 * x + (1 - y), 2 * (1 - x) + (1 - y)]).astype(jnp.int32)
    grid_spec = pltpu.PrefetchScalarGridSpec(
        num_scalar_prefetch=1, grid=(r // tr,),
        in_specs=[pick(0), pick(1), pick(2), pick(3)],
        out_specs=pl.BlockSpec((tr, c), lambda t, idx: (t, 0)))
    return pl.pallas_call(
        body, name=name, grid_spec=grid_spec,
        out_shape=jax.ShapeDtypeStruct((r, c), F32),
        compiler_params=_cparams("parallel"),
    )(idx, sums, parts, parts, parts)


def share_with_sibling(halves, name):
    n = len(halves)

    def body(*refs):
        h_refs, out_refs = refs[:n], refs[n:2 * n]
        send_sems, recv_sems = refs[2 * n], refs[2 * n + 1]
        x, y, cc = _mesh_pos()
        copies = [pltpu.make_async_remote_copy(
            src_ref=h_refs[i], dst_ref=out_refs[i],
            send_sem=send_sems.at[i], recv_sem=recv_sems.at[i],
            device_id=(x, y, 1 - cc), device_id_type=MESH_ID) for i in range(n)]
        for cp in copies:
            cp.start()
        for cp in copies:
            cp.wait_recv()
        for cp in copies:
            cp.wait_send()

    return pl.pallas_call(
        body, name=name,
        in_specs=[pl.BlockSpec(memory_space=pl.ANY)] * n,
        out_specs=[pl.BlockSpec(memory_space=pl.ANY)] * n,
        out_shape=[jax.ShapeDtypeStruct(h.shape, h.dtype) for h in halves],
        scratch_shapes=[pltpu.SemaphoreType.DMA((n,)), pltpu.SemaphoreType.DMA((n,))],
    )(*halves)


def _adam_update(w, g, m, v):
    m = ADAM_B1 * m + (1.0 - ADAM_B1) * g
    v = ADAM_B2 * v + (1.0 - ADAM_B2) * (g * g)
    m_hat = m / (1.0 - ADAM_B1 ** ADAM_STEP)
    v_hat = v / (1.0 - ADAM_B2 ** ADAM_STEP)
    delta = -ADAM_LR * (m_hat / (jnp.sqrt(v_hat) + ADAM_EPS) + ADAM_WD * w)
    return delta, m, v


ADAMW_STEPS = 8


def adamw(ws, g_mine, g_sibling, ms, vs, name, chip_sums=()):
    n, n_s = len(ws), len(chip_sums)
    per_half = ADAMW_STEPS // 2

    def body(*refs):
        cc_ref = refs[0]
        ins = refs[1:1 + 5 * n]
        outs = refs[1 + 5 * n + n_s:1 + 9 * n + n_s]
        step = pl.program_id(0)
        if n_s:
            start, finish = _chip_exchange_phases(
                refs[1 + 5 * n:1 + 5 * n + n_s], refs[1 + 9 * n + n_s:1 + 9 * n + 2 * n_s],
                *refs[1 + 9 * n + 2 * n_s:])
            pl.when(step == 0)(start)
        mine = (step // per_half) == cc_ref[0]
        for i in range(n):
            w_ref, ga_ref, gb_ref, m_ref, v_ref = ins[5 * i:5 * i + 5]
            g_ref, d_ref, nm_ref, nv_ref = outs[4 * i:4 * i + 4]
            g = jnp.where(mine, ga_ref[...], gb_ref[...])
            g_ref[...] = g
            d, nm, nv = _adam_update(w_ref[...], g, m_ref[...], v_ref[...])
            d_ref[...] = d
            nm_ref[...] = nm
            nv_ref[...] = nv
        if n_s:
            pl.when(step == ADAMW_STEPS - 1)(finish)

    in_specs, out_specs, out_shape, operands = [], [], [], []
    for w, ga, gb, m, v in zip(ws, g_mine, g_sibling, ms, vs):
        r, c = w.shape
        tr = r // ADAMW_STEPS
        full = pl.BlockSpec((tr, c), lambda t, cc: (t, 0))
        part = pl.BlockSpec((tr, c), lambda t, cc: (t % per_half, 0))
        in_specs += [full, part, part, full, full]
        out_specs += [full] * 4
        out_shape += [jax.ShapeDtypeStruct((r, c), F32)] * 4
        operands += [w, ga, gb, m, v]
    anywhere = pl.BlockSpec(memory_space=pl.ANY)
    grid_spec = pltpu.PrefetchScalarGridSpec(
        num_scalar_prefetch=1, grid=(ADAMW_STEPS,),
        in_specs=in_specs + [anywhere] * n_s, out_specs=out_specs + [anywhere] * n_s,
        scratch_shapes=_chip_exchange_scratch(n_s) if n_s else [])
    cc = lax.axis_index("c").astype(jnp.int32).reshape(1)
    res = pl.pallas_call(
        body, name=name, grid_spec=grid_spec,
        out_shape=out_shape + [jax.ShapeDtypeStruct(s.shape, s.dtype) for s in chip_sums],
        compiler_params=_cparams("arbitrary"),
    )(cc, *operands, *chip_sums)
    per_shard = [tuple(res[4 * i:4 * i + 4]) for i in range(n)]
    return (per_shard, list(res[4 * n:])) if n_s else per_shard


def _pack_gather_phases(p_ref, out_ref, send_sems, recv_sems, local_sem):
    x, y, cc = _mesh_pos()
    me = 4 * x + 2 * y + cc
    flips = [(fx, fy, fc) for fx in (0, 1) for fy in (0, 1) for fc in (0, 1)][1:]

    def copy(k, row):
        fx, fy, fc = flips[k]
        return pltpu.make_async_remote_copy(
            src_ref=p_ref, dst_ref=out_ref.at[row],
            send_sem=send_sems.at[k], recv_sem=recv_sems.at[k],
            device_id=(x ^ fx, y ^ fy, cc ^ fc), device_id_type=MESH_ID)

    def local():
        return pltpu.make_async_copy(p_ref, out_ref.at[me], local_sem)

    def start():
        local().start()
        for k in range(len(flips)):
            copy(k, me).start()

    def finish():
        for k, (fx, fy, fc) in enumerate(flips):
            copy(k, 4 * (x ^ fx) + 2 * (y ^ fy) + (cc ^ fc)).wait_recv()
        for k in range(len(flips)):
            copy(k, me).wait_send()
        local().wait()

    return start, finish


def small_update(gathered, wpack, mpack, vpack):
    hw = HGRN_WIDTH

    def body(g_ref, w_ref, m_ref, v_ref, go_ref, d_ref, nm_ref, nv_ref, loss_ref):
        g = g_ref[0]
        for d in range(1, N_DEV):
            g = g + g_ref[d]
        wv = w_ref[...]
        a0, a1 = wv[4:5, :hw], wv[4:5, hw:]
        mx = jnp.maximum(a0, a1)
        e0, e1 = jnp.exp(a0 - mx), jnp.exp(a1 - mx)
        lb = e0 / (e0 + e1)
        dl = g[4:5, :hw] * lb * (1.0 - lb)
        row = lax.broadcasted_iota(jnp.int32, g.shape, 0)
        lb_row = jnp.concatenate([dl, -dl], axis=1)
        grads = jnp.where(row == 4, lb_row, jnp.where(row < 4, g, 0.0))
        go_ref[...] = grads
        d, nm, nv = _adam_update(wv, grads, m_ref[...], v_ref[...])
        d_ref[...] = d
        nm_ref[...] = nm
        nv_ref[...] = nv
        loss_ref[...] = jnp.zeros((8, LANES), F32) + jnp.sum(g[5:6, :])

    vm = pl.BlockSpec(memory_space=pltpu.VMEM)
    return pl.pallas_call(
        body, name="small_update",
        in_specs=[vm] * 4, out_specs=[vm] * 5,
        out_shape=[jax.ShapeDtypeStruct(wpack.shape, F32)] * 4 + [jax.ShapeDtypeStruct((8, LANES), F32)],
    )(gathered, wpack, mpack, vpack)


def _pack_small(n1, n2, fn, hn, lbl):
    z = jnp.zeros((1, D_MODEL - HGRN_WIDTH), F32)
    rows = [n1.reshape(1, D_MODEL), n2.reshape(1, D_MODEL), fn.reshape(1, D_MODEL),
            jnp.concatenate([hn.reshape(1, HGRN_WIDTH), z], axis=1), lbl.reshape(1, 2 * HGRN_WIDTH),
            jnp.zeros((3, D_MODEL), F32)]
    return jnp.concatenate(rows, axis=0)


def _unpack_small(pack):
    return (pack[0:1], pack[4].reshape(2, HGRN_WIDTH), pack[3:4, :HGRN_WIDTH], pack[1:2], pack[2])


def kernel(x, norm1_w, w_in, lb_logits, hgrn_norm_w, w_out, norm2_w, w_gate_up, w_down, final_norm_w, loss_target, m_norm1_w, m_w_in, m_lb_logits, m_hgrn_norm_w, m_w_out, m_norm2_w, m_w_gate_up, m_w_down, m_final_norm_w, v_norm1_w, v_w_in, v_lb_logits, v_hgrn_norm_w, v_w_out, v_norm2_w, v_w_gate_up, v_w_down, v_final_norm_w):
    seq = x.shape[1]
    xs = x.reshape(seq, D_MODEL)
    target = loss_target.reshape(seq, D_MODEL)
    shards = {"w_in": w_in[0], "w_out": w_out[0], "w_gu": w_gate_up[0], "w_down": w_down[0]}

    cast = {k: cast_bf16(w, "cast_" + k) for k, w in shards.items()}
    w_in4 = allgather_halves(cast["w_in"], "gather_w_in").reshape(N_CHIPS, D_MODEL, -1)

    cos_t, sin_t = _rope_tables(seq)
    fw = final_norm_w.reshape(1, D_MODEL)

    qr, kr, va, hg, u, g_out, g_down = in_proj(
        xs, norm1_w, w_in4, cos_t, sin_t, [cast["w_out"], cast["w_down"]])
    ya, lse = attn_fwd(qr, kr, va)
    yb, o_pre, st0, g_gu = hgrn_fwd(hg, lb_logits, hgrn_norm_w, [cast["w_gu"]])
    w_out_f = g_out.reshape(D_MODEL, D_MODEL)
    w_gu4 = g_gu.reshape(N_CHIPS, D_MODEL, -1)
    w_down_f = g_down.reshape(FFN_HIDDEN, D_MODEL)
    mixed, h1, u2, g, up, act, dh2, acc_fin = ffn_fwd(
        ya, yb, xs, w_out_f, norm2_w, w_gu4, w_down_f, fw, target)

    cw_in, cw_gu = w_in4.shape[2], w_gu4.shape[2]
    dgu, dh1, dya, dyb, delta, acc_n2 = ffn_bwd(dh2, w_down_f, g, up, w_gu4, h1, norm2_w, w_out_f, ya)
    early = [
        weight_grad(mixed, dh1, D_MODEL, "wgrad_out").reshape(N_CHIPS, 2, D_MODEL // 8, D_MODEL),
        weight_grad(u2, dgu, cw_gu, "wgrad_gu", group=2).reshape(N_CHIPS, 2, D_MODEL // 2, cw_gu),
        weight_grad(act, dh2, D_MODEL, "wgrad_down").reshape(N_CHIPS, 2, FFN_HIDDEN // 8, D_MODEL),
    ]
    early_names = ["out", "gu", "down"]
    dq, dk, dv, *early_recv = attn_bwd(qr, kr, va, dya, lse, delta, early)
    early_sums = [add_own_half(gr, rc, "add_half_" + nm) for gr, rc, nm in zip(early, early_recv, early_names)]
    dhg, acc_hg, *early_parts = hgrn_bwd(hg, lb_logits, hgrn_norm_w, o_pre, st0, dyb, early_sums)
    dproj, dx, acc_n1 = in_bwd(dq, dk, dv, dhg, cos_t, sin_t, w_in4, xs, norm1_w, dh1)
    z512 = jnp.zeros((1, D_MODEL - HGRN_WIDTH), F32)
    gpack = jnp.concatenate([
        acc_n1[0:1], acc_n2[0:1], acc_fin[0:1],
        jnp.concatenate([acc_hg[0:1], z512], axis=1), jnp.concatenate([acc_hg[1:2], z512], axis=1),
        acc_fin[1:2], jnp.zeros((2, D_MODEL), F32)], axis=0)
    g_in, gathered_packs = weight_grad(u, dproj, cw_in, "wgrad_in", group=2, small_pack=gpack)
    late = [g_in.reshape(N_CHIPS, 2, D_MODEL // 2, cw_in)]
    late_recv = exchange_with_sibling(late, "grad_exchange_sibling_late")
    late_sums = [add_own_half(late[0], late_recv[0], "add_half_in")]

    early_halves = [sum_chips(s, p, "sum_chips_" + nm) for s, p, nm in zip(early_sums, early_parts, early_names)]
    early_others = share_with_sibling(early_halves, "grad_share_sibling_early")
    early_keys = ["w_out", "w_gu", "w_down"]
    moments = {"w_in": (m_w_in, v_w_in), "w_out": (m_w_out, v_w_out),
               "w_gu": (m_w_gate_up, v_w_gate_up), "w_down": (m_w_down, v_w_down)}
    early_updates, late_parts = adamw(
        [shards[k] for k in early_keys], early_halves, early_others,
        [moments[k][0][0] for k in early_keys], [moments[k][1][0] for k in early_keys],
        "adamw_early", chip_sums=late_sums)
    late_halves = [sum_chips(late_sums[0], late_parts[0], "sum_chips_in")]
    late_others = share_with_sibling(late_halves, "grad_share_sibling_late")
    late_updates = adamw([shards["w_in"]], late_halves, late_others,
                         [moments["w_in"][0][0]], [moments["w_in"][1][0]], "adamw_in")
    big = {k: tuple(t[None] for t in upd) for k, upd in zip(early_keys + ["w_in"], early_updates + late_updates)}

    wpack = _pack_small(norm1_w, norm2_w, final_norm_w, hgrn_norm_w, lb_logits)
    mpack = _pack_small(m_norm1_w, m_norm2_w, m_final_norm_w, m_hgrn_norm_w, m_lb_logits)
    vpack = _pack_small(v_norm1_w, v_norm2_w, v_final_norm_w, v_hgrn_norm_w, v_lb_logits)
    gs, ds, nms, nvs, loss8 = small_update(gathered_packs, wpack, mpack, vpack)
    loss = loss8[0, 0]

    def assemble(small_pack, idx):
        n1, lbl, hn, n2, fn = _unpack_small(small_pack)
        return (n1, big["w_in"][idx], lbl, hn, big["w_out"][idx], n2, big["w_gu"][idx], big["w_down"][idx], fn)

    return (loss, dx.reshape(x.shape), *assemble(gs, 0), *assemble(ds, 1), *assemble(nms, 2), *assemble(nvs, 3))
```

```python
import functools

import jax
import jax.numpy as jnp
from jax import lax
from jax.experimental import pallas as pl
from jax.experimental.pallas import tpu as pltpu

F32 = jnp.float32
BF16 = jnp.bfloat16

D_MODEL = 1024
ATTN_WIDTH = 512
HEAD_DIM = 64
DILATED_PAIRS = ((128, 1), (512, 4), (2048, 16))
ATTN_BLOCK = 128
ROPE_THETA = 10000.0
HGRN_WIDTH = 512
HGRN_CHUNK = 16
HGRN_HEADS = 4
IN_PROJ_WIDTH = 3584
FFN_HIDDEN = 2816
NORM_EPS = 1e-6
ATTN_SCALE = HEAD_DIM ** -0.5
N_CHIPS = 4
N_DEV = 8

ADAM_LR = 0.001
ADAM_B1 = 0.9
ADAM_B2 = 0.999
ADAM_EPS = 1e-08
ADAM_WD = 0.01
ADAM_STEP = 10

LANES = 128
HGRN_ROWS = 128
HGRN_STEP_BLOCKS = 4
ROW_TILE = 256
ATTN_STEP_ROWS = 2048
ATTN_FWD_UNROLL = 8
ATTN_BWD_UNROLL = 8
ATTN_TRANSPOSED_BELOW = 16
ATTN_MAJOR_DILATION = 16
VMEM_LIMIT = 56 * 1024 * 1024
NEG_BIG = -1e30
MESH_ID = pl.DeviceIdType.MESH


def _cparams(*sem):
    return pltpu.CompilerParams(dimension_semantics=tuple(sem), vmem_limit_bytes=VMEM_LIMIT)


def _dot(a, b):
    return jnp.dot(a, b, preferred_element_type=F32)


def _dot_nt(a, b):
    return lax.dot_general(a, b, (((1,), (1,)), ((), ())), preferred_element_type=F32)


def _dot_tn(a, b):
    return lax.dot_general(a, b, (((0,), (0,)), ((), ())), preferred_element_type=F32)


def _sigmoid(x):
    return 1.0 / (1.0 + jnp.exp(-x))


def _full(shape):
    n = len(shape)
    return pl.BlockSpec(shape, lambda *_: (0,) * n)


def _weight(shape):
    n = len(shape)
    return pl.BlockSpec(shape, lambda *_: (0,) * n, pipeline_mode=pl.Buffered(1))


def _rows(tm, width):
    return pl.BlockSpec((tm, width), lambda i: (i, 0))


def _swap32(x):
    lane = lax.broadcasted_iota(jnp.int32, x.shape, 1)
    first = (lane % HEAD_DIM) < (HEAD_DIM // 2)
    return jnp.where(first, pltpu.roll(x, LANES - 32, axis=1), pltpu.roll(x, 32, axis=1))


def _rotary_fwd(x, cos, sin_signed):
    parts = []
    for j in range(x.shape[1] // LANES):
        xc = x[:, j * LANES:(j + 1) * LANES]
        parts.append(xc * cos + _swap32(xc) * sin_signed)
    return jnp.concatenate(parts, axis=1)


def _rotary_bwd(dy, cos, sin_signed):
    parts = []
    for j in range(dy.shape[1] // LANES):
        dc = dy[:, j * LANES:(j + 1) * LANES]
        parts.append(dc * cos + _swap32(dc * sin_signed))
    return jnp.concatenate(parts, axis=1)


def _rope_tables(seq):
    half = HEAD_DIM // 2
    inv_freq = ROPE_THETA ** (-jnp.arange(half, dtype=F32) / half)
    ang = jnp.arange(seq, dtype=F32)[:, None] * inv_freq[None, :]
    cos, sin = jnp.cos(ang), jnp.sin(ang)
    cos_t = jnp.tile(cos, (1, LANES // half))
    sin_t = jnp.tile(jnp.concatenate([-sin, sin], axis=1), (1, LANES // HEAD_DIM))
    return cos_t, sin_t


def cast_bf16(w, name):
    r, c = w.shape
    half = r // 2

    def body(w_ref, o_ref):
        o_ref[...] = w_ref[...].astype(BF16)

    return pl.pallas_call(
        body, name=name, grid=(2,),
        in_specs=[pl.BlockSpec((half, c), lambda i: (i, 0))],
        out_specs=pl.BlockSpec((None, half, c), lambda i: (i, 0, 0)),
        out_shape=jax.ShapeDtypeStruct((2, half, c), BF16),
        compiler_params=_cparams("parallel"),
    )(w)


def _mesh_pos():
    return lax.axis_index("x"), lax.axis_index("y"), lax.axis_index("c")


GATHER_COPIES = 7


def _gather_phases(x_refs, out_refs, send_sems, recv_sems, local_sems):
    n = len(x_refs)
    x, y, cc = _mesh_pos()
    me, sibling = (x, y, cc), (x, y, 1 - cc)
    chips = [(1 - x, y), (x, 1 - y), (1 - x, 1 - y)]

    def rows(i, px, py, pc):
        return out_refs[i].at[4 * px + 2 * py + pc]

    def copy(i, k, block, to, src=None):
        return pltpu.make_async_remote_copy(
            src_ref=rows(i, *block) if src is None else src, dst_ref=rows(i, *block),
            send_sem=send_sems.at[GATHER_COPIES * i + k], recv_sem=recv_sems.at[GATHER_COPIES * i + k],
            device_id=to, device_id_type=MESH_ID)

    def local(i):
        return pltpu.make_async_copy(x_refs[i].at[cc], rows(i, *me), local_sems.at[i])

    def first(i):
        mine = x_refs[i].at[cc]
        return [copy(i, 0, me, sibling, src=mine)] + [
            copy(i, 1 + j, me, (*chip, cc), src=mine) for j, chip in enumerate(chips)]

    def passed(i):
        return [copy(i, 4 + j, (*chip, cc), sibling) for j, chip in enumerate(chips)]

    def start():
        for i in range(n):
            local(i).start()
            for cp in first(i):
                cp.start()

    def forward():
        for i in range(n):
            onward = passed(i)
            for j, chip in enumerate(chips):
                copy(i, 1 + j, (*chip, cc), me).wait_recv()
                onward[j].start()

    def finish():
        for i in range(n):
            copy(i, 0, sibling, me).wait_recv()
            for j, chip in enumerate(chips):
                copy(i, 4 + j, (*chip, 1 - cc), me).wait_recv()
            for cp in first(i) + passed(i):
                cp.wait_send()
            local(i).wait()

    return start, forward, finish


def _gather_scratch(n):
    return [pltpu.SemaphoreType.DMA((GATHER_COPIES * n,)), pltpu.SemaphoreType.DMA((GATHER_COPIES * n,)),
            pltpu.SemaphoreType.DMA((n,))]


def _gathered_shape(halves):
    return jax.ShapeDtypeStruct((N_DEV,) + halves.shape[1:], halves.dtype)


def allgather_halves(halves, name):
    def body(x_ref, out_ref, send_sems, recv_sems, local_sems):
        start, forward, finish = _gather_phases([x_ref], [out_ref], send_sems, recv_sems, local_sems)
        start()
        forward()
        finish()

    return pl.pallas_call(
        body, name=name,
        in_specs=[pl.BlockSpec(memory_space=pl.ANY)],
        out_specs=pl.BlockSpec(memory_space=pl.ANY),
        out_shape=_gathered_shape(halves),
        scratch_shapes=_gather_scratch(1),
    )(halves)


def _rms(x):
    return lax.rsqrt(jnp.mean(x * x, axis=-1, keepdims=True) + NORM_EPS)


def in_proj(x, norm1_w, w_in4, cos_t, sin_t, weight_halves=()):
    seq = x.shape[0]
    tm = ROW_TILE
    cw = w_in4.shape[2]
    n_w = len(weight_halves)
    steps = seq // tm
    major = ATTN_MAJOR_DILATION
    slabs = ATTN_WIDTH // LANES

    def body(*refs):
        x_ref, nw_ref, w_ref, cos_ref, sin_ref = refs[:5]
        q_ref, k_ref, v_ref, hg_ref, u_ref = refs[5 + n_w:10 + n_w]
        major_refs = refs[10 + n_w:13 + n_w]
        slab_scr = refs[13 + 2 * n_w]
        step = pl.program_id(0)
        if n_w:
            start, forward, finish = _gather_phases(
                refs[5:5 + n_w], refs[13 + n_w:13 + 2 * n_w], *refs[14 + 2 * n_w:])
            pl.when(step == 0)(start)
            pl.when(step == (3 * steps) // 4)(forward)
        xv = x_ref[...]
        u = ((xv * _rms(xv)) * nw_ref[...]).astype(BF16)
        u_ref[...] = u
        proj = jnp.concatenate([_dot(u, w_ref[j]) for j in range(N_CHIPS)], axis=1)
        cos, sin = cos_ref[...], sin_ref[...]
        a = ATTN_WIDTH
        qkv = (_rotary_fwd(proj[:, :a], cos, sin), _rotary_fwd(proj[:, a:2 * a], cos, sin), proj[:, 2 * a:3 * a])
        for ref, val in zip((q_ref, k_ref, v_ref), qkv):
            ref[...] = val
        hg_ref[...] = proj[:, 3 * a:]
        for idx, val in enumerate(qkv):
            for s in range(slabs):
                slab_scr[idx, s] = val[:, s * LANES:(s + 1) * LANES]
        for idx, out in enumerate(major_refs):
            for r in range(major):
                for s in range(slabs):
                    out[r, :, s * LANES:(s + 1) * LANES] = (
                        slab_scr.at[idx, s][pl.ds(r, tm // major, stride=major), :].astype(BF16))
        if n_w:
            pl.when(step == steps - 1)(finish)

    anywhere = pl.BlockSpec(memory_space=pl.ANY)
    major_spec = pl.BlockSpec((major, tm // major, ATTN_WIDTH), lambda i: (0, i, 0))
    return pl.pallas_call(
        body, name="in_proj", grid=(steps,),
        in_specs=[_rows(tm, D_MODEL), _full((1, D_MODEL)), _weight((N_CHIPS, D_MODEL, cw)),
                  _rows(tm, LANES), _rows(tm, LANES)] + [anywhere] * n_w,
        out_specs=[_rows(tm, ATTN_WIDTH)] * 3 + [_rows(tm, 4 * HGRN_WIDTH), _rows(tm, D_MODEL)]
        + [major_spec] * 3 + [anywhere] * n_w,
        out_shape=[jax.ShapeDtypeStruct((seq, ATTN_WIDTH), F32)] * 3
        + [jax.ShapeDtypeStruct((seq, 4 * HGRN_WIDTH), F32), jax.ShapeDtypeStruct((seq, D_MODEL), BF16)]
        + [jax.ShapeDtypeStruct((major, seq // major, ATTN_WIDTH), BF16)] * 3
        + [_gathered_shape(h) for h in weight_halves],
        scratch_shapes=[pltpu.VMEM((3, slabs, tm, LANES), F32)] + (_gather_scratch(n_w) if n_w else []),
        compiler_params=_cparams("arbitrary"),
    )(x, norm1_w, w_in4, cos_t, sin_t, *weight_halves)


def _head_masks():
    lane = lax.broadcasted_iota(jnp.int32, (1, LANES), 1)
    return [(lane // HEAD_DIM) == h for h in range(LANES // HEAD_DIM)]


def _window_valid(no_prev):
    qi = lax.broadcasted_iota(jnp.int32, (ATTN_BLOCK, 2 * ATTN_BLOCK), 0)
    kj = lax.broadcasted_iota(jnp.int32, (ATTN_BLOCK, 2 * ATTN_BLOCK), 1)
    valid = (kj >= qi) & (kj <= qi + ATTN_BLOCK)
    return valid & (jnp.logical_not(no_prev) | (kj >= ATTN_BLOCK))


def _strided_rows(start, dilation):
    if dilation == 1:
        return pl.ds(start, ATTN_BLOCK)
    return pl.ds(start, ATTN_BLOCK, stride=dilation)


def _block_before(edge_ref, cur_ref, t, r, span, dilation, per_step):
    edge = edge_ref[_strided_rows(ATTN_STEP_ROWS - span + r, dilation), :]
    if per_step == 1:
        return edge
    inside = cur_ref[_strided_rows(r + span * jnp.maximum(t - 1, 0), dilation), :]
    return jnp.where(t == 0, edge, inside)


def _attn_specs():
    cur = pl.BlockSpec((ATTN_STEP_ROWS, LANES), lambda hp, j: (j, hp))
    prev = pl.BlockSpec((ATTN_STEP_ROWS, LANES), lambda hp, j: (jnp.maximum(j - 1, 0), hp))
    return cur, prev


def _for_each_block(dilation, unroll, block):
    span = ATTN_BLOCK * dilation
    per_step = ATTN_STEP_ROWS // span

    def trip(it, carry):
        block(it // dilation, it % dilation, span, per_step)
        return carry

    lax.fori_loop(0, per_step * dilation, trip, 0, unroll=unroll)


def _load_qkv(natural, major, t, r, span, dilation, per_step):
    if dilation == ATTN_MAJOR_DILATION:
        q_ref, kc_ref, vc_ref, kp_ref, vp_ref = major
        return (q_ref[r] * ATTN_SCALE, jnp.concatenate([kp_ref[r], kc_ref[r]], axis=0),
                jnp.concatenate([vp_ref[r], vc_ref[r]], axis=0))
    q_ref, kc_ref, vc_ref, kp_ref, vp_ref = natural
    rows = _strided_rows(r + span * t, dilation)
    kp = _block_before(kp_ref, kc_ref, t, r, span, dilation, per_step)
    vp = _block_before(vp_ref, vc_ref, t, r, span, dilation, per_step)
    return ((q_ref[rows, :] * ATTN_SCALE).astype(BF16),
            jnp.concatenate([kp, kc_ref[rows, :]], axis=0).astype(BF16),
            jnp.concatenate([vp, vc_ref[rows, :]], axis=0).astype(BF16))


def _major_specs():
    assert ATTN_STEP_ROWS == ATTN_BLOCK * ATTN_MAJOR_DILATION
    shape = (ATTN_MAJOR_DILATION, ATTN_BLOCK, LANES)
    return (pl.BlockSpec(shape, lambda hp, j: (0, j, hp)),
            pl.BlockSpec(shape, lambda hp, j: (0, jnp.maximum(j - 1, 0), hp)))


def attn_fwd(q, k, v, q16, k16, v16):
    seq = q.shape[0]
    cur, prev = _attn_specs()
    cur16, prev16 = _major_specs()

    def body(*refs):
        natural, major, (y_ref, lse_ref) = refs[:5], refs[5:10], refs[10:]
        first_step = pl.program_id(1) == 0
        masks = _head_masks()
        for index, (_, dilation) in enumerate(DILATED_PAIRS):
            def block(t, r, span, per_step, dilation=dilation, merge=index > 0):
                rows = _strided_rows(r + span * t, dilation)
                q2, k2, v2 = _load_qkv(natural, major, t, r, span, dilation, per_step)
                valid = _window_valid(first_step & (t == 0))
                o_acc = jnp.zeros((ATTN_BLOCK, LANES), F32)
                l_acc = jnp.zeros((ATTN_BLOCK, LANES), F32)
                for mh in masks:
                    qm = jnp.where(mh, q2, jnp.zeros_like(q2))
                    s = jnp.where(valid, _dot_nt(qm, k2), NEG_BIG)
                    m = jnp.max(s, axis=-1, keepdims=True)
                    p = jnp.exp(s - m)
                    l = jnp.sum(p, axis=-1, keepdims=True)
                    o = _dot(p.astype(BF16), v2) / l
                    o_acc = jnp.where(mh, o, o_acc)
                    l_acc = jnp.where(mh, m + jnp.log(l), l_acc)
                if merge:
                    y_old, l_old = y_ref[rows, :], lse_ref[rows, :]
                    mx = jnp.maximum(l_old, l_acc)
                    e_old, e_new = jnp.exp(l_old - mx), jnp.exp(l_acc - mx)
                    den = e_old + e_new
                    o_acc = (y_old * e_old + o_acc * e_new) / den
                    l_acc = mx + jnp.log(den)
                y_ref[rows, :] = o_acc
                lse_ref[rows, :] = l_acc

            _for_each_block(dilation, ATTN_FWD_UNROLL, block)

    return pl.pallas_call(
        body, name="attn_fwd", grid=(ATTN_WIDTH // LANES, seq // ATTN_STEP_ROWS),
        in_specs=[cur, cur, cur, prev, prev, cur16, cur16, cur16, prev16, prev16],
        out_specs=[cur, cur],
        out_shape=[jax.ShapeDtypeStruct((seq, ATTN_WIDTH), F32)] * 2,
        compiler_params=_cparams("parallel", "parallel"),
    )(q, k, v, k, v, q16, k16, v16, k16, v16)


def _chunk_cumsum(x, reverse=False):
    rc = lax.broadcasted_iota(jnp.int32, x.shape, 0) % HGRN_CHUNK
    sh = 1
    while sh < HGRN_CHUNK:
        if reverse:
            x = x + jnp.where(rc + sh < HGRN_CHUNK, pltpu.roll(x, x.shape[0] - sh, axis=0), 0.0)
        else:
            x = x + jnp.where(rc >= sh, pltpu.roll(x, sh, axis=0), 0.0)
        sh *= 2
    return x


def _chunk_row(x, row):
    return _chunk_rows([x[n * HGRN_CHUNK + row:n * HGRN_CHUNK + row + 1, :]
                        for n in range(x.shape[0] // HGRN_CHUNK)])


def _chunk_rows(rows):
    return jnp.concatenate([jnp.broadcast_to(r, (HGRN_CHUNK, r.shape[1])) for r in rows], axis=0)


def _hgrn_prep(hg, lbl):
    w = HGRN_WIDTH
    a0, a1 = lbl[0:1, :], lbl[1:2, :]
    mx = jnp.maximum(a0, a1)
    e0, e1 = jnp.exp(a0 - mx), jnp.exp(a1 - mx)
    lb = e0 / (e0 + e1)
    qb, fb, gb = hg[:, :w], hg[:, w:2 * w], hg[:, 3 * w:]
    sg = _sigmoid(fb)
    f = lb + (1.0 - lb) * sg
    b = _chunk_cumsum(jnp.log(f))
    bmid, btot = _chunk_row(b, HGRN_CHUNK // 2 - 1), _chunk_row(b, HGRN_CHUNK - 1)
    sq = _sigmoid(qb)
    p = dict(lb=lb, sg=sg, f=f, kk=1.0 - f, sq=sq, qf=qb * sq, gb=gb,
             e_iq=jnp.exp(b - bmid), e_ik=jnp.exp(bmid - b), e_b=jnp.exp(b),
             e_bb=jnp.exp(btot - b), e_tot=jnp.exp(btot))
    p["qi"] = p["qf"] * p["e_iq"]
    p["ki"] = p["kk"] * p["e_ik"]
    p["qs"] = p["qf"] * p["e_b"]
    p["kb"] = p["kk"] * p["e_bb"]
    return p


def _chunk_masks():
    t = lax.broadcasted_iota(jnp.int32, (HGRN_ROWS, HGRN_ROWS), 0)
    s = lax.broadcasted_iota(jnp.int32, (HGRN_ROWS, HGRN_ROWS), 1)
    tril = ((t // HGRN_CHUNK) == (s // HGRN_CHUNK)) & (s <= t)
    n_chunks = HGRN_ROWS // HGRN_CHUNK
    tt = lax.broadcasted_iota(jnp.int32, (HGRN_ROWS, n_chunks * LANES), 0)
    cc = lax.broadcasted_iota(jnp.int32, (HGRN_ROWS, n_chunks * LANES), 1)
    block = (tt // HGRN_CHUNK) == (cc // LANES)
    return tril, block


def _spread(x, block):
    n_chunks = HGRN_ROWS // HGRN_CHUNK
    return jnp.where(block, jnp.tile(x, (1, n_chunks)), jnp.zeros((), x.dtype))


def _fold(x_full, block):
    n_chunks = HGRN_ROWS // HGRN_CHUNK
    z = jnp.where(block, x_full, 0.0)
    acc = z[:, :LANES]
    for n in range(1, n_chunks):
        acc = acc + z[:, n * LANES:(n + 1) * LANES]
    return acc


def hgrn_fwd(hg, lb_logits, hnw, weight_halves=()):
    seq = hg.shape[0]
    nblk = seq // HGRN_ROWS
    n_steps = nblk // HGRN_STEP_BLOCKS
    step_rows = HGRN_ROWS * HGRN_STEP_BLOCKS
    n_chunks = HGRN_ROWS // HGRN_CHUNK
    n_w = len(weight_halves)

    def body(*refs):
        hg_ref, lbl_ref, hnw_ref = refs[:3]
        w_refs = refs[3:3 + n_w]
        yb_ref, o_ref, st0_ref = refs[3 + n_w:6 + n_w]
        g_refs = refs[6 + n_w:6 + 2 * n_w]
        st_scr = refs[6 + 2 * n_w]
        step = pl.program_id(0)
        if n_w:
            start, forward, finish = _gather_phases(w_refs, g_refs, *refs[7 + 2 * n_w:])
            pl.when(step == 0)(start)
            pl.when(step == (3 * n_steps) // 4)(forward)

        @pl.when(step == 0)
        def _():
            st_scr[...] = jnp.zeros_like(st_scr)

        tril, block = _chunk_masks()
        for sub in range(HGRN_STEP_BLOCKS):
            rows = slice(sub * HGRN_ROWS, (sub + 1) * HGRN_ROWS)
            hg_v = hg_ref[rows, :]
            p = _hgrn_prep(hg_v, lbl_ref[...])
            vv = hg_v[:, 2 * HGRN_WIDTH:3 * HGRN_WIDTH].astype(BF16)
            outs = []
            for h in range(HGRN_HEADS):
                sl = slice(h * LANES, (h + 1) * LANES)
                v_h = vv[:, sl]
                a = jnp.where(tril, _dot_nt(p["qi"][:, sl].astype(BF16), p["ki"][:, sl].astype(BF16)), 0.0)
                o = _dot(a.astype(BF16), v_h)
                upd = _dot_tn(v_h, _spread(p["kb"][:, sl].astype(BF16), block))
                st = st_scr[h]
                st0_ref[sub, h] = st
                parts = []
                for n in range(n_chunks):
                    parts.append(st.astype(BF16))
                    decay = p["e_tot"][n * HGRN_CHUNK:n * HGRN_CHUNK + 1, sl]
                    st = st * decay + upd[:, n * LANES:(n + 1) * LANES]
                st_scr[h] = st
                o = o + _dot_nt(_spread(p["qs"][:, sl].astype(BF16), block), jnp.concatenate(parts, axis=1))
                outs.append(o)
            o_ref[rows, :] = jnp.concatenate(outs, axis=1)
            normed = jnp.concatenate(
                [outs[h] * _rms(outs[h]) for h in range(HGRN_HEADS)], axis=1)
            gb = p["gb"]
            yb_ref[rows, :] = (normed * hnw_ref[...]) * (gb * _sigmoid(gb))
        if n_w:
            pl.when(step == n_steps - 1)(finish)

    anywhere = pl.BlockSpec(memory_space=pl.ANY)
    return pl.pallas_call(
        body, name="hgrn_fwd", grid=(n_steps,),
        in_specs=[_rows(step_rows, 4 * HGRN_WIDTH), _full((2, HGRN_WIDTH)), _full((1, HGRN_WIDTH))]
        + [anywhere] * n_w,
        out_specs=[_rows(step_rows, HGRN_WIDTH), _rows(step_rows, HGRN_WIDTH),
                   pl.BlockSpec((HGRN_STEP_BLOCKS, HGRN_HEADS, LANES, LANES), lambda i: (i, 0, 0, 0))]
        + [anywhere] * n_w,
        out_shape=[jax.ShapeDtypeStruct((seq, HGRN_WIDTH), F32)] * 2
        + [jax.ShapeDtypeStruct((nblk, HGRN_HEADS, LANES, LANES), F32)]
        + [_gathered_shape(h) for h in weight_halves],
        scratch_shapes=[pltpu.VMEM((HGRN_HEADS, LANES, LANES), F32)] + (_gather_scratch(n_w) if n_w else []),
        compiler_params=_cparams("arbitrary"),
    )(hg, lb_logits, hnw, *weight_halves)


def ffn_fwd(ya, yb, x, w_out, norm2_w, w_gu4, w_down, final_w, target):
    seq = x.shape[0]
    tm = ROW_TILE
    cw = w_gu4.shape[2]
    inv_d = 1.0 / D_MODEL

    def body(ya_ref, yb_ref, x_ref, wo_ref, nw_ref, wgu_ref, wd_ref, fw_ref, t_ref,
             mixed_ref, h1_ref, u2_ref, g_ref, up_ref, act_ref, dh2_ref, acc_ref):
        @pl.when(pl.program_id(0) == 0)
        def _():
            acc_ref[...] = jnp.zeros_like(acc_ref)

        mixed = jnp.concatenate([ya_ref[...], yb_ref[...]], axis=1).astype(BF16)
        mixed_ref[...] = mixed
        h1 = x_ref[...] + _dot(mixed, wo_ref[...])
        h1_ref[...] = h1
        u2 = ((h1 * _rms(h1)) * nw_ref[...]).astype(BF16)
        u2_ref[...] = u2
        g = jnp.concatenate([_dot(u2, wgu_ref[0]), _dot(u2, wgu_ref[1])], axis=1)
        up = jnp.concatenate([_dot(u2, wgu_ref[2]), _dot(u2, wgu_ref[3])], axis=1)
        g_ref[...] = g.astype(BF16)
        up_ref[...] = up.astype(BF16)
        act = ((g * _sigmoid(g)) * up).astype(BF16)
        act_ref[...] = act
        h2 = h1 + _dot(act, wd_ref[...])
        rf = _rms(h2)
        n = h2 * rf
        fw = fw_ref[...]
        err = n * fw - t_ref[...]
        dy = err * inv_d
        acc_ref[0:1, :] += jnp.sum(dy * n, axis=0, keepdims=True)
        acc_ref[1:2, :] += (0.5 * inv_d) * jnp.sum(err * err, axis=0, keepdims=True)
        dn = dy * fw
        dh2_ref[...] = rf * (dn - n * jnp.mean(dn * n, axis=-1, keepdims=True))

    half = _rows(tm, ATTN_WIDTH)
    wide = _rows(tm, D_MODEL)
    ffn = _rows(tm, FFN_HIDDEN)
    return pl.pallas_call(
        body, name="ffn_fwd", grid=(seq // tm,),
        in_specs=[half, half, wide, _weight((D_MODEL, D_MODEL)), _full((1, D_MODEL)),
                  _weight((N_CHIPS, D_MODEL, cw)), _weight((FFN_HIDDEN, D_MODEL)), _full((1, D_MODEL)), wide],
        out_specs=[wide, wide, wide, ffn, ffn, ffn, wide, _full((8, D_MODEL))],
        out_shape=[jax.ShapeDtypeStruct((seq, D_MODEL), BF16), jax.ShapeDtypeStruct((seq, D_MODEL), F32),
                   jax.ShapeDtypeStruct((seq, D_MODEL), BF16)]
        + [jax.ShapeDtypeStruct((seq, FFN_HIDDEN), BF16)] * 3
        + [jax.ShapeDtypeStruct((seq, D_MODEL), F32), jax.ShapeDtypeStruct((8, D_MODEL), F32)],
        compiler_params=_cparams("arbitrary"),
    )(ya, yb, x, w_out, norm2_w, w_gu4, w_down, final_w, target)


def _head_sum_matrix():
    i = jnp.arange(ATTN_WIDTH)
    return ((i[:, None] // HEAD_DIM) == (i[None, :] // HEAD_DIM)).astype(BF16)


def ffn_bwd(dh2, w_down, g, up, w_gu4, h1, norm2_w, w_out, ya):
    seq = h1.shape[0]
    tm = ROW_TILE
    cw = w_gu4.shape[2]
    hsum = _head_sum_matrix()

    def body(dh2_ref, wd_ref, g_ref, up_ref, w_ref, h1_ref, nw_ref, wo_ref, ya_ref, hs_ref,
             dgu_ref, dh1_ref, dya_ref, dyb_ref, delta_ref, acc_ref):
        @pl.when(pl.program_id(0) == 0)
        def _():
            acc_ref[...] = jnp.zeros_like(acc_ref)

        dh2_b = dh2_ref[...].astype(BF16)
        du2 = jnp.zeros((tm, D_MODEL), F32)
        for j in range(N_CHIPS // 2):
            dact = _dot_nt(dh2_b, wd_ref[j * cw:(j + 1) * cw, :])
            gv = g_ref[:, j * cw:(j + 1) * cw].astype(F32)
            sg = _sigmoid(gv)
            dg = (dact * up_ref[:, j * cw:(j + 1) * cw].astype(F32) * (sg * (1.0 + gv * (1.0 - sg)))).astype(BF16)
            dup = (dact * (gv * sg)).astype(BF16)
            dgu_ref[:, j * cw:(j + 1) * cw] = dg
            dgu_ref[:, FFN_HIDDEN + j * cw:FFN_HIDDEN + (j + 1) * cw] = dup
            du2 = du2 + _dot_nt(dg, w_ref[j]) + _dot_nt(dup, w_ref[N_CHIPS // 2 + j])
        h1 = h1_ref[...]
        r2 = _rms(h1)
        nh = h1 * r2
        acc_ref[0:1, :] += jnp.sum(du2 * nh, axis=0, keepdims=True)
        dn = du2 * nw_ref[...]
        dh1 = dh2_ref[...] + r2 * (dn - nh * jnp.mean(dn * nh, axis=-1, keepdims=True))
        dh1_ref[...] = dh1
        dmixed = _dot_nt(dh1.astype(BF16), wo_ref[...])
        dya = dmixed[:, :ATTN_WIDTH]
        dya_ref[...] = dya
        dyb_ref[...] = dmixed[:, ATTN_WIDTH:]
        prod = dya * ya_ref[...]
        hi = prod.astype(BF16)
        lo = (prod - hi.astype(F32)).astype(BF16)
        delta_ref[...] = _dot(hi, hs_ref[...]) + _dot(lo, hs_ref[...])

    wide = _rows(tm, D_MODEL)
    half = _rows(tm, ATTN_WIDTH)
    ffn = _rows(tm, FFN_HIDDEN)
    return pl.pallas_call(
        body, name="ffn_bwd", grid=(seq // tm,),
        in_specs=[wide, _weight((FFN_HIDDEN, D_MODEL)), ffn, ffn, _weight((N_CHIPS, D_MODEL, cw)), wide,
                  _full((1, D_MODEL)), _weight((D_MODEL, D_MODEL)), half, _full((ATTN_WIDTH, ATTN_WIDTH))],
        out_specs=[_rows(tm, 2 * FFN_HIDDEN), wide, half, half, half, _full((8, D_MODEL))],
        out_shape=[jax.ShapeDtypeStruct((seq, 2 * FFN_HIDDEN), BF16), jax.ShapeDtypeStruct((seq, D_MODEL), F32)]
        + [jax.ShapeDtypeStruct((seq, ATTN_WIDTH), F32)] * 3 + [jax.ShapeDtypeStruct((8, D_MODEL), F32)],
        compiler_params=_cparams("arbitrary"),
    )(dh2, w_down, g, up, w_gu4, h1, norm2_w, w_out, ya, hsum)


def attn_bwd(q, k, v, q16, k16, v16, dy, lse, delta, sibling_grads=()):
    seq = q.shape[0]
    cur, prev = _attn_specs()
    cur16, prev16 = _major_specs()
    whole = pl.BlockSpec((seq, LANES), lambda hp, j: (0, hp))
    n_g = len(sibling_grads)
    n_hp, n_steps = ATTN_WIDTH // LANES, seq // ATTN_STEP_ROWS
    n_in = 13

    def body(*refs):
        natural, major = refs[:5], refs[5:10]
        dy_ref, lse_ref, dl_ref = refs[10:n_in]
        dq_ref, dk_ref, dv_ref = refs[n_in + n_g:n_in + 3 + n_g]
        first_step = pl.program_id(1) == 0
        base = pl.program_id(1) * ATTN_STEP_ROWS
        masks = _head_masks()
        if n_g:
            start, finish = _sibling_exchange_phases(
                refs[n_in:n_in + n_g], refs[n_in + 3 + n_g:n_in + 3 + 2 * n_g], *refs[n_in + 3 + 2 * n_g:])
            pl.when((pl.program_id(0) == 0) & first_step)(start)

        def block(t, r, span, per_step, dilation, add):
            rows = _strided_rows(r + span * t, dilation)
            at_edge = t == 0
            q2, k2, v2 = _load_qkv(natural, major, t, r, span, dilation, per_step)
            dy2 = dy_ref[rows, :].astype(BF16)
            lse2, dl2 = lse_ref[rows, :], dl_ref[rows, :]
            valid = _window_valid(first_step & at_edge)
            zero = jnp.zeros_like(q2)
            qms, dyms, ps, dss, kms = [], [], [], [], []
            for h, mh in enumerate(masks):
                c0 = h * HEAD_DIM
                qm, dym = jnp.where(mh, q2, zero), jnp.where(mh, dy2, zero)
                s = _dot_nt(qm, k2)
                p = jnp.where(valid, jnp.exp(s - lse2[:, c0:c0 + 1]), 0.0)
                dp = _dot_nt(dym, v2)
                dss.append((p * (dp - dl2[:, c0:c0 + 1])).astype(BF16))
                ps.append(p.astype(BF16))
                qms.append(qm)
                dyms.append(dym)
                kms.append(jnp.where(mh, k2, jnp.zeros_like(k2)))
            dq = _dot(jnp.concatenate(dss, axis=1), jnp.concatenate(kms, axis=0)) * ATTN_SCALE
            p_all, ds_all = jnp.concatenate(ps, axis=0), jnp.concatenate(dss, axis=0)
            dy_all, q_all = jnp.concatenate(dyms, axis=0), jnp.concatenate(qms, axis=0)
            if dilation < ATTN_TRANSPOSED_BELOW:
                dv_full, dk_full = _dot_tn(dy_all, p_all).T, _dot_tn(q_all, ds_all).T
            else:
                dv_full, dk_full = _dot_tn(p_all, dy_all), _dot_tn(ds_all, q_all)
            here = _strided_rows(base + r + span * t, dilation)
            if add:
                dq_ref[rows, :] += dq
                dk_ref[here, :] += dk_full[ATTN_BLOCK:]
                dv_ref[here, :] += dv_full[ATTN_BLOCK:]
            else:
                dq_ref[rows, :] = dq
                dk_ref[here, :] = dk_full[ATTN_BLOCK:]
                dv_ref[here, :] = dv_full[ATTN_BLOCK:]
            back = _strided_rows(jnp.maximum(base + r + span * t - span, r), dilation)
            dk_ref[back, :] += dk_full[:ATTN_BLOCK]
            dv_ref[back, :] += dv_full[:ATTN_BLOCK]

        for index, (_, dilation) in enumerate(DILATED_PAIRS):
            _for_each_block(dilation, ATTN_BWD_UNROLL,
                            functools.partial(block, dilation=dilation, add=index > 0))
        if n_g:
            pl.when((pl.program_id(0) == n_hp - 1) & (pl.program_id(1) == n_steps - 1))(finish)

    anywhere = pl.BlockSpec(memory_space=pl.ANY)
    return pl.pallas_call(
        body, name="attn_bwd", grid=(n_hp, n_steps),
        in_specs=[cur, cur, cur, prev, prev, cur16, cur16, cur16, prev16, prev16, cur, cur, cur]
        + [anywhere] * n_g,
        out_specs=[cur, whole, whole] + [anywhere] * n_g,
        out_shape=[jax.ShapeDtypeStruct((seq, ATTN_WIDTH), F32)] * 3 + _sibling_exchange_shapes(sibling_grads),
        scratch_shapes=_sibling_exchange_scratch(n_g) if n_g else [],
        compiler_params=_cparams("arbitrary", "arbitrary"),
    )(q, k, v, k, v, q16, k16, v16, k16, v16, dy, lse, delta, *sibling_grads)


def hgrn_bwd(hg, lb_logits, hnw, o_pre, st0, dyb, chip_sums=()):
    seq = hg.shape[0]
    step_rows = HGRN_ROWS * HGRN_STEP_BLOCKS
    n_steps = seq // step_rows
    n_chunks = HGRN_ROWS // HGRN_CHUNK
    w = HGRN_WIDTH
    n_s = len(chip_sums)

    def body(*refs):
        hg_ref, lbl_ref, hnw_ref, o_ref, st0_ref, dyb_ref = refs[:6]
        dhg_ref, acc_ref = refs[6 + n_s:8 + n_s]
        dst_scr = refs[8 + 2 * n_s]
        step = pl.program_id(0)
        if n_s:
            start, finish = _chip_exchange_phases(refs[6:6 + n_s], refs[8 + n_s:8 + 2 * n_s], *refs[9 + 2 * n_s:])
            pl.when(step == 0)(start)

        @pl.when(step == 0)
        def _():
            dst_scr[...] = jnp.zeros_like(dst_scr)
            acc_ref[...] = jnp.zeros_like(acc_ref)

        tril, block = _chunk_masks()
        for sub in reversed(range(HGRN_STEP_BLOCKS)):
            rows = slice(sub * HGRN_ROWS, (sub + 1) * HGRN_ROWS)
            hg_v = hg_ref[rows, :]
            p = _hgrn_prep(hg_v, lbl_ref[...])
            vv = hg_v[:, 2 * w:3 * w].astype(BF16)
            hnw_v = hnw_ref[...]
            gb = p["gb"]
            sgg = _sigmoid(gb)
            silu_g = gb * sgg
            dyb_v = dyb_ref[rows, :]
            o_v = o_ref[rows, :]

            d_on = dyb_v * hnw_v * silu_g
            on_parts, do_parts = [], []
            for h in range(HGRN_HEADS):
                sl = slice(h * LANES, (h + 1) * LANES)
                rs = _rms(o_v[:, sl])
                on = o_v[:, sl] * rs
                on_parts.append(on)
                do_parts.append(rs * (d_on[:, sl] - on * jnp.mean(d_on[:, sl] * on, axis=-1, keepdims=True)))
            on_all = jnp.concatenate(on_parts, axis=1)
            dgb = dyb_v * on_all * hnw_v * (sgg * (1.0 + gb * (1.0 - sgg)))
            acc_ref[0:1, :] += jnp.sum(dyb_v * on_all * silu_g, axis=0, keepdims=True)

            dqf_parts, dkk_parts, db_parts, dv_parts, dbt_parts, dkbkb_parts = [], [], [], [], [], []
            for h in range(HGRN_HEADS):
                sl = slice(h * LANES, (h + 1) * LANES)
                v_h = vv[:, sl]
                do_h = do_parts[h].astype(BF16)
                qi, ki, qs, kb = p["qi"][:, sl], p["ki"][:, sl], p["qs"][:, sl], p["kb"][:, sl]
                qi_b, ki_b = qi.astype(BF16), ki.astype(BF16)
                kb_cat = _spread(kb.astype(BF16), block)
                qs_cat = _spread(qs.astype(BF16), block)
                upd = _dot_tn(v_h, kb_cat)
                st = st0_ref[sub, h]
                st_parts = []
                for n in range(n_chunks):
                    st_parts.append(st)
                    decay = p["e_tot"][n * HGRN_CHUNK:n * HGRN_CHUNK + 1, sl]
                    st = st * decay + upd[:, n * LANES:(n + 1) * LANES]
                st_cat = jnp.concatenate([s_.astype(BF16) for s_ in st_parts], axis=1)
                wgt = _dot_tn(do_h, qs_cat)
                dst = dst_scr[h]
                dst_parts = [None] * n_chunks
                dbt_rows = [None] * n_chunks
                for n in reversed(range(n_chunks)):
                    dst_parts[n] = dst.astype(BF16)
                    decay = p["e_tot"][n * HGRN_CHUNK:n * HGRN_CHUNK + 1, sl]
                    dbt_rows[n] = jnp.sum(dst * st_parts[n], axis=0, keepdims=True) * decay
                    dst = dst * decay + wgt[:, n * LANES:(n + 1) * LANES]
                dst_scr[h] = dst
                dst_cat = jnp.concatenate(dst_parts, axis=1)
                dqs = _fold(_dot(do_h, st_cat), block)
                dkb = _fold(_dot(v_h, dst_cat), block)
                dv_state = _dot_nt(kb_cat, dst_cat)
                a = jnp.where(tril, _dot_nt(qi_b, ki_b), 0.0).astype(BF16)
                da = jnp.where(tril, _dot_nt(do_h, v_h), 0.0).astype(BF16)
                dv_parts.append(_dot_tn(a, do_h) + dv_state)
                dqi = _dot(da, ki_b)
                dki = _dot_tn(da, qi_b)
                dqf_parts.append(dqi * p["e_iq"][:, sl] + dqs * p["e_b"][:, sl])
                dkk_parts.append(dki * p["e_ik"][:, sl] + dkb * p["e_bb"][:, sl])
                dkbkb = dkb * kb
                db_parts.append(dqi * qi - dki * ki + dqs * qs - dkbkb)
                dkbkb_parts.append(dkbkb)
                dbt_parts.append(_chunk_rows(dbt_rows))

            cat = lambda parts: jnp.concatenate(parts, axis=1)
            dlogf = (_chunk_cumsum(cat(db_parts), reverse=True)
                     + _chunk_row(_chunk_cumsum(cat(dkbkb_parts)), HGRN_CHUNK - 1) + cat(dbt_parts))
            sq, qb = p["sq"], hg_v[:, :w]
            dqb = cat(dqf_parts) * (sq * (1.0 + qb * (1.0 - sq)))
            df = dlogf / p["f"] - cat(dkk_parts)
            sg, lb = p["sg"], p["lb"]
            dfb = df * (1.0 - lb) * sg * (1.0 - sg)
            acc_ref[1:2, :] += jnp.sum(df * (1.0 - sg), axis=0, keepdims=True)
            dhg_ref[rows, :] = jnp.concatenate([dqb, dfb, cat(dv_parts), dgb], axis=1).astype(BF16)
        if n_s:
            pl.when(step == n_steps - 1)(finish)

    rev = lambda i: (n_steps - 1 - i, 0)
    anywhere = pl.BlockSpec(memory_space=pl.ANY)
    return pl.pallas_call(
        body, name="hgrn_bwd", grid=(n_steps,),
        in_specs=[pl.BlockSpec((step_rows, 4 * w), rev), _full((2, w)), _full((1, w)),
                  pl.BlockSpec((step_rows, w), rev),
                  pl.BlockSpec((HGRN_STEP_BLOCKS, HGRN_HEADS, LANES, LANES), lambda i: (n_steps - 1 - i, 0, 0, 0)),
                  pl.BlockSpec((step_rows, w), rev)] + [anywhere] * n_s,
        out_specs=[pl.BlockSpec((step_rows, 4 * w), rev), _full((8, w))] + [anywhere] * n_s,
        out_shape=[jax.ShapeDtypeStruct((seq, 4 * w), BF16), jax.ShapeDtypeStruct((8, w), F32)]
        + [jax.ShapeDtypeStruct(s.shape, s.dtype) for s in chip_sums],
        scratch_shapes=[pltpu.VMEM((HGRN_HEADS, LANES, LANES), F32)] + (_chip_exchange_scratch(n_s) if n_s else []),
        compiler_params=_cparams("arbitrary"),
    )(hg, lb_logits, hnw, o_pre, st0, dyb, *chip_sums)


def in_bwd(dq, dk, dv, dhg, cos_t, sin_t, w_in4, x, norm1_w, dh1):
    seq = x.shape[0]
    tm = ROW_TILE
    cw = w_in4.shape[2]

    def body(dq_ref, dk_ref, dv_ref, dhg_ref, cos_ref, sin_ref, w_ref,
             x_ref, nw_ref, dh1_ref, dproj_ref, dx_ref, acc_ref):
        @pl.when(pl.program_id(0) == 0)
        def _():
            acc_ref[...] = jnp.zeros_like(acc_ref)

        cos, sin = cos_ref[...], sin_ref[...]
        dqa = _rotary_bwd(dq_ref[...], cos, sin)
        dka = _rotary_bwd(dk_ref[...], cos, sin)
        dproj = jnp.concatenate(
            [jnp.concatenate([dqa, dka, dv_ref[...]], axis=1).astype(BF16), dhg_ref[...]], axis=1)
        dproj_ref[...] = dproj
        du = _dot_nt(dproj[:, :cw], w_ref[0])
        for j in range(1, N_CHIPS):
            du = du + _dot_nt(dproj[:, j * cw:(j + 1) * cw], w_ref[j])
        xv = x_ref[...]
        r1 = _rms(xv)
        nx = xv * r1
        acc_ref[0:1, :] += jnp.sum(du * nx, axis=0, keepdims=True)
        dn = du * nw_ref[...]
        dx_ref[...] = dh1_ref[...] + r1 * (dn - nx * jnp.mean(dn * nx, axis=-1, keepdims=True))

    half = _rows(tm, ATTN_WIDTH)
    wide = _rows(tm, D_MODEL)
    return pl.pallas_call(
        body, name="in_bwd", grid=(seq // tm,),
        in_specs=[half] * 3 + [_rows(tm, 4 * HGRN_WIDTH), _rows(tm, LANES), _rows(tm, LANES),
                               _weight((N_CHIPS, D_MODEL, cw)), wide, _full((1, D_MODEL)), wide],
        out_specs=[_rows(tm, IN_PROJ_WIDTH), wide, _full((8, D_MODEL))],
        out_shape=[jax.ShapeDtypeStruct((seq, IN_PROJ_WIDTH), BF16), jax.ShapeDtypeStruct((seq, D_MODEL), F32),
                   jax.ShapeDtypeStruct((8, D_MODEL), F32)],
        compiler_params=_cparams("arbitrary"),
    )(dq, dk, dv, dhg, cos_t, sin_t, w_in4, x, norm1_w, dh1)


def weight_grad(a, b, col_block, name, group=1, small_pack=None):
    seq, kdim = a.shape
    ndim = b.shape[1]
    nj = ndim // col_block
    tk = min(1024, seq)
    hosting = small_pack is not None
    n_j, n_t = nj // group, seq // tk

    def body(*refs):
        a_ref, b_ref = refs[:2]
        o_ref = refs[3] if hosting else refs[2]
        if hosting:
            start, finish = _pack_gather_phases(refs[2], refs[4], *refs[5:])
            pl.when((pl.program_id(0) == 0) & (pl.program_id(1) == 0))(start)

        @pl.when(pl.program_id(1) == 0)
        def _():
            o_ref[...] = jnp.zeros_like(o_ref)

        acc = _dot_tn(a_ref[...].astype(BF16), b_ref[...].astype(BF16))
        for i in range(group):
            o_ref[i] += acc[:, i * col_block:(i + 1) * col_block]
        if hosting:
            pl.when((pl.program_id(0) == n_j - 1) & (pl.program_id(1) == n_t - 1))(finish)

    anywhere = pl.BlockSpec(memory_space=pl.ANY)
    out = pl.pallas_call(
        body, name=name, grid=(n_j, n_t),
        in_specs=[pl.BlockSpec((tk, kdim), lambda j, t: (t, 0)),
                  pl.BlockSpec((tk, group * col_block), lambda j, t: (t, j))] + [anywhere] * hosting,
        out_specs=[pl.BlockSpec((group, kdim, col_block), lambda j, t: (j, 0, 0))] + [anywhere] * hosting,
        out_shape=[jax.ShapeDtypeStruct((nj, kdim, col_block), F32)]
        + ([jax.ShapeDtypeStruct((N_DEV,) + small_pack.shape, F32)] if hosting else []),
        scratch_shapes=[pltpu.SemaphoreType.DMA((N_DEV - 1,)), pltpu.SemaphoreType.DMA((N_DEV - 1,)),
                        pltpu.SemaphoreType.DMA] if hosting else [],
        compiler_params=_cparams("arbitrary", "arbitrary"),
    )(a, b, *([small_pack] if hosting else []))
    return out if hosting else out[0]


def _sibling_exchange_phases(g_refs, out_refs, send_sems, recv_sems):
    x, y, cc = _mesh_pos()

    def copies():
        return [pltpu.make_async_remote_copy(
            src_ref=g_refs[i].at[j, 1 - cc], dst_ref=out_refs[i].at[j],
            send_sem=send_sems.at[i * N_CHIPS + j], recv_sem=recv_sems.at[i * N_CHIPS + j],
            device_id=(x, y, 1 - cc), device_id_type=MESH_ID)
            for i in range(len(g_refs)) for j in range(N_CHIPS)]

    def start():
        for cp in copies():
            cp.start()

    def finish():
        for cp in copies():
            cp.wait_recv()
        for cp in copies():
            cp.wait_send()

    return start, finish


def _sibling_exchange_scratch(n):
    return [pltpu.SemaphoreType.DMA((n * N_CHIPS,)), pltpu.SemaphoreType.DMA((n * N_CHIPS,))]


def _sibling_exchange_shapes(grads):
    return [jax.ShapeDtypeStruct((N_CHIPS,) + g.shape[2:], g.dtype) for g in grads]


def exchange_with_sibling(grads, name):
    n = len(grads)

    def body(*refs):
        start, finish = _sibling_exchange_phases(refs[:n], refs[n:2 * n], refs[2 * n], refs[2 * n + 1])
        start()
        finish()

    return pl.pallas_call(
        body, name=name,
        in_specs=[pl.BlockSpec(memory_space=pl.ANY)] * n,
        out_specs=[pl.BlockSpec(memory_space=pl.ANY)] * n,
        out_shape=_sibling_exchange_shapes(grads),
        scratch_shapes=_sibling_exchange_scratch(n),
    )(*grads)


def add_own_half(grad, recv, name):
    _, _, r, c = grad.shape
    tr = r // 2 if r % 32 == 0 else r

    def body(cc_ref, g_ref, r_ref, o_ref):
        o_ref[...] = (g_ref[...] + r_ref[...]).astype(BF16)

    grid_spec = pltpu.PrefetchScalarGridSpec(
        num_scalar_prefetch=1, grid=(N_CHIPS, r // tr),
        in_specs=[pl.BlockSpec((None, None, tr, c), lambda j, t, cc: (j, cc[0], t, 0)),
                  pl.BlockSpec((None, tr, c), lambda j, t, cc: (j, t, 0))],
        out_specs=pl.BlockSpec((None, tr, c), lambda j, t, cc: (j, t, 0)))
    cc = lax.axis_index("c").astype(jnp.int32).reshape(1)
    return pl.pallas_call(
        body, name=name, grid_spec=grid_spec,
        out_shape=jax.ShapeDtypeStruct((N_CHIPS, r, c), BF16),
        compiler_params=_cparams("parallel", "parallel"),
    )(cc, grad, recv)


def _chip_exchange_phases(s_refs, out_refs, send_sems, recv_sems):
    n = len(s_refs)
    x, y, cc = _mesh_pos()
    my_chip = 2 * x + y
    chips = [(1 - x, y), (x, 1 - y), (1 - x, 1 - y)]

    def outgoing():
        return [pltpu.make_async_remote_copy(
            src_ref=s_refs[i].at[2 * px + py], dst_ref=out_refs[i].at[my_chip],
            send_sem=send_sems.at[3 * i + j], recv_sem=recv_sems.at[3 * i + j],
            device_id=(px, py, cc), device_id_type=MESH_ID)
            for i in range(n) for j, (px, py) in enumerate(chips)]

    def start():
        for cp in outgoing():
            cp.start()

    def finish():
        for i in range(n):
            for j, (px, py) in enumerate(chips):
                pltpu.make_async_remote_copy(
                    src_ref=s_refs[i].at[my_chip], dst_ref=out_refs[i].at[2 * px + py],
                    send_sem=send_sems.at[3 * i + j], recv_sem=recv_sems.at[3 * i + j],
                    device_id=(px, py, cc), device_id_type=MESH_ID).wait_recv()
        for cp in outgoing():
            cp.wait_send()

    return start, finish


def _chip_exchange_scratch(n):
    return [pltpu.SemaphoreType.DMA((3 * n,)), pltpu.SemaphoreType.DMA((3 * n,))]


def sum_chips(sums, parts, name):
    _, r, c = parts.shape
    tr = r // 2 if r % 32 == 0 else r

    def body(idx_ref, s_ref, p1_ref, p2_ref, p3_ref, o_ref):
        o_ref[...] = ((s_ref[...].astype(F32) + p1_ref[...].astype(F32))
                      + p2_ref[...].astype(F32)) + p3_ref[...].astype(F32)

    def pick(k):
        return pl.BlockSpec((None, tr, c), lambda t, idx: (idx[k], t, 0))

    x, y = lax.axis_index("x"), lax.axis_index("y")
    idx = jnp.stack([2 * x + y, 2 * (1 - x) + y, 2 * x + (1 - y), 2 * (1 - x) + (1 - y)]).astype(jnp.int32)
    grid_spec = pltpu.PrefetchScalarGridSpec(
        num_scalar_prefetch=1, grid=(r // tr,),
        in_specs=[pick(0), pick(1), pick(2), pick(3)],
        out_specs=pl.BlockSpec((tr, c), lambda t, idx: (t, 0)))
    return pl.pallas_call(
        body, name=name, grid_spec=grid_spec,
        out_shape=jax.ShapeDtypeStruct((r, c), F32),
        compiler_params=_cparams("parallel"),
    )(idx, sums, parts, parts, parts)


def share_with_sibling(halves, name):
    n = len(halves)

    def body(*refs):
        h_refs, out_refs = refs[:n], refs[n:2 * n]
        send_sems, recv_sems = refs[2 * n], refs[2 * n + 1]
        x, y, cc = _mesh_pos()
        copies = [pltpu.make_async_remote_copy(
            src_ref=h_refs[i], dst_ref=out_refs[i],
            send_sem=send_sems.at[i], recv_sem=recv_sems.at[i],
            device_id=(x, y, 1 - cc), device_id_type=MESH_ID) for i in range(n)]
        for cp in copies:
            cp.start()
        for cp in copies:
            cp.wait_recv()
        for cp in copies:
            cp.wait_send()

    return pl.pallas_call(
        body, name=name,
        in_specs=[pl.BlockSpec(memory_space=pl.ANY)] * n,
        out_specs=[pl.BlockSpec(memory_space=pl.ANY)] * n,
        out_shape=[jax.ShapeDtypeStruct(h.shape, h.dtype) for h in halves],
        scratch_shapes=[pltpu.SemaphoreType.DMA((n,)), pltpu.SemaphoreType.DMA((n,))],
    )(*halves)


def _adam_update(w, g, m, v):
    m = ADAM_B1 * m + (1.0 - ADAM_B1) * g
    v = ADAM_B2 * v + (1.0 - ADAM_B2) * (g * g)
    m_hat = m / (1.0 - ADAM_B1 ** ADAM_STEP)
    v_hat = v / (1.0 - ADAM_B2 ** ADAM_STEP)
    delta = -ADAM_LR * (m_hat / (jnp.sqrt(v_hat) + ADAM_EPS) + ADAM_WD * w)
    return delta, m, v


ADAMW_STEPS = 8


def adamw(ws, g_mine, g_sibling, ms, vs, name, chip_sums=()):
    n, n_s = len(ws), len(chip_sums)
    per_half = ADAMW_STEPS // 2

    def body(*refs):
        cc_ref = refs[0]
        ins = refs[1:1 + 5 * n]
        outs = refs[1 + 5 * n + n_s:1 + 9 * n + n_s]
        step = pl.program_id(0)
        if n_s:
            start, finish = _chip_exchange_phases(
                refs[1 + 5 * n:1 + 5 * n + n_s], refs[1 + 9 * n + n_s:1 + 9 * n + 2 * n_s],
                *refs[1 + 9 * n + 2 * n_s:])
            pl.when(step == 0)(start)
        mine = (step // per_half) == cc_ref[0]
        for i in range(n):
            w_ref, ga_ref, gb_ref, m_ref, v_ref = ins[5 * i:5 * i + 5]
            g_ref, d_ref, nm_ref, nv_ref = outs[4 * i:4 * i + 4]
            g = jnp.where(mine, ga_ref[...], gb_ref[...])
            g_ref[...] = g
            d, nm, nv = _adam_update(w_ref[...], g, m_ref[...], v_ref[...])
            d_ref[...] = d
            nm_ref[...] = nm
            nv_ref[...] = nv
        if n_s:
            pl.when(step == ADAMW_STEPS - 1)(finish)

    in_specs, out_specs, out_shape, operands = [], [], [], []
    for w, ga, gb, m, v in zip(ws, g_mine, g_sibling, ms, vs):
        r, c = w.shape
        tr = r // ADAMW_STEPS
        full = pl.BlockSpec((tr, c), lambda t, cc: (t, 0))
        part = pl.BlockSpec((tr, c), lambda t, cc: (t % per_half, 0))
        in_specs += [full, part, part, full, full]
        out_specs += [full] * 4
        out_shape += [jax.ShapeDtypeStruct((r, c), F32)] * 4
        operands += [w, ga, gb, m, v]
    anywhere = pl.BlockSpec(memory_space=pl.ANY)
    grid_spec = pltpu.PrefetchScalarGridSpec(
        num_scalar_prefetch=1, grid=(ADAMW_STEPS,),
        in_specs=in_specs + [anywhere] * n_s, out_specs=out_specs + [anywhere] * n_s,
        scratch_shapes=_chip_exchange_scratch(n_s) if n_s else [])
    cc = lax.axis_index("c").astype(jnp.int32).reshape(1)
    res = pl.pallas_call(
        body, name=name, grid_spec=grid_spec,
        out_shape=out_shape + [jax.ShapeDtypeStruct(s.shape, s.dtype) for s in chip_sums],
        compiler_params=_cparams("arbitrary"),
    )(cc, *operands, *chip_sums)
    per_shard = [tuple(res[4 * i:4 * i + 4]) for i in range(n)]
    return (per_shard, list(res[4 * n:])) if n_s else per_shard


def _pack_gather_phases(p_ref, out_ref, send_sems, recv_sems, local_sem):
    x, y, cc = _mesh_pos()
    me = 4 * x + 2 * y + cc
    flips = [(fx, fy, fc) for fx in (0, 1) for fy in (0, 1) for fc in (0, 1)][1:]

    def copy(k, row):
        fx, fy, fc = flips[k]
        return pltpu.make_async_remote_copy(
            src_ref=p_ref, dst_ref=out_ref.at[row],
            send_sem=send_sems.at[k], recv_sem=recv_sems.at[k],
            device_id=(x ^ fx, y ^ fy, cc ^ fc), device_id_type=MESH_ID)

    def local():
        return pltpu.make_async_copy(p_ref, out_ref.at[me], local_sem)

    def start():
        local().start()
        for k in range(len(flips)):
            copy(k, me).start()

    def finish():
        for k, (fx, fy, fc) in enumerate(flips):
            copy(k, 4 * (x ^ fx) + 2 * (y ^ fy) + (cc ^ fc)).wait_recv()
        for k in range(len(flips)):
            copy(k, me).wait_send()
        local().wait()

    return start, finish


def small_update(gathered, wpack, mpack, vpack):
    hw = HGRN_WIDTH

    def body(g_ref, w_ref, m_ref, v_ref, go_ref, d_ref, nm_ref, nv_ref, loss_ref):
        g = g_ref[0]
        for d in range(1, N_DEV):
            g = g + g_ref[d]
        wv = w_ref[...]
        a0, a1 = wv[4:5, :hw], wv[4:5, hw:]
        mx = jnp.maximum(a0, a1)
        e0, e1 = jnp.exp(a0 - mx), jnp.exp(a1 - mx)
        lb = e0 / (e0 + e1)
        dl = g[4:5, :hw] * lb * (1.0 - lb)
        row = lax.broadcasted_iota(jnp.int32, g.shape, 0)
        lb_row = jnp.concatenate([dl, -dl], axis=1)
        grads = jnp.where(row == 4, lb_row, jnp.where(row < 4, g, 0.0))
        go_ref[...] = grads
        d, nm, nv = _adam_update(wv, grads, m_ref[...], v_ref[...])
        d_ref[...] = d
        nm_ref[...] = nm
        nv_ref[...] = nv
        loss_ref[...] = jnp.zeros((8, LANES), F32) + jnp.sum(g[5:6, :])

    vm = pl.BlockSpec(memory_space=pltpu.VMEM)
    return pl.pallas_call(
        body, name="small_update",
        in_specs=[vm] * 4, out_specs=[vm] * 5,
        out_shape=[jax.ShapeDtypeStruct(wpack.shape, F32)] * 4 + [jax.ShapeDtypeStruct((8, LANES), F32)],
    )(gathered, wpack, mpack, vpack)


def _pack_small(n1, n2, fn, hn, lbl):
    z = jnp.zeros((1, D_MODEL - HGRN_WIDTH), F32)
    rows = [n1.reshape(1, D_MODEL), n2.reshape(1, D_MODEL), fn.reshape(1, D_MODEL),
            jnp.concatenate([hn.reshape(1, HGRN_WIDTH), z], axis=1), lbl.reshape(1, 2 * HGRN_WIDTH),
            jnp.zeros((3, D_MODEL), F32)]
    return jnp.concatenate(rows, axis=0)


def _unpack_small(pack):
    return (pack[0:1], pack[4].reshape(2, HGRN_WIDTH), pack[3:4, :HGRN_WIDTH], pack[1:2], pack[2])


def kernel(x, norm1_w, w_in, lb_logits, hgrn_norm_w, w_out, norm2_w, w_gate_up, w_down, final_norm_w, loss_target, m_norm1_w, m_w_in, m_lb_logits, m_hgrn_norm_w, m_w_out, m_norm2_w, m_w_gate_up, m_w_down, m_final_norm_w, v_norm1_w, v_w_in, v_lb_logits, v_hgrn_norm_w, v_w_out, v_norm2_w, v_w_gate_up, v_w_down, v_final_norm_w):
    seq = x.shape[1]
    xs = x.reshape(seq, D_MODEL)
    target = loss_target.reshape(seq, D_MODEL)
    shards = {"w_in": w_in[0], "w_out": w_out[0], "w_gu": w_gate_up[0], "w_down": w_down[0]}

    cast = {k: cast_bf16(w, "cast_" + k) for k, w in shards.items()}
    w_in4 = allgather_halves(cast["w_in"], "gather_w_in").reshape(N_CHIPS, D_MODEL, -1)

    cos_t, sin_t = _rope_tables(seq)
    fw = final_norm_w.reshape(1, D_MODEL)

    qr, kr, va, hg, u, q16, k16, v16, g_out, g_down = in_proj(
        xs, norm1_w, w_in4, cos_t, sin_t, [cast["w_out"], cast["w_down"]])
    ya, lse = attn_fwd(qr, kr, va, q16, k16, v16)
    yb, o_pre, st0, g_gu = hgrn_fwd(hg, lb_logits, hgrn_norm_w, [cast["w_gu"]])
    w_out_f = g_out.reshape(D_MODEL, D_MODEL)
    w_gu4 = g_gu.reshape(N_CHIPS, D_MODEL, -1)
    w_down_f = g_down.reshape(FFN_HIDDEN, D_MODEL)
    mixed, h1, u2, g, up, act, dh2, acc_fin = ffn_fwd(
        ya, yb, xs, w_out_f, norm2_w, w_gu4, w_down_f, fw, target)

    cw_in, cw_gu = w_in4.shape[2], w_gu4.shape[2]
    dgu, dh1, dya, dyb, delta, acc_n2 = ffn_bwd(dh2, w_down_f, g, up, w_gu4, h1, norm2_w, w_out_f, ya)
    early = [
        weight_grad(mixed, dh1, D_MODEL, "wgrad_out").reshape(N_CHIPS, 2, D_MODEL // 8, D_MODEL),
        weight_grad(u2, dgu, cw_gu, "wgrad_gu", group=2).reshape(N_CHIPS, 2, D_MODEL // 2, cw_gu),
        weight_grad(act, dh2, D_MODEL, "wgrad_down").reshape(N_CHIPS, 2, FFN_HIDDEN // 8, D_MODEL),
    ]
    early_names = ["out", "gu", "down"]
    dq, dk, dv, *early_recv = attn_bwd(qr, kr, va, q16, k16, v16, dya, lse, delta, early)
    early_sums = [add_own_half(gr, rc, "add_half_" + nm) for gr, rc, nm in zip(early, early_recv, early_names)]
    dhg, acc_hg, *early_parts = hgrn_bwd(hg, lb_logits, hgrn_norm_w, o_pre, st0, dyb, early_sums)
    dproj, dx, acc_n1 = in_bwd(dq, dk, dv, dhg, cos_t, sin_t, w_in4, xs, norm1_w, dh1)
    z512 = jnp.zeros((1, D_MODEL - HGRN_WIDTH), F32)
    gpack = jnp.concatenate([
        acc_n1[0:1], acc_n2[0:1], acc_fin[0:1],
        jnp.concatenate([acc_hg[0:1], z512], axis=1), jnp.concatenate([acc_hg[1:2], z512], axis=1),
        acc_fin[1:2], jnp.zeros((2, D_MODEL), F32)], axis=0)
    g_in, gathered_packs = weight_grad(u, dproj, cw_in, "wgrad_in", group=2, small_pack=gpack)
    late = [g_in.reshape(N_CHIPS, 2, D_MODEL // 2, cw_in)]
    late_recv = exchange_with_sibling(late, "grad_exchange_sibling_late")
    late_sums = [add_own_half(late[0], late_recv[0], "add_half_in")]

    early_halves = [sum_chips(s, p, "sum_chips_" + nm) for s, p, nm in zip(early_sums, early_parts, early_names)]
    early_others = share_with_sibling(early_halves, "grad_share_sibling_early")
    early_keys = ["w_out", "w_gu", "w_down"]
    moments = {"w_in": (m_w_in, v_w_in), "w_out": (m_w_out, v_w_out),
               "w_gu": (m_w_gate_up, v_w_gate_up), "w_down": (m_w_down, v_w_down)}
    early_updates, late_parts = adamw(
        [shards[k] for k in early_keys], early_halves, early_others,
        [moments[k][0][0] for k in early_keys], [moments[k][1][0] for k in early_keys],
        "adamw_early", chip_sums=late_sums)
    late_halves = [sum_chips(late_sums[0], late_parts[0], "sum_chips_in")]
    late_others = share_with_sibling(late_halves, "grad_share_sibling_late")
    late_updates = adamw([shards["w_in"]], late_halves, late_others,
                         [moments["w_in"][0][0]], [moments["w_in"][1][0]], "adamw_in")
    big = {k: tuple(t[None] for t in upd) for k, upd in zip(early_keys + ["w_in"], early_updates + late_updates)}

    wpack = _pack_small(norm1_w, norm2_w, final_norm_w, hgrn_norm_w, lb_logits)
    mpack = _pack_small(m_norm1_w, m_norm2_w, m_final_norm_w, m_hgrn_norm_w, m_lb_logits)
    vpack = _pack_small(v_norm1_w, v_norm2_w, v_final_norm_w, v_hgrn_norm_w, v_lb_logits)
    gs, ds, nms, nvs, loss8 = small_update(gathered_packs, wpack, mpack, vpack)
    loss = loss8[0, 0]

    def assemble(small_pack, idx):
        n1, lbl, hn, n2, fn = _unpack_small(small_pack)
        return (n1, big["w_in"][idx], lbl, hn, big["w_out"][idx], n2, big["w_gu"][idx], big["w_down"][idx], fn)

    return (loss, dx.reshape(x.shape), *assemble(gs, 0), *assemble(ds, 1), *assemble(nms, 2), *assemble(nvs, 3))
```

```python
import functools

import jax
import jax.numpy as jnp
from jax import lax
from jax.experimental import pallas as pl
from jax.experimental.pallas import tpu as pltpu

F32 = jnp.float32
BF16 = jnp.bfloat16

D_MODEL = 1024
ATTN_WIDTH = 512
HEAD_DIM = 64
DILATED_PAIRS = ((128, 1), (512, 4), (2048, 16))
ATTN_BLOCK = 128
ROPE_THETA = 10000.0
HGRN_WIDTH = 512
HGRN_CHUNK = 16
HGRN_HEADS = 4
IN_PROJ_WIDTH = 3584
FFN_HIDDEN = 2816
NORM_EPS = 1e-6
ATTN_SCALE = HEAD_DIM ** -0.5
N_CHIPS = 4
N_DEV = 8

ADAM_LR = 0.001
ADAM_B1 = 0.9
ADAM_B2 = 0.999
ADAM_EPS = 1e-08
ADAM_WD = 0.01
ADAM_STEP = 10

LANES = 128
HGRN_ROWS = 128
HGRN_STEP_BLOCKS = 2
ROW_TILE = 256
ATTN_STEP_ROWS = 2048
ATTN_FWD_UNROLL = 8
ATTN_BWD_UNROLL = 8
ATTN_MAJOR_DILATION = 16
VMEM_LIMIT = 56 * 1024 * 1024
NEG_BIG = -1e30
MESH_ID = pl.DeviceIdType.MESH


def _cparams(*sem):
    return pltpu.CompilerParams(dimension_semantics=tuple(sem), vmem_limit_bytes=VMEM_LIMIT)


def _dot(a, b):
    return jnp.dot(a, b, preferred_element_type=F32)


def _dot_nt(a, b):
    return lax.dot_general(a, b, (((1,), (1,)), ((), ())), preferred_element_type=F32)


def _dot_tn(a, b):
    return lax.dot_general(a, b, (((0,), (0,)), ((), ())), preferred_element_type=F32)


def _sigmoid(x):
    return 1.0 / (1.0 + jnp.exp(-x))


def _full(shape):
    n = len(shape)
    return pl.BlockSpec(shape, lambda *_: (0,) * n)


def _weight(shape):
    n = len(shape)
    return pl.BlockSpec(shape, lambda *_: (0,) * n, pipeline_mode=pl.Buffered(1))


def _rows(tm, width):
    return pl.BlockSpec((tm, width), lambda i: (i, 0))


def _swap32(x):
    lane = lax.broadcasted_iota(jnp.int32, x.shape, 1)
    first = (lane % HEAD_DIM) < (HEAD_DIM // 2)
    return jnp.where(first, pltpu.roll(x, LANES - 32, axis=1), pltpu.roll(x, 32, axis=1))


def _rotary_fwd(x, cos, sin_signed):
    parts = []
    for j in range(x.shape[1] // LANES):
        xc = x[:, j * LANES:(j + 1) * LANES]
        parts.append(xc * cos + _swap32(xc) * sin_signed)
    return jnp.concatenate(parts, axis=1)


def _rotary_bwd(dy, cos, sin_signed):
    parts = []
    for j in range(dy.shape[1] // LANES):
        dc = dy[:, j * LANES:(j + 1) * LANES]
        parts.append(dc * cos + _swap32(dc * sin_signed))
    return jnp.concatenate(parts, axis=1)


def _rope_tables(seq):
    half = HEAD_DIM // 2
    inv_freq = ROPE_THETA ** (-jnp.arange(half, dtype=F32) / half)
    ang = jnp.arange(seq, dtype=F32)[:, None] * inv_freq[None, :]
    cos, sin = jnp.cos(ang), jnp.sin(ang)
    cos_t = jnp.tile(cos, (1, LANES // half))
    sin_t = jnp.tile(jnp.concatenate([-sin, sin], axis=1), (1, LANES // HEAD_DIM))
    return cos_t, sin_t


def cast_bf16(w, name):
    r, c = w.shape
    half = r // 2

    def body(w_ref, o_ref):
        o_ref[...] = w_ref[...].astype(BF16)

    return pl.pallas_call(
        body, name=name, grid=(2,),
        in_specs=[pl.BlockSpec((half, c), lambda i: (i, 0))],
        out_specs=pl.BlockSpec((None, half, c), lambda i: (i, 0, 0)),
        out_shape=jax.ShapeDtypeStruct((2, half, c), BF16),
        compiler_params=_cparams("parallel"),
    )(w)


def _mesh_pos():
    return lax.axis_index("x"), lax.axis_index("y"), lax.axis_index("c")


GATHER_COPIES = 7


def _gather_phases(x_refs, out_refs, send_sems, recv_sems, local_sems):
    n = len(x_refs)
    x, y, cc = _mesh_pos()
    me, sibling = (x, y, cc), (x, y, 1 - cc)
    chips = [(1 - x, y), (x, 1 - y), (1 - x, 1 - y)]

    def rows(i, px, py, pc):
        return out_refs[i].at[4 * px + 2 * py + pc]

    def copy(i, k, block, to, src=None):
        return pltpu.make_async_remote_copy(
            src_ref=rows(i, *block) if src is None else src, dst_ref=rows(i, *block),
            send_sem=send_sems.at[GATHER_COPIES * i + k], recv_sem=recv_sems.at[GATHER_COPIES * i + k],
            device_id=to, device_id_type=MESH_ID)

    def local(i):
        return pltpu.make_async_copy(x_refs[i].at[cc], rows(i, *me), local_sems.at[i])

    def first(i):
        mine = x_refs[i].at[cc]
        return [copy(i, 0, me, sibling, src=mine)] + [
            copy(i, 1 + j, me, (*chip, cc), src=mine) for j, chip in enumerate(chips)]

    def passed(i):
        return [copy(i, 4 + j, (*chip, cc), sibling) for j, chip in enumerate(chips)]

    def start():
        for i in range(n):
            local(i).start()
            for cp in first(i):
                cp.start()

    def forward():
        for i in range(n):
            onward = passed(i)
            for j, chip in enumerate(chips):
                copy(i, 1 + j, (*chip, cc), me).wait_recv()
                onward[j].start()

    def finish():
        for i in range(n):
            copy(i, 0, sibling, me).wait_recv()
            for j, chip in enumerate(chips):
                copy(i, 4 + j, (*chip, 1 - cc), me).wait_recv()
            for cp in first(i) + passed(i):
                cp.wait_send()
            local(i).wait()

    return start, forward, finish


def _gather_scratch(n):
    return [pltpu.SemaphoreType.DMA((GATHER_COPIES * n,)), pltpu.SemaphoreType.DMA((GATHER_COPIES * n,)),
            pltpu.SemaphoreType.DMA((n,))]


def _gathered_shape(halves):
    return jax.ShapeDtypeStruct((N_DEV,) + halves.shape[1:], halves.dtype)


def allgather_halves(halves, name):
    def body(x_ref, out_ref, send_sems, recv_sems, local_sems):
        start, forward, finish = _gather_phases([x_ref], [out_ref], send_sems, recv_sems, local_sems)
        start()
        forward()
        finish()

    return pl.pallas_call(
        body, name=name,
        in_specs=[pl.BlockSpec(memory_space=pl.ANY)],
        out_specs=pl.BlockSpec(memory_space=pl.ANY),
        out_shape=_gathered_shape(halves),
        scratch_shapes=_gather_scratch(1),
    )(halves)


def _rms(x):
    return lax.rsqrt(jnp.mean(x * x, axis=-1, keepdims=True) + NORM_EPS)


def in_proj(x, norm1_w, w_in4, cos_t, sin_t, weight_halves=()):
    seq = x.shape[0]
    tm = ROW_TILE
    cw = w_in4.shape[2]
    n_w = len(weight_halves)
    steps = seq // tm
    major = ATTN_MAJOR_DILATION
    slabs = ATTN_WIDTH // LANES

    def body(*refs):
        x_ref, nw_ref, w_ref, cos_ref, sin_ref = refs[:5]
        q_ref, k_ref, v_ref, hg_ref, u_ref = refs[5 + n_w:10 + n_w]
        major_refs = refs[10 + n_w:13 + n_w]
        slab_scr = refs[13 + 2 * n_w]
        step = pl.program_id(0)
        if n_w:
            start, forward, finish = _gather_phases(
                refs[5:5 + n_w], refs[13 + n_w:13 + 2 * n_w], *refs[14 + 2 * n_w:])
            pl.when(step == 0)(start)
            pl.when(step == (3 * steps) // 4)(forward)
        xv = x_ref[...]
        u = ((xv * _rms(xv)) * nw_ref[...]).astype(BF16)
        u_ref[...] = u
        proj = jnp.concatenate([_dot(u, w_ref[j]) for j in range(N_CHIPS)], axis=1)
        cos, sin = cos_ref[...], sin_ref[...]
        a = ATTN_WIDTH
        qkv = (_rotary_fwd(proj[:, :a], cos, sin), _rotary_fwd(proj[:, a:2 * a], cos, sin), proj[:, 2 * a:3 * a])
        for ref, val in zip((q_ref, k_ref, v_ref), qkv):
            ref[...] = val
        hg_ref[...] = proj[:, 3 * a:]
        for idx, val in enumerate(qkv):
            for s in range(slabs):
                slab_scr[idx, s] = val[:, s * LANES:(s + 1) * LANES]
        for idx, out in enumerate(major_refs):
            for r in range(major):
                for s in range(slabs):
                    out[r, :, s * LANES:(s + 1) * LANES] = (
                        slab_scr.at[idx, s][pl.ds(r, tm // major, stride=major), :].astype(BF16))
        if n_w:
            pl.when(step == steps - 1)(finish)

    anywhere = pl.BlockSpec(memory_space=pl.ANY)
    major_spec = pl.BlockSpec((major, tm // major, ATTN_WIDTH), lambda i: (0, i, 0))
    return pl.pallas_call(
        body, name="in_proj", grid=(steps,),
        in_specs=[_rows(tm, D_MODEL), _full((1, D_MODEL)), _weight((N_CHIPS, D_MODEL, cw)),
                  _rows(tm, LANES), _rows(tm, LANES)] + [anywhere] * n_w,
        out_specs=[_rows(tm, ATTN_WIDTH)] * 3 + [_rows(tm, 4 * HGRN_WIDTH), _rows(tm, D_MODEL)]
        + [major_spec] * 3 + [anywhere] * n_w,
        out_shape=[jax.ShapeDtypeStruct((seq, ATTN_WIDTH), F32)] * 3
        + [jax.ShapeDtypeStruct((seq, 4 * HGRN_WIDTH), F32), jax.ShapeDtypeStruct((seq, D_MODEL), BF16)]
        + [jax.ShapeDtypeStruct((major, seq // major, ATTN_WIDTH), BF16)] * 3
        + [_gathered_shape(h) for h in weight_halves],
        scratch_shapes=[pltpu.VMEM((3, slabs, tm, LANES), F32)] + (_gather_scratch(n_w) if n_w else []),
        compiler_params=_cparams("arbitrary"),
    )(x, norm1_w, w_in4, cos_t, sin_t, *weight_halves)


def _head_masks():
    lane = lax.broadcasted_iota(jnp.int32, (1, LANES), 1)
    return [(lane // HEAD_DIM) == h for h in range(LANES // HEAD_DIM)]


def _window_valid(no_prev):
    qi = lax.broadcasted_iota(jnp.int32, (ATTN_BLOCK, 2 * ATTN_BLOCK), 0)
    kj = lax.broadcasted_iota(jnp.int32, (ATTN_BLOCK, 2 * ATTN_BLOCK), 1)
    valid = (kj >= qi) & (kj <= qi + ATTN_BLOCK)
    return valid & (jnp.logical_not(no_prev) | (kj >= ATTN_BLOCK))


def _strided_rows(start, dilation):
    if dilation == 1:
        return pl.ds(start, ATTN_BLOCK)
    return pl.ds(start, ATTN_BLOCK, stride=dilation)


def _block_before(edge_ref, cur_ref, t, r, span, dilation, per_step):
    edge = edge_ref[_strided_rows(ATTN_STEP_ROWS - span + r, dilation), :]
    if per_step == 1:
        return edge
    inside = cur_ref[_strided_rows(r + span * jnp.maximum(t - 1, 0), dilation), :]
    return jnp.where(t == 0, edge, inside)


def _attn_specs():
    cur = pl.BlockSpec((ATTN_STEP_ROWS, LANES), lambda hp, j: (j, hp))
    prev = pl.BlockSpec((ATTN_STEP_ROWS, LANES), lambda hp, j: (jnp.maximum(j - 1, 0), hp))
    return cur, prev


def _for_each_block(dilation, unroll, block):
    span = ATTN_BLOCK * dilation
    per_step = ATTN_STEP_ROWS // span

    def trip(it, carry):
        block(it // dilation, it % dilation, span, per_step)
        return carry

    lax.fori_loop(0, per_step * dilation, trip, 0, unroll=unroll)


def _load_qkv(natural, major, t, r, span, dilation, per_step):
    if dilation == ATTN_MAJOR_DILATION:
        q_ref, kc_ref, vc_ref, kp_ref, vp_ref = major
        return (q_ref[r] * ATTN_SCALE, jnp.concatenate([kp_ref[r], kc_ref[r]], axis=0),
                jnp.concatenate([vp_ref[r], vc_ref[r]], axis=0))
    q_ref, kc_ref, vc_ref, kp_ref, vp_ref = natural
    rows = _strided_rows(r + span * t, dilation)
    kp = _block_before(kp_ref, kc_ref, t, r, span, dilation, per_step)
    vp = _block_before(vp_ref, vc_ref, t, r, span, dilation, per_step)
    return ((q_ref[rows, :] * ATTN_SCALE).astype(BF16),
            jnp.concatenate([kp, kc_ref[rows, :]], axis=0).astype(BF16),
            jnp.concatenate([vp, vc_ref[rows, :]], axis=0).astype(BF16))


def _major_specs():
    assert ATTN_STEP_ROWS == ATTN_BLOCK * ATTN_MAJOR_DILATION
    shape = (ATTN_MAJOR_DILATION, ATTN_BLOCK, LANES)
    return (pl.BlockSpec(shape, lambda hp, j: (0, j, hp)),
            pl.BlockSpec(shape, lambda hp, j: (0, jnp.maximum(j - 1, 0), hp)))


def attn_fwd(q, k, v, q16, k16, v16):
    seq = q.shape[0]
    cur, prev = _attn_specs()
    cur16, prev16 = _major_specs()

    def body(*refs):
        natural, major, (y_ref, lse_ref) = refs[:5], refs[5:10], refs[10:]
        first_step = pl.program_id(1) == 0
        masks = _head_masks()
        for index, (_, dilation) in enumerate(DILATED_PAIRS):
            def block(t, r, span, per_step, dilation=dilation, merge=index > 0):
                rows = _strided_rows(r + span * t, dilation)
                q2, k2, v2 = _load_qkv(natural, major, t, r, span, dilation, per_step)
                valid = _window_valid(first_step & (t == 0))
                o_acc = jnp.zeros((ATTN_BLOCK, LANES), F32)
                l_acc = jnp.zeros((ATTN_BLOCK, LANES), F32)
                for mh in masks:
                    qm = jnp.where(mh, q2, jnp.zeros_like(q2))
                    s = jnp.where(valid, _dot_nt(qm, k2), NEG_BIG)
                    m = jnp.max(s, axis=-1, keepdims=True)
                    p = jnp.exp(s - m)
                    l = jnp.sum(p, axis=-1, keepdims=True)
                    o = _dot(p.astype(BF16), v2) / l
                    o_acc = jnp.where(mh, o, o_acc)
                    l_acc = jnp.where(mh, m + jnp.log(l), l_acc)
                if merge:
                    y_old, l_old = y_ref[rows, :], lse_ref[rows, :]
                    mx = jnp.maximum(l_old, l_acc)
                    e_old, e_new = jnp.exp(l_old - mx), jnp.exp(l_acc - mx)
                    den = e_old + e_new
                    o_acc = (y_old * e_old + o_acc * e_new) / den
                    l_acc = mx + jnp.log(den)
                y_ref[rows, :] = o_acc
                lse_ref[rows, :] = l_acc

            _for_each_block(dilation, ATTN_FWD_UNROLL, block)

    return pl.pallas_call(
        body, name="attn_fwd", grid=(ATTN_WIDTH // LANES, seq // ATTN_STEP_ROWS),
        in_specs=[cur, cur, cur, prev, prev, cur16, cur16, cur16, prev16, prev16],
        out_specs=[cur, cur],
        out_shape=[jax.ShapeDtypeStruct((seq, ATTN_WIDTH), F32)] * 2,
        compiler_params=_cparams("parallel", "parallel"),
    )(q, k, v, k, v, q16, k16, v16, k16, v16)


def _chunk_cumsum(x, reverse=False):
    rc = lax.broadcasted_iota(jnp.int32, x.shape, 0) % HGRN_CHUNK
    sh = 1
    while sh < HGRN_CHUNK:
        if reverse:
            x = x + jnp.where(rc + sh < HGRN_CHUNK, pltpu.roll(x, x.shape[0] - sh, axis=0), 0.0)
        else:
            x = x + jnp.where(rc >= sh, pltpu.roll(x, sh, axis=0), 0.0)
        sh *= 2
    return x


def _chunk_row(x, row):
    return _chunk_rows([x[n * HGRN_CHUNK + row:n * HGRN_CHUNK + row + 1, :]
                        for n in range(x.shape[0] // HGRN_CHUNK)])


def _chunk_rows(rows):
    return jnp.concatenate([jnp.broadcast_to(r, (HGRN_CHUNK, r.shape[1])) for r in rows], axis=0)


def _hgrn_prep(hg, lbl):
    w = HGRN_WIDTH
    a0, a1 = lbl[0:1, :], lbl[1:2, :]
    mx = jnp.maximum(a0, a1)
    e0, e1 = jnp.exp(a0 - mx), jnp.exp(a1 - mx)
    lb = e0 / (e0 + e1)
    qb, fb, gb = hg[:, :w], hg[:, w:2 * w], hg[:, 3 * w:]
    sg = _sigmoid(fb)
    f = lb + (1.0 - lb) * sg
    b = _chunk_cumsum(jnp.log(f))
    bmid, btot = _chunk_row(b, HGRN_CHUNK // 2 - 1), _chunk_row(b, HGRN_CHUNK - 1)
    sq = _sigmoid(qb)
    p = dict(lb=lb, sg=sg, f=f, kk=1.0 - f, sq=sq, qf=qb * sq, gb=gb,
             e_iq=jnp.exp(b - bmid), e_ik=jnp.exp(bmid - b), e_b=jnp.exp(b),
             e_bb=jnp.exp(btot - b), e_tot=jnp.exp(btot))
    p["qi"] = p["qf"] * p["e_iq"]
    p["ki"] = p["kk"] * p["e_ik"]
    p["qs"] = p["qf"] * p["e_b"]
    p["kb"] = p["kk"] * p["e_bb"]
    return p


def _chunk_masks():
    t = lax.broadcasted_iota(jnp.int32, (HGRN_ROWS, HGRN_ROWS), 0)
    s = lax.broadcasted_iota(jnp.int32, (HGRN_ROWS, HGRN_ROWS), 1)
    tril = ((t // HGRN_CHUNK) == (s // HGRN_CHUNK)) & (s <= t)
    n_chunks = HGRN_ROWS // HGRN_CHUNK
    tt = lax.broadcasted_iota(jnp.int32, (HGRN_ROWS, n_chunks * LANES), 0)
    cc = lax.broadcasted_iota(jnp.int32, (HGRN_ROWS, n_chunks * LANES), 1)
    block = (tt // HGRN_CHUNK) == (cc // LANES)
    return tril, block


def _spread(x, block):
    n_chunks = HGRN_ROWS // HGRN_CHUNK
    return jnp.where(block, jnp.tile(x, (1, n_chunks)), jnp.zeros((), x.dtype))


def _fold(x_full, block):
    n_chunks = HGRN_ROWS // HGRN_CHUNK
    z = jnp.where(block, x_full, 0.0)
    acc = z[:, :LANES]
    for n in range(1, n_chunks):
        acc = acc + z[:, n * LANES:(n + 1) * LANES]
    return acc


def hgrn_fwd(hg, lb_logits, hnw, weight_halves=()):
    seq = hg.shape[0]
    nblk = seq // HGRN_ROWS
    n_steps = nblk // HGRN_STEP_BLOCKS
    step_rows = HGRN_ROWS * HGRN_STEP_BLOCKS
    n_chunks = HGRN_ROWS // HGRN_CHUNK
    n_w = len(weight_halves)

    def body(*refs):
        hg_ref, lbl_ref, hnw_ref = refs[:3]
        w_refs = refs[3:3 + n_w]
        yb_ref, o_ref, st0_ref = refs[3 + n_w:6 + n_w]
        g_refs = refs[6 + n_w:6 + 2 * n_w]
        st_scr = refs[6 + 2 * n_w]
        step = pl.program_id(0)
        if n_w:
            start, forward, finish = _gather_phases(w_refs, g_refs, *refs[7 + 2 * n_w:])
            pl.when(step == 0)(start)
            pl.when(step == (3 * n_steps) // 4)(forward)

        @pl.when(step == 0)
        def _():
            st_scr[...] = jnp.zeros_like(st_scr)

        tril, block = _chunk_masks()
        for sub in range(HGRN_STEP_BLOCKS):
            rows = slice(sub * HGRN_ROWS, (sub + 1) * HGRN_ROWS)
            hg_v = hg_ref[rows, :]
            p = _hgrn_prep(hg_v, lbl_ref[...])
            vv = hg_v[:, 2 * HGRN_WIDTH:3 * HGRN_WIDTH].astype(BF16)
            outs = []
            for h in range(HGRN_HEADS):
                sl = slice(h * LANES, (h + 1) * LANES)
                v_h = vv[:, sl]
                a = jnp.where(tril, _dot_nt(p["qi"][:, sl].astype(BF16), p["ki"][:, sl].astype(BF16)), 0.0)
                o = _dot(a.astype(BF16), v_h)
                upd = _dot_tn(v_h, _spread(p["kb"][:, sl].astype(BF16), block))
                st = st_scr[h]
                st0_ref[sub, h] = st
                parts = []
                for n in range(n_chunks):
                    parts.append(st.astype(BF16))
                    decay = p["e_tot"][n * HGRN_CHUNK:n * HGRN_CHUNK + 1, sl]
                    st = st * decay + upd[:, n * LANES:(n + 1) * LANES]
                st_scr[h] = st
                o = o + _dot_nt(_spread(p["qs"][:, sl].astype(BF16), block), jnp.concatenate(parts, axis=1))
                outs.append(o)
            o_ref[rows, :] = jnp.concatenate(outs, axis=1)
            normed = jnp.concatenate(
                [outs[h] * _rms(outs[h]) for h in range(HGRN_HEADS)], axis=1)
            gb = p["gb"]
            yb_ref[rows, :] = (normed * hnw_ref[...]) * (gb * _sigmoid(gb))
        if n_w:
            pl.when(step == n_steps - 1)(finish)

    anywhere = pl.BlockSpec(memory_space=pl.ANY)
    return pl.pallas_call(
        body, name="hgrn_fwd", grid=(n_steps,),
        in_specs=[_rows(step_rows, 4 * HGRN_WIDTH), _full((2, HGRN_WIDTH)), _full((1, HGRN_WIDTH))]
        + [anywhere] * n_w,
        out_specs=[_rows(step_rows, HGRN_WIDTH), _rows(step_rows, HGRN_WIDTH),
                   pl.BlockSpec((HGRN_STEP_BLOCKS, HGRN_HEADS, LANES, LANES), lambda i: (i, 0, 0, 0))]
        + [anywhere] * n_w,
        out_shape=[jax.ShapeDtypeStruct((seq, HGRN_WIDTH), F32)] * 2
        + [jax.ShapeDtypeStruct((nblk, HGRN_HEADS, LANES, LANES), F32)]
        + [_gathered_shape(h) for h in weight_halves],
        scratch_shapes=[pltpu.VMEM((HGRN_HEADS, LANES, LANES), F32)] + (_gather_scratch(n_w) if n_w else []),
        compiler_params=_cparams("arbitrary"),
    )(hg, lb_logits, hnw, *weight_halves)


def ffn_fwd(ya, yb, x, w_out, norm2_w, w_gu4, w_down, final_w, target):
    seq = x.shape[0]
    tm = ROW_TILE
    cw = w_gu4.shape[2]
    inv_d = 1.0 / D_MODEL

    def body(ya_ref, yb_ref, x_ref, wo_ref, nw_ref, wgu_ref, wd_ref, fw_ref, t_ref,
             mixed_ref, h1_ref, u2_ref, g_ref, up_ref, act_ref, dh2_ref, acc_ref):
        @pl.when(pl.program_id(0) == 0)
        def _():
            acc_ref[...] = jnp.zeros_like(acc_ref)

        mixed = jnp.concatenate([ya_ref[...], yb_ref[...]], axis=1).astype(BF16)
        mixed_ref[...] = mixed
        h1 = x_ref[...] + _dot(mixed, wo_ref[...])
        h1_ref[...] = h1
        u2 = ((h1 * _rms(h1)) * nw_ref[...]).astype(BF16)
        u2_ref[...] = u2
        g = jnp.concatenate([_dot(u2, wgu_ref[0]), _dot(u2, wgu_ref[1])], axis=1)
        up = jnp.concatenate([_dot(u2, wgu_ref[2]), _dot(u2, wgu_ref[3])], axis=1)
        g_ref[...] = g.astype(BF16)
        up_ref[...] = up.astype(BF16)
        act = ((g * _sigmoid(g)) * up).astype(BF16)
        act_ref[...] = act
        h2 = h1 + _dot(act, wd_ref[...])
        rf = _rms(h2)
        n = h2 * rf
        fw = fw_ref[...]
        err = n * fw - t_ref[...]
        dy = err * inv_d
        acc_ref[0:1, :] += jnp.sum(dy * n, axis=0, keepdims=True)
        acc_ref[1:2, :] += (0.5 * inv_d) * jnp.sum(err * err, axis=0, keepdims=True)
        dn = dy * fw
        dh2_ref[...] = rf * (dn - n * jnp.mean(dn * n, axis=-1, keepdims=True))

    half = _rows(tm, ATTN_WIDTH)
    wide = _rows(tm, D_MODEL)
    ffn = _rows(tm, FFN_HIDDEN)
    return pl.pallas_call(
        body, name="ffn_fwd", grid=(seq // tm,),
        in_specs=[half, half, wide, _weight((D_MODEL, D_MODEL)), _full((1, D_MODEL)),
                  _weight((N_CHIPS, D_MODEL, cw)), _weight((FFN_HIDDEN, D_MODEL)), _full((1, D_MODEL)), wide],
        out_specs=[wide, wide, wide, ffn, ffn, ffn, wide, _full((8, D_MODEL))],
        out_shape=[jax.ShapeDtypeStruct((seq, D_MODEL), BF16), jax.ShapeDtypeStruct((seq, D_MODEL), F32),
                   jax.ShapeDtypeStruct((seq, D_MODEL), BF16)]
        + [jax.ShapeDtypeStruct((seq, FFN_HIDDEN), BF16)] * 3
        + [jax.ShapeDtypeStruct((seq, D_MODEL), F32), jax.ShapeDtypeStruct((8, D_MODEL), F32)],
        compiler_params=_cparams("arbitrary"),
    )(ya, yb, x, w_out, norm2_w, w_gu4, w_down, final_w, target)


def _head_sum_matrix():
    i = jnp.arange(ATTN_WIDTH)
    return ((i[:, None] // HEAD_DIM) == (i[None, :] // HEAD_DIM)).astype(BF16)


def ffn_bwd(dh2, w_down, g, up, w_gu4, h1, norm2_w, w_out, ya):
    seq = h1.shape[0]
    tm = ROW_TILE
    cw = w_gu4.shape[2]
    hsum = _head_sum_matrix()

    def body(dh2_ref, wd_ref, g_ref, up_ref, w_ref, h1_ref, nw_ref, wo_ref, ya_ref, hs_ref,
             dgu_ref, dh1_ref, dya_ref, dyb_ref, delta_ref, acc_ref):
        @pl.when(pl.program_id(0) == 0)
        def _():
            acc_ref[...] = jnp.zeros_like(acc_ref)

        dh2_b = dh2_ref[...].astype(BF16)
        du2 = jnp.zeros((tm, D_MODEL), F32)
        for j in range(N_CHIPS // 2):
            dact = _dot_nt(dh2_b, wd_ref[j * cw:(j + 1) * cw, :])
            gv = g_ref[:, j * cw:(j + 1) * cw].astype(F32)
            sg = _sigmoid(gv)
            dg = (dact * up_ref[:, j * cw:(j + 1) * cw].astype(F32) * (sg * (1.0 + gv * (1.0 - sg)))).astype(BF16)
            dup = (dact * (gv * sg)).astype(BF16)
            dgu_ref[:, j * cw:(j + 1) * cw] = dg
            dgu_ref[:, FFN_HIDDEN + j * cw:FFN_HIDDEN + (j + 1) * cw] = dup
            du2 = du2 + _dot_nt(dg, w_ref[j]) + _dot_nt(dup, w_ref[N_CHIPS // 2 + j])
        h1 = h1_ref[...]
        r2 = _rms(h1)
        nh = h1 * r2
        acc_ref[0:1, :] += jnp.sum(du2 * nh, axis=0, keepdims=True)
        dn = du2 * nw_ref[...]
        dh1 = dh2_ref[...] + r2 * (dn - nh * jnp.mean(dn * nh, axis=-1, keepdims=True))
        dh1_ref[...] = dh1
        dmixed = _dot_nt(dh1.astype(BF16), wo_ref[...])
        dya = dmixed[:, :ATTN_WIDTH]
        dya_ref[...] = dya
        dyb_ref[...] = dmixed[:, ATTN_WIDTH:]
        prod = dya * ya_ref[...]
        hi = prod.astype(BF16)
        lo = (prod - hi.astype(F32)).astype(BF16)
        delta_ref[...] = _dot(hi, hs_ref[...]) + _dot(lo, hs_ref[...])

    wide = _rows(tm, D_MODEL)
    half = _rows(tm, ATTN_WIDTH)
    ffn = _rows(tm, FFN_HIDDEN)
    return pl.pallas_call(
        body, name="ffn_bwd", grid=(seq // tm,),
        in_specs=[wide, _weight((FFN_HIDDEN, D_MODEL)), ffn, ffn, _weight((N_CHIPS, D_MODEL, cw)), wide,
                  _full((1, D_MODEL)), _weight((D_MODEL, D_MODEL)), half, _full((ATTN_WIDTH, ATTN_WIDTH))],
        out_specs=[_rows(tm, 2 * FFN_HIDDEN), wide, half, half, half, _full((8, D_MODEL))],
        out_shape=[jax.ShapeDtypeStruct((seq, 2 * FFN_HIDDEN), BF16), jax.ShapeDtypeStruct((seq, D_MODEL), F32)]
        + [jax.ShapeDtypeStruct((seq, ATTN_WIDTH), F32)] * 3 + [jax.ShapeDtypeStruct((8, D_MODEL), F32)],
        compiler_params=_cparams("arbitrary"),
    )(dh2, w_down, g, up, w_gu4, h1, norm2_w, w_out, ya, hsum)


def attn_bwd(q, k, v, q16, k16, v16, dy, lse, delta, sibling_grads=()):
    seq = q.shape[0]
    cur, prev = _attn_specs()
    cur16, prev16 = _major_specs()
    whole = pl.BlockSpec((seq, LANES), lambda hp, j: (0, hp))
    n_g = len(sibling_grads)
    n_hp, n_steps = ATTN_WIDTH // LANES, seq // ATTN_STEP_ROWS
    n_in = 13

    def body(*refs):
        natural, major = refs[:5], refs[5:10]
        dy_ref, lse_ref, dl_ref = refs[10:n_in]
        dq_ref, dk_ref, dv_ref = refs[n_in + n_g:n_in + 3 + n_g]
        first_step = pl.program_id(1) == 0
        base = pl.program_id(1) * ATTN_STEP_ROWS
        masks = _head_masks()
        if n_g:
            start, finish = _sibling_exchange_phases(
                refs[n_in:n_in + n_g], refs[n_in + 3 + n_g:n_in + 3 + 2 * n_g], *refs[n_in + 3 + 2 * n_g:])
            pl.when((pl.program_id(0) == 0) & first_step)(start)

        def block(t, r, span, per_step, dilation, add):
            rows = _strided_rows(r + span * t, dilation)
            at_edge = t == 0
            q2, k2, v2 = _load_qkv(natural, major, t, r, span, dilation, per_step)
            dy2 = dy_ref[rows, :].astype(BF16)
            lse2, dl2 = lse_ref[rows, :], dl_ref[rows, :]
            valid = _window_valid(first_step & at_edge)
            zero = jnp.zeros_like(q2)
            qms, dyms, ps, dss, kms = [], [], [], [], []
            for h, mh in enumerate(masks):
                c0 = h * HEAD_DIM
                qm, dym = jnp.where(mh, q2, zero), jnp.where(mh, dy2, zero)
                s = _dot_nt(qm, k2)
                p = jnp.where(valid, jnp.exp(s - lse2[:, c0:c0 + 1]), 0.0)
                dp = _dot_nt(dym, v2)
                dss.append((p * (dp - dl2[:, c0:c0 + 1])).astype(BF16))
                ps.append(p.astype(BF16))
                qms.append(qm)
                dyms.append(dym)
                kms.append(jnp.where(mh, k2, jnp.zeros_like(k2)))
            dq = _dot(jnp.concatenate(dss, axis=1), jnp.concatenate(kms, axis=0)) * ATTN_SCALE
            p_all, ds_all = jnp.concatenate(ps, axis=0), jnp.concatenate(dss, axis=0)
            dy_all, q_all = jnp.concatenate(dyms, axis=0), jnp.concatenate(qms, axis=0)
            dv_full, dk_full = _dot_tn(dy_all, p_all).T, _dot_tn(q_all, ds_all).T
            here = _strided_rows(base + r + span * t, dilation)
            if add:
                dq_ref[rows, :] += dq
                dk_ref[here, :] += dk_full[ATTN_BLOCK:]
                dv_ref[here, :] += dv_full[ATTN_BLOCK:]
            else:
                dq_ref[rows, :] = dq
                dk_ref[here, :] = dk_full[ATTN_BLOCK:]
                dv_ref[here, :] = dv_full[ATTN_BLOCK:]
            back = _strided_rows(jnp.maximum(base + r + span * t - span, r), dilation)
            dk_ref[back, :] += dk_full[:ATTN_BLOCK]
            dv_ref[back, :] += dv_full[:ATTN_BLOCK]

        for index, (_, dilation) in enumerate(DILATED_PAIRS):
            _for_each_block(dilation, ATTN_BWD_UNROLL,
                            functools.partial(block, dilation=dilation, add=index > 0))
        if n_g:
            pl.when((pl.program_id(0) == n_hp - 1) & (pl.program_id(1) == n_steps - 1))(finish)

    anywhere = pl.BlockSpec(memory_space=pl.ANY)
    return pl.pallas_call(
        body, name="attn_bwd", grid=(n_hp, n_steps),
        in_specs=[cur, cur, cur, prev, prev, cur16, cur16, cur16, prev16, prev16, cur, cur, cur]
        + [anywhere] * n_g,
        out_specs=[cur, whole, whole] + [anywhere] * n_g,
        out_shape=[jax.ShapeDtypeStruct((seq, ATTN_WIDTH), F32)] * 3 + _sibling_exchange_shapes(sibling_grads),
        scratch_shapes=_sibling_exchange_scratch(n_g) if n_g else [],
        compiler_params=_cparams("arbitrary", "arbitrary"),
    )(q, k, v, k, v, q16, k16, v16, k16, v16, dy, lse, delta, *sibling_grads)


def hgrn_bwd(hg, lb_logits, hnw, o_pre, st0, dyb, chip_sums=()):
    seq = hg.shape[0]
    step_rows = HGRN_ROWS * HGRN_STEP_BLOCKS
    n_steps = seq // step_rows
    n_chunks = HGRN_ROWS // HGRN_CHUNK
    w = HGRN_WIDTH
    n_s = len(chip_sums)

    def body(*refs):
        hg_ref, lbl_ref, hnw_ref, o_ref, st0_ref, dyb_ref = refs[:6]
        dhg_ref, acc_ref = refs[6 + n_s:8 + n_s]
        dst_scr = refs[8 + 2 * n_s]
        step = pl.program_id(0)
        if n_s:
            start, finish = _chip_exchange_phases(refs[6:6 + n_s], refs[8 + n_s:8 + 2 * n_s], *refs[9 + 2 * n_s:])
            pl.when(step == 0)(start)

        @pl.when(step == 0)
        def _():
            dst_scr[...] = jnp.zeros_like(dst_scr)
            acc_ref[...] = jnp.zeros_like(acc_ref)

        tril, block = _chunk_masks()
        for sub in reversed(range(HGRN_STEP_BLOCKS)):
            rows = slice(sub * HGRN_ROWS, (sub + 1) * HGRN_ROWS)
            hg_v = hg_ref[rows, :]
            p = _hgrn_prep(hg_v, lbl_ref[...])
            vv = hg_v[:, 2 * w:3 * w].astype(BF16)
            hnw_v = hnw_ref[...]
            gb = p["gb"]
            sgg = _sigmoid(gb)
            silu_g = gb * sgg
            dyb_v = dyb_ref[rows, :]
            o_v = o_ref[rows, :]

            d_on = dyb_v * hnw_v * silu_g
            on_parts, do_parts = [], []
            for h in range(HGRN_HEADS):
                sl = slice(h * LANES, (h + 1) * LANES)
                rs = _rms(o_v[:, sl])
                on = o_v[:, sl] * rs
                on_parts.append(on)
                do_parts.append(rs * (d_on[:, sl] - on * jnp.mean(d_on[:, sl] * on, axis=-1, keepdims=True)))
            on_all = jnp.concatenate(on_parts, axis=1)
            dgb = dyb_v * on_all * hnw_v * (sgg * (1.0 + gb * (1.0 - sgg)))
            acc_ref[0:1, :] += jnp.sum(dyb_v * on_all * silu_g, axis=0, keepdims=True)

            dqf_parts, dkk_parts, db_parts, dv_parts, dbt_parts, dkbkb_parts = [], [], [], [], [], []
            for h in range(HGRN_HEADS):
                sl = slice(h * LANES, (h + 1) * LANES)
                v_h = vv[:, sl]
                do_h = do_parts[h].astype(BF16)
                qi, ki, qs, kb = p["qi"][:, sl], p["ki"][:, sl], p["qs"][:, sl], p["kb"][:, sl]
                qi_b, ki_b = qi.astype(BF16), ki.astype(BF16)
                kb_cat = _spread(kb.astype(BF16), block)
                qs_cat = _spread(qs.astype(BF16), block)
                upd = _dot_tn(v_h, kb_cat)
                st = st0_ref[sub, h]
                st_parts = []
                for n in range(n_chunks):
                    st_parts.append(st)
                    decay = p["e_tot"][n * HGRN_CHUNK:n * HGRN_CHUNK + 1, sl]
                    st = st * decay + upd[:, n * LANES:(n + 1) * LANES]
                st_cat = jnp.concatenate([s_.astype(BF16) for s_ in st_parts], axis=1)
                wgt = _dot_tn(do_h, qs_cat)
                dst = dst_scr[h]
                dst_parts = [None] * n_chunks
                dbt_rows = [None] * n_chunks
                for n in reversed(range(n_chunks)):
                    dst_parts[n] = dst.astype(BF16)
                    decay = p["e_tot"][n * HGRN_CHUNK:n * HGRN_CHUNK + 1, sl]
                    dbt_rows[n] = jnp.sum(dst * st_parts[n], axis=0, keepdims=True) * decay
                    dst = dst * decay + wgt[:, n * LANES:(n + 1) * LANES]
                dst_scr[h] = dst
                dst_cat = jnp.concatenate(dst_parts, axis=1)
                dqs = _fold(_dot(do_h, st_cat), block)
                dkb = _fold(_dot(v_h, dst_cat), block)
                dv_state = _dot_nt(kb_cat, dst_cat)
                a = jnp.where(tril, _dot_nt(qi_b, ki_b), 0.0).astype(BF16)
                da = jnp.where(tril, _dot_nt(do_h, v_h), 0.0).astype(BF16)
                dv_parts.append(_dot_tn(a, do_h) + dv_state)
                dqi = _dot(da, ki_b)
                dki = _dot_tn(da, qi_b)
                dqf_parts.append(dqi * p["e_iq"][:, sl] + dqs * p["e_b"][:, sl])
                dkk_parts.append(dki * p["e_ik"][:, sl] + dkb * p["e_bb"][:, sl])
                dkbkb = dkb * kb
                db_parts.append(dqi * qi - dki * ki + dqs * qs - dkbkb)
                dkbkb_parts.append(dkbkb)
                dbt_parts.append(_chunk_rows(dbt_rows))

            cat = lambda parts: jnp.concatenate(parts, axis=1)
            dlogf = (_chunk_cumsum(cat(db_parts), reverse=True)
                     + _chunk_row(_chunk_cumsum(cat(dkbkb_parts)), HGRN_CHUNK - 1) + cat(dbt_parts))
            sq, qb = p["sq"], hg_v[:, :w]
            dqb = cat(dqf_parts) * (sq * (1.0 + qb * (1.0 - sq)))
            df = dlogf / p["f"] - cat(dkk_parts)
            sg, lb = p["sg"], p["lb"]
            dfb = df * (1.0 - lb) * sg * (1.0 - sg)
            acc_ref[1:2, :] += jnp.sum(df * (1.0 - sg), axis=0, keepdims=True)
            dhg_ref[rows, :] = jnp.concatenate([dqb, dfb, cat(dv_parts), dgb], axis=1).astype(BF16)
        if n_s:
            pl.when(step == n_steps - 1)(finish)

    rev = lambda i: (n_steps - 1 - i, 0)
    anywhere = pl.BlockSpec(memory_space=pl.ANY)
    return pl.pallas_call(
        body, name="hgrn_bwd", grid=(n_steps,),
        in_specs=[pl.BlockSpec((step_rows, 4 * w), rev), _full((2, w)), _full((1, w)),
                  pl.BlockSpec((step_rows, w), rev),
                  pl.BlockSpec((HGRN_STEP_BLOCKS, HGRN_HEADS, LANES, LANES), lambda i: (n_steps - 1 - i, 0, 0, 0)),
                  pl.BlockSpec((step_rows, w), rev)] + [anywhere] * n_s,
        out_specs=[pl.BlockSpec((step_rows, 4 * w), rev), _full((8, w))] + [anywhere] * n_s,
        out_shape=[jax.ShapeDtypeStruct((seq, 4 * w), BF16), jax.ShapeDtypeStruct((8, w), F32)]
        + [jax.ShapeDtypeStruct(s.shape, s.dtype) for s in chip_sums],
        scratch_shapes=[pltpu.VMEM((HGRN_HEADS, LANES, LANES), F32)] + (_chip_exchange_scratch(n_s) if n_s else []),
        compiler_params=_cparams("arbitrary"),
    )(hg, lb_logits, hnw, o_pre, st0, dyb, *chip_sums)


def in_bwd(dq, dk, dv, dhg, cos_t, sin_t, w_in4, x, norm1_w, dh1):
    seq = x.shape[0]
    tm = ROW_TILE
    cw = w_in4.shape[2]

    def body(dq_ref, dk_ref, dv_ref, dhg_ref, cos_ref, sin_ref, w_ref,
             x_ref, nw_ref, dh1_ref, dproj_ref, dx_ref, acc_ref):
        @pl.when(pl.program_id(0) == 0)
        def _():
            acc_ref[...] = jnp.zeros_like(acc_ref)

        cos, sin = cos_ref[...], sin_ref[...]
        dqa = _rotary_bwd(dq_ref[...], cos, sin)
        dka = _rotary_bwd(dk_ref[...], cos, sin)
        dproj = jnp.concatenate(
            [jnp.concatenate([dqa, dka, dv_ref[...]], axis=1).astype(BF16), dhg_ref[...]], axis=1)
        dproj_ref[...] = dproj
        du = _dot_nt(dproj[:, :cw], w_ref[0])
        for j in range(1, N_CHIPS):
            du = du + _dot_nt(dproj[:, j * cw:(j + 1) * cw], w_ref[j])
        xv = x_ref[...]
        r1 = _rms(xv)
        nx = xv * r1
        acc_ref[0:1, :] += jnp.sum(du * nx, axis=0, keepdims=True)
        dn = du * nw_ref[...]
        dx_ref[...] = dh1_ref[...] + r1 * (dn - nx * jnp.mean(dn * nx, axis=-1, keepdims=True))

    half = _rows(tm, ATTN_WIDTH)
    wide = _rows(tm, D_MODEL)
    return pl.pallas_call(
        body, name="in_bwd", grid=(seq // tm,),
        in_specs=[half] * 3 + [_rows(tm, 4 * HGRN_WIDTH), _rows(tm, LANES), _rows(tm, LANES),
                               _weight((N_CHIPS, D_MODEL, cw)), wide, _full((1, D_MODEL)), wide],
        out_specs=[_rows(tm, IN_PROJ_WIDTH), wide, _full((8, D_MODEL))],
        out_shape=[jax.ShapeDtypeStruct((seq, IN_PROJ_WIDTH), BF16), jax.ShapeDtypeStruct((seq, D_MODEL), F32),
                   jax.ShapeDtypeStruct((8, D_MODEL), F32)],
        compiler_params=_cparams("arbitrary"),
    )(dq, dk, dv, dhg, cos_t, sin_t, w_in4, x, norm1_w, dh1)


def weight_grad(a, b, col_block, name, group=1, small_pack=None):
    seq, kdim = a.shape
    ndim = b.shape[1]
    nj = ndim // col_block
    tk = min(1024, seq)
    hosting = small_pack is not None
    n_j, n_t = nj // group, seq // tk

    def body(*refs):
        a_ref, b_ref = refs[:2]
        o_ref = refs[3] if hosting else refs[2]
        if hosting:
            start, finish = _pack_gather_phases(refs[2], refs[4], *refs[5:])
            pl.when((pl.program_id(0) == 0) & (pl.program_id(1) == 0))(start)

        @pl.when(pl.program_id(1) == 0)
        def _():
            o_ref[...] = jnp.zeros_like(o_ref)

        acc = _dot_tn(a_ref[...].astype(BF16), b_ref[...].astype(BF16))
        for i in range(group):
            o_ref[i] += acc[:, i * col_block:(i + 1) * col_block]
        if hosting:
            pl.when((pl.program_id(0) == n_j - 1) & (pl.program_id(1) == n_t - 1))(finish)

    anywhere = pl.BlockSpec(memory_space=pl.ANY)
    out = pl.pallas_call(
        body, name=name, grid=(n_j, n_t),
        in_specs=[pl.BlockSpec((tk, kdim), lambda j, t: (t, 0)),
                  pl.BlockSpec((tk, group * col_block), lambda j, t: (t, j))] + [anywhere] * hosting,
        out_specs=[pl.BlockSpec((group, kdim, col_block), lambda j, t: (j, 0, 0))] + [anywhere] * hosting,
        out_shape=[jax.ShapeDtypeStruct((nj, kdim, col_block), F32)]
        + ([jax.ShapeDtypeStruct((N_DEV,) + small_pack.shape, F32)] if hosting else []),
        scratch_shapes=[pltpu.SemaphoreType.DMA((N_DEV - 1,)), pltpu.SemaphoreType.DMA((N_DEV - 1,)),
                        pltpu.SemaphoreType.DMA] if hosting else [],
        compiler_params=_cparams("arbitrary", "arbitrary"),
    )(a, b, *([small_pack] if hosting else []))
    return out if hosting else out[0]


def _sibling_exchange_phases(g_refs, out_refs, send_sems, recv_sems):
    x, y, cc = _mesh_pos()

    def copies():
        return [pltpu.make_async_remote_copy(
            src_ref=g_refs[i].at[j, 1 - cc], dst_ref=out_refs[i].at[j],
            send_sem=send_sems.at[i * N_CHIPS + j], recv_sem=recv_sems.at[i * N_CHIPS + j],
            device_id=(x, y, 1 - cc), device_id_type=MESH_ID)
            for i in range(len(g_refs)) for j in range(N_CHIPS)]

    def start():
        for cp in copies():
            cp.start()

    def finish():
        for cp in copies():
            cp.wait_recv()
        for cp in copies():
            cp.wait_send()

    return start, finish


def _sibling_exchange_scratch(n):
    return [pltpu.SemaphoreType.DMA((n * N_CHIPS,)), pltpu.SemaphoreType.DMA((n * N_CHIPS,))]


def _sibling_exchange_shapes(grads):
    return [jax.ShapeDtypeStruct((N_CHIPS,) + g.shape[2:], g.dtype) for g in grads]


def exchange_with_sibling(grads, name):
    n = len(grads)

    def body(*refs):
        start, finish = _sibling_exchange_phases(refs[:n], refs[n:2 * n], refs[2 * n], refs[2 * n + 1])
        start()
        finish()

    return pl.pallas_call(
        body, name=name,
        in_specs=[pl.BlockSpec(memory_space=pl.ANY)] * n,
        out_specs=[pl.BlockSpec(memory_space=pl.ANY)] * n,
        out_shape=_sibling_exchange_shapes(grads),
        scratch_shapes=_sibling_exchange_scratch(n),
    )(*grads)


def add_own_half(grad, recv, name):
    _, _, r, c = grad.shape
    tr = r // 2 if r % 32 == 0 else r

    def body(cc_ref, g_ref, r_ref, o_ref):
        o_ref[...] = (g_ref[...] + r_ref[...]).astype(BF16)

    grid_spec = pltpu.PrefetchScalarGridSpec(
        num_scalar_prefetch=1, grid=(N_CHIPS, r // tr),
        in_specs=[pl.BlockSpec((None, None, tr, c), lambda j, t, cc: (j, cc[0], t, 0)),
                  pl.BlockSpec((None, tr, c), lambda j, t, cc: (j, t, 0))],
        out_specs=pl.BlockSpec((None, tr, c), lambda j, t, cc: (j, t, 0)))
    cc = lax.axis_index("c").astype(jnp.int32).reshape(1)
    return pl.pallas_call(
        body, name=name, grid_spec=grid_spec,
        out_shape=jax.ShapeDtypeStruct((N_CHIPS, r, c), BF16),
        compiler_params=_cparams("parallel", "parallel"),
    )(cc, grad, recv)


def _chip_exchange_phases(s_refs, out_refs, send_sems, recv_sems):
    n = len(s_refs)
    x, y, cc = _mesh_pos()
    my_chip = 2 * x + y
    chips = [(1 - x, y), (x, 1 - y), (1 - x, 1 - y)]

    def outgoing():
        return [pltpu.make_async_remote_copy(
            src_ref=s_refs[i].at[2 * px + py], dst_ref=out_refs[i].at[my_chip],
            send_sem=send_sems.at[3 * i + j], recv_sem=recv_sems.at[3 * i + j],
            device_id=(px, py, cc), device_id_type=MESH_ID)
            for i in range(n) for j, (px, py) in enumerate(chips)]

    def start():
        for cp in outgoing():
            cp.start()

    def finish():
        for i in range(n):
            for j, (px, py) in enumerate(chips):
                pltpu.make_async_remote_copy(
                    src_ref=s_refs[i].at[my_chip], dst_ref=out_refs[i].at[2 * px + py],
                    send_sem=send_sems.at[3 * i + j], recv_sem=recv_sems.at[3 * i + j],
                    device_id=(px, py, cc), device_id_type=MESH_ID).wait_recv()
        for cp in outgoing():
            cp.wait_send()

    return start, finish


def _chip_exchange_scratch(n):
    return [pltpu.SemaphoreType.DMA((3 * n,)), pltpu.SemaphoreType.DMA((3 * n,))]


def sum_chips(sums, parts, name):
    _, r, c = parts.shape
    tr = r // 2 if r % 32 == 0 else r

    def body(idx_ref, s_ref, p1_ref, p2_ref, p3_ref, o_ref):
        o_ref[...] = ((s_ref[...].astype(F32) + p1_ref[...].astype(F32))
                      + p2_ref[...].astype(F32)) + p3_ref[...].astype(F32)

    def pick(k):
        return pl.BlockSpec((None, tr, c), lambda t, idx: (idx[k], t, 0))

    x, y = lax.axis_index("x"), lax.axis_index("y")
    idx = jnp.stack([2 * x + y, 2 * (1 - x) + y, 2 * x + (1 - y), 2 * (1 - x) + (1 - y)]).astype(jnp.int32)
    grid_spec = pltpu.PrefetchScalarGridSpec(
        num_scalar_prefetch=1, grid=(r // tr,),
        in_specs=[pick(0), pick(1), pick(2), pick(3)],
        out_specs=pl.BlockSpec((tr, c), lambda t, idx: (t, 0)))
    return pl.pallas_call(
        body, name=name, grid_spec=grid_spec,
        out_shape=jax.ShapeDtypeStruct((r, c), F32),
        compiler_params=_cparams("parallel"),
    )(idx, sums, parts, parts, parts)


def share_with_sibling(halves, name):
    n = len(halves)

    def body(*refs):
        h_refs, out_refs = refs[:n], refs[n:2 * n]
        send_sems, recv_sems = refs[2 * n], refs[2 * n + 1]
        x, y, cc = _mesh_pos()
        copies = [pltpu.make_async_remote_copy(
            src_ref=h_refs[i], dst_ref=out_refs[i],
            send_sem=send_sems.at[i], recv_sem=recv_sems.at[i],
            device_id=(x, y, 1 - cc), device_id_type=MESH_ID) for i in range(n)]
        for cp in copies:
            cp.start()
        for cp in copies:
            cp.wait_recv()
        for cp in copies:
            cp.wait_send()

    return pl.pallas_call(
        body, name=name,
        in_specs=[pl.BlockSpec(memory_space=pl.ANY)] * n,
        out_specs=[pl.BlockSpec(memory_space=pl.ANY)] * n,
        out_shape=[jax.ShapeDtypeStruct(h.shape, h.dtype) for h in halves],
        scratch_shapes=[pltpu.SemaphoreType.DMA((n,)), pltpu.SemaphoreType.DMA((n,))],
    )(*halves)


def _adam_update(w, g, m, v):
    m = ADAM_B1 * m + (1.0 - ADAM_B1) * g
    v = ADAM_B2 * v + (1.0 - ADAM_B2) * (g * g)
    m_hat = m / (1.0 - ADAM_B1 ** ADAM_STEP)
    v_hat = v / (1.0 - ADAM_B2 ** ADAM_STEP)
    delta = -ADAM_LR * (m_hat / (jnp.sqrt(v_hat) + ADAM_EPS) + ADAM_WD * w)
    return delta, m, v


ADAMW_STEPS = 8


def adamw(ws, g_mine, g_sibling, ms, vs, name, chip_sums=()):
    n, n_s = len(ws), len(chip_sums)
    per_half = ADAMW_STEPS // 2

    def body(*refs):
        cc_ref = refs[0]
        ins = refs[1:1 + 5 * n]
        outs = refs[1 + 5 * n + n_s:1 + 9 * n + n_s]
        step = pl.program_id(0)
        if n_s:
            start, finish = _chip_exchange_phases(
                refs[1 + 5 * n:1 + 5 * n + n_s], refs[1 + 9 * n + n_s:1 + 9 * n + 2 * n_s],
                *refs[1 + 9 * n + 2 * n_s:])
            pl.when(step == 0)(start)
        mine = (step // per_half) == cc_ref[0]
        for i in range(n):
            w_ref, ga_ref, gb_ref, m_ref, v_ref = ins[5 * i:5 * i + 5]
            g_ref, d_ref, nm_ref, nv_ref = outs[4 * i:4 * i + 4]
            g = jnp.where(mine, ga_ref[...], gb_ref[...])
            g_ref[...] = g
            d, nm, nv = _adam_update(w_ref[...], g, m_ref[...], v_ref[...])
            d_ref[...] = d
            nm_ref[...] = nm
            nv_ref[...] = nv
        if n_s:
            pl.when(step == ADAMW_STEPS - 1)(finish)

    in_specs, out_specs, out_shape, operands = [], [], [], []
    for w, ga, gb, m, v in zip(ws, g_mine, g_sibling, ms, vs):
        r, c = w.shape
        tr = r // ADAMW_STEPS
        full = pl.BlockSpec((tr, c), lambda t, cc: (t, 0))
        part = pl.BlockSpec((tr, c), lambda t, cc: (t % per_half, 0))
        in_specs += [full, part, part, full, full]
        out_specs += [full] * 4
        out_shape += [jax.ShapeDtypeStruct((r, c), F32)] * 4
        operands += [w, ga, gb, m, v]
    anywhere = pl.BlockSpec(memory_space=pl.ANY)
    grid_spec = pltpu.PrefetchScalarGridSpec(
        num_scalar_prefetch=1, grid=(ADAMW_STEPS,),
        in_specs=in_specs + [anywhere] * n_s, out_specs=out_specs + [anywhere] * n_s,
        scratch_shapes=_chip_exchange_scratch(n_s) if n_s else [])
    cc = lax.axis_index("c").astype(jnp.int32).reshape(1)
    res = pl.pallas_call(
        body, name=name, grid_spec=grid_spec,
        out_shape=out_shape + [jax.ShapeDtypeStruct(s.shape, s.dtype) for s in chip_sums],
        compiler_params=_cparams("arbitrary"),
    )(cc, *operands, *chip_sums)
    per_shard = [tuple(res[4 * i:4 * i + 4]) for i in range(n)]
    return (per_shard, list(res[4 * n:])) if n_s else per_shard


def _pack_gather_phases(p_ref, out_ref, send_sems, recv_sems, local_sem):
    x, y, cc = _mesh_pos()
    me = 4 * x + 2 * y + cc
    flips = [(fx, fy, fc) for fx in (0, 1) for fy in (0, 1) for fc in (0, 1)][1:]

    def copy(k, row):
        fx, fy, fc = flips[k]
        return pltpu.make_async_remote_copy(
            src_ref=p_ref, dst_ref=out_ref.at[row],
            send_sem=send_sems.at[k], recv_sem=recv_sems.at[k],
            device_id=(x ^ fx, y ^ fy, cc ^ fc), device_id_type=MESH_ID)

    def local():
        return pltpu.make_async_copy(p_ref, out_ref.at[me], local_sem)

    def start():
        local().start()
        for k in range(len(flips)):
            copy(k, me).start()

    def finish():
        for k, (fx, fy, fc) in enumerate(flips):
            copy(k, 4 * (x ^ fx) + 2 * (y ^ fy) + (cc ^ fc)).wait_recv()
        for k in range(len(flips)):
            copy(k, me).wait_send()
        local().wait()

    return start, finish


def small_update(gathered, wpack, mpack, vpack):
    hw = HGRN_WIDTH

    def body(g_ref, w_ref, m_ref, v_ref, go_ref, d_ref, nm_ref, nv_ref, loss_ref):
        g = g_ref[0]
        for d in range(1, N_DEV):
            g = g + g_ref[d]
        wv = w_ref[...]
        a0, a1 = wv[4:5, :hw], wv[4:5, hw:]
        mx = jnp.maximum(a0, a1)
        e0, e1 = jnp.exp(a0 - mx), jnp.exp(a1 - mx)
        lb = e0 / (e0 + e1)
        dl = g[4:5, :hw] * lb * (1.0 - lb)
        row = lax.broadcasted_iota(jnp.int32, g.shape, 0)
        lb_row = jnp.concatenate([dl, -dl], axis=1)
        grads = jnp.where(row == 4, lb_row, jnp.where(row < 4, g, 0.0))
        go_ref[...] = grads
        d, nm, nv = _adam_update(wv, grads, m_ref[...], v_ref[...])
        d_ref[...] = d
        nm_ref[...] = nm
        nv_ref[...] = nv
        loss_ref[...] = jnp.zeros((8, LANES), F32) + jnp.sum(g[5:6, :])

    vm = pl.BlockSpec(memory_space=pltpu.VMEM)
    return pl.pallas_call(
        body, name="small_update",
        in_specs=[vm] * 4, out_specs=[vm] * 5,
        out_shape=[jax.ShapeDtypeStruct(wpack.shape, F32)] * 4 + [jax.ShapeDtypeStruct((8, LANES), F32)],
    )(gathered, wpack, mpack, vpack)


def _pack_small(n1, n2, fn, hn, lbl):
    z = jnp.zeros((1, D_MODEL - HGRN_WIDTH), F32)
    rows = [n1.reshape(1, D_MODEL), n2.reshape(1, D_MODEL), fn.reshape(1, D_MODEL),
            jnp.concatenate([hn.reshape(1, HGRN_WIDTH), z], axis=1), lbl.reshape(1, 2 * HGRN_WIDTH),
            jnp.zeros((3, D_MODEL), F32)]
    return jnp.concatenate(rows, axis=0)


def _unpack_small(pack):
    return (pack[0:1], pack[4].reshape(2, HGRN_WIDTH), pack[3:4, :HGRN_WIDTH], pack[1:2], pack[2])


def kernel(x, norm1_w, w_in, lb_logits, hgrn_norm_w, w_out, norm2_w, w_gate_up, w_down, final_norm_w, loss_target, m_norm1_w, m_w_in, m_lb_logits, m_hgrn_norm_w, m_w_out, m_norm2_w, m_w_gate_up, m_w_down, m_final_norm_w, v_norm1_w, v_w_in, v_lb_logits, v_hgrn_norm_w, v_w_out, v_norm2_w, v_w_gate_up, v_w_down, v_final_norm_w):
    seq = x.shape[1]
    xs = x.reshape(seq, D_MODEL)
    target = loss_target.reshape(seq, D_MODEL)
    shards = {"w_in": w_in[0], "w_out": w_out[0], "w_gu": w_gate_up[0], "w_down": w_down[0]}

    cast = {k: cast_bf16(w, "cast_" + k) for k, w in shards.items()}
    w_in4 = allgather_halves(cast["w_in"], "gather_w_in").reshape(N_CHIPS, D_MODEL, -1)

    cos_t, sin_t = _rope_tables(seq)
    fw = final_norm_w.reshape(1, D_MODEL)

    qr, kr, va, hg, u, q16, k16, v16, g_out, g_down = in_proj(
        xs, norm1_w, w_in4, cos_t, sin_t, [cast["w_out"], cast["w_down"]])
    ya, lse = attn_fwd(qr, kr, va, q16, k16, v16)
    yb, o_pre, st0, g_gu = hgrn_fwd(hg, lb_logits, hgrn_norm_w, [cast["w_gu"]])
    w_out_f = g_out.reshape(D_MODEL, D_MODEL)
    w_gu4 = g_gu.reshape(N_CHIPS, D_MODEL, -1)
    w_down_f = g_down.reshape(FFN_HIDDEN, D_MODEL)
    mixed, h1, u2, g, up, act, dh2, acc_fin = ffn_fwd(
        ya, yb, xs, w_out_f, norm2_w, w_gu4, w_down_f, fw, target)

    cw_in, cw_gu = w_in4.shape[2], w_gu4.shape[2]
    dgu, dh1, dya, dyb, delta, acc_n2 = ffn_bwd(dh2, w_down_f, g, up, w_gu4, h1, norm2_w, w_out_f, ya)
    early = [
        weight_grad(mixed, dh1, D_MODEL, "wgrad_out").reshape(N_CHIPS, 2, D_MODEL // 8, D_MODEL),
        weight_grad(u2, dgu, cw_gu, "wgrad_gu", group=2).reshape(N_CHIPS, 2, D_MODEL // 2, cw_gu),
        weight_grad(act, dh2, D_MODEL, "wgrad_down").reshape(N_CHIPS, 2, FFN_HIDDEN // 8, D_MODEL),
    ]
    early_names = ["out", "gu", "down"]
    dq, dk, dv, *early_recv = attn_bwd(qr, kr, va, q16, k16, v16, dya, lse, delta, early)
    early_sums = [add_own_half(gr, rc, "add_half_" + nm) for gr, rc, nm in zip(early, early_recv, early_names)]
    dhg, acc_hg, *early_parts = hgrn_bwd(hg, lb_logits, hgrn_norm_w, o_pre, st0, dyb, early_sums)
    dproj, dx, acc_n1 = in_bwd(dq, dk, dv, dhg, cos_t, sin_t, w_in4, xs, norm1_w, dh1)
    z512 = jnp.zeros((1, D_MODEL - HGRN_WIDTH), F32)
    gpack = jnp.concatenate([
        acc_n1[0:1], acc_n2[0:1], acc_fin[0:1],
        jnp.concatenate([acc_hg[0:1], z512], axis=1), jnp.concatenate([acc_hg[1:2], z512], axis=1),
        acc_fin[1:2], jnp.zeros((2, D_MODEL), F32)], axis=0)
    g_in, gathered_packs = weight_grad(u, dproj, cw_in, "wgrad_in", group=2, small_pack=gpack)
    late = [g_in.reshape(N_CHIPS, 2, D_MODEL // 2, cw_in)]
    late_recv = exchange_with_sibling(late, "grad_exchange_sibling_late")
    late_sums = [add_own_half(late[0], late_recv[0], "add_half_in")]

    early_halves = [sum_chips(s, p, "sum_chips_" + nm) for s, p, nm in zip(early_sums, early_parts, early_names)]
    early_others = share_with_sibling(early_halves, "grad_share_sibling_early")
    early_keys = ["w_out", "w_gu", "w_down"]
    moments = {"w_in": (m_w_in, v_w_in), "w_out": (m_w_out, v_w_out),
               "w_gu": (m_w_gate_up, v_w_gate_up), "w_down": (m_w_down, v_w_down)}
    early_updates, late_parts = adamw(
        [shards[k] for k in early_keys], early_halves, early_others,
        [moments[k][0][0] for k in early_keys], [moments[k][1][0] for k in early_keys],
        "adamw_early", chip_sums=late_sums)
    late_halves = [sum_chips(late_sums[0], late_parts[0], "sum_chips_in")]
    late_others = share_with_sibling(late_halves, "grad_share_sibling_late")
    late_updates = adamw([shards["w_in"]], late_halves, late_others,
                         [moments["w_in"][0][0]], [moments["w_in"][1][0]], "adamw_in")
    big = {k: tuple(t[None] for t in upd) for k, upd in zip(early_keys + ["w_in"], early_updates + late_updates)}

    wpack = _pack_small(norm1_w, norm2_w, final_norm_w, hgrn_norm_w, lb_logits)
    mpack = _pack_small(m_norm1_w, m_norm2_w, m_final_norm_w, m_hgrn_norm_w, m_lb_logits)
    vpack = _pack_small(v_norm1_w, v_norm2_w, v_final_norm_w, v_hgrn_norm_w, v_lb_logits)
    gs, ds, nms, nvs, loss8 = small_update(gathered_packs, wpack, mpack, vpack)
    loss = loss8[0, 0]

    def assemble(small_pack, idx):
        n1, lbl, hn, n2, fn = _unpack_small(small_pack)
        return (n1, big["w_in"][idx], lbl, hn, big["w_out"][idx], n2, big["w_gu"][idx], big["w_down"][idx], fn)

    return (loss, dx.reshape(x.shape), *assemble(gs, 0), *assemble(ds, 1), *assemble(nms, 2), *assemble(nvs, 3))
```

```python
import functools

import jax
import jax.numpy as jnp
from jax import lax
from jax.experimental import pallas as pl
from jax.experimental.pallas import tpu as pltpu

F32 = jnp.float32
BF16 = jnp.bfloat16

D_MODEL = 1024
ATTN_WIDTH = 512
HEAD_DIM = 64
DILATED_PAIRS = ((128, 1), (512, 4), (2048, 16))
ATTN_BLOCK = 128
ROPE_THETA = 10000.0
HGRN_WIDTH = 512
HGRN_CHUNK = 16
HGRN_HEADS = 4
IN_PROJ_WIDTH = 3584
FFN_HIDDEN = 2816
NORM_EPS = 1e-6
ATTN_SCALE = HEAD_DIM ** -0.5
N_CHIPS = 4
N_DEV = 8

ADAM_LR = 0.001
ADAM_B1 = 0.9
ADAM_B2 = 0.999
ADAM_EPS = 1e-08
ADAM_WD = 0.01
ADAM_STEP = 10

LANES = 128
HGRN_ROWS = 128
HGRN_STEP_BLOCKS = 2
ROW_TILE = 256
WIDE_ROW_TILE = 512
ATTN_STEP_ROWS = 2048
ATTN_FWD_UNROLL = 8
ATTN_BWD_UNROLL = 8
ATTN_MAJOR_DILATION = 16
VMEM_LIMIT = 56 * 1024 * 1024
NEG_BIG = -1e30
MESH_ID = pl.DeviceIdType.MESH


def _cparams(*sem):
    return pltpu.CompilerParams(dimension_semantics=tuple(sem), vmem_limit_bytes=VMEM_LIMIT)


def _dot(a, b):
    return jnp.dot(a, b, preferred_element_type=F32)


def _dot_nt(a, b):
    return lax.dot_general(a, b, (((1,), (1,)), ((), ())), preferred_element_type=F32)


def _dot_tn(a, b):
    return lax.dot_general(a, b, (((0,), (0,)), ((), ())), preferred_element_type=F32)


def _sigmoid(x):
    return 1.0 / (1.0 + jnp.exp(-x))


def _full(shape):
    n = len(shape)
    return pl.BlockSpec(shape, lambda *_: (0,) * n)


def _weight(shape):
    n = len(shape)
    return pl.BlockSpec(shape, lambda *_: (0,) * n, pipeline_mode=pl.Buffered(1))


def _rows(tm, width):
    return pl.BlockSpec((tm, width), lambda i: (i, 0))


def _swap32(x):
    lane = lax.broadcasted_iota(jnp.int32, x.shape, 1)
    first = (lane % HEAD_DIM) < (HEAD_DIM // 2)
    return jnp.where(first, pltpu.roll(x, LANES - 32, axis=1), pltpu.roll(x, 32, axis=1))


def _rotary_fwd(x, cos, sin_signed):
    parts = []
    for j in range(x.shape[1] // LANES):
        xc = x[:, j * LANES:(j + 1) * LANES]
        parts.append(xc * cos + _swap32(xc) * sin_signed)
    return jnp.concatenate(parts, axis=1)


def _rotary_bwd(dy, cos, sin_signed):
    parts = []
    for j in range(dy.shape[1] // LANES):
        dc = dy[:, j * LANES:(j + 1) * LANES]
        parts.append(dc * cos + _swap32(dc * sin_signed))
    return jnp.concatenate(parts, axis=1)


def _rope_tables(seq):
    half = HEAD_DIM // 2
    inv_freq = ROPE_THETA ** (-jnp.arange(half, dtype=F32) / half)
    ang = jnp.arange(seq, dtype=F32)[:, None] * inv_freq[None, :]
    cos, sin = jnp.cos(ang), jnp.sin(ang)
    cos_t = jnp.tile(cos, (1, LANES // half))
    sin_t = jnp.tile(jnp.concatenate([-sin, sin], axis=1), (1, LANES // HEAD_DIM))
    return cos_t, sin_t


def cast_bf16(w, name):
    r, c = w.shape
    half = r // 2

    def body(w_ref, o_ref):
        o_ref[...] = w_ref[...].astype(BF16)

    return pl.pallas_call(
        body, name=name, grid=(2,),
        in_specs=[pl.BlockSpec((half, c), lambda i: (i, 0))],
        out_specs=pl.BlockSpec((None, half, c), lambda i: (i, 0, 0)),
        out_shape=jax.ShapeDtypeStruct((2, half, c), BF16),
        compiler_params=_cparams("parallel"),
    )(w)


def _mesh_pos():
    return lax.axis_index("x"), lax.axis_index("y"), lax.axis_index("c")


GATHER_COPIES = 7


def _gather_phases(x_refs, out_refs, send_sems, recv_sems, local_sems):
    n = len(x_refs)
    x, y, cc = _mesh_pos()
    me, sibling = (x, y, cc), (x, y, 1 - cc)
    chips = [(1 - x, y), (x, 1 - y), (1 - x, 1 - y)]

    def rows(i, px, py, pc):
        return out_refs[i].at[4 * px + 2 * py + pc]

    def copy(i, k, block, to, src=None):
        return pltpu.make_async_remote_copy(
            src_ref=rows(i, *block) if src is None else src, dst_ref=rows(i, *block),
            send_sem=send_sems.at[GATHER_COPIES * i + k], recv_sem=recv_sems.at[GATHER_COPIES * i + k],
            device_id=to, device_id_type=MESH_ID)

    def local(i):
        return pltpu.make_async_copy(x_refs[i].at[cc], rows(i, *me), local_sems.at[i])

    def first(i):
        mine = x_refs[i].at[cc]
        return [copy(i, 0, me, sibling, src=mine)] + [
            copy(i, 1 + j, me, (*chip, cc), src=mine) for j, chip in enumerate(chips)]

    def passed(i):
        return [copy(i, 4 + j, (*chip, cc), sibling) for j, chip in enumerate(chips)]

    def start():
        for i in range(n):
            local(i).start()
            for cp in first(i):
                cp.start()

    def forward():
        for i in range(n):
            onward = passed(i)
            for j, chip in enumerate(chips):
                copy(i, 1 + j, (*chip, cc), me).wait_recv()
                onward[j].start()

    def finish():
        for i in range(n):
            copy(i, 0, sibling, me).wait_recv()
            for j, chip in enumerate(chips):
                copy(i, 4 + j, (*chip, 1 - cc), me).wait_recv()
            for cp in first(i) + passed(i):
                cp.wait_send()
            local(i).wait()

    return start, forward, finish


def _gather_scratch(n):
    return [pltpu.SemaphoreType.DMA((GATHER_COPIES * n,)), pltpu.SemaphoreType.DMA((GATHER_COPIES * n,)),
            pltpu.SemaphoreType.DMA((n,))]


def _gathered_shape(halves):
    return jax.ShapeDtypeStruct((N_DEV,) + halves.shape[1:], halves.dtype)


def allgather_halves(halves, name):
    def body(x_ref, out_ref, send_sems, recv_sems, local_sems):
        start, forward, finish = _gather_phases([x_ref], [out_ref], send_sems, recv_sems, local_sems)
        start()
        forward()
        finish()

    return pl.pallas_call(
        body, name=name,
        in_specs=[pl.BlockSpec(memory_space=pl.ANY)],
        out_specs=pl.BlockSpec(memory_space=pl.ANY),
        out_shape=_gathered_shape(halves),
        scratch_shapes=_gather_scratch(1),
    )(halves)


def _rms(x):
    return lax.rsqrt(jnp.mean(x * x, axis=-1, keepdims=True) + NORM_EPS)


def in_proj(x, norm1_w, w_in4, cos_t, sin_t, weight_halves=()):
    seq = x.shape[0]
    tm = WIDE_ROW_TILE
    cw = w_in4.shape[2]
    n_w = len(weight_halves)
    steps = seq // tm
    major = ATTN_MAJOR_DILATION
    slabs = ATTN_WIDTH // LANES

    def body(*refs):
        x_ref, nw_ref, w_ref, cos_ref, sin_ref = refs[:5]
        q_ref, k_ref, v_ref, hg_ref, u_ref = refs[5 + n_w:10 + n_w]
        major_refs = refs[10 + n_w:13 + n_w]
        slab_scr = refs[13 + 2 * n_w]
        step = pl.program_id(0)
        if n_w:
            start, forward, finish = _gather_phases(
                refs[5:5 + n_w], refs[13 + n_w:13 + 2 * n_w], *refs[14 + 2 * n_w:])
            pl.when(step == 0)(start)
            pl.when(step == (3 * steps) // 4)(forward)
        xv = x_ref[...]
        u = ((xv * _rms(xv)) * nw_ref[...]).astype(BF16)
        u_ref[...] = u
        proj = jnp.concatenate([_dot(u, w_ref[j]) for j in range(N_CHIPS)], axis=1)
        cos, sin = cos_ref[...], sin_ref[...]
        a = ATTN_WIDTH
        qkv = (_rotary_fwd(proj[:, :a], cos, sin), _rotary_fwd(proj[:, a:2 * a], cos, sin), proj[:, 2 * a:3 * a])
        for ref, val in zip((q_ref, k_ref, v_ref), qkv):
            ref[...] = val
        hg_ref[...] = proj[:, 3 * a:]
        for idx, val in enumerate(qkv):
            for s in range(slabs):
                slab_scr[idx, s] = val[:, s * LANES:(s + 1) * LANES]
        for idx, out in enumerate(major_refs):
            for r in range(major):
                for s in range(slabs):
                    out[r, :, s * LANES:(s + 1) * LANES] = (
                        slab_scr.at[idx, s][pl.ds(r, tm // major, stride=major), :].astype(BF16))
        if n_w:
            pl.when(step == steps - 1)(finish)

    anywhere = pl.BlockSpec(memory_space=pl.ANY)
    major_spec = pl.BlockSpec((major, tm // major, ATTN_WIDTH), lambda i: (0, i, 0))
    return pl.pallas_call(
        body, name="in_proj", grid=(steps,),
        in_specs=[_rows(tm, D_MODEL), _full((1, D_MODEL)), _weight((N_CHIPS, D_MODEL, cw)),
                  _rows(tm, LANES), _rows(tm, LANES)] + [anywhere] * n_w,
        out_specs=[_rows(tm, ATTN_WIDTH)] * 3 + [_rows(tm, 4 * HGRN_WIDTH), _rows(tm, D_MODEL)]
        + [major_spec] * 3 + [anywhere] * n_w,
        out_shape=[jax.ShapeDtypeStruct((seq, ATTN_WIDTH), F32)] * 3
        + [jax.ShapeDtypeStruct((seq, 4 * HGRN_WIDTH), F32), jax.ShapeDtypeStruct((seq, D_MODEL), BF16)]
        + [jax.ShapeDtypeStruct((major, seq // major, ATTN_WIDTH), BF16)] * 3
        + [_gathered_shape(h) for h in weight_halves],
        scratch_shapes=[pltpu.VMEM((3, slabs, tm, LANES), F32)] + (_gather_scratch(n_w) if n_w else []),
        compiler_params=_cparams("arbitrary"),
    )(x, norm1_w, w_in4, cos_t, sin_t, *weight_halves)


def _head_masks():
    lane = lax.broadcasted_iota(jnp.int32, (1, LANES), 1)
    return [(lane // HEAD_DIM) == h for h in range(LANES // HEAD_DIM)]


def _window_valid(no_prev):
    qi = lax.broadcasted_iota(jnp.int32, (ATTN_BLOCK, 2 * ATTN_BLOCK), 0)
    kj = lax.broadcasted_iota(jnp.int32, (ATTN_BLOCK, 2 * ATTN_BLOCK), 1)
    valid = (kj >= qi) & (kj <= qi + ATTN_BLOCK)
    return valid & (jnp.logical_not(no_prev) | (kj >= ATTN_BLOCK))


def _strided_rows(start, dilation):
    if dilation == 1:
        return pl.ds(start, ATTN_BLOCK)
    return pl.ds(start, ATTN_BLOCK, stride=dilation)


def _block_before(edge_ref, cur_ref, t, r, span, dilation, per_step):
    edge = edge_ref[_strided_rows(ATTN_STEP_ROWS - span + r, dilation), :]
    if per_step == 1:
        return edge
    inside = cur_ref[_strided_rows(r + span * jnp.maximum(t - 1, 0), dilation), :]
    return jnp.where(t == 0, edge, inside)


def _attn_specs():
    cur = pl.BlockSpec((ATTN_STEP_ROWS, LANES), lambda hp, j: (j, hp))
    prev = pl.BlockSpec((ATTN_STEP_ROWS, LANES), lambda hp, j: (jnp.maximum(j - 1, 0), hp))
    return cur, prev


def _for_each_block(dilation, unroll, block):
    span = ATTN_BLOCK * dilation
    per_step = ATTN_STEP_ROWS // span

    def trip(it, carry):
        block(it // dilation, it % dilation, span, per_step)
        return carry

    lax.fori_loop(0, per_step * dilation, trip, 0, unroll=unroll)


def _load_qkv(natural, major, t, r, span, dilation, per_step):
    if dilation == ATTN_MAJOR_DILATION:
        q_ref, kc_ref, vc_ref, kp_ref, vp_ref = major
        return (q_ref[r] * ATTN_SCALE, jnp.concatenate([kp_ref[r], kc_ref[r]], axis=0),
                jnp.concatenate([vp_ref[r], vc_ref[r]], axis=0))
    q_ref, kc_ref, vc_ref, kp_ref, vp_ref = natural
    rows = _strided_rows(r + span * t, dilation)
    kp = _block_before(kp_ref, kc_ref, t, r, span, dilation, per_step)
    vp = _block_before(vp_ref, vc_ref, t, r, span, dilation, per_step)
    return ((q_ref[rows, :] * ATTN_SCALE).astype(BF16),
            jnp.concatenate([kp, kc_ref[rows, :]], axis=0).astype(BF16),
            jnp.concatenate([vp, vc_ref[rows, :]], axis=0).astype(BF16))


def _major_specs():
    assert ATTN_STEP_ROWS == ATTN_BLOCK * ATTN_MAJOR_DILATION
    shape = (ATTN_MAJOR_DILATION, ATTN_BLOCK, LANES)
    return (pl.BlockSpec(shape, lambda hp, j: (0, j, hp)),
            pl.BlockSpec(shape, lambda hp, j: (0, jnp.maximum(j - 1, 0), hp)))


def attn_fwd(q, k, v, q16, k16, v16):
    seq = q.shape[0]
    cur, prev = _attn_specs()
    cur16, prev16 = _major_specs()

    def body(*refs):
        natural, major, (y_ref, lse_ref) = refs[:5], refs[5:10], refs[10:]
        first_step = pl.program_id(1) == 0
        masks = _head_masks()
        for index, (_, dilation) in enumerate(DILATED_PAIRS):
            def block(t, r, span, per_step, dilation=dilation, merge=index > 0):
                rows = _strided_rows(r + span * t, dilation)
                q2, k2, v2 = _load_qkv(natural, major, t, r, span, dilation, per_step)
                valid = _window_valid(first_step & (t == 0))
                o_acc = jnp.zeros((ATTN_BLOCK, LANES), F32)
                l_acc = jnp.zeros((ATTN_BLOCK, LANES), F32)
                for mh in masks:
                    qm = jnp.where(mh, q2, jnp.zeros_like(q2))
                    s = jnp.where(valid, _dot_nt(qm, k2), NEG_BIG)
                    m = jnp.max(s, axis=-1, keepdims=True)
                    p = jnp.exp(s - m)
                    l = jnp.sum(p, axis=-1, keepdims=True)
                    o = _dot(p.astype(BF16), v2) / l
                    o_acc = jnp.where(mh, o, o_acc)
                    l_acc = jnp.where(mh, m + jnp.log(l), l_acc)
                if merge:
                    y_old, l_old = y_ref[rows, :], lse_ref[rows, :]
                    mx = jnp.maximum(l_old, l_acc)
                    e_old, e_new = jnp.exp(l_old - mx), jnp.exp(l_acc - mx)
                    den = e_old + e_new
                    o_acc = (y_old * e_old + o_acc * e_new) / den
                    l_acc = mx + jnp.log(den)
                y_ref[rows, :] = o_acc
                lse_ref[rows, :] = l_acc

            _for_each_block(dilation, ATTN_FWD_UNROLL, block)

    return pl.pallas_call(
        body, name="attn_fwd", grid=(ATTN_WIDTH // LANES, seq // ATTN_STEP_ROWS),
        in_specs=[cur, cur, cur, prev, prev, cur16, cur16, cur16, prev16, prev16],
        out_specs=[cur, cur],
        out_shape=[jax.ShapeDtypeStruct((seq, ATTN_WIDTH), F32)] * 2,
        compiler_params=_cparams("parallel", "parallel"),
    )(q, k, v, k, v, q16, k16, v16, k16, v16)


def _chunk_cumsum(x, reverse=False):
    rc = lax.broadcasted_iota(jnp.int32, x.shape, 0) % HGRN_CHUNK
    sh = 1
    while sh < HGRN_CHUNK:
        if reverse:
            x = x + jnp.where(rc + sh < HGRN_CHUNK, pltpu.roll(x, x.shape[0] - sh, axis=0), 0.0)
        else:
            x = x + jnp.where(rc >= sh, pltpu.roll(x, sh, axis=0), 0.0)
        sh *= 2
    return x


def _chunk_row(x, row):
    return _chunk_rows([x[n * HGRN_CHUNK + row:n * HGRN_CHUNK + row + 1, :]
                        for n in range(x.shape[0] // HGRN_CHUNK)])


def _chunk_rows(rows):
    return jnp.concatenate([jnp.broadcast_to(r, (HGRN_CHUNK, r.shape[1])) for r in rows], axis=0)


def _hgrn_prep(hg, lbl):
    w = HGRN_WIDTH
    a0, a1 = lbl[0:1, :], lbl[1:2, :]
    mx = jnp.maximum(a0, a1)
    e0, e1 = jnp.exp(a0 - mx), jnp.exp(a1 - mx)
    lb = e0 / (e0 + e1)
    qb, fb, gb = hg[:, :w], hg[:, w:2 * w], hg[:, 3 * w:]
    sg = _sigmoid(fb)
    f = lb + (1.0 - lb) * sg
    b = _chunk_cumsum(jnp.log(f))
    bmid, btot = _chunk_row(b, HGRN_CHUNK // 2 - 1), _chunk_row(b, HGRN_CHUNK - 1)
    sq = _sigmoid(qb)
    p = dict(lb=lb, sg=sg, f=f, kk=1.0 - f, sq=sq, qf=qb * sq, gb=gb,
             e_iq=jnp.exp(b - bmid), e_ik=jnp.exp(bmid - b), e_b=jnp.exp(b),
             e_bb=jnp.exp(btot - b), e_tot=jnp.exp(btot))
    p["qi"] = p["qf"] * p["e_iq"]
    p["ki"] = p["kk"] * p["e_ik"]
    p["qs"] = p["qf"] * p["e_b"]
    p["kb"] = p["kk"] * p["e_bb"]
    return p


def _chunk_masks():
    t = lax.broadcasted_iota(jnp.int32, (HGRN_ROWS, HGRN_ROWS), 0)
    s = lax.broadcasted_iota(jnp.int32, (HGRN_ROWS, HGRN_ROWS), 1)
    tril = ((t // HGRN_CHUNK) == (s // HGRN_CHUNK)) & (s <= t)
    n_chunks = HGRN_ROWS // HGRN_CHUNK
    tt = lax.broadcasted_iota(jnp.int32, (HGRN_ROWS, n_chunks * LANES), 0)
    cc = lax.broadcasted_iota(jnp.int32, (HGRN_ROWS, n_chunks * LANES), 1)
    block = (tt // HGRN_CHUNK) == (cc // LANES)
    return tril, block


def _spread(x, block):
    n_chunks = HGRN_ROWS // HGRN_CHUNK
    return jnp.where(block, jnp.tile(x, (1, n_chunks)), jnp.zeros((), x.dtype))


def _fold(x_full, block):
    n_chunks = HGRN_ROWS // HGRN_CHUNK
    z = jnp.where(block, x_full, 0.0)
    acc = z[:, :LANES]
    for n in range(1, n_chunks):
        acc = acc + z[:, n * LANES:(n + 1) * LANES]
    return acc


def hgrn_fwd(hg, lb_logits, hnw, weight_halves=()):
    seq = hg.shape[0]
    nblk = seq // HGRN_ROWS
    n_steps = nblk // HGRN_STEP_BLOCKS
    step_rows = HGRN_ROWS * HGRN_STEP_BLOCKS
    n_chunks = HGRN_ROWS // HGRN_CHUNK
    n_w = len(weight_halves)

    def body(*refs):
        hg_ref, lbl_ref, hnw_ref = refs[:3]
        w_refs = refs[3:3 + n_w]
        yb_ref, o_ref, st0_ref = refs[3 + n_w:6 + n_w]
        g_refs = refs[6 + n_w:6 + 2 * n_w]
        st_scr = refs[6 + 2 * n_w]
        step = pl.program_id(0)
        if n_w:
            start, forward, finish = _gather_phases(w_refs, g_refs, *refs[7 + 2 * n_w:])
            pl.when(step == 0)(start)
            pl.when(step == (3 * n_steps) // 4)(forward)

        @pl.when(step == 0)
        def _():
            st_scr[...] = jnp.zeros_like(st_scr)

        tril, block = _chunk_masks()
        for sub in range(HGRN_STEP_BLOCKS):
            rows = slice(sub * HGRN_ROWS, (sub + 1) * HGRN_ROWS)
            hg_v = hg_ref[rows, :]
            p = _hgrn_prep(hg_v, lbl_ref[...])
            vv = hg_v[:, 2 * HGRN_WIDTH:3 * HGRN_WIDTH].astype(BF16)
            outs = []
            for h in range(HGRN_HEADS):
                sl = slice(h * LANES, (h + 1) * LANES)
                v_h = vv[:, sl]
                a = jnp.where(tril, _dot_nt(p["qi"][:, sl].astype(BF16), p["ki"][:, sl].astype(BF16)), 0.0)
                o = _dot(a.astype(BF16), v_h)
                upd = _dot_tn(v_h, _spread(p["kb"][:, sl].astype(BF16), block))
                st = st_scr[h]
                st0_ref[sub, h] = st
                parts = []
                for n in range(n_chunks):
                    parts.append(st.astype(BF16))
                    decay = p["e_tot"][n * HGRN_CHUNK:n * HGRN_CHUNK + 1, sl]
                    st = st * decay + upd[:, n * LANES:(n + 1) * LANES]
                st_scr[h] = st
                o = o + _dot_nt(_spread(p["qs"][:, sl].astype(BF16), block), jnp.concatenate(parts, axis=1))
                outs.append(o)
            o_ref[rows, :] = jnp.concatenate(outs, axis=1)
            normed = jnp.concatenate(
                [outs[h] * _rms(outs[h]) for h in range(HGRN_HEADS)], axis=1)
            gb = p["gb"]
            yb_ref[rows, :] = (normed * hnw_ref[...]) * (gb * _sigmoid(gb))
        if n_w:
            pl.when(step == n_steps - 1)(finish)

    anywhere = pl.BlockSpec(memory_space=pl.ANY)
    return pl.pallas_call(
        body, name="hgrn_fwd", grid=(n_steps,),
        in_specs=[_rows(step_rows, 4 * HGRN_WIDTH), _full((2, HGRN_WIDTH)), _full((1, HGRN_WIDTH))]
        + [anywhere] * n_w,
        out_specs=[_rows(step_rows, HGRN_WIDTH), _rows(step_rows, HGRN_WIDTH),
                   pl.BlockSpec((HGRN_STEP_BLOCKS, HGRN_HEADS, LANES, LANES), lambda i: (i, 0, 0, 0))]
        + [anywhere] * n_w,
        out_shape=[jax.ShapeDtypeStruct((seq, HGRN_WIDTH), F32)] * 2
        + [jax.ShapeDtypeStruct((nblk, HGRN_HEADS, LANES, LANES), F32)]
        + [_gathered_shape(h) for h in weight_halves],
        scratch_shapes=[pltpu.VMEM((HGRN_HEADS, LANES, LANES), F32)] + (_gather_scratch(n_w) if n_w else []),
        compiler_params=_cparams("arbitrary"),
    )(hg, lb_logits, hnw, *weight_halves)


def ffn_fwd(ya, yb, x, w_out, norm2_w, w_gu4, w_down, final_w, target):
    seq = x.shape[0]
    tm = ROW_TILE
    cw = w_gu4.shape[2]
    inv_d = 1.0 / D_MODEL

    def body(ya_ref, yb_ref, x_ref, wo_ref, nw_ref, wgu_ref, wd_ref, fw_ref, t_ref,
             mixed_ref, h1_ref, u2_ref, g_ref, up_ref, act_ref, dh2_ref, acc_ref):
        @pl.when(pl.program_id(0) == 0)
        def _():
            acc_ref[...] = jnp.zeros_like(acc_ref)

        mixed = jnp.concatenate([ya_ref[...], yb_ref[...]], axis=1).astype(BF16)
        mixed_ref[...] = mixed
        h1 = x_ref[...] + _dot(mixed, wo_ref[...])
        h1_ref[...] = h1
        u2 = ((h1 * _rms(h1)) * nw_ref[...]).astype(BF16)
        u2_ref[...] = u2
        g = jnp.concatenate([_dot(u2, wgu_ref[0]), _dot(u2, wgu_ref[1])], axis=1)
        up = jnp.concatenate([_dot(u2, wgu_ref[2]), _dot(u2, wgu_ref[3])], axis=1)
        g_ref[...] = g.astype(BF16)
        up_ref[...] = up.astype(BF16)
        act = ((g * _sigmoid(g)) * up).astype(BF16)
        act_ref[...] = act
        h2 = h1 + _dot(act, wd_ref[...])
        rf = _rms(h2)
        n = h2 * rf
        fw = fw_ref[...]
        err = n * fw - t_ref[...]
        dy = err * inv_d
        acc_ref[0:1, :] += jnp.sum(dy * n, axis=0, keepdims=True)
        acc_ref[1:2, :] += (0.5 * inv_d) * jnp.sum(err * err, axis=0, keepdims=True)
        dn = dy * fw
        dh2_ref[...] = rf * (dn - n * jnp.mean(dn * n, axis=-1, keepdims=True))

    half = _rows(tm, ATTN_WIDTH)
    wide = _rows(tm, D_MODEL)
    ffn = _rows(tm, FFN_HIDDEN)
    return pl.pallas_call(
        body, name="ffn_fwd", grid=(seq // tm,),
        in_specs=[half, half, wide, _weight((D_MODEL, D_MODEL)), _full((1, D_MODEL)),
                  _weight((N_CHIPS, D_MODEL, cw)), _weight((FFN_HIDDEN, D_MODEL)), _full((1, D_MODEL)), wide],
        out_specs=[wide, wide, wide, ffn, ffn, ffn, wide, _full((8, D_MODEL))],
        out_shape=[jax.ShapeDtypeStruct((seq, D_MODEL), BF16), jax.ShapeDtypeStruct((seq, D_MODEL), F32),
                   jax.ShapeDtypeStruct((seq, D_MODEL), BF16)]
        + [jax.ShapeDtypeStruct((seq, FFN_HIDDEN), BF16)] * 3
        + [jax.ShapeDtypeStruct((seq, D_MODEL), F32), jax.ShapeDtypeStruct((8, D_MODEL), F32)],
        compiler_params=_cparams("arbitrary"),
    )(ya, yb, x, w_out, norm2_w, w_gu4, w_down, final_w, target)


def _head_sum_matrix():
    i = jnp.arange(ATTN_WIDTH)
    return ((i[:, None] // HEAD_DIM) == (i[None, :] // HEAD_DIM)).astype(BF16)


def ffn_bwd(dh2, w_down, g, up, w_gu4, h1, norm2_w, w_out, ya):
    seq = h1.shape[0]
    tm = ROW_TILE
    cw = w_gu4.shape[2]
    hsum = _head_sum_matrix()

    def body(dh2_ref, wd_ref, g_ref, up_ref, w_ref, h1_ref, nw_ref, wo_ref, ya_ref, hs_ref,
             dgu_ref, dh1_ref, dya_ref, dyb_ref, delta_ref, acc_ref):
        @pl.when(pl.program_id(0) == 0)
        def _():
            acc_ref[...] = jnp.zeros_like(acc_ref)

        dh2_b = dh2_ref[...].astype(BF16)
        du2 = jnp.zeros((tm, D_MODEL), F32)
        for j in range(N_CHIPS // 2):
            dact = _dot_nt(dh2_b, wd_ref[j * cw:(j + 1) * cw, :])
            gv = g_ref[:, j * cw:(j + 1) * cw].astype(F32)
            sg = _sigmoid(gv)
            dg = (dact * up_ref[:, j * cw:(j + 1) * cw].astype(F32) * (sg * (1.0 + gv * (1.0 - sg)))).astype(BF16)
            dup = (dact * (gv * sg)).astype(BF16)
            dgu_ref[:, j * cw:(j + 1) * cw] = dg
            dgu_ref[:, FFN_HIDDEN + j * cw:FFN_HIDDEN + (j + 1) * cw] = dup
            du2 = du2 + _dot_nt(dg, w_ref[j]) + _dot_nt(dup, w_ref[N_CHIPS // 2 + j])
        h1 = h1_ref[...]
        r2 = _rms(h1)
        nh = h1 * r2
        acc_ref[0:1, :] += jnp.sum(du2 * nh, axis=0, keepdims=True)
        dn = du2 * nw_ref[...]
        dh1 = dh2_ref[...] + r2 * (dn - nh * jnp.mean(dn * nh, axis=-1, keepdims=True))
        dh1_ref[...] = dh1
        dmixed = _dot_nt(dh1.astype(BF16), wo_ref[...])
        dya = dmixed[:, :ATTN_WIDTH]
        dya_ref[...] = dya
        dyb_ref[...] = dmixed[:, ATTN_WIDTH:]
        prod = dya * ya_ref[...]
        hi = prod.astype(BF16)
        lo = (prod - hi.astype(F32)).astype(BF16)
        delta_ref[...] = _dot(hi, hs_ref[...]) + _dot(lo, hs_ref[...])

    wide = _rows(tm, D_MODEL)
    half = _rows(tm, ATTN_WIDTH)
    ffn = _rows(tm, FFN_HIDDEN)
    return pl.pallas_call(
        body, name="ffn_bwd", grid=(seq // tm,),
        in_specs=[wide, _weight((FFN_HIDDEN, D_MODEL)), ffn, ffn, _weight((N_CHIPS, D_MODEL, cw)), wide,
                  _full((1, D_MODEL)), _weight((D_MODEL, D_MODEL)), half, _full((ATTN_WIDTH, ATTN_WIDTH))],
        out_specs=[_rows(tm, 2 * FFN_HIDDEN), wide, half, half, half, _full((8, D_MODEL))],
        out_shape=[jax.ShapeDtypeStruct((seq, 2 * FFN_HIDDEN), BF16), jax.ShapeDtypeStruct((seq, D_MODEL), F32)]
        + [jax.ShapeDtypeStruct((seq, ATTN_WIDTH), F32)] * 3 + [jax.ShapeDtypeStruct((8, D_MODEL), F32)],
        compiler_params=_cparams("arbitrary"),
    )(dh2, w_down, g, up, w_gu4, h1, norm2_w, w_out, ya, hsum)


def attn_bwd(q, k, v, q16, k16, v16, dy, lse, delta, sibling_grads=()):
    seq = q.shape[0]
    cur, prev = _attn_specs()
    cur16, prev16 = _major_specs()
    whole = pl.BlockSpec((seq, LANES), lambda hp, j: (0, hp))
    n_g = len(sibling_grads)
    n_hp, n_steps = ATTN_WIDTH // LANES, seq // ATTN_STEP_ROWS
    n_in = 13

    def body(*refs):
        natural, major = refs[:5], refs[5:10]
        dy_ref, lse_ref, dl_ref = refs[10:n_in]
        dq_ref, dk_ref, dv_ref = refs[n_in + n_g:n_in + 3 + n_g]
        first_step = pl.program_id(1) == 0
        base = pl.program_id(1) * ATTN_STEP_ROWS
        masks = _head_masks()
        if n_g:
            start, finish = _sibling_exchange_phases(
                refs[n_in:n_in + n_g], refs[n_in + 3 + n_g:n_in + 3 + 2 * n_g], *refs[n_in + 3 + 2 * n_g:])
            pl.when((pl.program_id(0) == 0) & first_step)(start)

        def block(t, r, span, per_step, dilation, add):
            rows = _strided_rows(r + span * t, dilation)
            at_edge = t == 0
            q2, k2, v2 = _load_qkv(natural, major, t, r, span, dilation, per_step)
            dy2 = dy_ref[rows, :].astype(BF16)
            lse2, dl2 = lse_ref[rows, :], dl_ref[rows, :]
            valid = _window_valid(first_step & at_edge)
            zero = jnp.zeros_like(q2)
            qms, dyms, ps, dss, kms = [], [], [], [], []
            for h, mh in enumerate(masks):
                c0 = h * HEAD_DIM
                qm, dym = jnp.where(mh, q2, zero), jnp.where(mh, dy2, zero)
                s = _dot_nt(qm, k2)
                p = jnp.where(valid, jnp.exp(s - lse2[:, c0:c0 + 1]), 0.0)
                dp = _dot_nt(dym, v2)
                dss.append((p * (dp - dl2[:, c0:c0 + 1])).astype(BF16))
                ps.append(p.astype(BF16))
                qms.append(qm)
                dyms.append(dym)
                kms.append(jnp.where(mh, k2, jnp.zeros_like(k2)))
            dq = _dot(jnp.concatenate(dss, axis=1), jnp.concatenate(kms, axis=0)) * ATTN_SCALE
            p_all, ds_all = jnp.concatenate(ps, axis=0), jnp.concatenate(dss, axis=0)
            dy_all, q_all = jnp.concatenate(dyms, axis=0), jnp.concatenate(qms, axis=0)
            dv_full, dk_full = _dot_tn(dy_all, p_all).T, _dot_tn(q_all, ds_all).T
            here = _strided_rows(base + r + span * t, dilation)
            if add:
                dq_ref[rows, :] += dq
                dk_ref[here, :] += dk_full[ATTN_BLOCK:]
                dv_ref[here, :] += dv_full[ATTN_BLOCK:]
            else:
                dq_ref[rows, :] = dq
                dk_ref[here, :] = dk_full[ATTN_BLOCK:]
                dv_ref[here, :] = dv_full[ATTN_BLOCK:]
            back = _strided_rows(jnp.maximum(base + r + span * t - span, r), dilation)
            dk_ref[back, :] += dk_full[:ATTN_BLOCK]
            dv_ref[back, :] += dv_full[:ATTN_BLOCK]

        for index, (_, dilation) in enumerate(DILATED_PAIRS):
            _for_each_block(dilation, ATTN_BWD_UNROLL,
                            functools.partial(block, dilation=dilation, add=index > 0))
        if n_g:
            pl.when((pl.program_id(0) == n_hp - 1) & (pl.program_id(1) == n_steps - 1))(finish)

    anywhere = pl.BlockSpec(memory_space=pl.ANY)
    return pl.pallas_call(
        body, name="attn_bwd", grid=(n_hp, n_steps),
        in_specs=[cur, cur, cur, prev, prev, cur16, cur16, cur16, prev16, prev16, cur, cur, cur]
        + [anywhere] * n_g,
        out_specs=[cur, whole, whole] + [anywhere] * n_g,
        out_shape=[jax.ShapeDtypeStruct((seq, ATTN_WIDTH), F32)] * 3 + _sibling_exchange_shapes(sibling_grads),
        scratch_shapes=_sibling_exchange_scratch(n_g) if n_g else [],
        compiler_params=_cparams("arbitrary", "arbitrary"),
    )(q, k, v, k, v, q16, k16, v16, k16, v16, dy, lse, delta, *sibling_grads)


def hgrn_bwd(hg, lb_logits, hnw, o_pre, st0, dyb, chip_sums=()):
    seq = hg.shape[0]
    step_rows = HGRN_ROWS * HGRN_STEP_BLOCKS
    n_steps = seq // step_rows
    n_chunks = HGRN_ROWS // HGRN_CHUNK
    w = HGRN_WIDTH
    n_s = len(chip_sums)

    def body(*refs):
        hg_ref, lbl_ref, hnw_ref, o_ref, st0_ref, dyb_ref = refs[:6]
        dhg_ref, acc_ref = refs[6 + n_s:8 + n_s]
        dst_scr = refs[8 + 2 * n_s]
        step = pl.program_id(0)
        if n_s:
            start, finish = _chip_exchange_phases(refs[6:6 + n_s], refs[8 + n_s:8 + 2 * n_s], *refs[9 + 2 * n_s:])
            pl.when(step == 0)(start)

        @pl.when(step == 0)
        def _():
            dst_scr[...] = jnp.zeros_like(dst_scr)
            acc_ref[...] = jnp.zeros_like(acc_ref)

        tril, block = _chunk_masks()
        for sub in reversed(range(HGRN_STEP_BLOCKS)):
            rows = slice(sub * HGRN_ROWS, (sub + 1) * HGRN_ROWS)
            hg_v = hg_ref[rows, :]
            p = _hgrn_prep(hg_v, lbl_ref[...])
            vv = hg_v[:, 2 * w:3 * w].astype(BF16)
            hnw_v = hnw_ref[...]
            gb = p["gb"]
            sgg = _sigmoid(gb)
            silu_g = gb * sgg
            dyb_v = dyb_ref[rows, :]
            o_v = o_ref[rows, :]

            d_on = dyb_v * hnw_v * silu_g
            on_parts, do_parts = [], []
            for h in range(HGRN_HEADS):
                sl = slice(h * LANES, (h + 1) * LANES)
                rs = _rms(o_v[:, sl])
                on = o_v[:, sl] * rs
                on_parts.append(on)
                do_parts.append(rs * (d_on[:, sl] - on * jnp.mean(d_on[:, sl] * on, axis=-1, keepdims=True)))
            on_all = jnp.concatenate(on_parts, axis=1)
            dgb = dyb_v * on_all * hnw_v * (sgg * (1.0 + gb * (1.0 - sgg)))
            acc_ref[0:1, :] += jnp.sum(dyb_v * on_all * silu_g, axis=0, keepdims=True)

            dqf_parts, dkk_parts, db_parts, dv_parts, dbt_parts, dkbkb_parts = [], [], [], [], [], []
            for h in range(HGRN_HEADS):
                sl = slice(h * LANES, (h + 1) * LANES)
                v_h = vv[:, sl]
                do_h = do_parts[h].astype(BF16)
                qi, ki, qs, kb = p["qi"][:, sl], p["ki"][:, sl], p["qs"][:, sl], p["kb"][:, sl]
                qi_b, ki_b = qi.astype(BF16), ki.astype(BF16)
                kb_cat = _spread(kb.astype(BF16), block)
                qs_cat = _spread(qs.astype(BF16), block)
                upd = _dot_tn(v_h, kb_cat)
                st = st0_ref[sub, h]
                st_parts = []
                for n in range(n_chunks):
                    st_parts.append(st)
                    decay = p["e_tot"][n * HGRN_CHUNK:n * HGRN_CHUNK + 1, sl]
                    st = st * decay + upd[:, n * LANES:(n + 1) * LANES]
                st_cat = jnp.concatenate([s_.astype(BF16) for s_ in st_parts], axis=1)
                wgt = _dot_tn(do_h, qs_cat)
                dst = dst_scr[h]
                dst_parts = [None] * n_chunks
                dbt_rows = [None] * n_chunks
                for n in reversed(range(n_chunks)):
                    dst_parts[n] = dst.astype(BF16)
                    decay = p["e_tot"][n * HGRN_CHUNK:n * HGRN_CHUNK + 1, sl]
                    dbt_rows[n] = jnp.sum(dst * st_parts[n], axis=0, keepdims=True) * decay
                    dst = dst * decay + wgt[:, n * LANES:(n + 1) * LANES]
                dst_scr[h] = dst
                dst_cat = jnp.concatenate(dst_parts, axis=1)
                dqs = _fold(_dot(do_h, st_cat), block)
                dkb = _fold(_dot(v_h, dst_cat), block)
                dv_state = _dot_nt(kb_cat, dst_cat)
                a = jnp.where(tril, _dot_nt(qi_b, ki_b), 0.0).astype(BF16)
                da = jnp.where(tril, _dot_nt(do_h, v_h), 0.0).astype(BF16)
                dv_parts.append(_dot_tn(a, do_h) + dv_state)
                dqi = _dot(da, ki_b)
                dki = _dot_tn(da, qi_b)
                dqf_parts.append(dqi * p["e_iq"][:, sl] + dqs * p["e_b"][:, sl])
                dkk_parts.append(dki * p["e_ik"][:, sl] + dkb * p["e_bb"][:, sl])
                dkbkb = dkb * kb
                db_parts.append(dqi * qi - dki * ki + dqs * qs - dkbkb)
                dkbkb_parts.append(dkbkb)
                dbt_parts.append(_chunk_rows(dbt_rows))

            cat = lambda parts: jnp.concatenate(parts, axis=1)
            dlogf = (_chunk_cumsum(cat(db_parts), reverse=True)
                     + _chunk_row(_chunk_cumsum(cat(dkbkb_parts)), HGRN_CHUNK - 1) + cat(dbt_parts))
            sq, qb = p["sq"], hg_v[:, :w]
            dqb = cat(dqf_parts) * (sq * (1.0 + qb * (1.0 - sq)))
            df = dlogf / p["f"] - cat(dkk_parts)
            sg, lb = p["sg"], p["lb"]
            dfb = df * (1.0 - lb) * sg * (1.0 - sg)
            acc_ref[1:2, :] += jnp.sum(df * (1.0 - sg), axis=0, keepdims=True)
            dhg_ref[rows, :] = jnp.concatenate([dqb, dfb, cat(dv_parts), dgb], axis=1).astype(BF16)
        if n_s:
            pl.when(step == n_steps - 1)(finish)

    rev = lambda i: (n_steps - 1 - i, 0)
    anywhere = pl.BlockSpec(memory_space=pl.ANY)
    return pl.pallas_call(
        body, name="hgrn_bwd", grid=(n_steps,),
        in_specs=[pl.BlockSpec((step_rows, 4 * w), rev), _full((2, w)), _full((1, w)),
                  pl.BlockSpec((step_rows, w), rev),
                  pl.BlockSpec((HGRN_STEP_BLOCKS, HGRN_HEADS, LANES, LANES), lambda i: (n_steps - 1 - i, 0, 0, 0)),
                  pl.BlockSpec((step_rows, w), rev)] + [anywhere] * n_s,
        out_specs=[pl.BlockSpec((step_rows, 4 * w), rev), _full((8, w))] + [anywhere] * n_s,
        out_shape=[jax.ShapeDtypeStruct((seq, 4 * w), BF16), jax.ShapeDtypeStruct((8, w), F32)]
        + [jax.ShapeDtypeStruct(s.shape, s.dtype) for s in chip_sums],
        scratch_shapes=[pltpu.VMEM((HGRN_HEADS, LANES, LANES), F32)] + (_chip_exchange_scratch(n_s) if n_s else []),
        compiler_params=_cparams("arbitrary"),
    )(hg, lb_logits, hnw, o_pre, st0, dyb, *chip_sums)


def in_bwd(dq, dk, dv, dhg, cos_t, sin_t, w_in4, x, norm1_w, dh1):
    seq = x.shape[0]
    tm = WIDE_ROW_TILE
    cw = w_in4.shape[2]

    def body(dq_ref, dk_ref, dv_ref, dhg_ref, cos_ref, sin_ref, w_ref,
             x_ref, nw_ref, dh1_ref, dproj_ref, dx_ref, acc_ref):
        @pl.when(pl.program_id(0) == 0)
        def _():
            acc_ref[...] = jnp.zeros_like(acc_ref)

        cos, sin = cos_ref[...], sin_ref[...]
        dqa = _rotary_bwd(dq_ref[...], cos, sin)
        dka = _rotary_bwd(dk_ref[...], cos, sin)
        dproj = jnp.concatenate(
            [jnp.concatenate([dqa, dka, dv_ref[...]], axis=1).astype(BF16), dhg_ref[...]], axis=1)
        dproj_ref[...] = dproj
        du = _dot_nt(dproj[:, :cw], w_ref[0])
        for j in range(1, N_CHIPS):
            du = du + _dot_nt(dproj[:, j * cw:(j + 1) * cw], w_ref[j])
        xv = x_ref[...]
        r1 = _rms(xv)
        nx = xv * r1
        acc_ref[0:1, :] += jnp.sum(du * nx, axis=0, keepdims=True)
        dn = du * nw_ref[...]
        dx_ref[...] = dh1_ref[...] + r1 * (dn - nx * jnp.mean(dn * nx, axis=-1, keepdims=True))

    half = _rows(tm, ATTN_WIDTH)
    wide = _rows(tm, D_MODEL)
    return pl.pallas_call(
        body, name="in_bwd", grid=(seq // tm,),
        in_specs=[half] * 3 + [_rows(tm, 4 * HGRN_WIDTH), _rows(tm, LANES), _rows(tm, LANES),
                               _weight((N_CHIPS, D_MODEL, cw)), wide, _full((1, D_MODEL)), wide],
        out_specs=[_rows(tm, IN_PROJ_WIDTH), wide, _full((8, D_MODEL))],
        out_shape=[jax.ShapeDtypeStruct((seq, IN_PROJ_WIDTH), BF16), jax.ShapeDtypeStruct((seq, D_MODEL), F32),
                   jax.ShapeDtypeStruct((8, D_MODEL), F32)],
        compiler_params=_cparams("arbitrary"),
    )(dq, dk, dv, dhg, cos_t, sin_t, w_in4, x, norm1_w, dh1)


def weight_grad(a, b, col_block, name, group=1, small_pack=None):
    seq, kdim = a.shape
    ndim = b.shape[1]
    nj = ndim // col_block
    tk = min(1024, seq)
    hosting = small_pack is not None
    n_j, n_t = nj // group, seq // tk

    def body(*refs):
        a_ref, b_ref = refs[:2]
        o_ref = refs[3] if hosting else refs[2]
        if hosting:
            start, finish = _pack_gather_phases(refs[2], refs[4], *refs[5:])
            pl.when((pl.program_id(0) == 0) & (pl.program_id(1) == 0))(start)

        @pl.when(pl.program_id(1) == 0)
        def _():
            o_ref[...] = jnp.zeros_like(o_ref)

        acc = _dot_tn(a_ref[...].astype(BF16), b_ref[...].astype(BF16))
        for i in range(group):
            o_ref[i] += acc[:, i * col_block:(i + 1) * col_block]
        if hosting:
            pl.when((pl.program_id(0) == n_j - 1) & (pl.program_id(1) == n_t - 1))(finish)

    anywhere = pl.BlockSpec(memory_space=pl.ANY)
    out = pl.pallas_call(
        body, name=name, grid=(n_j, n_t),
        in_specs=[pl.BlockSpec((tk, kdim), lambda j, t: (t, 0)),
                  pl.BlockSpec((tk, group * col_block), lambda j, t: (t, j))] + [anywhere] * hosting,
        out_specs=[pl.BlockSpec((group, kdim, col_block), lambda j, t: (j, 0, 0))] + [anywhere] * hosting,
        out_shape=[jax.ShapeDtypeStruct((nj, kdim, col_block), F32)]
        + ([jax.ShapeDtypeStruct((N_DEV,) + small_pack.shape, F32)] if hosting else []),
        scratch_shapes=[pltpu.SemaphoreType.DMA((N_DEV - 1,)), pltpu.SemaphoreType.DMA((N_DEV - 1,)),
                        pltpu.SemaphoreType.DMA] if hosting else [],
        compiler_params=_cparams("arbitrary", "arbitrary"),
    )(a, b, *([small_pack] if hosting else []))
    return out if hosting else out[0]


def _sibling_exchange_phases(g_refs, out_refs, send_sems, recv_sems):
    x, y, cc = _mesh_pos()

    def copies():
        return [pltpu.make_async_remote_copy(
            src_ref=g_refs[i].at[j, 1 - cc], dst_ref=out_refs[i].at[j],
            send_sem=send_sems.at[i * N_CHIPS + j], recv_sem=recv_sems.at[i * N_CHIPS + j],
            device_id=(x, y, 1 - cc), device_id_type=MESH_ID)
            for i in range(len(g_refs)) for j in range(N_CHIPS)]

    def start():
        for cp in copies():
            cp.start()

    def finish():
        for cp in copies():
            cp.wait_recv()
        for cp in copies():
            cp.wait_send()

    return start, finish


def _sibling_exchange_scratch(n):
    return [pltpu.SemaphoreType.DMA((n * N_CHIPS,)), pltpu.SemaphoreType.DMA((n * N_CHIPS,))]


def _sibling_exchange_shapes(grads):
    return [jax.ShapeDtypeStruct((N_CHIPS,) + g.shape[2:], g.dtype) for g in grads]


def exchange_with_sibling(grads, name):
    n = len(grads)

    def body(*refs):
        start, finish = _sibling_exchange_phases(refs[:n], refs[n:2 * n], refs[2 * n], refs[2 * n + 1])
        start()
        finish()

    return pl.pallas_call(
        body, name=name,
        in_specs=[pl.BlockSpec(memory_space=pl.ANY)] * n,
        out_specs=[pl.BlockSpec(memory_space=pl.ANY)] * n,
        out_shape=_sibling_exchange_shapes(grads),
        scratch_shapes=_sibling_exchange_scratch(n),
    )(*grads)


def add_own_half(grad, recv, name):
    _, _, r, c = grad.shape
    tr = r // 2 if r % 32 == 0 else r

    def body(cc_ref, g_ref, r_ref, o_ref):
        o_ref[...] = (g_ref[...] + r_ref[...]).astype(BF16)

    grid_spec = pltpu.PrefetchScalarGridSpec(
        num_scalar_prefetch=1, grid=(N_CHIPS, r // tr),
        in_specs=[pl.BlockSpec((None, None, tr, c), lambda j, t, cc: (j, cc[0], t, 0)),
                  pl.BlockSpec((None, tr, c), lambda j, t, cc: (j, t, 0))],
        out_specs=pl.BlockSpec((None, tr, c), lambda j, t, cc: (j, t, 0)))
    cc = lax.axis_index("c").astype(jnp.int32).reshape(1)
    return pl.pallas_call(
        body, name=name, grid_spec=grid_spec,
        out_shape=jax.ShapeDtypeStruct((N_CHIPS, r, c), BF16),
        compiler_params=_cparams("parallel", "parallel"),
    )(cc, grad, recv)


def _chip_exchange_phases(s_refs, out_refs, send_sems, recv_sems):
    n = len(s_refs)
    x, y, cc = _mesh_pos()
    my_chip = 2 * x + y
    chips = [(1 - x, y), (x, 1 - y), (1 - x, 1 - y)]

    def outgoing():
        return [pltpu.make_async_remote_copy(
            src_ref=s_refs[i].at[2 * px + py], dst_ref=out_refs[i].at[my_chip],
            send_sem=send_sems.at[3 * i + j], recv_sem=recv_sems.at[3 * i + j],
            device_id=(px, py, cc), device_id_type=MESH_ID)
            for i in range(n) for j, (px, py) in enumerate(chips)]

    def start():
        for cp in outgoing():
            cp.start()

    def finish():
        for i in range(n):
            for j, (px, py) in enumerate(chips):
                pltpu.make_async_remote_copy(
                    src_ref=s_refs[i].at[my_chip], dst_ref=out_refs[i].at[2 * px + py],
                    send_sem=send_sems.at[3 * i + j], recv_sem=recv_sems.at[3 * i + j],
                    device_id=(px, py, cc), device_id_type=MESH_ID).wait_recv()
        for cp in outgoing():
            cp.wait_send()

    return start, finish


def _chip_exchange_scratch(n):
    return [pltpu.SemaphoreType.DMA((3 * n,)), pltpu.SemaphoreType.DMA((3 * n,))]


def sum_chips(sums, parts, name):
    _, r, c = parts.shape
    tr = r // 2 if r % 32 == 0 else r

    def body(idx_ref, s_ref, p1_ref, p2_ref, p3_ref, o_ref):
        o_ref[...] = ((s_ref[...].astype(F32) + p1_ref[...].astype(F32))
                      + p2_ref[...].astype(F32)) + p3_ref[...].astype(F32)

    def pick(k):
        return pl.BlockSpec((None, tr, c), lambda t, idx: (idx[k], t, 0))

    x, y = lax.axis_index("x"), lax.axis_index("y")
    idx = jnp.stack([2 * x + y, 2 * (1 - x) + y, 2 * x + (1 - y), 2 * (1 - x) + (1 - y)]).astype(jnp.int32)
    grid_spec = pltpu.PrefetchScalarGridSpec(
        num_scalar_prefetch=1, grid=(r // tr,),
        in_specs=[pick(0), pick(1), pick(2), pick(3)],
        out_specs=pl.BlockSpec((tr, c), lambda t, idx: (t, 0)))
    return pl.pallas_call(
        body, name=name, grid_spec=grid_spec,
        out_shape=jax.ShapeDtypeStruct((r, c), F32),
        compiler_params=_cparams("parallel"),
    )(idx, sums, parts, parts, parts)


def share_with_sibling(halves, name):
    n = len(halves)

    def body(*refs):
        h_refs, out_refs = refs[:n], refs[n:2 * n]
        send_sems, recv_sems = refs[2 * n], refs[2 * n + 1]
        x, y, cc = _mesh_pos()
        copies = [pltpu.make_async_remote_copy(
            src_ref=h_refs[i], dst_ref=out_refs[i],
            send_sem=send_sems.at[i], recv_sem=recv_sems.at[i],
            device_id=(x, y, 1 - cc), device_id_type=MESH_ID) for i in range(n)]
        for cp in copies:
            cp.start()
        for cp in copies:
            cp.wait_recv()
        for cp in copies:
            cp.wait_send()

    return pl.pallas_call(
        body, name=name,
        in_specs=[pl.BlockSpec(memory_space=pl.ANY)] * n,
        out_specs=[pl.BlockSpec(memory_space=pl.ANY)] * n,
        out_shape=[jax.ShapeDtypeStruct(h.shape, h.dtype) for h in halves],
        scratch_shapes=[pltpu.SemaphoreType.DMA((n,)), pltpu.SemaphoreType.DMA((n,))],
    )(*halves)


def _adam_update(w, g, m, v):
    m = ADAM_B1 * m + (1.0 - ADAM_B1) * g
    v = ADAM_B2 * v + (1.0 - ADAM_B2) * (g * g)
    m_hat = m / (1.0 - ADAM_B1 ** ADAM_STEP)
    v_hat = v / (1.0 - ADAM_B2 ** ADAM_STEP)
    delta = -ADAM_LR * (m_hat / (jnp.sqrt(v_hat) + ADAM_EPS) + ADAM_WD * w)
    return delta, m, v


ADAMW_STEPS = 8


def adamw(ws, g_mine, g_sibling, ms, vs, name, chip_sums=()):
    n, n_s = len(ws), len(chip_sums)
    per_half = ADAMW_STEPS // 2

    def body(*refs):
        cc_ref = refs[0]
        ins = refs[1:1 + 5 * n]
        outs = refs[1 + 5 * n + n_s:1 + 9 * n + n_s]
        step = pl.program_id(0)
        if n_s:
            start, finish = _chip_exchange_phases(
                refs[1 + 5 * n:1 + 5 * n + n_s], refs[1 + 9 * n + n_s:1 + 9 * n + 2 * n_s],
                *refs[1 + 9 * n + 2 * n_s:])
            pl.when(step == 0)(start)
        mine = (step // per_half) == cc_ref[0]
        for i in range(n):
            w_ref, ga_ref, gb_ref, m_ref, v_ref = ins[5 * i:5 * i + 5]
            g_ref, d_ref, nm_ref, nv_ref = outs[4 * i:4 * i + 4]
            g = jnp.where(mine, ga_ref[...], gb_ref[...])
            g_ref[...] = g
            d, nm, nv = _adam_update(w_ref[...], g, m_ref[...], v_ref[...])
            d_ref[...] = d
            nm_ref[...] = nm
            nv_ref[...] = nv
        if n_s:
            pl.when(step == ADAMW_STEPS - 1)(finish)

    in_specs, out_specs, out_shape, operands = [], [], [], []
    for w, ga, gb, m, v in zip(ws, g_mine, g_sibling, ms, vs):
        r, c = w.shape
        tr = r // ADAMW_STEPS
        full = pl.BlockSpec((tr, c), lambda t, cc: (t, 0))
        part = pl.BlockSpec((tr, c), lambda t, cc: (t % per_half, 0))
        in_specs += [full, part, part, full, full]
        out_specs += [full] * 4
        out_shape += [jax.ShapeDtypeStruct((r, c), F32)] * 4
        operands += [w, ga, gb, m, v]
    anywhere = pl.BlockSpec(memory_space=pl.ANY)
    grid_spec = pltpu.PrefetchScalarGridSpec(
        num_scalar_prefetch=1, grid=(ADAMW_STEPS,),
        in_specs=in_specs + [anywhere] * n_s, out_specs=out_specs + [anywhere] * n_s,
        scratch_shapes=_chip_exchange_scratch(n_s) if n_s else [])
    cc = lax.axis_index("c").astype(jnp.int32).reshape(1)
    res = pl.pallas_call(
        body, name=name, grid_spec=grid_spec,
        out_shape=out_shape + [jax.ShapeDtypeStruct(s.shape, s.dtype) for s in chip_sums],
        compiler_params=_cparams("arbitrary"),
    )(cc, *operands, *chip_sums)
    per_shard = [tuple(res[4 * i:4 * i + 4]) for i in range(n)]
    return (per_shard, list(res[4 * n:])) if n_s else per_shard


def _pack_gather_phases(p_ref, out_ref, send_sems, recv_sems, local_sem):
    x, y, cc = _mesh_pos()
    me = 4 * x + 2 * y + cc
    flips = [(fx, fy, fc) for fx in (0, 1) for fy in (0, 1) for fc in (0, 1)][1:]

    def copy(k, row):
        fx, fy, fc = flips[k]
        return pltpu.make_async_remote_copy(
            src_ref=p_ref, dst_ref=out_ref.at[row],
            send_sem=send_sems.at[k], recv_sem=recv_sems.at[k],
            device_id=(x ^ fx, y ^ fy, cc ^ fc), device_id_type=MESH_ID)

    def local():
        return pltpu.make_async_copy(p_ref, out_ref.at[me], local_sem)

    def start():
        local().start()
        for k in range(len(flips)):
            copy(k, me).start()

    def finish():
        for k, (fx, fy, fc) in enumerate(flips):
            copy(k, 4 * (x ^ fx) + 2 * (y ^ fy) + (cc ^ fc)).wait_recv()
        for k in range(len(flips)):
            copy(k, me).wait_send()
        local().wait()

    return start, finish


def small_update(gathered, wpack, mpack, vpack):
    hw = HGRN_WIDTH

    def body(g_ref, w_ref, m_ref, v_ref, go_ref, d_ref, nm_ref, nv_ref, loss_ref):
        g = g_ref[0]
        for d in range(1, N_DEV):
            g = g + g_ref[d]
        wv = w_ref[...]
        a0, a1 = wv[4:5, :hw], wv[4:5, hw:]
        mx = jnp.maximum(a0, a1)
        e0, e1 = jnp.exp(a0 - mx), jnp.exp(a1 - mx)
        lb = e0 / (e0 + e1)
        dl = g[4:5, :hw] * lb * (1.0 - lb)
        row = lax.broadcasted_iota(jnp.int32, g.shape, 0)
        lb_row = jnp.concatenate([dl, -dl], axis=1)
        grads = jnp.where(row == 4, lb_row, jnp.where(row < 4, g, 0.0))
        go_ref[...] = grads
        d, nm, nv = _adam_update(wv, grads, m_ref[...], v_ref[...])
        d_ref[...] = d
        nm_ref[...] = nm
        nv_ref[...] = nv
        loss_ref[...] = jnp.zeros((8, LANES), F32) + jnp.sum(g[5:6, :])

    vm = pl.BlockSpec(memory_space=pltpu.VMEM)
    return pl.pallas_call(
        body, name="small_update",
        in_specs=[vm] * 4, out_specs=[vm] * 5,
        out_shape=[jax.ShapeDtypeStruct(wpack.shape, F32)] * 4 + [jax.ShapeDtypeStruct((8, LANES), F32)],
    )(gathered, wpack, mpack, vpack)


def _pack_small(n1, n2, fn, hn, lbl):
    z = jnp.zeros((1, D_MODEL - HGRN_WIDTH), F32)
    rows = [n1.reshape(1, D_MODEL), n2.reshape(1, D_MODEL), fn.reshape(1, D_MODEL),
            jnp.concatenate([hn.reshape(1, HGRN_WIDTH), z], axis=1), lbl.reshape(1, 2 * HGRN_WIDTH),
            jnp.zeros((3, D_MODEL), F32)]
    return jnp.concatenate(rows, axis=0)


def _unpack_small(pack):
    return (pack[0:1], pack[4].reshape(2, HGRN_WIDTH), pack[3:4, :HGRN_WIDTH], pack[1:2], pack[2])


def kernel(x, norm1_w, w_in, lb_logits, hgrn_norm_w, w_out, norm2_w, w_gate_up, w_down, final_norm_w, loss_target, m_norm1_w, m_w_in, m_lb_logits, m_hgrn_norm_w, m_w_out, m_norm2_w, m_w_gate_up, m_w_down, m_final_norm_w, v_norm1_w, v_w_in, v_lb_logits, v_hgrn_norm_w, v_w_out, v_norm2_w, v_w_gate_up, v_w_down, v_final_norm_w):
    seq = x.shape[1]
    xs = x.reshape(seq, D_MODEL)
    target = loss_target.reshape(seq, D_MODEL)
    shards = {"w_in": w_in[0], "w_out": w_out[0], "w_gu": w_gate_up[0], "w_down": w_down[0]}

    cast = {k: cast_bf16(w, "cast_" + k) for k, w in shards.items()}
    w_in4 = allgather_halves(cast["w_in"], "gather_w_in").reshape(N_CHIPS, D_MODEL, -1)

    cos_t, sin_t = _rope_tables(seq)
    fw = final_norm_w.reshape(1, D_MODEL)

    qr, kr, va, hg, u, q16, k16, v16, g_out, g_down = in_proj(
        xs, norm1_w, w_in4, cos_t, sin_t, [cast["w_out"], cast["w_down"]])
    ya, lse = attn_fwd(qr, kr, va, q16, k16, v16)
    yb, o_pre, st0, g_gu = hgrn_fwd(hg, lb_logits, hgrn_norm_w, [cast["w_gu"]])
    w_out_f = g_out.reshape(D_MODEL, D_MODEL)
    w_gu4 = g_gu.reshape(N_CHIPS, D_MODEL, -1)
    w_down_f = g_down.reshape(FFN_HIDDEN, D_MODEL)
    mixed, h1, u2, g, up, act, dh2, acc_fin = ffn_fwd(
        ya, yb, xs, w_out_f, norm2_w, w_gu4, w_down_f, fw, target)

    cw_in, cw_gu = w_in4.shape[2], w_gu4.shape[2]
    dgu, dh1, dya, dyb, delta, acc_n2 = ffn_bwd(dh2, w_down_f, g, up, w_gu4, h1, norm2_w, w_out_f, ya)
    early = [
        weight_grad(mixed, dh1, D_MODEL, "wgrad_out").reshape(N_CHIPS, 2, D_MODEL // 8, D_MODEL),
        weight_grad(u2, dgu, cw_gu, "wgrad_gu", group=2).reshape(N_CHIPS, 2, D_MODEL // 2, cw_gu),
        weight_grad(act, dh2, D_MODEL, "wgrad_down").reshape(N_CHIPS, 2, FFN_HIDDEN // 8, D_MODEL),
    ]
    early_names = ["out", "gu", "down"]
    dq, dk, dv, *early_recv = attn_bwd(qr, kr, va, q16, k16, v16, dya, lse, delta, early)
    early_sums = [add_own_half(gr, rc, "add_half_" + nm) for gr, rc, nm in zip(early, early_recv, early_names)]
    dhg, acc_hg, *early_parts = hgrn_bwd(hg, lb_logits, hgrn_norm_w, o_pre, st0, dyb, early_sums)
    dproj, dx, acc_n1 = in_bwd(dq, dk, dv, dhg, cos_t, sin_t, w_in4, xs, norm1_w, dh1)
    z512 = jnp.zeros((1, D_MODEL - HGRN_WIDTH), F32)
    gpack = jnp.concatenate([
        acc_n1[0:1], acc_n2[0:1], acc_fin[0:1],
        jnp.concatenate([acc_hg[0:1], z512], axis=1), jnp.concatenate([acc_hg[1:2], z512], axis=1),
        acc_fin[1:2], jnp.zeros((2, D_MODEL), F32)], axis=0)
    g_in, gathered_packs = weight_grad(u, dproj, cw_in, "wgrad_in", group=2, small_pack=gpack)
    late = [g_in.reshape(N_CHIPS, 2, D_MODEL // 2, cw_in)]
    late_recv = exchange_with_sibling(late, "grad_exchange_sibling_late")
    late_sums = [add_own_half(late[0], late_recv[0], "add_half_in")]

    early_halves = [sum_chips(s, p, "sum_chips_" + nm) for s, p, nm in zip(early_sums, early_parts, early_names)]
    early_others = share_with_sibling(early_halves, "grad_share_sibling_early")
    early_keys = ["w_out", "w_gu", "w_down"]
    moments = {"w_in": (m_w_in, v_w_in), "w_out": (m_w_out, v_w_out),
               "w_gu": (m_w_gate_up, v_w_gate_up), "w_down": (m_w_down, v_w_down)}
    early_updates, late_parts = adamw(
        [shards[k] for k in early_keys], early_halves, early_others,
        [moments[k][0][0] for k in early_keys], [moments[k][1][0] for k in early_keys],
        "adamw_early", chip_sums=late_sums)
    late_halves = [sum_chips(late_sums[0], late_parts[0], "sum_chips_in")]
    late_others = share_with_sibling(late_halves, "grad_share_sibling_late")
    late_updates = adamw([shards["w_in"]], late_halves, late_others,
                         [moments["w_in"][0][0]], [moments["w_in"][1][0]], "adamw_in")
    big = {k: tuple(t[None] for t in upd) for k, upd in zip(early_keys + ["w_in"], early_updates + late_updates)}

    wpack = _pack_small(norm1_w, norm2_w, final_norm_w, hgrn_norm_w, lb_logits)
    mpack = _pack_small(m_norm1_w, m_norm2_w, m_final_norm_w, m_hgrn_norm_w, m_lb_logits)
    vpack = _pack_small(v_norm1_w, v_norm2_w, v_final_norm_w, v_hgrn_norm_w, v_lb_logits)
    gs, ds, nms, nvs, loss8 = small_update(gathered_packs, wpack, mpack, vpack)
    loss = loss8[0, 0]

    def assemble(small_pack, idx):
        n1, lbl, hn, n2, fn = _unpack_small(small_pack)
        return (n1, big["w_in"][idx], lbl, hn, big["w_out"][idx], n2, big["w_gu"][idx], big["w_down"][idx], fn)

    return (loss, dx.reshape(x.shape), *assemble(gs, 0), *assemble(ds, 1), *assemble(nms, 2), *assemble(nvs, 3))
```

```python
import functools

import jax
import jax.numpy as jnp
from jax import lax
from jax.experimental import pallas as pl
from jax.experimental.pallas import tpu as pltpu

F32 = jnp.float32
BF16 = jnp.bfloat16

D_MODEL = 1024
ATTN_WIDTH = 512
HEAD_DIM = 64
DILATED_PAIRS = ((128, 1), (512, 4), (2048, 16))
ATTN_BLOCK = 128
ROPE_THETA = 10000.0
HGRN_WIDTH = 512
HGRN_CHUNK = 16
HGRN_HEADS = 4
IN_PROJ_WIDTH = 3584
FFN_HIDDEN = 2816
NORM_EPS = 1e-6
ATTN_SCALE = HEAD_DIM ** -0.5
N_CHIPS = 4
N_DEV = 8

ADAM_LR = 0.001
ADAM_B1 = 0.9
ADAM_B2 = 0.999
ADAM_EPS = 1e-08
ADAM_WD = 0.01
ADAM_STEP = 10

LANES = 128
HGRN_ROWS = 128
HGRN_STEP_BLOCKS = 2
ROW_TILE = 256
WIDE_ROW_TILE = 512
ATTN_STEP_ROWS = 2048
ATTN_FWD_UNROLL = 16
ATTN_BWD_UNROLL = 16
ATTN_MAJOR_DILATION = 16
VMEM_LIMIT = 56 * 1024 * 1024
NEG_BIG = -1e30
MESH_ID = pl.DeviceIdType.MESH


def _cparams(*sem):
    return pltpu.CompilerParams(dimension_semantics=tuple(sem), vmem_limit_bytes=VMEM_LIMIT)


def _dot(a, b):
    return jnp.dot(a, b, preferred_element_type=F32)


def _dot_nt(a, b):
    return lax.dot_general(a, b, (((1,), (1,)), ((), ())), preferred_element_type=F32)


def _dot_tn(a, b):
    return lax.dot_general(a, b, (((0,), (0,)), ((), ())), preferred_element_type=F32)


def _sigmoid(x):
    return 1.0 / (1.0 + jnp.exp(-x))


def _full(shape):
    n = len(shape)
    return pl.BlockSpec(shape, lambda *_: (0,) * n)


def _weight(shape):
    n = len(shape)
    return pl.BlockSpec(shape, lambda *_: (0,) * n, pipeline_mode=pl.Buffered(1))


def _rows(tm, width):
    return pl.BlockSpec((tm, width), lambda i: (i, 0))


def _swap32(x):
    lane = lax.broadcasted_iota(jnp.int32, x.shape, 1)
    first = (lane % HEAD_DIM) < (HEAD_DIM // 2)
    return jnp.where(first, pltpu.roll(x, LANES - 32, axis=1), pltpu.roll(x, 32, axis=1))


def _rotary_fwd(x, cos, sin_signed):
    parts = []
    for j in range(x.shape[1] // LANES):
        xc = x[:, j * LANES:(j + 1) * LANES]
        parts.append(xc * cos + _swap32(xc) * sin_signed)
    return jnp.concatenate(parts, axis=1)


def _rotary_bwd(dy, cos, sin_signed):
    parts = []
    for j in range(dy.shape[1] // LANES):
        dc = dy[:, j * LANES:(j + 1) * LANES]
        parts.append(dc * cos + _swap32(dc * sin_signed))
    return jnp.concatenate(parts, axis=1)


def _rope_tables(seq):
    half = HEAD_DIM // 2
    inv_freq = ROPE_THETA ** (-jnp.arange(half, dtype=F32) / half)
    ang = jnp.arange(seq, dtype=F32)[:, None] * inv_freq[None, :]
    cos, sin = jnp.cos(ang), jnp.sin(ang)
    cos_t = jnp.tile(cos, (1, LANES // half))
    sin_t = jnp.tile(jnp.concatenate([-sin, sin], axis=1), (1, LANES // HEAD_DIM))
    return cos_t, sin_t


def cast_bf16(w, name):
    r, c = w.shape
    half = r // 2

    def body(w_ref, o_ref):
        o_ref[...] = w_ref[...].astype(BF16)

    return pl.pallas_call(
        body, name=name, grid=(2,),
        in_specs=[pl.BlockSpec((half, c), lambda i: (i, 0))],
        out_specs=pl.BlockSpec((None, half, c), lambda i: (i, 0, 0)),
        out_shape=jax.ShapeDtypeStruct((2, half, c), BF16),
        compiler_params=_cparams("parallel"),
    )(w)


def _mesh_pos():
    return lax.axis_index("x"), lax.axis_index("y"), lax.axis_index("c")


GATHER_COPIES = 7


def _gather_phases(x_refs, out_refs, send_sems, recv_sems, local_sems):
    n = len(x_refs)
    x, y, cc = _mesh_pos()
    me, sibling = (x, y, cc), (x, y, 1 - cc)
    chips = [(1 - x, y), (x, 1 - y), (1 - x, 1 - y)]

    def rows(i, px, py, pc):
        return out_refs[i].at[4 * px + 2 * py + pc]

    def copy(i, k, block, to, src=None):
        return pltpu.make_async_remote_copy(
            src_ref=rows(i, *block) if src is None else src, dst_ref=rows(i, *block),
            send_sem=send_sems.at[GATHER_COPIES * i + k], recv_sem=recv_sems.at[GATHER_COPIES * i + k],
            device_id=to, device_id_type=MESH_ID)

    def local(i):
        return pltpu.make_async_copy(x_refs[i].at[cc], rows(i, *me), local_sems.at[i])

    def first(i):
        mine = x_refs[i].at[cc]
        return [copy(i, 0, me, sibling, src=mine)] + [
            copy(i, 1 + j, me, (*chip, cc), src=mine) for j, chip in enumerate(chips)]

    def passed(i):
        return [copy(i, 4 + j, (*chip, cc), sibling) for j, chip in enumerate(chips)]

    def start():
        for i in range(n):
            local(i).start()
            for cp in first(i):
                cp.start()

    def forward():
        for i in range(n):
            onward = passed(i)
            for j, chip in enumerate(chips):
                copy(i, 1 + j, (*chip, cc), me).wait_recv()
                onward[j].start()

    def finish():
        for i in range(n):
            copy(i, 0, sibling, me).wait_recv()
            for j, chip in enumerate(chips):
                copy(i, 4 + j, (*chip, 1 - cc), me).wait_recv()
            for cp in first(i) + passed(i):
                cp.wait_send()
            local(i).wait()

    return start, forward, finish


def _gather_scratch(n):
    return [pltpu.SemaphoreType.DMA((GATHER_COPIES * n,)), pltpu.SemaphoreType.DMA((GATHER_COPIES * n,)),
            pltpu.SemaphoreType.DMA((n,))]


def _gathered_shape(halves):
    return jax.ShapeDtypeStruct((N_DEV,) + halves.shape[1:], halves.dtype)


def allgather_halves(halves, name):
    def body(x_ref, out_ref, send_sems, recv_sems, local_sems):
        start, forward, finish = _gather_phases([x_ref], [out_ref], send_sems, recv_sems, local_sems)
        start()
        forward()
        finish()

    return pl.pallas_call(
        body, name=name,
        in_specs=[pl.BlockSpec(memory_space=pl.ANY)],
        out_specs=pl.BlockSpec(memory_space=pl.ANY),
        out_shape=_gathered_shape(halves),
        scratch_shapes=_gather_scratch(1),
    )(halves)


def _rms(x):
    return lax.rsqrt(jnp.mean(x * x, axis=-1, keepdims=True) + NORM_EPS)


def in_proj(x, norm1_w, w_in4, cos_t, sin_t, weight_halves=()):
    seq = x.shape[0]
    tm = WIDE_ROW_TILE
    cw = w_in4.shape[2]
    n_w = len(weight_halves)
    steps = seq // tm
    major = ATTN_MAJOR_DILATION
    slabs = ATTN_WIDTH // LANES

    def body(*refs):
        x_ref, nw_ref, w_ref, cos_ref, sin_ref = refs[:5]
        q_ref, k_ref, v_ref, hg_ref, u_ref = refs[5 + n_w:10 + n_w]
        major_refs = refs[10 + n_w:13 + n_w]
        slab_scr = refs[13 + 2 * n_w]
        step = pl.program_id(0)
        if n_w:
            start, forward, finish = _gather_phases(
                refs[5:5 + n_w], refs[13 + n_w:13 + 2 * n_w], *refs[14 + 2 * n_w:])
            pl.when(step == 0)(start)
            pl.when(step == (3 * steps) // 4)(forward)
        xv = x_ref[...]
        u = ((xv * _rms(xv)) * nw_ref[...]).astype(BF16)
        u_ref[...] = u
        proj = jnp.concatenate([_dot(u, w_ref[j]) for j in range(N_CHIPS)], axis=1)
        cos, sin = cos_ref[...], sin_ref[...]
        a = ATTN_WIDTH
        qkv = (_rotary_fwd(proj[:, :a], cos, sin), _rotary_fwd(proj[:, a:2 * a], cos, sin), proj[:, 2 * a:3 * a])
        for ref, val in zip((q_ref, k_ref, v_ref), qkv):
            ref[...] = val
        hg_ref[...] = proj[:, 3 * a:]
        for idx, val in enumerate(qkv):
            for s in range(slabs):
                slab_scr[idx, s] = val[:, s * LANES:(s + 1) * LANES]
        for idx, out in enumerate(major_refs):
            for r in range(major):
                for s in range(slabs):
                    out[r, :, s * LANES:(s + 1) * LANES] = (
                        slab_scr.at[idx, s][pl.ds(r, tm // major, stride=major), :].astype(BF16))
        if n_w:
            pl.when(step == steps - 1)(finish)

    anywhere = pl.BlockSpec(memory_space=pl.ANY)
    major_spec = pl.BlockSpec((major, tm // major, ATTN_WIDTH), lambda i: (0, i, 0))
    return pl.pallas_call(
        body, name="in_proj", grid=(steps,),
        in_specs=[_rows(tm, D_MODEL), _full((1, D_MODEL)), _weight((N_CHIPS, D_MODEL, cw)),
                  _rows(tm, LANES), _rows(tm, LANES)] + [anywhere] * n_w,
        out_specs=[_rows(tm, ATTN_WIDTH)] * 3 + [_rows(tm, 4 * HGRN_WIDTH), _rows(tm, D_MODEL)]
        + [major_spec] * 3 + [anywhere] * n_w,
        out_shape=[jax.ShapeDtypeStruct((seq, ATTN_WIDTH), F32)] * 3
        + [jax.ShapeDtypeStruct((seq, 4 * HGRN_WIDTH), F32), jax.ShapeDtypeStruct((seq, D_MODEL), BF16)]
        + [jax.ShapeDtypeStruct((major, seq // major, ATTN_WIDTH), BF16)] * 3
        + [_gathered_shape(h) for h in weight_halves],
        scratch_shapes=[pltpu.VMEM((3, slabs, tm, LANES), F32)] + (_gather_scratch(n_w) if n_w else []),
        compiler_params=_cparams("arbitrary"),
    )(x, norm1_w, w_in4, cos_t, sin_t, *weight_halves)


def _head_masks():
    lane = lax.broadcasted_iota(jnp.int32, (1, LANES), 1)
    return [(lane // HEAD_DIM) == h for h in range(LANES // HEAD_DIM)]


def _window_valid(no_prev):
    qi = lax.broadcasted_iota(jnp.int32, (ATTN_BLOCK, 2 * ATTN_BLOCK), 0)
    kj = lax.broadcasted_iota(jnp.int32, (ATTN_BLOCK, 2 * ATTN_BLOCK), 1)
    valid = (kj >= qi) & (kj <= qi + ATTN_BLOCK)
    return valid & (jnp.logical_not(no_prev) | (kj >= ATTN_BLOCK))


def _strided_rows(start, dilation):
    if dilation == 1:
        return pl.ds(start, ATTN_BLOCK)
    return pl.ds(start, ATTN_BLOCK, stride=dilation)


def _block_before(edge_ref, cur_ref, t, r, span, dilation, per_step):
    edge = edge_ref[_strided_rows(ATTN_STEP_ROWS - span + r, dilation), :]
    if per_step == 1:
        return edge
    inside = cur_ref[_strided_rows(r + span * jnp.maximum(t - 1, 0), dilation), :]
    return jnp.where(t == 0, edge, inside)


def _attn_specs():
    cur = pl.BlockSpec((ATTN_STEP_ROWS, LANES), lambda hp, j: (j, hp))
    prev = pl.BlockSpec((ATTN_STEP_ROWS, LANES), lambda hp, j: (jnp.maximum(j - 1, 0), hp))
    return cur, prev


def _for_each_block(dilation, unroll, block):
    span = ATTN_BLOCK * dilation
    per_step = ATTN_STEP_ROWS // span

    def trip(it, carry):
        block(it // dilation, it % dilation, span, per_step)
        return carry

    lax.fori_loop(0, per_step * dilation, trip, 0, unroll=unroll)


def _load_qkv(natural, major, t, r, span, dilation, per_step):
    if dilation == ATTN_MAJOR_DILATION:
        q_ref, kc_ref, vc_ref, kp_ref, vp_ref = major
        return (q_ref[r] * ATTN_SCALE, jnp.concatenate([kp_ref[r], kc_ref[r]], axis=0),
                jnp.concatenate([vp_ref[r], vc_ref[r]], axis=0))
    q_ref, kc_ref, vc_ref, kp_ref, vp_ref = natural
    rows = _strided_rows(r + span * t, dilation)
    kp = _block_before(kp_ref, kc_ref, t, r, span, dilation, per_step)
    vp = _block_before(vp_ref, vc_ref, t, r, span, dilation, per_step)
    return ((q_ref[rows, :] * ATTN_SCALE).astype(BF16),
            jnp.concatenate([kp, kc_ref[rows, :]], axis=0).astype(BF16),
            jnp.concatenate([vp, vc_ref[rows, :]], axis=0).astype(BF16))


def _major_specs():
    assert ATTN_STEP_ROWS == ATTN_BLOCK * ATTN_MAJOR_DILATION
    shape = (ATTN_MAJOR_DILATION, ATTN_BLOCK, LANES)
    return (pl.BlockSpec(shape, lambda hp, j: (0, j, hp)),
            pl.BlockSpec(shape, lambda hp, j: (0, jnp.maximum(j - 1, 0), hp)))


def attn_fwd(q, k, v, q16, k16, v16):
    seq = q.shape[0]
    cur, prev = _attn_specs()
    cur16, prev16 = _major_specs()

    def body(*refs):
        natural, major, (y_ref, lse_ref) = refs[:5], refs[5:10], refs[10:]
        first_step = pl.program_id(1) == 0
        masks = _head_masks()
        for index, (_, dilation) in enumerate(DILATED_PAIRS):
            def block(t, r, span, per_step, dilation=dilation, merge=index > 0):
                rows = _strided_rows(r + span * t, dilation)
                q2, k2, v2 = _load_qkv(natural, major, t, r, span, dilation, per_step)
                valid = _window_valid(first_step & (t == 0))
                o_acc = jnp.zeros((ATTN_BLOCK, LANES), F32)
                l_acc = jnp.zeros((ATTN_BLOCK, LANES), F32)
                for mh in masks:
                    qm = jnp.where(mh, q2, jnp.zeros_like(q2))
                    s = jnp.where(valid, _dot_nt(qm, k2), NEG_BIG)
                    m = jnp.max(s, axis=-1, keepdims=True)
                    p = jnp.exp(s - m)
                    l = jnp.sum(p, axis=-1, keepdims=True)
                    o = _dot(p.astype(BF16), v2) / l
                    o_acc = jnp.where(mh, o, o_acc)
                    l_acc = jnp.where(mh, m + jnp.log(l), l_acc)
                if merge:
                    y_old, l_old = y_ref[rows, :], lse_ref[rows, :]
                    mx = jnp.maximum(l_old, l_acc)
                    e_old, e_new = jnp.exp(l_old - mx), jnp.exp(l_acc - mx)
                    den = e_old + e_new
                    o_acc = (y_old * e_old + o_acc * e_new) / den
                    l_acc = mx + jnp.log(den)
                y_ref[rows, :] = o_acc
                lse_ref[rows, :] = l_acc

            _for_each_block(dilation, ATTN_FWD_UNROLL, block)

    return pl.pallas_call(
        body, name="attn_fwd", grid=(ATTN_WIDTH // LANES, seq // ATTN_STEP_ROWS),
        in_specs=[cur, cur, cur, prev, prev, cur16, cur16, cur16, prev16, prev16],
        out_specs=[cur, cur],
        out_shape=[jax.ShapeDtypeStruct((seq, ATTN_WIDTH), F32)] * 2,
        compiler_params=_cparams("parallel", "parallel"),
    )(q, k, v, k, v, q16, k16, v16, k16, v16)


def _chunk_cumsum(x, reverse=False):
    rc = lax.broadcasted_iota(jnp.int32, x.shape, 0) % HGRN_CHUNK
    sh = 1
    while sh < HGRN_CHUNK:
        if reverse:
            x = x + jnp.where(rc + sh < HGRN_CHUNK, pltpu.roll(x, x.shape[0] - sh, axis=0), 0.0)
        else:
            x = x + jnp.where(rc >= sh, pltpu.roll(x, sh, axis=0), 0.0)
        sh *= 2
    return x


def _chunk_row(x, row):
    return _chunk_rows([x[n * HGRN_CHUNK + row:n * HGRN_CHUNK + row + 1, :]
                        for n in range(x.shape[0] // HGRN_CHUNK)])


def _chunk_rows(rows):
    return jnp.concatenate([jnp.broadcast_to(r, (HGRN_CHUNK, r.shape[1])) for r in rows], axis=0)


def _hgrn_prep(hg, lbl):
    w = HGRN_WIDTH
    a0, a1 = lbl[0:1, :], lbl[1:2, :]
    mx = jnp.maximum(a0, a1)
    e0, e1 = jnp.exp(a0 - mx), jnp.exp(a1 - mx)
    lb = e0 / (e0 + e1)
    qb, fb, gb = hg[:, :w], hg[:, w:2 * w], hg[:, 3 * w:]
    sg = _sigmoid(fb)
    f = lb + (1.0 - lb) * sg
    b = _chunk_cumsum(jnp.log(f))
    bmid, btot = _chunk_row(b, HGRN_CHUNK // 2 - 1), _chunk_row(b, HGRN_CHUNK - 1)
    sq = _sigmoid(qb)
    p = dict(lb=lb, sg=sg, f=f, kk=1.0 - f, sq=sq, qf=qb * sq, gb=gb,
             e_iq=jnp.exp(b - bmid), e_ik=jnp.exp(bmid - b), e_b=jnp.exp(b),
             e_bb=jnp.exp(btot - b), e_tot=jnp.exp(btot))
    p["qi"] = p["qf"] * p["e_iq"]
    p["ki"] = p["kk"] * p["e_ik"]
    p["qs"] = p["qf"] * p["e_b"]
    p["kb"] = p["kk"] * p["e_bb"]
    return p


def _chunk_masks():
    t = lax.broadcasted_iota(jnp.int32, (HGRN_ROWS, HGRN_ROWS), 0)
    s = lax.broadcasted_iota(jnp.int32, (HGRN_ROWS, HGRN_ROWS), 1)
    tril = ((t // HGRN_CHUNK) == (s // HGRN_CHUNK)) & (s <= t)
    n_chunks = HGRN_ROWS // HGRN_CHUNK
    tt = lax.broadcasted_iota(jnp.int32, (HGRN_ROWS, n_chunks * LANES), 0)
    cc = lax.broadcasted_iota(jnp.int32, (HGRN_ROWS, n_chunks * LANES), 1)
    block = (tt // HGRN_CHUNK) == (cc // LANES)
    return tril, block


def _spread(x, block):
    n_chunks = HGRN_ROWS // HGRN_CHUNK
    return jnp.where(block, jnp.tile(x, (1, n_chunks)), jnp.zeros((), x.dtype))


def _fold(x_full, block):
    n_chunks = HGRN_ROWS // HGRN_CHUNK
    z = jnp.where(block, x_full, 0.0)
    acc = z[:, :LANES]
    for n in range(1, n_chunks):
        acc = acc + z[:, n * LANES:(n + 1) * LANES]
    return acc


def hgrn_fwd(hg, lb_logits, hnw, weight_halves=()):
    seq = hg.shape[0]
    nblk = seq // HGRN_ROWS
    n_steps = nblk // HGRN_STEP_BLOCKS
    step_rows = HGRN_ROWS * HGRN_STEP_BLOCKS
    n_chunks = HGRN_ROWS // HGRN_CHUNK
    n_w = len(weight_halves)

    def body(*refs):
        hg_ref, lbl_ref, hnw_ref = refs[:3]
        w_refs = refs[3:3 + n_w]
        yb_ref, o_ref, st0_ref = refs[3 + n_w:6 + n_w]
        g_refs = refs[6 + n_w:6 + 2 * n_w]
        st_scr = refs[6 + 2 * n_w]
        step = pl.program_id(0)
        if n_w:
            start, forward, finish = _gather_phases(w_refs, g_refs, *refs[7 + 2 * n_w:])
            pl.when(step == 0)(start)
            pl.when(step == (3 * n_steps) // 4)(forward)

        @pl.when(step == 0)
        def _():
            st_scr[...] = jnp.zeros_like(st_scr)

        tril, block = _chunk_masks()
        for sub in range(HGRN_STEP_BLOCKS):
            rows = slice(sub * HGRN_ROWS, (sub + 1) * HGRN_ROWS)
            hg_v = hg_ref[rows, :]
            p = _hgrn_prep(hg_v, lbl_ref[...])
            vv = hg_v[:, 2 * HGRN_WIDTH:3 * HGRN_WIDTH].astype(BF16)
            outs = []
            for h in range(HGRN_HEADS):
                sl = slice(h * LANES, (h + 1) * LANES)
                v_h = vv[:, sl]
                a = jnp.where(tril, _dot_nt(p["qi"][:, sl].astype(BF16), p["ki"][:, sl].astype(BF16)), 0.0)
                o = _dot(a.astype(BF16), v_h)
                upd = _dot_tn(v_h, _spread(p["kb"][:, sl].astype(BF16), block))
                st = st_scr[h]
                st0_ref[sub, h] = st
                parts = []
                for n in range(n_chunks):
                    parts.append(st.astype(BF16))
                    decay = p["e_tot"][n * HGRN_CHUNK:n * HGRN_CHUNK + 1, sl]
                    st = st * decay + upd[:, n * LANES:(n + 1) * LANES]
                st_scr[h] = st
                o = o + _dot_nt(_spread(p["qs"][:, sl].astype(BF16), block), jnp.concatenate(parts, axis=1))
                outs.append(o)
            o_ref[rows, :] = jnp.concatenate(outs, axis=1)
            normed = jnp.concatenate(
                [outs[h] * _rms(outs[h]) for h in range(HGRN_HEADS)], axis=1)
            gb = p["gb"]
            yb_ref[rows, :] = (normed * hnw_ref[...]) * (gb * _sigmoid(gb))
        if n_w:
            pl.when(step == n_steps - 1)(finish)

    anywhere = pl.BlockSpec(memory_space=pl.ANY)
    return pl.pallas_call(
        body, name="hgrn_fwd", grid=(n_steps,),
        in_specs=[_rows(step_rows, 4 * HGRN_WIDTH), _full((2, HGRN_WIDTH)), _full((1, HGRN_WIDTH))]
        + [anywhere] * n_w,
        out_specs=[_rows(step_rows, HGRN_WIDTH), _rows(step_rows, HGRN_WIDTH),
                   pl.BlockSpec((HGRN_STEP_BLOCKS, HGRN_HEADS, LANES, LANES), lambda i: (i, 0, 0, 0))]
        + [anywhere] * n_w,
        out_shape=[jax.ShapeDtypeStruct((seq, HGRN_WIDTH), F32)] * 2
        + [jax.ShapeDtypeStruct((nblk, HGRN_HEADS, LANES, LANES), F32)]
        + [_gathered_shape(h) for h in weight_halves],
        scratch_shapes=[pltpu.VMEM((HGRN_HEADS, LANES, LANES), F32)] + (_gather_scratch(n_w) if n_w else []),
        compiler_params=_cparams("arbitrary"),
    )(hg, lb_logits, hnw, *weight_halves)


def ffn_fwd(ya, yb, x, w_out, norm2_w, w_gu4, w_down, final_w, target):
    seq = x.shape[0]
    tm = ROW_TILE
    cw = w_gu4.shape[2]
    inv_d = 1.0 / D_MODEL

    def body(ya_ref, yb_ref, x_ref, wo_ref, nw_ref, wgu_ref, wd_ref, fw_ref, t_ref,
             mixed_ref, h1_ref, u2_ref, g_ref, up_ref, act_ref, dh2_ref, acc_ref):
        @pl.when(pl.program_id(0) == 0)
        def _():
            acc_ref[...] = jnp.zeros_like(acc_ref)

        mixed = jnp.concatenate([ya_ref[...], yb_ref[...]], axis=1).astype(BF16)
        mixed_ref[...] = mixed
        h1 = x_ref[...] + _dot(mixed, wo_ref[...])
        h1_ref[...] = h1
        u2 = ((h1 * _rms(h1)) * nw_ref[...]).astype(BF16)
        u2_ref[...] = u2
        g = jnp.concatenate([_dot(u2, wgu_ref[0]), _dot(u2, wgu_ref[1])], axis=1)
        up = jnp.concatenate([_dot(u2, wgu_ref[2]), _dot(u2, wgu_ref[3])], axis=1)
        g_ref[...] = g.astype(BF16)
        up_ref[...] = up.astype(BF16)
        act = ((g * _sigmoid(g)) * up).astype(BF16)
        act_ref[...] = act
        h2 = h1 + _dot(act, wd_ref[...])
        rf = _rms(h2)
        n = h2 * rf
        fw = fw_ref[...]
        err = n * fw - t_ref[...]
        dy = err * inv_d
        acc_ref[0:1, :] += jnp.sum(dy * n, axis=0, keepdims=True)
        acc_ref[1:2, :] += (0.5 * inv_d) * jnp.sum(err * err, axis=0, keepdims=True)
        dn = dy * fw
        dh2_ref[...] = rf * (dn - n * jnp.mean(dn * n, axis=-1, keepdims=True))

    half = _rows(tm, ATTN_WIDTH)
    wide = _rows(tm, D_MODEL)
    ffn = _rows(tm, FFN_HIDDEN)
    return pl.pallas_call(
        body, name="ffn_fwd", grid=(seq // tm,),
        in_specs=[half, half, wide, _weight((D_MODEL, D_MODEL)), _full((1, D_MODEL)),
                  _weight((N_CHIPS, D_MODEL, cw)), _weight((FFN_HIDDEN, D_MODEL)), _full((1, D_MODEL)), wide],
        out_specs=[wide, wide, wide, ffn, ffn, ffn, wide, _full((8, D_MODEL))],
        out_shape=[jax.ShapeDtypeStruct((seq, D_MODEL), BF16), jax.ShapeDtypeStruct((seq, D_MODEL), F32),
                   jax.ShapeDtypeStruct((seq, D_MODEL), BF16)]
        + [jax.ShapeDtypeStruct((seq, FFN_HIDDEN), BF16)] * 3
        + [jax.ShapeDtypeStruct((seq, D_MODEL), F32), jax.ShapeDtypeStruct((8, D_MODEL), F32)],
        compiler_params=_cparams("arbitrary"),
    )(ya, yb, x, w_out, norm2_w, w_gu4, w_down, final_w, target)


def _head_sum_matrix():
    i = jnp.arange(ATTN_WIDTH)
    return ((i[:, None] // HEAD_DIM) == (i[None, :] // HEAD_DIM)).astype(BF16)


def ffn_bwd(dh2, w_down, g, up, w_gu4, h1, norm2_w, w_out, ya):
    seq = h1.shape[0]
    tm = ROW_TILE
    cw = w_gu4.shape[2]
    hsum = _head_sum_matrix()

    def body(dh2_ref, wd_ref, g_ref, up_ref, w_ref, h1_ref, nw_ref, wo_ref, ya_ref, hs_ref,
             dgu_ref, dh1_ref, dya_ref, dyb_ref, delta_ref, acc_ref):
        @pl.when(pl.program_id(0) == 0)
        def _():
            acc_ref[...] = jnp.zeros_like(acc_ref)

        dh2_b = dh2_ref[...].astype(BF16)
        du2 = jnp.zeros((tm, D_MODEL), F32)
        for j in range(N_CHIPS // 2):
            dact = _dot_nt(dh2_b, wd_ref[j * cw:(j + 1) * cw, :])
            gv = g_ref[:, j * cw:(j + 1) * cw].astype(F32)
            sg = _sigmoid(gv)
            dg = (dact * up_ref[:, j * cw:(j + 1) * cw].astype(F32) * (sg * (1.0 + gv * (1.0 - sg)))).astype(BF16)
            dup = (dact * (gv * sg)).astype(BF16)
            dgu_ref[:, j * cw:(j + 1) * cw] = dg
            dgu_ref[:, FFN_HIDDEN + j * cw:FFN_HIDDEN + (j + 1) * cw] = dup
            du2 = du2 + _dot_nt(dg, w_ref[j]) + _dot_nt(dup, w_ref[N_CHIPS // 2 + j])
        h1 = h1_ref[...]
        r2 = _rms(h1)
        nh = h1 * r2
        acc_ref[0:1, :] += jnp.sum(du2 * nh, axis=0, keepdims=True)
        dn = du2 * nw_ref[...]
        dh1 = dh2_ref[...] + r2 * (dn - nh * jnp.mean(dn * nh, axis=-1, keepdims=True))
        dh1_ref[...] = dh1
        dmixed = _dot_nt(dh1.astype(BF16), wo_ref[...])
        dya = dmixed[:, :ATTN_WIDTH]
        dya_ref[...] = dya
        dyb_ref[...] = dmixed[:, ATTN_WIDTH:]
        prod = dya * ya_ref[...]
        hi = prod.astype(BF16)
        lo = (prod - hi.astype(F32)).astype(BF16)
        delta_ref[...] = _dot(hi, hs_ref[...]) + _dot(lo, hs_ref[...])

    wide = _rows(tm, D_MODEL)
    half = _rows(tm, ATTN_WIDTH)
    ffn = _rows(tm, FFN_HIDDEN)
    return pl.pallas_call(
        body, name="ffn_bwd", grid=(seq // tm,),
        in_specs=[wide, _weight((FFN_HIDDEN, D_MODEL)), ffn, ffn, _weight((N_CHIPS, D_MODEL, cw)), wide,
                  _full((1, D_MODEL)), _weight((D_MODEL, D_MODEL)), half, _full((ATTN_WIDTH, ATTN_WIDTH))],
        out_specs=[_rows(tm, 2 * FFN_HIDDEN), wide, half, half, half, _full((8, D_MODEL))],
        out_shape=[jax.ShapeDtypeStruct((seq, 2 * FFN_HIDDEN), BF16), jax.ShapeDtypeStruct((seq, D_MODEL), F32)]
        + [jax.ShapeDtypeStruct((seq, ATTN_WIDTH), F32)] * 3 + [jax.ShapeDtypeStruct((8, D_MODEL), F32)],
        compiler_params=_cparams("arbitrary"),
    )(dh2, w_down, g, up, w_gu4, h1, norm2_w, w_out, ya, hsum)


def attn_bwd(q, k, v, q16, k16, v16, dy, lse, delta, sibling_grads=()):
    seq = q.shape[0]
    cur, prev = _attn_specs()
    cur16, prev16 = _major_specs()
    whole = pl.BlockSpec((seq, LANES), lambda hp, j: (0, hp))
    n_g = len(sibling_grads)
    n_hp, n_steps = ATTN_WIDTH // LANES, seq // ATTN_STEP_ROWS
    n_in = 13

    def body(*refs):
        natural, major = refs[:5], refs[5:10]
        dy_ref, lse_ref, dl_ref = refs[10:n_in]
        dq_ref, dk_ref, dv_ref = refs[n_in + n_g:n_in + 3 + n_g]
        first_step = pl.program_id(1) == 0
        base = pl.program_id(1) * ATTN_STEP_ROWS
        masks = _head_masks()
        if n_g:
            start, finish = _sibling_exchange_phases(
                refs[n_in:n_in + n_g], refs[n_in + 3 + n_g:n_in + 3 + 2 * n_g], *refs[n_in + 3 + 2 * n_g:])
            pl.when((pl.program_id(0) == 0) & first_step)(start)

        def block(t, r, span, per_step, dilation, add):
            rows = _strided_rows(r + span * t, dilation)
            at_edge = t == 0
            q2, k2, v2 = _load_qkv(natural, major, t, r, span, dilation, per_step)
            dy2 = dy_ref[rows, :].astype(BF16)
            lse2, dl2 = lse_ref[rows, :], dl_ref[rows, :]
            valid = _window_valid(first_step & at_edge)
            zero = jnp.zeros_like(q2)
            qms, dyms, ps, dss, kms = [], [], [], [], []
            for h, mh in enumerate(masks):
                c0 = h * HEAD_DIM
                qm, dym = jnp.where(mh, q2, zero), jnp.where(mh, dy2, zero)
                s = _dot_nt(qm, k2)
                p = jnp.where(valid, jnp.exp(s - lse2[:, c0:c0 + 1]), 0.0)
                dp = _dot_nt(dym, v2)
                dss.append((p * (dp - dl2[:, c0:c0 + 1])).astype(BF16))
                ps.append(p.astype(BF16))
                qms.append(qm)
                dyms.append(dym)
                kms.append(jnp.where(mh, k2, jnp.zeros_like(k2)))
            dq = _dot(jnp.concatenate(dss, axis=1), jnp.concatenate(kms, axis=0)) * ATTN_SCALE
            p_all, ds_all = jnp.concatenate(ps, axis=0), jnp.concatenate(dss, axis=0)
            dy_all, q_all = jnp.concatenate(dyms, axis=0), jnp.concatenate(qms, axis=0)
            dv_full, dk_full = _dot_tn(dy_all, p_all).T, _dot_tn(q_all, ds_all).T
            here = _strided_rows(base + r + span * t, dilation)
            if add:
                dq_ref[rows, :] += dq
                dk_ref[here, :] += dk_full[ATTN_BLOCK:]
                dv_ref[here, :] += dv_full[ATTN_BLOCK:]
            else:
                dq_ref[rows, :] = dq
                dk_ref[here, :] = dk_full[ATTN_BLOCK:]
                dv_ref[here, :] = dv_full[ATTN_BLOCK:]
            back = _strided_rows(jnp.maximum(base + r + span * t - span, r), dilation)
            dk_ref[back, :] += dk_full[:ATTN_BLOCK]
            dv_ref[back, :] += dv_full[:ATTN_BLOCK]

        for index, (_, dilation) in enumerate(DILATED_PAIRS):
            _for_each_block(dilation, ATTN_BWD_UNROLL,
                            functools.partial(block, dilation=dilation, add=index > 0))
        if n_g:
            pl.when((pl.program_id(0) == n_hp - 1) & (pl.program_id(1) == n_steps - 1))(finish)

    anywhere = pl.BlockSpec(memory_space=pl.ANY)
    return pl.pallas_call(
        body, name="attn_bwd", grid=(n_hp, n_steps),
        in_specs=[cur, cur, cur, prev, prev, cur16, cur16, cur16, prev16, prev16, cur, cur, cur]
        + [anywhere] * n_g,
        out_specs=[cur, whole, whole] + [anywhere] * n_g,
        out_shape=[jax.ShapeDtypeStruct((seq, ATTN_WIDTH), F32)] * 3 + _sibling_exchange_shapes(sibling_grads),
        scratch_shapes=_sibling_exchange_scratch(n_g) if n_g else [],
        compiler_params=_cparams("arbitrary", "arbitrary"),
    )(q, k, v, k, v, q16, k16, v16, k16, v16, dy, lse, delta, *sibling_grads)


def hgrn_bwd(hg, lb_logits, hnw, o_pre, st0, dyb, chip_sums=()):
    seq = hg.shape[0]
    step_rows = HGRN_ROWS * HGRN_STEP_BLOCKS
    n_steps = seq // step_rows
    n_chunks = HGRN_ROWS // HGRN_CHUNK
    w = HGRN_WIDTH
    n_s = len(chip_sums)

    def body(*refs):
        hg_ref, lbl_ref, hnw_ref, o_ref, st0_ref, dyb_ref = refs[:6]
        dhg_ref, acc_ref = refs[6 + n_s:8 + n_s]
        dst_scr = refs[8 + 2 * n_s]
        step = pl.program_id(0)
        if n_s:
            start, finish = _chip_exchange_phases(refs[6:6 + n_s], refs[8 + n_s:8 + 2 * n_s], *refs[9 + 2 * n_s:])
            pl.when(step == 0)(start)

        @pl.when(step == 0)
        def _():
            dst_scr[...] = jnp.zeros_like(dst_scr)
            acc_ref[...] = jnp.zeros_like(acc_ref)

        tril, block = _chunk_masks()
        for sub in reversed(range(HGRN_STEP_BLOCKS)):
            rows = slice(sub * HGRN_ROWS, (sub + 1) * HGRN_ROWS)
            hg_v = hg_ref[rows, :]
            p = _hgrn_prep(hg_v, lbl_ref[...])
            vv = hg_v[:, 2 * w:3 * w].astype(BF16)
            hnw_v = hnw_ref[...]
            gb = p["gb"]
            sgg = _sigmoid(gb)
            silu_g = gb * sgg
            dyb_v = dyb_ref[rows, :]
            o_v = o_ref[rows, :]

            d_on = dyb_v * hnw_v * silu_g
            on_parts, do_parts = [], []
            for h in range(HGRN_HEADS):
                sl = slice(h * LANES, (h + 1) * LANES)
                rs = _rms(o_v[:, sl])
                on = o_v[:, sl] * rs
                on_parts.append(on)
                do_parts.append(rs * (d_on[:, sl] - on * jnp.mean(d_on[:, sl] * on, axis=-1, keepdims=True)))
            on_all = jnp.concatenate(on_parts, axis=1)
            dgb = dyb_v * on_all * hnw_v * (sgg * (1.0 + gb * (1.0 - sgg)))
            acc_ref[0:1, :] += jnp.sum(dyb_v * on_all * silu_g, axis=0, keepdims=True)

            dqf_parts, dkk_parts, db_parts, dv_parts, dbt_parts, dkbkb_parts = [], [], [], [], [], []
            for h in range(HGRN_HEADS):
                sl = slice(h * LANES, (h + 1) * LANES)
                v_h = vv[:, sl]
                do_h = do_parts[h].astype(BF16)
                qi, ki, qs, kb = p["qi"][:, sl], p["ki"][:, sl], p["qs"][:, sl], p["kb"][:, sl]
                qi_b, ki_b = qi.astype(BF16), ki.astype(BF16)
                kb_cat = _spread(kb.astype(BF16), block)
                qs_cat = _spread(qs.astype(BF16), block)
                upd = _dot_tn(v_h, kb_cat)
                st = st0_ref[sub, h]
                st_parts = []
                for n in range(n_chunks):
                    st_parts.append(st)
                    decay = p["e_tot"][n * HGRN_CHUNK:n * HGRN_CHUNK + 1, sl]
                    st = st * decay + upd[:, n * LANES:(n + 1) * LANES]
                st_cat = jnp.concatenate([s_.astype(BF16) for s_ in st_parts], axis=1)
                wgt = _dot_tn(do_h, qs_cat)
                dst = dst_scr[h]
                dst_parts = [None] * n_chunks
                dbt_rows = [None] * n_chunks
                for n in reversed(range(n_chunks)):
                    dst_parts[n] = dst.astype(BF16)
                    decay = p["e_tot"][n * HGRN_CHUNK:n * HGRN_CHUNK + 1, sl]
                    dbt_rows[n] = jnp.sum(dst * st_parts[n], axis=0, keepdims=True) * decay
                    dst = dst * decay + wgt[:, n * LANES:(n + 1) * LANES]
                dst_scr[h] = dst
                dst_cat = jnp.concatenate(dst_parts, axis=1)
                dqs = _fold(_dot(do_h, st_cat), block)
                dkb = _fold(_dot(v_h, dst_cat), block)
                dv_state = _dot_nt(kb_cat, dst_cat)
                a = jnp.where(tril, _dot_nt(qi_b, ki_b), 0.0).astype(BF16)
                da = jnp.where(tril, _dot_nt(do_h, v_h), 0.0).astype(BF16)
                dv_parts.append(_dot_tn(a, do_h) + dv_state)
                dqi = _dot(da, ki_b)
                dki = _dot_tn(da, qi_b)
                dqf_parts.append(dqi * p["e_iq"][:, sl] + dqs * p["e_b"][:, sl])
                dkk_parts.append(dki * p["e_ik"][:, sl] + dkb * p["e_bb"][:, sl])
                dkbkb = dkb * kb
                db_parts.append(dqi * qi - dki * ki + dqs * qs - dkbkb)
                dkbkb_parts.append(dkbkb)
                dbt_parts.append(_chunk_rows(dbt_rows))

            cat = lambda parts: jnp.concatenate(parts, axis=1)
            dlogf = (_chunk_cumsum(cat(db_parts), reverse=True)
                     + _chunk_row(_chunk_cumsum(cat(dkbkb_parts)), HGRN_CHUNK - 1) + cat(dbt_parts))
            sq, qb = p["sq"], hg_v[:, :w]
            dqb = cat(dqf_parts) * (sq * (1.0 + qb * (1.0 - sq)))
            df = dlogf / p["f"] - cat(dkk_parts)
            sg, lb = p["sg"], p["lb"]
            dfb = df * (1.0 - lb) * sg * (1.0 - sg)
            acc_ref[1:2, :] += jnp.sum(df * (1.0 - sg), axis=0, keepdims=True)
            dhg_ref[rows, :] = jnp.concatenate([dqb, dfb, cat(dv_parts), dgb], axis=1).astype(BF16)
        if n_s:
            pl.when(step == n_steps - 1)(finish)

    rev = lambda i: (n_steps - 1 - i, 0)
    anywhere = pl.BlockSpec(memory_space=pl.ANY)
    return pl.pallas_call(
        body, name="hgrn_bwd", grid=(n_steps,),
        in_specs=[pl.BlockSpec((step_rows, 4 * w), rev), _full((2, w)), _full((1, w)),
                  pl.BlockSpec((step_rows, w), rev),
                  pl.BlockSpec((HGRN_STEP_BLOCKS, HGRN_HEADS, LANES, LANES), lambda i: (n_steps - 1 - i, 0, 0, 0)),
                  pl.BlockSpec((step_rows, w), rev)] + [anywhere] * n_s,
        out_specs=[pl.BlockSpec((step_rows, 4 * w), rev), _full((8, w))] + [anywhere] * n_s,
        out_shape=[jax.ShapeDtypeStruct((seq, 4 * w), BF16), jax.ShapeDtypeStruct((8, w), F32)]
        + [jax.ShapeDtypeStruct(s.shape, s.dtype) for s in chip_sums],
        scratch_shapes=[pltpu.VMEM((HGRN_HEADS, LANES, LANES), F32)] + (_chip_exchange_scratch(n_s) if n_s else []),
        compiler_params=_cparams("arbitrary"),
    )(hg, lb_logits, hnw, o_pre, st0, dyb, *chip_sums)


def in_bwd(dq, dk, dv, dhg, cos_t, sin_t, w_in4, x, norm1_w, dh1):
    seq = x.shape[0]
    tm = WIDE_ROW_TILE
    cw = w_in4.shape[2]

    def body(dq_ref, dk_ref, dv_ref, dhg_ref, cos_ref, sin_ref, w_ref,
             x_ref, nw_ref, dh1_ref, dproj_ref, dx_ref, acc_ref):
        @pl.when(pl.program_id(0) == 0)
        def _():
            acc_ref[...] = jnp.zeros_like(acc_ref)

        cos, sin = cos_ref[...], sin_ref[...]
        dqa = _rotary_bwd(dq_ref[...], cos, sin)
        dka = _rotary_bwd(dk_ref[...], cos, sin)
        dproj = jnp.concatenate(
            [jnp.concatenate([dqa, dka, dv_ref[...]], axis=1).astype(BF16), dhg_ref[...]], axis=1)
        dproj_ref[...] = dproj
        du = _dot_nt(dproj[:, :cw], w_ref[0])
        for j in range(1, N_CHIPS):
            du = du + _dot_nt(dproj[:, j * cw:(j + 1) * cw], w_ref[j])
        xv = x_ref[...]
        r1 = _rms(xv)
        nx = xv * r1
        acc_ref[0:1, :] += jnp.sum(du * nx, axis=0, keepdims=True)
        dn = du * nw_ref[...]
        dx_ref[...] = dh1_ref[...] + r1 * (dn - nx * jnp.mean(dn * nx, axis=-1, keepdims=True))

    half = _rows(tm, ATTN_WIDTH)
    wide = _rows(tm, D_MODEL)
    return pl.pallas_call(
        body, name="in_bwd", grid=(seq // tm,),
        in_specs=[half] * 3 + [_rows(tm, 4 * HGRN_WIDTH), _rows(tm, LANES), _rows(tm, LANES),
                               _weight((N_CHIPS, D_MODEL, cw)), wide, _full((1, D_MODEL)), wide],
        out_specs=[_rows(tm, IN_PROJ_WIDTH), wide, _full((8, D_MODEL))],
        out_shape=[jax.ShapeDtypeStruct((seq, IN_PROJ_WIDTH), BF16), jax.ShapeDtypeStruct((seq, D_MODEL), F32),
                   jax.ShapeDtypeStruct((8, D_MODEL), F32)],
        compiler_params=_cparams("arbitrary"),
    )(dq, dk, dv, dhg, cos_t, sin_t, w_in4, x, norm1_w, dh1)


def weight_grad(a, b, col_block, name, group=1, small_pack=None):
    seq, kdim = a.shape
    ndim = b.shape[1]
    nj = ndim // col_block
    tk = min(1024, seq)
    hosting = small_pack is not None
    n_j, n_t = nj // group, seq // tk

    def body(*refs):
        a_ref, b_ref = refs[:2]
        o_ref = refs[3] if hosting else refs[2]
        if hosting:
            start, finish = _pack_gather_phases(refs[2], refs[4], *refs[5:])
            pl.when((pl.program_id(0) == 0) & (pl.program_id(1) == 0))(start)

        @pl.when(pl.program_id(1) == 0)
        def _():
            o_ref[...] = jnp.zeros_like(o_ref)

        acc = _dot_tn(a_ref[...].astype(BF16), b_ref[...].astype(BF16))
        for i in range(group):
            o_ref[i] += acc[:, i * col_block:(i + 1) * col_block]
        if hosting:
            pl.when((pl.program_id(0) == n_j - 1) & (pl.program_id(1) == n_t - 1))(finish)

    anywhere = pl.BlockSpec(memory_space=pl.ANY)
    out = pl.pallas_call(
        body, name=name, grid=(n_j, n_t),
        in_specs=[pl.BlockSpec((tk, kdim), lambda j, t: (t, 0)),
                  pl.BlockSpec((tk, group * col_block), lambda j, t: (t, j))] + [anywhere] * hosting,
        out_specs=[pl.BlockSpec((group, kdim, col_block), lambda j, t: (j, 0, 0))] + [anywhere] * hosting,
        out_shape=[jax.ShapeDtypeStruct((nj, kdim, col_block), F32)]
        + ([jax.ShapeDtypeStruct((N_DEV,) + small_pack.shape, F32)] if hosting else []),
        scratch_shapes=[pltpu.SemaphoreType.DMA((N_DEV - 1,)), pltpu.SemaphoreType.DMA((N_DEV - 1,)),
                        pltpu.SemaphoreType.DMA] if hosting else [],
        compiler_params=_cparams("arbitrary", "arbitrary"),
    )(a, b, *([small_pack] if hosting else []))
    return out if hosting else out[0]


def _sibling_exchange_phases(g_refs, out_refs, send_sems, recv_sems):
    x, y, cc = _mesh_pos()

    def copies():
        return [pltpu.make_async_remote_copy(
            src_ref=g_refs[i].at[j, 1 - cc], dst_ref=out_refs[i].at[j],
            send_sem=send_sems.at[i * N_CHIPS + j], recv_sem=recv_sems.at[i * N_CHIPS + j],
            device_id=(x, y, 1 - cc), device_id_type=MESH_ID)
            for i in range(len(g_refs)) for j in range(N_CHIPS)]

    def start():
        for cp in copies():
            cp.start()

    def finish():
        for cp in copies():
            cp.wait_recv()
        for cp in copies():
            cp.wait_send()

    return start, finish


def _sibling_exchange_scratch(n):
    return [pltpu.SemaphoreType.DMA((n * N_CHIPS,)), pltpu.SemaphoreType.DMA((n * N_CHIPS,))]


def _sibling_exchange_shapes(grads):
    return [jax.ShapeDtypeStruct((N_CHIPS,) + g.shape[2:], g.dtype) for g in grads]


def exchange_with_sibling(grads, name):
    n = len(grads)

    def body(*refs):
        start, finish = _sibling_exchange_phases(refs[:n], refs[n:2 * n], refs[2 * n], refs[2 * n + 1])
        start()
        finish()

    return pl.pallas_call(
        body, name=name,
        in_specs=[pl.BlockSpec(memory_space=pl.ANY)] * n,
        out_specs=[pl.BlockSpec(memory_space=pl.ANY)] * n,
        out_shape=_sibling_exchange_shapes(grads),
        scratch_shapes=_sibling_exchange_scratch(n),
    )(*grads)


def add_own_half(grad, recv, name):
    _, _, r, c = grad.shape
    tr = r // 2 if r % 32 == 0 else r

    def body(cc_ref, g_ref, r_ref, o_ref):
        o_ref[...] = (g_ref[...] + r_ref[...]).astype(BF16)

    grid_spec = pltpu.PrefetchScalarGridSpec(
        num_scalar_prefetch=1, grid=(N_CHIPS, r // tr),
        in_specs=[pl.BlockSpec((None, None, tr, c), lambda j, t, cc: (j, cc[0], t, 0)),
                  pl.BlockSpec((None, tr, c), lambda j, t, cc: (j, t, 0))],
        out_specs=pl.BlockSpec((None, tr, c), lambda j, t, cc: (j, t, 0)))
    cc = lax.axis_index("c").astype(jnp.int32).reshape(1)
    return pl.pallas_call(
        body, name=name, grid_spec=grid_spec,
        out_shape=jax.ShapeDtypeStruct((N_CHIPS, r, c), BF16),
        compiler_params=_cparams("parallel", "parallel"),
    )(cc, grad, recv)


def _chip_exchange_phases(s_refs, out_refs, send_sems, recv_sems):
    n = len(s_refs)
    x, y, cc = _mesh_pos()
    my_chip = 2 * x + y
    chips = [(1 - x, y), (x, 1 - y), (1 - x, 1 - y)]

    def outgoing():
        return [pltpu.make_async_remote_copy(
            src_ref=s_refs[i].at[2 * px + py], dst_ref=out_refs[i].at[my_chip],
            send_sem=send_sems.at[3 * i + j], recv_sem=recv_sems.at[3 * i + j],
            device_id=(px, py, cc), device_id_type=MESH_ID)
            for i in range(n) for j, (px, py) in enumerate(chips)]

    def start():
        for cp in outgoing():
            cp.start()

    def finish():
        for i in range(n):
            for j, (px, py) in enumerate(chips):
                pltpu.make_async_remote_copy(
                    src_ref=s_refs[i].at[my_chip], dst_ref=out_refs[i].at[2 * px + py],
                    send_sem=send_sems.at[3 * i + j], recv_sem=recv_sems.at[3 * i + j],
                    device_id=(px, py, cc), device_id_type=MESH_ID).wait_recv()
        for cp in outgoing():
            cp.wait_send()

    return start, finish


def _chip_exchange_scratch(n):
    return [pltpu.SemaphoreType.DMA((3 * n,)), pltpu.SemaphoreType.DMA((3 * n,))]


def sum_chips(sums, parts, name):
    _, r, c = parts.shape
    tr = r // 2 if r % 32 == 0 else r

    def body(idx_ref, s_ref, p1_ref, p2_ref, p3_ref, o_ref):
        o_ref[...] = ((s_ref[...].astype(F32) + p1_ref[...].astype(F32))
                      + p2_ref[...].astype(F32)) + p3_ref[...].astype(F32)

    def pick(k):
        return pl.BlockSpec((None, tr, c), lambda t, idx: (idx[k], t, 0))

    x, y = lax.axis_index("x"), lax.axis_index("y")
    idx = jnp.stack([2 * x + y, 2 * (1 - x) + y, 2 * x + (1 - y), 2 * (1 - x) + (1 - y)]).astype(jnp.int32)
    grid_spec = pltpu.PrefetchScalarGridSpec(
        num_scalar_prefetch=1, grid=(r // tr,),
        in_specs=[pick(0), pick(1), pick(2), pick(3)],
        out_specs=pl.BlockSpec((tr, c), lambda t, idx: (t, 0)))
    return pl.pallas_call(
        body, name=name, grid_spec=grid_spec,
        out_shape=jax.ShapeDtypeStruct((r, c), F32),
        compiler_params=_cparams("parallel"),
    )(idx, sums, parts, parts, parts)


def share_with_sibling(halves, name):
    n = len(halves)

    def body(*refs):
        h_refs, out_refs = refs[:n], refs[n:2 * n]
        send_sems, recv_sems = refs[2 * n], refs[2 * n + 1]
        x, y, cc = _mesh_pos()
        copies = [pltpu.make_async_remote_copy(
            src_ref=h_refs[i], dst_ref=out_refs[i],
            send_sem=send_sems.at[i], recv_sem=recv_sems.at[i],
            device_id=(x, y, 1 - cc), device_id_type=MESH_ID) for i in range(n)]
        for cp in copies:
            cp.start()
        for cp in copies:
            cp.wait_recv()
        for cp in copies:
            cp.wait_send()

    return pl.pallas_call(
        body, name=name,
        in_specs=[pl.BlockSpec(memory_space=pl.ANY)] * n,
        out_specs=[pl.BlockSpec(memory_space=pl.ANY)] * n,
        out_shape=[jax.ShapeDtypeStruct(h.shape, h.dtype) for h in halves],
        scratch_shapes=[pltpu.SemaphoreType.DMA((n,)), pltpu.SemaphoreType.DMA((n,))],
    )(*halves)


def _adam_update(w, g, m, v):
    m = ADAM_B1 * m + (1.0 - ADAM_B1) * g
    v = ADAM_B2 * v + (1.0 - ADAM_B2) * (g * g)
    m_hat = m / (1.0 - ADAM_B1 ** ADAM_STEP)
    v_hat = v / (1.0 - ADAM_B2 ** ADAM_STEP)
    delta = -ADAM_LR * (m_hat / (jnp.sqrt(v_hat) + ADAM_EPS) + ADAM_WD * w)
    return delta, m, v


ADAMW_STEPS = 8


def adamw(ws, g_mine, g_sibling, ms, vs, name, chip_sums=()):
    n, n_s = len(ws), len(chip_sums)
    per_half = ADAMW_STEPS // 2

    def body(*refs):
        cc_ref = refs[0]
        ins = refs[1:1 + 5 * n]
        outs = refs[1 + 5 * n + n_s:1 + 9 * n + n_s]
        step = pl.program_id(0)
        if n_s:
            start, finish = _chip_exchange_phases(
                refs[1 + 5 * n:1 + 5 * n + n_s], refs[1 + 9 * n + n_s:1 + 9 * n + 2 * n_s],
                *refs[1 + 9 * n + 2 * n_s:])
            pl.when(step == 0)(start)
        mine = (step // per_half) == cc_ref[0]
        for i in range(n):
            w_ref, ga_ref, gb_ref, m_ref, v_ref = ins[5 * i:5 * i + 5]
            g_ref, d_ref, nm_ref, nv_ref = outs[4 * i:4 * i + 4]
            g = jnp.where(mine, ga_ref[...], gb_ref[...])
            g_ref[...] = g
            d, nm, nv = _adam_update(w_ref[...], g, m_ref[...], v_ref[...])
            d_ref[...] = d
            nm_ref[...] = nm
            nv_ref[...] = nv
        if n_s:
            pl.when(step == ADAMW_STEPS - 1)(finish)

    in_specs, out_specs, out_shape, operands = [], [], [], []
    for w, ga, gb, m, v in zip(ws, g_mine, g_sibling, ms, vs):
        r, c = w.shape
        tr = r // ADAMW_STEPS
        full = pl.BlockSpec((tr, c), lambda t, cc: (t, 0))
        part = pl.BlockSpec((tr, c), lambda t, cc: (t % per_half, 0))
        in_specs += [full, part, part, full, full]
        out_specs += [full] * 4
        out_shape += [jax.ShapeDtypeStruct((r, c), F32)] * 4
        operands += [w, ga, gb, m, v]
    anywhere = pl.BlockSpec(memory_space=pl.ANY)
    grid_spec = pltpu.PrefetchScalarGridSpec(
        num_scalar_prefetch=1, grid=(ADAMW_STEPS,),
        in_specs=in_specs + [anywhere] * n_s, out_specs=out_specs + [anywhere] * n_s,
        scratch_shapes=_chip_exchange_scratch(n_s) if n_s else [])
    cc = lax.axis_index("c").astype(jnp.int32).reshape(1)
    res = pl.pallas_call(
        body, name=name, grid_spec=grid_spec,
        out_shape=out_shape + [jax.ShapeDtypeStruct(s.shape, s.dtype) for s in chip_sums],
        compiler_params=_cparams("arbitrary"),
    )(cc, *operands, *chip_sums)
    per_shard = [tuple(res[4 * i:4 * i + 4]) for i in range(n)]
    return (per_shard, list(res[4 * n:])) if n_s else per_shard


def _pack_gather_phases(p_ref, out_ref, send_sems, recv_sems, local_sem):
    x, y, cc = _mesh_pos()
    me = 4 * x + 2 * y + cc
    flips = [(fx, fy, fc) for fx in (0, 1) for fy in (0, 1) for fc in (0, 1)][1:]

    def copy(k, row):
        fx, fy, fc = flips[k]
        return pltpu.make_async_remote_copy(
            src_ref=p_ref, dst_ref=out_ref.at[row],
            send_sem=send_sems.at[k], recv_sem=recv_sems.at[k],
            device_id=(x ^ fx, y ^ fy, cc ^ fc), device_id_type=MESH_ID)

    def local():
        return pltpu.make_async_copy(p_ref, out_ref.at[me], local_sem)

    def start():
        local().start()
        for k in range(len(flips)):
            copy(k, me).start()

    def finish():
        for k, (fx, fy, fc) in enumerate(flips):
            copy(k, 4 * (x ^ fx) + 2 * (y ^ fy) + (cc ^ fc)).wait_recv()
        for k in range(len(flips)):
            copy(k, me).wait_send()
        local().wait()

    return start, finish


def small_update(gathered, wpack, mpack, vpack):
    hw = HGRN_WIDTH

    def body(g_ref, w_ref, m_ref, v_ref, go_ref, d_ref, nm_ref, nv_ref, loss_ref):
        g = g_ref[0]
        for d in range(1, N_DEV):
            g = g + g_ref[d]
        wv = w_ref[...]
        a0, a1 = wv[4:5, :hw], wv[4:5, hw:]
        mx = jnp.maximum(a0, a1)
        e0, e1 = jnp.exp(a0 - mx), jnp.exp(a1 - mx)
        lb = e0 / (e0 + e1)
        dl = g[4:5, :hw] * lb * (1.0 - lb)
        row = lax.broadcasted_iota(jnp.int32, g.shape, 0)
        lb_row = jnp.concatenate([dl, -dl], axis=1)
        grads = jnp.where(row == 4, lb_row, jnp.where(row < 4, g, 0.0))
        go_ref[...] = grads
        d, nm, nv = _adam_update(wv, grads, m_ref[...], v_ref[...])
        d_ref[...] = d
        nm_ref[...] = nm
        nv_ref[...] = nv
        loss_ref[...] = jnp.zeros((8, LANES), F32) + jnp.sum(g[5:6, :])

    vm = pl.BlockSpec(memory_space=pltpu.VMEM)
    return pl.pallas_call(
        body, name="small_update",
        in_specs=[vm] * 4, out_specs=[vm] * 5,
        out_shape=[jax.ShapeDtypeStruct(wpack.shape, F32)] * 4 + [jax.ShapeDtypeStruct((8, LANES), F32)],
    )(gathered, wpack, mpack, vpack)


def _pack_small(n1, n2, fn, hn, lbl):
    z = jnp.zeros((1, D_MODEL - HGRN_WIDTH), F32)
    rows = [n1.reshape(1, D_MODEL), n2.reshape(1, D_MODEL), fn.reshape(1, D_MODEL),
            jnp.concatenate([hn.reshape(1, HGRN_WIDTH), z], axis=1), lbl.reshape(1, 2 * HGRN_WIDTH),
            jnp.zeros((3, D_MODEL), F32)]
    return jnp.concatenate(rows, axis=0)


def _unpack_small(pack):
    return (pack[0:1], pack[4].reshape(2, HGRN_WIDTH), pack[3:4, :HGRN_WIDTH], pack[1:2], pack[2])


def kernel(x, norm1_w, w_in, lb_logits, hgrn_norm_w, w_out, norm2_w, w_gate_up, w_down, final_norm_w, loss_target, m_norm1_w, m_w_in, m_lb_logits, m_hgrn_norm_w, m_w_out, m_norm2_w, m_w_gate_up, m_w_down, m_final_norm_w, v_norm1_w, v_w_in, v_lb_logits, v_hgrn_norm_w, v_w_out, v_norm2_w, v_w_gate_up, v_w_down, v_final_norm_w):
    seq = x.shape[1]
    xs = x.reshape(seq, D_MODEL)
    target = loss_target.reshape(seq, D_MODEL)
    shards = {"w_in": w_in[0], "w_out": w_out[0], "w_gu": w_gate_up[0], "w_down": w_down[0]}

    cast = {k: cast_bf16(w, "cast_" + k) for k, w in shards.items()}
    w_in4 = allgather_halves(cast["w_in"], "gather_w_in").reshape(N_CHIPS, D_MODEL, -1)

    cos_t, sin_t = _rope_tables(seq)
    fw = final_norm_w.reshape(1, D_MODEL)

    qr, kr, va, hg, u, q16, k16, v16, g_out, g_down = in_proj(
        xs, norm1_w, w_in4, cos_t, sin_t, [cast["w_out"], cast["w_down"]])
    ya, lse = attn_fwd(qr, kr, va, q16, k16, v16)
    yb, o_pre, st0, g_gu = hgrn_fwd(hg, lb_logits, hgrn_norm_w, [cast["w_gu"]])
    w_out_f = g_out.reshape(D_MODEL, D_MODEL)
    w_gu4 = g_gu.reshape(N_CHIPS, D_MODEL, -1)
    w_down_f = g_down.reshape(FFN_HIDDEN, D_MODEL)
    mixed, h1, u2, g, up, act, dh2, acc_fin = ffn_fwd(
        ya, yb, xs, w_out_f, norm2_w, w_gu4, w_down_f, fw, target)

    cw_in, cw_gu = w_in4.shape[2], w_gu4.shape[2]
    dgu, dh1, dya, dyb, delta, acc_n2 = ffn_bwd(dh2, w_down_f, g, up, w_gu4, h1, norm2_w, w_out_f, ya)
    early = [
        weight_grad(mixed, dh1, D_MODEL, "wgrad_out").reshape(N_CHIPS, 2, D_MODEL // 8, D_MODEL),
        weight_grad(u2, dgu, cw_gu, "wgrad_gu", group=2).reshape(N_CHIPS, 2, D_MODEL // 2, cw_gu),
        weight_grad(act, dh2, D_MODEL, "wgrad_down").reshape(N_CHIPS, 2, FFN_HIDDEN // 8, D_MODEL),
    ]
    early_names = ["out", "gu", "down"]
    dq, dk, dv, *early_recv = attn_bwd(qr, kr, va, q16, k16, v16, dya, lse, delta, early)
    early_sums = [add_own_half(gr, rc, "add_half_" + nm) for gr, rc, nm in zip(early, early_recv, early_names)]
    dhg, acc_hg, *early_parts = hgrn_bwd(hg, lb_logits, hgrn_norm_w, o_pre, st0, dyb, early_sums)
    dproj, dx, acc_n1 = in_bwd(dq, dk, dv, dhg, cos_t, sin_t, w_in4, xs, norm1_w, dh1)
    z512 = jnp.zeros((1, D_MODEL - HGRN_WIDTH), F32)
    gpack = jnp.concatenate([
        acc_n1[0:1], acc_n2[0:1], acc_fin[0:1],
        jnp.concatenate([acc_hg[0:1], z512], axis=1), jnp.concatenate([acc_hg[1:2], z512], axis=1),
        acc_fin[1:2], jnp.zeros((2, D_MODEL), F32)], axis=0)
    g_in, gathered_packs = weight_grad(u, dproj, cw_in, "wgrad_in", group=2, small_pack=gpack)
    late = [g_in.reshape(N_CHIPS, 2, D_MODEL // 2, cw_in)]
    late_recv = exchange_with_sibling(late, "grad_exchange_sibling_late")
    late_sums = [add_own_half(late[0], late_recv[0], "add_half_in")]

    early_halves = [sum_chips(s, p, "sum_chips_" + nm) for s, p, nm in zip(early_sums, early_parts, early_names)]
    early_others = share_with_sibling(early_halves, "grad_share_sibling_early")
    early_keys = ["w_out", "w_gu", "w_down"]
    moments = {"w_in": (m_w_in, v_w_in), "w_out": (m_w_out, v_w_out),
               "w_gu": (m_w_gate_up, v_w_gate_up), "w_down": (m_w_down, v_w_down)}
    early_updates, late_parts = adamw(
        [shards[k] for k in early_keys], early_halves, early_others,
        [moments[k][0][0] for k in early_keys], [moments[k][1][0] for k in early_keys],
        "adamw_early", chip_sums=late_sums)
    late_halves = [sum_chips(late_sums[0], late_parts[0], "sum_chips_in")]
    late_others = share_with_sibling(late_halves, "grad_share_sibling_late")
    late_updates = adamw([shards["w_in"]], late_halves, late_others,
                         [moments["w_in"][0][0]], [moments["w_in"][1][0]], "adamw_in")
    big = {k: tuple(t[None] for t in upd) for k, upd in zip(early_keys + ["w_in"], early_updates + late_updates)}

    wpack = _pack_small(norm1_w, norm2_w, final_norm_w, hgrn_norm_w, lb_logits)
    mpack = _pack_small(m_norm1_w, m_norm2_w, m_final_norm_w, m_hgrn_norm_w, m_lb_logits)
    vpack = _pack_small(v_norm1_w, v_norm2_w, v_final_norm_w, v_hgrn_norm_w, v_lb_logits)
    gs, ds, nms, nvs, loss8 = small_update(gathered_packs, wpack, mpack, vpack)
    loss = loss8[0, 0]

    def assemble(small_pack, idx):
        n1, lbl, hn, n2, fn = _unpack_small(small_pack)
        return (n1, big["w_in"][idx], lbl, hn, big["w_out"][idx], n2, big["w_gu"][idx], big["w_down"][idx], fn)

    return (loss, dx.reshape(x.shape), *assemble(gs, 0), *assemble(ds, 1), *assemble(nms, 2), *assemble(nvs, 3))
```

```python
import functools

import jax
import jax.numpy as jnp
from jax import lax
from jax.experimental import pallas as pl
from jax.experimental.pallas import tpu as pltpu

F32 = jnp.float32
BF16 = jnp.bfloat16

D_MODEL = 1024
ATTN_WIDTH = 512
HEAD_DIM = 64
DILATED_PAIRS = ((128, 1), (512, 4), (2048, 16))
ATTN_BLOCK = 128
ROPE_THETA = 10000.0
HGRN_WIDTH = 512
HGRN_CHUNK = 16
HGRN_HEADS = 4
IN_PROJ_WIDTH = 3584
FFN_HIDDEN = 2816
NORM_EPS = 1e-6
ATTN_SCALE = HEAD_DIM ** -0.5
N_CHIPS = 4
N_DEV = 8

ADAM_LR = 0.001
ADAM_B1 = 0.9
ADAM_B2 = 0.999
ADAM_EPS = 1e-08
ADAM_WD = 0.01
ADAM_STEP = 10

LANES = 128
HGRN_ROWS = 128
HGRN_STEP_BLOCKS = 2
ROW_TILE = 256
WIDE_ROW_TILE = 512
ATTN_STEP_ROWS = 2048
ATTN_FWD_UNROLL = 16
ATTN_BWD_UNROLL = 16
ATTN_MAJOR_DILATION = 16
VMEM_LIMIT = 56 * 1024 * 1024
NEG_BIG = -1e30
MESH_ID = pl.DeviceIdType.MESH


def _cparams(*sem):
    return pltpu.CompilerParams(dimension_semantics=tuple(sem), vmem_limit_bytes=VMEM_LIMIT)


def _dot(a, b):
    return jnp.dot(a, b, preferred_element_type=F32)


def _dot_nt(a, b):
    return lax.dot_general(a, b, (((1,), (1,)), ((), ())), preferred_element_type=F32)


def _dot_tn(a, b):
    return lax.dot_general(a, b, (((0,), (0,)), ((), ())), preferred_element_type=F32)


def _sigmoid(x):
    return 1.0 / (1.0 + jnp.exp(-x))


def _full(shape):
    n = len(shape)
    return pl.BlockSpec(shape, lambda *_: (0,) * n)


def _weight(shape):
    n = len(shape)
    return pl.BlockSpec(shape, lambda *_: (0,) * n, pipeline_mode=pl.Buffered(1))


def _rows(tm, width):
    return pl.BlockSpec((tm, width), lambda i: (i, 0))


def _swap32(x):
    lane = lax.broadcasted_iota(jnp.int32, x.shape, 1)
    first = (lane % HEAD_DIM) < (HEAD_DIM // 2)
    return jnp.where(first, pltpu.roll(x, LANES - 32, axis=1), pltpu.roll(x, 32, axis=1))


def _rotary_fwd(x, cos, sin_signed):
    parts = []
    for j in range(x.shape[1] // LANES):
        xc = x[:, j * LANES:(j + 1) * LANES]
        parts.append(xc * cos + _swap32(xc) * sin_signed)
    return jnp.concatenate(parts, axis=1)


def _rotary_bwd(dy, cos, sin_signed):
    parts = []
    for j in range(dy.shape[1] // LANES):
        dc = dy[:, j * LANES:(j + 1) * LANES]
        parts.append(dc * cos + _swap32(dc * sin_signed))
    return jnp.concatenate(parts, axis=1)


def _rope_tables(seq):
    half = HEAD_DIM // 2
    inv_freq = ROPE_THETA ** (-jnp.arange(half, dtype=F32) / half)
    ang = jnp.arange(seq, dtype=F32)[:, None] * inv_freq[None, :]
    cos, sin = jnp.cos(ang), jnp.sin(ang)
    cos_t = jnp.tile(cos, (1, LANES // half))
    sin_t = jnp.tile(jnp.concatenate([-sin, sin], axis=1), (1, LANES // HEAD_DIM))
    return cos_t, sin_t


def cast_bf16(w, name):
    r, c = w.shape
    half = r // 2

    def body(w_ref, o_ref):
        o_ref[...] = w_ref[...].astype(BF16)

    return pl.pallas_call(
        body, name=name, grid=(2,),
        in_specs=[pl.BlockSpec((half, c), lambda i: (i, 0))],
        out_specs=pl.BlockSpec((None, half, c), lambda i: (i, 0, 0)),
        out_shape=jax.ShapeDtypeStruct((2, half, c), BF16),
        compiler_params=_cparams("parallel"),
    )(w)


def _mesh_pos():
    return lax.axis_index("x"), lax.axis_index("y"), lax.axis_index("c")


GATHER_COPIES = 7


def _gather_phases(x_refs, out_refs, send_sems, recv_sems, local_sems):
    n = len(x_refs)
    x, y, cc = _mesh_pos()
    me, sibling = (x, y, cc), (x, y, 1 - cc)
    chips = [(1 - x, y), (x, 1 - y), (1 - x, 1 - y)]

    def rows(i, px, py, pc):
        return out_refs[i].at[4 * px + 2 * py + pc]

    def copy(i, k, block, to, src=None):
        return pltpu.make_async_remote_copy(
            src_ref=rows(i, *block) if src is None else src, dst_ref=rows(i, *block),
            send_sem=send_sems.at[GATHER_COPIES * i + k], recv_sem=recv_sems.at[GATHER_COPIES * i + k],
            device_id=to, device_id_type=MESH_ID)

    def local(i):
        return pltpu.make_async_copy(x_refs[i].at[cc], rows(i, *me), local_sems.at[i])

    def first(i):
        mine = x_refs[i].at[cc]
        return [copy(i, 0, me, sibling, src=mine)] + [
            copy(i, 1 + j, me, (*chip, cc), src=mine) for j, chip in enumerate(chips)]

    def passed(i):
        return [copy(i, 4 + j, (*chip, cc), sibling) for j, chip in enumerate(chips)]

    def start():
        for i in range(n):
            local(i).start()
            for cp in first(i):
                cp.start()

    def forward():
        for i in range(n):
            onward = passed(i)
            for j, chip in enumerate(chips):
                copy(i, 1 + j, (*chip, cc), me).wait_recv()
                onward[j].start()

    def finish():
        for i in range(n):
            copy(i, 0, sibling, me).wait_recv()
            for j, chip in enumerate(chips):
                copy(i, 4 + j, (*chip, 1 - cc), me).wait_recv()
            for cp in first(i) + passed(i):
                cp.wait_send()
            local(i).wait()

    return start, forward, finish


def _gather_scratch(n):
    return [pltpu.SemaphoreType.DMA((GATHER_COPIES * n,)), pltpu.SemaphoreType.DMA((GATHER_COPIES * n,)),
            pltpu.SemaphoreType.DMA((n,))]


def _gathered_shape(halves):
    return jax.ShapeDtypeStruct((N_DEV,) + halves.shape[1:], halves.dtype)


def allgather_halves(halves, name):
    def body(x_ref, out_ref, send_sems, recv_sems, local_sems):
        start, forward, finish = _gather_phases([x_ref], [out_ref], send_sems, recv_sems, local_sems)
        start()
        forward()
        finish()

    return pl.pallas_call(
        body, name=name,
        in_specs=[pl.BlockSpec(memory_space=pl.ANY)],
        out_specs=pl.BlockSpec(memory_space=pl.ANY),
        out_shape=_gathered_shape(halves),
        scratch_shapes=_gather_scratch(1),
    )(halves)


def _rms(x):
    return lax.rsqrt(jnp.mean(x * x, axis=-1, keepdims=True) + NORM_EPS)


def in_proj(x, norm1_w, w_in4, cos_t, sin_t, weight_halves=()):
    seq = x.shape[0]
    tm = WIDE_ROW_TILE
    cw = w_in4.shape[2]
    n_w = len(weight_halves)
    steps = seq // tm
    major = ATTN_MAJOR_DILATION
    slabs = ATTN_WIDTH // LANES

    def body(*refs):
        x_ref, nw_ref, w_ref, cos_ref, sin_ref = refs[:5]
        q_ref, k_ref, v_ref, hg_ref, u_ref = refs[5 + n_w:10 + n_w]
        major_refs = refs[10 + n_w:13 + n_w]
        slab_scr = refs[13 + 2 * n_w]
        step = pl.program_id(0)
        if n_w:
            start, forward, finish = _gather_phases(
                refs[5:5 + n_w], refs[13 + n_w:13 + 2 * n_w], *refs[14 + 2 * n_w:])
            pl.when(step == 0)(start)
            pl.when(step == (3 * steps) // 4)(forward)
        xv = x_ref[...]
        u = ((xv * _rms(xv)) * nw_ref[...]).astype(BF16)
        u_ref[...] = u
        proj = jnp.concatenate([_dot(u, w_ref[j]) for j in range(N_CHIPS)], axis=1)
        cos, sin = cos_ref[...], sin_ref[...]
        a = ATTN_WIDTH
        qkv = (_rotary_fwd(proj[:, :a], cos, sin), _rotary_fwd(proj[:, a:2 * a], cos, sin), proj[:, 2 * a:3 * a])
        for ref, val in zip((q_ref, k_ref, v_ref), qkv):
            ref[...] = val
        hg_ref[...] = proj[:, 3 * a:]
        for idx, val in enumerate(qkv):
            for s in range(slabs):
                slab_scr[idx, s] = val[:, s * LANES:(s + 1) * LANES]
        for idx, out in enumerate(major_refs):
            for r in range(major):
                for s in range(slabs):
                    out[r, :, s * LANES:(s + 1) * LANES] = (
                        slab_scr.at[idx, s][pl.ds(r, tm // major, stride=major), :].astype(BF16))
        if n_w:
            pl.when(step == steps - 1)(finish)

    anywhere = pl.BlockSpec(memory_space=pl.ANY)
    major_spec = pl.BlockSpec((major, tm // major, ATTN_WIDTH), lambda i: (0, i, 0))
    return pl.pallas_call(
        body, name="in_proj", grid=(steps,),
        in_specs=[_rows(tm, D_MODEL), _full((1, D_MODEL)), _weight((N_CHIPS, D_MODEL, cw)),
                  _rows(tm, LANES), _rows(tm, LANES)] + [anywhere] * n_w,
        out_specs=[_rows(tm, ATTN_WIDTH)] * 3 + [_rows(tm, 4 * HGRN_WIDTH), _rows(tm, D_MODEL)]
        + [major_spec] * 3 + [anywhere] * n_w,
        out_shape=[jax.ShapeDtypeStruct((seq, ATTN_WIDTH), F32)] * 3
        + [jax.ShapeDtypeStruct((seq, 4 * HGRN_WIDTH), F32), jax.ShapeDtypeStruct((seq, D_MODEL), BF16)]
        + [jax.ShapeDtypeStruct((major, seq // major, ATTN_WIDTH), BF16)] * 3
        + [_gathered_shape(h) for h in weight_halves],
        scratch_shapes=[pltpu.VMEM((3, slabs, tm, LANES), F32)] + (_gather_scratch(n_w) if n_w else []),
        compiler_params=_cparams("arbitrary"),
    )(x, norm1_w, w_in4, cos_t, sin_t, *weight_halves)


def _head_masks():
    lane = lax.broadcasted_iota(jnp.int32, (1, LANES), 1)
    return [(lane // HEAD_DIM) == h for h in range(LANES // HEAD_DIM)]


def _window_valid(no_prev):
    qi = lax.broadcasted_iota(jnp.int32, (ATTN_BLOCK, 2 * ATTN_BLOCK), 0)
    kj = lax.broadcasted_iota(jnp.int32, (ATTN_BLOCK, 2 * ATTN_BLOCK), 1)
    valid = (kj >= qi) & (kj <= qi + ATTN_BLOCK)
    return valid & (jnp.logical_not(no_prev) | (kj >= ATTN_BLOCK))


def _strided_rows(start, dilation):
    if dilation == 1:
        return pl.ds(start, ATTN_BLOCK)
    return pl.ds(start, ATTN_BLOCK, stride=dilation)


def _block_before(edge_ref, cur_ref, t, r, span, dilation, per_step):
    edge = edge_ref[_strided_rows(ATTN_STEP_ROWS - span + r, dilation), :]
    if per_step == 1:
        return edge
    inside = cur_ref[_strided_rows(r + span * jnp.maximum(t - 1, 0), dilation), :]
    return jnp.where(t == 0, edge, inside)


def _attn_specs():
    cur = pl.BlockSpec((ATTN_STEP_ROWS, LANES), lambda hp, j: (j, hp))
    prev = pl.BlockSpec((ATTN_STEP_ROWS, LANES), lambda hp, j: (jnp.maximum(j - 1, 0), hp))
    return cur, prev


def _for_each_block(dilation, unroll, block):
    span = ATTN_BLOCK * dilation
    per_step = ATTN_STEP_ROWS // span

    def trip(it, carry):
        block(it // dilation, it % dilation, span, per_step)
        return carry

    lax.fori_loop(0, per_step * dilation, trip, 0, unroll=unroll)


def _load_qkv(natural, major, t, r, span, dilation, per_step):
    if dilation == ATTN_MAJOR_DILATION:
        q_ref, kc_ref, vc_ref, kp_ref, vp_ref = major
        return (q_ref[r] * ATTN_SCALE, jnp.concatenate([kp_ref[r], kc_ref[r]], axis=0),
                jnp.concatenate([vp_ref[r], vc_ref[r]], axis=0))
    q_ref, kc_ref, vc_ref, kp_ref, vp_ref = natural
    rows = _strided_rows(r + span * t, dilation)
    kp = _block_before(kp_ref, kc_ref, t, r, span, dilation, per_step)
    vp = _block_before(vp_ref, vc_ref, t, r, span, dilation, per_step)
    return ((q_ref[rows, :] * ATTN_SCALE).astype(BF16),
            jnp.concatenate([kp, kc_ref[rows, :]], axis=0).astype(BF16),
            jnp.concatenate([vp, vc_ref[rows, :]], axis=0).astype(BF16))


def _major_specs():
    assert ATTN_STEP_ROWS == ATTN_BLOCK * ATTN_MAJOR_DILATION
    shape = (ATTN_MAJOR_DILATION, ATTN_BLOCK, LANES)
    return (pl.BlockSpec(shape, lambda hp, j: (0, j, hp)),
            pl.BlockSpec(shape, lambda hp, j: (0, jnp.maximum(j - 1, 0), hp)))


def attn_fwd(q, k, v, q16, k16, v16):
    seq = q.shape[0]
    cur, prev = _attn_specs()
    cur16, prev16 = _major_specs()

    def body(*refs):
        natural, major, (y_ref, lse_ref) = refs[:5], refs[5:10], refs[10:]
        first_step = pl.program_id(1) == 0
        masks = _head_masks()
        for index, (_, dilation) in enumerate(DILATED_PAIRS):
            def block(t, r, span, per_step, dilation=dilation, merge=index > 0):
                rows = _strided_rows(r + span * t, dilation)
                q2, k2, v2 = _load_qkv(natural, major, t, r, span, dilation, per_step)
                valid = _window_valid(first_step & (t == 0))
                o_acc = jnp.zeros((ATTN_BLOCK, LANES), F32)
                l_acc = jnp.zeros((ATTN_BLOCK, LANES), F32)
                for mh in masks:
                    qm = jnp.where(mh, q2, jnp.zeros_like(q2))
                    s = jnp.where(valid, _dot_nt(qm, k2), NEG_BIG)
                    m = jnp.max(s, axis=-1, keepdims=True)
                    p = jnp.exp(s - m)
                    l = jnp.sum(p, axis=-1, keepdims=True)
                    o = _dot(p.astype(BF16), v2) / l
                    o_acc = jnp.where(mh, o, o_acc)
                    l_acc = jnp.where(mh, m + jnp.log(l), l_acc)
                if merge:
                    y_old, l_old = y_ref[rows, :], lse_ref[rows, :]
                    mx = jnp.maximum(l_old, l_acc)
                    e_old, e_new = jnp.exp(l_old - mx), jnp.exp(l_acc - mx)
                    den = e_old + e_new
                    o_acc = (y_old * e_old + o_acc * e_new) / den
                    l_acc = mx + jnp.log(den)
                y_ref[rows, :] = o_acc
                lse_ref[rows, :] = l_acc

            _for_each_block(dilation, ATTN_FWD_UNROLL, block)

    return pl.pallas_call(
        body, name="attn_fwd", grid=(ATTN_WIDTH // LANES, seq // ATTN_STEP_ROWS),
        in_specs=[cur, cur, cur, prev, prev, cur16, cur16, cur16, prev16, prev16],
        out_specs=[cur, cur],
        out_shape=[jax.ShapeDtypeStruct((seq, ATTN_WIDTH), F32)] * 2,
        compiler_params=_cparams("parallel", "parallel"),
    )(q, k, v, k, v, q16, k16, v16, k16, v16)


def _chunk_cumsum(x, reverse=False):
    rc = lax.broadcasted_iota(jnp.int32, x.shape, 0) % HGRN_CHUNK
    sh = 1
    while sh < HGRN_CHUNK:
        if reverse:
            x = x + jnp.where(rc + sh < HGRN_CHUNK, pltpu.roll(x, x.shape[0] - sh, axis=0), 0.0)
        else:
            x = x + jnp.where(rc >= sh, pltpu.roll(x, sh, axis=0), 0.0)
        sh *= 2
    return x


def _chunk_row(x, row):
    return _chunk_rows([x[n * HGRN_CHUNK + row:n * HGRN_CHUNK + row + 1, :]
                        for n in range(x.shape[0] // HGRN_CHUNK)])


def _chunk_rows(rows):
    return jnp.concatenate([jnp.broadcast_to(r, (HGRN_CHUNK, r.shape[1])) for r in rows], axis=0)


def _hgrn_prep(hg, lbl):
    w = HGRN_WIDTH
    a0, a1 = lbl[0:1, :], lbl[1:2, :]
    mx = jnp.maximum(a0, a1)
    e0, e1 = jnp.exp(a0 - mx), jnp.exp(a1 - mx)
    lb = e0 / (e0 + e1)
    qb, fb, gb = hg[:, :w], hg[:, w:2 * w], hg[:, 3 * w:]
    sg = _sigmoid(fb)
    f = lb + (1.0 - lb) * sg
    b = _chunk_cumsum(jnp.log(f))
    bmid, btot = _chunk_row(b, HGRN_CHUNK // 2 - 1), _chunk_row(b, HGRN_CHUNK - 1)
    sq = _sigmoid(qb)
    p = dict(lb=lb, sg=sg, f=f, kk=1.0 - f, sq=sq, qf=qb * sq, gb=gb,
             e_iq=jnp.exp(b - bmid), e_ik=jnp.exp(bmid - b), e_b=jnp.exp(b),
             e_bb=jnp.exp(btot - b), e_tot=jnp.exp(btot))
    p["qi"] = p["qf"] * p["e_iq"]
    p["ki"] = p["kk"] * p["e_ik"]
    p["qs"] = p["qf"] * p["e_b"]
    p["kb"] = p["kk"] * p["e_bb"]
    return p


def _chunk_masks():
    t = lax.broadcasted_iota(jnp.int32, (HGRN_ROWS, HGRN_ROWS), 0)
    s = lax.broadcasted_iota(jnp.int32, (HGRN_ROWS, HGRN_ROWS), 1)
    tril = ((t // HGRN_CHUNK) == (s // HGRN_CHUNK)) & (s <= t)
    n_chunks = HGRN_ROWS // HGRN_CHUNK
    tt = lax.broadcasted_iota(jnp.int32, (HGRN_ROWS, n_chunks * LANES), 0)
    cc = lax.broadcasted_iota(jnp.int32, (HGRN_ROWS, n_chunks * LANES), 1)
    block = (tt // HGRN_CHUNK) == (cc // LANES)
    return tril, block


def _spread(x, block):
    n_chunks = HGRN_ROWS // HGRN_CHUNK
    return jnp.where(block, jnp.tile(x, (1, n_chunks)), jnp.zeros((), x.dtype))


def _fold(x_full, block):
    n_chunks = HGRN_ROWS // HGRN_CHUNK
    z = jnp.where(block, x_full, 0.0)
    acc = z[:, :LANES]
    for n in range(1, n_chunks):
        acc = acc + z[:, n * LANES:(n + 1) * LANES]
    return acc


def hgrn_fwd(hg, lb_logits, hnw, weight_halves=()):
    seq = hg.shape[0]
    nblk = seq // HGRN_ROWS
    n_steps = nblk // HGRN_STEP_BLOCKS
    step_rows = HGRN_ROWS * HGRN_STEP_BLOCKS
    n_chunks = HGRN_ROWS // HGRN_CHUNK
    n_w = len(weight_halves)

    def body(*refs):
        hg_ref, lbl_ref, hnw_ref = refs[:3]
        w_refs = refs[3:3 + n_w]
        yb_ref, o_ref, st0_ref = refs[3 + n_w:6 + n_w]
        g_refs = refs[6 + n_w:6 + 2 * n_w]
        st_scr = refs[6 + 2 * n_w]
        step = pl.program_id(0)
        if n_w:
            start, forward, finish = _gather_phases(w_refs, g_refs, *refs[7 + 2 * n_w:])
            pl.when(step == 0)(start)
            pl.when(step == (3 * n_steps) // 4)(forward)

        @pl.when(step == 0)
        def _():
            st_scr[...] = jnp.zeros_like(st_scr)

        tril, block = _chunk_masks()
        for sub in range(HGRN_STEP_BLOCKS):
            rows = slice(sub * HGRN_ROWS, (sub + 1) * HGRN_ROWS)
            hg_v = hg_ref[rows, :]
            p = _hgrn_prep(hg_v, lbl_ref[...])
            vv = hg_v[:, 2 * HGRN_WIDTH:3 * HGRN_WIDTH].astype(BF16)
            outs = []
            for h in range(HGRN_HEADS):
                sl = slice(h * LANES, (h + 1) * LANES)
                v_h = vv[:, sl]
                a = jnp.where(tril, _dot_nt(p["qi"][:, sl].astype(BF16), p["ki"][:, sl].astype(BF16)), 0.0)
                o = _dot(a.astype(BF16), v_h)
                upd = _dot_tn(v_h, _spread(p["kb"][:, sl].astype(BF16), block))
                st = st_scr[h]
                st0_ref[sub, h] = st
                parts = []
                for n in range(n_chunks):
                    parts.append(st.astype(BF16))
                    decay = p["e_tot"][n * HGRN_CHUNK:n * HGRN_CHUNK + 1, sl]
                    st = st * decay + upd[:, n * LANES:(n + 1) * LANES]
                st_scr[h] = st
                o = o + _dot_nt(_spread(p["qs"][:, sl].astype(BF16), block), jnp.concatenate(parts, axis=1))
                outs.append(o)
            o_ref[rows, :] = jnp.concatenate(outs, axis=1)
            normed = jnp.concatenate(
                [outs[h] * _rms(outs[h]) for h in range(HGRN_HEADS)], axis=1)
            gb = p["gb"]
            yb_ref[rows, :] = (normed * hnw_ref[...]) * (gb * _sigmoid(gb))
        if n_w:
            pl.when(step == n_steps - 1)(finish)

    anywhere = pl.BlockSpec(memory_space=pl.ANY)
    return pl.pallas_call(
        body, name="hgrn_fwd", grid=(n_steps,),
        in_specs=[_rows(step_rows, 4 * HGRN_WIDTH), _full((2, HGRN_WIDTH)), _full((1, HGRN_WIDTH))]
        + [anywhere] * n_w,
        out_specs=[_rows(step_rows, HGRN_WIDTH), _rows(step_rows, HGRN_WIDTH),
                   pl.BlockSpec((HGRN_STEP_BLOCKS, HGRN_HEADS, LANES, LANES), lambda i: (i, 0, 0, 0))]
        + [anywhere] * n_w,
        out_shape=[jax.ShapeDtypeStruct((seq, HGRN_WIDTH), F32)] * 2
        + [jax.ShapeDtypeStruct((nblk, HGRN_HEADS, LANES, LANES), F32)]
        + [_gathered_shape(h) for h in weight_halves],
        scratch_shapes=[pltpu.VMEM((HGRN_HEADS, LANES, LANES), F32)] + (_gather_scratch(n_w) if n_w else []),
        compiler_params=_cparams("arbitrary"),
    )(hg, lb_logits, hnw, *weight_halves)


def ffn_fwd(ya, yb, x, w_out, norm2_w, w_gu4, w_down, final_w, target):
    seq = x.shape[0]
    tm = ROW_TILE
    cw = w_gu4.shape[2]
    inv_d = 1.0 / D_MODEL

    def body(ya_ref, yb_ref, x_ref, wo_ref, nw_ref, wgu_ref, wd_ref, fw_ref, t_ref,
             mixed_ref, h1_ref, u2_ref, g_ref, up_ref, act_ref, dh2_ref, acc_ref):
        @pl.when(pl.program_id(0) == 0)
        def _():
            acc_ref[...] = jnp.zeros_like(acc_ref)

        mixed = jnp.concatenate([ya_ref[...], yb_ref[...]], axis=1).astype(BF16)
        mixed_ref[...] = mixed
        h1 = x_ref[...] + _dot(mixed, wo_ref[...])
        h1_ref[...] = h1
        u2 = ((h1 * _rms(h1)) * nw_ref[...]).astype(BF16)
        u2_ref[...] = u2
        g = jnp.concatenate([_dot(u2, wgu_ref[0]), _dot(u2, wgu_ref[1])], axis=1)
        up = jnp.concatenate([_dot(u2, wgu_ref[2]), _dot(u2, wgu_ref[3])], axis=1)
        g_ref[...] = g.astype(BF16)
        up_ref[...] = up.astype(BF16)
        act = ((g * (0.5 * jnp.tanh(0.5 * g) + 0.5)) * up).astype(BF16)
        act_ref[...] = act
        h2 = h1 + _dot(act, wd_ref[...])
        rf = _rms(h2)
        n = h2 * rf
        fw = fw_ref[...]
        err = n * fw - t_ref[...]
        dy = err * inv_d
        acc_ref[0:1, :] += jnp.sum(dy * n, axis=0, keepdims=True)
        acc_ref[1:2, :] += (0.5 * inv_d) * jnp.sum(err * err, axis=0, keepdims=True)
        dn = dy * fw
        dh2_ref[...] = rf * (dn - n * jnp.mean(dn * n, axis=-1, keepdims=True))

    half = _rows(tm, ATTN_WIDTH)
    wide = _rows(tm, D_MODEL)
    ffn = _rows(tm, FFN_HIDDEN)
    return pl.pallas_call(
        body, name="ffn_fwd", grid=(seq // tm,),
        in_specs=[half, half, wide, _weight((D_MODEL, D_MODEL)), _full((1, D_MODEL)),
                  _weight((N_CHIPS, D_MODEL, cw)), _weight((FFN_HIDDEN, D_MODEL)), _full((1, D_MODEL)), wide],
        out_specs=[wide, wide, wide, ffn, ffn, ffn, wide, _full((8, D_MODEL))],
        out_shape=[jax.ShapeDtypeStruct((seq, D_MODEL), BF16), jax.ShapeDtypeStruct((seq, D_MODEL), F32),
                   jax.ShapeDtypeStruct((seq, D_MODEL), BF16)]
        + [jax.ShapeDtypeStruct((seq, FFN_HIDDEN), BF16)] * 3
        + [jax.ShapeDtypeStruct((seq, D_MODEL), F32), jax.ShapeDtypeStruct((8, D_MODEL), F32)],
        compiler_params=_cparams("arbitrary"),
    )(ya, yb, x, w_out, norm2_w, w_gu4, w_down, final_w, target)


def _head_sum_matrix():
    i = jnp.arange(ATTN_WIDTH)
    return ((i[:, None] // HEAD_DIM) == (i[None, :] // HEAD_DIM)).astype(BF16)


def ffn_bwd(dh2, w_down, g, up, w_gu4, h1, norm2_w, w_out, ya):
    seq = h1.shape[0]
    tm = ROW_TILE
    cw = w_gu4.shape[2]
    hsum = _head_sum_matrix()

    def body(dh2_ref, wd_ref, g_ref, up_ref, w_ref, h1_ref, nw_ref, wo_ref, ya_ref, hs_ref,
             dgu_ref, dh1_ref, dya_ref, dyb_ref, delta_ref, acc_ref):
        @pl.when(pl.program_id(0) == 0)
        def _():
            acc_ref[...] = jnp.zeros_like(acc_ref)

        dh2_b = dh2_ref[...].astype(BF16)
        du2 = jnp.zeros((tm, D_MODEL), F32)
        for j in range(N_CHIPS // 2):
            dact = _dot_nt(dh2_b, wd_ref[j * cw:(j + 1) * cw, :])
            gv = g_ref[:, j * cw:(j + 1) * cw].astype(F32)
            sg = _sigmoid(gv)
            dg = (dact * up_ref[:, j * cw:(j + 1) * cw].astype(F32) * (sg * (1.0 + gv * (1.0 - sg)))).astype(BF16)
            dup = (dact * (gv * sg)).astype(BF16)
            dgu_ref[:, j * cw:(j + 1) * cw] = dg
            dgu_ref[:, FFN_HIDDEN + j * cw:FFN_HIDDEN + (j + 1) * cw] = dup
            du2 = du2 + _dot_nt(dg, w_ref[j]) + _dot_nt(dup, w_ref[N_CHIPS // 2 + j])
        h1 = h1_ref[...]
        r2 = _rms(h1)
        nh = h1 * r2
        acc_ref[0:1, :] += jnp.sum(du2 * nh, axis=0, keepdims=True)
        dn = du2 * nw_ref[...]
        dh1 = dh2_ref[...] + r2 * (dn - nh * jnp.mean(dn * nh, axis=-1, keepdims=True))
        dh1_ref[...] = dh1
        dmixed = _dot_nt(dh1.astype(BF16), wo_ref[...])
        dya = dmixed[:, :ATTN_WIDTH]
        dya_ref[...] = dya
        dyb_ref[...] = dmixed[:, ATTN_WIDTH:]
        prod = dya * ya_ref[...]
        hi = prod.astype(BF16)
        lo = (prod - hi.astype(F32)).astype(BF16)
        delta_ref[...] = _dot(hi, hs_ref[...]) + _dot(lo, hs_ref[...])

    wide = _rows(tm, D_MODEL)
    half = _rows(tm, ATTN_WIDTH)
    ffn = _rows(tm, FFN_HIDDEN)
    return pl.pallas_call(
        body, name="ffn_bwd", grid=(seq // tm,),
        in_specs=[wide, _weight((FFN_HIDDEN, D_MODEL)), ffn, ffn, _weight((N_CHIPS, D_MODEL, cw)), wide,
                  _full((1, D_MODEL)), _weight((D_MODEL, D_MODEL)), half, _full((ATTN_WIDTH, ATTN_WIDTH))],
        out_specs=[_rows(tm, 2 * FFN_HIDDEN), wide, half, half, half, _full((8, D_MODEL))],
        out_shape=[jax.ShapeDtypeStruct((seq, 2 * FFN_HIDDEN), BF16), jax.ShapeDtypeStruct((seq, D_MODEL), F32)]
        + [jax.ShapeDtypeStruct((seq, ATTN_WIDTH), F32)] * 3 + [jax.ShapeDtypeStruct((8, D_MODEL), F32)],
        compiler_params=_cparams("arbitrary"),
    )(dh2, w_down, g, up, w_gu4, h1, norm2_w, w_out, ya, hsum)


def attn_bwd(q, k, v, q16, k16, v16, dy, lse, delta, sibling_grads=()):
    seq = q.shape[0]
    cur, prev = _attn_specs()
    cur16, prev16 = _major_specs()
    whole = pl.BlockSpec((seq, LANES), lambda hp, j: (0, hp))
    n_g = len(sibling_grads)
    n_hp, n_steps = ATTN_WIDTH // LANES, seq // ATTN_STEP_ROWS
    n_in = 13

    def body(*refs):
        natural, major = refs[:5], refs[5:10]
        dy_ref, lse_ref, dl_ref = refs[10:n_in]
        dq_ref, dk_ref, dv_ref = refs[n_in + n_g:n_in + 3 + n_g]
        first_step = pl.program_id(1) == 0
        base = pl.program_id(1) * ATTN_STEP_ROWS
        masks = _head_masks()
        if n_g:
            start, finish = _sibling_exchange_phases(
                refs[n_in:n_in + n_g], refs[n_in + 3 + n_g:n_in + 3 + 2 * n_g], *refs[n_in + 3 + 2 * n_g:])
            pl.when((pl.program_id(0) == 0) & first_step)(start)

        def block(t, r, span, per_step, dilation, add):
            rows = _strided_rows(r + span * t, dilation)
            at_edge = t == 0
            q2, k2, v2 = _load_qkv(natural, major, t, r, span, dilation, per_step)
            dy2 = dy_ref[rows, :].astype(BF16)
            lse2, dl2 = lse_ref[rows, :], dl_ref[rows, :]
            valid = _window_valid(first_step & at_edge)
            zero = jnp.zeros_like(q2)
            qms, dyms, ps, dss, kms = [], [], [], [], []
            for h, mh in enumerate(masks):
                c0 = h * HEAD_DIM
                qm, dym = jnp.where(mh, q2, zero), jnp.where(mh, dy2, zero)
                s = _dot_nt(qm, k2)
                p = jnp.where(valid, jnp.exp(s - lse2[:, c0:c0 + 1]), 0.0)
                dp = _dot_nt(dym, v2)
                dss.append((p * (dp - dl2[:, c0:c0 + 1])).astype(BF16))
                ps.append(p.astype(BF16))
                qms.append(qm)
                dyms.append(dym)
                kms.append(jnp.where(mh, k2, jnp.zeros_like(k2)))
            dq = _dot(jnp.concatenate(dss, axis=1), jnp.concatenate(kms, axis=0)) * ATTN_SCALE
            p_all, ds_all = jnp.concatenate(ps, axis=0), jnp.concatenate(dss, axis=0)
            dy_all, q_all = jnp.concatenate(dyms, axis=0), jnp.concatenate(qms, axis=0)
            dv_full, dk_full = _dot_tn(dy_all, p_all).T, _dot_tn(q_all, ds_all).T
            here = _strided_rows(base + r + span * t, dilation)
            if add:
                dq_ref[rows, :] += dq
                dk_ref[here, :] += dk_full[ATTN_BLOCK:]
                dv_ref[here, :] += dv_full[ATTN_BLOCK:]
            else:
                dq_ref[rows, :] = dq
                dk_ref[here, :] = dk_full[ATTN_BLOCK:]
                dv_ref[here, :] = dv_full[ATTN_BLOCK:]
            back = _strided_rows(jnp.maximum(base + r + span * t - span, r), dilation)
            dk_ref[back, :] += dk_full[:ATTN_BLOCK]
            dv_ref[back, :] += dv_full[:ATTN_BLOCK]

        for index, (_, dilation) in enumerate(DILATED_PAIRS):
            _for_each_block(dilation, ATTN_BWD_UNROLL,
                            functools.partial(block, dilation=dilation, add=index > 0))
        if n_g:
            pl.when((pl.program_id(0) == n_hp - 1) & (pl.program_id(1) == n_steps - 1))(finish)

    anywhere = pl.BlockSpec(memory_space=pl.ANY)
    return pl.pallas_call(
        body, name="attn_bwd", grid=(n_hp, n_steps),
        in_specs=[cur, cur, cur, prev, prev, cur16, cur16, cur16, prev16, prev16, cur, cur, cur]
        + [anywhere] * n_g,
        out_specs=[cur, whole, whole] + [anywhere] * n_g,
        out_shape=[jax.ShapeDtypeStruct((seq, ATTN_WIDTH), F32)] * 3 + _sibling_exchange_shapes(sibling_grads),
        scratch_shapes=_sibling_exchange_scratch(n_g) if n_g else [],
        compiler_params=_cparams("arbitrary", "arbitrary"),
    )(q, k, v, k, v, q16, k16, v16, k16, v16, dy, lse, delta, *sibling_grads)


def hgrn_bwd(hg, lb_logits, hnw, o_pre, st0, dyb, chip_sums=()):
    seq = hg.shape[0]
    step_rows = HGRN_ROWS * HGRN_STEP_BLOCKS
    n_steps = seq // step_rows
    n_chunks = HGRN_ROWS // HGRN_CHUNK
    w = HGRN_WIDTH
    n_s = len(chip_sums)

    def body(*refs):
        hg_ref, lbl_ref, hnw_ref, o_ref, st0_ref, dyb_ref = refs[:6]
        dhg_ref, acc_ref = refs[6 + n_s:8 + n_s]
        dst_scr = refs[8 + 2 * n_s]
        step = pl.program_id(0)
        if n_s:
            start, finish = _chip_exchange_phases(refs[6:6 + n_s], refs[8 + n_s:8 + 2 * n_s], *refs[9 + 2 * n_s:])
            pl.when(step == 0)(start)

        @pl.when(step == 0)
        def _():
            dst_scr[...] = jnp.zeros_like(dst_scr)
            acc_ref[...] = jnp.zeros_like(acc_ref)

        tril, block = _chunk_masks()
        for sub in reversed(range(HGRN_STEP_BLOCKS)):
            rows = slice(sub * HGRN_ROWS, (sub + 1) * HGRN_ROWS)
            hg_v = hg_ref[rows, :]
            p = _hgrn_prep(hg_v, lbl_ref[...])
            vv = hg_v[:, 2 * w:3 * w].astype(BF16)
            hnw_v = hnw_ref[...]
            gb = p["gb"]
            sgg = _sigmoid(gb)
            silu_g = gb * sgg
            dyb_v = dyb_ref[rows, :]
            o_v = o_ref[rows, :]

            d_on = dyb_v * hnw_v * silu_g
            on_parts, do_parts = [], []
            for h in range(HGRN_HEADS):
                sl = slice(h * LANES, (h + 1) * LANES)
                rs = _rms(o_v[:, sl])
                on = o_v[:, sl] * rs
                on_parts.append(on)
                do_parts.append(rs * (d_on[:, sl] - on * jnp.mean(d_on[:, sl] * on, axis=-1, keepdims=True)))
            on_all = jnp.concatenate(on_parts, axis=1)
            dgb = dyb_v * on_all * hnw_v * (sgg * (1.0 + gb * (1.0 - sgg)))
            acc_ref[0:1, :] += jnp.sum(dyb_v * on_all * silu_g, axis=0, keepdims=True)

            dqf_parts, dkk_parts, db_parts, dv_parts, dbt_parts, dkbkb_parts = [], [], [], [], [], []
            for h in range(HGRN_HEADS):
                sl = slice(h * LANES, (h + 1) * LANES)
                v_h = vv[:, sl]
                do_h = do_parts[h].astype(BF16)
                qi, ki, qs, kb = p["qi"][:, sl], p["ki"][:, sl], p["qs"][:, sl], p["kb"][:, sl]
                qi_b, ki_b = qi.astype(BF16), ki.astype(BF16)
                kb_cat = _spread(kb.astype(BF16), block)
                qs_cat = _spread(qs.astype(BF16), block)
                upd = _dot_tn(v_h, kb_cat)
                st = st0_ref[sub, h]
                st_parts = []
                for n in range(n_chunks):
                    st_parts.append(st)
                    decay = p["e_tot"][n * HGRN_CHUNK:n * HGRN_CHUNK + 1, sl]
                    st = st * decay + upd[:, n * LANES:(n + 1) * LANES]
                st_cat = jnp.concatenate([s_.astype(BF16) for s_ in st_parts], axis=1)
                wgt = _dot_tn(do_h, qs_cat)
                dst = dst_scr[h]
                dst_parts = [None] * n_chunks
                dbt_rows = [None] * n_chunks
                for n in reversed(range(n_chunks)):
                    dst_parts[n] = dst.astype(BF16)
                    decay = p["e_tot"][n * HGRN_CHUNK:n * HGRN_CHUNK + 1, sl]
                    dbt_rows[n] = jnp.sum(dst * st_parts[n], axis=0, keepdims=True) * decay
                    dst = dst * decay + wgt[:, n * LANES:(n + 1) * LANES]
                dst_scr[h] = dst
                dst_cat = jnp.concatenate(dst_parts, axis=1)
                dqs = _fold(_dot(do_h, st_cat), block)
                dkb = _fold(_dot(v_h, dst_cat), block)
                dv_state = _dot_nt(kb_cat, dst_cat)
                a = jnp.where(tril, _dot_nt(qi_b, ki_b), 0.0).astype(BF16)
                da = jnp.where(tril, _dot_nt(do_h, v_h), 0.0).astype(BF16)
                dv_parts.append(_dot_tn(a, do_h) + dv_state)
                dqi = _dot(da, ki_b)
                dki = _dot_tn(da, qi_b)
                dqf_parts.append(dqi * p["e_iq"][:, sl] + dqs * p["e_b"][:, sl])
                dkk_parts.append(dki * p["e_ik"][:, sl] + dkb * p["e_bb"][:, sl])
                dkbkb = dkb * kb
                db_parts.append(dqi * qi - dki * ki + dqs * qs - dkbkb)
                dkbkb_parts.append(dkbkb)
                dbt_parts.append(_chunk_rows(dbt_rows))

            cat = lambda parts: jnp.concatenate(parts, axis=1)
            dlogf = (_chunk_cumsum(cat(db_parts), reverse=True)
                     + _chunk_row(_chunk_cumsum(cat(dkbkb_parts)), HGRN_CHUNK - 1) + cat(dbt_parts))
            sq, qb = p["sq"], hg_v[:, :w]
            dqb = cat(dqf_parts) * (sq * (1.0 + qb * (1.0 - sq)))
            df = dlogf / p["f"] - cat(dkk_parts)
            sg, lb = p["sg"], p["lb"]
            dfb = df * (1.0 - lb) * sg * (1.0 - sg)
            acc_ref[1:2, :] += jnp.sum(df * (1.0 - sg), axis=0, keepdims=True)
            dhg_ref[rows, :] = jnp.concatenate([dqb, dfb, cat(dv_parts), dgb], axis=1).astype(BF16)
        if n_s:
            pl.when(step == n_steps - 1)(finish)

    rev = lambda i: (n_steps - 1 - i, 0)
    anywhere = pl.BlockSpec(memory_space=pl.ANY)
    return pl.pallas_call(
        body, name="hgrn_bwd", grid=(n_steps,),
        in_specs=[pl.BlockSpec((step_rows, 4 * w), rev), _full((2, w)), _full((1, w)),
                  pl.BlockSpec((step_rows, w), rev),
                  pl.BlockSpec((HGRN_STEP_BLOCKS, HGRN_HEADS, LANES, LANES), lambda i: (n_steps - 1 - i, 0, 0, 0)),
                  pl.BlockSpec((step_rows, w), rev)] + [anywhere] * n_s,
        out_specs=[pl.BlockSpec((step_rows, 4 * w), rev), _full((8, w))] + [anywhere] * n_s,
        out_shape=[jax.ShapeDtypeStruct((seq, 4 * w), BF16), jax.ShapeDtypeStruct((8, w), F32)]
        + [jax.ShapeDtypeStruct(s.shape, s.dtype) for s in chip_sums],
        scratch_shapes=[pltpu.VMEM((HGRN_HEADS, LANES, LANES), F32)] + (_chip_exchange_scratch(n_s) if n_s else []),
        compiler_params=_cparams("arbitrary"),
    )(hg, lb_logits, hnw, o_pre, st0, dyb, *chip_sums)


def in_bwd(dq, dk, dv, dhg, cos_t, sin_t, w_in4, x, norm1_w, dh1):
    seq = x.shape[0]
    tm = WIDE_ROW_TILE
    cw = w_in4.shape[2]

    def body(dq_ref, dk_ref, dv_ref, dhg_ref, cos_ref, sin_ref, w_ref,
             x_ref, nw_ref, dh1_ref, dproj_ref, dx_ref, acc_ref):
        @pl.when(pl.program_id(0) == 0)
        def _():
            acc_ref[...] = jnp.zeros_like(acc_ref)

        cos, sin = cos_ref[...], sin_ref[...]
        dqa = _rotary_bwd(dq_ref[...], cos, sin)
        dka = _rotary_bwd(dk_ref[...], cos, sin)
        dproj = jnp.concatenate(
            [jnp.concatenate([dqa, dka, dv_ref[...]], axis=1).astype(BF16), dhg_ref[...]], axis=1)
        dproj_ref[...] = dproj
        du = _dot_nt(dproj[:, :cw], w_ref[0])
        for j in range(1, N_CHIPS):
            du = du + _dot_nt(dproj[:, j * cw:(j + 1) * cw], w_ref[j])
        xv = x_ref[...]
        r1 = _rms(xv)
        nx = xv * r1
        acc_ref[0:1, :] += jnp.sum(du * nx, axis=0, keepdims=True)
        dn = du * nw_ref[...]
        dx_ref[...] = dh1_ref[...] + r1 * (dn - nx * jnp.mean(dn * nx, axis=-1, keepdims=True))

    half = _rows(tm, ATTN_WIDTH)
    wide = _rows(tm, D_MODEL)
    return pl.pallas_call(
        body, name="in_bwd", grid=(seq // tm,),
        in_specs=[half] * 3 + [_rows(tm, 4 * HGRN_WIDTH), _rows(tm, LANES), _rows(tm, LANES),
                               _weight((N_CHIPS, D_MODEL, cw)), wide, _full((1, D_MODEL)), wide],
        out_specs=[_rows(tm, IN_PROJ_WIDTH), wide, _full((8, D_MODEL))],
        out_shape=[jax.ShapeDtypeStruct((seq, IN_PROJ_WIDTH), BF16), jax.ShapeDtypeStruct((seq, D_MODEL), F32),
                   jax.ShapeDtypeStruct((8, D_MODEL), F32)],
        compiler_params=_cparams("arbitrary"),
    )(dq, dk, dv, dhg, cos_t, sin_t, w_in4, x, norm1_w, dh1)


def weight_grad(a, b, col_block, name, group=1, small_pack=None, rows_per_step=1024):
    seq, kdim = a.shape
    ndim = b.shape[1]
    nj = ndim // col_block
    tk = min(rows_per_step, seq)
    hosting = small_pack is not None
    n_j, n_t = nj // group, seq // tk

    def body(*refs):
        a_ref, b_ref = refs[:2]
        o_ref = refs[3] if hosting else refs[2]
        if hosting:
            start, finish = _pack_gather_phases(refs[2], refs[4], *refs[5:])
            pl.when((pl.program_id(0) == 0) & (pl.program_id(1) == 0))(start)

        @pl.when(pl.program_id(1) == 0)
        def _():
            o_ref[...] = jnp.zeros_like(o_ref)

        acc = _dot_tn(a_ref[...].astype(BF16), b_ref[...].astype(BF16))
        for i in range(group):
            o_ref[i] += acc[:, i * col_block:(i + 1) * col_block]
        if hosting:
            pl.when((pl.program_id(0) == n_j - 1) & (pl.program_id(1) == n_t - 1))(finish)

    anywhere = pl.BlockSpec(memory_space=pl.ANY)
    out = pl.pallas_call(
        body, name=name, grid=(n_j, n_t),
        in_specs=[pl.BlockSpec((tk, kdim), lambda j, t: (t, 0)),
                  pl.BlockSpec((tk, group * col_block), lambda j, t: (t, j))] + [anywhere] * hosting,
        out_specs=[pl.BlockSpec((group, kdim, col_block), lambda j, t: (j, 0, 0))] + [anywhere] * hosting,
        out_shape=[jax.ShapeDtypeStruct((nj, kdim, col_block), F32)]
        + ([jax.ShapeDtypeStruct((N_DEV,) + small_pack.shape, F32)] if hosting else []),
        scratch_shapes=[pltpu.SemaphoreType.DMA((N_DEV - 1,)), pltpu.SemaphoreType.DMA((N_DEV - 1,)),
                        pltpu.SemaphoreType.DMA] if hosting else [],
        compiler_params=_cparams("arbitrary", "arbitrary"),
    )(a, b, *([small_pack] if hosting else []))
    return out if hosting else out[0]


def _sibling_exchange_phases(g_refs, out_refs, send_sems, recv_sems):
    x, y, cc = _mesh_pos()

    def copies():
        return [pltpu.make_async_remote_copy(
            src_ref=g_refs[i].at[j, 1 - cc], dst_ref=out_refs[i].at[j],
            send_sem=send_sems.at[i * N_CHIPS + j], recv_sem=recv_sems.at[i * N_CHIPS + j],
            device_id=(x, y, 1 - cc), device_id_type=MESH_ID)
            for i in range(len(g_refs)) for j in range(N_CHIPS)]

    def start():
        for cp in copies():
            cp.start()

    def finish():
        for cp in copies():
            cp.wait_recv()
        for cp in copies():
            cp.wait_send()

    return start, finish


def _sibling_exchange_scratch(n):
    return [pltpu.SemaphoreType.DMA((n * N_CHIPS,)), pltpu.SemaphoreType.DMA((n * N_CHIPS,))]


def _sibling_exchange_shapes(grads):
    return [jax.ShapeDtypeStruct((N_CHIPS,) + g.shape[2:], g.dtype) for g in grads]


def exchange_with_sibling(grads, name):
    n = len(grads)

    def body(*refs):
        start, finish = _sibling_exchange_phases(refs[:n], refs[n:2 * n], refs[2 * n], refs[2 * n + 1])
        start()
        finish()

    return pl.pallas_call(
        body, name=name,
        in_specs=[pl.BlockSpec(memory_space=pl.ANY)] * n,
        out_specs=[pl.BlockSpec(memory_space=pl.ANY)] * n,
        out_shape=_sibling_exchange_shapes(grads),
        scratch_shapes=_sibling_exchange_scratch(n),
    )(*grads)


def add_own_half(grad, recv, name):
    _, _, r, c = grad.shape
    tr = r // 2 if r % 32 == 0 else r

    def body(cc_ref, g_ref, r_ref, o_ref):
        o_ref[...] = (g_ref[...] + r_ref[...]).astype(BF16)

    grid_spec = pltpu.PrefetchScalarGridSpec(
        num_scalar_prefetch=1, grid=(N_CHIPS, r // tr),
        in_specs=[pl.BlockSpec((None, None, tr, c), lambda j, t, cc: (j, cc[0], t, 0)),
                  pl.BlockSpec((None, tr, c), lambda j, t, cc: (j, t, 0))],
        out_specs=pl.BlockSpec((None, tr, c), lambda j, t, cc: (j, t, 0)))
    cc = lax.axis_index("c").astype(jnp.int32).reshape(1)
    return pl.pallas_call(
        body, name=name, grid_spec=grid_spec,
        out_shape=jax.ShapeDtypeStruct((N_CHIPS, r, c), BF16),
        compiler_params=_cparams("parallel", "parallel"),
    )(cc, grad, recv)


def _chip_exchange_phases(s_refs, out_refs, send_sems, recv_sems):
    n = len(s_refs)
    x, y, cc = _mesh_pos()
    my_chip = 2 * x + y
    chips = [(1 - x, y), (x, 1 - y), (1 - x, 1 - y)]

    def outgoing():
        return [pltpu.make_async_remote_copy(
            src_ref=s_refs[i].at[2 * px + py], dst_ref=out_refs[i].at[my_chip],
            send_sem=send_sems.at[3 * i + j], recv_sem=recv_sems.at[3 * i + j],
            device_id=(px, py, cc), device_id_type=MESH_ID)
            for i in range(n) for j, (px, py) in enumerate(chips)]

    def start():
        for cp in outgoing():
            cp.start()

    def finish():
        for i in range(n):
            for j, (px, py) in enumerate(chips):
                pltpu.make_async_remote_copy(
                    src_ref=s_refs[i].at[my_chip], dst_ref=out_refs[i].at[2 * px + py],
                    send_sem=send_sems.at[3 * i + j], recv_sem=recv_sems.at[3 * i + j],
                    device_id=(px, py, cc), device_id_type=MESH_ID).wait_recv()
        for cp in outgoing():
            cp.wait_send()

    return start, finish


def _chip_exchange_scratch(n):
    return [pltpu.SemaphoreType.DMA((3 * n,)), pltpu.SemaphoreType.DMA((3 * n,))]


def sum_chips(sums, parts, name):
    _, r, c = parts.shape
    tr = r // 2 if r % 32 == 0 else r

    def body(idx_ref, s_ref, p1_ref, p2_ref, p3_ref, o_ref):
        o_ref[...] = ((s_ref[...].astype(F32) + p1_ref[...].astype(F32))
                      + p2_ref[...].astype(F32)) + p3_ref[...].astype(F32)

    def pick(k):
        return pl.BlockSpec((None, tr, c), lambda t, idx: (idx[k], t, 0))

    x, y = lax.axis_index("x"), lax.axis_index("y")
    idx = jnp.stack([2 * x + y, 2 * (1 - x) + y, 2 * x + (1 - y), 2 * (1 - x) + (1 - y)]).astype(jnp.int32)
    grid_spec = pltpu.PrefetchScalarGridSpec(
        num_scalar_prefetch=1, grid=(r // tr,),
        in_specs=[pick(0), pick(1), pick(2), pick(3)],
        out_specs=pl.BlockSpec((tr, c), lambda t, idx: (t, 0)))
    return pl.pallas_call(
        body, name=name, grid_spec=grid_spec,
        out_shape=jax.ShapeDtypeStruct((r, c), F32),
        compiler_params=_cparams("parallel"),
    )(idx, sums, parts, parts, parts)


def share_with_sibling(halves, name):
    n = len(halves)

    def body(*refs):
        h_refs, out_refs = refs[:n], refs[n:2 * n]
        send_sems, recv_sems = refs[2 * n], refs[2 * n + 1]
        x, y, cc = _mesh_pos()
        copies = [pltpu.make_async_remote_copy(
            src_ref=h_refs[i], dst_ref=out_refs[i],
            send_sem=send_sems.at[i], recv_sem=recv_sems.at[i],
            device_id=(x, y, 1 - cc), device_id_type=MESH_ID) for i in range(n)]
        for cp in copies:
            cp.start()
        for cp in copies:
            cp.wait_recv()
        for cp in copies:
            cp.wait_send()

    return pl.pallas_call(
        body, name=name,
        in_specs=[pl.BlockSpec(memory_space=pl.ANY)] * n,
        out_specs=[pl.BlockSpec(memory_space=pl.ANY)] * n,
        out_shape=[jax.ShapeDtypeStruct(h.shape, h.dtype) for h in halves],
        scratch_shapes=[pltpu.SemaphoreType.DMA((n,)), pltpu.SemaphoreType.DMA((n,))],
    )(*halves)


def _adam_update(w, g, m, v):
    m = ADAM_B1 * m + (1.0 - ADAM_B1) * g
    v = ADAM_B2 * v + (1.0 - ADAM_B2) * (g * g)
    m_hat = m / (1.0 - ADAM_B1 ** ADAM_STEP)
    v_hat = v / (1.0 - ADAM_B2 ** ADAM_STEP)
    delta = -ADAM_LR * (m_hat / (jnp.sqrt(v_hat) + ADAM_EPS) + ADAM_WD * w)
    return delta, m, v


ADAMW_STEPS = 8


def adamw(ws, g_mine, g_sibling, ms, vs, name, chip_sums=()):
    n, n_s = len(ws), len(chip_sums)
    per_half = ADAMW_STEPS // 2

    def body(*refs):
        cc_ref = refs[0]
        ins = refs[1:1 + 5 * n]
        outs = refs[1 + 5 * n + n_s:1 + 9 * n + n_s]
        step = pl.program_id(0)
        if n_s:
            start, finish = _chip_exchange_phases(
                refs[1 + 5 * n:1 + 5 * n + n_s], refs[1 + 9 * n + n_s:1 + 9 * n + 2 * n_s],
                *refs[1 + 9 * n + 2 * n_s:])
            pl.when(step == 0)(start)
        mine = (step // per_half) == cc_ref[0]
        for i in range(n):
            w_ref, ga_ref, gb_ref, m_ref, v_ref = ins[5 * i:5 * i + 5]
            g_ref, d_ref, nm_ref, nv_ref = outs[4 * i:4 * i + 4]
            g = jnp.where(mine, ga_ref[...], gb_ref[...])
            g_ref[...] = g
            d, nm, nv = _adam_update(w_ref[...], g, m_ref[...], v_ref[...])
            d_ref[...] = d
            nm_ref[...] = nm
            nv_ref[...] = nv
        if n_s:
            pl.when(step == ADAMW_STEPS - 1)(finish)

    in_specs, out_specs, out_shape, operands = [], [], [], []
    for w, ga, gb, m, v in zip(ws, g_mine, g_sibling, ms, vs):
        r, c = w.shape
        tr = r // ADAMW_STEPS
        full = pl.BlockSpec((tr, c), lambda t, cc: (t, 0))
        part = pl.BlockSpec((tr, c), lambda t, cc: (t % per_half, 0))
        in_specs += [full, part, part, full, full]
        out_specs += [full] * 4
        out_shape += [jax.ShapeDtypeStruct((r, c), F32)] * 4
        operands += [w, ga, gb, m, v]
    anywhere = pl.BlockSpec(memory_space=pl.ANY)
    grid_spec = pltpu.PrefetchScalarGridSpec(
        num_scalar_prefetch=1, grid=(ADAMW_STEPS,),
        in_specs=in_specs + [anywhere] * n_s, out_specs=out_specs + [anywhere] * n_s,
        scratch_shapes=_chip_exchange_scratch(n_s) if n_s else [])
    cc = lax.axis_index("c").astype(jnp.int32).reshape(1)
    res = pl.pallas_call(
        body, name=name, grid_spec=grid_spec,
        out_shape=out_shape + [jax.ShapeDtypeStruct(s.shape, s.dtype) for s in chip_sums],
        compiler_params=_cparams("arbitrary"),
    )(cc, *operands, *chip_sums)
    per_shard = [tuple(res[4 * i:4 * i + 4]) for i in range(n)]
    return (per_shard, list(res[4 * n:])) if n_s else per_shard


def _pack_gather_phases(p_ref, out_ref, send_sems, recv_sems, local_sem):
    x, y, cc = _mesh_pos()
    me = 4 * x + 2 * y + cc
    flips = [(fx, fy, fc) for fx in (0, 1) for fy in (0, 1) for fc in (0, 1)][1:]

    def copy(k, row):
        fx, fy, fc = flips[k]
        return pltpu.make_async_remote_copy(
            src_ref=p_ref, dst_ref=out_ref.at[row],
            send_sem=send_sems.at[k], recv_sem=recv_sems.at[k],
            device_id=(x ^ fx, y ^ fy, cc ^ fc), device_id_type=MESH_ID)

    def local():
        return pltpu.make_async_copy(p_ref, out_ref.at[me], local_sem)

    def start():
        local().start()
        for k in range(len(flips)):
            copy(k, me).start()

    def finish():
        for k, (fx, fy, fc) in enumerate(flips):
            copy(k, 4 * (x ^ fx) + 2 * (y ^ fy) + (cc ^ fc)).wait_recv()
        for k in range(len(flips)):
            copy(k, me).wait_send()
        local().wait()

    return start, finish


def small_update(gathered, wpack, mpack, vpack):
    hw = HGRN_WIDTH

    def body(g_ref, w_ref, m_ref, v_ref, go_ref, d_ref, nm_ref, nv_ref, loss_ref):
        g = g_ref[0]
        for d in range(1, N_DEV):
            g = g + g_ref[d]
        wv = w_ref[...]
        a0, a1 = wv[4:5, :hw], wv[4:5, hw:]
        mx = jnp.maximum(a0, a1)
        e0, e1 = jnp.exp(a0 - mx), jnp.exp(a1 - mx)
        lb = e0 / (e0 + e1)
        dl = g[4:5, :hw] * lb * (1.0 - lb)
        row = lax.broadcasted_iota(jnp.int32, g.shape, 0)
        lb_row = jnp.concatenate([dl, -dl], axis=1)
        grads = jnp.where(row == 4, lb_row, jnp.where(row < 4, g, 0.0))
        go_ref[...] = grads
        d, nm, nv = _adam_update(wv, grads, m_ref[...], v_ref[...])
        d_ref[...] = d
        nm_ref[...] = nm
        nv_ref[...] = nv
        loss_ref[...] = jnp.zeros((8, LANES), F32) + jnp.sum(g[5:6, :])

    vm = pl.BlockSpec(memory_space=pltpu.VMEM)
    return pl.pallas_call(
        body, name="small_update",
        in_specs=[vm] * 4, out_specs=[vm] * 5,
        out_shape=[jax.ShapeDtypeStruct(wpack.shape, F32)] * 4 + [jax.ShapeDtypeStruct((8, LANES), F32)],
    )(gathered, wpack, mpack, vpack)


def _pack_small(n1, n2, fn, hn, lbl):
    z = jnp.zeros((1, D_MODEL - HGRN_WIDTH), F32)
    rows = [n1.reshape(1, D_MODEL), n2.reshape(1, D_MODEL), fn.reshape(1, D_MODEL),
            jnp.concatenate([hn.reshape(1, HGRN_WIDTH), z], axis=1), lbl.reshape(1, 2 * HGRN_WIDTH),
            jnp.zeros((3, D_MODEL), F32)]
    return jnp.concatenate(rows, axis=0)


def _unpack_small(pack):
    return (pack[0:1], pack[4].reshape(2, HGRN_WIDTH), pack[3:4, :HGRN_WIDTH], pack[1:2], pack[2])


def kernel(x, norm1_w, w_in, lb_logits, hgrn_norm_w, w_out, norm2_w, w_gate_up, w_down, final_norm_w, loss_target, m_norm1_w, m_w_in, m_lb_logits, m_hgrn_norm_w, m_w_out, m_norm2_w, m_w_gate_up, m_w_down, m_final_norm_w, v_norm1_w, v_w_in, v_lb_logits, v_hgrn_norm_w, v_w_out, v_norm2_w, v_w_gate_up, v_w_down, v_final_norm_w):
    seq = x.shape[1]
    xs = x.reshape(seq, D_MODEL)
    target = loss_target.reshape(seq, D_MODEL)
    shards = {"w_in": w_in[0], "w_out": w_out[0], "w_gu": w_gate_up[0], "w_down": w_down[0]}

    cast = {k: cast_bf16(w, "cast_" + k) for k, w in shards.items()}
    w_in4 = allgather_halves(cast["w_in"], "gather_w_in").reshape(N_CHIPS, D_MODEL, -1)

    cos_t, sin_t = _rope_tables(seq)
    fw = final_norm_w.reshape(1, D_MODEL)

    qr, kr, va, hg, u, q16, k16, v16, g_out, g_down = in_proj(
        xs, norm1_w, w_in4, cos_t, sin_t, [cast["w_out"], cast["w_down"]])
    ya, lse = attn_fwd(qr, kr, va, q16, k16, v16)
    yb, o_pre, st0, g_gu = hgrn_fwd(hg, lb_logits, hgrn_norm_w, [cast["w_gu"]])
    w_out_f = g_out.reshape(D_MODEL, D_MODEL)
    w_gu4 = g_gu.reshape(N_CHIPS, D_MODEL, -1)
    w_down_f = g_down.reshape(FFN_HIDDEN, D_MODEL)
    mixed, h1, u2, g, up, act, dh2, acc_fin = ffn_fwd(
        ya, yb, xs, w_out_f, norm2_w, w_gu4, w_down_f, fw, target)

    cw_in, cw_gu = w_in4.shape[2], w_gu4.shape[2]
    dgu, dh1, dya, dyb, delta, acc_n2 = ffn_bwd(dh2, w_down_f, g, up, w_gu4, h1, norm2_w, w_out_f, ya)
    early = [
        weight_grad(mixed, dh1, D_MODEL, "wgrad_out", rows_per_step=2048).reshape(N_CHIPS, 2, D_MODEL // 8, D_MODEL),
        weight_grad(u2, dgu, cw_gu, "wgrad_gu", group=2).reshape(N_CHIPS, 2, D_MODEL // 2, cw_gu),
        weight_grad(act, dh2, D_MODEL, "wgrad_down").reshape(N_CHIPS, 2, FFN_HIDDEN // 8, D_MODEL),
    ]
    early_names = ["out", "gu", "down"]
    dq, dk, dv, *early_recv = attn_bwd(qr, kr, va, q16, k16, v16, dya, lse, delta, early)
    early_sums = [add_own_half(gr, rc, "add_half_" + nm) for gr, rc, nm in zip(early, early_recv, early_names)]
    dhg, acc_hg, *early_parts = hgrn_bwd(hg, lb_logits, hgrn_norm_w, o_pre, st0, dyb, early_sums)
    dproj, dx, acc_n1 = in_bwd(dq, dk, dv, dhg, cos_t, sin_t, w_in4, xs, norm1_w, dh1)
    z512 = jnp.zeros((1, D_MODEL - HGRN_WIDTH), F32)
    gpack = jnp.concatenate([
        acc_n1[0:1], acc_n2[0:1], acc_fin[0:1],
        jnp.concatenate([acc_hg[0:1], z512], axis=1), jnp.concatenate([acc_hg[1:2], z512], axis=1),
        acc_fin[1:2], jnp.zeros((2, D_MODEL), F32)], axis=0)
    g_in, gathered_packs = weight_grad(u, dproj, cw_in, "wgrad_in", group=2, small_pack=gpack, rows_per_step=2048)
    late = [g_in.reshape(N_CHIPS, 2, D_MODEL // 2, cw_in)]
    late_recv = exchange_with_sibling(late, "grad_exchange_sibling_late")
    late_sums = [add_own_half(late[0], late_recv[0], "add_half_in")]

    early_halves = [sum_chips(s, p, "sum_chips_" + nm) for s, p, nm in zip(early_sums, early_parts, early_names)]
    early_others = share_with_sibling(early_halves, "grad_share_sibling_early")
    early_keys = ["w_out", "w_gu", "w_down"]
    moments = {"w_in": (m_w_in, v_w_in), "w_out": (m_w_out, v_w_out),
               "w_gu": (m_w_gate_up, v_w_gate_up), "w_down": (m_w_down, v_w_down)}
    early_updates, late_parts = adamw(
        [shards[k] for k in early_keys], early_halves, early_others,
        [moments[k][0][0] for k in early_keys], [moments[k][1][0] for k in early_keys],
        "adamw_early", chip_sums=late_sums)
    late_halves = [sum_chips(late_sums[0], late_parts[0], "sum_chips_in")]
    late_others = share_with_sibling(late_halves, "grad_share_sibling_late")
    late_updates = adamw([shards["w_in"]], late_halves, late_others,
                         [moments["w_in"][0][0]], [moments["w_in"][1][0]], "adamw_in")
    big = {k: tuple(t[None] for t in upd) for k, upd in zip(early_keys + ["w_in"], early_updates + late_updates)}

    wpack = _pack_small(norm1_w, norm2_w, final_norm_w, hgrn_norm_w, lb_logits)
    mpack = _pack_small(m_norm1_w, m_norm2_w, m_final_norm_w, m_hgrn_norm_w, m_lb_logits)
    vpack = _pack_small(v_norm1_w, v_norm2_w, v_final_norm_w, v_hgrn_norm_w, v_lb_logits)
    gs, ds, nms, nvs, loss8 = small_update(gathered_packs, wpack, mpack, vpack)
    loss = loss8[0, 0]

    def assemble(small_pack, idx):
        n1, lbl, hn, n2, fn = _unpack_small(small_pack)
        return (n1, big["w_in"][idx], lbl, hn, big["w_out"][idx], n2, big["w_gu"][idx], big["w_down"][idx], fn)

    return (loss, dx.reshape(x.shape), *assemble(gs, 0), *assemble(ds, 1), *assemble(nms, 2), *assemble(nvs, 3))
```

```python
import functools

import jax
import jax.numpy as jnp
from jax import lax
from jax.experimental import pallas as pl
from jax.experimental.pallas import tpu as pltpu

F32 = jnp.float32
BF16 = jnp.bfloat16

D_MODEL = 1024
ATTN_WIDTH = 512
HEAD_DIM = 64
DILATED_PAIRS = ((128, 1), (512, 4), (2048, 16))
ATTN_BLOCK = 128
ROPE_THETA = 10000.0
HGRN_WIDTH = 512
HGRN_CHUNK = 16
HGRN_HEADS = 4
IN_PROJ_WIDTH = 3584
FFN_HIDDEN = 2816
NORM_EPS = 1e-6
ATTN_SCALE = HEAD_DIM ** -0.5
N_CHIPS = 4
N_DEV = 8

ADAM_LR = 0.001
ADAM_B1 = 0.9
ADAM_B2 = 0.999
ADAM_EPS = 1e-08
ADAM_WD = 0.01
ADAM_STEP = 10

LANES = 128
HGRN_ROWS = 128
HGRN_STEP_BLOCKS = 2
ROW_TILE = 256
WIDE_ROW_TILE = 512
ATTN_STEP_ROWS = 2048
ATTN_FWD_UNROLL = 16
ATTN_BWD_UNROLL = 16
ATTN_MAJOR_DILATION = 16
VMEM_LIMIT = 56 * 1024 * 1024
NEG_BIG = -1e30
MESH_ID = pl.DeviceIdType.MESH


def _cparams(*sem):
    return pltpu.CompilerParams(dimension_semantics=tuple(sem), vmem_limit_bytes=VMEM_LIMIT)


def _dot(a, b):
    return jnp.dot(a, b, preferred_element_type=F32)


def _dot_nt(a, b):
    return lax.dot_general(a, b, (((1,), (1,)), ((), ())), preferred_element_type=F32)


def _dot_tn(a, b):
    return lax.dot_general(a, b, (((0,), (0,)), ((), ())), preferred_element_type=F32)


def _sigmoid(x):
    return 1.0 / (1.0 + jnp.exp(-x))


def _full(shape):
    n = len(shape)
    return pl.BlockSpec(shape, lambda *_: (0,) * n)


def _weight(shape):
    n = len(shape)
    return pl.BlockSpec(shape, lambda *_: (0,) * n, pipeline_mode=pl.Buffered(1))


def _rows(tm, width):
    return pl.BlockSpec((tm, width), lambda i: (i, 0))


def _swap32(x):
    lane = lax.broadcasted_iota(jnp.int32, x.shape, 1)
    first = (lane % HEAD_DIM) < (HEAD_DIM // 2)
    return jnp.where(first, pltpu.roll(x, LANES - 32, axis=1), pltpu.roll(x, 32, axis=1))


def _rotary_fwd(x, cos, sin_signed):
    parts = []
    for j in range(x.shape[1] // LANES):
        xc = x[:, j * LANES:(j + 1) * LANES]
        parts.append(xc * cos + _swap32(xc) * sin_signed)
    return jnp.concatenate(parts, axis=1)


def _rotary_bwd(dy, cos, sin_signed):
    parts = []
    for j in range(dy.shape[1] // LANES):
        dc = dy[:, j * LANES:(j + 1) * LANES]
        parts.append(dc * cos + _swap32(dc * sin_signed))
    return jnp.concatenate(parts, axis=1)


def _rope_tables(seq):
    half = HEAD_DIM // 2
    inv_freq = ROPE_THETA ** (-jnp.arange(half, dtype=F32) / half)
    ang = jnp.arange(seq, dtype=F32)[:, None] * inv_freq[None, :]
    cos, sin = jnp.cos(ang), jnp.sin(ang)
    cos_t = jnp.tile(cos, (1, LANES // half))
    sin_t = jnp.tile(jnp.concatenate([-sin, sin], axis=1), (1, LANES // HEAD_DIM))
    return cos_t, sin_t


def cast_bf16(w, name):
    r, c = w.shape
    half = r // 2

    def body(w_ref, o_ref):
        o_ref[...] = w_ref[...].astype(BF16)

    return pl.pallas_call(
        body, name=name, grid=(2,),
        in_specs=[pl.BlockSpec((half, c), lambda i: (i, 0))],
        out_specs=pl.BlockSpec((None, half, c), lambda i: (i, 0, 0)),
        out_shape=jax.ShapeDtypeStruct((2, half, c), BF16),
        compiler_params=_cparams("parallel"),
    )(w)


def _mesh_pos():
    return lax.axis_index("x"), lax.axis_index("y"), lax.axis_index("c")


GATHER_COPIES = 7


def _gather_phases(x_refs, out_refs, send_sems, recv_sems, local_sems):
    n = len(x_refs)
    x, y, cc = _mesh_pos()
    me, sibling = (x, y, cc), (x, y, 1 - cc)
    chips = [(1 - x, y), (x, 1 - y), (1 - x, 1 - y)]

    def rows(i, px, py, pc):
        return out_refs[i].at[4 * px + 2 * py + pc]

    def copy(i, k, block, to, src=None):
        return pltpu.make_async_remote_copy(
            src_ref=rows(i, *block) if src is None else src, dst_ref=rows(i, *block),
            send_sem=send_sems.at[GATHER_COPIES * i + k], recv_sem=recv_sems.at[GATHER_COPIES * i + k],
            device_id=to, device_id_type=MESH_ID)

    def local(i):
        return pltpu.make_async_copy(x_refs[i].at[cc], rows(i, *me), local_sems.at[i])

    def first(i):
        mine = x_refs[i].at[cc]
        return [copy(i, 0, me, sibling, src=mine)] + [
            copy(i, 1 + j, me, (*chip, cc), src=mine) for j, chip in enumerate(chips)]

    def passed(i):
        return [copy(i, 4 + j, (*chip, cc), sibling) for j, chip in enumerate(chips)]

    def start():
        for i in range(n):
            local(i).start()
            for cp in first(i):
                cp.start()

    def forward():
        for i in range(n):
            onward = passed(i)
            for j, chip in enumerate(chips):
                copy(i, 1 + j, (*chip, cc), me).wait_recv()
                onward[j].start()

    def finish():
        for i in range(n):
            copy(i, 0, sibling, me).wait_recv()
            for j, chip in enumerate(chips):
                copy(i, 4 + j, (*chip, 1 - cc), me).wait_recv()
            for cp in first(i) + passed(i):
                cp.wait_send()
            local(i).wait()

    return start, forward, finish


def _gather_scratch(n):
    return [pltpu.SemaphoreType.DMA((GATHER_COPIES * n,)), pltpu.SemaphoreType.DMA((GATHER_COPIES * n,)),
            pltpu.SemaphoreType.DMA((n,))]


def _gathered_shape(halves):
    return jax.ShapeDtypeStruct((N_DEV,) + halves.shape[1:], halves.dtype)


def allgather_halves(halves, name):
    def body(x_ref, out_ref, send_sems, recv_sems, local_sems):
        start, forward, finish = _gather_phases([x_ref], [out_ref], send_sems, recv_sems, local_sems)
        start()
        forward()
        finish()

    return pl.pallas_call(
        body, name=name,
        in_specs=[pl.BlockSpec(memory_space=pl.ANY)],
        out_specs=pl.BlockSpec(memory_space=pl.ANY),
        out_shape=_gathered_shape(halves),
        scratch_shapes=_gather_scratch(1),
    )(halves)


def _rms(x):
    return lax.rsqrt(jnp.mean(x * x, axis=-1, keepdims=True) + NORM_EPS)


def in_proj(x, norm1_w, w_in4, cos_t, sin_t, weight_halves=()):
    seq = x.shape[0]
    tm = WIDE_ROW_TILE
    cw = w_in4.shape[2]
    n_w = len(weight_halves)
    steps = seq // tm
    major = ATTN_MAJOR_DILATION
    slabs = ATTN_WIDTH // LANES

    def body(*refs):
        x_ref, nw_ref, w_ref, cos_ref, sin_ref = refs[:5]
        q_ref, k_ref, v_ref, hg_ref, u_ref = refs[5 + n_w:10 + n_w]
        major_refs = refs[10 + n_w:13 + n_w]
        slab_scr = refs[13 + 2 * n_w]
        step = pl.program_id(0)
        if n_w:
            start, forward, finish = _gather_phases(
                refs[5:5 + n_w], refs[13 + n_w:13 + 2 * n_w], *refs[14 + 2 * n_w:])
            pl.when(step == 0)(start)
            pl.when(step == (3 * steps) // 4)(forward)
        xv = x_ref[...]
        u = ((xv * _rms(xv)) * nw_ref[...]).astype(BF16)
        u_ref[...] = u
        proj = jnp.concatenate([_dot(u, w_ref[j]) for j in range(N_CHIPS)], axis=1)
        cos, sin = cos_ref[...], sin_ref[...]
        a = ATTN_WIDTH
        qkv = (_rotary_fwd(proj[:, :a], cos, sin), _rotary_fwd(proj[:, a:2 * a], cos, sin), proj[:, 2 * a:3 * a])
        for ref, val in zip((q_ref, k_ref, v_ref), qkv):
            ref[...] = val
        hg_ref[...] = proj[:, 3 * a:]
        for idx, val in enumerate(qkv):
            for s in range(slabs):
                slab_scr[idx, s] = val[:, s * LANES:(s + 1) * LANES]
        for idx, out in enumerate(major_refs):
            for r in range(major):
                for s in range(slabs):
                    out[r, :, s * LANES:(s + 1) * LANES] = (
                        slab_scr.at[idx, s][pl.ds(r, tm // major, stride=major), :].astype(BF16))
        if n_w:
            pl.when(step == steps - 1)(finish)

    anywhere = pl.BlockSpec(memory_space=pl.ANY)
    major_spec = pl.BlockSpec((major, tm // major, ATTN_WIDTH), lambda i: (0, i, 0))
    return pl.pallas_call(
        body, name="in_proj", grid=(steps,),
        in_specs=[_rows(tm, D_MODEL), _full((1, D_MODEL)), _weight((N_CHIPS, D_MODEL, cw)),
                  _rows(tm, LANES), _rows(tm, LANES)] + [anywhere] * n_w,
        out_specs=[_rows(tm, ATTN_WIDTH)] * 3 + [_rows(tm, 4 * HGRN_WIDTH), _rows(tm, D_MODEL)]
        + [major_spec] * 3 + [anywhere] * n_w,
        out_shape=[jax.ShapeDtypeStruct((seq, ATTN_WIDTH), F32)] * 3
        + [jax.ShapeDtypeStruct((seq, 4 * HGRN_WIDTH), F32), jax.ShapeDtypeStruct((seq, D_MODEL), BF16)]
        + [jax.ShapeDtypeStruct((major, seq // major, ATTN_WIDTH), BF16)] * 3
        + [_gathered_shape(h) for h in weight_halves],
        scratch_shapes=[pltpu.VMEM((3, slabs, tm, LANES), F32)] + (_gather_scratch(n_w) if n_w else []),
        compiler_params=_cparams("arbitrary"),
    )(x, norm1_w, w_in4, cos_t, sin_t, *weight_halves)


def _head_masks():
    lane = lax.broadcasted_iota(jnp.int32, (1, LANES), 1)
    return [(lane // HEAD_DIM) == h for h in range(LANES // HEAD_DIM)]


def _window_valid(no_prev):
    qi = lax.broadcasted_iota(jnp.int32, (ATTN_BLOCK, 2 * ATTN_BLOCK), 0)
    kj = lax.broadcasted_iota(jnp.int32, (ATTN_BLOCK, 2 * ATTN_BLOCK), 1)
    valid = (kj >= qi) & (kj <= qi + ATTN_BLOCK)
    return valid & (jnp.logical_not(no_prev) | (kj >= ATTN_BLOCK))


def _strided_rows(start, dilation):
    if dilation == 1:
        return pl.ds(start, ATTN_BLOCK)
    return pl.ds(start, ATTN_BLOCK, stride=dilation)


def _block_before(edge_ref, cur_ref, t, r, span, dilation, per_step):
    edge = edge_ref[_strided_rows(ATTN_STEP_ROWS - span + r, dilation), :]
    if per_step == 1:
        return edge
    inside = cur_ref[_strided_rows(r + span * jnp.maximum(t - 1, 0), dilation), :]
    return jnp.where(t == 0, edge, inside)


def _attn_specs():
    cur = pl.BlockSpec((ATTN_STEP_ROWS, LANES), lambda hp, j: (j, hp))
    prev = pl.BlockSpec((ATTN_STEP_ROWS, LANES), lambda hp, j: (jnp.maximum(j - 1, 0), hp))
    return cur, prev


def _for_each_block(dilation, unroll, block):
    span = ATTN_BLOCK * dilation
    per_step = ATTN_STEP_ROWS // span

    def trip(it, carry):
        block(it // dilation, it % dilation, span, per_step)
        return carry

    lax.fori_loop(0, per_step * dilation, trip, 0, unroll=unroll)


def _load_qkv(natural, major, t, r, span, dilation, per_step):
    if dilation == ATTN_MAJOR_DILATION:
        q_ref, kc_ref, vc_ref, kp_ref, vp_ref = major
        return (q_ref[r] * ATTN_SCALE, jnp.concatenate([kp_ref[r], kc_ref[r]], axis=0),
                jnp.concatenate([vp_ref[r], vc_ref[r]], axis=0))
    q_ref, kc_ref, vc_ref, kp_ref, vp_ref = natural
    rows = _strided_rows(r + span * t, dilation)
    kp = _block_before(kp_ref, kc_ref, t, r, span, dilation, per_step)
    vp = _block_before(vp_ref, vc_ref, t, r, span, dilation, per_step)
    return ((q_ref[rows, :] * ATTN_SCALE).astype(BF16),
            jnp.concatenate([kp, kc_ref[rows, :]], axis=0).astype(BF16),
            jnp.concatenate([vp, vc_ref[rows, :]], axis=0).astype(BF16))


def _major_specs():
    assert ATTN_STEP_ROWS == ATTN_BLOCK * ATTN_MAJOR_DILATION
    shape = (ATTN_MAJOR_DILATION, ATTN_BLOCK, LANES)
    return (pl.BlockSpec(shape, lambda hp, j: (0, j, hp)),
            pl.BlockSpec(shape, lambda hp, j: (0, jnp.maximum(j - 1, 0), hp)))


def attn_fwd(q, k, v, q16, k16, v16):
    seq = q.shape[0]
    cur, prev = _attn_specs()
    cur16, prev16 = _major_specs()

    def body(*refs):
        natural, major, (y_ref, lse_ref) = refs[:5], refs[5:10], refs[10:]
        first_step = pl.program_id(1) == 0
        masks = _head_masks()
        for index, (_, dilation) in enumerate(DILATED_PAIRS):
            def block(t, r, span, per_step, dilation=dilation, merge=index > 0):
                rows = _strided_rows(r + span * t, dilation)
                q2, k2, v2 = _load_qkv(natural, major, t, r, span, dilation, per_step)
                valid = _window_valid(first_step & (t == 0))
                o_acc = jnp.zeros((ATTN_BLOCK, LANES), F32)
                l_acc = jnp.zeros((ATTN_BLOCK, LANES), F32)
                for mh in masks:
                    qm = jnp.where(mh, q2, jnp.zeros_like(q2))
                    s = jnp.where(valid, _dot_nt(qm, k2), NEG_BIG)
                    m = jnp.max(s, axis=-1, keepdims=True)
                    p = jnp.exp(s - m)
                    l = jnp.sum(p, axis=-1, keepdims=True)
                    o = _dot(p.astype(BF16), v2) / l
                    o_acc = jnp.where(mh, o, o_acc)
                    l_acc = jnp.where(mh, m + jnp.log(l), l_acc)
                if merge:
                    y_old, l_old = y_ref[rows, :], lse_ref[rows, :]
                    mx = jnp.maximum(l_old, l_acc)
                    e_old, e_new = jnp.exp(l_old - mx), jnp.exp(l_acc - mx)
                    den = e_old + e_new
                    o_acc = (y_old * e_old + o_acc * e_new) / den
                    l_acc = mx + jnp.log(den)
                y_ref[rows, :] = o_acc
                lse_ref[rows, :] = l_acc

            _for_each_block(dilation, ATTN_FWD_UNROLL, block)

    return pl.pallas_call(
        body, name="attn_fwd", grid=(ATTN_WIDTH // LANES, seq // ATTN_STEP_ROWS),
        in_specs=[cur, cur, cur, prev, prev, cur16, cur16, cur16, prev16, prev16],
        out_specs=[cur, cur],
        out_shape=[jax.ShapeDtypeStruct((seq, ATTN_WIDTH), F32)] * 2,
        compiler_params=_cparams("parallel", "parallel"),
    )(q, k, v, k, v, q16, k16, v16, k16, v16)


def _chunk_cumsum(x, reverse=False):
    rc = lax.broadcasted_iota(jnp.int32, x.shape, 0) % HGRN_CHUNK
    sh = 1
    while sh < HGRN_CHUNK:
        if reverse:
            x = x + jnp.where(rc + sh < HGRN_CHUNK, pltpu.roll(x, x.shape[0] - sh, axis=0), 0.0)
        else:
            x = x + jnp.where(rc >= sh, pltpu.roll(x, sh, axis=0), 0.0)
        sh *= 2
    return x


def _chunk_row(x, row):
    return _chunk_rows([x[n * HGRN_CHUNK + row:n * HGRN_CHUNK + row + 1, :]
                        for n in range(x.shape[0] // HGRN_CHUNK)])


def _chunk_rows(rows):
    return jnp.concatenate([jnp.broadcast_to(r, (HGRN_CHUNK, r.shape[1])) for r in rows], axis=0)


def _hgrn_prep(hg, lbl):
    w = HGRN_WIDTH
    a0, a1 = lbl[0:1, :], lbl[1:2, :]
    mx = jnp.maximum(a0, a1)
    e0, e1 = jnp.exp(a0 - mx), jnp.exp(a1 - mx)
    lb = e0 / (e0 + e1)
    qb, fb, gb = hg[:, :w], hg[:, w:2 * w], hg[:, 3 * w:]
    sg = _sigmoid(fb)
    f = lb + (1.0 - lb) * sg
    b = _chunk_cumsum(jnp.log(f))
    bmid, btot = _chunk_row(b, HGRN_CHUNK // 2 - 1), _chunk_row(b, HGRN_CHUNK - 1)
    sq = _sigmoid(qb)
    p = dict(lb=lb, sg=sg, f=f, kk=1.0 - f, sq=sq, qf=qb * sq, gb=gb,
             e_iq=jnp.exp(b - bmid), e_ik=jnp.exp(bmid - b), e_b=jnp.exp(b),
             e_bb=jnp.exp(btot - b), e_tot=jnp.exp(btot))
    p["qi"] = p["qf"] * p["e_iq"]
    p["ki"] = p["kk"] * p["e_ik"]
    p["qs"] = p["qf"] * p["e_b"]
    p["kb"] = p["kk"] * p["e_bb"]
    return p


def _chunk_masks():
    t = lax.broadcasted_iota(jnp.int32, (HGRN_ROWS, HGRN_ROWS), 0)
    s = lax.broadcasted_iota(jnp.int32, (HGRN_ROWS, HGRN_ROWS), 1)
    tril = ((t // HGRN_CHUNK) == (s // HGRN_CHUNK)) & (s <= t)
    n_chunks = HGRN_ROWS // HGRN_CHUNK
    tt = lax.broadcasted_iota(jnp.int32, (HGRN_ROWS, n_chunks * LANES), 0)
    cc = lax.broadcasted_iota(jnp.int32, (HGRN_ROWS, n_chunks * LANES), 1)
    block = (tt // HGRN_CHUNK) == (cc // LANES)
    return tril, block


def _spread(x, block):
    n_chunks = HGRN_ROWS // HGRN_CHUNK
    return jnp.where(block, jnp.tile(x, (1, n_chunks)), jnp.zeros((), x.dtype))


def _fold(x_full, block):
    n_chunks = HGRN_ROWS // HGRN_CHUNK
    z = jnp.where(block, x_full, 0.0)
    acc = z[:, :LANES]
    for n in range(1, n_chunks):
        acc = acc + z[:, n * LANES:(n + 1) * LANES]
    return acc


def hgrn_fwd(hg, lb_logits, hnw, weight_halves=()):
    seq = hg.shape[0]
    nblk = seq // HGRN_ROWS
    n_steps = nblk // HGRN_STEP_BLOCKS
    step_rows = HGRN_ROWS * HGRN_STEP_BLOCKS
    n_chunks = HGRN_ROWS // HGRN_CHUNK
    n_w = len(weight_halves)

    def body(*refs):
        hg_ref, lbl_ref, hnw_ref = refs[:3]
        w_refs = refs[3:3 + n_w]
        yb_ref, o_ref, st0_ref = refs[3 + n_w:6 + n_w]
        g_refs = refs[6 + n_w:6 + 2 * n_w]
        st_scr = refs[6 + 2 * n_w]
        step = pl.program_id(0)
        if n_w:
            start, forward, finish = _gather_phases(w_refs, g_refs, *refs[7 + 2 * n_w:])
            pl.when(step == 0)(start)
            pl.when(step == (3 * n_steps) // 4)(forward)

        @pl.when(step == 0)
        def _():
            st_scr[...] = jnp.zeros_like(st_scr)

        tril, block = _chunk_masks()
        for sub in range(HGRN_STEP_BLOCKS):
            rows = slice(sub * HGRN_ROWS, (sub + 1) * HGRN_ROWS)
            hg_v = hg_ref[rows, :]
            p = _hgrn_prep(hg_v, lbl_ref[...])
            vv = hg_v[:, 2 * HGRN_WIDTH:3 * HGRN_WIDTH].astype(BF16)
            outs = []
            for h in range(HGRN_HEADS):
                sl = slice(h * LANES, (h + 1) * LANES)
                v_h = vv[:, sl]
                a = jnp.where(tril, _dot_nt(p["qi"][:, sl].astype(BF16), p["ki"][:, sl].astype(BF16)), 0.0)
                o = _dot(a.astype(BF16), v_h)
                upd = _dot_tn(v_h, _spread(p["kb"][:, sl].astype(BF16), block))
                st = st_scr[h]
                st0_ref[sub, h] = st
                parts = []
                for n in range(n_chunks):
                    parts.append(st.astype(BF16))
                    decay = p["e_tot"][n * HGRN_CHUNK:n * HGRN_CHUNK + 1, sl]
                    st = st * decay + upd[:, n * LANES:(n + 1) * LANES]
                st_scr[h] = st
                o = o + _dot_nt(_spread(p["qs"][:, sl].astype(BF16), block), jnp.concatenate(parts, axis=1))
                outs.append(o)
            o_ref[rows, :] = jnp.concatenate(outs, axis=1)
            normed = jnp.concatenate(
                [outs[h] * _rms(outs[h]) for h in range(HGRN_HEADS)], axis=1)
            gb = p["gb"]
            yb_ref[rows, :] = (normed * hnw_ref[...]) * (gb * _sigmoid(gb))
        if n_w:
            pl.when(step == n_steps - 1)(finish)

    anywhere = pl.BlockSpec(memory_space=pl.ANY)
    return pl.pallas_call(
        body, name="hgrn_fwd", grid=(n_steps,),
        in_specs=[_rows(step_rows, 4 * HGRN_WIDTH), _full((2, HGRN_WIDTH)), _full((1, HGRN_WIDTH))]
        + [anywhere] * n_w,
        out_specs=[_rows(step_rows, HGRN_WIDTH), _rows(step_rows, HGRN_WIDTH),
                   pl.BlockSpec((HGRN_STEP_BLOCKS, HGRN_HEADS, LANES, LANES), lambda i: (i, 0, 0, 0))]
        + [anywhere] * n_w,
        out_shape=[jax.ShapeDtypeStruct((seq, HGRN_WIDTH), F32)] * 2
        + [jax.ShapeDtypeStruct((nblk, HGRN_HEADS, LANES, LANES), F32)]
        + [_gathered_shape(h) for h in weight_halves],
        scratch_shapes=[pltpu.VMEM((HGRN_HEADS, LANES, LANES), F32)] + (_gather_scratch(n_w) if n_w else []),
        compiler_params=_cparams("arbitrary"),
    )(hg, lb_logits, hnw, *weight_halves)


def ffn_fwd(ya, yb, x, w_out, norm2_w, w_gu4, w_down, final_w, target):
    seq = x.shape[0]
    tm = ROW_TILE
    cw = w_gu4.shape[2]
    inv_d = 1.0 / D_MODEL

    def body(ya_ref, yb_ref, x_ref, wo_ref, nw_ref, wgu_ref, wd_ref, fw_ref, t_ref,
             mixed_ref, h1_ref, u2_ref, g_ref, up_ref, act_ref, dh2_ref, acc_ref):
        @pl.when(pl.program_id(0) == 0)
        def _():
            acc_ref[...] = jnp.zeros_like(acc_ref)

        mixed = jnp.concatenate([ya_ref[...], yb_ref[...]], axis=1).astype(BF16)
        mixed_ref[...] = mixed
        h1 = x_ref[...] + _dot(mixed, wo_ref[...])
        h1_ref[...] = h1
        u2 = ((h1 * _rms(h1)) * nw_ref[...]).astype(BF16)
        u2_ref[...] = u2
        g = jnp.concatenate([_dot(u2, wgu_ref[0]), _dot(u2, wgu_ref[1])], axis=1)
        up = jnp.concatenate([_dot(u2, wgu_ref[2]), _dot(u2, wgu_ref[3])], axis=1)
        g_ref[...] = g.astype(BF16)
        up_ref[...] = up.astype(BF16)
        act = ((g * (0.5 * jnp.tanh(0.5 * g) + 0.5)) * up).astype(BF16)
        act_ref[...] = act
        h2 = h1 + _dot(act, wd_ref[...])
        rf = _rms(h2)
        n = h2 * rf
        fw = fw_ref[...]
        err = n * fw - t_ref[...]
        dy = err * inv_d
        acc_ref[0:1, :] += jnp.sum(dy * n, axis=0, keepdims=True)
        acc_ref[1:2, :] += (0.5 * inv_d) * jnp.sum(err * err, axis=0, keepdims=True)
        dn = dy * fw
        dh2_ref[...] = rf * (dn - n * jnp.mean(dn * n, axis=-1, keepdims=True))

    half = _rows(tm, ATTN_WIDTH)
    wide = _rows(tm, D_MODEL)
    ffn = _rows(tm, FFN_HIDDEN)
    return pl.pallas_call(
        body, name="ffn_fwd", grid=(seq // tm,),
        in_specs=[half, half, wide, _weight((D_MODEL, D_MODEL)), _full((1, D_MODEL)),
                  _weight((N_CHIPS, D_MODEL, cw)), _weight((FFN_HIDDEN, D_MODEL)), _full((1, D_MODEL)), wide],
        out_specs=[wide, wide, wide, ffn, ffn, ffn, wide, _full((8, D_MODEL))],
        out_shape=[jax.ShapeDtypeStruct((seq, D_MODEL), BF16), jax.ShapeDtypeStruct((seq, D_MODEL), F32),
                   jax.ShapeDtypeStruct((seq, D_MODEL), BF16)]
        + [jax.ShapeDtypeStruct((seq, FFN_HIDDEN), BF16)] * 3
        + [jax.ShapeDtypeStruct((seq, D_MODEL), F32), jax.ShapeDtypeStruct((8, D_MODEL), F32)],
        compiler_params=_cparams("arbitrary"),
    )(ya, yb, x, w_out, norm2_w, w_gu4, w_down, final_w, target)


def _head_sum_matrix():
    i = jnp.arange(ATTN_WIDTH)
    return ((i[:, None] // HEAD_DIM) == (i[None, :] // HEAD_DIM)).astype(BF16)


def ffn_bwd(dh2, w_down, g, up, w_gu4, h1, norm2_w, w_out, ya):
    seq = h1.shape[0]
    tm = ROW_TILE
    cw = w_gu4.shape[2]
    hsum = _head_sum_matrix()

    def body(dh2_ref, wd_ref, g_ref, up_ref, w_ref, h1_ref, nw_ref, wo_ref, ya_ref, hs_ref,
             dgu_ref, dh1_ref, dya_ref, dyb_ref, delta_ref, acc_ref):
        @pl.when(pl.program_id(0) == 0)
        def _():
            acc_ref[...] = jnp.zeros_like(acc_ref)

        dh2_b = dh2_ref[...].astype(BF16)
        du2 = jnp.zeros((tm, D_MODEL), F32)
        for j in range(N_CHIPS // 2):
            dact = _dot_nt(dh2_b, wd_ref[j * cw:(j + 1) * cw, :])
            gv = g_ref[:, j * cw:(j + 1) * cw].astype(F32)
            sg = _sigmoid(gv)
            dg = (dact * up_ref[:, j * cw:(j + 1) * cw].astype(F32) * (sg * (1.0 + gv * (1.0 - sg)))).astype(BF16)
            dup = (dact * (gv * sg)).astype(BF16)
            dgu_ref[:, j * cw:(j + 1) * cw] = dg
            dgu_ref[:, FFN_HIDDEN + j * cw:FFN_HIDDEN + (j + 1) * cw] = dup
            du2 = du2 + _dot_nt(dg, w_ref[j]) + _dot_nt(dup, w_ref[N_CHIPS // 2 + j])
        h1 = h1_ref[...]
        r2 = _rms(h1)
        nh = h1 * r2
        acc_ref[0:1, :] += jnp.sum(du2 * nh, axis=0, keepdims=True)
        dn = du2 * nw_ref[...]
        dh1 = dh2_ref[...] + r2 * (dn - nh * jnp.mean(dn * nh, axis=-1, keepdims=True))
        dh1_ref[...] = dh1
        dmixed = _dot_nt(dh1.astype(BF16), wo_ref[...])
        dya = dmixed[:, :ATTN_WIDTH]
        dya_ref[...] = dya
        dyb_ref[...] = dmixed[:, ATTN_WIDTH:]
        prod = dya * ya_ref[...]
        hi = prod.astype(BF16)
        lo = (prod - hi.astype(F32)).astype(BF16)
        delta_ref[...] = _dot(hi, hs_ref[...]) + _dot(lo, hs_ref[...])

    wide = _rows(tm, D_MODEL)
    half = _rows(tm, ATTN_WIDTH)
    ffn = _rows(tm, FFN_HIDDEN)
    return pl.pallas_call(
        body, name="ffn_bwd", grid=(seq // tm,),
        in_specs=[wide, _weight((FFN_HIDDEN, D_MODEL)), ffn, ffn, _weight((N_CHIPS, D_MODEL, cw)), wide,
                  _full((1, D_MODEL)), _weight((D_MODEL, D_MODEL)), half, _full((ATTN_WIDTH, ATTN_WIDTH))],
        out_specs=[_rows(tm, 2 * FFN_HIDDEN), wide, half, half, half, _full((8, D_MODEL))],
        out_shape=[jax.ShapeDtypeStruct((seq, 2 * FFN_HIDDEN), BF16), jax.ShapeDtypeStruct((seq, D_MODEL), F32)]
        + [jax.ShapeDtypeStruct((seq, ATTN_WIDTH), F32)] * 3 + [jax.ShapeDtypeStruct((8, D_MODEL), F32)],
        compiler_params=_cparams("arbitrary"),
    )(dh2, w_down, g, up, w_gu4, h1, norm2_w, w_out, ya, hsum)


def attn_bwd(q, k, v, q16, k16, v16, dy, lse, delta, sibling_grads=()):
    seq = q.shape[0]
    cur, prev = _attn_specs()
    cur16, prev16 = _major_specs()
    whole = pl.BlockSpec((seq, LANES), lambda hp, j: (0, hp))
    n_g = len(sibling_grads)
    n_hp, n_steps = ATTN_WIDTH // LANES, seq // ATTN_STEP_ROWS
    n_in = 13

    def body(*refs):
        natural, major = refs[:5], refs[5:10]
        dy_ref, lse_ref, dl_ref = refs[10:n_in]
        dq_ref, dk_ref, dv_ref = refs[n_in + n_g:n_in + 3 + n_g]
        first_step = pl.program_id(1) == 0
        base = pl.program_id(1) * ATTN_STEP_ROWS
        masks = _head_masks()
        if n_g:
            start, finish = _sibling_exchange_phases(
                refs[n_in:n_in + n_g], refs[n_in + 3 + n_g:n_in + 3 + 2 * n_g], *refs[n_in + 3 + 2 * n_g:])
            pl.when((pl.program_id(0) == 0) & first_step)(start)

        def block(t, r, span, per_step, dilation, add):
            rows = _strided_rows(r + span * t, dilation)
            at_edge = t == 0
            q2, k2, v2 = _load_qkv(natural, major, t, r, span, dilation, per_step)
            dy2 = dy_ref[rows, :].astype(BF16)
            lse2, dl2 = lse_ref[rows, :], dl_ref[rows, :]
            valid = _window_valid(first_step & at_edge)
            zero = jnp.zeros_like(q2)
            qms, dyms, ps, dss, kms = [], [], [], [], []
            for h, mh in enumerate(masks):
                c0 = h * HEAD_DIM
                qm, dym = jnp.where(mh, q2, zero), jnp.where(mh, dy2, zero)
                s = _dot_nt(qm, k2)
                p = jnp.where(valid, jnp.exp(s - lse2[:, c0:c0 + 1]), 0.0)
                dp = _dot_nt(dym, v2)
                dss.append((p * (dp - dl2[:, c0:c0 + 1])).astype(BF16))
                ps.append(p.astype(BF16))
                qms.append(qm)
                dyms.append(dym)
                kms.append(jnp.where(mh, k2, jnp.zeros_like(k2)))
            dq = _dot(jnp.concatenate(dss, axis=1), jnp.concatenate(kms, axis=0)) * ATTN_SCALE
            p_all, ds_all = jnp.concatenate(ps, axis=0), jnp.concatenate(dss, axis=0)
            dy_all, q_all = jnp.concatenate(dyms, axis=0), jnp.concatenate(qms, axis=0)
            dv_full, dk_full = _dot_tn(dy_all, p_all).T, _dot_tn(q_all, ds_all).T
            here = _strided_rows(base + r + span * t, dilation)
            if add:
                dq_ref[rows, :] += dq
                dk_ref[here, :] += dk_full[ATTN_BLOCK:]
                dv_ref[here, :] += dv_full[ATTN_BLOCK:]
            else:
                dq_ref[rows, :] = dq
                dk_ref[here, :] = dk_full[ATTN_BLOCK:]
                dv_ref[here, :] = dv_full[ATTN_BLOCK:]
            back = _strided_rows(jnp.maximum(base + r + span * t - span, r), dilation)
            dk_ref[back, :] += dk_full[:ATTN_BLOCK]
            dv_ref[back, :] += dv_full[:ATTN_BLOCK]

        for index, (_, dilation) in enumerate(DILATED_PAIRS):
            _for_each_block(dilation, ATTN_BWD_UNROLL,
                            functools.partial(block, dilation=dilation, add=index > 0))
        if n_g:
            pl.when((pl.program_id(0) == n_hp - 1) & (pl.program_id(1) == n_steps - 1))(finish)

    anywhere = pl.BlockSpec(memory_space=pl.ANY)
    return pl.pallas_call(
        body, name="attn_bwd", grid=(n_hp, n_steps),
        in_specs=[cur, cur, cur, prev, prev, cur16, cur16, cur16, prev16, prev16, cur, cur, cur]
        + [anywhere] * n_g,
        out_specs=[cur, whole, whole] + [anywhere] * n_g,
        out_shape=[jax.ShapeDtypeStruct((seq, ATTN_WIDTH), F32)] * 3 + _sibling_exchange_shapes(sibling_grads),
        scratch_shapes=_sibling_exchange_scratch(n_g) if n_g else [],
        compiler_params=_cparams("arbitrary", "arbitrary"),
    )(q, k, v, k, v, q16, k16, v16, k16, v16, dy, lse, delta, *sibling_grads)


def hgrn_bwd(hg, lb_logits, hnw, o_pre, st0, dyb, chip_sums=()):
    seq = hg.shape[0]
    step_rows = HGRN_ROWS * HGRN_STEP_BLOCKS
    n_steps = seq // step_rows
    n_chunks = HGRN_ROWS // HGRN_CHUNK
    w = HGRN_WIDTH
    n_s = len(chip_sums)

    def body(*refs):
        hg_ref, lbl_ref, hnw_ref, o_ref, st0_ref, dyb_ref = refs[:6]
        dhg_ref, acc_ref = refs[6 + n_s:8 + n_s]
        dst_scr = refs[8 + 2 * n_s]
        step = pl.program_id(0)
        if n_s:
            start, finish = _chip_exchange_phases(refs[6:6 + n_s], refs[8 + n_s:8 + 2 * n_s], *refs[9 + 2 * n_s:])
            pl.when(step == 0)(start)

        @pl.when(step == 0)
        def _():
            dst_scr[...] = jnp.zeros_like(dst_scr)
            acc_ref[...] = jnp.zeros_like(acc_ref)

        tril, block = _chunk_masks()
        for sub in reversed(range(HGRN_STEP_BLOCKS)):
            rows = slice(sub * HGRN_ROWS, (sub + 1) * HGRN_ROWS)
            hg_v = hg_ref[rows, :]
            p = _hgrn_prep(hg_v, lbl_ref[...])
            vv = hg_v[:, 2 * w:3 * w].astype(BF16)
            hnw_v = hnw_ref[...]
            gb = p["gb"]
            sgg = _sigmoid(gb)
            silu_g = gb * sgg
            dyb_v = dyb_ref[rows, :]
            o_v = o_ref[rows, :]

            d_on = dyb_v * hnw_v * silu_g
            on_parts, do_parts = [], []
            for h in range(HGRN_HEADS):
                sl = slice(h * LANES, (h + 1) * LANES)
                rs = _rms(o_v[:, sl])
                on = o_v[:, sl] * rs
                on_parts.append(on)
                do_parts.append(rs * (d_on[:, sl] - on * jnp.mean(d_on[:, sl] * on, axis=-1, keepdims=True)))
            on_all = jnp.concatenate(on_parts, axis=1)
            dgb = dyb_v * on_all * hnw_v * (sgg * (1.0 + gb * (1.0 - sgg)))
            acc_ref[0:1, :] += jnp.sum(dyb_v * on_all * silu_g, axis=0, keepdims=True)

            dqf_parts, dkk_parts, db_parts, dv_parts, dbt_parts, dkbkb_parts = [], [], [], [], [], []
            for h in range(HGRN_HEADS):
                sl = slice(h * LANES, (h + 1) * LANES)
                v_h = vv[:, sl]
                do_h = do_parts[h].astype(BF16)
                qi, ki, qs, kb = p["qi"][:, sl], p["ki"][:, sl], p["qs"][:, sl], p["kb"][:, sl]
                qi_b, ki_b = qi.astype(BF16), ki.astype(BF16)
                kb_cat = _spread(kb.astype(BF16), block)
                qs_cat = _spread(qs.astype(BF16), block)
                upd = _dot_tn(v_h, kb_cat)
                st = st0_ref[sub, h]
                st_parts = []
                for n in range(n_chunks):
                    st_parts.append(st)
                    decay = p["e_tot"][n * HGRN_CHUNK:n * HGRN_CHUNK + 1, sl]
                    st = st * decay + upd[:, n * LANES:(n + 1) * LANES]
                st_cat = jnp.concatenate([s_.astype(BF16) for s_ in st_parts], axis=1)
                wgt = _dot_tn(do_h, qs_cat)
                dst = dst_scr[h]
                dst_parts = [None] * n_chunks
                dbt_rows = [None] * n_chunks
                for n in reversed(range(n_chunks)):
                    dst_parts[n] = dst.astype(BF16)
                    decay = p["e_tot"][n * HGRN_CHUNK:n * HGRN_CHUNK + 1, sl]
                    dbt_rows[n] = jnp.sum(dst * st_parts[n], axis=0, keepdims=True) * decay
                    dst = dst * decay + wgt[:, n * LANES:(n + 1) * LANES]
                dst_scr[h] = dst
                dst_cat = jnp.concatenate(dst_parts, axis=1)
                dqs = _fold(_dot(do_h, st_cat), block)
                dkb = _fold(_dot(v_h, dst_cat), block)
                dv_state = _dot_nt(kb_cat, dst_cat)
                a = jnp.where(tril, _dot_nt(qi_b, ki_b), 0.0).astype(BF16)
                da = jnp.where(tril, _dot_nt(do_h, v_h), 0.0).astype(BF16)
                dv_parts.append(_dot_tn(a, do_h) + dv_state)
                dqi = _dot(da, ki_b)
                dki = _dot_tn(da, qi_b)
                dqf_parts.append(dqi * p["e_iq"][:, sl] + dqs * p["e_b"][:, sl])
                dkk_parts.append(dki * p["e_ik"][:, sl] + dkb * p["e_bb"][:, sl])
                dkbkb = dkb * kb
                db_parts.append(dqi * qi - dki * ki + dqs * qs - dkbkb)
                dkbkb_parts.append(dkbkb)
                dbt_parts.append(_chunk_rows(dbt_rows))

            cat = lambda parts: jnp.concatenate(parts, axis=1)
            dlogf = (_chunk_cumsum(cat(db_parts), reverse=True)
                     + _chunk_row(_chunk_cumsum(cat(dkbkb_parts)), HGRN_CHUNK - 1) + cat(dbt_parts))
            sq, qb = p["sq"], hg_v[:, :w]
            dqb = cat(dqf_parts) * (sq * (1.0 + qb * (1.0 - sq)))
            df = dlogf / p["f"] - cat(dkk_parts)
            sg, lb = p["sg"], p["lb"]
            dfb = df * (1.0 - lb) * sg * (1.0 - sg)
            acc_ref[1:2, :] += jnp.sum(df * (1.0 - sg), axis=0, keepdims=True)
            dhg_ref[rows, :] = jnp.concatenate([dqb, dfb, cat(dv_parts), dgb], axis=1).astype(BF16)
        if n_s:
            pl.when(step == n_steps - 1)(finish)

    rev = lambda i: (n_steps - 1 - i, 0)
    anywhere = pl.BlockSpec(memory_space=pl.ANY)
    return pl.pallas_call(
        body, name="hgrn_bwd", grid=(n_steps,),
        in_specs=[pl.BlockSpec((step_rows, 4 * w), rev), _full((2, w)), _full((1, w)),
                  pl.BlockSpec((step_rows, w), rev),
                  pl.BlockSpec((HGRN_STEP_BLOCKS, HGRN_HEADS, LANES, LANES), lambda i: (n_steps - 1 - i, 0, 0, 0)),
                  pl.BlockSpec((step_rows, w), rev)] + [anywhere] * n_s,
        out_specs=[pl.BlockSpec((step_rows, 4 * w), rev), _full((8, w))] + [anywhere] * n_s,
        out_shape=[jax.ShapeDtypeStruct((seq, 4 * w), BF16), jax.ShapeDtypeStruct((8, w), F32)]
        + [jax.ShapeDtypeStruct(s.shape, s.dtype) for s in chip_sums],
        scratch_shapes=[pltpu.VMEM((HGRN_HEADS, LANES, LANES), F32)] + (_chip_exchange_scratch(n_s) if n_s else []),
        compiler_params=_cparams("arbitrary"),
    )(hg, lb_logits, hnw, o_pre, st0, dyb, *chip_sums)


def in_bwd(dq, dk, dv, dhg, cos_t, sin_t, w_in4, x, norm1_w, dh1):
    seq = x.shape[0]
    tm = WIDE_ROW_TILE
    cw = w_in4.shape[2]

    def body(dq_ref, dk_ref, dv_ref, dhg_ref, cos_ref, sin_ref, w_ref,
             x_ref, nw_ref, dh1_ref, dproj_ref, dx_ref, acc_ref):
        @pl.when(pl.program_id(0) == 0)
        def _():
            acc_ref[...] = jnp.zeros_like(acc_ref)

        cos, sin = cos_ref[...], sin_ref[...]
        dqa = _rotary_bwd(dq_ref[...], cos, sin)
        dka = _rotary_bwd(dk_ref[...], cos, sin)
        dproj = jnp.concatenate(
            [jnp.concatenate([dqa, dka, dv_ref[...]], axis=1).astype(BF16), dhg_ref[...]], axis=1)
        dproj_ref[...] = dproj
        du = _dot_nt(dproj[:, :cw], w_ref[0])
        for j in range(1, N_CHIPS):
            du = du + _dot_nt(dproj[:, j * cw:(j + 1) * cw], w_ref[j])
        xv = x_ref[...]
        r1 = _rms(xv)
        nx = xv * r1
        acc_ref[0:1, :] += jnp.sum(du * nx, axis=0, keepdims=True)
        dn = du * nw_ref[...]
        dx_ref[...] = dh1_ref[...] + r1 * (dn - nx * jnp.mean(dn * nx, axis=-1, keepdims=True))

    half = _rows(tm, ATTN_WIDTH)
    wide = _rows(tm, D_MODEL)
    return pl.pallas_call(
        body, name="in_bwd", grid=(seq // tm,),
        in_specs=[half] * 3 + [_rows(tm, 4 * HGRN_WIDTH), _rows(tm, LANES), _rows(tm, LANES),
                               _weight((N_CHIPS, D_MODEL, cw)), wide, _full((1, D_MODEL)), wide],
        out_specs=[_rows(tm, IN_PROJ_WIDTH), wide, _full((8, D_MODEL))],
        out_shape=[jax.ShapeDtypeStruct((seq, IN_PROJ_WIDTH), BF16), jax.ShapeDtypeStruct((seq, D_MODEL), F32),
                   jax.ShapeDtypeStruct((8, D_MODEL), F32)],
        compiler_params=_cparams("arbitrary"),
    )(dq, dk, dv, dhg, cos_t, sin_t, w_in4, x, norm1_w, dh1)


def weight_grad(a, b, col_block, name, group=1, small_pack=None, rows_per_step=1024):
    seq, kdim = a.shape
    ndim = b.shape[1]
    nj = ndim // col_block
    tk = min(rows_per_step, seq)
    hosting = small_pack is not None
    n_j, n_t = nj // group, seq // tk

    def body(*refs):
        a_ref, b_ref = refs[:2]
        o_ref = refs[3] if hosting else refs[2]
        if hosting:
            start, finish = _pack_gather_phases(refs[2], refs[4], *refs[5:])
            pl.when((pl.program_id(0) == 0) & (pl.program_id(1) == 0))(start)

        @pl.when(pl.program_id(1) == 0)
        def _():
            o_ref[...] = jnp.zeros_like(o_ref)

        acc = _dot_tn(a_ref[...].astype(BF16), b_ref[...].astype(BF16))
        for i in range(group):
            o_ref[i] += acc[:, i * col_block:(i + 1) * col_block]
        if hosting:
            pl.when((pl.program_id(0) == n_j - 1) & (pl.program_id(1) == n_t - 1))(finish)

    anywhere = pl.BlockSpec(memory_space=pl.ANY)
    out = pl.pallas_call(
        body, name=name, grid=(n_j, n_t),
        in_specs=[pl.BlockSpec((tk, kdim), lambda j, t: (t, 0)),
                  pl.BlockSpec((tk, group * col_block), lambda j, t: (t, j))] + [anywhere] * hosting,
        out_specs=[pl.BlockSpec((group, kdim, col_block), lambda j, t: (j, 0, 0))] + [anywhere] * hosting,
        out_shape=[jax.ShapeDtypeStruct((nj, kdim, col_block), F32)]
        + ([jax.ShapeDtypeStruct((N_DEV,) + small_pack.shape, F32)] if hosting else []),
        scratch_shapes=[pltpu.SemaphoreType.DMA((N_DEV - 1,)), pltpu.SemaphoreType.DMA((N_DEV - 1,)),
                        pltpu.SemaphoreType.DMA] if hosting else [],
        compiler_params=_cparams("arbitrary", "arbitrary"),
    )(a, b, *([small_pack] if hosting else []))
    return out if hosting else out[0]


def _sibling_exchange_phases(g_refs, out_refs, send_sems, recv_sems):
    x, y, cc = _mesh_pos()

    def copies():
        return [pltpu.make_async_remote_copy(
            src_ref=g_refs[i].at[j, 1 - cc], dst_ref=out_refs[i].at[j],
            send_sem=send_sems.at[i * N_CHIPS + j], recv_sem=recv_sems.at[i * N_CHIPS + j],
            device_id=(x, y, 1 - cc), device_id_type=MESH_ID)
            for i in range(len(g_refs)) for j in range(N_CHIPS)]

    def start():
        for cp in copies():
            cp.start()

    def finish():
        for cp in copies():
            cp.wait_recv()
        for cp in copies():
            cp.wait_send()

    return start, finish


def _sibling_exchange_scratch(n):
    return [pltpu.SemaphoreType.DMA((n * N_CHIPS,)), pltpu.SemaphoreType.DMA((n * N_CHIPS,))]


def _sibling_exchange_shapes(grads):
    return [jax.ShapeDtypeStruct((N_CHIPS,) + g.shape[2:], g.dtype) for g in grads]


def exchange_with_sibling(grads, name):
    n = len(grads)

    def body(*refs):
        start, finish = _sibling_exchange_phases(refs[:n], refs[n:2 * n], refs[2 * n], refs[2 * n + 1])
        start()
        finish()

    return pl.pallas_call(
        body, name=name,
        in_specs=[pl.BlockSpec(memory_space=pl.ANY)] * n,
        out_specs=[pl.BlockSpec(memory_space=pl.ANY)] * n,
        out_shape=_sibling_exchange_shapes(grads),
        scratch_shapes=_sibling_exchange_scratch(n),
    )(*grads)


def add_own_half(grads, recvs, name):
    n = len(grads)

    def body(cc_ref, *refs):
        for g_ref, r_ref, o_ref in zip(refs[:n], refs[n:2 * n], refs[2 * n:]):
            o_ref[...] = (g_ref[...] + r_ref[...]).astype(BF16)

    own, got = [], []
    for g in grads:
        _, _, r, c = g.shape
        own.append(pl.BlockSpec((None, None, r // 2, c), lambda j, t, cc: (j, cc[0], t, 0)))
        got.append(pl.BlockSpec((None, r // 2, c), lambda j, t, cc: (j, t, 0)))
    grid_spec = pltpu.PrefetchScalarGridSpec(
        num_scalar_prefetch=1, grid=(N_CHIPS, 2), in_specs=own + got, out_specs=got)
    cc = lax.axis_index("c").astype(jnp.int32).reshape(1)
    return pl.pallas_call(
        body, name=name, grid_spec=grid_spec,
        out_shape=[jax.ShapeDtypeStruct((N_CHIPS,) + g.shape[2:], BF16) for g in grads],
        compiler_params=_cparams("parallel", "parallel"),
    )(cc, *grads, *recvs)


def _chip_exchange_phases(s_refs, out_refs, send_sems, recv_sems):
    n = len(s_refs)
    x, y, cc = _mesh_pos()
    my_chip = 2 * x + y
    chips = [(1 - x, y), (x, 1 - y), (1 - x, 1 - y)]

    def outgoing():
        return [pltpu.make_async_remote_copy(
            src_ref=s_refs[i].at[2 * px + py], dst_ref=out_refs[i].at[my_chip],
            send_sem=send_sems.at[3 * i + j], recv_sem=recv_sems.at[3 * i + j],
            device_id=(px, py, cc), device_id_type=MESH_ID)
            for i in range(n) for j, (px, py) in enumerate(chips)]

    def start():
        for cp in outgoing():
            cp.start()

    def finish():
        for i in range(n):
            for j, (px, py) in enumerate(chips):
                pltpu.make_async_remote_copy(
                    src_ref=s_refs[i].at[my_chip], dst_ref=out_refs[i].at[2 * px + py],
                    send_sem=send_sems.at[3 * i + j], recv_sem=recv_sems.at[3 * i + j],
                    device_id=(px, py, cc), device_id_type=MESH_ID).wait_recv()
        for cp in outgoing():
            cp.wait_send()

    return start, finish


def _chip_exchange_scratch(n):
    return [pltpu.SemaphoreType.DMA((3 * n,)), pltpu.SemaphoreType.DMA((3 * n,))]


def sum_chips(sums, parts, name):
    n = len(sums)

    def body(idx_ref, *refs):
        for i in range(n):
            s_ref, p1_ref, p2_ref, p3_ref = refs[4 * i:4 * i + 4]
            refs[4 * n + i][...] = ((s_ref[...].astype(F32) + p1_ref[...].astype(F32))
                                    + p2_ref[...].astype(F32)) + p3_ref[...].astype(F32)

    in_specs, out_specs, operands = [], [], []
    for s_, p_ in zip(sums, parts):
        _, r, c = p_.shape
        in_specs += [pl.BlockSpec((None, r // 2, c), functools.partial(lambda t, idx, k: (idx[k], t, 0), k=k))
                     for k in range(N_CHIPS)]
        out_specs.append(pl.BlockSpec((r // 2, c), lambda t, idx: (t, 0)))
        operands += [s_, p_, p_, p_]
    x, y = lax.axis_index("x"), lax.axis_index("y")
    idx = jnp.stack([2 * x + y, 2 * (1 - x) + y, 2 * x + (1 - y), 2 * (1 - x) + (1 - y)]).astype(jnp.int32)
    grid_spec = pltpu.PrefetchScalarGridSpec(
        num_scalar_prefetch=1, grid=(2,), in_specs=in_specs, out_specs=out_specs)
    return pl.pallas_call(
        body, name=name, grid_spec=grid_spec,
        out_shape=[jax.ShapeDtypeStruct(p_.shape[1:], F32) for p_ in parts],
        compiler_params=_cparams("parallel"),
    )(idx, *operands)


def share_with_sibling(halves, name):
    n = len(halves)

    def body(*refs):
        h_refs, out_refs = refs[:n], refs[n:2 * n]
        send_sems, recv_sems = refs[2 * n], refs[2 * n + 1]
        x, y, cc = _mesh_pos()
        copies = [pltpu.make_async_remote_copy(
            src_ref=h_refs[i], dst_ref=out_refs[i],
            send_sem=send_sems.at[i], recv_sem=recv_sems.at[i],
            device_id=(x, y, 1 - cc), device_id_type=MESH_ID) for i in range(n)]
        for cp in copies:
            cp.start()
        for cp in copies:
            cp.wait_recv()
        for cp in copies:
            cp.wait_send()

    return pl.pallas_call(
        body, name=name,
        in_specs=[pl.BlockSpec(memory_space=pl.ANY)] * n,
        out_specs=[pl.BlockSpec(memory_space=pl.ANY)] * n,
        out_shape=[jax.ShapeDtypeStruct(h.shape, h.dtype) for h in halves],
        scratch_shapes=[pltpu.SemaphoreType.DMA((n,)), pltpu.SemaphoreType.DMA((n,))],
    )(*halves)


def _adam_update(w, g, m, v):
    m = ADAM_B1 * m + (1.0 - ADAM_B1) * g
    v = ADAM_B2 * v + (1.0 - ADAM_B2) * (g * g)
    m_hat = m / (1.0 - ADAM_B1 ** ADAM_STEP)
    v_hat = v / (1.0 - ADAM_B2 ** ADAM_STEP)
    delta = -ADAM_LR * (m_hat / (jnp.sqrt(v_hat) + ADAM_EPS) + ADAM_WD * w)
    return delta, m, v


ADAMW_STEPS = 8


def adamw(ws, g_mine, g_sibling, ms, vs, name, chip_sums=()):
    n, n_s = len(ws), len(chip_sums)
    per_half = ADAMW_STEPS // 2

    def body(*refs):
        cc_ref = refs[0]
        ins = refs[1:1 + 5 * n]
        outs = refs[1 + 5 * n + n_s:1 + 9 * n + n_s]
        step = pl.program_id(0)
        if n_s:
            start, finish = _chip_exchange_phases(
                refs[1 + 5 * n:1 + 5 * n + n_s], refs[1 + 9 * n + n_s:1 + 9 * n + 2 * n_s],
                *refs[1 + 9 * n + 2 * n_s:])
            pl.when(step == 0)(start)
        mine = (step // per_half) == cc_ref[0]
        for i in range(n):
            w_ref, ga_ref, gb_ref, m_ref, v_ref = ins[5 * i:5 * i + 5]
            g_ref, d_ref, nm_ref, nv_ref = outs[4 * i:4 * i + 4]
            g = jnp.where(mine, ga_ref[...], gb_ref[...])
            g_ref[...] = g
            d, nm, nv = _adam_update(w_ref[...], g, m_ref[...], v_ref[...])
            d_ref[...] = d
            nm_ref[...] = nm
            nv_ref[...] = nv
        if n_s:
            pl.when(step == ADAMW_STEPS - 1)(finish)

    in_specs, out_specs, out_shape, operands = [], [], [], []
    for w, ga, gb, m, v in zip(ws, g_mine, g_sibling, ms, vs):
        r, c = w.shape
        tr = r // ADAMW_STEPS
        full = pl.BlockSpec((tr, c), lambda t, cc: (t, 0))
        part = pl.BlockSpec((tr, c), lambda t, cc: (t % per_half, 0))
        in_specs += [full, part, part, full, full]
        out_specs += [full] * 4
        out_shape += [jax.ShapeDtypeStruct((r, c), F32)] * 4
        operands += [w, ga, gb, m, v]
    anywhere = pl.BlockSpec(memory_space=pl.ANY)
    grid_spec = pltpu.PrefetchScalarGridSpec(
        num_scalar_prefetch=1, grid=(ADAMW_STEPS,),
        in_specs=in_specs + [anywhere] * n_s, out_specs=out_specs + [anywhere] * n_s,
        scratch_shapes=_chip_exchange_scratch(n_s) if n_s else [])
    cc = lax.axis_index("c").astype(jnp.int32).reshape(1)
    res = pl.pallas_call(
        body, name=name, grid_spec=grid_spec,
        out_shape=out_shape + [jax.ShapeDtypeStruct(s.shape, s.dtype) for s in chip_sums],
        compiler_params=_cparams("arbitrary"),
    )(cc, *operands, *chip_sums)
    per_shard = [tuple(res[4 * i:4 * i + 4]) for i in range(n)]
    return (per_shard, list(res[4 * n:])) if n_s else per_shard


def _pack_gather_phases(p_ref, out_ref, send_sems, recv_sems, local_sem):
    x, y, cc = _mesh_pos()
    me = 4 * x + 2 * y + cc
    flips = [(fx, fy, fc) for fx in (0, 1) for fy in (0, 1) for fc in (0, 1)][1:]

    def copy(k, row):
        fx, fy, fc = flips[k]
        return pltpu.make_async_remote_copy(
            src_ref=p_ref, dst_ref=out_ref.at[row],
            send_sem=send_sems.at[k], recv_sem=recv_sems.at[k],
            device_id=(x ^ fx, y ^ fy, cc ^ fc), device_id_type=MESH_ID)

    def local():
        return pltpu.make_async_copy(p_ref, out_ref.at[me], local_sem)

    def start():
        local().start()
        for k in range(len(flips)):
            copy(k, me).start()

    def finish():
        for k, (fx, fy, fc) in enumerate(flips):
            copy(k, 4 * (x ^ fx) + 2 * (y ^ fy) + (cc ^ fc)).wait_recv()
        for k in range(len(flips)):
            copy(k, me).wait_send()
        local().wait()

    return start, finish


def small_update(gathered, wpack, mpack, vpack):
    hw = HGRN_WIDTH

    def body(g_ref, w_ref, m_ref, v_ref, go_ref, d_ref, nm_ref, nv_ref, loss_ref):
        g = g_ref[0]
        for d in range(1, N_DEV):
            g = g + g_ref[d]
        wv = w_ref[...]
        a0, a1 = wv[4:5, :hw], wv[4:5, hw:]
        mx = jnp.maximum(a0, a1)
        e0, e1 = jnp.exp(a0 - mx), jnp.exp(a1 - mx)
        lb = e0 / (e0 + e1)
        dl = g[4:5, :hw] * lb * (1.0 - lb)
        row = lax.broadcasted_iota(jnp.int32, g.shape, 0)
        lb_row = jnp.concatenate([dl, -dl], axis=1)
        grads = jnp.where(row == 4, lb_row, jnp.where(row < 4, g, 0.0))
        go_ref[...] = grads
        d, nm, nv = _adam_update(wv, grads, m_ref[...], v_ref[...])
        d_ref[...] = d
        nm_ref[...] = nm
        nv_ref[...] = nv
        loss_ref[...] = jnp.zeros((8, LANES), F32) + jnp.sum(g[5:6, :])

    vm = pl.BlockSpec(memory_space=pltpu.VMEM)
    return pl.pallas_call(
        body, name="small_update",
        in_specs=[vm] * 4, out_specs=[vm] * 5,
        out_shape=[jax.ShapeDtypeStruct(wpack.shape, F32)] * 4 + [jax.ShapeDtypeStruct((8, LANES), F32)],
    )(gathered, wpack, mpack, vpack)


def _pack_small(n1, n2, fn, hn, lbl):
    z = jnp.zeros((1, D_MODEL - HGRN_WIDTH), F32)
    rows = [n1.reshape(1, D_MODEL), n2.reshape(1, D_MODEL), fn.reshape(1, D_MODEL),
            jnp.concatenate([hn.reshape(1, HGRN_WIDTH), z], axis=1), lbl.reshape(1, 2 * HGRN_WIDTH),
            jnp.zeros((3, D_MODEL), F32)]
    return jnp.concatenate(rows, axis=0)


def _unpack_small(pack):
    return (pack[0:1], pack[4].reshape(2, HGRN_WIDTH), pack[3:4, :HGRN_WIDTH], pack[1:2], pack[2])


def kernel(x, norm1_w, w_in, lb_logits, hgrn_norm_w, w_out, norm2_w, w_gate_up, w_down, final_norm_w, loss_target, m_norm1_w, m_w_in, m_lb_logits, m_hgrn_norm_w, m_w_out, m_norm2_w, m_w_gate_up, m_w_down, m_final_norm_w, v_norm1_w, v_w_in, v_lb_logits, v_hgrn_norm_w, v_w_out, v_norm2_w, v_w_gate_up, v_w_down, v_final_norm_w):
    seq = x.shape[1]
    xs = x.reshape(seq, D_MODEL)
    target = loss_target.reshape(seq, D_MODEL)
    shards = {"w_in": w_in[0], "w_out": w_out[0], "w_gu": w_gate_up[0], "w_down": w_down[0]}

    cast = {k: cast_bf16(w, "cast_" + k) for k, w in shards.items()}
    w_in4 = allgather_halves(cast["w_in"], "gather_w_in").reshape(N_CHIPS, D_MODEL, -1)

    cos_t, sin_t = _rope_tables(seq)
    fw = final_norm_w.reshape(1, D_MODEL)

    qr, kr, va, hg, u, q16, k16, v16, g_out, g_down = in_proj(
        xs, norm1_w, w_in4, cos_t, sin_t, [cast["w_out"], cast["w_down"]])
    ya, lse = attn_fwd(qr, kr, va, q16, k16, v16)
    yb, o_pre, st0, g_gu = hgrn_fwd(hg, lb_logits, hgrn_norm_w, [cast["w_gu"]])
    w_out_f = g_out.reshape(D_MODEL, D_MODEL)
    w_gu4 = g_gu.reshape(N_CHIPS, D_MODEL, -1)
    w_down_f = g_down.reshape(FFN_HIDDEN, D_MODEL)
    mixed, h1, u2, g, up, act, dh2, acc_fin = ffn_fwd(
        ya, yb, xs, w_out_f, norm2_w, w_gu4, w_down_f, fw, target)

    cw_in, cw_gu = w_in4.shape[2], w_gu4.shape[2]
    dgu, dh1, dya, dyb, delta, acc_n2 = ffn_bwd(dh2, w_down_f, g, up, w_gu4, h1, norm2_w, w_out_f, ya)
    early = [
        weight_grad(mixed, dh1, D_MODEL, "wgrad_out", rows_per_step=2048).reshape(N_CHIPS, 2, D_MODEL // 8, D_MODEL),
        weight_grad(u2, dgu, cw_gu, "wgrad_gu", group=2).reshape(N_CHIPS, 2, D_MODEL // 2, cw_gu),
        weight_grad(act, dh2, D_MODEL, "wgrad_down").reshape(N_CHIPS, 2, FFN_HIDDEN // 8, D_MODEL),
    ]
    dq, dk, dv, *early_recv = attn_bwd(qr, kr, va, q16, k16, v16, dya, lse, delta, early)
    early_sums = add_own_half(early, early_recv, "add_half_early")
    dhg, acc_hg, *early_parts = hgrn_bwd(hg, lb_logits, hgrn_norm_w, o_pre, st0, dyb, early_sums)
    dproj, dx, acc_n1 = in_bwd(dq, dk, dv, dhg, cos_t, sin_t, w_in4, xs, norm1_w, dh1)
    z512 = jnp.zeros((1, D_MODEL - HGRN_WIDTH), F32)
    gpack = jnp.concatenate([
        acc_n1[0:1], acc_n2[0:1], acc_fin[0:1],
        jnp.concatenate([acc_hg[0:1], z512], axis=1), jnp.concatenate([acc_hg[1:2], z512], axis=1),
        acc_fin[1:2], jnp.zeros((2, D_MODEL), F32)], axis=0)
    g_in, gathered_packs = weight_grad(u, dproj, cw_in, "wgrad_in", group=2, small_pack=gpack, rows_per_step=2048)
    late = [g_in.reshape(N_CHIPS, 2, D_MODEL // 2, cw_in)]
    late_recv = exchange_with_sibling(late, "grad_exchange_sibling_late")
    late_sums = add_own_half(late, late_recv, "add_half_in")

    early_halves = sum_chips(early_sums, early_parts, "sum_chips_early")
    early_others = share_with_sibling(early_halves, "grad_share_sibling_early")
    early_keys = ["w_out", "w_gu", "w_down"]
    moments = {"w_in": (m_w_in, v_w_in), "w_out": (m_w_out, v_w_out),
               "w_gu": (m_w_gate_up, v_w_gate_up), "w_down": (m_w_down, v_w_down)}
    early_updates, late_parts = adamw(
        [shards[k] for k in early_keys], early_halves, early_others,
        [moments[k][0][0] for k in early_keys], [moments[k][1][0] for k in early_keys],
        "adamw_early", chip_sums=late_sums)
    late_halves = sum_chips(late_sums, late_parts, "sum_chips_in")
    late_others = share_with_sibling(late_halves, "grad_share_sibling_late")
    late_updates = adamw([shards["w_in"]], late_halves, late_others,
                         [moments["w_in"][0][0]], [moments["w_in"][1][0]], "adamw_in")
    big = {k: tuple(t[None] for t in upd) for k, upd in zip(early_keys + ["w_in"], early_updates + late_updates)}

    wpack = _pack_small(norm1_w, norm2_w, final_norm_w, hgrn_norm_w, lb_logits)
    mpack = _pack_small(m_norm1_w, m_norm2_w, m_final_norm_w, m_hgrn_norm_w, m_lb_logits)
    vpack = _pack_small(v_norm1_w, v_norm2_w, v_final_norm_w, v_hgrn_norm_w, v_lb_logits)
    gs, ds, nms, nvs, loss8 = small_update(gathered_packs, wpack, mpack, vpack)
    loss = loss8[0, 0]

    def assemble(small_pack, idx):
        n1, lbl, hn, n2, fn = _unpack_small(small_pack)
        return (n1, big["w_in"][idx], lbl, hn, big["w_out"][idx], n2, big["w_gu"][idx], big["w_down"][idx], fn)

    return (loss, dx.reshape(x.shape), *assemble(gs, 0), *assemble(ds, 1), *assemble(nms, 2), *assemble(nvs, 3))
```

```python
import functools

import jax
import jax.numpy as jnp
from jax import lax
from jax.experimental import pallas as pl
from jax.experimental.pallas import tpu as pltpu

F32 = jnp.float32
BF16 = jnp.bfloat16

D_MODEL = 1024
ATTN_WIDTH = 512
HEAD_DIM = 64
DILATED_PAIRS = ((128, 1), (512, 4), (2048, 16))
ATTN_BLOCK = 128
ROPE_THETA = 10000.0
HGRN_WIDTH = 512
HGRN_CHUNK = 16
HGRN_HEADS = 4
IN_PROJ_WIDTH = 3584
FFN_HIDDEN = 2816
NORM_EPS = 1e-6
ATTN_SCALE = HEAD_DIM ** -0.5
N_CHIPS = 4
N_DEV = 8

ADAM_LR = 0.001
ADAM_B1 = 0.9
ADAM_B2 = 0.999
ADAM_EPS = 1e-08
ADAM_WD = 0.01
ADAM_STEP = 10

LANES = 128
HGRN_ROWS = 128
HGRN_STEP_BLOCKS = 2
ROW_TILE = 256
INPUT_RING_SLOTS = 3
WIDE_ROW_TILE = 512
ATTN_STEP_ROWS = 2048
ATTN_FWD_UNROLL = 16
ATTN_BWD_UNROLL = 16
ATTN_MAJOR_DILATION = 16
VMEM_LIMIT = 56 * 1024 * 1024
NEG_BIG = -1e30
MESH_ID = pl.DeviceIdType.MESH


def _cparams(*sem):
    return pltpu.CompilerParams(dimension_semantics=tuple(sem), vmem_limit_bytes=VMEM_LIMIT)


def _dot(a, b):
    return jnp.dot(a, b, preferred_element_type=F32)


def _dot_nt(a, b):
    return lax.dot_general(a, b, (((1,), (1,)), ((), ())), preferred_element_type=F32)


def _dot_tn(a, b):
    return lax.dot_general(a, b, (((0,), (0,)), ((), ())), preferred_element_type=F32)


def _sigmoid(x):
    return 1.0 / (1.0 + jnp.exp(-x))


def _full(shape):
    n = len(shape)
    return pl.BlockSpec(shape, lambda *_: (0,) * n)


def _weight(shape):
    n = len(shape)
    return pl.BlockSpec(shape, lambda *_: (0,) * n, pipeline_mode=pl.Buffered(1))


def _rows(tm, width):
    return pl.BlockSpec((tm, width), lambda i: (i, 0))


def _swap32(x):
    lane = lax.broadcasted_iota(jnp.int32, x.shape, 1)
    first = (lane % HEAD_DIM) < (HEAD_DIM // 2)
    return jnp.where(first, pltpu.roll(x, LANES - 32, axis=1), pltpu.roll(x, 32, axis=1))


def _rotary_fwd(x, cos, sin_signed):
    parts = []
    for j in range(x.shape[1] // LANES):
        xc = x[:, j * LANES:(j + 1) * LANES]
        parts.append(xc * cos + _swap32(xc) * sin_signed)
    return jnp.concatenate(parts, axis=1)


def _rotary_bwd(dy, cos, sin_signed):
    parts = []
    for j in range(dy.shape[1] // LANES):
        dc = dy[:, j * LANES:(j + 1) * LANES]
        parts.append(dc * cos + _swap32(dc * sin_signed))
    return jnp.concatenate(parts, axis=1)


def _rope_tables(seq):
    half = HEAD_DIM // 2
    inv_freq = ROPE_THETA ** (-jnp.arange(half, dtype=F32) / half)
    ang = jnp.arange(seq, dtype=F32)[:, None] * inv_freq[None, :]
    cos, sin = jnp.cos(ang), jnp.sin(ang)
    cos_t = jnp.tile(cos, (1, LANES // half))
    sin_t = jnp.tile(jnp.concatenate([-sin, sin], axis=1), (1, LANES // HEAD_DIM))
    return cos_t, sin_t


def cast_bf16(w, name):
    r, c = w.shape
    half = r // 2

    def body(w_ref, o_ref):
        o_ref[...] = w_ref[...].astype(BF16)

    return pl.pallas_call(
        body, name=name, grid=(2,),
        in_specs=[pl.BlockSpec((half, c), lambda i: (i, 0))],
        out_specs=pl.BlockSpec((None, half, c), lambda i: (i, 0, 0)),
        out_shape=jax.ShapeDtypeStruct((2, half, c), BF16),
        compiler_params=_cparams("parallel"),
    )(w)


def _mesh_pos():
    return lax.axis_index("x"), lax.axis_index("y"), lax.axis_index("c")


GATHER_COPIES = 7


def _gather_phases(x_refs, out_refs, send_sems, recv_sems, local_sems):
    n = len(x_refs)
    x, y, cc = _mesh_pos()
    me, sibling = (x, y, cc), (x, y, 1 - cc)
    chips = [(1 - x, y), (x, 1 - y), (1 - x, 1 - y)]

    def rows(i, px, py, pc):
        return out_refs[i].at[4 * px + 2 * py + pc]

    def copy(i, k, block, to, src=None):
        return pltpu.make_async_remote_copy(
            src_ref=rows(i, *block) if src is None else src, dst_ref=rows(i, *block),
            send_sem=send_sems.at[GATHER_COPIES * i + k], recv_sem=recv_sems.at[GATHER_COPIES * i + k],
            device_id=to, device_id_type=MESH_ID)

    def local(i):
        return pltpu.make_async_copy(x_refs[i].at[cc], rows(i, *me), local_sems.at[i])

    def first(i):
        mine = x_refs[i].at[cc]
        return [copy(i, 0, me, sibling, src=mine)] + [
            copy(i, 1 + j, me, (*chip, cc), src=mine) for j, chip in enumerate(chips)]

    def passed(i):
        return [copy(i, 4 + j, (*chip, cc), sibling) for j, chip in enumerate(chips)]

    def start():
        for i in range(n):
            local(i).start()
            for cp in first(i):
                cp.start()

    def forward():
        for i in range(n):
            onward = passed(i)
            for j, chip in enumerate(chips):
                copy(i, 1 + j, (*chip, cc), me).wait_recv()
                onward[j].start()

    def finish():
        for i in range(n):
            copy(i, 0, sibling, me).wait_recv()
            for j, chip in enumerate(chips):
                copy(i, 4 + j, (*chip, 1 - cc), me).wait_recv()
            for cp in first(i) + passed(i):
                cp.wait_send()
            local(i).wait()

    return start, forward, finish


def _gather_scratch(n):
    return [pltpu.SemaphoreType.DMA((GATHER_COPIES * n,)), pltpu.SemaphoreType.DMA((GATHER_COPIES * n,)),
            pltpu.SemaphoreType.DMA((n,))]


def _gathered_shape(halves):
    return jax.ShapeDtypeStruct((N_DEV,) + halves.shape[1:], halves.dtype)


def allgather_halves(halves, name):
    def body(x_ref, out_ref, send_sems, recv_sems, local_sems):
        start, forward, finish = _gather_phases([x_ref], [out_ref], send_sems, recv_sems, local_sems)
        start()
        forward()
        finish()

    return pl.pallas_call(
        body, name=name,
        in_specs=[pl.BlockSpec(memory_space=pl.ANY)],
        out_specs=pl.BlockSpec(memory_space=pl.ANY),
        out_shape=_gathered_shape(halves),
        scratch_shapes=_gather_scratch(1),
    )(halves)


def _rms(x):
    return lax.rsqrt(jnp.mean(x * x, axis=-1, keepdims=True) + NORM_EPS)


def in_proj(x, norm1_w, w_in4, cos_t, sin_t, weight_halves=()):
    seq = x.shape[0]
    tm = WIDE_ROW_TILE
    cw = w_in4.shape[2]
    n_w = len(weight_halves)
    steps = seq // tm
    major = ATTN_MAJOR_DILATION
    slabs = ATTN_WIDTH // LANES

    def body(*refs):
        x_ref, nw_ref, w_ref, cos_ref, sin_ref = refs[:5]
        q_ref, k_ref, v_ref, hg_ref, u_ref = refs[5 + n_w:10 + n_w]
        major_refs = refs[10 + n_w:13 + n_w]
        slab_scr = refs[13 + 2 * n_w]
        step = pl.program_id(0)
        if n_w:
            start, forward, finish = _gather_phases(
                refs[5:5 + n_w], refs[13 + n_w:13 + 2 * n_w], *refs[14 + 2 * n_w:])
            pl.when(step == 0)(start)
            pl.when(step == (3 * steps) // 4)(forward)
        xv = x_ref[...]
        u = ((xv * _rms(xv)) * nw_ref[...]).astype(BF16)
        u_ref[...] = u
        proj = jnp.concatenate([_dot(u, w_ref[j]) for j in range(N_CHIPS)], axis=1)
        cos, sin = cos_ref[...], sin_ref[...]
        a = ATTN_WIDTH
        qkv = (_rotary_fwd(proj[:, :a], cos, sin), _rotary_fwd(proj[:, a:2 * a], cos, sin), proj[:, 2 * a:3 * a])
        for ref, val in zip((q_ref, k_ref, v_ref), qkv):
            ref[...] = val
        hg_ref[...] = proj[:, 3 * a:]
        for idx, val in enumerate(qkv):
            for s in range(slabs):
                slab_scr[idx, s] = val[:, s * LANES:(s + 1) * LANES]
        for idx, out in enumerate(major_refs):
            for r in range(major):
                for s in range(slabs):
                    out[r, :, s * LANES:(s + 1) * LANES] = (
                        slab_scr.at[idx, s][pl.ds(r, tm // major, stride=major), :].astype(BF16))
        if n_w:
            pl.when(step == steps - 1)(finish)

    anywhere = pl.BlockSpec(memory_space=pl.ANY)
    major_spec = pl.BlockSpec((major, tm // major, ATTN_WIDTH), lambda i: (0, i, 0))
    return pl.pallas_call(
        body, name="in_proj", grid=(steps,),
        in_specs=[_rows(tm, D_MODEL), _full((1, D_MODEL)), _weight((N_CHIPS, D_MODEL, cw)),
                  _rows(tm, LANES), _rows(tm, LANES)] + [anywhere] * n_w,
        out_specs=[_rows(tm, ATTN_WIDTH)] * 3 + [_rows(tm, 4 * HGRN_WIDTH), _rows(tm, D_MODEL)]
        + [major_spec] * 3 + [anywhere] * n_w,
        out_shape=[jax.ShapeDtypeStruct((seq, ATTN_WIDTH), F32)] * 3
        + [jax.ShapeDtypeStruct((seq, 4 * HGRN_WIDTH), F32), jax.ShapeDtypeStruct((seq, D_MODEL), BF16)]
        + [jax.ShapeDtypeStruct((major, seq // major, ATTN_WIDTH), BF16)] * 3
        + [_gathered_shape(h) for h in weight_halves],
        scratch_shapes=[pltpu.VMEM((3, slabs, tm, LANES), F32)] + (_gather_scratch(n_w) if n_w else []),
        compiler_params=_cparams("arbitrary"),
    )(x, norm1_w, w_in4, cos_t, sin_t, *weight_halves)


def _head_masks():
    lane = lax.broadcasted_iota(jnp.int32, (1, LANES), 1)
    return [(lane // HEAD_DIM) == h for h in range(LANES // HEAD_DIM)]


def _window_valid(no_prev):
    qi = lax.broadcasted_iota(jnp.int32, (ATTN_BLOCK, 2 * ATTN_BLOCK), 0)
    kj = lax.broadcasted_iota(jnp.int32, (ATTN_BLOCK, 2 * ATTN_BLOCK), 1)
    valid = (kj >= qi) & (kj <= qi + ATTN_BLOCK)
    return valid & (jnp.logical_not(no_prev) | (kj >= ATTN_BLOCK))


def _strided_rows(start, dilation):
    if dilation == 1:
        return pl.ds(start, ATTN_BLOCK)
    return pl.ds(start, ATTN_BLOCK, stride=dilation)


def _block_before(edge_ref, cur_ref, t, r, span, dilation, per_step):
    edge = edge_ref[_strided_rows(ATTN_STEP_ROWS - span + r, dilation), :]
    if per_step == 1:
        return edge
    inside = cur_ref[_strided_rows(r + span * jnp.maximum(t - 1, 0), dilation), :]
    return jnp.where(t == 0, edge, inside)


def _attn_specs():
    cur = pl.BlockSpec((ATTN_STEP_ROWS, LANES), lambda hp, j: (j, hp))
    prev = pl.BlockSpec((ATTN_STEP_ROWS, LANES), lambda hp, j: (jnp.maximum(j - 1, 0), hp))
    return cur, prev


def _for_each_block(dilation, unroll, block):
    span = ATTN_BLOCK * dilation
    per_step = ATTN_STEP_ROWS // span

    def trip(it, carry):
        block(it // dilation, it % dilation, span, per_step)
        return carry

    lax.fori_loop(0, per_step * dilation, trip, 0, unroll=unroll)


def _load_qkv(natural, major, t, r, span, dilation, per_step):
    if dilation == ATTN_MAJOR_DILATION:
        q_ref, kc_ref, vc_ref, kp_ref, vp_ref = major
        return (q_ref[r] * ATTN_SCALE, jnp.concatenate([kp_ref[r], kc_ref[r]], axis=0),
                jnp.concatenate([vp_ref[r], vc_ref[r]], axis=0))
    q_ref, kc_ref, vc_ref, kp_ref, vp_ref = natural
    rows = _strided_rows(r + span * t, dilation)
    kp = _block_before(kp_ref, kc_ref, t, r, span, dilation, per_step)
    vp = _block_before(vp_ref, vc_ref, t, r, span, dilation, per_step)
    return ((q_ref[rows, :] * ATTN_SCALE).astype(BF16),
            jnp.concatenate([kp, kc_ref[rows, :]], axis=0).astype(BF16),
            jnp.concatenate([vp, vc_ref[rows, :]], axis=0).astype(BF16))


def _major_specs():
    assert ATTN_STEP_ROWS == ATTN_BLOCK * ATTN_MAJOR_DILATION
    shape = (ATTN_MAJOR_DILATION, ATTN_BLOCK, LANES)
    return (pl.BlockSpec(shape, lambda hp, j: (0, j, hp)),
            pl.BlockSpec(shape, lambda hp, j: (0, jnp.maximum(j - 1, 0), hp)))


def attn_fwd(q, k, v, q16, k16, v16):
    seq = q.shape[0]
    cur, prev = _attn_specs()
    cur16, prev16 = _major_specs()

    def body(*refs):
        natural, major, (y_ref, lse_ref) = refs[:5], refs[5:10], refs[10:]
        first_step = pl.program_id(1) == 0
        masks = _head_masks()
        for index, (_, dilation) in enumerate(DILATED_PAIRS):
            def block(t, r, span, per_step, dilation=dilation, merge=index > 0):
                rows = _strided_rows(r + span * t, dilation)
                q2, k2, v2 = _load_qkv(natural, major, t, r, span, dilation, per_step)
                valid = _window_valid(first_step & (t == 0))
                o_acc = jnp.zeros((ATTN_BLOCK, LANES), F32)
                l_acc = jnp.zeros((ATTN_BLOCK, LANES), F32)
                for mh in masks:
                    qm = jnp.where(mh, q2, jnp.zeros_like(q2))
                    s = jnp.where(valid, _dot_nt(qm, k2), NEG_BIG)
                    m = jnp.max(s, axis=-1, keepdims=True)
                    p = jnp.exp(s - m)
                    l = jnp.sum(p, axis=-1, keepdims=True)
                    o = _dot(p.astype(BF16), v2) / l
                    o_acc = jnp.where(mh, o, o_acc)
                    l_acc = jnp.where(mh, m + jnp.log(l), l_acc)
                if merge:
                    y_old, l_old = y_ref[rows, :], lse_ref[rows, :]
                    mx = jnp.maximum(l_old, l_acc)
                    e_old, e_new = jnp.exp(l_old - mx), jnp.exp(l_acc - mx)
                    den = e_old + e_new
                    o_acc = (y_old * e_old + o_acc * e_new) / den
                    l_acc = mx + jnp.log(den)
                y_ref[rows, :] = o_acc
                lse_ref[rows, :] = l_acc

            _for_each_block(dilation, ATTN_FWD_UNROLL, block)

    return pl.pallas_call(
        body, name="attn_fwd", grid=(ATTN_WIDTH // LANES, seq // ATTN_STEP_ROWS),
        in_specs=[cur, cur, cur, prev, prev, cur16, cur16, cur16, prev16, prev16],
        out_specs=[cur, cur],
        out_shape=[jax.ShapeDtypeStruct((seq, ATTN_WIDTH), F32)] * 2,
        compiler_params=_cparams("parallel", "parallel"),
    )(q, k, v, k, v, q16, k16, v16, k16, v16)


def _chunk_cumsum(x, reverse=False):
    rc = lax.broadcasted_iota(jnp.int32, x.shape, 0) % HGRN_CHUNK
    sh = 1
    while sh < HGRN_CHUNK:
        if reverse:
            x = x + jnp.where(rc + sh < HGRN_CHUNK, pltpu.roll(x, x.shape[0] - sh, axis=0), 0.0)
        else:
            x = x + jnp.where(rc >= sh, pltpu.roll(x, sh, axis=0), 0.0)
        sh *= 2
    return x


def _chunk_row(x, row):
    return _chunk_rows([x[n * HGRN_CHUNK + row:n * HGRN_CHUNK + row + 1, :]
                        for n in range(x.shape[0] // HGRN_CHUNK)])


def _chunk_rows(rows):
    return jnp.concatenate([jnp.broadcast_to(r, (HGRN_CHUNK, r.shape[1])) for r in rows], axis=0)


def _hgrn_prep(hg, lbl):
    w = HGRN_WIDTH
    a0, a1 = lbl[0:1, :], lbl[1:2, :]
    mx = jnp.maximum(a0, a1)
    e0, e1 = jnp.exp(a0 - mx), jnp.exp(a1 - mx)
    lb = e0 / (e0 + e1)
    qb, fb, gb = hg[:, :w], hg[:, w:2 * w], hg[:, 3 * w:]
    sg = _sigmoid(fb)
    f = lb + (1.0 - lb) * sg
    b = _chunk_cumsum(jnp.log(f))
    bmid, btot = _chunk_row(b, HGRN_CHUNK // 2 - 1), _chunk_row(b, HGRN_CHUNK - 1)
    sq = _sigmoid(qb)
    p = dict(lb=lb, sg=sg, f=f, kk=1.0 - f, sq=sq, qf=qb * sq, gb=gb,
             e_iq=jnp.exp(b - bmid), e_ik=jnp.exp(bmid - b), e_b=jnp.exp(b),
             e_bb=jnp.exp(btot - b), e_tot=jnp.exp(btot))
    p["qi"] = p["qf"] * p["e_iq"]
    p["ki"] = p["kk"] * p["e_ik"]
    p["qs"] = p["qf"] * p["e_b"]
    p["kb"] = p["kk"] * p["e_bb"]
    return p


def _chunk_masks():
    t = lax.broadcasted_iota(jnp.int32, (HGRN_ROWS, HGRN_ROWS), 0)
    s = lax.broadcasted_iota(jnp.int32, (HGRN_ROWS, HGRN_ROWS), 1)
    tril = ((t // HGRN_CHUNK) == (s // HGRN_CHUNK)) & (s <= t)
    n_chunks = HGRN_ROWS // HGRN_CHUNK
    tt = lax.broadcasted_iota(jnp.int32, (HGRN_ROWS, n_chunks * LANES), 0)
    cc = lax.broadcasted_iota(jnp.int32, (HGRN_ROWS, n_chunks * LANES), 1)
    block = (tt // HGRN_CHUNK) == (cc // LANES)
    return tril, block


def _spread(x, block):
    n_chunks = HGRN_ROWS // HGRN_CHUNK
    return jnp.where(block, jnp.tile(x, (1, n_chunks)), jnp.zeros((), x.dtype))


def _fold(x_full, block):
    n_chunks = HGRN_ROWS // HGRN_CHUNK
    z = jnp.where(block, x_full, 0.0)
    acc = z[:, :LANES]
    for n in range(1, n_chunks):
        acc = acc + z[:, n * LANES:(n + 1) * LANES]
    return acc


def hgrn_fwd(hg, lb_logits, hnw, weight_halves=()):
    seq = hg.shape[0]
    nblk = seq // HGRN_ROWS
    n_steps = nblk // HGRN_STEP_BLOCKS
    step_rows = HGRN_ROWS * HGRN_STEP_BLOCKS
    n_chunks = HGRN_ROWS // HGRN_CHUNK
    n_w = len(weight_halves)

    def body(*refs):
        hg_ref, lbl_ref, hnw_ref = refs[:3]
        w_refs = refs[3:3 + n_w]
        yb_ref, o_ref, st0_ref = refs[3 + n_w:6 + n_w]
        g_refs = refs[6 + n_w:6 + 2 * n_w]
        st_scr = refs[6 + 2 * n_w]
        step = pl.program_id(0)
        if n_w:
            start, forward, finish = _gather_phases(w_refs, g_refs, *refs[7 + 2 * n_w:])
            pl.when(step == 0)(start)
            pl.when(step == (3 * n_steps) // 4)(forward)

        @pl.when(step == 0)
        def _():
            st_scr[...] = jnp.zeros_like(st_scr)

        tril, block = _chunk_masks()
        for sub in range(HGRN_STEP_BLOCKS):
            rows = slice(sub * HGRN_ROWS, (sub + 1) * HGRN_ROWS)
            hg_v = hg_ref[rows, :]
            p = _hgrn_prep(hg_v, lbl_ref[...])
            vv = hg_v[:, 2 * HGRN_WIDTH:3 * HGRN_WIDTH].astype(BF16)
            outs = []
            for h in range(HGRN_HEADS):
                sl = slice(h * LANES, (h + 1) * LANES)
                v_h = vv[:, sl]
                a = jnp.where(tril, _dot_nt(p["qi"][:, sl].astype(BF16), p["ki"][:, sl].astype(BF16)), 0.0)
                o = _dot(a.astype(BF16), v_h)
                upd = _dot_tn(v_h, _spread(p["kb"][:, sl].astype(BF16), block))
                st = st_scr[h]
                st0_ref[sub, h] = st
                parts = []
                for n in range(n_chunks):
                    parts.append(st.astype(BF16))
                    decay = p["e_tot"][n * HGRN_CHUNK:n * HGRN_CHUNK + 1, sl]
                    st = st * decay + upd[:, n * LANES:(n + 1) * LANES]
                st_scr[h] = st
                o = o + _dot_nt(_spread(p["qs"][:, sl].astype(BF16), block), jnp.concatenate(parts, axis=1))
                outs.append(o)
            o_ref[rows, :] = jnp.concatenate(outs, axis=1)
            normed = jnp.concatenate(
                [outs[h] * _rms(outs[h]) for h in range(HGRN_HEADS)], axis=1)
            gb = p["gb"]
            yb_ref[rows, :] = (normed * hnw_ref[...]) * (gb * _sigmoid(gb))
        if n_w:
            pl.when(step == n_steps - 1)(finish)

    anywhere = pl.BlockSpec(memory_space=pl.ANY)
    return pl.pallas_call(
        body, name="hgrn_fwd", grid=(n_steps,),
        in_specs=[_rows(step_rows, 4 * HGRN_WIDTH), _full((2, HGRN_WIDTH)), _full((1, HGRN_WIDTH))]
        + [anywhere] * n_w,
        out_specs=[_rows(step_rows, HGRN_WIDTH), _rows(step_rows, HGRN_WIDTH),
                   pl.BlockSpec((HGRN_STEP_BLOCKS, HGRN_HEADS, LANES, LANES), lambda i: (i, 0, 0, 0))]
        + [anywhere] * n_w,
        out_shape=[jax.ShapeDtypeStruct((seq, HGRN_WIDTH), F32)] * 2
        + [jax.ShapeDtypeStruct((nblk, HGRN_HEADS, LANES, LANES), F32)]
        + [_gathered_shape(h) for h in weight_halves],
        scratch_shapes=[pltpu.VMEM((HGRN_HEADS, LANES, LANES), F32)] + (_gather_scratch(n_w) if n_w else []),
        compiler_params=_cparams("arbitrary"),
    )(hg, lb_logits, hnw, *weight_halves)


def ffn_fwd(ya, yb, x, w_out, norm2_w, w_gu4, w_down, final_w, target):
    seq = x.shape[0]
    tm = ROW_TILE
    cw = w_gu4.shape[2]
    inv_d = 1.0 / D_MODEL

    def body(ya_ref, yb_ref, x_ref, wo_ref, nw_ref, wgu_ref, wd_ref, fw_ref, t_ref,
             mixed_ref, h1_ref, u2_ref, g_ref, up_ref, act_ref, dh2_ref, acc_ref):
        @pl.when(pl.program_id(0) == 0)
        def _():
            acc_ref[...] = jnp.zeros_like(acc_ref)

        mixed = jnp.concatenate([ya_ref[...], yb_ref[...]], axis=1).astype(BF16)
        mixed_ref[...] = mixed
        h1 = x_ref[...] + _dot(mixed, wo_ref[...])
        h1_ref[...] = h1
        u2 = ((h1 * _rms(h1)) * nw_ref[...]).astype(BF16)
        u2_ref[...] = u2
        g = jnp.concatenate([_dot(u2, wgu_ref[0]), _dot(u2, wgu_ref[1])], axis=1)
        up = jnp.concatenate([_dot(u2, wgu_ref[2]), _dot(u2, wgu_ref[3])], axis=1)
        g_ref[...] = g.astype(BF16)
        up_ref[...] = up.astype(BF16)
        act = ((g * (0.5 * jnp.tanh(0.5 * g) + 0.5)) * up).astype(BF16)
        act_ref[...] = act
        h2 = h1 + _dot(act, wd_ref[...])
        rf = _rms(h2)
        n = h2 * rf
        fw = fw_ref[...]
        err = n * fw - t_ref[...]
        dy = err * inv_d
        acc_ref[0:1, :] += jnp.sum(dy * n, axis=0, keepdims=True)
        acc_ref[1:2, :] += (0.5 * inv_d) * jnp.sum(err * err, axis=0, keepdims=True)
        dn = dy * fw
        dh2_ref[...] = rf * (dn - n * jnp.mean(dn * n, axis=-1, keepdims=True))

    half = _rows(tm, ATTN_WIDTH)
    wide = _rows(tm, D_MODEL)
    ffn = _rows(tm, FFN_HIDDEN)
    return pl.pallas_call(
        body, name="ffn_fwd", grid=(seq // tm,),
        in_specs=[half, half, wide, _weight((D_MODEL, D_MODEL)), _full((1, D_MODEL)),
                  _weight((N_CHIPS, D_MODEL, cw)), _weight((FFN_HIDDEN, D_MODEL)), _full((1, D_MODEL)), wide],
        out_specs=[wide, wide, wide, ffn, ffn, ffn, wide, _full((8, D_MODEL))],
        out_shape=[jax.ShapeDtypeStruct((seq, D_MODEL), BF16), jax.ShapeDtypeStruct((seq, D_MODEL), F32),
                   jax.ShapeDtypeStruct((seq, D_MODEL), BF16)]
        + [jax.ShapeDtypeStruct((seq, FFN_HIDDEN), BF16)] * 3
        + [jax.ShapeDtypeStruct((seq, D_MODEL), F32), jax.ShapeDtypeStruct((8, D_MODEL), F32)],
        compiler_params=_cparams("arbitrary"),
    )(ya, yb, x, w_out, norm2_w, w_gu4, w_down, final_w, target)


def _head_sum_matrix():
    i = jnp.arange(ATTN_WIDTH)
    return ((i[:, None] // HEAD_DIM) == (i[None, :] // HEAD_DIM)).astype(BF16)


def ffn_bwd(dh2, w_down, g, up, w_gu4, h1, norm2_w, w_out, ya):
    seq = h1.shape[0]
    tm = ROW_TILE
    cw = w_gu4.shape[2]
    hsum = _head_sum_matrix()

    def body(dh2_ref, wd_ref, g_ref, up_ref, w_ref, h1_ref, nw_ref, wo_ref, ya_ref, hs_ref,
             dgu_ref, dh1_ref, dya_ref, dyb_ref, delta_ref, acc_ref):
        @pl.when(pl.program_id(0) == 0)
        def _():
            acc_ref[...] = jnp.zeros_like(acc_ref)

        dh2_b = dh2_ref[...].astype(BF16)
        du2 = jnp.zeros((tm, D_MODEL), F32)
        for j in range(N_CHIPS // 2):
            dact = _dot_nt(dh2_b, wd_ref[j * cw:(j + 1) * cw, :])
            gv = g_ref[:, j * cw:(j + 1) * cw].astype(F32)
            sg = _sigmoid(gv)
            dg = (dact * up_ref[:, j * cw:(j + 1) * cw].astype(F32) * (sg * (1.0 + gv * (1.0 - sg)))).astype(BF16)
            dup = (dact * (gv * sg)).astype(BF16)
            dgu_ref[:, j * cw:(j + 1) * cw] = dg
            dgu_ref[:, FFN_HIDDEN + j * cw:FFN_HIDDEN + (j + 1) * cw] = dup
            du2 = du2 + _dot_nt(dg, w_ref[j]) + _dot_nt(dup, w_ref[N_CHIPS // 2 + j])
        h1 = h1_ref[...]
        r2 = _rms(h1)
        nh = h1 * r2
        acc_ref[0:1, :] += jnp.sum(du2 * nh, axis=0, keepdims=True)
        dn = du2 * nw_ref[...]
        dh1 = dh2_ref[...] + r2 * (dn - nh * jnp.mean(dn * nh, axis=-1, keepdims=True))
        dh1_ref[...] = dh1
        dmixed = _dot_nt(dh1.astype(BF16), wo_ref[...])
        dya = dmixed[:, :ATTN_WIDTH]
        dya_ref[...] = dya
        dyb_ref[...] = dmixed[:, ATTN_WIDTH:]
        prod = dya * ya_ref[...]
        hi = prod.astype(BF16)
        lo = (prod - hi.astype(F32)).astype(BF16)
        delta_ref[...] = _dot(hi, hs_ref[...]) + _dot(lo, hs_ref[...])

    wide = _rows(tm, D_MODEL)
    half = _rows(tm, ATTN_WIDTH)
    ffn = _rows(tm, FFN_HIDDEN)
    return pl.pallas_call(
        body, name="ffn_bwd", grid=(seq // tm,),
        in_specs=[wide, _weight((FFN_HIDDEN, D_MODEL)), ffn, ffn, _weight((N_CHIPS, D_MODEL, cw)), wide,
                  _full((1, D_MODEL)), _weight((D_MODEL, D_MODEL)), half, _full((ATTN_WIDTH, ATTN_WIDTH))],
        out_specs=[_rows(tm, 2 * FFN_HIDDEN), wide, half, half, half, _full((8, D_MODEL))],
        out_shape=[jax.ShapeDtypeStruct((seq, 2 * FFN_HIDDEN), BF16), jax.ShapeDtypeStruct((seq, D_MODEL), F32)]
        + [jax.ShapeDtypeStruct((seq, ATTN_WIDTH), F32)] * 3 + [jax.ShapeDtypeStruct((8, D_MODEL), F32)],
        compiler_params=_cparams("arbitrary"),
    )(dh2, w_down, g, up, w_gu4, h1, norm2_w, w_out, ya, hsum)


def attn_bwd(q, k, v, q16, k16, v16, dy, lse, delta, sibling_grads=()):
    seq = q.shape[0]
    cur, prev = _attn_specs()
    cur16, prev16 = _major_specs()
    whole = pl.BlockSpec((seq, LANES), lambda hp, j: (0, hp))
    n_g = len(sibling_grads)
    n_hp, n_steps = ATTN_WIDTH // LANES, seq // ATTN_STEP_ROWS
    n_in = 13

    def body(*refs):
        natural, major = refs[:5], refs[5:10]
        dy_ref, lse_ref, dl_ref = refs[10:n_in]
        dq_ref, dk_ref, dv_ref = refs[n_in + n_g:n_in + 3 + n_g]
        first_step = pl.program_id(1) == 0
        base = pl.program_id(1) * ATTN_STEP_ROWS
        masks = _head_masks()
        if n_g:
            start, finish = _sibling_exchange_phases(
                refs[n_in:n_in + n_g], refs[n_in + 3 + n_g:n_in + 3 + 2 * n_g], *refs[n_in + 3 + 2 * n_g:])
            pl.when((pl.program_id(0) == 0) & first_step)(start)

        def block(t, r, span, per_step, dilation, add):
            rows = _strided_rows(r + span * t, dilation)
            at_edge = t == 0
            q2, k2, v2 = _load_qkv(natural, major, t, r, span, dilation, per_step)
            dy2 = dy_ref[rows, :].astype(BF16)
            lse2, dl2 = lse_ref[rows, :], dl_ref[rows, :]
            valid = _window_valid(first_step & at_edge)
            zero = jnp.zeros_like(q2)
            qms, dyms, ps, dss, kms = [], [], [], [], []
            for h, mh in enumerate(masks):
                c0 = h * HEAD_DIM
                qm, dym = jnp.where(mh, q2, zero), jnp.where(mh, dy2, zero)
                s = _dot_nt(qm, k2)
                p = jnp.where(valid, jnp.exp(s - lse2[:, c0:c0 + 1]), 0.0)
                dp = _dot_nt(dym, v2)
                dss.append((p * (dp - dl2[:, c0:c0 + 1])).astype(BF16))
                ps.append(p.astype(BF16))
                qms.append(qm)
                dyms.append(dym)
                kms.append(jnp.where(mh, k2, jnp.zeros_like(k2)))
            dq = _dot(jnp.concatenate(dss, axis=1), jnp.concatenate(kms, axis=0)) * ATTN_SCALE
            p_all, ds_all = jnp.concatenate(ps, axis=0), jnp.concatenate(dss, axis=0)
            dy_all, q_all = jnp.concatenate(dyms, axis=0), jnp.concatenate(qms, axis=0)
            dv_full, dk_full = _dot_tn(dy_all, p_all).T, _dot_tn(q_all, ds_all).T
            here = _strided_rows(base + r + span * t, dilation)
            if add:
                dq_ref[rows, :] += dq
                dk_ref[here, :] += dk_full[ATTN_BLOCK:]
                dv_ref[here, :] += dv_full[ATTN_BLOCK:]
            else:
                dq_ref[rows, :] = dq
                dk_ref[here, :] = dk_full[ATTN_BLOCK:]
                dv_ref[here, :] = dv_full[ATTN_BLOCK:]
            back = _strided_rows(jnp.maximum(base + r + span * t - span, r), dilation)
            dk_ref[back, :] += dk_full[:ATTN_BLOCK]
            dv_ref[back, :] += dv_full[:ATTN_BLOCK]

        for index, (_, dilation) in enumerate(DILATED_PAIRS):
            _for_each_block(dilation, ATTN_BWD_UNROLL,
                            functools.partial(block, dilation=dilation, add=index > 0))
        if n_g:
            pl.when((pl.program_id(0) == n_hp - 1) & (pl.program_id(1) == n_steps - 1))(finish)

    anywhere = pl.BlockSpec(memory_space=pl.ANY)
    return pl.pallas_call(
        body, name="attn_bwd", grid=(n_hp, n_steps),
        in_specs=[cur, cur, cur, prev, prev, cur16, cur16, cur16, prev16, prev16, cur, cur, cur]
        + [anywhere] * n_g,
        out_specs=[cur, whole, whole] + [anywhere] * n_g,
        out_shape=[jax.ShapeDtypeStruct((seq, ATTN_WIDTH), F32)] * 3 + _sibling_exchange_shapes(sibling_grads),
        scratch_shapes=_sibling_exchange_scratch(n_g) if n_g else [],
        compiler_params=_cparams("arbitrary", "arbitrary"),
    )(q, k, v, k, v, q16, k16, v16, k16, v16, dy, lse, delta, *sibling_grads)


def hgrn_bwd(hg, lb_logits, hnw, o_pre, st0, dyb, chip_sums=()):
    seq = hg.shape[0]
    step_rows = HGRN_ROWS * HGRN_STEP_BLOCKS
    n_steps = seq // step_rows
    n_chunks = HGRN_ROWS // HGRN_CHUNK
    w = HGRN_WIDTH
    n_s = len(chip_sums)

    def body(*refs):
        hg_ref, lbl_ref, hnw_ref, o_ref, st0_ref, dyb_ref = refs[:6]
        dhg_ref, acc_ref = refs[6 + n_s:8 + n_s]
        dst_scr = refs[8 + 2 * n_s]
        step = pl.program_id(0)
        if n_s:
            start, finish = _chip_exchange_phases(refs[6:6 + n_s], refs[8 + n_s:8 + 2 * n_s], *refs[9 + 2 * n_s:])
            pl.when(step == 0)(start)

        @pl.when(step == 0)
        def _():
            dst_scr[...] = jnp.zeros_like(dst_scr)
            acc_ref[...] = jnp.zeros_like(acc_ref)

        tril, block = _chunk_masks()
        for sub in reversed(range(HGRN_STEP_BLOCKS)):
            rows = slice(sub * HGRN_ROWS, (sub + 1) * HGRN_ROWS)
            hg_v = hg_ref[rows, :]
            p = _hgrn_prep(hg_v, lbl_ref[...])
            vv = hg_v[:, 2 * w:3 * w].astype(BF16)
            hnw_v = hnw_ref[...]
            gb = p["gb"]
            sgg = _sigmoid(gb)
            silu_g = gb * sgg
            dyb_v = dyb_ref[rows, :]
            o_v = o_ref[rows, :]

            d_on = dyb_v * hnw_v * silu_g
            on_parts, do_parts = [], []
            for h in range(HGRN_HEADS):
                sl = slice(h * LANES, (h + 1) * LANES)
                rs = _rms(o_v[:, sl])
                on = o_v[:, sl] * rs
                on_parts.append(on)
                do_parts.append(rs * (d_on[:, sl] - on * jnp.mean(d_on[:, sl] * on, axis=-1, keepdims=True)))
            on_all = jnp.concatenate(on_parts, axis=1)
            dgb = dyb_v * on_all * hnw_v * (sgg * (1.0 + gb * (1.0 - sgg)))
            acc_ref[0:1, :] += jnp.sum(dyb_v * on_all * silu_g, axis=0, keepdims=True)

            dqf_parts, dkk_parts, db_parts, dv_parts, dbt_parts, dkbkb_parts = [], [], [], [], [], []
            for h in range(HGRN_HEADS):
                sl = slice(h * LANES, (h + 1) * LANES)
                v_h = vv[:, sl]
                do_h = do_parts[h].astype(BF16)
                qi, ki, qs, kb = p["qi"][:, sl], p["ki"][:, sl], p["qs"][:, sl], p["kb"][:, sl]
                qi_b, ki_b = qi.astype(BF16), ki.astype(BF16)
                kb_cat = _spread(kb.astype(BF16), block)
                qs_cat = _spread(qs.astype(BF16), block)
                upd = _dot_tn(v_h, kb_cat)
                st = st0_ref[sub, h]
                st_parts = []
                for n in range(n_chunks):
                    st_parts.append(st)
                    decay = p["e_tot"][n * HGRN_CHUNK:n * HGRN_CHUNK + 1, sl]
                    st = st * decay + upd[:, n * LANES:(n + 1) * LANES]
                st_cat = jnp.concatenate([s_.astype(BF16) for s_ in st_parts], axis=1)
                wgt = _dot_tn(do_h, qs_cat)
                dst = dst_scr[h]
                dst_parts = [None] * n_chunks
                dbt_rows = [None] * n_chunks
                for n in reversed(range(n_chunks)):
                    dst_parts[n] = dst.astype(BF16)
                    decay = p["e_tot"][n * HGRN_CHUNK:n * HGRN_CHUNK + 1, sl]
                    dbt_rows[n] = jnp.sum(dst * st_parts[n], axis=0, keepdims=True) * decay
                    dst = dst * decay + wgt[:, n * LANES:(n + 1) * LANES]
                dst_scr[h] = dst
                dst_cat = jnp.concatenate(dst_parts, axis=1)
                dqs = _fold(_dot(do_h, st_cat), block)
                dkb = _fold(_dot(v_h, dst_cat), block)
                dv_state = _dot_nt(kb_cat, dst_cat)
                a = jnp.where(tril, _dot_nt(qi_b, ki_b), 0.0).astype(BF16)
                da = jnp.where(tril, _dot_nt(do_h, v_h), 0.0).astype(BF16)
                dv_parts.append(_dot_tn(a, do_h) + dv_state)
                dqi = _dot(da, ki_b)
                dki = _dot_tn(da, qi_b)
                dqf_parts.append(dqi * p["e_iq"][:, sl] + dqs * p["e_b"][:, sl])
                dkk_parts.append(dki * p["e_ik"][:, sl] + dkb * p["e_bb"][:, sl])
                dkbkb = dkb * kb
                db_parts.append(dqi * qi - dki * ki + dqs * qs - dkbkb)
                dkbkb_parts.append(dkbkb)
                dbt_parts.append(_chunk_rows(dbt_rows))

            cat = lambda parts: jnp.concatenate(parts, axis=1)
            dlogf = (_chunk_cumsum(cat(db_parts), reverse=True)
                     + _chunk_row(_chunk_cumsum(cat(dkbkb_parts)), HGRN_CHUNK - 1) + cat(dbt_parts))
            sq, qb = p["sq"], hg_v[:, :w]
            dqb = cat(dqf_parts) * (sq * (1.0 + qb * (1.0 - sq)))
            df = dlogf / p["f"] - cat(dkk_parts)
            sg, lb = p["sg"], p["lb"]
            dfb = df * (1.0 - lb) * sg * (1.0 - sg)
            acc_ref[1:2, :] += jnp.sum(df * (1.0 - sg), axis=0, keepdims=True)
            dhg_ref[rows, :] = jnp.concatenate([dqb, dfb, cat(dv_parts), dgb], axis=1).astype(BF16)
        if n_s:
            pl.when(step == n_steps - 1)(finish)

    rev = lambda i: (n_steps - 1 - i, 0)
    anywhere = pl.BlockSpec(memory_space=pl.ANY)
    return pl.pallas_call(
        body, name="hgrn_bwd", grid=(n_steps,),
        in_specs=[pl.BlockSpec((step_rows, 4 * w), rev), _full((2, w)), _full((1, w)),
                  pl.BlockSpec((step_rows, w), rev),
                  pl.BlockSpec((HGRN_STEP_BLOCKS, HGRN_HEADS, LANES, LANES), lambda i: (n_steps - 1 - i, 0, 0, 0)),
                  pl.BlockSpec((step_rows, w), rev)] + [anywhere] * n_s,
        out_specs=[pl.BlockSpec((step_rows, 4 * w), rev), _full((8, w))] + [anywhere] * n_s,
        out_shape=[jax.ShapeDtypeStruct((seq, 4 * w), BF16), jax.ShapeDtypeStruct((8, w), F32)]
        + [jax.ShapeDtypeStruct(s.shape, s.dtype) for s in chip_sums],
        scratch_shapes=[pltpu.VMEM((HGRN_HEADS, LANES, LANES), F32)] + (_chip_exchange_scratch(n_s) if n_s else []),
        compiler_params=_cparams("arbitrary"),
    )(hg, lb_logits, hnw, o_pre, st0, dyb, *chip_sums)


def in_bwd(dq, dk, dv, dhg, cos_t, sin_t, w_in4, x, norm1_w, dh1):
    seq = x.shape[0]
    tm = WIDE_ROW_TILE
    cw = w_in4.shape[2]
    steps = seq // tm
    slots = INPUT_RING_SLOTS

    def body(dq_ref, dk_ref, dv_ref, dhg_hbm, cos_ref, sin_ref, w_ref,
             x_ref, nw_ref, dh1_ref, dproj_ref, dx_ref, acc_ref, ring, ring_sems):
        step = pl.program_id(0)

        def fetch(s):
            return pltpu.make_async_copy(
                dhg_hbm.at[pl.ds(pl.multiple_of(s * tm, tm), tm), :], ring.at[s % slots], ring_sems.at[s % slots])

        @pl.when(step == 0)
        def _():
            acc_ref[...] = jnp.zeros_like(acc_ref)
            for s in range(min(slots - 1, steps)):
                fetch(s).start()

        @pl.when(step + slots - 1 < steps)
        def _():
            fetch(step + slots - 1).start()

        cos, sin = cos_ref[...], sin_ref[...]
        dqa = _rotary_bwd(dq_ref[...], cos, sin)
        dka = _rotary_bwd(dk_ref[...], cos, sin)
        front = jnp.concatenate([dqa, dka, dv_ref[...]], axis=1).astype(BF16)
        fetch(step).wait()
        dproj = jnp.concatenate([front, ring[step % slots]], axis=1)
        dproj_ref[...] = dproj
        du = _dot_nt(dproj[:, :cw], w_ref[0])
        for j in range(1, N_CHIPS):
            du = du + _dot_nt(dproj[:, j * cw:(j + 1) * cw], w_ref[j])
        xv = x_ref[...]
        r1 = _rms(xv)
        nx = xv * r1
        acc_ref[0:1, :] += jnp.sum(du * nx, axis=0, keepdims=True)
        dn = du * nw_ref[...]
        dx_ref[...] = dh1_ref[...] + r1 * (dn - nx * jnp.mean(dn * nx, axis=-1, keepdims=True))

    half = _rows(tm, ATTN_WIDTH)
    wide = _rows(tm, D_MODEL)
    return pl.pallas_call(
        body, name="in_bwd", grid=(steps,),
        in_specs=[half] * 3 + [pl.BlockSpec(memory_space=pl.ANY), _rows(tm, LANES), _rows(tm, LANES),
                               _weight((N_CHIPS, D_MODEL, cw)), wide, _full((1, D_MODEL)), wide],
        out_specs=[_rows(tm, IN_PROJ_WIDTH), wide, _full((8, D_MODEL))],
        out_shape=[jax.ShapeDtypeStruct((seq, IN_PROJ_WIDTH), BF16), jax.ShapeDtypeStruct((seq, D_MODEL), F32),
                   jax.ShapeDtypeStruct((8, D_MODEL), F32)],
        scratch_shapes=[pltpu.VMEM((slots, tm, 4 * HGRN_WIDTH), BF16), pltpu.SemaphoreType.DMA((slots,))],
        compiler_params=_cparams("arbitrary"),
    )(dq, dk, dv, dhg, cos_t, sin_t, w_in4, x, norm1_w, dh1)


def weight_grad(a, b, col_block, name, group=1, small_pack=None, rows_per_step=1024):
    seq, kdim = a.shape
    ndim = b.shape[1]
    nj = ndim // col_block
    tk = min(rows_per_step, seq)
    hosting = small_pack is not None
    n_j, n_t = nj // group, seq // tk

    def body(*refs):
        a_ref, b_ref = refs[:2]
        o_ref = refs[3] if hosting else refs[2]
        if hosting:
            start, finish = _pack_gather_phases(refs[2], refs[4], *refs[5:])
            pl.when((pl.program_id(0) == 0) & (pl.program_id(1) == 0))(start)

        @pl.when(pl.program_id(1) == 0)
        def _():
            o_ref[...] = jnp.zeros_like(o_ref)

        acc = _dot_tn(a_ref[...].astype(BF16), b_ref[...].astype(BF16))
        for i in range(group):
            o_ref[i] += acc[:, i * col_block:(i + 1) * col_block]
        if hosting:
            pl.when((pl.program_id(0) == n_j - 1) & (pl.program_id(1) == n_t - 1))(finish)

    anywhere = pl.BlockSpec(memory_space=pl.ANY)
    out = pl.pallas_call(
        body, name=name, grid=(n_j, n_t),
        in_specs=[pl.BlockSpec((tk, kdim), lambda j, t: (t, 0)),
                  pl.BlockSpec((tk, group * col_block), lambda j, t: (t, j))] + [anywhere] * hosting,
        out_specs=[pl.BlockSpec((group, kdim, col_block), lambda j, t: (j, 0, 0))] + [anywhere] * hosting,
        out_shape=[jax.ShapeDtypeStruct((nj, kdim, col_block), F32)]
        + ([jax.ShapeDtypeStruct((N_DEV,) + small_pack.shape, F32)] if hosting else []),
        scratch_shapes=[pltpu.SemaphoreType.DMA((N_DEV - 1,)), pltpu.SemaphoreType.DMA((N_DEV - 1,)),
                        pltpu.SemaphoreType.DMA] if hosting else [],
        compiler_params=_cparams("arbitrary", "arbitrary"),
    )(a, b, *([small_pack] if hosting else []))
    return out if hosting else out[0]


def _sibling_exchange_phases(g_refs, out_refs, send_sems, recv_sems):
    x, y, cc = _mesh_pos()

    def copies():
        return [pltpu.make_async_remote_copy(
            src_ref=g_refs[i].at[j, 1 - cc], dst_ref=out_refs[i].at[j],
            send_sem=send_sems.at[i * N_CHIPS + j], recv_sem=recv_sems.at[i * N_CHIPS + j],
            device_id=(x, y, 1 - cc), device_id_type=MESH_ID)
            for i in range(len(g_refs)) for j in range(N_CHIPS)]

    def start():
        for cp in copies():
            cp.start()

    def finish():
        for cp in copies():
            cp.wait_recv()
        for cp in copies():
            cp.wait_send()

    return start, finish


def _sibling_exchange_scratch(n):
    return [pltpu.SemaphoreType.DMA((n * N_CHIPS,)), pltpu.SemaphoreType.DMA((n * N_CHIPS,))]


def _sibling_exchange_shapes(grads):
    return [jax.ShapeDtypeStruct((N_CHIPS,) + g.shape[2:], g.dtype) for g in grads]


def exchange_with_sibling(grads, name):
    n = len(grads)

    def body(*refs):
        start, finish = _sibling_exchange_phases(refs[:n], refs[n:2 * n], refs[2 * n], refs[2 * n + 1])
        start()
        finish()

    return pl.pallas_call(
        body, name=name,
        in_specs=[pl.BlockSpec(memory_space=pl.ANY)] * n,
        out_specs=[pl.BlockSpec(memory_space=pl.ANY)] * n,
        out_shape=_sibling_exchange_shapes(grads),
        scratch_shapes=_sibling_exchange_scratch(n),
    )(*grads)


def add_own_half(grads, recvs, name):
    n = len(grads)

    def body(cc_ref, *refs):
        for g_ref, r_ref, o_ref in zip(refs[:n], refs[n:2 * n], refs[2 * n:]):
            o_ref[...] = (g_ref[...] + r_ref[...]).astype(BF16)

    own, got = [], []
    for g in grads:
        _, _, r, c = g.shape
        own.append(pl.BlockSpec((None, None, r // 2, c), lambda j, t, cc: (j, cc[0], t, 0)))
        got.append(pl.BlockSpec((None, r // 2, c), lambda j, t, cc: (j, t, 0)))
    grid_spec = pltpu.PrefetchScalarGridSpec(
        num_scalar_prefetch=1, grid=(N_CHIPS, 2), in_specs=own + got, out_specs=got)
    cc = lax.axis_index("c").astype(jnp.int32).reshape(1)
    return pl.pallas_call(
        body, name=name, grid_spec=grid_spec,
        out_shape=[jax.ShapeDtypeStruct((N_CHIPS,) + g.shape[2:], BF16) for g in grads],
        compiler_params=_cparams("parallel", "parallel"),
    )(cc, *grads, *recvs)


def _chip_exchange_phases(s_refs, out_refs, send_sems, recv_sems):
    n = len(s_refs)
    x, y, cc = _mesh_pos()
    my_chip = 2 * x + y
    chips = [(1 - x, y), (x, 1 - y), (1 - x, 1 - y)]

    def outgoing():
        return [pltpu.make_async_remote_copy(
            src_ref=s_refs[i].at[2 * px + py], dst_ref=out_refs[i].at[my_chip],
            send_sem=send_sems.at[3 * i + j], recv_sem=recv_sems.at[3 * i + j],
            device_id=(px, py, cc), device_id_type=MESH_ID)
            for i in range(n) for j, (px, py) in enumerate(chips)]

    def start():
        for cp in outgoing():
            cp.start()

    def finish():
        for i in range(n):
            for j, (px, py) in enumerate(chips):
                pltpu.make_async_remote_copy(
                    src_ref=s_refs[i].at[my_chip], dst_ref=out_refs[i].at[2 * px + py],
                    send_sem=send_sems.at[3 * i + j], recv_sem=recv_sems.at[3 * i + j],
                    device_id=(px, py, cc), device_id_type=MESH_ID).wait_recv()
        for cp in outgoing():
            cp.wait_send()

    return start, finish


def _chip_exchange_scratch(n):
    return [pltpu.SemaphoreType.DMA((3 * n,)), pltpu.SemaphoreType.DMA((3 * n,))]


def sum_chips(sums, parts, name):
    n = len(sums)

    def body(idx_ref, *refs):
        for i in range(n):
            s_ref, p1_ref, p2_ref, p3_ref = refs[4 * i:4 * i + 4]
            refs[4 * n + i][...] = ((s_ref[...].astype(F32) + p1_ref[...].astype(F32))
                                    + p2_ref[...].astype(F32)) + p3_ref[...].astype(F32)

    in_specs, out_specs, operands = [], [], []
    for s_, p_ in zip(sums, parts):
        _, r, c = p_.shape
        in_specs += [pl.BlockSpec((None, r // 2, c), functools.partial(lambda t, idx, k: (idx[k], t, 0), k=k))
                     for k in range(N_CHIPS)]
        out_specs.append(pl.BlockSpec((r // 2, c), lambda t, idx: (t, 0)))
        operands += [s_, p_, p_, p_]
    x, y = lax.axis_index("x"), lax.axis_index("y")
    idx = jnp.stack([2 * x + y, 2 * (1 - x) + y, 2 * x + (1 - y), 2 * (1 - x) + (1 - y)]).astype(jnp.int32)
    grid_spec = pltpu.PrefetchScalarGridSpec(
        num_scalar_prefetch=1, grid=(2,), in_specs=in_specs, out_specs=out_specs)
    return pl.pallas_call(
        body, name=name, grid_spec=grid_spec,
        out_shape=[jax.ShapeDtypeStruct(p_.shape[1:], F32) for p_ in parts],
        compiler_params=_cparams("parallel"),
    )(idx, *operands)


def share_with_sibling(halves, name):
    n = len(halves)

    def body(*refs):
        h_refs, out_refs = refs[:n], refs[n:2 * n]
        send_sems, recv_sems = refs[2 * n], refs[2 * n + 1]
        x, y, cc = _mesh_pos()
        copies = [pltpu.make_async_remote_copy(
            src_ref=h_refs[i], dst_ref=out_refs[i],
            send_sem=send_sems.at[i], recv_sem=recv_sems.at[i],
            device_id=(x, y, 1 - cc), device_id_type=MESH_ID) for i in range(n)]
        for cp in copies:
            cp.start()
        for cp in copies:
            cp.wait_recv()
        for cp in copies:
            cp.wait_send()

    return pl.pallas_call(
        body, name=name,
        in_specs=[pl.BlockSpec(memory_space=pl.ANY)] * n,
        out_specs=[pl.BlockSpec(memory_space=pl.ANY)] * n,
        out_shape=[jax.ShapeDtypeStruct(h.shape, h.dtype) for h in halves],
        scratch_shapes=[pltpu.SemaphoreType.DMA((n,)), pltpu.SemaphoreType.DMA((n,))],
    )(*halves)


def _adam_update(w, g, m, v):
    m = ADAM_B1 * m + (1.0 - ADAM_B1) * g
    v = ADAM_B2 * v + (1.0 - ADAM_B2) * (g * g)
    m_hat = m / (1.0 - ADAM_B1 ** ADAM_STEP)
    v_hat = v / (1.0 - ADAM_B2 ** ADAM_STEP)
    delta = -ADAM_LR * (m_hat / (jnp.sqrt(v_hat) + ADAM_EPS) + ADAM_WD * w)
    return delta, m, v


ADAMW_STEPS = 8


def adamw(ws, g_mine, g_sibling, ms, vs, name, chip_sums=()):
    n, n_s = len(ws), len(chip_sums)
    per_half = ADAMW_STEPS // 2

    def body(*refs):
        cc_ref = refs[0]
        ins = refs[1:1 + 5 * n]
        outs = refs[1 + 5 * n + n_s:1 + 9 * n + n_s]
        step = pl.program_id(0)
        if n_s:
            start, finish = _chip_exchange_phases(
                refs[1 + 5 * n:1 + 5 * n + n_s], refs[1 + 9 * n + n_s:1 + 9 * n + 2 * n_s],
                *refs[1 + 9 * n + 2 * n_s:])
            pl.when(step == 0)(start)
        mine = (step // per_half) == cc_ref[0]
        for i in range(n):
            w_ref, ga_ref, gb_ref, m_ref, v_ref = ins[5 * i:5 * i + 5]
            g_ref, d_ref, nm_ref, nv_ref = outs[4 * i:4 * i + 4]
            g = jnp.where(mine, ga_ref[...], gb_ref[...])
            g_ref[...] = g
            d, nm, nv = _adam_update(w_ref[...], g, m_ref[...], v_ref[...])
            d_ref[...] = d
            nm_ref[...] = nm
            nv_ref[...] = nv
        if n_s:
            pl.when(step == ADAMW_STEPS - 1)(finish)

    in_specs, out_specs, out_shape, operands = [], [], [], []
    for w, ga, gb, m, v in zip(ws, g_mine, g_sibling, ms, vs):
        r, c = w.shape
        tr = r // ADAMW_STEPS
        full = pl.BlockSpec((tr, c), lambda t, cc: (t, 0))
        part = pl.BlockSpec((tr, c), lambda t, cc: (t % per_half, 0))
        in_specs += [full, part, part, full, full]
        out_specs += [full] * 4
        out_shape += [jax.ShapeDtypeStruct((r, c), F32)] * 4
        operands += [w, ga, gb, m, v]
    anywhere = pl.BlockSpec(memory_space=pl.ANY)
    grid_spec = pltpu.PrefetchScalarGridSpec(
        num_scalar_prefetch=1, grid=(ADAMW_STEPS,),
        in_specs=in_specs + [anywhere] * n_s, out_specs=out_specs + [anywhere] * n_s,
        scratch_shapes=_chip_exchange_scratch(n_s) if n_s else [])
    cc = lax.axis_index("c").astype(jnp.int32).reshape(1)
    res = pl.pallas_call(
        body, name=name, grid_spec=grid_spec,
        out_shape=out_shape + [jax.ShapeDtypeStruct(s.shape, s.dtype) for s in chip_sums],
        compiler_params=_cparams("arbitrary"),
    )(cc, *operands, *chip_sums)
    per_shard = [tuple(res[4 * i:4 * i + 4]) for i in range(n)]
    return (per_shard, list(res[4 * n:])) if n_s else per_shard


def _pack_gather_phases(p_ref, out_ref, send_sems, recv_sems, local_sem):
    x, y, cc = _mesh_pos()
    me = 4 * x + 2 * y + cc
    flips = [(fx, fy, fc) for fx in (0, 1) for fy in (0, 1) for fc in (0, 1)][1:]

    def copy(k, row):
        fx, fy, fc = flips[k]
        return pltpu.make_async_remote_copy(
            src_ref=p_ref, dst_ref=out_ref.at[row],
            send_sem=send_sems.at[k], recv_sem=recv_sems.at[k],
            device_id=(x ^ fx, y ^ fy, cc ^ fc), device_id_type=MESH_ID)

    def local():
        return pltpu.make_async_copy(p_ref, out_ref.at[me], local_sem)

    def start():
        local().start()
        for k in range(len(flips)):
            copy(k, me).start()

    def finish():
        for k, (fx, fy, fc) in enumerate(flips):
            copy(k, 4 * (x ^ fx) + 2 * (y ^ fy) + (cc ^ fc)).wait_recv()
        for k in range(len(flips)):
            copy(k, me).wait_send()
        local().wait()

    return start, finish


def small_update(gathered, wpack, mpack, vpack):
    hw = HGRN_WIDTH

    def body(g_ref, w_ref, m_ref, v_ref, go_ref, d_ref, nm_ref, nv_ref, loss_ref):
        g = g_ref[0]
        for d in range(1, N_DEV):
            g = g + g_ref[d]
        wv = w_ref[...]
        a0, a1 = wv[4:5, :hw], wv[4:5, hw:]
        mx = jnp.maximum(a0, a1)
        e0, e1 = jnp.exp(a0 - mx), jnp.exp(a1 - mx)
        lb = e0 / (e0 + e1)
        dl = g[4:5, :hw] * lb * (1.0 - lb)
        row = lax.broadcasted_iota(jnp.int32, g.shape, 0)
        lb_row = jnp.concatenate([dl, -dl], axis=1)
        grads = jnp.where(row == 4, lb_row, jnp.where(row < 4, g, 0.0))
        go_ref[...] = grads
        d, nm, nv = _adam_update(wv, grads, m_ref[...], v_ref[...])
        d_ref[...] = d
        nm_ref[...] = nm
        nv_ref[...] = nv
        loss_ref[...] = jnp.zeros((8, LANES), F32) + jnp.sum(g[5:6, :])

    vm = pl.BlockSpec(memory_space=pltpu.VMEM)
    return pl.pallas_call(
        body, name="small_update",
        in_specs=[vm] * 4, out_specs=[vm] * 5,
        out_shape=[jax.ShapeDtypeStruct(wpack.shape, F32)] * 4 + [jax.ShapeDtypeStruct((8, LANES), F32)],
    )(gathered, wpack, mpack, vpack)


def _pack_small(n1, n2, fn, hn, lbl):
    z = jnp.zeros((1, D_MODEL - HGRN_WIDTH), F32)
    rows = [n1.reshape(1, D_MODEL), n2.reshape(1, D_MODEL), fn.reshape(1, D_MODEL),
            jnp.concatenate([hn.reshape(1, HGRN_WIDTH), z], axis=1), lbl.reshape(1, 2 * HGRN_WIDTH),
            jnp.zeros((3, D_MODEL), F32)]
    return jnp.concatenate(rows, axis=0)


def _unpack_small(pack):
    return (pack[0:1], pack[4].reshape(2, HGRN_WIDTH), pack[3:4, :HGRN_WIDTH], pack[1:2], pack[2])


def kernel(x, norm1_w, w_in, lb_logits, hgrn_norm_w, w_out, norm2_w, w_gate_up, w_down, final_norm_w, loss_target, m_norm1_w, m_w_in, m_lb_logits, m_hgrn_norm_w, m_w_out, m_norm2_w, m_w_gate_up, m_w_down, m_final_norm_w, v_norm1_w, v_w_in, v_lb_logits, v_hgrn_norm_w, v_w_out, v_norm2_w, v_w_gate_up, v_w_down, v_final_norm_w):
    seq = x.shape[1]
    xs = x.reshape(seq, D_MODEL)
    target = loss_target.reshape(seq, D_MODEL)
    shards = {"w_in": w_in[0], "w_out": w_out[0], "w_gu": w_gate_up[0], "w_down": w_down[0]}

    cast = {k: cast_bf16(w, "cast_" + k) for k, w in shards.items()}
    w_in4 = allgather_halves(cast["w_in"], "gather_w_in").reshape(N_CHIPS, D_MODEL, -1)

    cos_t, sin_t = _rope_tables(seq)
    fw = final_norm_w.reshape(1, D_MODEL)

    qr, kr, va, hg, u, q16, k16, v16, g_out, g_down = in_proj(
        xs, norm1_w, w_in4, cos_t, sin_t, [cast["w_out"], cast["w_down"]])
    ya, lse = attn_fwd(qr, kr, va, q16, k16, v16)
    yb, o_pre, st0, g_gu = hgrn_fwd(hg, lb_logits, hgrn_norm_w, [cast["w_gu"]])
    w_out_f = g_out.reshape(D_MODEL, D_MODEL)
    w_gu4 = g_gu.reshape(N_CHIPS, D_MODEL, -1)
    w_down_f = g_down.reshape(FFN_HIDDEN, D_MODEL)
    mixed, h1, u2, g, up, act, dh2, acc_fin = ffn_fwd(
        ya, yb, xs, w_out_f, norm2_w, w_gu4, w_down_f, fw, target)

    cw_in, cw_gu = w_in4.shape[2], w_gu4.shape[2]
    dgu, dh1, dya, dyb, delta, acc_n2 = ffn_bwd(dh2, w_down_f, g, up, w_gu4, h1, norm2_w, w_out_f, ya)
    early = [
        weight_grad(mixed, dh1, D_MODEL, "wgrad_out", rows_per_step=2048).reshape(N_CHIPS, 2, D_MODEL // 8, D_MODEL),
        weight_grad(u2, dgu, cw_gu, "wgrad_gu", group=2).reshape(N_CHIPS, 2, D_MODEL // 2, cw_gu),
        weight_grad(act, dh2, D_MODEL, "wgrad_down").reshape(N_CHIPS, 2, FFN_HIDDEN // 8, D_MODEL),
    ]
    dq, dk, dv, *early_recv = attn_bwd(qr, kr, va, q16, k16, v16, dya, lse, delta, early)
    early_sums = add_own_half(early, early_recv, "add_half_early")
    dhg, acc_hg, *early_parts = hgrn_bwd(hg, lb_logits, hgrn_norm_w, o_pre, st0, dyb, early_sums)
    dproj, dx, acc_n1 = in_bwd(dq, dk, dv, dhg, cos_t, sin_t, w_in4, xs, norm1_w, dh1)
    z512 = jnp.zeros((1, D_MODEL - HGRN_WIDTH), F32)
    gpack = jnp.concatenate([
        acc_n1[0:1], acc_n2[0:1], acc_fin[0:1],
        jnp.concatenate([acc_hg[0:1], z512], axis=1), jnp.concatenate([acc_hg[1:2], z512], axis=1),
        acc_fin[1:2], jnp.zeros((2, D_MODEL), F32)], axis=0)
    g_in, gathered_packs = weight_grad(u, dproj, cw_in, "wgrad_in", group=2, small_pack=gpack, rows_per_step=2048)
    late = [g_in.reshape(N_CHIPS, 2, D_MODEL // 2, cw_in)]
    late_recv = exchange_with_sibling(late, "grad_exchange_sibling_late")
    late_sums = add_own_half(late, late_recv, "add_half_in")

    early_halves = sum_chips(early_sums, early_parts, "sum_chips_early")
    early_others = share_with_sibling(early_halves, "grad_share_sibling_early")
    early_keys = ["w_out", "w_gu", "w_down"]
    moments = {"w_in": (m_w_in, v_w_in), "w_out": (m_w_out, v_w_out),
               "w_gu": (m_w_gate_up, v_w_gate_up), "w_down": (m_w_down, v_w_down)}
    early_updates, late_parts = adamw(
        [shards[k] for k in early_keys], early_halves, early_others,
        [moments[k][0][0] for k in early_keys], [moments[k][1][0] for k in early_keys],
        "adamw_early", chip_sums=late_sums)
    late_halves = sum_chips(late_sums, late_parts, "sum_chips_in")
    late_others = share_with_sibling(late_halves, "grad_share_sibling_late")
    late_updates = adamw([shards["w_in"]], late_halves, late_others,
                         [moments["w_in"][0][0]], [moments["w_in"][1][0]], "adamw_in")
    big = {k: tuple(t[None] for t in upd) for k, upd in zip(early_keys + ["w_in"], early_updates + late_updates)}

    wpack = _pack_small(norm1_w, norm2_w, final_norm_w, hgrn_norm_w, lb_logits)
    mpack = _pack_small(m_norm1_w, m_norm2_w, m_final_norm_w, m_hgrn_norm_w, m_lb_logits)
    vpack = _pack_small(v_norm1_w, v_norm2_w, v_final_norm_w, v_hgrn_norm_w, v_lb_logits)
    gs, ds, nms, nvs, loss8 = small_update(gathered_packs, wpack, mpack, vpack)
    loss = loss8[0, 0]

    def assemble(small_pack, idx):
        n1, lbl, hn, n2, fn = _unpack_small(small_pack)
        return (n1, big["w_in"][idx], lbl, hn, big["w_out"][idx], n2, big["w_gu"][idx], big["w_down"][idx], fn)

    return (loss, dx.reshape(x.shape), *assemble(gs, 0), *assemble(ds, 1), *assemble(nms, 2), *assemble(nvs, 3))
```
